```python
import jax, jax.numpy as jnp
from jax import lax
import numpy as np

D_MODEL = 1024
BATCH = 8
SEQ = 8192
DEPTH = 1

GRID_W = 64
N_Q_HEADS = 16
N_KV_HEADS = 4
HEAD_DIM = 64
ROPE_THETA = 10000.0
Q_BLOCK = 128
SSD_EXPAND = 2
D_INNER = SSD_EXPAND * D_MODEL
SSD_HEAD_DIM = 64
N_SSD_HEADS = D_INNER // SSD_HEAD_DIM
N_SSD_GROUPS = 4
D_STATE = 128
D_CONV = 5
CHUNK = 128
D_FF = 4 * D_MODEL
EPS = 1e-6

ATTN_Q_DIM = N_Q_HEADS * HEAD_DIM
ATTN_KV_DIM = N_KV_HEADS * HEAD_DIM
CONV_DIM = D_INNER + 2 * N_SSD_GROUPS * D_STATE
D_IN_PROJ = ATTN_Q_DIM + 2 * ATTN_KV_DIM + CONV_DIM + D_INNER + 2 * N_SSD_HEADS + 2 * D_MODEL

kernel_name = "hybrid_gqa_ssd_griffin_merge_block"


def rms_norm(x, w):
    xf = x.astype(jnp.float32)
    xf = xf * lax.rsqrt(jnp.mean(xf * xf, axis=-1, keepdims=True) + EPS)
    return xf.astype(x.dtype) * w


def rotate(u, cos, sin):
    f = u.shape[-1] // 2
    u1, u2 = u[..., :f], u[..., f:]
    cos = cos[None, :, None, :]
    sin = sin[None, :, None, :]
    return jnp.concatenate([u1 * cos - u2 * sin, u2 * cos + u1 * sin], axis=-1)


def axial_rope(t, cos_r, sin_r, cos_c, sin_c):
    half = t.shape[-1] // 2
    out = jnp.concatenate([rotate(t[..., :half], cos_r, sin_r),
                           rotate(t[..., half:], cos_c, sin_c)], axis=-1)
    return out.astype(t.dtype)


def block_attention(q, k, v):
    b, s, hq, dh = q.shape
    hkv = k.shape[2]
    rep = hq // hkv
    nb = s // Q_BLOCK
    qb = q.reshape(b, nb, Q_BLOCK, hkv, rep, dh).transpose(1, 0, 2, 3, 4, 5)
    scale = dh ** -0.5

    def one_block(qi):
        sc = jnp.einsum("bqgrd,bkgd->bgrqk", qi, k).astype(jnp.float32) * scale
        p = jax.nn.softmax(sc, axis=-1).astype(v.dtype)
        return jnp.einsum("bgrqk,bkgd->bqgrd", p, v)

    out = lax.map(one_block, qb)
    return out.transpose(1, 0, 2, 3, 4, 5).reshape(b, s, hq * dh)


def segsum(a):
    t = a.shape[-1]
    cs = jnp.cumsum(a, axis=-1)
    diff = cs[..., :, None] - cs[..., None, :]
    mask = jnp.tril(jnp.ones((t, t), dtype=bool))
    return jnp.where(mask, diff, -jnp.inf)


def ssd_chunked(xdt, a, bm, cm):
    b, s, h, p = xdt.shape
    g, n = bm.shape[2], bm.shape[3]
    r = h // g
    nc = s // CHUNK
    X = xdt.astype(jnp.float32).reshape(b, nc, CHUNK, g, r, p)
    A = a.reshape(b, nc, CHUNK, g, r).transpose(0, 3, 4, 1, 2)
    Bc = bm.astype(jnp.float32).reshape(b, nc, CHUNK, g, n)
    Cc = cm.astype(jnp.float32).reshape(b, nc, CHUNK, g, n)
    A_cs = jnp.cumsum(A, axis=-1)
    CB = jnp.einsum("bclgn,bcsgn->bgcls", Cc, Bc)
    M = CB[:, :, None] * jnp.exp(segsum(A))
    y_diag = jnp.einsum("bgrcls,bcsgrp->bclgrp", M, X)
    decay_states = jnp.exp(A_cs[..., -1:] - A_cs)
    states = jnp.einsum("bclgn,bgrcl,bclgrp->cbgrpn", Bc, decay_states, X)
    chunk_decay = jnp.moveaxis(jnp.exp(A_cs[..., -1]), -1, 0)

    def step(hs, inp):
        st, dec = inp
        return dec[..., None, None] * hs + st, hs

    h0 = jnp.zeros(states.shape[1:], jnp.float32)
    _, prev = lax.scan(step, h0, (states, chunk_decay))
    y_off = jnp.einsum("bclgn,cbgrpn,bgrcl->bclgrp", Cc, prev, jnp.exp(A_cs))
    return (y_diag + y_off).reshape(b, s, h, p)


def depthwise_conv_centred(u, w, bias):
    pad = (w.shape[0] - 1) // 2
    out = lax.conv_general_dilated(u, w[:, None, :].astype(u.dtype), window_strides=(1,),
                                   padding=[(pad, pad)], dimension_numbers=("NWC", "WIO", "NWC"),
                                   feature_group_count=u.shape[-1])
    return out + bias


def ssd_mixer(xBC, z, dt_raw, conv_w, conv_b, A_log, dt_bias, ssd_D, ssd_norm_w):
    b, s, _ = xBC.shape
    xBC = jax.nn.silu(depthwise_conv_centred(xBC, conv_w, conv_b))
    gn = N_SSD_GROUPS * D_STATE
    xs = xBC[..., :D_INNER].reshape(b, s, N_SSD_HEADS, SSD_HEAD_DIM)
    bm = xBC[..., D_INNER:D_INNER + gn].reshape(b, s, N_SSD_GROUPS, D_STATE)
    cm = xBC[..., D_INNER + gn:].reshape(b, s, N_SSD_GROUPS, D_STATE)
    dt = jax.nn.softplus(dt_raw.astype(jnp.float32).reshape(b, s, 2, N_SSD_HEADS)
                         + dt_bias.astype(jnp.float32))
    A = -jnp.exp(A_log.astype(jnp.float32))
    dt_f, dt_b = dt[:, :, 0], dt[:, :, 1]
    xf = xs.astype(jnp.float32)
    y_fwd = ssd_chunked(xf * dt_f[..., None], dt_f * A[0], bm, cm)
    flip = lambda t: jnp.flip(t, axis=1)
    y_bwd = flip(ssd_chunked(flip(xf * dt_b[..., None]), flip(dt_b * A[1]), flip(bm), flip(cm)))
    y = y_fwd + y_bwd + ssd_D.astype(jnp.float32)[:, None] * xf
    y = y.reshape(b, s, D_INNER).astype(xBC.dtype)
    return rms_norm(y * jax.nn.silu(z), ssd_norm_w)


def _fwd_setup_inputs(seed: int = 0) -> dict:
    key = jax.random.key(seed)
    ks = jax.random.split(key, 20)
    f32 = jnp.float32
    L = DEPTH
    nrm = lambda k, shape, s: jax.random.normal(k, shape, f32) * s
    x = jax.random.normal(ks[0], (BATCH, SEQ, D_MODEL), f32)
    c = jax.random.normal(ks[1], (BATCH, D_MODEL), f32)
    w_ada = nrm(ks[2], (L, D_MODEL, 6 * D_MODEL), 0.5 * D_MODEL ** -0.5)
    b_ada = nrm(ks[3], (L, 6 * D_MODEL), 0.02)
    norm1_w = 1.0 + nrm(ks[4], (L, D_MODEL), 0.02)
    norm2_w = 1.0 + nrm(ks[5], (L, D_MODEL), 0.02)
    w_in = nrm(ks[6], (L, D_MODEL, D_IN_PROJ), D_MODEL ** -0.5)
    q_norm_w = 1.0 + nrm(ks[7], (L, HEAD_DIM), 0.02)
    k_norm_w = 1.0 + nrm(ks[8], (L, HEAD_DIM), 0.02)
    conv_w = nrm(ks[9], (L, D_CONV, CONV_DIM), D_CONV ** -0.5)
    conv_b = nrm(ks[10], (L, CONV_DIM), 0.02)
    A_log = jnp.log(jax.random.uniform(ks[11], (L, 2, N_SSD_HEADS), f32, 1.0, 16.0))
    dt0 = jnp.exp(jax.random.uniform(ks[12], (L, 2, N_SSD_HEADS), f32, np.log(1e-3), np.log(1e-1)))
    dt_bias = dt0 + jnp.log(-jnp.expm1(-dt0))
    ssd_D = 1.0 + nrm(ks[13], (L, N_SSD_HEADS), 0.02)
    ssd_norm_w = 1.0 + nrm(ks[14], (L, D_INNER), 0.02)
    w_attn_out = nrm(ks[15], (L, ATTN_Q_DIM, D_MODEL), ATTN_Q_DIM ** -0.5)
    w_ssd_out = nrm(ks[16], (L, D_INNER, D_MODEL), D_INNER ** -0.5)
    w_o = nrm(ks[17], (L, D_MODEL, D_MODEL), D_MODEL ** -0.5)
    w_mlp1 = nrm(ks[18], (L, D_MODEL, D_FF), D_MODEL ** -0.5)
    w_mlp2 = nrm(ks[19], (L, D_FF, D_MODEL), D_FF ** -0.5)
    return {"x": x, "c": c, "w_ada": w_ada, "b_ada": b_ada, "norm1_w": norm1_w, "norm2_w": norm2_w,
            "w_in": w_in, "q_norm_w": q_norm_w, "k_norm_w": k_norm_w, "conv_w": conv_w, "conv_b": conv_b,
            "A_log": A_log, "dt_bias": dt_bias, "ssd_D": ssd_D, "ssd_norm_w": ssd_norm_w,
            "w_attn_out": w_attn_out, "w_ssd_out": w_ssd_out, "w_o": w_o,
            "w_mlp1": w_mlp1, "w_mlp2": w_mlp2}


def _fwd_reference(x, c, w_ada, b_ada, norm1_w, norm2_w, w_in, q_norm_w, k_norm_w, conv_w, conv_b,
              A_log, dt_bias, ssd_D, ssd_norm_w, w_attn_out, w_ssd_out, w_o, w_mlp1, w_mlp2):
    b, s, d = x.shape
    rows = s // GRID_W
    pos_row = jnp.repeat(jnp.arange(rows, dtype=jnp.int32), GRID_W).astype(jnp.float32)
    pos_col = jnp.tile(jnp.arange(GRID_W, dtype=jnp.int32), rows).astype(jnp.float32)
    axis_dim = HEAD_DIM // 2
    inv_freq = ROPE_THETA ** (-jnp.arange(0, axis_dim, 2, dtype=jnp.float32) / axis_dim)
    ang_r = pos_row[:, None] * inv_freq[None, :]
    ang_c = pos_col[:, None] * inv_freq[None, :]
    cos_r, sin_r = jnp.cos(ang_r), jnp.sin(ang_r)
    cos_c, sin_c = jnp.cos(ang_c), jnp.sin(ang_c)

    sizes = [ATTN_Q_DIM, ATTN_KV_DIM, ATTN_KV_DIM, CONV_DIM, D_INNER, 2 * N_SSD_HEADS, 2 * D_MODEL]
    offsets = []
    acc = 0
    for sz in sizes[:-1]:
        acc += sz
        offsets.append(acc)

    for l in range(DEPTH):
        mod = jax.nn.silu(c) @ w_ada[l] + b_ada[l]
        shift1, scale1, gate1, shift2, scale2, gate2 = [m[:, None, :] for m in jnp.split(mod, 6, axis=-1)]

        h = rms_norm(x, norm1_w[l]) * (1.0 + scale1) + shift1
        proj = h @ w_in[l]
        q, k, v, xBC, z, dt_raw, gates = jnp.split(proj, offsets, axis=-1)

        q = rms_norm(q.reshape(b, s, N_Q_HEADS, HEAD_DIM), q_norm_w[l])
        k = rms_norm(k.reshape(b, s, N_KV_HEADS, HEAD_DIM), k_norm_w[l])
        v = v.reshape(b, s, N_KV_HEADS, HEAD_DIM)
        q = axial_rope(q, cos_r, sin_r, cos_c, sin_c)
        k = axial_rope(k, cos_r, sin_r, cos_c, sin_c)
        attn = block_attention(q, k, v)

        ssd = ssd_mixer(xBC, z, dt_raw, conv_w[l], conv_b[l], A_log[l], dt_bias[l], ssd_D[l], ssd_norm_w[l])

        g_attn = jax.nn.sigmoid(gates[..., :D_MODEL])
        g_ssd = jax.nn.sigmoid(gates[..., D_MODEL:])
        merged = g_attn * (attn @ w_attn_out[l]) + g_ssd * (ssd @ w_ssd_out[l])
        x = x + gate1 * (merged @ w_o[l])

        h2 = rms_norm(x, norm2_w[l]) * (1.0 + scale2) + shift2
        ff = jnp.square(jax.nn.relu(h2 @ w_mlp1[l])) @ w_mlp2[l]
        x = x + gate2 * ff
    return x


import jax as _jax
import jax.numpy as _jnp

TWIN_FORMAT = 'train_step'
FWD_PARAMS = ['x', 'c', 'w_ada', 'b_ada', 'norm1_w', 'norm2_w', 'w_in', 'q_norm_w', 'k_norm_w', 'conv_w', 'conv_b', 'A_log', 'dt_bias', 'ssd_D', 'ssd_norm_w', 'w_attn_out', 'w_ssd_out', 'w_o', 'w_mlp1', 'w_mlp2']
TWIN_WEIGHTS = ['w_ada', 'b_ada', 'norm1_w', 'norm2_w', 'w_in', 'q_norm_w', 'k_norm_w', 'conv_w', 'conv_b', 'A_log', 'dt_bias', 'ssd_D', 'ssd_norm_w', 'w_attn_out', 'w_ssd_out', 'w_o', 'w_mlp1', 'w_mlp2']
TWIN_DIFF_INPUT = 'x'
TWIN_INPUTS = ['x', 'c', 'w_ada', 'b_ada', 'norm1_w', 'norm2_w', 'w_in', 'q_norm_w', 'k_norm_w', 'conv_w', 'conv_b', 'A_log', 'dt_bias', 'ssd_D', 'ssd_norm_w', 'w_attn_out', 'w_ssd_out', 'w_o', 'w_mlp1', 'w_mlp2', 'loss_target', 'm_w_ada', 'm_b_ada', 'm_norm1_w', 'm_norm2_w', 'm_w_in', 'm_q_norm_w', 'm_k_norm_w', 'm_conv_w', 'm_conv_b', 'm_A_log', 'm_dt_bias', 'm_ssd_D', 'm_ssd_norm_w', 'm_w_attn_out', 'm_w_ssd_out', 'm_w_o', 'm_w_mlp1', 'm_w_mlp2', 'v_w_ada', 'v_b_ada', 'v_norm1_w', 'v_norm2_w', 'v_w_in', 'v_q_norm_w', 'v_k_norm_w', 'v_conv_w', 'v_conv_b', 'v_A_log', 'v_dt_bias', 'v_ssd_D', 'v_ssd_norm_w', 'v_w_attn_out', 'v_w_ssd_out', 'v_w_o', 'v_w_mlp1', 'v_w_mlp2']
TWIN_OUTPUTS = ['loss', 'grad_x', 'grad_w_ada', 'grad_b_ada', 'grad_norm1_w', 'grad_norm2_w', 'grad_w_in', 'grad_q_norm_w', 'grad_k_norm_w', 'grad_conv_w', 'grad_conv_b', 'grad_A_log', 'grad_dt_bias', 'grad_ssd_D', 'grad_ssd_norm_w', 'grad_w_attn_out', 'grad_w_ssd_out', 'grad_w_o', 'grad_w_mlp1', 'grad_w_mlp2', 'delta_w_ada', 'delta_b_ada', 'delta_norm1_w', 'delta_norm2_w', 'delta_w_in', 'delta_q_norm_w', 'delta_k_norm_w', 'delta_conv_w', 'delta_conv_b', 'delta_A_log', 'delta_dt_bias', 'delta_ssd_D', 'delta_ssd_norm_w', 'delta_w_attn_out', 'delta_w_ssd_out', 'delta_w_o', 'delta_w_mlp1', 'delta_w_mlp2', 'new_m_w_ada', 'new_m_b_ada', 'new_m_norm1_w', 'new_m_norm2_w', 'new_m_w_in', 'new_m_q_norm_w', 'new_m_k_norm_w', 'new_m_conv_w', 'new_m_conv_b', 'new_m_A_log', 'new_m_dt_bias', 'new_m_ssd_D', 'new_m_ssd_norm_w', 'new_m_w_attn_out', 'new_m_w_ssd_out', 'new_m_w_o', 'new_m_w_mlp1', 'new_m_w_mlp2', 'new_v_w_ada', 'new_v_b_ada', 'new_v_norm1_w', 'new_v_norm2_w', 'new_v_w_in', 'new_v_q_norm_w', 'new_v_k_norm_w', 'new_v_conv_w', 'new_v_conv_b', 'new_v_A_log', 'new_v_dt_bias', 'new_v_ssd_D', 'new_v_ssd_norm_w', 'new_v_w_attn_out', 'new_v_w_ssd_out', 'new_v_w_o', 'new_v_w_mlp1', 'new_v_w_mlp2']
TWIN_LEAF_KINDS = {'loss': 'loss', 'grad_x': 'grad_x', 'grad_w_ada': 'grad_w', 'grad_b_ada': 'grad_w', 'grad_norm1_w': 'grad_w', 'grad_norm2_w': 'grad_w', 'grad_w_in': 'grad_w', 'grad_q_norm_w': 'grad_w', 'grad_k_norm_w': 'grad_w', 'grad_conv_w': 'grad_w', 'grad_conv_b': 'grad_w', 'grad_A_log': 'grad_w', 'grad_dt_bias': 'grad_w', 'grad_ssd_D': 'grad_w', 'grad_ssd_norm_w': 'grad_w', 'grad_w_attn_out': 'grad_w', 'grad_w_ssd_out': 'grad_w', 'grad_w_o': 'grad_w', 'grad_w_mlp1': 'grad_w', 'grad_w_mlp2': 'grad_w', 'delta_w_ada': 'delta_w', 'delta_b_ada': 'delta_w', 'delta_norm1_w': 'delta_w', 'delta_norm2_w': 'delta_w', 'delta_w_in': 'delta_w', 'delta_q_norm_w': 'delta_w', 'delta_k_norm_w': 'delta_w', 'delta_conv_w': 'delta_w', 'delta_conv_b': 'delta_w', 'delta_A_log': 'delta_w', 'delta_dt_bias': 'delta_w', 'delta_ssd_D': 'delta_w', 'delta_ssd_norm_w': 'delta_w', 'delta_w_attn_out': 'delta_w', 'delta_w_ssd_out': 'delta_w', 'delta_w_o': 'delta_w', 'delta_w_mlp1': 'delta_w', 'delta_w_mlp2': 'delta_w', 'new_m_w_ada': 'new_m', 'new_m_b_ada': 'new_m', 'new_m_norm1_w': 'new_m', 'new_m_norm2_w': 'new_m', 'new_m_w_in': 'new_m', 'new_m_q_norm_w': 'new_m', 'new_m_k_norm_w': 'new_m', 'new_m_conv_w': 'new_m', 'new_m_conv_b': 'new_m', 'new_m_A_log': 'new_m', 'new_m_dt_bias': 'new_m', 'new_m_ssd_D': 'new_m', 'new_m_ssd_norm_w': 'new_m', 'new_m_w_attn_out': 'new_m', 'new_m_w_ssd_out': 'new_m', 'new_m_w_o': 'new_m', 'new_m_w_mlp1': 'new_m', 'new_m_w_mlp2': 'new_m', 'new_v_w_ada': 'new_v', 'new_v_b_ada': 'new_v', 'new_v_norm1_w': 'new_v', 'new_v_norm2_w': 'new_v', 'new_v_w_in': 'new_v', 'new_v_q_norm_w': 'new_v', 'new_v_k_norm_w': 'new_v', 'new_v_conv_w': 'new_v', 'new_v_conv_b': 'new_v', 'new_v_A_log': 'new_v', 'new_v_dt_bias': 'new_v', 'new_v_ssd_D': 'new_v', 'new_v_ssd_norm_w': 'new_v', 'new_v_w_attn_out': 'new_v', 'new_v_w_ssd_out': 'new_v', 'new_v_w_o': 'new_v', 'new_v_w_mlp1': 'new_v', 'new_v_w_mlp2': 'new_v'}


def _forward(args):
    return _fwd_reference(*[args[k] for k in FWD_PARAMS])


def _output_shape():
    out = _jax.eval_shape(lambda: _forward(_fwd_setup_inputs(0)))
    return out.shape, out.dtype

N_MICROBATCH = 1
ADAM_LR = 0.001
ADAM_B1 = 0.9
ADAM_B2 = 0.999
ADAM_EPS = 1e-08
ADAM_WD = 0.01
ADAM_STEP = 10
PER_EXAMPLE_BATCH_AXIS = {'x': 0, 'c': 0, 'loss_target': 0}
SHARED_INPUTS = []
_WEIGHT_DTYPES = {'w_ada': _jnp.float32, 'b_ada': _jnp.float32, 'norm1_w': _jnp.float32, 'norm2_w': _jnp.float32, 'w_in': _jnp.float32, 'q_norm_w': _jnp.float32, 'k_norm_w': _jnp.float32, 'conv_w': _jnp.float32, 'conv_b': _jnp.float32, 'A_log': _jnp.float32, 'dt_bias': _jnp.float32, 'ssd_D': _jnp.float32, 'ssd_norm_w': _jnp.float32, 'w_attn_out': _jnp.float32, 'w_ssd_out': _jnp.float32, 'w_o': _jnp.float32, 'w_mlp1': _jnp.float32, 'w_mlp2': _jnp.float32}
MOMENT_SCALE = {'w_ada': 6.543238e+00, 'b_ada': 1.404521e+01, 'norm1_w': 1.380645e-01, 'norm2_w': 2.442758e+01, 'w_in': 1.447487e-01, 'q_norm_w': 4.328466e-02, 'k_norm_w': 4.398303e-02, 'conv_w': 1.556356e-01, 'conv_b': 4.307987e-01, 'A_log': 8.747578e-01, 'dt_bias': 1.169901e-01, 'ssd_D': 2.384434e-01, 'ssd_norm_w': 1.547028e+00, 'w_attn_out': 4.290927e-01, 'w_ssd_out': 3.894889e-01, 'w_o': 5.253425e-01, 'w_mlp1': 6.662097e-01, 'w_mlp2': 2.730119e+00}


def _to_microbatches(a, axis):
    t = _jnp.moveaxis(a, axis, 0)
    t = t.reshape((N_MICROBATCH, t.shape[0] // N_MICROBATCH) + t.shape[1:])
    return _jnp.moveaxis(t, 1, axis + 1)


def setup_inputs(seed: int = 0) -> dict:
    inp = _fwd_setup_inputs(seed)
    key = _jax.random.fold_in(_jax.random.key(seed), 7919)
    shape, _ = _output_shape()
    out = dict(inp)
    out["loss_target"] = _jax.random.normal(_jax.random.fold_in(key, 0), shape, _jnp.float32)
    for i, name in enumerate(TWIN_WEIGHTS):
        w = inp[name].astype(_jnp.float32)
        if MOMENT_SCALE is None:
            s = _jnp.sqrt(_jnp.mean(_jnp.square(w)) + 1e-30)
        else:
            s = MOMENT_SCALE[name]
        km, kv = _jax.random.split(_jax.random.fold_in(key, i + 1))
        out[name] = w
        out["m_" + name] = s * _jax.random.normal(km, w.shape, _jnp.float32)
        out["v_" + name] = (s * s) * _jax.random.uniform(kv, w.shape, _jnp.float32, 0.5, 1.5)
    if N_MICROBATCH > 1:
        for name, axis in PER_EXAMPLE_BATCH_AXIS.items():
            out[name] = _to_microbatches(out[name], axis)
    return {'x': out['x'], 'c': out['c'], 'w_ada': out['w_ada'], 'b_ada': out['b_ada'], 'norm1_w': out['norm1_w'], 'norm2_w': out['norm2_w'], 'w_in': out['w_in'], 'q_norm_w': out['q_norm_w'], 'k_norm_w': out['k_norm_w'], 'conv_w': out['conv_w'], 'conv_b': out['conv_b'], 'A_log': out['A_log'], 'dt_bias': out['dt_bias'], 'ssd_D': out['ssd_D'], 'ssd_norm_w': out['ssd_norm_w'], 'w_attn_out': out['w_attn_out'], 'w_ssd_out': out['w_ssd_out'], 'w_o': out['w_o'], 'w_mlp1': out['w_mlp1'], 'w_mlp2': out['w_mlp2'], 'loss_target': out['loss_target'], 'm_w_ada': out['m_w_ada'], 'm_b_ada': out['m_b_ada'], 'm_norm1_w': out['m_norm1_w'], 'm_norm2_w': out['m_norm2_w'], 'm_w_in': out['m_w_in'], 'm_q_norm_w': out['m_q_norm_w'], 'm_k_norm_w': out['m_k_norm_w'], 'm_conv_w': out['m_conv_w'], 'm_conv_b': out['m_conv_b'], 'm_A_log': out['m_A_log'], 'm_dt_bias': out['m_dt_bias'], 'm_ssd_D': out['m_ssd_D'], 'm_ssd_norm_w': out['m_ssd_norm_w'], 'm_w_attn_out': out['m_w_attn_out'], 'm_w_ssd_out': out['m_w_ssd_out'], 'm_w_o': out['m_w_o'], 'm_w_mlp1': out['m_w_mlp1'], 'm_w_mlp2': out['m_w_mlp2'], 'v_w_ada': out['v_w_ada'], 'v_b_ada': out['v_b_ada'], 'v_norm1_w': out['v_norm1_w'], 'v_norm2_w': out['v_norm2_w'], 'v_w_in': out['v_w_in'], 'v_q_norm_w': out['v_q_norm_w'], 'v_k_norm_w': out['v_k_norm_w'], 'v_conv_w': out['v_conv_w'], 'v_conv_b': out['v_conv_b'], 'v_A_log': out['v_A_log'], 'v_dt_bias': out['v_dt_bias'], 'v_ssd_D': out['v_ssd_D'], 'v_ssd_norm_w': out['v_ssd_norm_w'], 'v_w_attn_out': out['v_w_attn_out'], 'v_w_ssd_out': out['v_w_ssd_out'], 'v_w_o': out['v_w_o'], 'v_w_mlp1': out['v_w_mlp1'], 'v_w_mlp2': out['v_w_mlp2']}


def _loss(weights, diff, rest, loss_target):
    with _jax.named_scope("forward"):
        args = {**rest, TWIN_DIFF_INPUT: diff, **{k: w.astype(_WEIGHT_DTYPES[k]) for k, w in weights.items()}}
        y = _forward(args)
    with _jax.named_scope("loss_head"):
        err = _jnp.square(y.astype(_jnp.float32) - loss_target)
        return 0.5 * _jnp.sum(_jnp.mean(err, axis=-1)) if err.ndim else 0.5 * err


def _adamw(w, g, m, v):
    m = ADAM_B1 * m + (1.0 - ADAM_B1) * g
    v = ADAM_B2 * v + (1.0 - ADAM_B2) * _jnp.square(g)
    m_hat = m / (1.0 - ADAM_B1 ** ADAM_STEP)
    v_hat = v / (1.0 - ADAM_B2 ** ADAM_STEP)
    delta = -ADAM_LR * (m_hat / (_jnp.sqrt(v_hat) + ADAM_EPS) + ADAM_WD * w)
    return delta, m, v


def reference(x, c, w_ada, b_ada, norm1_w, norm2_w, w_in, q_norm_w, k_norm_w, conv_w, conv_b, A_log, dt_bias, ssd_D, ssd_norm_w, w_attn_out, w_ssd_out, w_o, w_mlp1, w_mlp2, loss_target, m_w_ada, m_b_ada, m_norm1_w, m_norm2_w, m_w_in, m_q_norm_w, m_k_norm_w, m_conv_w, m_conv_b, m_A_log, m_dt_bias, m_ssd_D, m_ssd_norm_w, m_w_attn_out, m_w_ssd_out, m_w_o, m_w_mlp1, m_w_mlp2, v_w_ada, v_b_ada, v_norm1_w, v_norm2_w, v_w_in, v_q_norm_w, v_k_norm_w, v_conv_w, v_conv_b, v_A_log, v_dt_bias, v_ssd_D, v_ssd_norm_w, v_w_attn_out, v_w_ssd_out, v_w_o, v_w_mlp1, v_w_mlp2):
    given = dict(x=x, c=c, w_ada=w_ada, b_ada=b_ada, norm1_w=norm1_w, norm2_w=norm2_w, w_in=w_in, q_norm_w=q_norm_w, k_norm_w=k_norm_w, conv_w=conv_w, conv_b=conv_b, A_log=A_log, dt_bias=dt_bias, ssd_D=ssd_D, ssd_norm_w=ssd_norm_w, w_attn_out=w_attn_out, w_ssd_out=w_ssd_out, w_o=w_o, w_mlp1=w_mlp1, w_mlp2=w_mlp2, loss_target=loss_target, m_w_ada=m_w_ada, m_b_ada=m_b_ada, m_norm1_w=m_norm1_w, m_norm2_w=m_norm2_w, m_w_in=m_w_in, m_q_norm_w=m_q_norm_w, m_k_norm_w=m_k_norm_w, m_conv_w=m_conv_w, m_conv_b=m_conv_b, m_A_log=m_A_log, m_dt_bias=m_dt_bias, m_ssd_D=m_ssd_D, m_ssd_norm_w=m_ssd_norm_w, m_w_attn_out=m_w_attn_out, m_w_ssd_out=m_w_ssd_out, m_w_o=m_w_o, m_w_mlp1=m_w_mlp1, m_w_mlp2=m_w_mlp2, v_w_ada=v_w_ada, v_b_ada=v_b_ada, v_norm1_w=v_norm1_w, v_norm2_w=v_norm2_w, v_w_in=v_w_in, v_q_norm_w=v_q_norm_w, v_k_norm_w=v_k_norm_w, v_conv_w=v_conv_w, v_conv_b=v_conv_b, v_A_log=v_A_log, v_dt_bias=v_dt_bias, v_ssd_D=v_ssd_D, v_ssd_norm_w=v_ssd_norm_w, v_w_attn_out=v_w_attn_out, v_w_ssd_out=v_w_ssd_out, v_w_o=v_w_o, v_w_mlp1=v_w_mlp1, v_w_mlp2=v_w_mlp2)
    weights = {n: given[n] for n in TWIN_WEIGHTS}
    shared = {n: given[n] for n in SHARED_INPUTS}
    per_example = {n: given[n] for n in ['x', 'c']}
    grad_fn = _jax.value_and_grad(_loss, argnums=(0, 1))

    def one_microbatch(ex, loss_target):
        ex = dict(ex)
        diff = ex.pop(TWIN_DIFF_INPUT)
        return grad_fn(weights, diff, {**shared, **ex}, loss_target)

    if N_MICROBATCH == 1:
        loss, (grad_w, grad_x) = one_microbatch(per_example, given["loss_target"])
    else:
        def body(carry, xs):
            loss_sum, grad_sum = carry
            l_k, (gw_k, gx_k) = one_microbatch(xs[0], xs[1])
            with _jax.named_scope("update"):
                return (loss_sum + l_k, _jax.tree.map(_jnp.add, grad_sum, gw_k)), gx_k

        init = (_jnp.zeros((), _jnp.float32), _jax.tree.map(_jnp.zeros_like, weights))
        (loss, grad_w), grad_x = _jax.lax.scan(body, init, (per_example, given["loss_target"]))
    with _jax.named_scope("update"):
        delta_w, new_m, new_v = {}, {}, {}
        for n in TWIN_WEIGHTS:
            delta_w[n], new_m[n], new_v[n] = _adamw(weights[n], grad_w[n], given["m_" + n], given["v_" + n])
    return (loss, grad_x, *[grad_w[n] for n in TWIN_WEIGHTS], *[delta_w[n] for n in TWIN_WEIGHTS],
            *[new_m[n] for n in TWIN_WEIGHTS], *[new_v[n] for n in TWIN_WEIGHTS])
```

```python
import functools
import math

import jax
import jax.numpy as jnp
import numpy as np
from jax import lax
from jax.experimental import pallas as pl
from jax.experimental.pallas import tpu as pltpu

f32 = jnp.float32
bf16 = jnp.bfloat16
HIGHEST = lax.Precision.HIGHEST
MESH = pl.DeviceIdType.MESH

N_DEV = 8
D_MODEL = 1024
GRID_W = 64
N_Q_HEADS = 16
N_KV_HEADS = 4
HEAD_DIM = 64
ROPE_THETA = 10000.0
D_INNER = 2048
SSD_HEAD_DIM = 64
N_SSD_HEADS = 32
N_SSD_GROUPS = 4
D_STATE = 128
D_CONV = 5
CHUNK = 128
D_FF = 4096
EPS = 1e-6
CONV_DIM = D_INNER + 2 * N_SSD_GROUPS * D_STATE
GN = N_SSD_GROUPS * D_STATE
PROJ_SIZES = (N_Q_HEADS * HEAD_DIM, N_KV_HEADS * HEAD_DIM, N_KV_HEADS * HEAD_DIM, CONV_DIM, D_INNER,
              2 * N_SSD_HEADS, 2 * D_MODEL)
D_IN_PROJ = sum(PROJ_SIZES)
DT_PAD = 128

ADAM_LR, ADAM_B1, ADAM_B2, ADAM_EPS, ADAM_WD, ADAM_STEP = 0.001, 0.9, 0.999, 1e-08, 0.01, 10

V7X_VMEM_LIMIT = 56 * 1024 * 1024
LANE = 128
PACK_COLS = 1024


def _cparams(**kw):
    return pltpu.CompilerParams(vmem_limit_bytes=V7X_VMEM_LIMIT, **kw)


def _pick(dim, prefs):
    for p in prefs:
        if dim % p == 0:
            return p
    return dim


def _my_index():
    return 4 * lax.axis_index("x") + 2 * lax.axis_index("y") + lax.axis_index("c")


def _all_gather(block, name, in_vmem):
    r, c = block.shape

    def body(x_ref, out_ref, send_sems, recv_sems, local_sem):
        x, y, cc = lax.axis_index("x"), lax.axis_index("y"), lax.axis_index("c")
        me, sibling = (x, y, cc), (x, y, 1 - cc)
        chips = [(1 - x, y), (x, 1 - y), (1 - x, 1 - y)]

        def slot(px, py, pc):
            return out_ref.at[4 * px + 2 * py + pc]

        def copy(k, blk, to, src=None):
            return pltpu.make_async_remote_copy(
                src_ref=slot(*blk) if src is None else src, dst_ref=slot(*blk),
                send_sem=send_sems.at[k], recv_sem=recv_sems.at[k], device_id=to, device_id_type=MESH)

        mine = pltpu.make_async_copy(x_ref, slot(*me), local_sem)
        mine.start()
        first = [copy(0, me, sibling, src=x_ref)]
        first += [copy(1 + j, me, (*chip, cc), src=x_ref) for j, chip in enumerate(chips)]
        for cp in first:
            cp.start()
        passed = [copy(4 + j, (*chip, cc), sibling) for j, chip in enumerate(chips)]
        for j, chip in enumerate(chips):
            copy(1 + j, (*chip, cc), me).wait_recv()
            passed[j].start()
        copy(0, sibling, me).wait_recv()
        for j, chip in enumerate(chips):
            copy(4 + j, (*chip, 1 - cc), me).wait_recv()
        for cp in first + passed:
            cp.wait_send()
        mine.wait()

    space = pltpu.VMEM if in_vmem else pl.ANY
    return pl.pallas_call(
        body, name=name,
        out_shape=jax.ShapeDtypeStruct((N_DEV, r, c), block.dtype),
        in_specs=[pl.BlockSpec(memory_space=space)],
        out_specs=pl.BlockSpec(memory_space=space),
        scratch_shapes=[pltpu.SemaphoreType.DMA((7,)), pltpu.SemaphoreType.DMA((7,)), pltpu.SemaphoreType.DMA],
    )(block)


def _scatter_blocks(g, name):
    _, r, c = g.shape

    def body(g_ref, out_ref, send_sems, recv_sems, local_sem):
        x, y, cc = lax.axis_index("x"), lax.axis_index("y"), lax.axis_index("c")
        me = 4 * x + 2 * y + cc
        mine = pltpu.make_async_copy(g_ref.at[me], out_ref.at[me], local_sem)
        mine.start()

        def copy(k):
            fx, fy, fc = (k >> 2) & 1, (k >> 1) & 1, k & 1
            px = x + fx - 2 * x * fx
            py = y + fy - 2 * y * fy
            pc = cc + fc - 2 * cc * fc
            peer = 4 * px + 2 * py + pc
            send = pltpu.make_async_remote_copy(
                src_ref=g_ref.at[peer], dst_ref=out_ref.at[me],
                send_sem=send_sems.at[k - 1], recv_sem=recv_sems.at[k - 1],
                device_id=(px, py, pc), device_id_type=MESH)
            recv = pltpu.make_async_remote_copy(
                src_ref=g_ref.at[peer], dst_ref=out_ref.at[peer],
                send_sem=send_sems.at[k - 1], recv_sem=recv_sems.at[k - 1],
                device_id=(px, py, pc), device_id_type=MESH)
            return send, recv

        pairs = [copy(k) for k in range(1, N_DEV)]
        for send, _ in pairs:
            send.start()
        for _, recv in pairs:
            recv.wait_recv()
        for send, _ in pairs:
            send.wait_send()
        mine.wait()

    return pl.pallas_call(
        body, name=name,
        out_shape=jax.ShapeDtypeStruct(g.shape, g.dtype),
        in_specs=[pl.BlockSpec(memory_space=pl.ANY)],
        out_specs=pl.BlockSpec(memory_space=pl.ANY),
        scratch_shapes=[pltpu.SemaphoreType.DMA((7,)), pltpu.SemaphoreType.DMA((7,)), pltpu.SemaphoreType.DMA],
    )(g)


_DIMS = {"nn": (((1,), (0,)), ((), ())), "nt": (((1,), (1,)), ((), ())), "tn": (((0,), (0,)), ((), ()))}


def _matmul(a, b, mode, out_dtype, name):
    if mode == "nn":
        (m, k), (_, n) = a.shape, b.shape
    elif mode == "nt":
        (m, k), (n, _) = a.shape, b.shape
    else:
        (k, m), (_, n) = a.shape, b.shape
    tm = _pick(m, (512, 256, 128))
    tn = _pick(n, (512, 384, 256, 128))
    tk = _pick(k, (1024, 512, 256, 128))
    nk = k // tk
    dims = _DIMS[mode]

    def body(a_ref, b_ref, o_ref, acc_ref):
        kk = pl.program_id(2)

        @pl.when(kk == 0)
        def _():
            acc_ref[...] = jnp.zeros_like(acc_ref)

        acc_ref[...] += lax.dot_general(a_ref[...].astype(bf16), b_ref[...].astype(bf16), dims,
                                        preferred_element_type=f32)

        @pl.when(kk == nk - 1)
        def _():
            o_ref[...] = acc_ref[...].astype(out_dtype)

    if mode == "tn":
        a_spec = pl.BlockSpec((tk, tm), lambda i, j, kk: (kk, i))
    else:
        a_spec = pl.BlockSpec((tm, tk), lambda i, j, kk: (i, kk))
    if mode == "nt":
        b_spec = pl.BlockSpec((tn, tk), lambda i, j, kk: (j, kk))
    else:
        b_spec = pl.BlockSpec((tk, tn), lambda i, j, kk: (kk, j))
    return pl.pallas_call(
        body, name=name, grid=(m // tm, n // tn, nk),
        in_specs=[a_spec, b_spec],
        out_specs=pl.BlockSpec((tm, tn), lambda i, j, kk: (i, j)),
        out_shape=jax.ShapeDtypeStruct((m, n), out_dtype),
        scratch_shapes=[pltpu.VMEM((tm, tn), f32)],
        compiler_params=_cparams(dimension_semantics=("parallel", "parallel", "arbitrary")),
    )(a, b)


def make_linear(name):
    @jax.custom_vjp
    def linear(a, w, wgrad):
        return _matmul(a, w, "nn", f32, name + "_fwd")

    def fwd(a, w, wgrad):
        return linear(a, w, wgrad), (a, w)

    def bwd(res, dy):
        a, w = res
        da = _matmul(dy, w, "nt", a.dtype, name + "_dgrad")
        dw = _matmul(a, dy, "tn", f32, name + "_wgrad")
        return da, jnp.zeros_like(w), dw

    linear.defvjp(fwd, bwd)
    return linear


def make_rowwise(name, fn, row_out, sum_out=(), tm_pref=256):
    def specs(rows, gpars, cpars, consts, tm):
        s = [pl.BlockSpec((tm, r.shape[1]), lambda i: (i, 0)) for r in rows]
        s += [pl.BlockSpec(p.shape, lambda i: (0, 0)) for p in gpars]
        s += [pl.BlockSpec(p.shape, lambda i: (0, 0)) for p in cpars]
        for cst in consts:
            nb = cst.shape[0] // tm
            s.append(pl.BlockSpec((tm, cst.shape[1]), lambda i, nb=nb: (i % nb, 0)))
        return s

    def tile_rows(rows, consts):
        r = rows[0].shape[0]
        tm = _pick(r, (tm_pref, 128, 64, 32, 16, 8))
        for cst in consts:
            assert cst.shape[0] % tm == 0
        return r, tm

    def forward(rows, gpars, cpars, consts):
        r, tm = tile_rows(rows, consts)
        nr, ng, nc, nk = len(rows), len(gpars), len(cpars), len(consts)

        def body(*refs):
            ins = refs[:nr + ng + nc + nk]
            outs = refs[nr + ng + nc + nk:]
            rv = [t[...].astype(f32) for t in ins[:nr]]
            gv = [t[...].astype(f32) for t in ins[nr:nr + ng]]
            cv = [t[...] for t in ins[nr + ng:nr + ng + nc]]
            kv = [t[...].astype(f32) for t in ins[nr + ng + nc:]]
            ro, so = fn(rv, gv, cv, kv)
            for o_ref, val in zip(outs[:len(row_out)], ro):
                o_ref[...] = val.astype(o_ref.dtype)
            if sum_out:
                @pl.when(pl.program_id(0) == 0)
                def _():
                    for o_ref in outs[len(row_out):]:
                        o_ref[...] = jnp.zeros_like(o_ref)
                for o_ref, val in zip(outs[len(row_out):], so):
                    o_ref[...] += val

        out_specs = [pl.BlockSpec((tm, w), lambda i: (i, 0)) for w, _ in row_out]
        out_specs += [pl.BlockSpec(shp, lambda i: (0, 0)) for shp in sum_out]
        out_shape = [jax.ShapeDtypeStruct((r, w), dt) for w, dt in row_out]
        out_shape += [jax.ShapeDtypeStruct(shp, f32) for shp in sum_out]
        res = pl.pallas_call(
            body, name=name + "_fwd", grid=(r // tm,),
            in_specs=specs(rows, gpars, cpars, consts, tm), out_specs=out_specs, out_shape=out_shape,
            compiler_params=_cparams(dimension_semantics=("arbitrary",)),
        )(*rows, *gpars, *cpars, *consts)
        return tuple(res[:len(row_out)]), tuple(res[len(row_out):])

    def backward(rows, gpars, cpars, consts, d_ro, d_so):
        r, tm = tile_rows(rows, consts)
        nr, ng, nc, nk = len(rows), len(gpars), len(cpars), len(consts)
        n_in = nr + ng + nc + nk + len(row_out) + len(sum_out)

        def body(*refs):
            ins, outs = refs[:n_in], refs[n_in:]
            rv = [t[...].astype(f32) for t in ins[:nr]]
            gv = [t[...].astype(f32) for t in ins[nr:nr + ng]]
            cv = [t[...] for t in ins[nr + ng:nr + ng + nc]]
            kv = [t[...].astype(f32) for t in ins[nr + ng + nc:nr + ng + nc + nk]]
            o = nr + ng + nc + nk
            dro = [t[...].astype(f32) for t in ins[o:o + len(row_out)]]
            dso = [t[...] for t in ins[o + len(row_out):]]
            _, vjp = jax.vjp(lambda a, b: tuple(tuple(t) for t in fn(a, b, cv, kv)), rv, gv)
            drv, dgv = vjp((tuple(dro), tuple(dso)))
            for o_ref, val in zip(outs[:nr], drv):
                o_ref[...] = val.astype(o_ref.dtype)
            if ng:
                @pl.when(pl.program_id(0) == 0)
                def _():
                    for o_ref in outs[nr:]:
                        o_ref[...] = jnp.zeros_like(o_ref)
                for o_ref, val in zip(outs[nr:], dgv):
                    o_ref[...] += val

        in_specs = specs(rows, gpars, cpars, consts, tm)
        in_specs += [pl.BlockSpec((tm, w), lambda i: (i, 0)) for w, _ in row_out]
        in_specs += [pl.BlockSpec(shp, lambda i: (0, 0)) for shp in sum_out]
        out_specs = [pl.BlockSpec((tm, t.shape[1]), lambda i: (i, 0)) for t in rows]
        out_specs += [pl.BlockSpec(p.shape, lambda i: (0, 0)) for p in gpars]
        out_shape = [jax.ShapeDtypeStruct(t.shape, t.dtype) for t in rows]
        out_shape += [jax.ShapeDtypeStruct(p.shape, f32) for p in gpars]
        res = pl.pallas_call(
            body, name=name + "_bwd", grid=(r // tm,),
            in_specs=in_specs, out_specs=out_specs, out_shape=out_shape,
            compiler_params=_cparams(dimension_semantics=("arbitrary",)),
        )(*rows, *gpars, *cpars, *consts, *d_ro, *d_so)
        return tuple(res[:nr]), tuple(res[nr:])

    @jax.custom_vjp
    def op(rows, gpars, cpars, consts):
        return forward(rows, gpars, cpars, consts)

    def op_fwd(rows, gpars, cpars, consts):
        return forward(rows, gpars, cpars, consts), (rows, gpars, cpars, consts)

    def op_bwd(res, cts):
        rows, gpars, cpars, consts = res
        d_ro, d_so = cts
        drows, dg = backward(rows, gpars, cpars, consts, d_ro, d_so)
        dg = tuple(d.astype(p.dtype) for d, p in zip(dg, gpars))
        return (drows, dg, tuple(jnp.zeros_like(p) for p in cpars), tuple(jnp.zeros_like(k) for k in consts))

    op.defvjp(op_fwd, op_bwd)
    return op


def _rms(x):
    return x * lax.rsqrt(jnp.mean(x * x, axis=-1, keepdims=True) + EPS)


def _silu(x):
    return x * jax.nn.sigmoid(x)


def _fn_norm_mod(rows, gp, cp, ks):
    (x,), (nw, sc, sh) = rows, gp
    return ((_rms(x) * nw) * (1.0 + sc) + sh,), ()


def _fn_qk_norm_rope(rows, gp, cp, ks):
    (t,), (w,), (pm,), (cos, sin) = rows, gp, cp, ks
    u = _rms(t) * w
    pu = jnp.dot(u, pm, precision=HIGHEST, preferred_element_type=f32)
    return (u * cos + pu * sin,), ()


def _fn_softplus(rows, gp, cp, ks):
    (x,), (b,) = rows, gp
    v = x + b
    return (jnp.maximum(v, 0.0) + jnp.log(1.0 + jnp.exp(-jnp.abs(v))),), ()


def _fn_ssd_gate(rows, gp, cp, ks):
    (yf, yb, xs, z), (dexp, nw) = rows, gp
    y = yf + yb + xs * dexp
    return (_rms(y * _silu(z)) * nw,), ()


def _fn_merge(rows, gp, cp, ks):
    ao, so, ga, gs = rows
    return (jax.nn.sigmoid(ga) * ao + jax.nn.sigmoid(gs) * so,), ()


def _fn_res_norm(rows, gp, cp, ks):
    (x, mo), (g1, nw, sc, sh) = rows, gp
    x1 = x + g1 * mo
    return (x1, (_rms(x1) * nw) * (1.0 + sc) + sh), ()


def _fn_relu2(rows, gp, cp, ks):
    (u,) = rows
    r = jnp.maximum(u, 0.0)
    return (r * r,), ()


def _fn_loss(rows, gp, cp, ks):
    (x1, ff), (g2,), (tgt,) = rows, gp, ks
    err = x1 + g2 * ff - tgt
    return (), (0.5 * jnp.sum(jnp.sum(err * err, axis=-1, keepdims=True), axis=0, keepdims=True) / D_MODEL,)


HALO = 8


def _conv_tiles(s, c):
    return _pick(s, (512, 256, 128)), _pick(c, (512, 256, 128))


def _halo_specs(tm, tc, s):
    nb = tm // HALO
    last = s // HALO - 1
    cur = pl.BlockSpec((tm, tc), lambda j, i: (i, j))
    prev = pl.BlockSpec((HALO, tc), lambda j, i: (jnp.maximum(i * nb - 1, 0), j))
    nxt = pl.BlockSpec((HALO, tc), lambda j, i: (jnp.minimum((i + 1) * nb, last), j))
    return cur, prev, nxt


def _fill_halo(buf, cur, prev, nxt, tm, i, n_i):
    buf[HALO:HALO + tm, :] = cur[...]
    buf[0:HALO, :] = jnp.where(i > 0, prev[...], 0.0)
    buf[HALO + tm:, :] = jnp.where(i < n_i - 1, nxt[...], 0.0)


def _conv_fwd(x, w, b):
    s, c = x.shape
    tm, tc = _conv_tiles(s, c)
    n_i = s // tm

    def body(cur, prev, nxt, w_ref, b_ref, o_ref, buf):
        i = pl.program_id(1)
        _fill_halo(buf, cur, prev, nxt, tm, i, n_i)
        pre = jnp.zeros((tm, tc), f32) + b_ref[...]
        for k in range(D_CONV):
            pre = pre + buf[HALO - 2 + k:HALO - 2 + k + tm, :] * w_ref[k:k + 1, :]
        o_ref[...] = _silu(pre)

    cur, prev, nxt = _halo_specs(tm, tc, s)
    return pl.pallas_call(
        body, name="conv_silu_fwd", grid=(c // tc, n_i),
        in_specs=[cur, prev, nxt, pl.BlockSpec((D_CONV, tc), lambda j, i: (0, j)),
                  pl.BlockSpec((1, tc), lambda j, i: (0, j))],
        out_specs=pl.BlockSpec((tm, tc), lambda j, i: (i, j)),
        out_shape=jax.ShapeDtypeStruct((s, c), f32),
        scratch_shapes=[pltpu.VMEM((tm + 2 * HALO, tc), f32)],
        compiler_params=_cparams(dimension_semantics=("parallel", "arbitrary")),
    )(x, x, x, w, b)


def _conv_bwd(x, w, b, dy):
    s, c = x.shape
    tm, tc = _conv_tiles(s, c)
    n_i = s // tm
    ext = tm + 8

    def body(cur, prev, nxt, dcur, dprev, dnxt, w_ref, b_ref, dx_ref, dw_ref, db_ref, xbuf, dbuf, pbuf):
        i = pl.program_id(1)
        _fill_halo(xbuf, cur, prev, nxt, tm, i, n_i)
        _fill_halo(dbuf, dcur, dprev, dnxt, tm, i, n_i)
        pre = jnp.zeros((ext, tc), f32) + b_ref[...]
        for k in range(D_CONV):
            pre = pre + xbuf[2 + k:2 + k + ext, :] * w_ref[k:k + 1, :]
        sg = jax.nn.sigmoid(pre)
        pbuf[...] = dbuf[4:4 + ext, :] * (sg * (1.0 + pre * (1.0 - sg)))
        dx = jnp.zeros((tm, tc), f32)
        for k in range(D_CONV):
            dx = dx + pbuf[6 - k:6 - k + tm, :] * w_ref[k:k + 1, :]
        dx_ref[...] = dx

        @pl.when(i == 0)
        def _():
            dw_ref[...] = jnp.zeros_like(dw_ref)
            db_ref[...] = jnp.zeros_like(db_ref)

        dpre = pbuf[4:4 + tm, :]
        db_ref[...] += jnp.sum(dpre, axis=0, keepdims=True)
        for k in range(D_CONV):
            dw_ref[k:k + 1, :] += jnp.sum(dpre * xbuf[HALO - 2 + k:HALO - 2 + k + tm, :], axis=0, keepdims=True)

    cur, prev, nxt = _halo_specs(tm, tc, s)
    return pl.pallas_call(
        body, name="conv_silu_bwd", grid=(c // tc, n_i),
        in_specs=[cur, prev, nxt, cur, prev, nxt, pl.BlockSpec((D_CONV, tc), lambda j, i: (0, j)),
                  pl.BlockSpec((1, tc), lambda j, i: (0, j))],
        out_specs=[pl.BlockSpec((tm, tc), lambda j, i: (i, j)), pl.BlockSpec((D_CONV, tc), lambda j, i: (0, j)),
                   pl.BlockSpec((1, tc), lambda j, i: (0, j))],
        out_shape=[jax.ShapeDtypeStruct((s, c), f32), jax.ShapeDtypeStruct((D_CONV, c), f32),
                   jax.ShapeDtypeStruct((1, c), f32)],
        scratch_shapes=[pltpu.VMEM((tm + 2 * HALO, tc), f32), pltpu.VMEM((tm + 2 * HALO, tc), f32),
                        pltpu.VMEM((ext, tc), f32)],
        compiler_params=_cparams(dimension_semantics=("parallel", "arbitrary")),
    )(x, x, x, dy, dy, dy, w, b)


@jax.custom_vjp
def conv_silu(x, w, b):
    return _conv_fwd(x, w, b)


def _conv_silu_fwd(x, w, b):
    return _conv_fwd(x, w, b), (x, w, b)


def _conv_silu_bwd(res, dy):
    return _conv_bwd(*res, dy)


conv_silu.defvjp(_conv_silu_fwd, _conv_silu_bwd)


ATT_SCALE = HEAD_DIM ** -0.5
REP = N_Q_HEADS // N_KV_HEADS


def _attn_fwd(q, k, v):
    hq, s, dh = q.shape
    tq = _pick(s, (256, 128))

    def body(q_ref, k_ref, v_ref, o_ref, lse_ref):
        sc = lax.dot_general(q_ref[0], k_ref[0], _DIMS["nt"], preferred_element_type=f32) * ATT_SCALE
        m = jnp.max(sc, axis=-1, keepdims=True)
        p = jnp.exp(sc - m)
        l = jnp.sum(p, axis=-1, keepdims=True)
        o = jnp.dot(p.astype(bf16), v_ref[0], preferred_element_type=f32)
        o_ref[0] = o / l
        lse_ref[0] = m + jnp.log(l)

    return pl.pallas_call(
        body, name="attn_fwd", grid=(hq, s // tq),
        in_specs=[pl.BlockSpec((1, tq, dh), lambda h, i: (h, i, 0)),
                  pl.BlockSpec((1, s, dh), lambda h, i: (h // REP, 0, 0)),
                  pl.BlockSpec((1, s, dh), lambda h, i: (h // REP, 0, 0))],
        out_specs=[pl.BlockSpec((1, tq, dh), lambda h, i: (h, i, 0)),
                   pl.BlockSpec((1, tq, 1), lambda h, i: (h, i, 0))],
        out_shape=[jax.ShapeDtypeStruct((hq, s, dh), f32), jax.ShapeDtypeStruct((hq, s, 1), f32)],
        compiler_params=_cparams(dimension_semantics=("parallel", "arbitrary")),
    )(q, k, v)


def _attn_bwd(q, k, v, kt, do, lse_row, d_row):
    hq, s, dh = q.shape
    tk = _pick(s, (256, 128))
    cq = _pick(s, (2048, 1024, 512, 256, 128))
    n_c = s // cq

    def body(q_ref, do_ref, lse_ref, d_ref, k_ref, v_ref, kt_ref, dqt_ref, dk_ref, dv_ref):
        j = pl.program_id(1)

        @pl.when(j == 0)
        def _():
            dqt_ref[...] = jnp.zeros_like(dqt_ref)

        kk, vv, ktt = k_ref[0], v_ref[0], kt_ref[0]
        dk = jnp.zeros((tk, dh), f32)
        dv = jnp.zeros((tk, dh), f32)
        for c in range(n_c):
            sl = slice(c * cq, (c + 1) * cq)
            qc, doc = q_ref[0, sl, :], do_ref[0, sl, :]
            st = lax.dot_general(kk, qc, _DIMS["nt"], preferred_element_type=f32) * ATT_SCALE
            pt = jnp.exp(st - lse_ref[0, :, sl])
            dv = dv + jnp.dot(pt.astype(bf16), doc, preferred_element_type=f32)
            dpt = lax.dot_general(vv, doc, _DIMS["nt"], preferred_element_type=f32)
            dst = (pt * (dpt - d_ref[0, :, sl]) * ATT_SCALE).astype(bf16)
            dk = dk + jnp.dot(dst, qc, preferred_element_type=f32)
            dqt_ref[0, :, sl] += jnp.dot(ktt, dst, preferred_element_type=f32)
        dk_ref[0] = dk
        dv_ref[0] = dv

    return pl.pallas_call(
        body, name="attn_bwd", grid=(hq, s // tk),
        in_specs=[pl.BlockSpec((1, s, dh), lambda h, j: (h, 0, 0)),
                  pl.BlockSpec((1, s, dh), lambda h, j: (h, 0, 0)),
                  pl.BlockSpec((1, 1, s), lambda h, j: (h, 0, 0)),
                  pl.BlockSpec((1, 1, s), lambda h, j: (h, 0, 0)),
                  pl.BlockSpec((1, tk, dh), lambda h, j: (h // REP, j, 0)),
                  pl.BlockSpec((1, tk, dh), lambda h, j: (h // REP, j, 0)),
                  pl.BlockSpec((1, dh, tk), lambda h, j: (h // REP, 0, j))],
        out_specs=[pl.BlockSpec((1, dh, s), lambda h, j: (h, 0, 0)),
                   pl.BlockSpec((1, tk, dh), lambda h, j: (h, j, 0)),
                   pl.BlockSpec((1, tk, dh), lambda h, j: (h, j, 0))],
        out_shape=[jax.ShapeDtypeStruct((hq, dh, s), f32), jax.ShapeDtypeStruct((hq, s, dh), f32),
                   jax.ShapeDtypeStruct((hq, s, dh), f32)],
        compiler_params=_cparams(dimension_semantics=("parallel", "arbitrary")),
    )(q, do, lse_row, d_row, k, v, kt)


@jax.custom_vjp
def attention(q, k, v):
    return _attn_fwd(q, k, v)[0]


def _attention_fwd(q, k, v):
    o, lse = _attn_fwd(q, k, v)
    return o, (q, k, v, o, lse)


def _attention_bwd(res, do):
    q, k, v, o, lse = res
    hq, s, dh = q.shape
    d_row = jnp.sum(do * o, axis=-1).reshape(hq, 1, s)
    dqt, dkp, dvp = _attn_bwd(q, k, v, jnp.swapaxes(k, 1, 2), do.astype(bf16), lse.reshape(hq, 1, s), d_row)
    dq = jnp.swapaxes(dqt, 1, 2).astype(q.dtype)
    dk = dkp.reshape(N_KV_HEADS, REP, s, dh).sum(axis=1).astype(k.dtype)
    dv = dvp.reshape(N_KV_HEADS, REP, s, dh).sum(axis=1).astype(v.dtype)
    return dq, dk, dv


attention.defvjp(_attention_fwd, _attention_bwd)


HPG = N_SSD_HEADS // N_SSD_GROUPS
NEG = -1e30


def _ssd_head(xs, dtc, ec, tt, cb, bg, cg, hst, mask):
    l = xs.shape[0]
    xd = xs * dtc
    col = jnp.broadcast_to(ec, (l, l))
    lam = jnp.exp(jnp.where(mask, col - col.T, NEG))
    y = jnp.dot((cb * lam).astype(bf16), xd.astype(bf16), preferred_element_type=f32)
    y = y + lax.dot_general(cg.astype(bf16), hst.astype(bf16), _DIMS["nt"], preferred_element_type=f32) * jnp.exp(ec)
    dec = jnp.exp(tt - ec)
    hn = jnp.exp(tt) * hst + lax.dot_general((xd * dec).astype(bf16), bg.astype(bf16), _DIMS["tn"],
                                             preferred_element_type=f32)
    return y, hn


def _ssd_masks(reverse):
    r = lax.broadcasted_iota(jnp.int32, (CHUNK, CHUNK), 0)
    c = lax.broadcasted_iota(jnp.int32, (CHUNK, CHUNK), 1)
    return (r <= c) if reverse else (r >= c)


def _ssd_in_specs(cidx):
    return [pl.BlockSpec((CHUNK, D_INNER), lambda c: (cidx(c), 0)),
            pl.BlockSpec((CHUNK, GN), lambda c: (cidx(c), D_INNER // GN)),
            pl.BlockSpec((CHUNK, GN), lambda c: (cidx(c), D_INNER // GN + 1)),
            pl.BlockSpec((CHUNK, N_SSD_HEADS), lambda c: (cidx(c), 0)),
            pl.BlockSpec((1, N_SSD_HEADS), lambda c: (0, 0))]


def _ssd_fwd(xbc, dt, a_neg, reverse):
    s = xbc.shape[0]
    nc = s // CHUNK
    cidx = (lambda c: nc - 1 - c) if reverse else (lambda c: c)

    def body(xs_ref, b_ref, c_ref, dt_ref, a_ref, y_ref, st_ref, h_ref):
        @pl.when(pl.program_id(0) == 0)
        def _():
            h_ref[...] = jnp.zeros_like(h_ref)

        mask = _ssd_masks(reverse)
        tri = mask.astype(f32)
        dtv = dt_ref[...]
        a = dtv * a_ref[...]
        e = jnp.dot(tri, a, precision=HIGHEST, preferred_element_type=f32)
        tot = jnp.sum(a, axis=0, keepdims=True)
        for g in range(N_SSD_GROUPS):
            bg = b_ref[:, g * D_STATE:(g + 1) * D_STATE]
            cg = c_ref[:, g * D_STATE:(g + 1) * D_STATE]
            cb = lax.dot_general(cg.astype(bf16), bg.astype(bf16), _DIMS["nt"], preferred_element_type=f32)
            for r in range(HPG):
                h = g * HPG + r
                hs = slice(h * SSD_HEAD_DIM, (h + 1) * SSD_HEAD_DIM)
                hst = h_ref[h]
                st_ref[0, h] = hst
                y, hn = _ssd_head(xs_ref[:, hs], dtv[:, h:h + 1], e[:, h:h + 1], tot[:, h:h + 1], cb, bg, cg, hst,
                                  mask)
                y_ref[:, hs] = y
                h_ref[h] = hn

    return pl.pallas_call(
        body, name="ssd_fwd_rev" if reverse else "ssd_fwd", grid=(nc,),
        in_specs=_ssd_in_specs(cidx),
        out_specs=[pl.BlockSpec((CHUNK, D_INNER), lambda c: (cidx(c), 0)),
                   pl.BlockSpec((1, N_SSD_HEADS, SSD_HEAD_DIM, D_STATE), lambda c: (cidx(c), 0, 0, 0))],
        out_shape=[jax.ShapeDtypeStruct((s, D_INNER), f32),
                   jax.ShapeDtypeStruct((nc, N_SSD_HEADS, SSD_HEAD_DIM, D_STATE), f32)],
        scratch_shapes=[pltpu.VMEM((N_SSD_HEADS, SSD_HEAD_DIM, D_STATE), f32)],
        compiler_params=_cparams(dimension_semantics=("arbitrary",)),
    )(xbc, xbc, xbc, dt, a_neg)


def _ssd_bwd(xbc, dt, a_neg, states, dy, reverse):
    s = xbc.shape[0]
    nc = s // CHUNK
    cidx = (lambda c: c) if reverse else (lambda c: nc - 1 - c)

    def body(xs_ref, b_ref, c_ref, dt_ref, a_ref, st_ref, dy_ref, dxbc_ref, ddt_ref, da_ref, dh_ref):
        @pl.when(pl.program_id(0) == 0)
        def _():
            dh_ref[...] = jnp.zeros_like(dh_ref)
            da_ref[...] = jnp.zeros_like(da_ref)

        mask = _ssd_masks(reverse)
        tri = mask.astype(f32)
        lane = lax.broadcasted_iota(jnp.int32, (1, N_SSD_HEADS), 1)
        dtv = dt_ref[...]
        av = a_ref[...]
        a = dtv * av
        e = jnp.dot(tri, a, precision=HIGHEST, preferred_element_type=f32)
        tot = jnp.sum(a, axis=0, keepdims=True)
        de_all = jnp.zeros((CHUNK, N_SSD_HEADS), f32)
        ddt_all = jnp.zeros((CHUNK, N_SSD_HEADS), f32)
        dtot_all = jnp.zeros((1, N_SSD_HEADS), f32)
        for g in range(N_SSD_GROUPS):
            gs = slice(g * D_STATE, (g + 1) * D_STATE)
            bg, cg = b_ref[:, gs], c_ref[:, gs]
            cb = lax.dot_general(cg.astype(bf16), bg.astype(bf16), _DIMS["nt"], preferred_element_type=f32)
            dcb = jnp.zeros((CHUNK, CHUNK), f32)
            dbg = jnp.zeros((CHUNK, D_STATE), f32)
            dcg = jnp.zeros((CHUNK, D_STATE), f32)
            for r in range(HPG):
                h = g * HPG + r
                hs = slice(h * SSD_HEAD_DIM, (h + 1) * SSD_HEAD_DIM)
                onehot = (lane == h).astype(f32)
                _, vjp = jax.vjp(functools.partial(_ssd_head, mask=mask), xs_ref[:, hs], dtv[:, h:h + 1],
                                 e[:, h:h + 1], tot[:, h:h + 1], cb, bg, cg, st_ref[0, h])
                dxs, ddtc, dec, dtt, dcb_h, dbg_h, dcg_h, dhp = vjp((dy_ref[:, hs], dh_ref[h]))
                dxbc_ref[:, hs] = dxs
                dh_ref[h] = dhp
                ddt_all = ddt_all + ddtc * onehot
                de_all = de_all + dec * onehot
                dtot_all = dtot_all + dtt * onehot
                dcb, dbg, dcg = dcb + dcb_h, dbg + dbg_h, dcg + dcg_h
            dcb16 = dcb.astype(bf16)
            dcg = dcg + jnp.dot(dcb16, bg.astype(bf16), preferred_element_type=f32)
            dbg = dbg + lax.dot_general(dcb16, cg.astype(bf16), _DIMS["tn"], preferred_element_type=f32)
            dxbc_ref[:, D_INNER + g * D_STATE:D_INNER + (g + 1) * D_STATE] = dbg
            dxbc_ref[:, D_INNER + GN + g * D_STATE:D_INNER + GN + (g + 1) * D_STATE] = dcg
        da = lax.dot_general(tri, de_all, _DIMS["tn"], precision=HIGHEST, preferred_element_type=f32) + dtot_all
        ddt_ref[...] = ddt_all + da * av
        da_ref[...] += jnp.sum(da * dtv, axis=0, keepdims=True)

    in_specs = _ssd_in_specs(cidx) + [
        pl.BlockSpec((1, N_SSD_HEADS, SSD_HEAD_DIM, D_STATE), lambda c: (cidx(c), 0, 0, 0)),
        pl.BlockSpec((CHUNK, D_INNER), lambda c: (cidx(c), 0))]
    return pl.pallas_call(
        body, name="ssd_bwd_rev" if reverse else "ssd_bwd", grid=(nc,),
        in_specs=in_specs,
        out_specs=[pl.BlockSpec((CHUNK, CONV_DIM), lambda c: (cidx(c), 0)),
                   pl.BlockSpec((CHUNK, N_SSD_HEADS), lambda c: (cidx(c), 0)),
                   pl.BlockSpec((1, N_SSD_HEADS), lambda c: (0, 0))],
        out_shape=[jax.ShapeDtypeStruct((s, CONV_DIM), f32), jax.ShapeDtypeStruct((s, N_SSD_HEADS), f32),
                   jax.ShapeDtypeStruct((1, N_SSD_HEADS), f32)],
        scratch_shapes=[pltpu.VMEM((N_SSD_HEADS, SSD_HEAD_DIM, D_STATE), f32)],
        compiler_params=_cparams(dimension_semantics=("arbitrary",)),
    )(xbc, xbc, xbc, dt, a_neg, states, dy)


def make_ssd(reverse):
    @jax.custom_vjp
    def ssd(xbc, dt, a_neg):
        return _ssd_fwd(xbc, dt, a_neg, reverse)[0]

    def fwd(xbc, dt, a_neg):
        y, st = _ssd_fwd(xbc, dt, a_neg, reverse)
        return y, (xbc, dt, a_neg, st)

    def bwd(res, dy):
        xbc, dt, a_neg, st = res
        return tuple(_ssd_bwd(xbc, dt, a_neg, st, dy, reverse))

    ssd.defvjp(fwd, bwd)
    return ssd


W_NAMES = ("q", "k", "v", "xbc", "z", "dt", "gates", "attn_out", "ssd_out", "o", "mlp1", "mlp2")


def _rope_tables(s):
    rows = s // GRID_W
    pos_row = jnp.repeat(jnp.arange(rows, dtype=jnp.int32), GRID_W).astype(f32)
    pos_col = jnp.tile(jnp.arange(GRID_W, dtype=jnp.int32), rows).astype(f32)
    axis_dim = HEAD_DIM // 2
    inv_freq = ROPE_THETA ** (-jnp.arange(0, axis_dim, 2, dtype=f32) / axis_dim)
    ang_r = pos_row[:, None] * inv_freq[None, :]
    ang_c = pos_col[:, None] * inv_freq[None, :]
    cos = jnp.concatenate([jnp.cos(ang_r), jnp.cos(ang_r), jnp.cos(ang_c), jnp.cos(ang_c)], axis=-1)
    sin = jnp.concatenate([jnp.sin(ang_r), jnp.sin(ang_r), jnp.sin(ang_c), jnp.sin(ang_c)], axis=-1)
    return cos, sin


def _rope_perm():
    p = np.zeros((HEAD_DIM, HEAD_DIM), np.float32)
    for j in range(HEAD_DIM):
        if (j % 32) < 16:
            p[j + 16, j] = -1.0
        else:
            p[j - 16, j] = 1.0
    return jnp.asarray(p)


def local_loss(x, mod, small, wgrads, wfull, target):
    s = x.shape[0]
    lin = {n: make_linear("lin_" + n) for n in W_NAMES}
    shift1, scale1, gate1, shift2, scale2, gate2 = [mod[i] for i in range(6)]

    norm_mod = make_rowwise("norm_mod", _fn_norm_mod, [(D_MODEL, f32)])
    (h,), _ = norm_mod((x,), (small["norm1_w"], scale1, shift1), (), ())

    proj = {n: lin[n](h, wfull[n], wgrads[n]) for n in ("q", "k", "v", "xbc", "z", "dt", "gates")}

    cos, sin = _rope_tables(s)
    pm = _rope_perm()

    def heads(t, nh):
        return t.reshape(s, nh, HEAD_DIM).transpose(1, 0, 2)

    qk_q = make_rowwise("q_norm_rope", _fn_qk_norm_rope, [(HEAD_DIM, bf16)])
    qk_k = make_rowwise("k_norm_rope", _fn_qk_norm_rope, [(HEAD_DIM, bf16)])
    (qr,), _ = qk_q((heads(proj["q"], N_Q_HEADS).reshape(N_Q_HEADS * s, HEAD_DIM),), (small["q_norm_w"],), (pm,),
                    (cos, sin))
    (kr,), _ = qk_k((heads(proj["k"], N_KV_HEADS).reshape(N_KV_HEADS * s, HEAD_DIM),), (small["k_norm_w"],), (pm,),
                    (cos, sin))
    vh = heads(proj["v"], N_KV_HEADS).astype(bf16)
    att = attention(qr.reshape(N_Q_HEADS, s, HEAD_DIM), kr.reshape(N_KV_HEADS, s, HEAD_DIM), vh)
    att = att.transpose(1, 0, 2).reshape(s, N_Q_HEADS * HEAD_DIM)
    ao = lin["attn_out"](att, wfull["attn_out"], wgrads["attn_out"])

    xbc = conv_silu(proj["xbc"], small["conv_w"], small["conv_b"])
    softplus = make_rowwise("dt_softplus", _fn_softplus, [(2 * N_SSD_HEADS, f32)])
    (dt,), _ = softplus((proj["dt"][:, :2 * N_SSD_HEADS],), (small["dt_bias"].reshape(1, 2 * N_SSD_HEADS),), (), ())
    a_neg = -jnp.exp(small["A_log"])
    y_f = make_ssd(False)(xbc, dt[:, :N_SSD_HEADS], a_neg[0:1])
    y_b = make_ssd(True)(xbc, dt[:, N_SSD_HEADS:], a_neg[1:2])
    dexp = jnp.repeat(small["ssd_D"].reshape(N_SSD_HEADS), SSD_HEAD_DIM).reshape(1, D_INNER)
    ssd_gate = make_rowwise("ssd_gate", _fn_ssd_gate, [(D_INNER, f32)], tm_pref=128)
    (ssd_out,), _ = ssd_gate((y_f, y_b, xbc[:, :D_INNER], proj["z"]), (dexp, small["ssd_norm_w"]), (), ())
    so = lin["ssd_out"](ssd_out, wfull["ssd_out"], wgrads["ssd_out"])

    merge = make_rowwise("merge", _fn_merge, [(D_MODEL, f32)])
    (merged,), _ = merge((ao, so, proj["gates"][:, :D_MODEL], proj["gates"][:, D_MODEL:]), (), (), ())
    mo = lin["o"](merged, wfull["o"], wgrads["o"])

    res_norm = make_rowwise("res_norm", _fn_res_norm, [(D_MODEL, f32), (D_MODEL, f32)])
    (x1, h2), _ = res_norm((x, mo), (gate1, small["norm2_w"], scale2, shift2), (), ())
    u = lin["mlp1"](h2, wfull["mlp1"], wgrads["mlp1"])
    relu2 = make_rowwise("relu2", _fn_relu2, [(D_FF, f32)], tm_pref=128)
    (r,), _ = relu2((u,), (), (), ())
    ff = lin["mlp2"](r, wfull["mlp2"], wgrads["mlp2"])
    loss_op = make_rowwise("loss", _fn_loss, [], [(1, 1)])
    _, (loss,) = loss_op((x1, ff), (gate2,), (), (target,))
    return loss[0, 0]


_BC1 = 1.0 - ADAM_B1 ** ADAM_STEP
_BC2 = 1.0 - ADAM_B2 ** ADAM_STEP


def _adamw(w, g, m, v):
    m = ADAM_B1 * m + (1.0 - ADAM_B1) * g
    v = ADAM_B2 * v + (1.0 - ADAM_B2) * (g * g)
    delta = -ADAM_LR * ((m / _BC1) / (jnp.sqrt(v / _BC2) + ADAM_EPS) + ADAM_WD * w)
    return delta, m, v


def _ada_fwd(c_all, w, b):
    n = w.shape[1]

    def body(c_ref, w_ref, b_ref, o_ref):
        o_ref[...] = jnp.dot(_silu(c_ref[...]), w_ref[...], precision=HIGHEST, preferred_element_type=f32) + b_ref[...]

    return pl.pallas_call(body, name="ada_fwd", out_shape=jax.ShapeDtypeStruct((N_DEV, n), f32),
                          compiler_params=_cparams())(c_all, w, b)


def _ada_bwd_adamw(c_all, dmod, w, m, v):
    d, n = w.shape
    tr = _pick(d, (256, 128))

    def body(c_ref, dm_ref, w_ref, m_ref, v_ref, g_ref, dl_ref, mo_ref, vo_ref):
        g = lax.dot_general(_silu(c_ref[...]), dm_ref[...], _DIMS["tn"], precision=HIGHEST,
                            preferred_element_type=f32)
        g_ref[...] = g
        dl_ref[...], mo_ref[...], vo_ref[...] = _adamw(w_ref[...], g, m_ref[...], v_ref[...])

    blk = pl.BlockSpec((tr, n), lambda i: (i, 0))
    return pl.pallas_call(
        body, name="ada_bwd_adamw", grid=(d // tr,),
        in_specs=[pl.BlockSpec((N_DEV, tr), lambda i: (0, i)), pl.BlockSpec((N_DEV, n), lambda i: (0, 0)), blk, blk, blk],
        out_specs=[blk] * 4, out_shape=[jax.ShapeDtypeStruct((d, n), f32)] * 4,
        compiler_params=_cparams(dimension_semantics=("parallel",)),
    )(c_all, dmod, w, m, v)


def _sum_over_mesh(g):
    def body(g_ref, o_ref):
        acc = g_ref[0]
        for d in range(1, N_DEV):
            acc = acc + g_ref[d]
        o_ref[...] = acc

    return pl.pallas_call(body, name="sum_small", out_shape=jax.ShapeDtypeStruct(g.shape[1:], f32),
                          compiler_params=_cparams())(g)


def _adamw_small(w, g, m, v):
    def body(w_ref, g_ref, m_ref, v_ref, dl_ref, mo_ref, vo_ref):
        dl_ref[...], mo_ref[...], vo_ref[...] = _adamw(w_ref[...], g_ref[...], m_ref[...], v_ref[...])

    return pl.pallas_call(body, name="adamw_small", out_shape=[jax.ShapeDtypeStruct(w.shape, f32)] * 3,
                          compiler_params=_cparams())(w, g, m, v)


def _sum_adamw(recv, w, m, v):
    _, r, c = recv.shape
    tr = _pick(r, (264, 256, 128, 64, 8))

    def body(g_ref, w_ref, m_ref, v_ref, go_ref, dl_ref, mo_ref, vo_ref):
        g = g_ref[0]
        for d in range(1, N_DEV):
            g = g + g_ref[d]
        go_ref[...] = g
        dl_ref[...], mo_ref[...], vo_ref[...] = _adamw(w_ref[...], g, m_ref[...], v_ref[...])

    blk = pl.BlockSpec((tr, c), lambda i: (i, 0))
    return pl.pallas_call(
        body, name="sum_adamw", grid=(r // tr,),
        in_specs=[pl.BlockSpec((N_DEV, tr, c), lambda i: (0, i, 0)), blk, blk, blk],
        out_specs=[blk] * 4, out_shape=[jax.ShapeDtypeStruct((r, c), f32)] * 4,
        compiler_params=_cparams(dimension_semantics=("parallel",)),
    )(recv, w, m, v)


def _pack_small(arrs):
    parts = []
    for a in arrs:
        flat = a.reshape(-1).astype(f32)
        parts.append(jnp.pad(flat, (0, (-flat.shape[0]) % LANE)))
    flat = jnp.concatenate(parts)
    flat = jnp.pad(flat, (0, (-flat.shape[0]) % (8 * LANE)))
    return flat.reshape(-1, LANE)


def _unpack_small(packed, shapes):
    flat = packed.reshape(-1)
    out, off = [], 0
    for shp in shapes:
        n = int(np.prod(shp))
        out.append(flat[off:off + n].reshape(shp))
        off += n + (-n) % LANE
    return out


BIG = ("w_in", "w_attn_out", "w_ssd_out", "w_o", "w_mlp1", "w_mlp2")
BIG_ROWS = (D_MODEL * (D_IN_PROJ // N_DEV) // PACK_COLS, N_Q_HEADS * HEAD_DIM // N_DEV, D_INNER // N_DEV,
            D_MODEL // N_DEV, D_MODEL * (D_FF // N_DEV) // PACK_COLS, D_FF // N_DEV)
BIG_PAD_ROWS = (-sum(BIG_ROWS)) % 16


def _pack_big(shards, dtype):
    parts = [s.astype(dtype).reshape(-1, PACK_COLS) for s in shards]
    parts.append(jnp.zeros((BIG_PAD_ROWS, PACK_COLS), dtype))
    return jnp.concatenate(parts, axis=0)


def _unpack_big(packed, shapes):
    out, off = [], 0
    for rows, shp in zip(BIG_ROWS, shapes):
        out.append(packed[off:off + rows].reshape(shp))
        off += rows
    return out


def _split_gathered(g):
    offs = np.cumsum((0,) + BIG_ROWS)
    sl = [g[:, offs[i]:offs[i + 1]] for i in range(len(BIG))]
    n_in = D_IN_PROJ // N_DEV
    w_in = sl[0].reshape(N_DEV, D_MODEL, n_in).transpose(1, 0, 2).reshape(D_MODEL, D_IN_PROJ)
    w = {}
    off = 0
    for name, size in zip(("q", "k", "v", "xbc", "z", "dt", "gates"), PROJ_SIZES):
        w[name] = w_in[:, off:off + size]
        off += size
    w["dt"] = jnp.pad(w["dt"], ((0, 0), (0, DT_PAD - 2 * N_SSD_HEADS)))
    w["attn_out"] = sl[1].reshape(N_Q_HEADS * HEAD_DIM, D_MODEL)
    w["ssd_out"] = sl[2].reshape(D_INNER, D_MODEL)
    w["o"] = sl[3].reshape(D_MODEL, D_MODEL)
    w["mlp1"] = sl[4].reshape(N_DEV, D_MODEL, D_FF // N_DEV).transpose(1, 0, 2).reshape(D_MODEL, D_FF)
    w["mlp2"] = sl[5].reshape(D_FF, D_MODEL)
    return w


def _pack_full_grads(gw):
    n_in = D_IN_PROJ // N_DEV
    g_in = jnp.concatenate([gw["q"], gw["k"], gw["v"], gw["xbc"], gw["z"], gw["dt"][:, :2 * N_SSD_HEADS],
                            gw["gates"]], axis=1)
    parts = [
        g_in.reshape(D_MODEL, N_DEV, n_in).transpose(1, 0, 2).reshape(N_DEV, -1, PACK_COLS),
        gw["attn_out"].reshape(N_DEV, -1, PACK_COLS),
        gw["ssd_out"].reshape(N_DEV, -1, PACK_COLS),
        gw["o"].reshape(N_DEV, -1, PACK_COLS),
        gw["mlp1"].reshape(D_MODEL, N_DEV, D_FF // N_DEV).transpose(1, 0, 2).reshape(N_DEV, -1, PACK_COLS),
        gw["mlp2"].reshape(N_DEV, -1, PACK_COLS),
        jnp.zeros((N_DEV, BIG_PAD_ROWS, PACK_COLS), f32),
    ]
    return jnp.concatenate(parts, axis=1)


SMALL = ("norm1_w", "norm2_w", "q_norm_w", "k_norm_w", "conv_w", "conv_b", "A_log", "dt_bias", "ssd_D", "ssd_norm_w")


def kernel(x, c, w_ada, b_ada, norm1_w, norm2_w, w_in, q_norm_w, k_norm_w, conv_w, conv_b, A_log, dt_bias, ssd_D, ssd_norm_w, w_attn_out, w_ssd_out, w_o, w_mlp1, w_mlp2, loss_target, m_w_ada, m_b_ada, m_norm1_w, m_norm2_w, m_w_in, m_q_norm_w, m_k_norm_w, m_conv_w, m_conv_b, m_A_log, m_dt_bias, m_ssd_D, m_ssd_norm_w, m_w_attn_out, m_w_ssd_out, m_w_o, m_w_mlp1, m_w_mlp2, v_w_ada, v_b_ada, v_norm1_w, v_norm2_w, v_w_in, v_q_norm_w, v_k_norm_w, v_conv_w, v_conv_b, v_A_log, v_dt_bias, v_ssd_D, v_ssd_norm_w, v_w_attn_out, v_w_ssd_out, v_w_o, v_w_mlp1, v_w_mlp2):
    args = dict(locals())
    me = _my_index()
    n_ada = 6 * D_MODEL // N_DEV
    n_cw = CONV_DIM // N_DEV

    blk = jnp.zeros((8, D_MODEL), f32)
    blk = blk.at[0:1, :].set(c)
    blk = blk.at[1:1 + D_CONV, :n_cw].set(conv_w[0])
    g0 = _all_gather(blk, "gather_c_convw", in_vmem=True)
    c_all = g0[:, 0, :]
    conv_w_full = g0[:, 1:1 + D_CONV, :n_cw].transpose(1, 0, 2).reshape(D_CONV, CONV_DIM)

    b_shard = lax.dynamic_slice(b_ada, (0, me * n_ada), (1, n_ada))
    mod_cols = _ada_fwd(c_all, w_ada[0], b_shard)
    g1 = _all_gather(mod_cols, "gather_mod", in_vmem=True)
    mod_mine = lax.dynamic_index_in_dim(g1, me, axis=1, keepdims=False)
    mod = mod_mine.reshape(6, 1, D_MODEL)

    big_shapes = [args[n].shape[1:] for n in BIG]
    packed16 = _pack_big([args[n][0] for n in BIG], bf16)
    wfull = _split_gathered(_all_gather(packed16, "gather_weights", in_vmem=False))
    wgrads = {n: jnp.zeros(wfull[n].shape, f32) for n in W_NAMES}

    small = {"norm1_w": norm1_w, "norm2_w": norm2_w, "q_norm_w": q_norm_w, "k_norm_w": k_norm_w,
             "conv_w": conv_w_full, "conv_b": conv_b, "A_log": A_log[0], "dt_bias": dt_bias[0], "ssd_D": ssd_D,
             "ssd_norm_w": ssd_norm_w}

    loss, (gx, gmod, gsmall, gw) = jax.value_and_grad(local_loss, argnums=(0, 1, 2, 3))(
        x[0], mod, small, wgrads, wfull, loss_target[0])

    small_list = [gmod, gsmall["norm1_w"], gsmall["norm2_w"], gsmall["q_norm_w"], gsmall["k_norm_w"], gsmall["conv_w"],
                  gsmall["conv_b"], gsmall["A_log"], gsmall["dt_bias"], gsmall["ssd_D"], gsmall["ssd_norm_w"],
                  loss.reshape(1)]
    small_shapes = [a.shape for a in small_list]
    g2 = _all_gather(_pack_small(small_list), "gather_small_grads", in_vmem=True)
    summed = _unpack_small(_sum_over_mesh(g2), small_shapes)
    loss_total = summed[-1][0]
    g_b_ada = summed[0].reshape(1, 6 * D_MODEL)
    g_small = dict(zip(SMALL, summed[1:-1]))
    g_conv_w = lax.dynamic_slice(g_small["conv_w"], (0, me * n_cw), (D_CONV, n_cw))

    dmod_all = g2[:, :6 * D_MODEL // LANE, :].reshape(N_DEV, 6 * D_MODEL)
    dmod_shard = lax.dynamic_slice(dmod_all, (0, me * n_ada), (N_DEV, n_ada))
    ada = _ada_bwd_adamw(c_all, dmod_shard, w_ada[0], m_w_ada[0], v_w_ada[0])

    small_grads = {"b_ada": g_b_ada, "norm1_w": g_small["norm1_w"], "norm2_w": g_small["norm2_w"],
                   "q_norm_w": g_small["q_norm_w"], "k_norm_w": g_small["k_norm_w"], "conv_w": g_conv_w[None],
                   "conv_b": g_small["conv_b"], "A_log": g_small["A_log"][None], "dt_bias": g_small["dt_bias"][None],
                   "ssd_D": g_small["ssd_D"], "ssd_norm_w": g_small["ssd_norm_w"]}
    sm_names = list(small_grads)
    sm_shapes = [args[n].shape for n in sm_names]
    sm = _adamw_small(_pack_small([args[n] for n in sm_names]), _pack_small([small_grads[n] for n in sm_names]),
                      _pack_small([args["m_" + n] for n in sm_names]), _pack_small([args["v_" + n] for n in sm_names]))
    sm_delta, sm_m, sm_v = [dict(zip(sm_names, _unpack_small(t, sm_shapes))) for t in sm]
    small_grads = {n: small_grads[n].reshape(args[n].shape) for n in sm_names}

    recv = _scatter_blocks(_pack_full_grads(gw), "scatter_grads")
    big = _sum_adamw(recv, _pack_big([args[n][0] for n in BIG], f32), _pack_big([args["m_" + n][0] for n in BIG], f32),
                     _pack_big([args["v_" + n][0] for n in BIG], f32))
    big_g, big_delta, big_m, big_v = [dict(zip(BIG, [t[None] for t in _unpack_big(p, big_shapes)])) for p in big]

    names = ("w_ada", "b_ada", "norm1_w", "norm2_w", "w_in", "q_norm_w", "k_norm_w", "conv_w", "conv_b", "A_log",
             "dt_bias", "ssd_D", "ssd_norm_w", "w_attn_out", "w_ssd_out", "w_o", "w_mlp1", "w_mlp2")
    grads, deltas, new_m, new_v = {}, {}, {}, {}
    for n in names:
        if n == "w_ada":
            grads[n], deltas[n], new_m[n], new_v[n] = [t[None] for t in ada]
        elif n in BIG:
            grads[n], deltas[n], new_m[n], new_v[n] = big_g[n], big_delta[n], big_m[n], big_v[n]
        else:
            grads[n], deltas[n], new_m[n], new_v[n] = small_grads[n], sm_delta[n], sm_m[n], sm_v[n]
    return (loss_total, gx[None], *[grads[n] for n in names], *[deltas[n] for n in names],
            *[new_m[n] for n in names], *[new_v[n] for n in names])
```

```python
import functools
import math

import jax
import jax.numpy as jnp
import numpy as np
from jax import lax
from jax.experimental import pallas as pl
from jax.experimental.pallas import tpu as pltpu

f32 = jnp.float32
bf16 = jnp.bfloat16
HIGHEST = lax.Precision.HIGHEST
MESH = pl.DeviceIdType.MESH

N_DEV = 8
D_MODEL = 1024
GRID_W = 64
N_Q_HEADS = 16
N_KV_HEADS = 4
HEAD_DIM = 64
ROPE_THETA = 10000.0
D_INNER = 2048
SSD_HEAD_DIM = 64
N_SSD_HEADS = 32
N_SSD_GROUPS = 4
D_STATE = 128
D_CONV = 5
CHUNK = 128
D_FF = 4096
EPS = 1e-6
CONV_DIM = D_INNER + 2 * N_SSD_GROUPS * D_STATE
GN = N_SSD_GROUPS * D_STATE
PROJ_SIZES = (N_Q_HEADS * HEAD_DIM, N_KV_HEADS * HEAD_DIM, N_KV_HEADS * HEAD_DIM, CONV_DIM, D_INNER,
              2 * N_SSD_HEADS, 2 * D_MODEL)
D_IN_PROJ = sum(PROJ_SIZES)
DT_PAD = 128

ADAM_LR, ADAM_B1, ADAM_B2, ADAM_EPS, ADAM_WD, ADAM_STEP = 0.001, 0.9, 0.999, 1e-08, 0.01, 10

V7X_VMEM_LIMIT = 56 * 1024 * 1024
LANE = 128
PACK_COLS = 1024


def _cparams(**kw):
    return pltpu.CompilerParams(vmem_limit_bytes=V7X_VMEM_LIMIT, **kw)


def _pick(dim, prefs):
    for p in prefs:
        if dim % p == 0:
            return p
    return dim


def _my_index():
    return 4 * lax.axis_index("x") + 2 * lax.axis_index("y") + lax.axis_index("c")


def _all_gather(block, name, in_vmem):
    r, c = block.shape

    def body(x_ref, out_ref, send_sems, recv_sems, local_sem):
        x, y, cc = lax.axis_index("x"), lax.axis_index("y"), lax.axis_index("c")
        me, sibling = (x, y, cc), (x, y, 1 - cc)
        chips = [(1 - x, y), (x, 1 - y), (1 - x, 1 - y)]

        def slot(px, py, pc):
            return out_ref.at[4 * px + 2 * py + pc]

        def copy(k, blk, to, src=None):
            return pltpu.make_async_remote_copy(
                src_ref=slot(*blk) if src is None else src, dst_ref=slot(*blk),
                send_sem=send_sems.at[k], recv_sem=recv_sems.at[k], device_id=to, device_id_type=MESH)

        mine = pltpu.make_async_copy(x_ref, slot(*me), local_sem)
        mine.start()
        first = [copy(0, me, sibling, src=x_ref)]
        first += [copy(1 + j, me, (*chip, cc), src=x_ref) for j, chip in enumerate(chips)]
        for cp in first:
            cp.start()
        passed = [copy(4 + j, (*chip, cc), sibling) for j, chip in enumerate(chips)]
        for j, chip in enumerate(chips):
            copy(1 + j, (*chip, cc), me).wait_recv()
            passed[j].start()
        copy(0, sibling, me).wait_recv()
        for j, chip in enumerate(chips):
            copy(4 + j, (*chip, 1 - cc), me).wait_recv()
        for cp in first + passed:
            cp.wait_send()
        mine.wait()

    space = pltpu.VMEM if in_vmem else pl.ANY
    return pl.pallas_call(
        body, name=name,
        out_shape=jax.ShapeDtypeStruct((N_DEV, r, c), block.dtype),
        in_specs=[pl.BlockSpec(memory_space=space)],
        out_specs=pl.BlockSpec(memory_space=space),
        scratch_shapes=[pltpu.SemaphoreType.DMA((7,)), pltpu.SemaphoreType.DMA((7,)), pltpu.SemaphoreType.DMA],
    )(block)


def _scatter_blocks(g, name):
    _, r, c = g.shape

    def body(g_ref, out_ref, send_sems, recv_sems, local_sem):
        x, y, cc = lax.axis_index("x"), lax.axis_index("y"), lax.axis_index("c")
        me = 4 * x + 2 * y + cc
        mine = pltpu.make_async_copy(g_ref.at[me], out_ref.at[me], local_sem)
        mine.start()

        def copy(k):
            fx, fy, fc = (k >> 2) & 1, (k >> 1) & 1, k & 1
            px = x + fx - 2 * x * fx
            py = y + fy - 2 * y * fy
            pc = cc + fc - 2 * cc * fc
            peer = 4 * px + 2 * py + pc
            send = pltpu.make_async_remote_copy(
                src_ref=g_ref.at[peer], dst_ref=out_ref.at[me],
                send_sem=send_sems.at[k - 1], recv_sem=recv_sems.at[k - 1],
                device_id=(px, py, pc), device_id_type=MESH)
            recv = pltpu.make_async_remote_copy(
                src_ref=g_ref.at[peer], dst_ref=out_ref.at[peer],
                send_sem=send_sems.at[k - 1], recv_sem=recv_sems.at[k - 1],
                device_id=(px, py, pc), device_id_type=MESH)
            return send, recv

        pairs = [copy(k) for k in range(1, N_DEV)]
        for send, _ in pairs:
            send.start()
        for _, recv in pairs:
            recv.wait_recv()
        for send, _ in pairs:
            send.wait_send()
        mine.wait()

    return pl.pallas_call(
        body, name=name,
        out_shape=jax.ShapeDtypeStruct(g.shape, g.dtype),
        in_specs=[pl.BlockSpec(memory_space=pl.ANY)],
        out_specs=pl.BlockSpec(memory_space=pl.ANY),
        scratch_shapes=[pltpu.SemaphoreType.DMA((7,)), pltpu.SemaphoreType.DMA((7,)), pltpu.SemaphoreType.DMA],
    )(g)


_DIMS = {"nn": (((1,), (0,)), ((), ())), "nt": (((1,), (1,)), ((), ())), "tn": (((0,), (0,)), ((), ()))}


def _matmul(a, b, mode, out_dtype, name):
    if mode == "nn":
        (m, k), (_, n) = a.shape, b.shape
    elif mode == "nt":
        (m, k), (n, _) = a.shape, b.shape
    else:
        (k, m), (_, n) = a.shape, b.shape
    tm = _pick(m, (1024, 512, 256, 128))
    tn = _pick(n, (512, 384, 256, 128))
    tk = _pick(k, (1024, 512, 256, 128))
    nk = k // tk
    dims = _DIMS[mode]

    def body(a_ref, b_ref, o_ref, acc_ref):
        kk = pl.program_id(2)

        @pl.when(kk == 0)
        def _():
            acc_ref[...] = jnp.zeros_like(acc_ref)

        acc_ref[...] += lax.dot_general(a_ref[...].astype(bf16), b_ref[...].astype(bf16), dims,
                                        preferred_element_type=f32)

        @pl.when(kk == nk - 1)
        def _():
            o_ref[...] = acc_ref[...].astype(out_dtype)

    if mode == "tn":
        a_spec = pl.BlockSpec((tk, tm), lambda i, j, kk: (kk, i))
    else:
        a_spec = pl.BlockSpec((tm, tk), lambda i, j, kk: (i, kk))
    if mode == "nt":
        b_spec = pl.BlockSpec((tn, tk), lambda i, j, kk: (j, kk))
    else:
        b_spec = pl.BlockSpec((tk, tn), lambda i, j, kk: (kk, j))
    return pl.pallas_call(
        body, name=name, grid=(m // tm, n // tn, nk),
        in_specs=[a_spec, b_spec],
        out_specs=pl.BlockSpec((tm, tn), lambda i, j, kk: (i, j)),
        out_shape=jax.ShapeDtypeStruct((m, n), out_dtype),
        scratch_shapes=[pltpu.VMEM((tm, tn), f32)],
        compiler_params=_cparams(dimension_semantics=("parallel", "parallel", "arbitrary")),
    )(a, b)


def make_linear(name):
    @jax.custom_vjp
    def linear(a, w, wgrad):
        return _matmul(a, w, "nn", f32, name + "_fwd")

    def fwd(a, w, wgrad):
        return linear(a, w, wgrad), (a, w)

    def bwd(res, dy):
        a, w = res
        da = _matmul(dy, w, "nt", a.dtype, name + "_dgrad")
        dw = _matmul(a, dy, "tn", f32, name + "_wgrad")
        return da, jnp.zeros_like(w), dw

    linear.defvjp(fwd, bwd)
    return linear


def make_rowwise(name, fn, row_out, sum_out=(), tm_pref=256):
    def specs(rows, gpars, cpars, consts, tm):
        s = [pl.BlockSpec((tm, r.shape[1]), lambda i: (i, 0)) for r in rows]
        s += [pl.BlockSpec(p.shape, lambda i: (0, 0)) for p in gpars]
        s += [pl.BlockSpec(p.shape, lambda i: (0, 0)) for p in cpars]
        for cst in consts:
            nb = cst.shape[0] // tm
            s.append(pl.BlockSpec((tm, cst.shape[1]), lambda i, nb=nb: (i % nb, 0)))
        return s

    def tile_rows(rows, consts):
        r = rows[0].shape[0]
        common = math.gcd(r, *[cst.shape[0] for cst in consts])
        tm = _pick(common, (tm_pref, 512, 256, 128, 64, 32, 16, 8))
        return r, tm

    def forward(rows, gpars, cpars, consts):
        r, tm = tile_rows(rows, consts)
        nr, ng, nc, nk = len(rows), len(gpars), len(cpars), len(consts)

        def body(*refs):
            ins = refs[:nr + ng + nc + nk]
            outs = refs[nr + ng + nc + nk:]
            rv = [t[...].astype(f32) for t in ins[:nr]]
            gv = [t[...].astype(f32) for t in ins[nr:nr + ng]]
            cv = [t[...] for t in ins[nr + ng:nr + ng + nc]]
            kv = [t[...].astype(f32) for t in ins[nr + ng + nc:]]
            ro, so = fn(rv, gv, cv, kv)
            for o_ref, val in zip(outs[:len(row_out)], ro):
                o_ref[...] = val.astype(o_ref.dtype)
            if sum_out:
                @pl.when(pl.program_id(0) == 0)
                def _():
                    for o_ref in outs[len(row_out):]:
                        o_ref[...] = jnp.zeros_like(o_ref)
                for o_ref, val in zip(outs[len(row_out):], so):
                    o_ref[...] += val

        out_specs = [pl.BlockSpec((tm, w), lambda i: (i, 0)) for w, _ in row_out]
        out_specs += [pl.BlockSpec(shp, lambda i: (0, 0)) for shp in sum_out]
        out_shape = [jax.ShapeDtypeStruct((r, w), dt) for w, dt in row_out]
        out_shape += [jax.ShapeDtypeStruct(shp, f32) for shp in sum_out]
        res = pl.pallas_call(
            body, name=name + "_fwd", grid=(r // tm,),
            in_specs=specs(rows, gpars, cpars, consts, tm), out_specs=out_specs, out_shape=out_shape,
            compiler_params=_cparams(dimension_semantics=("arbitrary",)),
        )(*rows, *gpars, *cpars, *consts)
        return tuple(res[:len(row_out)]), tuple(res[len(row_out):])

    def backward(rows, gpars, cpars, consts, d_ro, d_so):
        r, tm = tile_rows(rows, consts)
        nr, ng, nc, nk = len(rows), len(gpars), len(cpars), len(consts)
        n_in = nr + ng + nc + nk + len(row_out) + len(sum_out)

        def body(*refs):
            ins, outs = refs[:n_in], refs[n_in:]
            rv = [t[...].astype(f32) for t in ins[:nr]]
            gv = [t[...].astype(f32) for t in ins[nr:nr + ng]]
            cv = [t[...] for t in ins[nr + ng:nr + ng + nc]]
            kv = [t[...].astype(f32) for t in ins[nr + ng + nc:nr + ng + nc + nk]]
            o = nr + ng + nc + nk
            dro = [t[...].astype(f32) for t in ins[o:o + len(row_out)]]
            dso = [t[...] for t in ins[o + len(row_out):]]
            _, vjp = jax.vjp(lambda a, b: tuple(tuple(t) for t in fn(a, b, cv, kv)), rv, gv)
            drv, dgv = vjp((tuple(dro), tuple(dso)))
            for o_ref, val in zip(outs[:nr], drv):
                o_ref[...] = val.astype(o_ref.dtype)
            if ng:
                @pl.when(pl.program_id(0) == 0)
                def _():
                    for o_ref in outs[nr:]:
                        o_ref[...] = jnp.zeros_like(o_ref)
                for o_ref, val in zip(outs[nr:], dgv):
                    o_ref[...] += val

        in_specs = specs(rows, gpars, cpars, consts, tm)
        in_specs += [pl.BlockSpec((tm, w), lambda i: (i, 0)) for w, _ in row_out]
        in_specs += [pl.BlockSpec(shp, lambda i: (0, 0)) for shp in sum_out]
        out_specs = [pl.BlockSpec((tm, t.shape[1]), lambda i: (i, 0)) for t in rows]
        out_specs += [pl.BlockSpec(p.shape, lambda i: (0, 0)) for p in gpars]
        out_shape = [jax.ShapeDtypeStruct(t.shape, t.dtype) for t in rows]
        out_shape += [jax.ShapeDtypeStruct(p.shape, f32) for p in gpars]
        res = pl.pallas_call(
            body, name=name + "_bwd", grid=(r // tm,),
            in_specs=in_specs, out_specs=out_specs, out_shape=out_shape,
            compiler_params=_cparams(dimension_semantics=("arbitrary",)),
        )(*rows, *gpars, *cpars, *consts, *d_ro, *d_so)
        return tuple(res[:nr]), tuple(res[nr:])

    @jax.custom_vjp
    def op(rows, gpars, cpars, consts):
        return forward(rows, gpars, cpars, consts)

    def op_fwd(rows, gpars, cpars, consts):
        return forward(rows, gpars, cpars, consts), (rows, gpars, cpars, consts)

    def op_bwd(res, cts):
        rows, gpars, cpars, consts = res
        d_ro, d_so = cts
        drows, dg = backward(rows, gpars, cpars, consts, d_ro, d_so)
        dg = tuple(d.astype(p.dtype) for d, p in zip(dg, gpars))
        return (drows, dg, tuple(jnp.zeros_like(p) for p in cpars), tuple(jnp.zeros_like(k) for k in consts))

    op.defvjp(op_fwd, op_bwd)
    return op


def _rms(x):
    return x * lax.rsqrt(jnp.mean(x * x, axis=-1, keepdims=True) + EPS)


def _silu(x):
    return x * jax.nn.sigmoid(x)


def _fn_norm_mod(rows, gp, cp, ks):
    (x,), (nw, sc, sh) = rows, gp
    return ((_rms(x) * nw) * (1.0 + sc) + sh,), ()


def _fn_qk_norm_rope(rows, gp, cp, ks, out_scale=1.0):
    (t,), (w,), (pm,), (cos, sin) = rows, gp, cp, ks
    u = _rms(t) * w
    pu = jnp.dot(u, pm, precision=HIGHEST, preferred_element_type=f32)
    return ((u * cos + pu * sin) * out_scale,), ()


def _fn_softplus(rows, gp, cp, ks):
    (x,), (b,) = rows, gp
    v = x + b
    return (jnp.maximum(v, 0.0) + jnp.log(1.0 + jnp.exp(-jnp.abs(v))),), ()


def _fn_ssd_gate(rows, gp, cp, ks):
    (yf, yb, xs, z), (dexp, nw) = rows, gp
    y = yf + yb + xs * dexp
    return (_rms(y * _silu(z)) * nw,), ()


def _fn_merge(rows, gp, cp, ks):
    ao, so, ga, gs = rows
    return (jax.nn.sigmoid(ga) * ao + jax.nn.sigmoid(gs) * so,), ()


def _fn_res_norm(rows, gp, cp, ks):
    (x, mo), (g1, nw, sc, sh) = rows, gp
    x1 = x + g1 * mo
    return (x1, (_rms(x1) * nw) * (1.0 + sc) + sh), ()


def _fn_relu2(rows, gp, cp, ks):
    (u,) = rows
    r = jnp.maximum(u, 0.0)
    return (r * r,), ()


def _fn_loss(rows, gp, cp, ks):
    (x1, ff), (g2,), (tgt,) = rows, gp, ks
    err = x1 + g2 * ff - tgt
    return (), (0.5 * jnp.sum(jnp.sum(err * err, axis=-1, keepdims=True), axis=0, keepdims=True) / D_MODEL,)


HALO = 8


def _conv_tiles(s, c):
    return _pick(s, (512, 256, 128)), _pick(c, (512, 256, 128))


def _halo_specs(tm, tc, s):
    nb = tm // HALO
    last = s // HALO - 1
    cur = pl.BlockSpec((tm, tc), lambda j, i: (i, j))
    prev = pl.BlockSpec((HALO, tc), lambda j, i: (jnp.maximum(i * nb - 1, 0), j))
    nxt = pl.BlockSpec((HALO, tc), lambda j, i: (jnp.minimum((i + 1) * nb, last), j))
    return cur, prev, nxt


def _fill_halo(buf, cur, prev, nxt, tm, i, n_i):
    buf[HALO:HALO + tm, :] = cur[...]
    buf[0:HALO, :] = jnp.where(i > 0, prev[...], 0.0)
    buf[HALO + tm:, :] = jnp.where(i < n_i - 1, nxt[...], 0.0)


def _conv_fwd(x, w, b):
    s, c = x.shape
    tm, tc = _conv_tiles(s, c)
    n_i = s // tm

    def body(cur, prev, nxt, w_ref, b_ref, o_ref, buf):
        i = pl.program_id(1)
        _fill_halo(buf, cur, prev, nxt, tm, i, n_i)
        pre = jnp.zeros((tm, tc), f32) + b_ref[...]
        for k in range(D_CONV):
            pre = pre + buf[HALO - 2 + k:HALO - 2 + k + tm, :] * w_ref[k:k + 1, :]
        o_ref[...] = _silu(pre)

    cur, prev, nxt = _halo_specs(tm, tc, s)
    return pl.pallas_call(
        body, name="conv_silu_fwd", grid=(c // tc, n_i),
        in_specs=[cur, prev, nxt, pl.BlockSpec((D_CONV, tc), lambda j, i: (0, j)),
                  pl.BlockSpec((1, tc), lambda j, i: (0, j))],
        out_specs=pl.BlockSpec((tm, tc), lambda j, i: (i, j)),
        out_shape=jax.ShapeDtypeStruct((s, c), f32),
        scratch_shapes=[pltpu.VMEM((tm + 2 * HALO, tc), f32)],
        compiler_params=_cparams(dimension_semantics=("parallel", "arbitrary")),
    )(x, x, x, w, b)


def _conv_bwd(x, w, b, dy):
    s, c = x.shape
    tm, tc = _conv_tiles(s, c)
    n_i = s // tm
    ext = tm + 8

    def body(cur, prev, nxt, dcur, dprev, dnxt, w_ref, b_ref, dx_ref, dw_ref, db_ref, xbuf, dbuf, pbuf):
        i = pl.program_id(1)
        _fill_halo(xbuf, cur, prev, nxt, tm, i, n_i)
        _fill_halo(dbuf, dcur, dprev, dnxt, tm, i, n_i)
        pre = jnp.zeros((ext, tc), f32) + b_ref[...]
        for k in range(D_CONV):
            pre = pre + xbuf[2 + k:2 + k + ext, :] * w_ref[k:k + 1, :]
        sg = jax.nn.sigmoid(pre)
        pbuf[...] = dbuf[4:4 + ext, :] * (sg * (1.0 + pre * (1.0 - sg)))
        dx = jnp.zeros((tm, tc), f32)
        for k in range(D_CONV):
            dx = dx + pbuf[6 - k:6 - k + tm, :] * w_ref[k:k + 1, :]
        dx_ref[...] = dx

        @pl.when(i == 0)
        def _():
            dw_ref[...] = jnp.zeros_like(dw_ref)
            db_ref[...] = jnp.zeros_like(db_ref)

        dpre = pbuf[4:4 + tm, :]
        db_ref[...] += jnp.sum(dpre, axis=0, keepdims=True)
        for k in range(D_CONV):
            dw_ref[k:k + 1, :] += jnp.sum(dpre * xbuf[HALO - 2 + k:HALO - 2 + k + tm, :], axis=0, keepdims=True)

    cur, prev, nxt = _halo_specs(tm, tc, s)
    return pl.pallas_call(
        body, name="conv_silu_bwd", grid=(c // tc, n_i),
        in_specs=[cur, prev, nxt, cur, prev, nxt, pl.BlockSpec((D_CONV, tc), lambda j, i: (0, j)),
                  pl.BlockSpec((1, tc), lambda j, i: (0, j))],
        out_specs=[pl.BlockSpec((tm, tc), lambda j, i: (i, j)), pl.BlockSpec((D_CONV, tc), lambda j, i: (0, j)),
                   pl.BlockSpec((1, tc), lambda j, i: (0, j))],
        out_shape=[jax.ShapeDtypeStruct((s, c), f32), jax.ShapeDtypeStruct((D_CONV, c), f32),
                   jax.ShapeDtypeStruct((1, c), f32)],
        scratch_shapes=[pltpu.VMEM((tm + 2 * HALO, tc), f32), pltpu.VMEM((tm + 2 * HALO, tc), f32),
                        pltpu.VMEM((ext, tc), f32)],
        compiler_params=_cparams(dimension_semantics=("parallel", "arbitrary")),
    )(x, x, x, dy, dy, dy, w, b)


@jax.custom_vjp
def conv_silu(x, w, b):
    return _conv_fwd(x, w, b)


def _conv_silu_fwd(x, w, b):
    return _conv_fwd(x, w, b), (x, w, b)


def _conv_silu_bwd(res, dy):
    return _conv_bwd(*res, dy)


conv_silu.defvjp(_conv_silu_fwd, _conv_silu_bwd)


ATT_SCALE = HEAD_DIM ** -0.5
Q_SCALE = ATT_SCALE * math.log2(math.e)
LN2 = math.log(2.0)
REP = N_Q_HEADS // N_KV_HEADS


def _attn_fwd(q, k, v):
    hq, s, dh = q.shape
    tq = _pick(s, (256, 128))

    def body(q_ref, k_ref, v_ref, o_ref, lse_ref):
        sc = lax.dot_general(q_ref[0], k_ref[0], _DIMS["nt"], preferred_element_type=f32)
        m = jnp.max(sc, axis=-1, keepdims=True)
        p = jnp.exp2(sc - m)
        l = jnp.sum(p, axis=-1, keepdims=True)
        o = jnp.dot(p.astype(bf16), v_ref[0], preferred_element_type=f32)
        o_ref[0] = o / l
        lse_ref[0] = m + jnp.log2(l)

    return pl.pallas_call(
        body, name="attn_fwd", grid=(hq, s // tq),
        in_specs=[pl.BlockSpec((1, tq, dh), lambda h, i: (h, i, 0)),
                  pl.BlockSpec((1, s, dh), lambda h, i: (h // REP, 0, 0)),
                  pl.BlockSpec((1, s, dh), lambda h, i: (h // REP, 0, 0))],
        out_specs=[pl.BlockSpec((1, tq, dh), lambda h, i: (h, i, 0)),
                   pl.BlockSpec((1, tq, 1), lambda h, i: (h, i, 0))],
        out_shape=[jax.ShapeDtypeStruct((hq, s, dh), f32), jax.ShapeDtypeStruct((hq, s, 1), f32)],
        compiler_params=_cparams(dimension_semantics=("parallel", "arbitrary")),
    )(q, k, v)


def _attn_bwd(q, k, v, kt, do, lse_row, d_row):
    hq, s, dh = q.shape
    tk = _pick(s, (256, 128))
    cq = _pick(s, (2048, 1024, 512, 256, 128))
    n_c = s // cq

    def body(q_ref, do_ref, lse_ref, d_ref, k_ref, v_ref, kt_ref, dqt_ref, dk_ref, dv_ref):
        j = pl.program_id(1)

        @pl.when(j == 0)
        def _():
            dqt_ref[...] = jnp.zeros_like(dqt_ref)

        kk, vv, ktt = k_ref[0], v_ref[0], kt_ref[0]
        dk = jnp.zeros((tk, dh), f32)
        dv = jnp.zeros((tk, dh), f32)
        for c in range(n_c):
            sl = slice(c * cq, (c + 1) * cq)
            qc, doc = q_ref[0, sl, :], do_ref[0, sl, :]
            st = lax.dot_general(kk, qc, _DIMS["nt"], preferred_element_type=f32)
            pt = jnp.exp2(st - lse_ref[0, :, sl])
            dv = dv + jnp.dot(pt.astype(bf16), doc, preferred_element_type=f32)
            dpt = lax.dot_general(vv, doc, _DIMS["nt"], preferred_element_type=f32)
            dst = (pt * (dpt - d_ref[0, :, sl])).astype(bf16)
            dk = dk + jnp.dot(dst, qc, preferred_element_type=f32)
            dqt_ref[0, :, sl] += jnp.dot(ktt, dst, preferred_element_type=f32)
        dk_ref[0] = dk * LN2
        dv_ref[0] = dv

        @pl.when(j == s // tk - 1)
        def _():
            dqt_ref[...] = dqt_ref[...] * LN2

    return pl.pallas_call(
        body, name="attn_bwd", grid=(hq, s // tk),
        in_specs=[pl.BlockSpec((1, s, dh), lambda h, j: (h, 0, 0)),
                  pl.BlockSpec((1, s, dh), lambda h, j: (h, 0, 0)),
                  pl.BlockSpec((1, 1, s), lambda h, j: (h, 0, 0)),
                  pl.BlockSpec((1, 1, s), lambda h, j: (h, 0, 0)),
                  pl.BlockSpec((1, tk, dh), lambda h, j: (h // REP, j, 0)),
                  pl.BlockSpec((1, tk, dh), lambda h, j: (h // REP, j, 0)),
                  pl.BlockSpec((1, dh, tk), lambda h, j: (h // REP, 0, j))],
        out_specs=[pl.BlockSpec((1, dh, s), lambda h, j: (h, 0, 0)),
                   pl.BlockSpec((1, tk, dh), lambda h, j: (h, j, 0)),
                   pl.BlockSpec((1, tk, dh), lambda h, j: (h, j, 0))],
        out_shape=[jax.ShapeDtypeStruct((hq, dh, s), f32), jax.ShapeDtypeStruct((hq, s, dh), f32),
                   jax.ShapeDtypeStruct((hq, s, dh), f32)],
        compiler_params=_cparams(dimension_semantics=("parallel", "arbitrary")),
    )(q, do, lse_row, d_row, k, v, kt)


@jax.custom_vjp
def attention(q, k, v):
    return _attn_fwd(q, k, v)[0]


def _attention_fwd(q, k, v):
    o, lse = _attn_fwd(q, k, v)
    return o, (q, k, v, o, lse)


def _attention_bwd(res, do):
    q, k, v, o, lse = res
    hq, s, dh = q.shape
    d_row = jnp.sum(do * o, axis=-1).reshape(hq, 1, s)
    dqt, dkp, dvp = _attn_bwd(q, k, v, jnp.swapaxes(k, 1, 2), do.astype(bf16), lse.reshape(hq, 1, s), d_row)
    dq = jnp.swapaxes(dqt, 1, 2).astype(q.dtype)
    dk = dkp.reshape(N_KV_HEADS, REP, s, dh).sum(axis=1).astype(k.dtype)
    dv = dvp.reshape(N_KV_HEADS, REP, s, dh).sum(axis=1).astype(v.dtype)
    return dq, dk, dv


attention.defvjp(_attention_fwd, _attention_bwd)


HPG = N_SSD_HEADS // N_SSD_GROUPS
GW = HPG * SSD_HEAD_DIM
NEG = -1e30
SPLIT_ROWS = 32


def _ssd_consts():
    k = np.arange(SPLIT_ROWS)[:, None]
    live = k < 3 * HPG
    sel_chunk = ((k % HPG) == (np.arange(HPG * CHUNK)[None, :] // CHUNK)) & live
    sel_head = ((k % HPG) == (np.arange(GW)[None, :] // SSD_HEAD_DIM)) & live
    return jnp.asarray(sel_chunk, bf16), jnp.asarray(sel_head, bf16)


def _split3(x):
    hi = x.astype(bf16).astype(f32)
    r1 = x - hi
    mid = r1.astype(bf16).astype(f32)
    lo = (r1 - mid).astype(bf16).astype(f32)
    return jnp.concatenate([hi, mid, lo, jnp.zeros_like(hi)], axis=0).astype(bf16)


def _tn(a, b):
    return lax.dot_general(a, b, _DIMS["tn"], preferred_element_type=f32)


def _nt(a, b):
    return lax.dot_general(a, b, _DIMS["nt"], preferred_element_type=f32)


def _nn(a, b):
    return jnp.dot(a, b, preferred_element_type=f32)


def _head_sum(sel8, x):
    hi = x.astype(bf16)
    lo = (x - hi.astype(f32)).astype(bf16)
    return _nt(sel8, hi) + _nt(sel8, lo)


def _ssd_masks(reverse):
    r = lax.broadcasted_iota(jnp.int32, (CHUNK, CHUNK), 0)
    c = lax.broadcasted_iota(jnp.int32, (CHUNK, CHUNK), 1)
    lower, upper = r >= c, r <= c
    return (upper, lower) if reverse else (lower, upper)


def _ssd_in_specs(cidx):
    return [pl.BlockSpec((CHUNK, D_INNER), lambda c: (cidx(c), 0)),
            pl.BlockSpec((CHUNK, GN), lambda c: (cidx(c), D_INNER // GN)),
            pl.BlockSpec((CHUNK, GN), lambda c: (cidx(c), D_INNER // GN + 1)),
            pl.BlockSpec((N_SSD_HEADS, CHUNK), lambda c: (0, cidx(c))),
            pl.BlockSpec((N_SSD_HEADS, 1), lambda c: (0, 0)),
            pl.BlockSpec((SPLIT_ROWS, HPG * CHUNK), lambda c: (0, 0)),
            pl.BlockSpec((SPLIT_ROWS, GW), lambda c: (0, 0))]


def _ssd_chunk_common(dtt_ref, a_ref, et_ref, mask_t):
    dtt = dtt_ref[...]
    et = jnp.dot(dtt * a_ref[...], mask_t.astype(f32), precision=HIGHEST, preferred_element_type=f32)
    et_ref[...] = et
    return dtt, et


def _ssd_group_common(g, dtt, et, selc_ref, selh_ref, xs_ref, b_ref, c_ref, last):
    gr = slice(g * HPG, (g + 1) * HPG)
    e3 = _split3(et[gr])
    col = _tn(e3, selc_ref[...])
    eb = _tn(e3, selh_ref[...])
    dtb = _tn(_split3(dtt[gr]), selh_ref[...])
    tbc = eb[last:last + 1, :]
    xs = xs_ref[:, g * GW:(g + 1) * GW]
    bg = b_ref[:, g * D_STATE:(g + 1) * D_STATE].astype(bf16)
    cg = c_ref[:, g * D_STATE:(g + 1) * D_STATE].astype(bf16)
    return col, eb, dtb, tbc, xs, bg, cg


def _ssd_fwd(xbc, dtt, a_col, reverse):
    s = xbc.shape[0]
    nc = s // CHUNK
    cidx = (lambda c: nc - 1 - c) if reverse else (lambda c: c)
    last = 0 if reverse else CHUNK - 1
    selc, selh = _ssd_consts()

    def body(xs_ref, b_ref, c_ref, dtt_ref, a_ref, selc_ref, selh_ref, y_ref, st_ref, ht_ref, et_ref):
        @pl.when(pl.program_id(0) == 0)
        def _():
            ht_ref[...] = jnp.zeros_like(ht_ref)

        mask, mask_t = _ssd_masks(reverse)
        dtt_v, et = _ssd_chunk_common(dtt_ref, a_ref, et_ref, mask_t)
        for g in range(N_SSD_GROUPS):
            col, eb, dtb, tbc, xs, bg, cg = _ssd_group_common(g, dtt_v, et, selc_ref, selh_ref, xs_ref, b_ref, c_ref,
                                                              last)
            xd = xs * dtb
            cb = _nt(cg, bg)
            ht = ht_ref[g]
            st_ref[0, g] = ht
            yoff = _nn(cg, ht.astype(bf16)) * jnp.exp(eb)
            for j in range(HPG):
                h = g * HPG + j
                hs = slice(j * SSD_HEAD_DIM, (j + 1) * SSD_HEAD_DIM)
                lam = jnp.exp(jnp.where(mask, col[:, j * CHUNK:(j + 1) * CHUNK] - et_ref[h:h + 1, :], NEG))
                yj = _nn((cb * lam).astype(bf16), xd[:, hs].astype(bf16))
                y_ref[:, g * GW + j * SSD_HEAD_DIM:g * GW + (j + 1) * SSD_HEAD_DIM] = yj + yoff[:, hs]
            ht_ref[g] = jnp.exp(tbc) * ht + _tn(bg, (xd * jnp.exp(tbc - eb)).astype(bf16))

    return pl.pallas_call(
        body, name="ssd_fwd_rev" if reverse else "ssd_fwd", grid=(nc,),
        in_specs=_ssd_in_specs(cidx),
        out_specs=[pl.BlockSpec((CHUNK, D_INNER), lambda c: (cidx(c), 0)),
                   pl.BlockSpec((1, N_SSD_GROUPS, D_STATE, GW), lambda c: (cidx(c), 0, 0, 0))],
        out_shape=[jax.ShapeDtypeStruct((s, D_INNER), f32),
                   jax.ShapeDtypeStruct((nc, N_SSD_GROUPS, D_STATE, GW), f32)],
        scratch_shapes=[pltpu.VMEM((N_SSD_GROUPS, D_STATE, GW), f32), pltpu.VMEM((N_SSD_HEADS, CHUNK), f32)],
        compiler_params=_cparams(dimension_semantics=("arbitrary",)),
    )(xbc, xbc, xbc, dtt, a_col, selc, selh)


def _ssd_bwd(xbc, dtt, a_col, states, dy, reverse):
    s = xbc.shape[0]
    nc = s // CHUNK
    cidx = (lambda c: c) if reverse else (lambda c: nc - 1 - c)
    last = 0 if reverse else CHUNK - 1
    selc, selh = _ssd_consts()

    def body(xs_ref, b_ref, c_ref, dtt_ref, a_ref, selc_ref, selh_ref, st_ref, dy_ref,
             dxbc_ref, ddtt_ref, da_ref, dh_ref, et_ref, det_ref, det2_ref, ddt_ref, q_ref):
        @pl.when(pl.program_id(0) == 0)
        def _():
            dh_ref[...] = jnp.zeros_like(dh_ref)
            da_ref[...] = jnp.zeros_like(da_ref)

        mask, mask_t = _ssd_masks(reverse)
        dtt_v, et = _ssd_chunk_common(dtt_ref, a_ref, et_ref, mask_t)
        sel8 = selh_ref[0:HPG, :]
        is_last = lax.broadcasted_iota(jnp.int32, (CHUNK, GW), 0) == last
        for g in range(N_SSD_GROUPS):
            col, eb, dtb, tbc, xs, bg, cg = _ssd_group_common(g, dtt_v, et, selc_ref, selh_ref, xs_ref, b_ref, c_ref,
                                                              last)
            xd = xs * dtb
            cb = _nt(cg, bg)
            cbt = _nt(bg, cg)
            exp_t = jnp.exp(tbc)
            dfac = jnp.exp(tbc - eb)
            ht = st_ref[0, g]
            dhn = dh_ref[g]
            ht16, dhn16 = ht.astype(bf16), dhn.astype(bf16)
            dy = dy_ref[:, g * GW:(g + 1) * GW]
            dye = dy * jnp.exp(eb)
            dye16 = dye.astype(bf16)
            dc = _nt(dye16, ht16)
            dh_ref[g] = exp_t * dhn + _tn(cg, dye16)
            deb = dye * _nn(cg, ht16)
            xdd = xd * dfac
            dxdd = _nn(bg, dhn16)
            db = _nt(xdd.astype(bf16), dhn16)
            dxd_state = dxdd * dfac
            ddf = dxdd * xdd
            dtbc = jnp.sum(ddf, axis=0, keepdims=True) + exp_t * jnp.sum(dhn * ht, axis=0, keepdims=True)
            deb = deb - ddf + jnp.where(is_last, dtbc, 0.0)
            dcb = jnp.zeros((CHUNK, CHUNK), f32)
            dcbt = jnp.zeros((CHUNK, CHUNK), f32)
            for j in range(HPG):
                h = g * HPG + j
                hs = slice(j * SSD_HEAD_DIM, (j + 1) * SSD_HEAD_DIM)
                colj = col[:, j * CHUNK:(j + 1) * CHUNK]
                row = et_ref[h:h + 1, :]
                lam = jnp.exp(jnp.where(mask, colj - row, NEG))
                lam_t = jnp.exp(jnp.where(mask_t, row - colj, NEG))
                xdj, dyj = xd[:, hs].astype(bf16), dy[:, hs].astype(bf16)
                t1 = _nt(dyj, xdj) * lam
                t2 = _nt(xdj, dyj) * lam_t
                dcb, dcbt = dcb + t1, dcbt + t2
                det_ref[h:h + 1, :] = -jnp.sum(t1 * cb - t2 * cbt, axis=0, keepdims=True)
                dxdj = _nn((cbt * lam_t).astype(bf16), dyj) + dxd_state[:, hs]
                dxbc_ref[:, g * GW + j * SSD_HEAD_DIM:g * GW + (j + 1) * SSD_HEAD_DIM] = dxdj * dtb[:, hs]
                q_ref[:, hs] = dxdj * xs[:, hs]
            dxbc_ref[:, D_INNER + g * D_STATE:D_INNER + (g + 1) * D_STATE] = db + _nn(dcbt.astype(bf16), cg)
            dxbc_ref[:, D_INNER + GN + g * D_STATE:D_INNER + GN + (g + 1) * D_STATE] = dc + _nn(dcb.astype(bf16), bg)
            det2_ref[g * HPG:(g + 1) * HPG, :] = _head_sum(sel8, deb)
            ddt_ref[g * HPG:(g + 1) * HPG, :] = _head_sum(sel8, q_ref[...])
        dat = jnp.dot(det_ref[...] + det2_ref[...], mask.astype(f32), precision=HIGHEST, preferred_element_type=f32)
        ddtt_ref[...] = ddt_ref[...] + dat * a_ref[...]
        da_ref[...] += jnp.sum(dat * dtt_v, axis=1, keepdims=True)

    in_specs = _ssd_in_specs(cidx) + [
        pl.BlockSpec((1, N_SSD_GROUPS, D_STATE, GW), lambda c: (cidx(c), 0, 0, 0)),
        pl.BlockSpec((CHUNK, D_INNER), lambda c: (cidx(c), 0))]
    hl = pltpu.VMEM((N_SSD_HEADS, CHUNK), f32)
    return pl.pallas_call(
        body, name="ssd_bwd_rev" if reverse else "ssd_bwd", grid=(nc,),
        in_specs=in_specs,
        out_specs=[pl.BlockSpec((CHUNK, CONV_DIM), lambda c: (cidx(c), 0)),
                   pl.BlockSpec((N_SSD_HEADS, CHUNK), lambda c: (0, cidx(c))),
                   pl.BlockSpec((N_SSD_HEADS, 1), lambda c: (0, 0))],
        out_shape=[jax.ShapeDtypeStruct((s, CONV_DIM), f32), jax.ShapeDtypeStruct((N_SSD_HEADS, s), f32),
                   jax.ShapeDtypeStruct((N_SSD_HEADS, 1), f32)],
        scratch_shapes=[pltpu.VMEM((N_SSD_GROUPS, D_STATE, GW), f32), hl, hl, hl, hl, pltpu.VMEM((CHUNK, GW), f32)],
        compiler_params=_cparams(dimension_semantics=("arbitrary",)),
    )(xbc, xbc, xbc, dtt, a_col, selc, selh, states, dy)


def make_ssd(reverse):
    @jax.custom_vjp
    def ssd(xbc, dtt, a_col):
        return _ssd_fwd(xbc, dtt, a_col, reverse)[0]

    def fwd(xbc, dtt, a_col):
        y, st = _ssd_fwd(xbc, dtt, a_col, reverse)
        return y, (xbc, dtt, a_col, st)

    def bwd(res, dy):
        xbc, dtt, a_col, st = res
        return tuple(_ssd_bwd(xbc, dtt, a_col, st, dy, reverse))

    ssd.defvjp(fwd, bwd)
    return ssd


W_NAMES = ("q", "k", "v", "xbc", "z", "dt", "gates", "attn_out", "ssd_out", "o", "mlp1", "mlp2")


def _rope_tables(s):
    rows = s // GRID_W
    pos_row = jnp.repeat(jnp.arange(rows, dtype=jnp.int32), GRID_W).astype(f32)
    pos_col = jnp.tile(jnp.arange(GRID_W, dtype=jnp.int32), rows).astype(f32)
    axis_dim = HEAD_DIM // 2
    inv_freq = ROPE_THETA ** (-jnp.arange(0, axis_dim, 2, dtype=f32) / axis_dim)
    ang_r = pos_row[:, None] * inv_freq[None, :]
    ang_c = pos_col[:, None] * inv_freq[None, :]
    cos = jnp.concatenate([jnp.cos(ang_r), jnp.cos(ang_r), jnp.cos(ang_c), jnp.cos(ang_c)], axis=-1)
    sin = jnp.concatenate([jnp.sin(ang_r), jnp.sin(ang_r), jnp.sin(ang_c), jnp.sin(ang_c)], axis=-1)
    return cos, sin


def _rope_perm():
    p = np.zeros((HEAD_DIM, HEAD_DIM), np.float32)
    for j in range(HEAD_DIM):
        if (j % 32) < 16:
            p[j + 16, j] = -1.0
        else:
            p[j - 16, j] = 1.0
    return jnp.asarray(p)


def local_loss(x, mod, small, wgrads, wfull, target):
    s = x.shape[0]
    lin = {n: make_linear("lin_" + n) for n in W_NAMES}
    shift1, scale1, gate1, shift2, scale2, gate2 = [mod[i] for i in range(6)]

    norm_mod = make_rowwise("norm_mod", _fn_norm_mod, [(D_MODEL, f32)])
    (h,), _ = norm_mod((x,), (small["norm1_w"], scale1, shift1), (), ())

    proj = {n: lin[n](h, wfull[n], wgrads[n]) for n in ("q", "k", "v", "xbc", "z", "dt", "gates")}

    cos, sin = _rope_tables(s)
    pm = _rope_perm()

    def heads(t, nh):
        return t.reshape(s, nh, HEAD_DIM).transpose(1, 0, 2)

    qk_q = make_rowwise("q_norm_rope", functools.partial(_fn_qk_norm_rope, out_scale=Q_SCALE), [(HEAD_DIM, bf16)],
                        tm_pref=1024)
    qk_k = make_rowwise("k_norm_rope", _fn_qk_norm_rope, [(HEAD_DIM, bf16)], tm_pref=1024)
    (qr,), _ = qk_q((heads(proj["q"], N_Q_HEADS).reshape(N_Q_HEADS * s, HEAD_DIM),), (small["q_norm_w"],), (pm,),
                    (cos, sin))
    (kr,), _ = qk_k((heads(proj["k"], N_KV_HEADS).reshape(N_KV_HEADS * s, HEAD_DIM),), (small["k_norm_w"],), (pm,),
                    (cos, sin))
    vh = heads(proj["v"], N_KV_HEADS).astype(bf16)
    att = attention(qr.reshape(N_Q_HEADS, s, HEAD_DIM), kr.reshape(N_KV_HEADS, s, HEAD_DIM), vh)
    att = att.transpose(1, 0, 2).reshape(s, N_Q_HEADS * HEAD_DIM)
    ao = lin["attn_out"](att, wfull["attn_out"], wgrads["attn_out"])

    xbc = conv_silu(proj["xbc"], small["conv_w"], small["conv_b"])
    softplus = make_rowwise("dt_softplus", _fn_softplus, [(2 * N_SSD_HEADS, f32)])
    (dt,), _ = softplus((proj["dt"][:, :2 * N_SSD_HEADS],), (small["dt_bias"].reshape(1, 2 * N_SSD_HEADS),), (), ())
    a_neg = -jnp.exp(small["A_log"])
    dtt = dt.T
    y_f = make_ssd(False)(xbc, dtt[:N_SSD_HEADS], a_neg[0].reshape(N_SSD_HEADS, 1))
    y_b = make_ssd(True)(xbc, dtt[N_SSD_HEADS:], a_neg[1].reshape(N_SSD_HEADS, 1))
    dexp = jnp.repeat(small["ssd_D"].reshape(N_SSD_HEADS), SSD_HEAD_DIM).reshape(1, D_INNER)
    ssd_gate = make_rowwise("ssd_gate", _fn_ssd_gate, [(D_INNER, f32)], tm_pref=128)
    (ssd_out,), _ = ssd_gate((y_f, y_b, xbc[:, :D_INNER], proj["z"]), (dexp, small["ssd_norm_w"]), (), ())
    so = lin["ssd_out"](ssd_out, wfull["ssd_out"], wgrads["ssd_out"])

    merge = make_rowwise("merge", _fn_merge, [(D_MODEL, f32)])
    (merged,), _ = merge((ao, so, proj["gates"][:, :D_MODEL], proj["gates"][:, D_MODEL:]), (), (), ())
    mo = lin["o"](merged, wfull["o"], wgrads["o"])

    res_norm = make_rowwise("res_norm", _fn_res_norm, [(D_MODEL, f32), (D_MODEL, f32)])
    (x1, h2), _ = res_norm((x, mo), (gate1, small["norm2_w"], scale2, shift2), (), ())
    u = lin["mlp1"](h2, wfull["mlp1"], wgrads["mlp1"])
    relu2 = make_rowwise("relu2", _fn_relu2, [(D_FF, f32)], tm_pref=128)
    (r,), _ = relu2((u,), (), (), ())
    ff = lin["mlp2"](r, wfull["mlp2"], wgrads["mlp2"])
    loss_op = make_rowwise("loss", _fn_loss, [], [(1, 1)])
    _, (loss,) = loss_op((x1, ff), (gate2,), (), (target,))
    return loss[0, 0]


_BC1 = 1.0 - ADAM_B1 ** ADAM_STEP
_BC2 = 1.0 - ADAM_B2 ** ADAM_STEP


def _adamw(w, g, m, v):
    m = ADAM_B1 * m + (1.0 - ADAM_B1) * g
    v = ADAM_B2 * v + (1.0 - ADAM_B2) * (g * g)
    delta = -ADAM_LR * ((m / _BC1) / (jnp.sqrt(v / _BC2) + ADAM_EPS) + ADAM_WD * w)
    return delta, m, v


def _ada_fwd(c_all, w, b):
    n = w.shape[1]

    def body(c_ref, w_ref, b_ref, o_ref):
        o_ref[...] = jnp.dot(_silu(c_ref[...]), w_ref[...], precision=HIGHEST, preferred_element_type=f32) + b_ref[...]

    return pl.pallas_call(body, name="ada_fwd", out_shape=jax.ShapeDtypeStruct((N_DEV, n), f32),
                          compiler_params=_cparams())(c_all, w, b)


def _ada_bwd_adamw(c_all, dmod, w, m, v):
    d, n = w.shape
    tr = _pick(d, (256, 128))

    def body(c_ref, dm_ref, w_ref, m_ref, v_ref, g_ref, dl_ref, mo_ref, vo_ref):
        g = lax.dot_general(_silu(c_ref[...]), dm_ref[...], _DIMS["tn"], precision=HIGHEST,
                            preferred_element_type=f32)
        g_ref[...] = g
        dl_ref[...], mo_ref[...], vo_ref[...] = _adamw(w_ref[...], g, m_ref[...], v_ref[...])

    blk = pl.BlockSpec((tr, n), lambda i: (i, 0))
    return pl.pallas_call(
        body, name="ada_bwd_adamw", grid=(d // tr,),
        in_specs=[pl.BlockSpec((N_DEV, tr), lambda i: (0, i)), pl.BlockSpec((N_DEV, n), lambda i: (0, 0)), blk, blk, blk],
        out_specs=[blk] * 4, out_shape=[jax.ShapeDtypeStruct((d, n), f32)] * 4,
        compiler_params=_cparams(dimension_semantics=("parallel",)),
    )(c_all, dmod, w, m, v)


def _sum_over_mesh(g):
    def body(g_ref, o_ref):
        acc = g_ref[0]
        for d in range(1, N_DEV):
            acc = acc + g_ref[d]
        o_ref[...] = acc

    return pl.pallas_call(body, name="sum_small", out_shape=jax.ShapeDtypeStruct(g.shape[1:], f32),
                          compiler_params=_cparams())(g)


def _adamw_small(w, g, m, v):
    def body(w_ref, g_ref, m_ref, v_ref, dl_ref, mo_ref, vo_ref):
        dl_ref[...], mo_ref[...], vo_ref[...] = _adamw(w_ref[...], g_ref[...], m_ref[...], v_ref[...])

    return pl.pallas_call(body, name="adamw_small", out_shape=[jax.ShapeDtypeStruct(w.shape, f32)] * 3,
                          compiler_params=_cparams())(w, g, m, v)


def _sum_adamw(recv, w, m, v):
    _, r, c = recv.shape
    tr = _pick(r, (264, 256, 128, 64, 8))

    def body(g_ref, w_ref, m_ref, v_ref, go_ref, dl_ref, mo_ref, vo_ref):
        g = g_ref[0]
        for d in range(1, N_DEV):
            g = g + g_ref[d]
        go_ref[...] = g
        dl_ref[...], mo_ref[...], vo_ref[...] = _adamw(w_ref[...], g, m_ref[...], v_ref[...])

    blk = pl.BlockSpec((tr, c), lambda i: (i, 0))
    return pl.pallas_call(
        body, name="sum_adamw", grid=(r // tr,),
        in_specs=[pl.BlockSpec((N_DEV, tr, c), lambda i: (0, i, 0)), blk, blk, blk],
        out_specs=[blk] * 4, out_shape=[jax.ShapeDtypeStruct((r, c), f32)] * 4,
        compiler_params=_cparams(dimension_semantics=("parallel",)),
    )(recv, w, m, v)


def _pack_small(arrs):
    parts = []
    for a in arrs:
        flat = a.reshape(-1).astype(f32)
        parts.append(jnp.pad(flat, (0, (-flat.shape[0]) % LANE)))
    flat = jnp.concatenate(parts)
    flat = jnp.pad(flat, (0, (-flat.shape[0]) % (8 * LANE)))
    return flat.reshape(-1, LANE)


def _unpack_small(packed, shapes):
    flat = packed.reshape(-1)
    out, off = [], 0
    for shp in shapes:
        n = int(np.prod(shp))
        out.append(flat[off:off + n].reshape(shp))
        off += n + (-n) % LANE
    return out


BIG = ("w_in", "w_attn_out", "w_ssd_out", "w_o", "w_mlp1", "w_mlp2")
BIG_ROWS = (D_MODEL * (D_IN_PROJ // N_DEV) // PACK_COLS, N_Q_HEADS * HEAD_DIM // N_DEV, D_INNER // N_DEV,
            D_MODEL // N_DEV, D_MODEL * (D_FF // N_DEV) // PACK_COLS, D_FF // N_DEV)
BIG_PAD_ROWS = (-sum(BIG_ROWS)) % 16


def _pack_big(shards, dtype):
    parts = [s.astype(dtype).reshape(-1, PACK_COLS) for s in shards]
    parts.append(jnp.zeros((BIG_PAD_ROWS, PACK_COLS), dtype))
    return jnp.concatenate(parts, axis=0)


def _unpack_big(packed, shapes):
    out, off = [], 0
    for rows, shp in zip(BIG_ROWS, shapes):
        out.append(packed[off:off + rows].reshape(shp))
        off += rows
    return out


def _split_gathered(g):
    offs = np.cumsum((0,) + BIG_ROWS)
    sl = [g[:, offs[i]:offs[i + 1]] for i in range(len(BIG))]
    n_in = D_IN_PROJ // N_DEV
    w_in = sl[0].reshape(N_DEV, D_MODEL, n_in).transpose(1, 0, 2).reshape(D_MODEL, D_IN_PROJ)
    w = {}
    off = 0
    for name, size in zip(("q", "k", "v", "xbc", "z", "dt", "gates"), PROJ_SIZES):
        w[name] = w_in[:, off:off + size]
        off += size
    w["dt"] = jnp.pad(w["dt"], ((0, 0), (0, DT_PAD - 2 * N_SSD_HEADS)))
    w["attn_out"] = sl[1].reshape(N_Q_HEADS * HEAD_DIM, D_MODEL)
    w["ssd_out"] = sl[2].reshape(D_INNER, D_MODEL)
    w["o"] = sl[3].reshape(D_MODEL, D_MODEL)
    w["mlp1"] = sl[4].reshape(N_DEV, D_MODEL, D_FF // N_DEV).transpose(1, 0, 2).reshape(D_MODEL, D_FF)
    w["mlp2"] = sl[5].reshape(D_FF, D_MODEL)
    return w


def _pack_full_grads(gw):
    n_in = D_IN_PROJ // N_DEV
    g_in = jnp.concatenate([gw["q"], gw["k"], gw["v"], gw["xbc"], gw["z"], gw["dt"][:, :2 * N_SSD_HEADS],
                            gw["gates"]], axis=1)
    parts = [
        g_in.reshape(D_MODEL, N_DEV, n_in).transpose(1, 0, 2).reshape(N_DEV, -1, PACK_COLS),
        gw["attn_out"].reshape(N_DEV, -1, PACK_COLS),
        gw["ssd_out"].reshape(N_DEV, -1, PACK_COLS),
        gw["o"].reshape(N_DEV, -1, PACK_COLS),
        gw["mlp1"].reshape(D_MODEL, N_DEV, D_FF // N_DEV).transpose(1, 0, 2).reshape(N_DEV, -1, PACK_COLS),
        gw["mlp2"].reshape(N_DEV, -1, PACK_COLS),
        jnp.zeros((N_DEV, BIG_PAD_ROWS, PACK_COLS), f32),
    ]
    return jnp.concatenate(parts, axis=1)


SMALL = ("norm1_w", "norm2_w", "q_norm_w", "k_norm_w", "conv_w", "conv_b", "A_log", "dt_bias", "ssd_D", "ssd_norm_w")


def kernel(x, c, w_ada, b_ada, norm1_w, norm2_w, w_in, q_norm_w, k_norm_w, conv_w, conv_b, A_log, dt_bias, ssd_D, ssd_norm_w, w_attn_out, w_ssd_out, w_o, w_mlp1, w_mlp2, loss_target, m_w_ada, m_b_ada, m_norm1_w, m_norm2_w, m_w_in, m_q_norm_w, m_k_norm_w, m_conv_w, m_conv_b, m_A_log, m_dt_bias, m_ssd_D, m_ssd_norm_w, m_w_attn_out, m_w_ssd_out, m_w_o, m_w_mlp1, m_w_mlp2, v_w_ada, v_b_ada, v_norm1_w, v_norm2_w, v_w_in, v_q_norm_w, v_k_norm_w, v_conv_w, v_conv_b, v_A_log, v_dt_bias, v_ssd_D, v_ssd_norm_w, v_w_attn_out, v_w_ssd_out, v_w_o, v_w_mlp1, v_w_mlp2):
    args = dict(locals())
    me = _my_index()
    n_ada = 6 * D_MODEL // N_DEV
    n_cw = CONV_DIM // N_DEV

    blk = jnp.zeros((8, D_MODEL), f32)
    blk = blk.at[0:1, :].set(c)
    blk = blk.at[1:1 + D_CONV, :n_cw].set(conv_w[0])
    g0 = _all_gather(blk, "gather_c_convw", in_vmem=True)
    c_all = g0[:, 0, :]
    conv_w_full = g0[:, 1:1 + D_CONV, :n_cw].transpose(1, 0, 2).reshape(D_CONV, CONV_DIM)

    b_shard = lax.dynamic_slice(b_ada, (0, me * n_ada), (1, n_ada))
    mod_cols = _ada_fwd(c_all, w_ada[0], b_shard)
    g1 = _all_gather(mod_cols, "gather_mod", in_vmem=True)
    mod_mine = lax.dynamic_index_in_dim(g1, me, axis=1, keepdims=False)
    mod = mod_mine.reshape(6, 1, D_MODEL)

    big_shapes = [args[n].shape[1:] for n in BIG]
    packed16 = _pack_big([args[n][0] for n in BIG], bf16)
    wfull = _split_gathered(_all_gather(packed16, "gather_weights", in_vmem=False))
    wgrads = {n: jnp.zeros(wfull[n].shape, f32) for n in W_NAMES}

    small = {"norm1_w": norm1_w, "norm2_w": norm2_w, "q_norm_w": q_norm_w, "k_norm_w": k_norm_w,
             "conv_w": conv_w_full, "conv_b": conv_b, "A_log": A_log[0], "dt_bias": dt_bias[0], "ssd_D": ssd_D,
             "ssd_norm_w": ssd_norm_w}

    loss, (gx, gmod, gsmall, gw) = jax.value_and_grad(local_loss, argnums=(0, 1, 2, 3))(
        x[0], mod, small, wgrads, wfull, loss_target[0])

    small_list = [gmod, gsmall["norm1_w"], gsmall["norm2_w"], gsmall["q_norm_w"], gsmall["k_norm_w"], gsmall["conv_w"],
                  gsmall["conv_b"], gsmall["A_log"], gsmall["dt_bias"], gsmall["ssd_D"], gsmall["ssd_norm_w"],
                  loss.reshape(1)]
    small_shapes = [a.shape for a in small_list]
    g2 = _all_gather(_pack_small(small_list), "gather_small_grads", in_vmem=True)
    summed = _unpack_small(_sum_over_mesh(g2), small_shapes)
    loss_total = summed[-1][0]
    g_b_ada = summed[0].reshape(1, 6 * D_MODEL)
    g_small = dict(zip(SMALL, summed[1:-1]))
    g_conv_w = lax.dynamic_slice(g_small["conv_w"], (0, me * n_cw), (D_CONV, n_cw))

    dmod_all = g2[:, :6 * D_MODEL // LANE, :].reshape(N_DEV, 6 * D_MODEL)
    dmod_shard = lax.dynamic_slice(dmod_all, (0, me * n_ada), (N_DEV, n_ada))
    ada = _ada_bwd_adamw(c_all, dmod_shard, w_ada[0], m_w_ada[0], v_w_ada[0])

    small_grads = {"b_ada": g_b_ada, "norm1_w": g_small["norm1_w"], "norm2_w": g_small["norm2_w"],
                   "q_norm_w": g_small["q_norm_w"], "k_norm_w": g_small["k_norm_w"], "conv_w": g_conv_w[None],
                   "conv_b": g_small["conv_b"], "A_log": g_small["A_log"][None], "dt_bias": g_small["dt_bias"][None],
                   "ssd_D": g_small["ssd_D"], "ssd_norm_w": g_small["ssd_norm_w"]}
    sm_names = list(small_grads)
    sm_shapes = [args[n].shape for n in sm_names]
    sm = _adamw_small(_pack_small([args[n] for n in sm_names]), _pack_small([small_grads[n] for n in sm_names]),
                      _pack_small([args["m_" + n] for n in sm_names]), _pack_small([args["v_" + n] for n in sm_names]))
    sm_delta, sm_m, sm_v = [dict(zip(sm_names, _unpack_small(t, sm_shapes))) for t in sm]
    small_grads = {n: small_grads[n].reshape(args[n].shape) for n in sm_names}

    recv = _scatter_blocks(_pack_full_grads(gw), "scatter_grads")
    big = _sum_adamw(recv, _pack_big([args[n][0] for n in BIG], f32), _pack_big([args["m_" + n][0] for n in BIG], f32),
                     _pack_big([args["v_" + n][0] for n in BIG], f32))
    big_g, big_delta, big_m, big_v = [dict(zip(BIG, [t[None] for t in _unpack_big(p, big_shapes)])) for p in big]

    names = ("w_ada", "b_ada", "norm1_w", "norm2_w", "w_in", "q_norm_w", "k_norm_w", "conv_w", "conv_b", "A_log",
             "dt_bias", "ssd_D", "ssd_norm_w", "w_attn_out", "w_ssd_out", "w_o", "w_mlp1", "w_mlp2")
    grads, deltas, new_m, new_v = {}, {}, {}, {}
    for n in names:
        if n == "w_ada":
            grads[n], deltas[n], new_m[n], new_v[n] = [t[None] for t in ada]
        elif n in BIG:
            grads[n], deltas[n], new_m[n], new_v[n] = big_g[n], big_delta[n], big_m[n], big_v[n]
        else:
            grads[n], deltas[n], new_m[n], new_v[n] = small_grads[n], sm_delta[n], sm_m[n], sm_v[n]
    return (loss_total, gx[None], *[grads[n] for n in names], *[deltas[n] for n in names],
            *[new_m[n] for n in names], *[new_v[n] for n in names])
```

```python
import functools
import math

import jax
import jax.numpy as jnp
import numpy as np
from jax import lax
from jax.experimental import pallas as pl
from jax.experimental.pallas import tpu as pltpu

f32 = jnp.float32
bf16 = jnp.bfloat16
HIGHEST = lax.Precision.HIGHEST
MESH = pl.DeviceIdType.MESH

N_DEV = 8
D_MODEL = 1024
GRID_W = 64
N_Q_HEADS = 16
N_KV_HEADS = 4
HEAD_DIM = 64
ROPE_THETA = 10000.0
D_INNER = 2048
SSD_HEAD_DIM = 64
N_SSD_HEADS = 32
N_SSD_GROUPS = 4
D_STATE = 128
D_CONV = 5
CHUNK = 128
D_FF = 4096
EPS = 1e-6
CONV_DIM = D_INNER + 2 * N_SSD_GROUPS * D_STATE
GN = N_SSD_GROUPS * D_STATE
PROJ_SIZES = (N_Q_HEADS * HEAD_DIM, N_KV_HEADS * HEAD_DIM, N_KV_HEADS * HEAD_DIM, CONV_DIM, D_INNER,
              2 * N_SSD_HEADS, 2 * D_MODEL)
D_IN_PROJ = sum(PROJ_SIZES)
DT_PAD = 128

ADAM_LR, ADAM_B1, ADAM_B2, ADAM_EPS, ADAM_WD, ADAM_STEP = 0.001, 0.9, 0.999, 1e-08, 0.01, 10

V7X_VMEM_LIMIT = 56 * 1024 * 1024
LANE = 128
PACK_COLS = 1024


def _cparams(**kw):
    return pltpu.CompilerParams(vmem_limit_bytes=V7X_VMEM_LIMIT, **kw)


def _pick(dim, prefs):
    for p in prefs:
        if dim % p == 0:
            return p
    return dim


def _my_index():
    return 4 * lax.axis_index("x") + 2 * lax.axis_index("y") + lax.axis_index("c")


def _all_gather(block, name, in_vmem):
    r, c = block.shape

    def body(x_ref, out_ref, send_sems, recv_sems, local_sem):
        x, y, cc = lax.axis_index("x"), lax.axis_index("y"), lax.axis_index("c")
        me, sibling = (x, y, cc), (x, y, 1 - cc)
        chips = [(1 - x, y), (x, 1 - y), (1 - x, 1 - y)]

        def slot(px, py, pc):
            return out_ref.at[4 * px + 2 * py + pc]

        def copy(k, blk, to, src=None):
            return pltpu.make_async_remote_copy(
                src_ref=slot(*blk) if src is None else src, dst_ref=slot(*blk),
                send_sem=send_sems.at[k], recv_sem=recv_sems.at[k], device_id=to, device_id_type=MESH)

        mine = pltpu.make_async_copy(x_ref, slot(*me), local_sem)
        mine.start()
        first = [copy(0, me, sibling, src=x_ref)]
        first += [copy(1 + j, me, (*chip, cc), src=x_ref) for j, chip in enumerate(chips)]
        for cp in first:
            cp.start()
        passed = [copy(4 + j, (*chip, cc), sibling) for j, chip in enumerate(chips)]
        for j, chip in enumerate(chips):
            copy(1 + j, (*chip, cc), me).wait_recv()
            passed[j].start()
        copy(0, sibling, me).wait_recv()
        for j, chip in enumerate(chips):
            copy(4 + j, (*chip, 1 - cc), me).wait_recv()
        for cp in first + passed:
            cp.wait_send()
        mine.wait()

    space = pltpu.VMEM if in_vmem else pl.ANY
    return pl.pallas_call(
        body, name=name,
        out_shape=jax.ShapeDtypeStruct((N_DEV, r, c), block.dtype),
        in_specs=[pl.BlockSpec(memory_space=space)],
        out_specs=pl.BlockSpec(memory_space=space),
        scratch_shapes=[pltpu.SemaphoreType.DMA((7,)), pltpu.SemaphoreType.DMA((7,)), pltpu.SemaphoreType.DMA],
    )(block)


def _scatter_blocks(g, name):
    _, r, c = g.shape

    def body(g_ref, out_ref, send_sems, recv_sems, local_sem):
        x, y, cc = lax.axis_index("x"), lax.axis_index("y"), lax.axis_index("c")
        me = 4 * x + 2 * y + cc
        mine = pltpu.make_async_copy(g_ref.at[me], out_ref.at[me], local_sem)
        mine.start()

        def copy(k):
            fx, fy, fc = (k >> 2) & 1, (k >> 1) & 1, k & 1
            px = x + fx - 2 * x * fx
            py = y + fy - 2 * y * fy
            pc = cc + fc - 2 * cc * fc
            peer = 4 * px + 2 * py + pc
            send = pltpu.make_async_remote_copy(
                src_ref=g_ref.at[peer], dst_ref=out_ref.at[me],
                send_sem=send_sems.at[k - 1], recv_sem=recv_sems.at[k - 1],
                device_id=(px, py, pc), device_id_type=MESH)
            recv = pltpu.make_async_remote_copy(
                src_ref=g_ref.at[peer], dst_ref=out_ref.at[peer],
                send_sem=send_sems.at[k - 1], recv_sem=recv_sems.at[k - 1],
                device_id=(px, py, pc), device_id_type=MESH)
            return send, recv

        pairs = [copy(k) for k in range(1, N_DEV)]
        for send, _ in pairs:
            send.start()
        for _, recv in pairs:
            recv.wait_recv()
        for send, _ in pairs:
            send.wait_send()
        mine.wait()

    return pl.pallas_call(
        body, name=name,
        out_shape=jax.ShapeDtypeStruct(g.shape, g.dtype),
        in_specs=[pl.BlockSpec(memory_space=pl.ANY)],
        out_specs=pl.BlockSpec(memory_space=pl.ANY),
        scratch_shapes=[pltpu.SemaphoreType.DMA((7,)), pltpu.SemaphoreType.DMA((7,)), pltpu.SemaphoreType.DMA],
    )(g)


_DIMS = {"nn": (((1,), (0,)), ((), ())), "nt": (((1,), (1,)), ((), ())), "tn": (((0,), (0,)), ((), ()))}


def _matmul(a, b, mode, out_dtype, name, epilogue=None, side=None):
    if mode == "nn":
        (m, k), (_, n) = a.shape, b.shape
    elif mode == "nt":
        (m, k), (n, _) = a.shape, b.shape
    else:
        (k, m), (_, n) = a.shape, b.shape
    tm = _pick(m, (1024, 512, 256, 128))
    tn = _pick(n, (512, 384, 256, 128))
    tk = _pick(k, (1024, 512, 256, 128))
    nk = k // tk
    dims = _DIMS[mode]
    n_in = 3 if epilogue == "drelu2" else 2
    n_out = 2 if epilogue == "relu2" else 1

    def body(*refs):
        a_ref, b_ref = refs[:2]
        outs, acc_ref = refs[n_in:n_in + n_out], refs[n_in + n_out]
        kk = pl.program_id(2)

        @pl.when(kk == 0)
        def _():
            acc_ref[...] = jnp.zeros_like(acc_ref)

        acc_ref[...] += lax.dot_general(a_ref[...].astype(bf16), b_ref[...].astype(bf16), dims,
                                        preferred_element_type=f32)

        @pl.when(kk == nk - 1)
        def _():
            acc = acc_ref[...]
            if epilogue == "relu2":
                r = jnp.maximum(acc, 0.0)
                outs[0][...] = acc.astype(out_dtype)
                outs[1][...] = (r * r).astype(out_dtype)
            elif epilogue == "drelu2":
                outs[0][...] = (acc * (2.0 * jnp.maximum(refs[2][...].astype(f32), 0.0))).astype(out_dtype)
            else:
                outs[0][...] = acc.astype(out_dtype)

    if mode == "tn":
        a_spec = pl.BlockSpec((tk, tm), lambda i, j, kk: (kk, i))
    else:
        a_spec = pl.BlockSpec((tm, tk), lambda i, j, kk: (i, kk))
    if mode == "nt":
        b_spec = pl.BlockSpec((tn, tk), lambda i, j, kk: (j, kk))
    else:
        b_spec = pl.BlockSpec((tk, tn), lambda i, j, kk: (kk, j))
    o_spec = pl.BlockSpec((tm, tn), lambda i, j, kk: (i, j))
    o_shape = jax.ShapeDtypeStruct((m, n), out_dtype)
    res = pl.pallas_call(
        body, name=name, grid=(m // tm, n // tn, nk),
        in_specs=[a_spec, b_spec] + ([o_spec] if epilogue == "drelu2" else []),
        out_specs=[o_spec] * n_out, out_shape=[o_shape] * n_out,
        scratch_shapes=[pltpu.VMEM((tm, tn), f32)],
        compiler_params=_cparams(dimension_semantics=("parallel", "parallel", "arbitrary")),
    )(*((a, b, side) if epilogue == "drelu2" else (a, b)))
    return res if n_out == 2 else res[0]


@jax.custom_vjp
def mlp(h, w1, w1grad, w2, w2grad):
    _, r = _matmul(h, w1, "nn", bf16, "mlp1_fwd", epilogue="relu2")
    return _matmul(r, w2, "nn", f32, "mlp2_fwd")


def _mlp_fwd(h, w1, w1grad, w2, w2grad):
    u, r = _matmul(h, w1, "nn", bf16, "mlp1_fwd", epilogue="relu2")
    return _matmul(r, w2, "nn", f32, "mlp2_fwd"), (h, w1, w2, u, r)


def _mlp_bwd(res, dy):
    h, w1, w2, u, r = res
    du = _matmul(dy, w2, "nt", bf16, "mlp2_dgrad", epilogue="drelu2", side=u)
    dw2 = _matmul(r, dy, "tn", f32, "mlp2_wgrad")
    dh = _matmul(du, w1, "nt", h.dtype, "mlp1_dgrad")
    dw1 = _matmul(h, du, "tn", f32, "mlp1_wgrad")
    return dh, jnp.zeros_like(w1), dw1, jnp.zeros_like(w2), dw2


mlp.defvjp(_mlp_fwd, _mlp_bwd)


def make_linear(name):
    @jax.custom_vjp
    def linear(a, w, wgrad):
        return _matmul(a, w, "nn", f32, name + "_fwd")

    def fwd(a, w, wgrad):
        return linear(a, w, wgrad), (a, w)

    def bwd(res, dy):
        a, w = res
        da = _matmul(dy, w, "nt", a.dtype, name + "_dgrad")
        dw = _matmul(a, dy, "tn", f32, name + "_wgrad")
        return da, jnp.zeros_like(w), dw

    linear.defvjp(fwd, bwd)
    return linear


def make_rowwise(name, fn, row_out, sum_out=(), tm_pref=256):
    def specs(rows, gpars, cpars, consts, tm):
        s = [pl.BlockSpec((tm, r.shape[1]), lambda i: (i, 0)) for r in rows]
        s += [pl.BlockSpec(p.shape, lambda i: (0, 0)) for p in gpars]
        s += [pl.BlockSpec(p.shape, lambda i: (0, 0)) for p in cpars]
        for cst in consts:
            nb = cst.shape[0] // tm
            s.append(pl.BlockSpec((tm, cst.shape[1]), lambda i, nb=nb: (i % nb, 0)))
        return s

    def tile_rows(rows, consts):
        r = rows[0].shape[0]
        common = math.gcd(r, *[cst.shape[0] for cst in consts])
        tm = _pick(common, (tm_pref, 512, 256, 128, 64, 32, 16, 8))
        return r, tm

    def forward(rows, gpars, cpars, consts):
        r, tm = tile_rows(rows, consts)
        nr, ng, nc, nk = len(rows), len(gpars), len(cpars), len(consts)

        def body(*refs):
            ins = refs[:nr + ng + nc + nk]
            outs = refs[nr + ng + nc + nk:]
            rv = [t[...].astype(f32) for t in ins[:nr]]
            gv = [t[...].astype(f32) for t in ins[nr:nr + ng]]
            cv = [t[...] for t in ins[nr + ng:nr + ng + nc]]
            kv = [t[...].astype(f32) for t in ins[nr + ng + nc:]]
            ro, so = fn(rv, gv, cv, kv)
            for o_ref, val in zip(outs[:len(row_out)], ro):
                o_ref[...] = val.astype(o_ref.dtype)
            if sum_out:
                @pl.when(pl.program_id(0) == 0)
                def _():
                    for o_ref in outs[len(row_out):]:
                        o_ref[...] = jnp.zeros_like(o_ref)
                for o_ref, val in zip(outs[len(row_out):], so):
                    o_ref[...] += val

        out_specs = [pl.BlockSpec((tm, w), lambda i: (i, 0)) for w, _ in row_out]
        out_specs += [pl.BlockSpec(shp, lambda i: (0, 0)) for shp in sum_out]
        out_shape = [jax.ShapeDtypeStruct((r, w), dt) for w, dt in row_out]
        out_shape += [jax.ShapeDtypeStruct(shp, f32) for shp in sum_out]
        res = pl.pallas_call(
            body, name=name + "_fwd", grid=(r // tm,),
            in_specs=specs(rows, gpars, cpars, consts, tm), out_specs=out_specs, out_shape=out_shape,
            compiler_params=_cparams(dimension_semantics=("arbitrary",)),
        )(*rows, *gpars, *cpars, *consts)
        return tuple(res[:len(row_out)]), tuple(res[len(row_out):])

    def backward(rows, gpars, cpars, consts, d_ro, d_so):
        r, tm = tile_rows(rows, consts)
        nr, ng, nc, nk = len(rows), len(gpars), len(cpars), len(consts)
        n_in = nr + ng + nc + nk + len(row_out) + len(sum_out)

        def body(*refs):
            ins, outs = refs[:n_in], refs[n_in:]
            rv = [t[...].astype(f32) for t in ins[:nr]]
            gv = [t[...].astype(f32) for t in ins[nr:nr + ng]]
            cv = [t[...] for t in ins[nr + ng:nr + ng + nc]]
            kv = [t[...].astype(f32) for t in ins[nr + ng + nc:nr + ng + nc + nk]]
            o = nr + ng + nc + nk
            dro = [t[...].astype(f32) for t in ins[o:o + len(row_out)]]
            dso = [t[...] for t in ins[o + len(row_out):]]
            _, vjp = jax.vjp(lambda a, b: tuple(tuple(t) for t in fn(a, b, cv, kv)), rv, gv)
            drv, dgv = vjp((tuple(dro), tuple(dso)))
            for o_ref, val in zip(outs[:nr], drv):
                o_ref[...] = val.astype(o_ref.dtype)
            if ng:
                @pl.when(pl.program_id(0) == 0)
                def _():
                    for o_ref in outs[nr:]:
                        o_ref[...] = jnp.zeros_like(o_ref)
                for o_ref, val in zip(outs[nr:], dgv):
                    o_ref[...] += val

        in_specs = specs(rows, gpars, cpars, consts, tm)
        in_specs += [pl.BlockSpec((tm, w), lambda i: (i, 0)) for w, _ in row_out]
        in_specs += [pl.BlockSpec(shp, lambda i: (0, 0)) for shp in sum_out]
        out_specs = [pl.BlockSpec((tm, t.shape[1]), lambda i: (i, 0)) for t in rows]
        out_specs += [pl.BlockSpec(p.shape, lambda i: (0, 0)) for p in gpars]
        out_shape = [jax.ShapeDtypeStruct(t.shape, t.dtype) for t in rows]
        out_shape += [jax.ShapeDtypeStruct(p.shape, f32) for p in gpars]
        res = pl.pallas_call(
            body, name=name + "_bwd", grid=(r // tm,),
            in_specs=in_specs, out_specs=out_specs, out_shape=out_shape,
            compiler_params=_cparams(dimension_semantics=("arbitrary",)),
        )(*rows, *gpars, *cpars, *consts, *d_ro, *d_so)
        return tuple(res[:nr]), tuple(res[nr:])

    @jax.custom_vjp
    def op(rows, gpars, cpars, consts):
        return forward(rows, gpars, cpars, consts)

    def op_fwd(rows, gpars, cpars, consts):
        return forward(rows, gpars, cpars, consts), (rows, gpars, cpars, consts)

    def op_bwd(res, cts):
        rows, gpars, cpars, consts = res
        d_ro, d_so = cts
        drows, dg = backward(rows, gpars, cpars, consts, d_ro, d_so)
        dg = tuple(d.astype(p.dtype) for d, p in zip(dg, gpars))
        return (drows, dg, tuple(jnp.zeros_like(p) for p in cpars), tuple(jnp.zeros_like(k) for k in consts))

    op.defvjp(op_fwd, op_bwd)
    return op


def _rms(x):
    return x * lax.rsqrt(jnp.mean(x * x, axis=-1, keepdims=True) + EPS)


def _silu(x):
    return x * jax.nn.sigmoid(x)


def _fn_norm_mod(rows, gp, cp, ks):
    (x,), (nw, sc, sh) = rows, gp
    return ((_rms(x) * nw) * (1.0 + sc) + sh,), ()


def _fn_qk_norm_rope(rows, gp, cp, ks, out_scale=1.0):
    (t,), (w,), (pm,), (cos, sin) = rows, gp, cp, ks
    u = _rms(t) * w
    pu = jnp.dot(u, pm, precision=HIGHEST, preferred_element_type=f32)
    return ((u * cos + pu * sin) * out_scale,), ()


def _fn_softplus(rows, gp, cp, ks):
    (x,), (b,) = rows, gp
    v = x + b
    return (jnp.maximum(v, 0.0) + jnp.log(1.0 + jnp.exp(-jnp.abs(v))),), ()


def _fn_ssd_gate(rows, gp, cp, ks):
    (y, z), (nw,) = rows, gp
    return (_rms(y * _silu(z)) * nw,), ()


def _fn_merge(rows, gp, cp, ks):
    ao, so, ga, gs = rows
    return (jax.nn.sigmoid(ga) * ao + jax.nn.sigmoid(gs) * so,), ()


def _fn_res_norm(rows, gp, cp, ks):
    (x, mo), (g1, nw, sc, sh) = rows, gp
    x1 = x + g1 * mo
    return (x1, (_rms(x1) * nw) * (1.0 + sc) + sh), ()


def _fn_loss(rows, gp, cp, ks):
    (x1, ff), (g2,), (tgt,) = rows, gp, ks
    err = x1 + g2 * ff - tgt
    return (), (0.5 * jnp.sum(jnp.sum(err * err, axis=-1, keepdims=True), axis=0, keepdims=True) / D_MODEL,)


HALO = 8


def _conv_tiles(s, c):
    return _pick(s, (512, 256, 128)), _pick(c, (512, 256, 128))


def _halo_specs(tm, tc, s):
    nb = tm // HALO
    last = s // HALO - 1
    cur = pl.BlockSpec((tm, tc), lambda j, i: (i, j))
    prev = pl.BlockSpec((HALO, tc), lambda j, i: (jnp.maximum(i * nb - 1, 0), j))
    nxt = pl.BlockSpec((HALO, tc), lambda j, i: (jnp.minimum((i + 1) * nb, last), j))
    return cur, prev, nxt


def _fill_halo(buf, cur, prev, nxt, tm, i, n_i):
    buf[HALO:HALO + tm, :] = cur[...]
    buf[0:HALO, :] = jnp.where(i > 0, prev[...], 0.0)
    buf[HALO + tm:, :] = jnp.where(i < n_i - 1, nxt[...], 0.0)


def _conv_fwd(x, w, b):
    s, c = x.shape
    tm, tc = _conv_tiles(s, c)
    n_i = s // tm

    def body(cur, prev, nxt, w_ref, b_ref, o_ref, buf):
        i = pl.program_id(1)
        _fill_halo(buf, cur, prev, nxt, tm, i, n_i)
        pre = jnp.zeros((tm, tc), f32) + b_ref[...]
        for k in range(D_CONV):
            pre = pre + buf[HALO - 2 + k:HALO - 2 + k + tm, :] * w_ref[k:k + 1, :]
        o_ref[...] = _silu(pre)

    cur, prev, nxt = _halo_specs(tm, tc, s)
    return pl.pallas_call(
        body, name="conv_silu_fwd", grid=(c // tc, n_i),
        in_specs=[cur, prev, nxt, pl.BlockSpec((D_CONV, tc), lambda j, i: (0, j)),
                  pl.BlockSpec((1, tc), lambda j, i: (0, j))],
        out_specs=pl.BlockSpec((tm, tc), lambda j, i: (i, j)),
        out_shape=jax.ShapeDtypeStruct((s, c), f32),
        scratch_shapes=[pltpu.VMEM((tm + 2 * HALO, tc), f32)],
        compiler_params=_cparams(dimension_semantics=("parallel", "arbitrary")),
    )(x, x, x, w, b)


def _conv_bwd(x, w, b, dy):
    s, c = x.shape
    tm, tc = _conv_tiles(s, c)
    n_i = s // tm
    ext = tm + 8

    def body(cur, prev, nxt, dcur, dprev, dnxt, w_ref, b_ref, dx_ref, dw_ref, db_ref, xbuf, dbuf, pbuf):
        i = pl.program_id(1)
        _fill_halo(xbuf, cur, prev, nxt, tm, i, n_i)
        _fill_halo(dbuf, dcur, dprev, dnxt, tm, i, n_i)
        pre = jnp.zeros((ext, tc), f32) + b_ref[...]
        for k in range(D_CONV):
            pre = pre + xbuf[2 + k:2 + k + ext, :] * w_ref[k:k + 1, :]
        sg = jax.nn.sigmoid(pre)
        pbuf[...] = dbuf[4:4 + ext, :] * (sg * (1.0 + pre * (1.0 - sg)))
        dx = jnp.zeros((tm, tc), f32)
        for k in range(D_CONV):
            dx = dx + pbuf[6 - k:6 - k + tm, :] * w_ref[k:k + 1, :]
        dx_ref[...] = dx

        @pl.when(i == 0)
        def _():
            dw_ref[...] = jnp.zeros_like(dw_ref)
            db_ref[...] = jnp.zeros_like(db_ref)

        dpre = pbuf[4:4 + tm, :]
        db_ref[...] += jnp.sum(dpre, axis=0, keepdims=True)
        for k in range(D_CONV):
            dw_ref[k:k + 1, :] += jnp.sum(dpre * xbuf[HALO - 2 + k:HALO - 2 + k + tm, :], axis=0, keepdims=True)

    cur, prev, nxt = _halo_specs(tm, tc, s)
    return pl.pallas_call(
        body, name="conv_silu_bwd", grid=(c // tc, n_i),
        in_specs=[cur, prev, nxt, cur, prev, nxt, pl.BlockSpec((D_CONV, tc), lambda j, i: (0, j)),
                  pl.BlockSpec((1, tc), lambda j, i: (0, j))],
        out_specs=[pl.BlockSpec((tm, tc), lambda j, i: (i, j)), pl.BlockSpec((D_CONV, tc), lambda j, i: (0, j)),
                   pl.BlockSpec((1, tc), lambda j, i: (0, j))],
        out_shape=[jax.ShapeDtypeStruct((s, c), f32), jax.ShapeDtypeStruct((D_CONV, c), f32),
                   jax.ShapeDtypeStruct((1, c), f32)],
        scratch_shapes=[pltpu.VMEM((tm + 2 * HALO, tc), f32), pltpu.VMEM((tm + 2 * HALO, tc), f32),
                        pltpu.VMEM((ext, tc), f32)],
        compiler_params=_cparams(dimension_semantics=("parallel", "arbitrary")),
    )(x, x, x, dy, dy, dy, w, b)


@jax.custom_vjp
def conv_silu(x, w, b):
    return _conv_fwd(x, w, b)


def _conv_silu_fwd(x, w, b):
    return _conv_fwd(x, w, b), (x, w, b)


def _conv_silu_bwd(res, dy):
    return _conv_bwd(*res, dy)


conv_silu.defvjp(_conv_silu_fwd, _conv_silu_bwd)


ATT_SCALE = HEAD_DIM ** -0.5
Q_SCALE = ATT_SCALE * math.log2(math.e)
LN2 = math.log(2.0)
REP = N_Q_HEADS // N_KV_HEADS


def _attn_fwd(q, k, v):
    hq, s, dh = q.shape
    tq = _pick(s, (256, 128))

    v1 = jnp.concatenate([v, jnp.ones(v.shape[:2] + (1,), v.dtype), jnp.zeros(v.shape[:2] + (dh - 1,), v.dtype)],
                         axis=-1)

    def body(q_ref, k_ref, v_ref, o_ref, lse_ref):
        sc = lax.dot_general(q_ref[0], k_ref[0], _DIMS["nt"], preferred_element_type=f32)
        m = jnp.max(sc, axis=-1, keepdims=True)
        p = jnp.exp2(sc - m).astype(bf16)
        o1 = jnp.dot(p, v_ref[0], preferred_element_type=f32)
        l = o1[:, dh:dh + 1]
        o_ref[0] = (o1[:, :dh] / l).astype(o_ref.dtype)
        lse_ref[0] = m + jnp.log2(l)

    return pl.pallas_call(
        body, name="attn_fwd", grid=(hq, s // tq),
        in_specs=[pl.BlockSpec((1, tq, dh), lambda h, i: (h, i, 0)),
                  pl.BlockSpec((1, s, dh), lambda h, i: (h // REP, 0, 0)),
                  pl.BlockSpec((1, s, 2 * dh), lambda h, i: (h // REP, 0, 0))],
        out_specs=[pl.BlockSpec((1, tq, dh), lambda h, i: (h, i, 0)),
                   pl.BlockSpec((1, tq, 1), lambda h, i: (h, i, 0))],
        out_shape=[jax.ShapeDtypeStruct((hq, s, dh), bf16), jax.ShapeDtypeStruct((hq, s, 1), f32)],
        compiler_params=_cparams(dimension_semantics=("parallel", "arbitrary")),
    )(q, k, v1)


def _attn_bwd(q, k, v, kt, do, lse_row, d_row):
    hq, s, dh = q.shape
    tk = _pick(s, (256, 128))
    cq = _pick(s, (2048, 1024, 512, 256, 128))
    n_c = s // cq

    def body(q_ref, do_ref, lse_ref, d_ref, k_ref, v_ref, kt_ref, dqt_ref, dk_ref, dv_ref):
        j = pl.program_id(1)

        @pl.when(j == 0)
        def _():
            dqt_ref[...] = jnp.zeros_like(dqt_ref)

        kk, vv, ktt = k_ref[0], v_ref[0], kt_ref[0]
        dk = jnp.zeros((tk, dh), f32)
        dv = jnp.zeros((tk, dh), f32)
        for c in range(n_c):
            sl = slice(c * cq, (c + 1) * cq)
            qc, doc = q_ref[0, sl, :], do_ref[0, sl, :]
            st = lax.dot_general(kk, qc, _DIMS["nt"], preferred_element_type=f32)
            pt = jnp.exp2(st - lse_ref[0, :, sl])
            dv = dv + jnp.dot(pt.astype(bf16), doc, preferred_element_type=f32)
            dpt = lax.dot_general(vv, doc, _DIMS["nt"], preferred_element_type=f32)
            dst = (pt * (dpt - d_ref[0, :, sl])).astype(bf16)
            dk = dk + jnp.dot(dst, qc, preferred_element_type=f32)
            dqt_ref[0, :, sl] += jnp.dot(ktt, dst, preferred_element_type=f32)
        dk_ref[0] = dk * LN2
        dv_ref[0] = dv

        @pl.when(j == s // tk - 1)
        def _():
            dqt_ref[...] = dqt_ref[...] * LN2

    return pl.pallas_call(
        body, name="attn_bwd", grid=(hq, s // tk),
        in_specs=[pl.BlockSpec((1, s, dh), lambda h, j: (h, 0, 0)),
                  pl.BlockSpec((1, s, dh), lambda h, j: (h, 0, 0)),
                  pl.BlockSpec((1, 1, s), lambda h, j: (h, 0, 0)),
                  pl.BlockSpec((1, 1, s), lambda h, j: (h, 0, 0)),
                  pl.BlockSpec((1, tk, dh), lambda h, j: (h // REP, j, 0)),
                  pl.BlockSpec((1, tk, dh), lambda h, j: (h // REP, j, 0)),
                  pl.BlockSpec((1, dh, tk), lambda h, j: (h // REP, 0, j))],
        out_specs=[pl.BlockSpec((1, dh, s), lambda h, j: (h, 0, 0)),
                   pl.BlockSpec((1, tk, dh), lambda h, j: (h, j, 0)),
                   pl.BlockSpec((1, tk, dh), lambda h, j: (h, j, 0))],
        out_shape=[jax.ShapeDtypeStruct((hq, dh, s), f32), jax.ShapeDtypeStruct((hq, s, dh), f32),
                   jax.ShapeDtypeStruct((hq, s, dh), f32)],
        compiler_params=_cparams(dimension_semantics=("parallel", "arbitrary")),
    )(q, do, lse_row, d_row, k, v, kt)


@jax.custom_vjp
def attention(q, k, v):
    return _attn_fwd(q, k, v)[0]


def _attention_fwd(q, k, v):
    o, lse = _attn_fwd(q, k, v)
    return o, (q, k, v, o, lse)


def _attention_bwd(res, do):
    q, k, v, o, lse = res
    hq, s, dh = q.shape
    d_row = jnp.sum(do.astype(f32) * o.astype(f32), axis=-1).reshape(hq, 1, s)
    dqt, dkp, dvp = _attn_bwd(q, k, v, jnp.swapaxes(k, 1, 2), do.astype(bf16), lse.reshape(hq, 1, s), d_row)
    dq = jnp.swapaxes(dqt, 1, 2).astype(q.dtype)
    dk = dkp.reshape(N_KV_HEADS, REP, s, dh).sum(axis=1).astype(k.dtype)
    dv = dvp.reshape(N_KV_HEADS, REP, s, dh).sum(axis=1).astype(v.dtype)
    return dq, dk, dv


attention.defvjp(_attention_fwd, _attention_bwd)


HPG = N_SSD_HEADS // N_SSD_GROUPS
GW = HPG * SSD_HEAD_DIM
NEG = -1e30
SPLIT_ROWS = 32


def _ssd_consts():
    k = np.arange(SPLIT_ROWS)[:, None]
    live = k < 3 * HPG
    sel_chunk = ((k % HPG) == (np.arange(HPG * CHUNK)[None, :] // CHUNK)) & live
    sel_head = ((k % HPG) == (np.arange(GW)[None, :] // SSD_HEAD_DIM)) & live
    return jnp.asarray(sel_chunk, bf16), jnp.asarray(sel_head, bf16)


def _split3(x):
    hi = x.astype(bf16).astype(f32)
    r1 = x - hi
    mid = r1.astype(bf16).astype(f32)
    lo = (r1 - mid).astype(bf16).astype(f32)
    return jnp.concatenate([hi, mid, lo, jnp.zeros_like(hi)], axis=0).astype(bf16)


def _tn(a, b):
    return lax.dot_general(a, b, _DIMS["tn"], preferred_element_type=f32)


def _nt(a, b):
    return lax.dot_general(a, b, _DIMS["nt"], preferred_element_type=f32)


def _nn(a, b):
    return jnp.dot(a, b, preferred_element_type=f32)


def _head_sum(sel8, x):
    hi = x.astype(bf16)
    lo = (x - hi.astype(f32)).astype(bf16)
    return _nt(sel8, hi) + _nt(sel8, lo)


def _ssd_masks(reverse):
    r = lax.broadcasted_iota(jnp.int32, (CHUNK, CHUNK), 0)
    c = lax.broadcasted_iota(jnp.int32, (CHUNK, CHUNK), 1)
    lower, upper = r >= c, r <= c
    return (upper, lower) if reverse else (lower, upper)


def _ssd_in_specs(cidx):
    return [pl.BlockSpec((CHUNK, D_INNER), lambda c: (cidx(c), 0)),
            pl.BlockSpec((CHUNK, GN), lambda c: (cidx(c), D_INNER // GN)),
            pl.BlockSpec((CHUNK, GN), lambda c: (cidx(c), D_INNER // GN + 1)),
            pl.BlockSpec((N_SSD_HEADS, CHUNK), lambda c: (0, cidx(c))),
            pl.BlockSpec((N_SSD_HEADS, 1), lambda c: (0, 0)),
            pl.BlockSpec((SPLIT_ROWS, HPG * CHUNK), lambda c: (0, 0)),
            pl.BlockSpec((SPLIT_ROWS, GW), lambda c: (0, 0))]


def _ssd_chunk_common(dtt_ref, a_ref, et_ref, mask_t):
    dtt = dtt_ref[...]
    et = jnp.dot(dtt * a_ref[...], mask_t.astype(f32), precision=HIGHEST, preferred_element_type=f32)
    et_ref[...] = et
    return dtt, et


def _ssd_group_common(g, dtt, et, selc_ref, selh_ref, xs_ref, b_ref, c_ref, last):
    gr = slice(g * HPG, (g + 1) * HPG)
    e3 = _split3(et[gr])
    col = _tn(e3, selc_ref[...])
    eb = _tn(e3, selh_ref[...])
    dtb = _tn(_split3(dtt[gr]), selh_ref[...])
    tbc = eb[last:last + 1, :]
    xs = xs_ref[:, g * GW:(g + 1) * GW]
    bg = b_ref[:, g * D_STATE:(g + 1) * D_STATE].astype(bf16)
    cg = c_ref[:, g * D_STATE:(g + 1) * D_STATE].astype(bf16)
    return col, eb, dtb, tbc, xs, bg, cg


def _ssd_fwd(xbc, dtt, a_col, reverse, y_prev=None, dexp=None):
    s = xbc.shape[0]
    nc = s // CHUNK
    cidx = (lambda c: nc - 1 - c) if reverse else (lambda c: c)
    last = 0 if reverse else CHUNK - 1
    selc, selh = _ssd_consts()
    final = y_prev is not None
    n_in = 9 if final else 7

    def body(*refs):
        xs_ref, b_ref, c_ref, dtt_ref, a_ref, selc_ref, selh_ref = refs[:7]
        y_ref, st_ref, ht_ref, et_ref = refs[n_in:]

        @pl.when(pl.program_id(0) == 0)
        def _():
            ht_ref[...] = jnp.zeros_like(ht_ref)

        mask, mask_t = _ssd_masks(reverse)
        dtt_v, et = _ssd_chunk_common(dtt_ref, a_ref, et_ref, mask_t)
        for g in range(N_SSD_GROUPS):
            col, eb, dtb, tbc, xs, bg, cg = _ssd_group_common(g, dtt_v, et, selc_ref, selh_ref, xs_ref, b_ref, c_ref,
                                                              last)
            xd = xs * dtb
            cb = _nt(cg, bg)
            ht = ht_ref[g]
            st_ref[0, g] = ht
            yoff = _nn(cg, ht.astype(bf16)) * jnp.exp(eb)
            for j in range(HPG):
                h = g * HPG + j
                hs = slice(j * SSD_HEAD_DIM, (j + 1) * SSD_HEAD_DIM)
                lam = jnp.exp(jnp.where(mask, col[:, j * CHUNK:(j + 1) * CHUNK] - et_ref[h:h + 1, :], NEG))
                yj = _nn((cb * lam).astype(bf16), xd[:, hs].astype(bf16)) + yoff[:, hs]
                cols = slice(g * GW + j * SSD_HEAD_DIM, g * GW + (j + 1) * SSD_HEAD_DIM)
                if final:
                    yj = yj + refs[7][:, cols] + xs[:, hs] * refs[8][:, cols]
                y_ref[:, cols] = yj
            ht_ref[g] = jnp.exp(tbc) * ht + _tn(bg, (xd * jnp.exp(tbc - eb)).astype(bf16))

    y_spec = pl.BlockSpec((CHUNK, D_INNER), lambda c: (cidx(c), 0))
    extra_specs = [y_spec, pl.BlockSpec((1, D_INNER), lambda c: (0, 0))] if final else []
    return pl.pallas_call(
        body, name="ssd_fwd_rev" if reverse else "ssd_fwd", grid=(nc,),
        in_specs=_ssd_in_specs(cidx) + extra_specs,
        out_specs=[y_spec, pl.BlockSpec((1, N_SSD_GROUPS, D_STATE, GW), lambda c: (cidx(c), 0, 0, 0))],
        out_shape=[jax.ShapeDtypeStruct((s, D_INNER), f32),
                   jax.ShapeDtypeStruct((nc, N_SSD_GROUPS, D_STATE, GW), f32)],
        scratch_shapes=[pltpu.VMEM((N_SSD_GROUPS, D_STATE, GW), f32), pltpu.VMEM((N_SSD_HEADS, CHUNK), f32)],
        compiler_params=_cparams(dimension_semantics=("arbitrary",)),
    )(xbc, xbc, xbc, dtt, a_col, selc, selh, *((y_prev, dexp) if final else ()))


def _ssd_bwd(xbc, dtt, a_col, states, dy, reverse, dxbc_prev=None, dexp=None):
    s = xbc.shape[0]
    nc = s // CHUNK
    cidx = (lambda c: c) if reverse else (lambda c: nc - 1 - c)
    last = 0 if reverse else CHUNK - 1
    selc, selh = _ssd_consts()
    final = dxbc_prev is not None
    n_in = 11 if final else 9
    n_out = 4 if final else 3

    def body(*refs):
        xs_ref, b_ref, c_ref, dtt_ref, a_ref, selc_ref, selh_ref, st_ref, dy_ref = refs[:9]
        dxbc_ref, ddtt_ref, da_ref = refs[n_in:n_in + 3]
        dh_ref, et_ref, det_ref, det2_ref, ddt_ref, q_ref = refs[n_in + n_out:]
        if final:
            prev_ref, dexp_ref, ddexp_ref = refs[9], refs[10], refs[n_in + 3]

        @pl.when(pl.program_id(0) == 0)
        def _():
            dh_ref[...] = jnp.zeros_like(dh_ref)
            da_ref[...] = jnp.zeros_like(da_ref)
            if final:
                ddexp_ref[...] = jnp.zeros_like(ddexp_ref)

        mask, mask_t = _ssd_masks(reverse)
        dtt_v, et = _ssd_chunk_common(dtt_ref, a_ref, et_ref, mask_t)
        sel8 = selh_ref[0:HPG, :]
        is_last = lax.broadcasted_iota(jnp.int32, (CHUNK, GW), 0) == last
        for g in range(N_SSD_GROUPS):
            col, eb, dtb, tbc, xs, bg, cg = _ssd_group_common(g, dtt_v, et, selc_ref, selh_ref, xs_ref, b_ref, c_ref,
                                                              last)
            xd = xs * dtb
            cb = _nt(cg, bg)
            cbt = _nt(bg, cg)
            exp_t = jnp.exp(tbc)
            dfac = jnp.exp(tbc - eb)
            ht = st_ref[0, g]
            dhn = dh_ref[g]
            ht16, dhn16 = ht.astype(bf16), dhn.astype(bf16)
            dy = dy_ref[:, g * GW:(g + 1) * GW]
            dye = dy * jnp.exp(eb)
            dye16 = dye.astype(bf16)
            dc = _nt(dye16, ht16)
            dh_ref[g] = exp_t * dhn + _tn(cg, dye16)
            deb = dye * _nn(cg, ht16)
            xdd = xd * dfac
            dxdd = _nn(bg, dhn16)
            db = _nt(xdd.astype(bf16), dhn16)
            dxd_state = dxdd * dfac
            ddf = dxdd * xdd
            dtbc = jnp.sum(ddf, axis=0, keepdims=True) + exp_t * jnp.sum(dhn * ht, axis=0, keepdims=True)
            deb = deb - ddf + jnp.where(is_last, dtbc, 0.0)
            dcb = jnp.zeros((CHUNK, CHUNK), f32)
            dcbt = jnp.zeros((CHUNK, CHUNK), f32)
            for j in range(HPG):
                h = g * HPG + j
                hs = slice(j * SSD_HEAD_DIM, (j + 1) * SSD_HEAD_DIM)
                colj = col[:, j * CHUNK:(j + 1) * CHUNK]
                row = et_ref[h:h + 1, :]
                lam = jnp.exp(jnp.where(mask, colj - row, NEG))
                lam_t = jnp.exp(jnp.where(mask_t, row - colj, NEG))
                xdj, dyj = xd[:, hs].astype(bf16), dy[:, hs].astype(bf16)
                t1 = _nt(dyj, xdj) * lam
                t2 = _nt(xdj, dyj) * lam_t
                dcb, dcbt = dcb + t1, dcbt + t2
                det_ref[h:h + 1, :] = -jnp.sum(t1 * cb - t2 * cbt, axis=0, keepdims=True)
                dxdj = _nn((cbt * lam_t).astype(bf16), dyj) + dxd_state[:, hs]
                cols = slice(g * GW + j * SSD_HEAD_DIM, g * GW + (j + 1) * SSD_HEAD_DIM)
                dxs = dxdj * dtb[:, hs]
                if final:
                    dxs = dxs + prev_ref[:, cols] + dy[:, hs] * dexp_ref[:, cols]
                dxbc_ref[:, cols] = dxs
                q_ref[:, hs] = dxdj * xs[:, hs]
            b_cols = slice(D_INNER + g * D_STATE, D_INNER + (g + 1) * D_STATE)
            c_cols = slice(D_INNER + GN + g * D_STATE, D_INNER + GN + (g + 1) * D_STATE)
            db = db + _nn(dcbt.astype(bf16), cg)
            dc = dc + _nn(dcb.astype(bf16), bg)
            if final:
                db, dc = db + prev_ref[:, b_cols], dc + prev_ref[:, c_cols]
                ddexp_ref[:, g * GW:(g + 1) * GW] += jnp.sum(dy * xs, axis=0, keepdims=True)
            dxbc_ref[:, b_cols] = db
            dxbc_ref[:, c_cols] = dc
            det2_ref[g * HPG:(g + 1) * HPG, :] = _head_sum(sel8, deb)
            ddt_ref[g * HPG:(g + 1) * HPG, :] = _head_sum(sel8, q_ref[...])
        dat = jnp.dot(det_ref[...] + det2_ref[...], mask.astype(f32), precision=HIGHEST, preferred_element_type=f32)
        ddtt_ref[...] = ddt_ref[...] + dat * a_ref[...]
        da_ref[...] += jnp.sum(dat * dtt_v, axis=1, keepdims=True)

    in_specs = _ssd_in_specs(cidx) + [
        pl.BlockSpec((1, N_SSD_GROUPS, D_STATE, GW), lambda c: (cidx(c), 0, 0, 0)),
        pl.BlockSpec((CHUNK, D_INNER), lambda c: (cidx(c), 0))]
    hl = pltpu.VMEM((N_SSD_HEADS, CHUNK), f32)
    dxbc_spec = pl.BlockSpec((CHUNK, CONV_DIM), lambda c: (cidx(c), 0))
    dexp_spec = pl.BlockSpec((1, D_INNER), lambda c: (0, 0))
    return pl.pallas_call(
        body, name="ssd_bwd_rev" if reverse else "ssd_bwd", grid=(nc,),
        in_specs=in_specs + ([dxbc_spec, dexp_spec] if final else []),
        out_specs=[dxbc_spec, pl.BlockSpec((N_SSD_HEADS, CHUNK), lambda c: (0, cidx(c))),
                   pl.BlockSpec((N_SSD_HEADS, 1), lambda c: (0, 0))] + ([dexp_spec] if final else []),
        out_shape=[jax.ShapeDtypeStruct((s, CONV_DIM), f32), jax.ShapeDtypeStruct((N_SSD_HEADS, s), f32),
                   jax.ShapeDtypeStruct((N_SSD_HEADS, 1), f32)]
        + ([jax.ShapeDtypeStruct((1, D_INNER), f32)] if final else []),
        scratch_shapes=[pltpu.VMEM((N_SSD_GROUPS, D_STATE, GW), f32), hl, hl, hl, hl, pltpu.VMEM((CHUNK, GW), f32)],
        compiler_params=_cparams(dimension_semantics=("arbitrary",)),
    )(xbc, xbc, xbc, dtt, a_col, selc, selh, states, dy, *((dxbc_prev, dexp) if final else ()))


@jax.custom_vjp
def ssd_bidir(xbc, dtt, a_col, dexp):
    y_f, _ = _ssd_fwd(xbc, dtt[:N_SSD_HEADS], a_col[:N_SSD_HEADS], False)
    return _ssd_fwd(xbc, dtt[N_SSD_HEADS:], a_col[N_SSD_HEADS:], True, y_prev=y_f, dexp=dexp)[0]


def _ssd_bidir_fwd(xbc, dtt, a_col, dexp):
    y_f, st_f = _ssd_fwd(xbc, dtt[:N_SSD_HEADS], a_col[:N_SSD_HEADS], False)
    y, st_b = _ssd_fwd(xbc, dtt[N_SSD_HEADS:], a_col[N_SSD_HEADS:], True, y_prev=y_f, dexp=dexp)
    return y, (xbc, dtt, a_col, dexp, st_f, st_b)


def _ssd_bidir_bwd(res, dy):
    xbc, dtt, a_col, dexp, st_f, st_b = res
    dxbc_f, ddtt_f, da_f = _ssd_bwd(xbc, dtt[:N_SSD_HEADS], a_col[:N_SSD_HEADS], st_f, dy, False)
    dxbc, ddtt_b, da_b, ddexp = _ssd_bwd(xbc, dtt[N_SSD_HEADS:], a_col[N_SSD_HEADS:], st_b, dy, True,
                                         dxbc_prev=dxbc_f, dexp=dexp)
    return dxbc, jnp.concatenate([ddtt_f, ddtt_b], axis=0), jnp.concatenate([da_f, da_b], axis=0), ddexp


ssd_bidir.defvjp(_ssd_bidir_fwd, _ssd_bidir_bwd)


W_NAMES = ("q", "k", "v", "xbc", "z", "dt", "gates", "attn_out", "ssd_out", "o", "mlp1", "mlp2")


def _rope_tables(s):
    rows = s // GRID_W
    pos_row = jnp.repeat(jnp.arange(rows, dtype=jnp.int32), GRID_W).astype(f32)
    pos_col = jnp.tile(jnp.arange(GRID_W, dtype=jnp.int32), rows).astype(f32)
    axis_dim = HEAD_DIM // 2
    inv_freq = ROPE_THETA ** (-jnp.arange(0, axis_dim, 2, dtype=f32) / axis_dim)
    ang_r = pos_row[:, None] * inv_freq[None, :]
    ang_c = pos_col[:, None] * inv_freq[None, :]
    cos = jnp.concatenate([jnp.cos(ang_r), jnp.cos(ang_r), jnp.cos(ang_c), jnp.cos(ang_c)], axis=-1)
    sin = jnp.concatenate([jnp.sin(ang_r), jnp.sin(ang_r), jnp.sin(ang_c), jnp.sin(ang_c)], axis=-1)
    return cos, sin


def _rope_perm():
    p = np.zeros((HEAD_DIM, HEAD_DIM), np.float32)
    for j in range(HEAD_DIM):
        if (j % 32) < 16:
            p[j + 16, j] = -1.0
        else:
            p[j - 16, j] = 1.0
    return jnp.asarray(p)


def local_loss(x, mod, small, wgrads, wfull, target):
    s = x.shape[0]
    lin = {n: make_linear("lin_" + n) for n in W_NAMES if not n.startswith("mlp")}
    shift1, scale1, gate1, shift2, scale2, gate2 = [mod[i] for i in range(6)]

    norm_mod = make_rowwise("norm_mod", _fn_norm_mod, [(D_MODEL, bf16)])
    (h,), _ = norm_mod((x,), (small["norm1_w"], scale1, shift1), (), ())

    proj = {n: lin[n](h, wfull[n], wgrads[n]) for n in ("q", "k", "v", "xbc", "z", "dt", "gates")}

    cos, sin = _rope_tables(s)
    pm = _rope_perm()

    def heads(t, nh):
        return t.reshape(s, nh, HEAD_DIM).transpose(1, 0, 2)

    qk_q = make_rowwise("q_norm_rope", functools.partial(_fn_qk_norm_rope, out_scale=Q_SCALE), [(HEAD_DIM, bf16)],
                        tm_pref=1024)
    qk_k = make_rowwise("k_norm_rope", _fn_qk_norm_rope, [(HEAD_DIM, bf16)], tm_pref=1024)
    (qr,), _ = qk_q((heads(proj["q"], N_Q_HEADS).reshape(N_Q_HEADS * s, HEAD_DIM),), (small["q_norm_w"],), (pm,),
                    (cos, sin))
    (kr,), _ = qk_k((heads(proj["k"], N_KV_HEADS).reshape(N_KV_HEADS * s, HEAD_DIM),), (small["k_norm_w"],), (pm,),
                    (cos, sin))
    vh = heads(proj["v"], N_KV_HEADS).astype(bf16)
    att = attention(qr.reshape(N_Q_HEADS, s, HEAD_DIM), kr.reshape(N_KV_HEADS, s, HEAD_DIM), vh)
    att = att.transpose(1, 0, 2).reshape(s, N_Q_HEADS * HEAD_DIM)
    ao = lin["attn_out"](att, wfull["attn_out"], wgrads["attn_out"])

    xbc = conv_silu(proj["xbc"], small["conv_w"], small["conv_b"])
    softplus = make_rowwise("dt_softplus", _fn_softplus, [(2 * N_SSD_HEADS, f32)])
    (dt,), _ = softplus((proj["dt"][:, :2 * N_SSD_HEADS],), (small["dt_bias"].reshape(1, 2 * N_SSD_HEADS),), (), ())
    a_neg = -jnp.exp(small["A_log"])
    dexp = jnp.repeat(small["ssd_D"].reshape(N_SSD_HEADS), SSD_HEAD_DIM).reshape(1, D_INNER)
    y = ssd_bidir(xbc, dt.T, a_neg.reshape(2 * N_SSD_HEADS, 1), dexp)
    ssd_gate = make_rowwise("ssd_gate", _fn_ssd_gate, [(D_INNER, bf16)], tm_pref=128)
    (ssd_out,), _ = ssd_gate((y, proj["z"]), (small["ssd_norm_w"],), (), ())
    so = lin["ssd_out"](ssd_out, wfull["ssd_out"], wgrads["ssd_out"])

    merge = make_rowwise("merge", _fn_merge, [(D_MODEL, bf16)])
    (merged,), _ = merge((ao, so, proj["gates"][:, :D_MODEL], proj["gates"][:, D_MODEL:]), (), (), ())
    mo = lin["o"](merged, wfull["o"], wgrads["o"])

    res_norm = make_rowwise("res_norm", _fn_res_norm, [(D_MODEL, f32), (D_MODEL, bf16)])
    (x1, h2), _ = res_norm((x, mo), (gate1, small["norm2_w"], scale2, shift2), (), ())
    ff = mlp(h2, wfull["mlp1"], wgrads["mlp1"], wfull["mlp2"], wgrads["mlp2"])
    loss_op = make_rowwise("loss", _fn_loss, [], [(1, 1)])
    _, (loss,) = loss_op((x1, ff), (gate2,), (), (target,))
    return loss[0, 0]


_BC1 = 1.0 - ADAM_B1 ** ADAM_STEP
_BC2 = 1.0 - ADAM_B2 ** ADAM_STEP


def _adamw(w, g, m, v):
    m = ADAM_B1 * m + (1.0 - ADAM_B1) * g
    v = ADAM_B2 * v + (1.0 - ADAM_B2) * (g * g)
    delta = -ADAM_LR * ((m / _BC1) / (jnp.sqrt(v / _BC2) + ADAM_EPS) + ADAM_WD * w)
    return delta, m, v


def _ada_fwd(c_all, w, b):
    n = w.shape[1]

    def body(c_ref, w_ref, b_ref, o_ref):
        o_ref[...] = jnp.dot(_silu(c_ref[...]), w_ref[...], precision=HIGHEST, preferred_element_type=f32) + b_ref[...]

    return pl.pallas_call(body, name="ada_fwd", out_shape=jax.ShapeDtypeStruct((N_DEV, n), f32),
                          compiler_params=_cparams())(c_all, w, b)


def _ada_bwd_adamw(c_all, dmod, w, m, v):
    d, n = w.shape
    tr = _pick(d, (256, 128))

    def body(c_ref, dm_ref, w_ref, m_ref, v_ref, g_ref, dl_ref, mo_ref, vo_ref):
        g = lax.dot_general(_silu(c_ref[...]), dm_ref[...], _DIMS["tn"], precision=HIGHEST,
                            preferred_element_type=f32)
        g_ref[...] = g
        dl_ref[...], mo_ref[...], vo_ref[...] = _adamw(w_ref[...], g, m_ref[...], v_ref[...])

    blk = pl.BlockSpec((tr, n), lambda i: (i, 0))
    return pl.pallas_call(
        body, name="ada_bwd_adamw", grid=(d // tr,),
        in_specs=[pl.BlockSpec((N_DEV, tr), lambda i: (0, i)), pl.BlockSpec((N_DEV, n), lambda i: (0, 0)), blk, blk, blk],
        out_specs=[blk] * 4, out_shape=[jax.ShapeDtypeStruct((d, n), f32)] * 4,
        compiler_params=_cparams(dimension_semantics=("parallel",)),
    )(c_all, dmod, w, m, v)


def _sum_over_mesh(g):
    def body(g_ref, o_ref):
        acc = g_ref[0]
        for d in range(1, N_DEV):
            acc = acc + g_ref[d]
        o_ref[...] = acc

    return pl.pallas_call(body, name="sum_small", out_shape=jax.ShapeDtypeStruct(g.shape[1:], f32),
                          compiler_params=_cparams())(g)


def _adamw_small(w, g, m, v):
    def body(w_ref, g_ref, m_ref, v_ref, dl_ref, mo_ref, vo_ref):
        dl_ref[...], mo_ref[...], vo_ref[...] = _adamw(w_ref[...], g_ref[...], m_ref[...], v_ref[...])

    return pl.pallas_call(body, name="adamw_small", out_shape=[jax.ShapeDtypeStruct(w.shape, f32)] * 3,
                          compiler_params=_cparams())(w, g, m, v)


def _sum_adamw(recv, w, m, v):
    _, r, c = recv.shape
    tr = _pick(r, (240, 256, 128, 64, 16))

    def body(g_ref, w_ref, m_ref, v_ref, go_ref, dl_ref, mo_ref, vo_ref):
        g = g_ref[0].astype(f32)
        for d in range(1, N_DEV):
            g = g + g_ref[d].astype(f32)
        go_ref[...] = g
        dl_ref[...], mo_ref[...], vo_ref[...] = _adamw(w_ref[...], g, m_ref[...], v_ref[...])

    blk = pl.BlockSpec((tr, c), lambda i: (i, 0))
    return pl.pallas_call(
        body, name="sum_adamw", grid=(r // tr,),
        in_specs=[pl.BlockSpec((N_DEV, tr, c), lambda i: (0, i, 0)), blk, blk, blk],
        out_specs=[blk] * 4, out_shape=[jax.ShapeDtypeStruct((r, c), f32)] * 4,
        compiler_params=_cparams(dimension_semantics=("parallel",)),
    )(recv, w, m, v)


def _pack_small(arrs):
    parts = []
    for a in arrs:
        flat = a.reshape(-1).astype(f32)
        parts.append(jnp.pad(flat, (0, (-flat.shape[0]) % LANE)))
    flat = jnp.concatenate(parts)
    flat = jnp.pad(flat, (0, (-flat.shape[0]) % (8 * LANE)))
    return flat.reshape(-1, LANE)


def _unpack_small(packed, shapes):
    flat = packed.reshape(-1)
    out, off = [], 0
    for shp in shapes:
        n = int(np.prod(shp))
        out.append(flat[off:off + n].reshape(shp))
        off += n + (-n) % LANE
    return out


BIG = ("w_in", "w_attn_out", "w_ssd_out", "w_o", "w_mlp1", "w_mlp2")
BIG_ROWS = (D_MODEL * (D_IN_PROJ // N_DEV) // PACK_COLS, N_Q_HEADS * HEAD_DIM // N_DEV, D_INNER // N_DEV,
            D_MODEL // N_DEV, D_MODEL * (D_FF // N_DEV) // PACK_COLS, D_FF // N_DEV)
BIG_PAD_ROWS = (-sum(BIG_ROWS)) % 16


def _pack_big(shards, dtype):
    parts = [s.astype(dtype).reshape(-1, PACK_COLS) for s in shards]
    parts.append(jnp.zeros((BIG_PAD_ROWS, PACK_COLS), dtype))
    return jnp.concatenate(parts, axis=0)


def _unpack_big(packed, shapes):
    out, off = [], 0
    for rows, shp in zip(BIG_ROWS, shapes):
        out.append(packed[off:off + rows].reshape(shp))
        off += rows
    return out


def _split_gathered(g):
    offs = np.cumsum((0,) + BIG_ROWS)
    sl = [g[:, offs[i]:offs[i + 1]] for i in range(len(BIG))]
    n_in = D_IN_PROJ // N_DEV
    w_in = sl[0].reshape(N_DEV, D_MODEL, n_in).transpose(1, 0, 2).reshape(D_MODEL, D_IN_PROJ)
    w = {}
    off = 0
    for name, size in zip(("q", "k", "v", "xbc", "z", "dt", "gates"), PROJ_SIZES):
        w[name] = w_in[:, off:off + size]
        off += size
    w["dt"] = jnp.pad(w["dt"], ((0, 0), (0, DT_PAD - 2 * N_SSD_HEADS)))
    w["attn_out"] = sl[1].reshape(N_Q_HEADS * HEAD_DIM, D_MODEL)
    w["ssd_out"] = sl[2].reshape(D_INNER, D_MODEL)
    w["o"] = sl[3].reshape(D_MODEL, D_MODEL)
    w["mlp1"] = sl[4].reshape(N_DEV, D_MODEL, D_FF // N_DEV).transpose(1, 0, 2).reshape(D_MODEL, D_FF)
    w["mlp2"] = sl[5].reshape(D_FF, D_MODEL)
    return w


def _pack_full_grads(gw):
    n_in = D_IN_PROJ // N_DEV
    gw = {n: g.astype(bf16) for n, g in gw.items()}
    g_in = jnp.concatenate([gw["q"], gw["k"], gw["v"], gw["xbc"], gw["z"], gw["dt"][:, :2 * N_SSD_HEADS],
                            gw["gates"]], axis=1)
    parts = [
        g_in.reshape(D_MODEL, N_DEV, n_in).transpose(1, 0, 2).reshape(N_DEV, -1, PACK_COLS),
        gw["attn_out"].reshape(N_DEV, -1, PACK_COLS),
        gw["ssd_out"].reshape(N_DEV, -1, PACK_COLS),
        gw["o"].reshape(N_DEV, -1, PACK_COLS),
        gw["mlp1"].reshape(D_MODEL, N_DEV, D_FF // N_DEV).transpose(1, 0, 2).reshape(N_DEV, -1, PACK_COLS),
        gw["mlp2"].reshape(N_DEV, -1, PACK_COLS),
        jnp.zeros((N_DEV, BIG_PAD_ROWS, PACK_COLS), bf16),
    ]
    return jnp.concatenate(parts, axis=1)


SMALL = ("norm1_w", "norm2_w", "q_norm_w", "k_norm_w", "conv_w", "conv_b", "A_log", "dt_bias", "ssd_D", "ssd_norm_w")


def kernel(x, c, w_ada, b_ada, norm1_w, norm2_w, w_in, q_norm_w, k_norm_w, conv_w, conv_b, A_log, dt_bias, ssd_D, ssd_norm_w, w_attn_out, w_ssd_out, w_o, w_mlp1, w_mlp2, loss_target, m_w_ada, m_b_ada, m_norm1_w, m_norm2_w, m_w_in, m_q_norm_w, m_k_norm_w, m_conv_w, m_conv_b, m_A_log, m_dt_bias, m_ssd_D, m_ssd_norm_w, m_w_attn_out, m_w_ssd_out, m_w_o, m_w_mlp1, m_w_mlp2, v_w_ada, v_b_ada, v_norm1_w, v_norm2_w, v_w_in, v_q_norm_w, v_k_norm_w, v_conv_w, v_conv_b, v_A_log, v_dt_bias, v_ssd_D, v_ssd_norm_w, v_w_attn_out, v_w_ssd_out, v_w_o, v_w_mlp1, v_w_mlp2):
    args = dict(locals())
    me = _my_index()
    n_ada = 6 * D_MODEL // N_DEV
    n_cw = CONV_DIM // N_DEV

    blk = jnp.zeros((8, D_MODEL), f32)
    blk = blk.at[0:1, :].set(c)
    blk = blk.at[1:1 + D_CONV, :n_cw].set(conv_w[0])
    g0 = _all_gather(blk, "gather_c_convw", in_vmem=True)
    c_all = g0[:, 0, :]
    conv_w_full = g0[:, 1:1 + D_CONV, :n_cw].transpose(1, 0, 2).reshape(D_CONV, CONV_DIM)

    b_shard = lax.dynamic_slice(b_ada, (0, me * n_ada), (1, n_ada))
    mod_cols = _ada_fwd(c_all, w_ada[0], b_shard)
    g1 = _all_gather(mod_cols, "gather_mod", in_vmem=True)
    mod_mine = lax.dynamic_index_in_dim(g1, me, axis=1, keepdims=False)
    mod = mod_mine.reshape(6, 1, D_MODEL)

    big_shapes = [args[n].shape[1:] for n in BIG]
    packed16 = _pack_big([args[n][0] for n in BIG], bf16)
    wfull = _split_gathered(_all_gather(packed16, "gather_weights", in_vmem=False))
    wgrads = {n: jnp.zeros(wfull[n].shape, f32) for n in W_NAMES}

    small = {"norm1_w": norm1_w, "norm2_w": norm2_w, "q_norm_w": q_norm_w, "k_norm_w": k_norm_w,
             "conv_w": conv_w_full, "conv_b": conv_b, "A_log": A_log[0], "dt_bias": dt_bias[0], "ssd_D": ssd_D,
             "ssd_norm_w": ssd_norm_w}

    loss, (gx, gmod, gsmall, gw) = jax.value_and_grad(local_loss, argnums=(0, 1, 2, 3))(
        x[0], mod, small, wgrads, wfull, loss_target[0])

    small_list = [gmod, gsmall["norm1_w"], gsmall["norm2_w"], gsmall["q_norm_w"], gsmall["k_norm_w"], gsmall["conv_w"],
                  gsmall["conv_b"], gsmall["A_log"], gsmall["dt_bias"], gsmall["ssd_D"], gsmall["ssd_norm_w"],
                  loss.reshape(1)]
    small_shapes = [a.shape for a in small_list]
    g2 = _all_gather(_pack_small(small_list), "gather_small_grads", in_vmem=True)
    summed = _unpack_small(_sum_over_mesh(g2), small_shapes)
    loss_total = summed[-1][0]
    g_b_ada = summed[0].reshape(1, 6 * D_MODEL)
    g_small = dict(zip(SMALL, summed[1:-1]))
    g_conv_w = lax.dynamic_slice(g_small["conv_w"], (0, me * n_cw), (D_CONV, n_cw))

    dmod_all = g2[:, :6 * D_MODEL // LANE, :].reshape(N_DEV, 6 * D_MODEL)
    dmod_shard = lax.dynamic_slice(dmod_all, (0, me * n_ada), (N_DEV, n_ada))
    ada = _ada_bwd_adamw(c_all, dmod_shard, w_ada[0], m_w_ada[0], v_w_ada[0])

    small_grads = {"b_ada": g_b_ada, "norm1_w": g_small["norm1_w"], "norm2_w": g_small["norm2_w"],
                   "q_norm_w": g_small["q_norm_w"], "k_norm_w": g_small["k_norm_w"], "conv_w": g_conv_w[None],
                   "conv_b": g_small["conv_b"], "A_log": g_small["A_log"][None], "dt_bias": g_small["dt_bias"][None],
                   "ssd_D": g_small["ssd_D"], "ssd_norm_w": g_small["ssd_norm_w"]}
    sm_names = list(small_grads)
    sm_shapes = [args[n].shape for n in sm_names]
    sm = _adamw_small(_pack_small([args[n] for n in sm_names]), _pack_small([small_grads[n] for n in sm_names]),
                      _pack_small([args["m_" + n] for n in sm_names]), _pack_small([args["v_" + n] for n in sm_names]))
    sm_delta, sm_m, sm_v = [dict(zip(sm_names, _unpack_small(t, sm_shapes))) for t in sm]
    small_grads = {n: small_grads[n].reshape(args[n].shape) for n in sm_names}

    recv = _scatter_blocks(_pack_full_grads(gw), "scatter_grads")
    big = _sum_adamw(recv, _pack_big([args[n][0] for n in BIG], f32), _pack_big([args["m_" + n][0] for n in BIG], f32),
                     _pack_big([args["v_" + n][0] for n in BIG], f32))
    big_g, big_delta, big_m, big_v = [dict(zip(BIG, [t[None] for t in _unpack_big(p, big_shapes)])) for p in big]

    names = ("w_ada", "b_ada", "norm1_w", "norm2_w", "w_in", "q_norm_w", "k_norm_w", "conv_w", "conv_b", "A_log",
             "dt_bias", "ssd_D", "ssd_norm_w", "w_attn_out", "w_ssd_out", "w_o", "w_mlp1", "w_mlp2")
    grads, deltas, new_m, new_v = {}, {}, {}, {}
    for n in names:
        if n == "w_ada":
            grads[n], deltas[n], new_m[n], new_v[n] = [t[None] for t in ada]
        elif n in BIG:
            grads[n], deltas[n], new_m[n], new_v[n] = big_g[n], big_delta[n], big_m[n], big_v[n]
        else:
            grads[n], deltas[n], new_m[n], new_v[n] = small_grads[n], sm_delta[n], sm_m[n], sm_v[n]
    return (loss_total, gx[None], *[grads[n] for n in names], *[deltas[n] for n in names],
            *[new_m[n] for n in names], *[new_v[n] for n in names])
```

```python
import functools
import math

import jax
import jax.numpy as jnp
import numpy as np
from jax import lax
from jax.experimental import pallas as pl
from jax.experimental.pallas import tpu as pltpu

f32 = jnp.float32
bf16 = jnp.bfloat16
HIGHEST = lax.Precision.HIGHEST
MESH = pl.DeviceIdType.MESH

N_DEV = 8
D_MODEL = 1024
GRID_W = 64
N_Q_HEADS = 16
N_KV_HEADS = 4
HEAD_DIM = 64
ROPE_THETA = 10000.0
D_INNER = 2048
SSD_HEAD_DIM = 64
N_SSD_HEADS = 32
N_SSD_GROUPS = 4
D_STATE = 128
D_CONV = 5
CHUNK = 128
D_FF = 4096
EPS = 1e-6
CONV_DIM = D_INNER + 2 * N_SSD_GROUPS * D_STATE
GN = N_SSD_GROUPS * D_STATE
PROJ_SIZES = (N_Q_HEADS * HEAD_DIM, N_KV_HEADS * HEAD_DIM, N_KV_HEADS * HEAD_DIM, CONV_DIM, D_INNER,
              2 * N_SSD_HEADS, 2 * D_MODEL)
D_IN_PROJ = sum(PROJ_SIZES)
DT_PAD = 128

ADAM_LR, ADAM_B1, ADAM_B2, ADAM_EPS, ADAM_WD, ADAM_STEP = 0.001, 0.9, 0.999, 1e-08, 0.01, 10

V7X_VMEM_LIMIT = 56 * 1024 * 1024
LANE = 128
PACK_COLS = 1024


def _cparams(**kw):
    return pltpu.CompilerParams(vmem_limit_bytes=V7X_VMEM_LIMIT, **kw)


def _pick(dim, prefs):
    for p in prefs:
        if dim % p == 0:
            return p
    return dim


def _my_index():
    return 4 * lax.axis_index("x") + 2 * lax.axis_index("y") + lax.axis_index("c")


def _all_gather(block, name, in_vmem):
    r, c = block.shape

    def body(x_ref, out_ref, send_sems, recv_sems, local_sem):
        x, y, cc = lax.axis_index("x"), lax.axis_index("y"), lax.axis_index("c")
        me, sibling = (x, y, cc), (x, y, 1 - cc)
        chips = [(1 - x, y), (x, 1 - y), (1 - x, 1 - y)]

        def slot(px, py, pc):
            return out_ref.at[4 * px + 2 * py + pc]

        def copy(k, blk, to, src=None):
            return pltpu.make_async_remote_copy(
                src_ref=slot(*blk) if src is None else src, dst_ref=slot(*blk),
                send_sem=send_sems.at[k], recv_sem=recv_sems.at[k], device_id=to, device_id_type=MESH)

        mine = pltpu.make_async_copy(x_ref, slot(*me), local_sem)
        mine.start()
        first = [copy(0, me, sibling, src=x_ref)]
        first += [copy(1 + j, me, (*chip, cc), src=x_ref) for j, chip in enumerate(chips)]
        for cp in first:
            cp.start()
        passed = [copy(4 + j, (*chip, cc), sibling) for j, chip in enumerate(chips)]
        for j, chip in enumerate(chips):
            copy(1 + j, (*chip, cc), me).wait_recv()
            passed[j].start()
        copy(0, sibling, me).wait_recv()
        for j, chip in enumerate(chips):
            copy(4 + j, (*chip, 1 - cc), me).wait_recv()
        for cp in first + passed:
            cp.wait_send()
        mine.wait()

    space = pltpu.VMEM if in_vmem else pl.ANY
    return pl.pallas_call(
        body, name=name,
        out_shape=jax.ShapeDtypeStruct((N_DEV, r, c), block.dtype),
        in_specs=[pl.BlockSpec(memory_space=space)],
        out_specs=pl.BlockSpec(memory_space=space),
        scratch_shapes=[pltpu.SemaphoreType.DMA((7,)), pltpu.SemaphoreType.DMA((7,)), pltpu.SemaphoreType.DMA],
    )(block)


def _scatter_blocks(g, name):
    _, r, c = g.shape

    def body(g_ref, out_ref, send_sems, recv_sems, local_sem):
        x, y, cc = lax.axis_index("x"), lax.axis_index("y"), lax.axis_index("c")
        me = 4 * x + 2 * y + cc
        mine = pltpu.make_async_copy(g_ref.at[me], out_ref.at[me], local_sem)
        mine.start()

        def copy(k):
            fx, fy, fc = (k >> 2) & 1, (k >> 1) & 1, k & 1
            px = x + fx - 2 * x * fx
            py = y + fy - 2 * y * fy
            pc = cc + fc - 2 * cc * fc
            peer = 4 * px + 2 * py + pc
            send = pltpu.make_async_remote_copy(
                src_ref=g_ref.at[peer], dst_ref=out_ref.at[me],
                send_sem=send_sems.at[k - 1], recv_sem=recv_sems.at[k - 1],
                device_id=(px, py, pc), device_id_type=MESH)
            recv = pltpu.make_async_remote_copy(
                src_ref=g_ref.at[peer], dst_ref=out_ref.at[peer],
                send_sem=send_sems.at[k - 1], recv_sem=recv_sems.at[k - 1],
                device_id=(px, py, pc), device_id_type=MESH)
            return send, recv

        pairs = [copy(k) for k in range(1, N_DEV)]
        for send, _ in pairs:
            send.start()
        for _, recv in pairs:
            recv.wait_recv()
        for send, _ in pairs:
            send.wait_send()
        mine.wait()

    return pl.pallas_call(
        body, name=name,
        out_shape=jax.ShapeDtypeStruct(g.shape, g.dtype),
        in_specs=[pl.BlockSpec(memory_space=pl.ANY)],
        out_specs=pl.BlockSpec(memory_space=pl.ANY),
        scratch_shapes=[pltpu.SemaphoreType.DMA((7,)), pltpu.SemaphoreType.DMA((7,)), pltpu.SemaphoreType.DMA],
    )(g)


_DIMS = {"nn": (((1,), (0,)), ((), ())), "nt": (((1,), (1,)), ((), ())), "tn": (((0,), (0,)), ((), ()))}


def _matmul(a, b, mode, out_dtype, name, epilogue=None, side=None):
    if mode == "nn":
        (m, k), (_, n) = a.shape, b.shape
    elif mode == "nt":
        (m, k), (n, _) = a.shape, b.shape
    else:
        (k, m), (_, n) = a.shape, b.shape
    tm = _pick(m, (1024, 512, 256, 128))
    tn = _pick(n, (512, 384, 256, 128))
    tk = _pick(k, (1024, 512, 256, 128))
    nk = k // tk
    dims = _DIMS[mode]
    n_in = 3 if epilogue == "drelu2" else 2
    n_out = 2 if epilogue == "relu2" else 1

    def body(*refs):
        a_ref, b_ref = refs[:2]
        outs, acc_ref = refs[n_in:n_in + n_out], refs[n_in + n_out]
        kk = pl.program_id(2)

        @pl.when(kk == 0)
        def _():
            acc_ref[...] = jnp.zeros_like(acc_ref)

        acc_ref[...] += lax.dot_general(a_ref[...].astype(bf16), b_ref[...].astype(bf16), dims,
                                        preferred_element_type=f32)

        @pl.when(kk == nk - 1)
        def _():
            acc = acc_ref[...]
            if epilogue == "relu2":
                r = jnp.maximum(acc, 0.0)
                outs[0][...] = acc.astype(out_dtype)
                outs[1][...] = (r * r).astype(out_dtype)
            elif epilogue == "drelu2":
                outs[0][...] = (acc * (2.0 * jnp.maximum(refs[2][...].astype(f32), 0.0))).astype(out_dtype)
            else:
                outs[0][...] = acc.astype(out_dtype)

    if mode == "tn":
        a_spec = pl.BlockSpec((tk, tm), lambda i, j, kk: (kk, i))
    else:
        a_spec = pl.BlockSpec((tm, tk), lambda i, j, kk: (i, kk))
    if mode == "nt":
        b_spec = pl.BlockSpec((tn, tk), lambda i, j, kk: (j, kk))
    else:
        b_spec = pl.BlockSpec((tk, tn), lambda i, j, kk: (kk, j))
    o_spec = pl.BlockSpec((tm, tn), lambda i, j, kk: (i, j))
    o_shape = jax.ShapeDtypeStruct((m, n), out_dtype)
    res = pl.pallas_call(
        body, name=name, grid=(m // tm, n // tn, nk),
        in_specs=[a_spec, b_spec] + ([o_spec] if epilogue == "drelu2" else []),
        out_specs=[o_spec] * n_out, out_shape=[o_shape] * n_out,
        scratch_shapes=[pltpu.VMEM((tm, tn), f32)],
        compiler_params=_cparams(dimension_semantics=("parallel", "parallel", "arbitrary")),
    )(*((a, b, side) if epilogue == "drelu2" else (a, b)))
    return res if n_out == 2 else res[0]


@jax.custom_vjp
def mlp(h, w1, w1grad, w2, w2grad):
    _, r = _matmul(h, w1, "nn", bf16, "mlp1_fwd", epilogue="relu2")
    return _matmul(r, w2, "nn", f32, "mlp2_fwd")


def _mlp_fwd(h, w1, w1grad, w2, w2grad):
    u, r = _matmul(h, w1, "nn", bf16, "mlp1_fwd", epilogue="relu2")
    return _matmul(r, w2, "nn", f32, "mlp2_fwd"), (h, w1, w2, u, r)


def _mlp_bwd(res, dy):
    h, w1, w2, u, r = res
    du = _matmul(dy, w2, "nt", bf16, "mlp2_dgrad", epilogue="drelu2", side=u)
    dw2 = _matmul(r, dy, "tn", f32, "mlp2_wgrad")
    dh = _matmul(du, w1, "nt", h.dtype, "mlp1_dgrad")
    dw1 = _matmul(h, du, "tn", f32, "mlp1_wgrad")
    return dh, jnp.zeros_like(w1), dw1, jnp.zeros_like(w2), dw2


mlp.defvjp(_mlp_fwd, _mlp_bwd)


def make_linear(name):
    @jax.custom_vjp
    def linear(a, w, wgrad):
        return _matmul(a, w, "nn", f32, name + "_fwd")

    def fwd(a, w, wgrad):
        return linear(a, w, wgrad), (a, w)

    def bwd(res, dy):
        a, w = res
        da = _matmul(dy, w, "nt", a.dtype, name + "_dgrad")
        dw = _matmul(a, dy, "tn", f32, name + "_wgrad")
        return da, jnp.zeros_like(w), dw

    linear.defvjp(fwd, bwd)
    return linear


def make_rowwise(name, fn, row_out, sum_out=(), tm_pref=256):
    def specs(rows, gpars, cpars, consts, tm):
        s = [pl.BlockSpec((tm, r.shape[1]), lambda i: (i, 0)) for r in rows]
        s += [pl.BlockSpec(p.shape, lambda i: (0, 0)) for p in gpars]
        s += [pl.BlockSpec(p.shape, lambda i: (0, 0)) for p in cpars]
        for cst in consts:
            nb = cst.shape[0] // tm
            s.append(pl.BlockSpec((tm, cst.shape[1]), lambda i, nb=nb: (i % nb, 0)))
        return s

    def tile_rows(rows, consts):
        r = rows[0].shape[0]
        common = math.gcd(r, *[cst.shape[0] for cst in consts])
        tm = _pick(common, (tm_pref, 512, 256, 128, 64, 32, 16, 8))
        return r, tm

    def forward(rows, gpars, cpars, consts):
        r, tm = tile_rows(rows, consts)
        nr, ng, nc, nk = len(rows), len(gpars), len(cpars), len(consts)

        def body(*refs):
            ins = refs[:nr + ng + nc + nk]
            outs = refs[nr + ng + nc + nk:]
            rv = [t[...].astype(f32) for t in ins[:nr]]
            gv = [t[...].astype(f32) for t in ins[nr:nr + ng]]
            cv = [t[...] for t in ins[nr + ng:nr + ng + nc]]
            kv = [t[...].astype(f32) for t in ins[nr + ng + nc:]]
            ro, so = fn(rv, gv, cv, kv)
            for o_ref, val in zip(outs[:len(row_out)], ro):
                o_ref[...] = val.astype(o_ref.dtype)
            if sum_out:
                @pl.when(pl.program_id(0) == 0)
                def _():
                    for o_ref in outs[len(row_out):]:
                        o_ref[...] = jnp.zeros_like(o_ref)
                for o_ref, val in zip(outs[len(row_out):], so):
                    o_ref[...] += val

        out_specs = [pl.BlockSpec((tm, w), lambda i: (i, 0)) for w, _ in row_out]
        out_specs += [pl.BlockSpec(shp, lambda i: (0, 0)) for shp in sum_out]
        out_shape = [jax.ShapeDtypeStruct((r, w), dt) for w, dt in row_out]
        out_shape += [jax.ShapeDtypeStruct(shp, f32) for shp in sum_out]
        res = pl.pallas_call(
            body, name=name + "_fwd", grid=(r // tm,),
            in_specs=specs(rows, gpars, cpars, consts, tm), out_specs=out_specs, out_shape=out_shape,
            compiler_params=_cparams(dimension_semantics=("arbitrary",)),
        )(*rows, *gpars, *cpars, *consts)
        return tuple(res[:len(row_out)]), tuple(res[len(row_out):])

    def backward(rows, gpars, cpars, consts, d_ro, d_so):
        r, tm = tile_rows(rows, consts)
        nr, ng, nc, nk = len(rows), len(gpars), len(cpars), len(consts)
        n_in = nr + ng + nc + nk + len(row_out) + len(sum_out)

        def body(*refs):
            ins, outs = refs[:n_in], refs[n_in:]
            rv = [t[...].astype(f32) for t in ins[:nr]]
            gv = [t[...].astype(f32) for t in ins[nr:nr + ng]]
            cv = [t[...] for t in ins[nr + ng:nr + ng + nc]]
            kv = [t[...].astype(f32) for t in ins[nr + ng + nc:nr + ng + nc + nk]]
            o = nr + ng + nc + nk
            dro = [t[...].astype(f32) for t in ins[o:o + len(row_out)]]
            dso = [t[...] for t in ins[o + len(row_out):]]
            _, vjp = jax.vjp(lambda a, b: tuple(tuple(t) for t in fn(a, b, cv, kv)), rv, gv)
            drv, dgv = vjp((tuple(dro), tuple(dso)))
            for o_ref, val in zip(outs[:nr], drv):
                o_ref[...] = val.astype(o_ref.dtype)
            if ng:
                @pl.when(pl.program_id(0) == 0)
                def _():
                    for o_ref in outs[nr:]:
                        o_ref[...] = jnp.zeros_like(o_ref)
                for o_ref, val in zip(outs[nr:], dgv):
                    o_ref[...] += val

        in_specs = specs(rows, gpars, cpars, consts, tm)
        in_specs += [pl.BlockSpec((tm, w), lambda i: (i, 0)) for w, _ in row_out]
        in_specs += [pl.BlockSpec(shp, lambda i: (0, 0)) for shp in sum_out]
        out_specs = [pl.BlockSpec((tm, t.shape[1]), lambda i: (i, 0)) for t in rows]
        out_specs += [pl.BlockSpec(p.shape, lambda i: (0, 0)) for p in gpars]
        out_shape = [jax.ShapeDtypeStruct(t.shape, t.dtype) for t in rows]
        out_shape += [jax.ShapeDtypeStruct(p.shape, f32) for p in gpars]
        res = pl.pallas_call(
            body, name=name + "_bwd", grid=(r // tm,),
            in_specs=in_specs, out_specs=out_specs, out_shape=out_shape,
            compiler_params=_cparams(dimension_semantics=("arbitrary",)),
        )(*rows, *gpars, *cpars, *consts, *d_ro, *d_so)
        return tuple(res[:nr]), tuple(res[nr:])

    @jax.custom_vjp
    def op(rows, gpars, cpars, consts):
        return forward(rows, gpars, cpars, consts)

    def op_fwd(rows, gpars, cpars, consts):
        return forward(rows, gpars, cpars, consts), (rows, gpars, cpars, consts)

    def op_bwd(res, cts):
        rows, gpars, cpars, consts = res
        d_ro, d_so = cts
        drows, dg = backward(rows, gpars, cpars, consts, d_ro, d_so)
        dg = tuple(d.astype(p.dtype) for d, p in zip(dg, gpars))
        return (drows, dg, tuple(jnp.zeros_like(p) for p in cpars), tuple(jnp.zeros_like(k) for k in consts))

    op.defvjp(op_fwd, op_bwd)
    return op


def _rms(x):
    return x * lax.rsqrt(jnp.mean(x * x, axis=-1, keepdims=True) + EPS)


def _silu(x):
    return x * jax.nn.sigmoid(x)


def _fn_norm_mod(rows, gp, cp, ks):
    (x,), (nw, sc, sh) = rows, gp
    return ((_rms(x) * nw) * (1.0 + sc) + sh,), ()


def _fn_qk_norm_rope(rows, gp, cp, ks, out_scale=1.0):
    (t,), (w,), (pm,), (cos, sin) = rows, gp, cp, ks
    u = _rms(t) * w
    pu = jnp.dot(u, pm, precision=HIGHEST, preferred_element_type=f32)
    return ((u * cos + pu * sin) * out_scale,), ()


def make_head_rope(name, nh, out_scale):
    fn = functools.partial(_fn_qk_norm_rope, out_scale=out_scale)
    width = nh * HEAD_DIM

    def one_head(t, w, pm, cos, sin):
        return fn([t], [w], [pm], [cos, sin])[0][0]

    def specs(tm):
        return [pl.BlockSpec((tm, width), lambda i: (i, 0)), pl.BlockSpec((1, HEAD_DIM), lambda i: (0, 0)),
                pl.BlockSpec((HEAD_DIM, HEAD_DIM), lambda i: (0, 0)), pl.BlockSpec((tm, HEAD_DIM), lambda i: (i, 0)),
                pl.BlockSpec((tm, HEAD_DIM), lambda i: (i, 0))]

    def forward(t, w, pm, cos, sin):
        s = t.shape[0]
        tm = _pick(s, (512, 256, 128))

        def body(t_ref, w_ref, pm_ref, cos_ref, sin_ref, o_ref):
            for h in range(nh):
                val = one_head(t_ref[:, h * HEAD_DIM:(h + 1) * HEAD_DIM], w_ref[...], pm_ref[...], cos_ref[...],
                               sin_ref[...])
                o_ref[h] = val.astype(o_ref.dtype)

        return pl.pallas_call(
            body, name=name + "_fwd", grid=(s // tm,), in_specs=specs(tm),
            out_specs=pl.BlockSpec((nh, tm, HEAD_DIM), lambda i: (0, i, 0)),
            out_shape=jax.ShapeDtypeStruct((nh, s, HEAD_DIM), bf16),
            compiler_params=_cparams(dimension_semantics=("arbitrary",)),
        )(t, w, pm, cos, sin)

    def backward(t, w, pm, cos, sin, dout):
        s = t.shape[0]
        tm = _pick(s, (512, 256, 128))

        def body(t_ref, w_ref, pm_ref, cos_ref, sin_ref, do_ref, dt_ref, dw_ref):
            @pl.when(pl.program_id(0) == 0)
            def _():
                dw_ref[...] = jnp.zeros_like(dw_ref)

            pm_v, cos_v, sin_v = pm_ref[...], cos_ref[...], sin_ref[...]
            dw = jnp.zeros((1, HEAD_DIM), f32)
            for h in range(nh):
                sl = slice(h * HEAD_DIM, (h + 1) * HEAD_DIM)
                _, vjp = jax.vjp(lambda a, b: one_head(a, b, pm_v, cos_v, sin_v), t_ref[:, sl], w_ref[...])
                dth, dwh = vjp(do_ref[h].astype(f32))
                dt_ref[:, sl] = dth
                dw = dw + dwh
            dw_ref[...] += dw

        return pl.pallas_call(
            body, name=name + "_bwd", grid=(s // tm,),
            in_specs=specs(tm) + [pl.BlockSpec((nh, tm, HEAD_DIM), lambda i: (0, i, 0))],
            out_specs=[pl.BlockSpec((tm, width), lambda i: (i, 0)), pl.BlockSpec((1, HEAD_DIM), lambda i: (0, 0))],
            out_shape=[jax.ShapeDtypeStruct((s, width), f32), jax.ShapeDtypeStruct((1, HEAD_DIM), f32)],
            compiler_params=_cparams(dimension_semantics=("arbitrary",)),
        )(t, w, pm, cos, sin, dout)

    @jax.custom_vjp
    def op(t, w, pm, cos, sin):
        return forward(t, w, pm, cos, sin)

    def op_fwd(t, w, pm, cos, sin):
        return forward(t, w, pm, cos, sin), (t, w, pm, cos, sin)

    def op_bwd(res, dout):
        dt, dw = backward(*res, dout)
        return dt, dw, jnp.zeros_like(res[2]), jnp.zeros_like(res[3]), jnp.zeros_like(res[4])

    op.defvjp(op_fwd, op_bwd)
    return op


def _fn_softplus(rows, gp, cp, ks):
    (x,), (b,) = rows, gp
    v = x + b
    return (jnp.maximum(v, 0.0) + jnp.log(1.0 + jnp.exp(-jnp.abs(v))),), ()


def _fn_ssd_gate(rows, gp, cp, ks):
    (y, z), (nw,) = rows, gp
    return (_rms(y * _silu(z)) * nw,), ()


def _fn_merge(rows, gp, cp, ks):
    ao, so, ga, gs = rows
    return (jax.nn.sigmoid(ga) * ao + jax.nn.sigmoid(gs) * so,), ()


def _fn_res_norm(rows, gp, cp, ks):
    (x, mo), (g1, nw, sc, sh) = rows, gp
    x1 = x + g1 * mo
    return (x1, (_rms(x1) * nw) * (1.0 + sc) + sh), ()


def _fn_loss(rows, gp, cp, ks):
    (x1, ff), (g2,), (tgt,) = rows, gp, ks
    err = x1 + g2 * ff - tgt
    return (), (0.5 * jnp.sum(jnp.sum(err * err, axis=-1, keepdims=True), axis=0, keepdims=True) / D_MODEL,)


HALO = 8


def _conv_tiles(s, c):
    return _pick(s, (512, 256, 128)), _pick(c, (512, 256, 128))


def _halo_specs(tm, tc, s):
    nb = tm // HALO
    last = s // HALO - 1
    cur = pl.BlockSpec((tm, tc), lambda j, i: (i, j))
    prev = pl.BlockSpec((HALO, tc), lambda j, i: (jnp.maximum(i * nb - 1, 0), j))
    nxt = pl.BlockSpec((HALO, tc), lambda j, i: (jnp.minimum((i + 1) * nb, last), j))
    return cur, prev, nxt


def _fill_halo(buf, cur, prev, nxt, tm, i, n_i):
    buf[HALO:HALO + tm, :] = cur[...]
    buf[0:HALO, :] = jnp.where(i > 0, prev[...], 0.0)
    buf[HALO + tm:, :] = jnp.where(i < n_i - 1, nxt[...], 0.0)


def _conv_fwd(x, w, b):
    s, c = x.shape
    tm, tc = _conv_tiles(s, c)
    n_i = s // tm

    def body(cur, prev, nxt, w_ref, b_ref, o_ref, buf):
        i = pl.program_id(1)
        _fill_halo(buf, cur, prev, nxt, tm, i, n_i)
        pre = jnp.zeros((tm, tc), f32) + b_ref[...]
        for k in range(D_CONV):
            pre = pre + buf[HALO - 2 + k:HALO - 2 + k + tm, :] * w_ref[k:k + 1, :]
        o_ref[...] = _silu(pre)

    cur, prev, nxt = _halo_specs(tm, tc, s)
    return pl.pallas_call(
        body, name="conv_silu_fwd", grid=(c // tc, n_i),
        in_specs=[cur, prev, nxt, pl.BlockSpec((D_CONV, tc), lambda j, i: (0, j)),
                  pl.BlockSpec((1, tc), lambda j, i: (0, j))],
        out_specs=pl.BlockSpec((tm, tc), lambda j, i: (i, j)),
        out_shape=jax.ShapeDtypeStruct((s, c), f32),
        scratch_shapes=[pltpu.VMEM((tm + 2 * HALO, tc), f32)],
        compiler_params=_cparams(dimension_semantics=("parallel", "arbitrary")),
    )(x, x, x, w, b)


def _conv_bwd(x, w, b, dy):
    s, c = x.shape
    tm, tc = _conv_tiles(s, c)
    n_i = s // tm
    ext = tm + 8

    def body(cur, prev, nxt, dcur, dprev, dnxt, w_ref, b_ref, dx_ref, dw_ref, db_ref, xbuf, dbuf, pbuf):
        i = pl.program_id(1)
        _fill_halo(xbuf, cur, prev, nxt, tm, i, n_i)
        _fill_halo(dbuf, dcur, dprev, dnxt, tm, i, n_i)
        pre = jnp.zeros((ext, tc), f32) + b_ref[...]
        for k in range(D_CONV):
            pre = pre + xbuf[2 + k:2 + k + ext, :] * w_ref[k:k + 1, :]
        sg = jax.nn.sigmoid(pre)
        pbuf[...] = dbuf[4:4 + ext, :] * (sg * (1.0 + pre * (1.0 - sg)))
        dx = jnp.zeros((tm, tc), f32)
        for k in range(D_CONV):
            dx = dx + pbuf[6 - k:6 - k + tm, :] * w_ref[k:k + 1, :]
        dx_ref[...] = dx

        @pl.when(i == 0)
        def _():
            dw_ref[...] = jnp.zeros_like(dw_ref)
            db_ref[...] = jnp.zeros_like(db_ref)

        dpre = pbuf[4:4 + tm, :]
        db_ref[...] += jnp.sum(dpre, axis=0, keepdims=True)
        for k in range(D_CONV):
            dw_ref[k:k + 1, :] += jnp.sum(dpre * xbuf[HALO - 2 + k:HALO - 2 + k + tm, :], axis=0, keepdims=True)

    cur, prev, nxt = _halo_specs(tm, tc, s)
    return pl.pallas_call(
        body, name="conv_silu_bwd", grid=(c // tc, n_i),
        in_specs=[cur, prev, nxt, cur, prev, nxt, pl.BlockSpec((D_CONV, tc), lambda j, i: (0, j)),
                  pl.BlockSpec((1, tc), lambda j, i: (0, j))],
        out_specs=[pl.BlockSpec((tm, tc), lambda j, i: (i, j)), pl.BlockSpec((D_CONV, tc), lambda j, i: (0, j)),
                   pl.BlockSpec((1, tc), lambda j, i: (0, j))],
        out_shape=[jax.ShapeDtypeStruct((s, c), f32), jax.ShapeDtypeStruct((D_CONV, c), f32),
                   jax.ShapeDtypeStruct((1, c), f32)],
        scratch_shapes=[pltpu.VMEM((tm + 2 * HALO, tc), f32), pltpu.VMEM((tm + 2 * HALO, tc), f32),
                        pltpu.VMEM((ext, tc), f32)],
        compiler_params=_cparams(dimension_semantics=("parallel", "arbitrary")),
    )(x, x, x, dy, dy, dy, w, b)


@jax.custom_vjp
def conv_silu(x, w, b):
    return _conv_fwd(x, w, b)


def _conv_silu_fwd(x, w, b):
    return _conv_fwd(x, w, b), (x, w, b)


def _conv_silu_bwd(res, dy):
    return _conv_bwd(*res, dy)


conv_silu.defvjp(_conv_silu_fwd, _conv_silu_bwd)


ATT_SCALE = HEAD_DIM ** -0.5
Q_SCALE = ATT_SCALE * math.log2(math.e)
LN2 = math.log(2.0)
REP = N_Q_HEADS // N_KV_HEADS


def _attn_fwd(q, k, v):
    hq, s, dh = q.shape
    tq = _pick(s, (256, 128))

    v1 = jnp.concatenate([v, jnp.ones(v.shape[:2] + (1,), v.dtype), jnp.zeros(v.shape[:2] + (dh - 1,), v.dtype)],
                         axis=-1)

    def body(q_ref, k_ref, v_ref, o_ref, p_ref, linv_ref):
        sc = lax.dot_general(q_ref[0], k_ref[0], _DIMS["nt"], preferred_element_type=f32)
        m = jnp.max(sc, axis=-1, keepdims=True)
        p = jnp.exp2(sc - m).astype(bf16)
        p_ref[0] = p
        o1 = jnp.dot(p, v_ref[0], preferred_element_type=f32)
        linv = 1.0 / o1[:, dh:dh + 1]
        o_ref[0] = (o1[:, :dh] * linv).astype(o_ref.dtype)
        linv_ref[0] = linv

    return pl.pallas_call(
        body, name="attn_fwd", grid=(hq, s // tq),
        in_specs=[pl.BlockSpec((1, tq, dh), lambda h, i: (h, i, 0)),
                  pl.BlockSpec((1, s, dh), lambda h, i: (h // REP, 0, 0)),
                  pl.BlockSpec((1, s, 2 * dh), lambda h, i: (h // REP, 0, 0))],
        out_specs=[pl.BlockSpec((1, tq, dh), lambda h, i: (h, i, 0)),
                   pl.BlockSpec((1, tq, s), lambda h, i: (h, i, 0)),
                   pl.BlockSpec((1, tq, 1), lambda h, i: (h, i, 0))],
        out_shape=[jax.ShapeDtypeStruct((hq, s, dh), bf16), jax.ShapeDtypeStruct((hq, s, s), bf16),
                   jax.ShapeDtypeStruct((hq, s, 1), f32)],
        compiler_params=_cparams(dimension_semantics=("parallel", "arbitrary")),
    )(q, k, v1)


def _attn_bwd(p, do, dot_l, qt, k, v, linv, d_col):
    hq, s, _ = p.shape
    dh = do.shape[-1]
    tq = _pick(s, (256, 128))

    def body(p_ref, do_ref, dot_ref, qt_ref, k_ref, v_ref, linv_ref, d_ref, dq_ref, dkt_ref, dvt_ref):
        @pl.when(pl.program_id(1) == 0)
        def _():
            dkt_ref[...] = jnp.zeros_like(dkt_ref)
            dvt_ref[...] = jnp.zeros_like(dvt_ref)

        pp = p_ref[0]
        dp = lax.dot_general(do_ref[0], v_ref[0], _DIMS["nt"], preferred_element_type=f32)
        ds = (pp.astype(f32) * ((dp - d_ref[0]) * linv_ref[0])).astype(bf16)
        dq_ref[0] = jnp.dot(ds, k_ref[0], preferred_element_type=f32)
        dvt_ref[0] += jnp.dot(dot_ref[0], pp, preferred_element_type=f32)
        dkt_ref[0] += jnp.dot(qt_ref[0], ds, preferred_element_type=f32)

    return pl.pallas_call(
        body, name="attn_bwd", grid=(hq, s // tq),
        in_specs=[pl.BlockSpec((1, tq, s), lambda h, i: (h, i, 0)),
                  pl.BlockSpec((1, tq, dh), lambda h, i: (h, i, 0)),
                  pl.BlockSpec((1, dh, tq), lambda h, i: (h, 0, i)),
                  pl.BlockSpec((1, dh, tq), lambda h, i: (h, 0, i)),
                  pl.BlockSpec((1, s, dh), lambda h, i: (h // REP, 0, 0)),
                  pl.BlockSpec((1, s, dh), lambda h, i: (h // REP, 0, 0)),
                  pl.BlockSpec((1, tq, 1), lambda h, i: (h, i, 0)),
                  pl.BlockSpec((1, tq, 1), lambda h, i: (h, i, 0))],
        out_specs=[pl.BlockSpec((1, tq, dh), lambda h, i: (h, i, 0)),
                   pl.BlockSpec((1, dh, s), lambda h, i: (h, 0, 0)),
                   pl.BlockSpec((1, dh, s), lambda h, i: (h, 0, 0))],
        out_shape=[jax.ShapeDtypeStruct((hq, s, dh), f32), jax.ShapeDtypeStruct((hq, dh, s), f32),
                   jax.ShapeDtypeStruct((hq, dh, s), f32)],
        compiler_params=_cparams(dimension_semantics=("parallel", "arbitrary")),
    )(p, do, dot_l, qt, k, v, linv, d_col)


@jax.custom_vjp
def attention(q, k, v):
    return _attn_fwd(q, k, v)[0]


def _attention_fwd(q, k, v):
    o, p, linv = _attn_fwd(q, k, v)
    return o, (q, k, v, o, p, linv)


def _attention_bwd(res, do):
    q, k, v, o, p, linv = res
    hq, s, dh = q.shape
    do32 = do.astype(f32)
    d_col = jnp.sum(do32 * o.astype(f32), axis=-1, keepdims=True)
    dot_l = jnp.swapaxes(do32 * linv, 1, 2).astype(bf16)
    dq, dkt, dvt = _attn_bwd(p, do.astype(bf16), dot_l, jnp.swapaxes(q, 1, 2), k, v, linv, d_col)

    def per_kv_head(t):
        return jnp.swapaxes(t.reshape(N_KV_HEADS, REP, dh, s).sum(axis=1), 1, 2)

    return (dq * LN2).astype(q.dtype), (per_kv_head(dkt) * LN2).astype(k.dtype), per_kv_head(dvt).astype(v.dtype)


attention.defvjp(_attention_fwd, _attention_bwd)


HPG = N_SSD_HEADS // N_SSD_GROUPS
GW = HPG * SSD_HEAD_DIM
NEG = -1e30
SPLIT_ROWS = 32


def _ssd_consts():
    k = np.arange(SPLIT_ROWS)[:, None]
    live = k < 3 * HPG
    sel_chunk = ((k % HPG) == (np.arange(HPG * CHUNK)[None, :] // CHUNK)) & live
    sel_head = ((k % HPG) == (np.arange(GW)[None, :] // SSD_HEAD_DIM)) & live
    return jnp.asarray(sel_chunk, bf16), jnp.asarray(sel_head, bf16)


def _split3(x):
    hi = x.astype(bf16).astype(f32)
    r1 = x - hi
    mid = r1.astype(bf16).astype(f32)
    lo = (r1 - mid).astype(bf16).astype(f32)
    return jnp.concatenate([hi, mid, lo, jnp.zeros_like(hi)], axis=0).astype(bf16)


def _tn(a, b):
    return lax.dot_general(a, b, _DIMS["tn"], preferred_element_type=f32)


def _nt(a, b):
    return lax.dot_general(a, b, _DIMS["nt"], preferred_element_type=f32)


def _nn(a, b):
    return jnp.dot(a, b, preferred_element_type=f32)


def _head_sum(sel8, x):
    hi = x.astype(bf16)
    lo = (x - hi.astype(f32)).astype(bf16)
    return _nt(sel8, hi) + _nt(sel8, lo)


def _ssd_masks(reverse):
    r = lax.broadcasted_iota(jnp.int32, (CHUNK, CHUNK), 0)
    c = lax.broadcasted_iota(jnp.int32, (CHUNK, CHUNK), 1)
    lower, upper = r >= c, r <= c
    return (upper, lower) if reverse else (lower, upper)


def _ssd_in_specs(cidx):
    return [pl.BlockSpec((CHUNK, D_INNER), lambda c: (cidx(c), 0)),
            pl.BlockSpec((CHUNK, GN), lambda c: (cidx(c), D_INNER // GN)),
            pl.BlockSpec((CHUNK, GN), lambda c: (cidx(c), D_INNER // GN + 1)),
            pl.BlockSpec((N_SSD_HEADS, CHUNK), lambda c: (0, cidx(c))),
            pl.BlockSpec((N_SSD_HEADS, 1), lambda c: (0, 0)),
            pl.BlockSpec((SPLIT_ROWS, HPG * CHUNK), lambda c: (0, 0)),
            pl.BlockSpec((SPLIT_ROWS, GW), lambda c: (0, 0))]


def _ssd_chunk_common(dtt_ref, a_ref, et_ref, mask_t):
    dtt = dtt_ref[...]
    et = jnp.dot(dtt * a_ref[...], mask_t.astype(f32), precision=HIGHEST, preferred_element_type=f32)
    et_ref[...] = et
    return dtt, et


def _ssd_group_common(g, dtt, et, selc_ref, selh_ref, xs_ref, b_ref, c_ref, last):
    gr = slice(g * HPG, (g + 1) * HPG)
    e3 = _split3(et[gr])
    col = _tn(e3, selc_ref[...])
    eb = _tn(e3, selh_ref[...])
    dtb = _tn(_split3(dtt[gr]), selh_ref[...])
    tbc = eb[last:last + 1, :]
    xs = xs_ref[:, g * GW:(g + 1) * GW]
    bg = b_ref[:, g * D_STATE:(g + 1) * D_STATE].astype(bf16)
    cg = c_ref[:, g * D_STATE:(g + 1) * D_STATE].astype(bf16)
    return col, eb, dtb, tbc, xs, bg, cg


def _ssd_fwd(xbc, dtt, a_col, reverse, y_prev=None, dexp=None):
    s = xbc.shape[0]
    nc = s // CHUNK
    cidx = (lambda c: nc - 1 - c) if reverse else (lambda c: c)
    last = 0 if reverse else CHUNK - 1
    selc, selh = _ssd_consts()
    final = y_prev is not None
    n_in = 9 if final else 7

    def body(*refs):
        xs_ref, b_ref, c_ref, dtt_ref, a_ref, selc_ref, selh_ref = refs[:7]
        y_ref, st_ref, ht_ref, et_ref = refs[n_in:]

        @pl.when(pl.program_id(0) == 0)
        def _():
            ht_ref[...] = jnp.zeros_like(ht_ref)

        mask, mask_t = _ssd_masks(reverse)
        dtt_v, et = _ssd_chunk_common(dtt_ref, a_ref, et_ref, mask_t)
        for g in range(N_SSD_GROUPS):
            col, eb, dtb, tbc, xs, bg, cg = _ssd_group_common(g, dtt_v, et, selc_ref, selh_ref, xs_ref, b_ref, c_ref,
                                                              last)
            xd = xs * dtb
            cb = _nt(cg, bg)
            ht = ht_ref[g]
            st_ref[0, g] = ht
            yoff = _nn(cg, ht.astype(bf16)) * jnp.exp(eb)
            for j in range(HPG):
                h = g * HPG + j
                hs = slice(j * SSD_HEAD_DIM, (j + 1) * SSD_HEAD_DIM)
                lam = jnp.exp(jnp.where(mask, col[:, j * CHUNK:(j + 1) * CHUNK] - et_ref[h:h + 1, :], NEG))
                yj = _nn((cb * lam).astype(bf16), xd[:, hs].astype(bf16)) + yoff[:, hs]
                cols = slice(g * GW + j * SSD_HEAD_DIM, g * GW + (j + 1) * SSD_HEAD_DIM)
                if final:
                    yj = yj + refs[7][:, cols] + xs[:, hs] * refs[8][:, cols]
                y_ref[:, cols] = yj
            ht_ref[g] = jnp.exp(tbc) * ht + _tn(bg, (xd * jnp.exp(tbc - eb)).astype(bf16))

    y_spec = pl.BlockSpec((CHUNK, D_INNER), lambda c: (cidx(c), 0))
    extra_specs = [y_spec, pl.BlockSpec((1, D_INNER), lambda c: (0, 0))] if final else []
    return pl.pallas_call(
        body, name="ssd_fwd_rev" if reverse else "ssd_fwd", grid=(nc,),
        in_specs=_ssd_in_specs(cidx) + extra_specs,
        out_specs=[y_spec, pl.BlockSpec((1, N_SSD_GROUPS, D_STATE, GW), lambda c: (cidx(c), 0, 0, 0))],
        out_shape=[jax.ShapeDtypeStruct((s, D_INNER), f32),
                   jax.ShapeDtypeStruct((nc, N_SSD_GROUPS, D_STATE, GW), f32)],
        scratch_shapes=[pltpu.VMEM((N_SSD_GROUPS, D_STATE, GW), f32), pltpu.VMEM((N_SSD_HEADS, CHUNK), f32)],
        compiler_params=_cparams(dimension_semantics=("arbitrary",)),
    )(xbc, xbc, xbc, dtt, a_col, selc, selh, *((y_prev, dexp) if final else ()))


def _ssd_bwd(xbc, dtt, a_col, states, dy, reverse, dxbc_prev=None, dexp=None):
    s = xbc.shape[0]
    nc = s // CHUNK
    cidx = (lambda c: c) if reverse else (lambda c: nc - 1 - c)
    last = 0 if reverse else CHUNK - 1
    selc, selh = _ssd_consts()
    final = dxbc_prev is not None
    n_in = 11 if final else 9
    n_out = 4 if final else 3

    def body(*refs):
        xs_ref, b_ref, c_ref, dtt_ref, a_ref, selc_ref, selh_ref, st_ref, dy_ref = refs[:9]
        dxbc_ref, ddtt_ref, da_ref = refs[n_in:n_in + 3]
        dh_ref, et_ref, det_ref, det2_ref, ddt_ref, q_ref = refs[n_in + n_out:]
        if final:
            prev_ref, dexp_ref, ddexp_ref = refs[9], refs[10], refs[n_in + 3]

        @pl.when(pl.program_id(0) == 0)
        def _():
            dh_ref[...] = jnp.zeros_like(dh_ref)
            da_ref[...] = jnp.zeros_like(da_ref)
            if final:
                ddexp_ref[...] = jnp.zeros_like(ddexp_ref)

        mask, mask_t = _ssd_masks(reverse)
        dtt_v, et = _ssd_chunk_common(dtt_ref, a_ref, et_ref, mask_t)
        sel8 = selh_ref[0:HPG, :]
        is_last = lax.broadcasted_iota(jnp.int32, (CHUNK, GW), 0) == last
        for g in range(N_SSD_GROUPS):
            col, eb, dtb, tbc, xs, bg, cg = _ssd_group_common(g, dtt_v, et, selc_ref, selh_ref, xs_ref, b_ref, c_ref,
                                                              last)
            xd = xs * dtb
            cb = _nt(cg, bg)
            cbt = _nt(bg, cg)
            exp_t = jnp.exp(tbc)
            dfac = jnp.exp(tbc - eb)
            ht = st_ref[0, g]
            dhn = dh_ref[g]
            ht16, dhn16 = ht.astype(bf16), dhn.astype(bf16)
            dy = dy_ref[:, g * GW:(g + 1) * GW]
            dye = dy * jnp.exp(eb)
            dye16 = dye.astype(bf16)
            dc = _nt(dye16, ht16)
            dh_ref[g] = exp_t * dhn + _tn(cg, dye16)
            deb = dye * _nn(cg, ht16)
            xdd = xd * dfac
            dxdd = _nn(bg, dhn16)
            db = _nt(xdd.astype(bf16), dhn16)
            dxd_state = dxdd * dfac
            ddf = dxdd * xdd
            dtbc = jnp.sum(ddf, axis=0, keepdims=True) + exp_t * jnp.sum(dhn * ht, axis=0, keepdims=True)
            deb = deb - ddf + jnp.where(is_last, dtbc, 0.0)
            dcb = jnp.zeros((CHUNK, CHUNK), f32)
            dcbt = jnp.zeros((CHUNK, CHUNK), f32)
            for j in range(HPG):
                h = g * HPG + j
                hs = slice(j * SSD_HEAD_DIM, (j + 1) * SSD_HEAD_DIM)
                colj = col[:, j * CHUNK:(j + 1) * CHUNK]
                row = et_ref[h:h + 1, :]
                lam = jnp.exp(jnp.where(mask, colj - row, NEG))
                lam_t = jnp.exp(jnp.where(mask_t, row - colj, NEG))
                xdj, dyj = xd[:, hs].astype(bf16), dy[:, hs].astype(bf16)
                t1 = _nt(dyj, xdj) * lam
                t2 = _nt(xdj, dyj) * lam_t
                dcb, dcbt = dcb + t1, dcbt + t2
                det_ref[h:h + 1, :] = -jnp.sum(t1 * cb - t2 * cbt, axis=0, keepdims=True)
                dxdj = _nn((cbt * lam_t).astype(bf16), dyj) + dxd_state[:, hs]
                cols = slice(g * GW + j * SSD_HEAD_DIM, g * GW + (j + 1) * SSD_HEAD_DIM)
                dxs = dxdj * dtb[:, hs]
                if final:
                    dxs = dxs + prev_ref[:, cols] + dy[:, hs] * dexp_ref[:, cols]
                dxbc_ref[:, cols] = dxs
                q_ref[:, hs] = dxdj * xs[:, hs]
            b_cols = slice(D_INNER + g * D_STATE, D_INNER + (g + 1) * D_STATE)
            c_cols = slice(D_INNER + GN + g * D_STATE, D_INNER + GN + (g + 1) * D_STATE)
            db = db + _nn(dcbt.astype(bf16), cg)
            dc = dc + _nn(dcb.astype(bf16), bg)
            if final:
                db, dc = db + prev_ref[:, b_cols], dc + prev_ref[:, c_cols]
                ddexp_ref[:, g * GW:(g + 1) * GW] += jnp.sum(dy * xs, axis=0, keepdims=True)
            dxbc_ref[:, b_cols] = db
            dxbc_ref[:, c_cols] = dc
            det2_ref[g * HPG:(g + 1) * HPG, :] = _head_sum(sel8, deb)
            ddt_ref[g * HPG:(g + 1) * HPG, :] = _head_sum(sel8, q_ref[...])
        dat = jnp.dot(det_ref[...] + det2_ref[...], mask.astype(f32), precision=HIGHEST, preferred_element_type=f32)
        ddtt_ref[...] = ddt_ref[...] + dat * a_ref[...]
        da_ref[...] += jnp.sum(dat * dtt_v, axis=1, keepdims=True)

    in_specs = _ssd_in_specs(cidx) + [
        pl.BlockSpec((1, N_SSD_GROUPS, D_STATE, GW), lambda c: (cidx(c), 0, 0, 0)),
        pl.BlockSpec((CHUNK, D_INNER), lambda c: (cidx(c), 0))]
    hl = pltpu.VMEM((N_SSD_HEADS, CHUNK), f32)
    dxbc_spec = pl.BlockSpec((CHUNK, CONV_DIM), lambda c: (cidx(c), 0))
    dexp_spec = pl.BlockSpec((1, D_INNER), lambda c: (0, 0))
    return pl.pallas_call(
        body, name="ssd_bwd_rev" if reverse else "ssd_bwd", grid=(nc,),
        in_specs=in_specs + ([dxbc_spec, dexp_spec] if final else []),
        out_specs=[dxbc_spec, pl.BlockSpec((N_SSD_HEADS, CHUNK), lambda c: (0, cidx(c))),
                   pl.BlockSpec((N_SSD_HEADS, 1), lambda c: (0, 0))] + ([dexp_spec] if final else []),
        out_shape=[jax.ShapeDtypeStruct((s, CONV_DIM), f32), jax.ShapeDtypeStruct((N_SSD_HEADS, s), f32),
                   jax.ShapeDtypeStruct((N_SSD_HEADS, 1), f32)]
        + ([jax.ShapeDtypeStruct((1, D_INNER), f32)] if final else []),
        scratch_shapes=[pltpu.VMEM((N_SSD_GROUPS, D_STATE, GW), f32), hl, hl, hl, hl, pltpu.VMEM((CHUNK, GW), f32)],
        compiler_params=_cparams(dimension_semantics=("arbitrary",)),
    )(xbc, xbc, xbc, dtt, a_col, selc, selh, states, dy, *((dxbc_prev, dexp) if final else ()))


@jax.custom_vjp
def ssd_bidir(xbc, dtt, a_col, dexp):
    y_f, _ = _ssd_fwd(xbc, dtt[:N_SSD_HEADS], a_col[:N_SSD_HEADS], False)
    return _ssd_fwd(xbc, dtt[N_SSD_HEADS:], a_col[N_SSD_HEADS:], True, y_prev=y_f, dexp=dexp)[0]


def _ssd_bidir_fwd(xbc, dtt, a_col, dexp):
    y_f, st_f = _ssd_fwd(xbc, dtt[:N_SSD_HEADS], a_col[:N_SSD_HEADS], False)
    y, st_b = _ssd_fwd(xbc, dtt[N_SSD_HEADS:], a_col[N_SSD_HEADS:], True, y_prev=y_f, dexp=dexp)
    return y, (xbc, dtt, a_col, dexp, st_f, st_b)


def _ssd_bidir_bwd(res, dy):
    xbc, dtt, a_col, dexp, st_f, st_b = res
    dxbc_f, ddtt_f, da_f = _ssd_bwd(xbc, dtt[:N_SSD_HEADS], a_col[:N_SSD_HEADS], st_f, dy, False)
    dxbc, ddtt_b, da_b, ddexp = _ssd_bwd(xbc, dtt[N_SSD_HEADS:], a_col[N_SSD_HEADS:], st_b, dy, True,
                                         dxbc_prev=dxbc_f, dexp=dexp)
    return dxbc, jnp.concatenate([ddtt_f, ddtt_b], axis=0), jnp.concatenate([da_f, da_b], axis=0), ddexp


ssd_bidir.defvjp(_ssd_bidir_fwd, _ssd_bidir_bwd)


W_NAMES = ("q", "k", "v", "xbc", "z", "dt", "gates", "attn_out", "ssd_out", "o", "mlp1", "mlp2")


def _rope_tables(s):
    rows = s // GRID_W
    pos_row = jnp.repeat(jnp.arange(rows, dtype=jnp.int32), GRID_W).astype(f32)
    pos_col = jnp.tile(jnp.arange(GRID_W, dtype=jnp.int32), rows).astype(f32)
    axis_dim = HEAD_DIM // 2
    inv_freq = ROPE_THETA ** (-jnp.arange(0, axis_dim, 2, dtype=f32) / axis_dim)
    ang_r = pos_row[:, None] * inv_freq[None, :]
    ang_c = pos_col[:, None] * inv_freq[None, :]
    cos = jnp.concatenate([jnp.cos(ang_r), jnp.cos(ang_r), jnp.cos(ang_c), jnp.cos(ang_c)], axis=-1)
    sin = jnp.concatenate([jnp.sin(ang_r), jnp.sin(ang_r), jnp.sin(ang_c), jnp.sin(ang_c)], axis=-1)
    return cos, sin


def _rope_perm():
    p = np.zeros((HEAD_DIM, HEAD_DIM), np.float32)
    for j in range(HEAD_DIM):
        if (j % 32) < 16:
            p[j + 16, j] = -1.0
        else:
            p[j - 16, j] = 1.0
    return jnp.asarray(p)


def local_loss(x, mod, small, wgrads, wfull, target):
    s = x.shape[0]
    lin = {n: make_linear("lin_" + n) for n in W_NAMES if not n.startswith("mlp")}
    shift1, scale1, gate1, shift2, scale2, gate2 = [mod[i] for i in range(6)]

    norm_mod = make_rowwise("norm_mod", _fn_norm_mod, [(D_MODEL, bf16)])
    (h,), _ = norm_mod((x,), (small["norm1_w"], scale1, shift1), (), ())

    proj = {n: lin[n](h, wfull[n], wgrads[n]) for n in ("q", "k", "v", "xbc", "z", "dt", "gates")}

    cos, sin = _rope_tables(s)
    pm = _rope_perm()

    def heads(t, nh):
        return t.reshape(s, nh, HEAD_DIM).transpose(1, 0, 2)

    qr = make_head_rope("q_norm_rope", N_Q_HEADS, Q_SCALE)(proj["q"], small["q_norm_w"], pm, cos, sin)
    kr = make_head_rope("k_norm_rope", N_KV_HEADS, 1.0)(proj["k"], small["k_norm_w"], pm, cos, sin)
    vh = heads(proj["v"], N_KV_HEADS).astype(bf16)
    att = attention(qr, kr, vh)
    att = att.transpose(1, 0, 2).reshape(s, N_Q_HEADS * HEAD_DIM)
    ao = lin["attn_out"](att, wfull["attn_out"], wgrads["attn_out"])

    xbc = conv_silu(proj["xbc"], small["conv_w"], small["conv_b"])
    softplus = make_rowwise("dt_softplus", _fn_softplus, [(2 * N_SSD_HEADS, f32)])
    (dt,), _ = softplus((proj["dt"][:, :2 * N_SSD_HEADS],), (small["dt_bias"].reshape(1, 2 * N_SSD_HEADS),), (), ())
    a_neg = -jnp.exp(small["A_log"])
    dexp = jnp.repeat(small["ssd_D"].reshape(N_SSD_HEADS), SSD_HEAD_DIM).reshape(1, D_INNER)
    y = ssd_bidir(xbc, dt.T, a_neg.reshape(2 * N_SSD_HEADS, 1), dexp)
    ssd_gate = make_rowwise("ssd_gate", _fn_ssd_gate, [(D_INNER, bf16)], tm_pref=128)
    (ssd_out,), _ = ssd_gate((y, proj["z"]), (small["ssd_norm_w"],), (), ())
    so = lin["ssd_out"](ssd_out, wfull["ssd_out"], wgrads["ssd_out"])

    merge = make_rowwise("merge", _fn_merge, [(D_MODEL, bf16)])
    (merged,), _ = merge((ao, so, proj["gates"][:, :D_MODEL], proj["gates"][:, D_MODEL:]), (), (), ())
    mo = lin["o"](merged, wfull["o"], wgrads["o"])

    res_norm = make_rowwise("res_norm", _fn_res_norm, [(D_MODEL, f32), (D_MODEL, bf16)])
    (x1, h2), _ = res_norm((x, mo), (gate1, small["norm2_w"], scale2, shift2), (), ())
    ff = mlp(h2, wfull["mlp1"], wgrads["mlp1"], wfull["mlp2"], wgrads["mlp2"])
    loss_op = make_rowwise("loss", _fn_loss, [], [(1, 1)])
    _, (loss,) = loss_op((x1, ff), (gate2,), (), (target,))
    return loss[0, 0]


_BC1 = 1.0 - ADAM_B1 ** ADAM_STEP
_BC2 = 1.0 - ADAM_B2 ** ADAM_STEP


def _adamw(w, g, m, v):
    m = ADAM_B1 * m + (1.0 - ADAM_B1) * g
    v = ADAM_B2 * v + (1.0 - ADAM_B2) * (g * g)
    delta = -ADAM_LR * ((m / _BC1) / (jnp.sqrt(v / _BC2) + ADAM_EPS) + ADAM_WD * w)
    return delta, m, v


def _ada_fwd(c_all, w, b):
    n = w.shape[1]

    def body(c_ref, w_ref, b_ref, o_ref):
        o_ref[...] = jnp.dot(_silu(c_ref[...]), w_ref[...], precision=HIGHEST, preferred_element_type=f32) + b_ref[...]

    return pl.pallas_call(body, name="ada_fwd", out_shape=jax.ShapeDtypeStruct((N_DEV, n), f32),
                          compiler_params=_cparams())(c_all, w, b)


def _ada_bwd_adamw(c_all, dmod, w, m, v):
    d, n = w.shape
    tr = _pick(d, (256, 128))

    def body(c_ref, dm_ref, w_ref, m_ref, v_ref, g_ref, dl_ref, mo_ref, vo_ref):
        g = lax.dot_general(_silu(c_ref[...]), dm_ref[...], _DIMS["tn"], precision=HIGHEST,
                            preferred_element_type=f32)
        g_ref[...] = g
        dl_ref[...], mo_ref[...], vo_ref[...] = _adamw(w_ref[...], g, m_ref[...], v_ref[...])

    blk = pl.BlockSpec((tr, n), lambda i: (i, 0))
    return pl.pallas_call(
        body, name="ada_bwd_adamw", grid=(d // tr,),
        in_specs=[pl.BlockSpec((N_DEV, tr), lambda i: (0, i)), pl.BlockSpec((N_DEV, n), lambda i: (0, 0)), blk, blk, blk],
        out_specs=[blk] * 4, out_shape=[jax.ShapeDtypeStruct((d, n), f32)] * 4,
        compiler_params=_cparams(dimension_semantics=("parallel",)),
    )(c_all, dmod, w, m, v)


def _sum_over_mesh(g):
    def body(g_ref, o_ref):
        acc = g_ref[0]
        for d in range(1, N_DEV):
            acc = acc + g_ref[d]
        o_ref[...] = acc

    return pl.pallas_call(body, name="sum_small", out_shape=jax.ShapeDtypeStruct(g.shape[1:], f32),
                          compiler_params=_cparams())(g)


def _adamw_small(w, g, m, v):
    def body(w_ref, g_ref, m_ref, v_ref, dl_ref, mo_ref, vo_ref):
        dl_ref[...], mo_ref[...], vo_ref[...] = _adamw(w_ref[...], g_ref[...], m_ref[...], v_ref[...])

    return pl.pallas_call(body, name="adamw_small", out_shape=[jax.ShapeDtypeStruct(w.shape, f32)] * 3,
                          compiler_params=_cparams())(w, g, m, v)


def _sum_adamw(recv, w, m, v):
    _, r, c = recv.shape
    tr = _pick(r, (240, 256, 128, 64, 16))

    def body(g_ref, w_ref, m_ref, v_ref, go_ref, dl_ref, mo_ref, vo_ref):
        g = g_ref[0].astype(f32)
        for d in range(1, N_DEV):
            g = g + g_ref[d].astype(f32)
        go_ref[...] = g
        dl_ref[...], mo_ref[...], vo_ref[...] = _adamw(w_ref[...], g, m_ref[...], v_ref[...])

    blk = pl.BlockSpec((tr, c), lambda i: (i, 0))
    return pl.pallas_call(
        body, name="sum_adamw", grid=(r // tr,),
        in_specs=[pl.BlockSpec((N_DEV, tr, c), lambda i: (0, i, 0)), blk, blk, blk],
        out_specs=[blk] * 4, out_shape=[jax.ShapeDtypeStruct((r, c), f32)] * 4,
        compiler_params=_cparams(dimension_semantics=("parallel",)),
    )(recv, w, m, v)


def _pack_small(arrs):
    parts = []
    for a in arrs:
        flat = a.reshape(-1).astype(f32)
        parts.append(jnp.pad(flat, (0, (-flat.shape[0]) % LANE)))
    flat = jnp.concatenate(parts)
    flat = jnp.pad(flat, (0, (-flat.shape[0]) % (8 * LANE)))
    return flat.reshape(-1, LANE)


def _unpack_small(packed, shapes):
    flat = packed.reshape(-1)
    out, off = [], 0
    for shp in shapes:
        n = int(np.prod(shp))
        out.append(flat[off:off + n].reshape(shp))
        off += n + (-n) % LANE
    return out


BIG = ("w_in", "w_attn_out", "w_ssd_out", "w_o", "w_mlp1", "w_mlp2")
BIG_ROWS = (D_MODEL * (D_IN_PROJ // N_DEV) // PACK_COLS, N_Q_HEADS * HEAD_DIM // N_DEV, D_INNER // N_DEV,
            D_MODEL // N_DEV, D_MODEL * (D_FF // N_DEV) // PACK_COLS, D_FF // N_DEV)
BIG_PAD_ROWS = (-sum(BIG_ROWS)) % 16


def _pack_big(shards, dtype):
    parts = [s.astype(dtype).reshape(-1, PACK_COLS) for s in shards]
    parts.append(jnp.zeros((BIG_PAD_ROWS, PACK_COLS), dtype))
    return jnp.concatenate(parts, axis=0)


def _unpack_big(packed, shapes):
    out, off = [], 0
    for rows, shp in zip(BIG_ROWS, shapes):
        out.append(packed[off:off + rows].reshape(shp))
        off += rows
    return out


def _split_gathered(g):
    offs = np.cumsum((0,) + BIG_ROWS)
    sl = [g[:, offs[i]:offs[i + 1]] for i in range(len(BIG))]
    n_in = D_IN_PROJ // N_DEV
    w_in = sl[0].reshape(N_DEV, D_MODEL, n_in).transpose(1, 0, 2).reshape(D_MODEL, D_IN_PROJ)
    w = {}
    off = 0
    for name, size in zip(("q", "k", "v", "xbc", "z", "dt", "gates"), PROJ_SIZES):
        w[name] = w_in[:, off:off + size]
        off += size
    w["dt"] = jnp.pad(w["dt"], ((0, 0), (0, DT_PAD - 2 * N_SSD_HEADS)))
    w["attn_out"] = sl[1].reshape(N_Q_HEADS * HEAD_DIM, D_MODEL)
    w["ssd_out"] = sl[2].reshape(D_INNER, D_MODEL)
    w["o"] = sl[3].reshape(D_MODEL, D_MODEL)
    w["mlp1"] = sl[4].reshape(N_DEV, D_MODEL, D_FF // N_DEV).transpose(1, 0, 2).reshape(D_MODEL, D_FF)
    w["mlp2"] = sl[5].reshape(D_FF, D_MODEL)
    return w


def _pack_full_grads(gw):
    n_in = D_IN_PROJ // N_DEV
    gw = {n: g.astype(bf16) for n, g in gw.items()}
    g_in = jnp.concatenate([gw["q"], gw["k"], gw["v"], gw["xbc"], gw["z"], gw["dt"][:, :2 * N_SSD_HEADS],
                            gw["gates"]], axis=1)
    parts = [
        g_in.reshape(D_MODEL, N_DEV, n_in).transpose(1, 0, 2).reshape(N_DEV, -1, PACK_COLS),
        gw["attn_out"].reshape(N_DEV, -1, PACK_COLS),
        gw["ssd_out"].reshape(N_DEV, -1, PACK_COLS),
        gw["o"].reshape(N_DEV, -1, PACK_COLS),
        gw["mlp1"].reshape(D_MODEL, N_DEV, D_FF // N_DEV).transpose(1, 0, 2).reshape(N_DEV, -1, PACK_COLS),
        gw["mlp2"].reshape(N_DEV, -1, PACK_COLS),
        jnp.zeros((N_DEV, BIG_PAD_ROWS, PACK_COLS), bf16),
    ]
    return jnp.concatenate(parts, axis=1)


SMALL = ("norm1_w", "norm2_w", "q_norm_w", "k_norm_w", "conv_w", "conv_b", "A_log", "dt_bias", "ssd_D", "ssd_norm_w")


def kernel(x, c, w_ada, b_ada, norm1_w, norm2_w, w_in, q_norm_w, k_norm_w, conv_w, conv_b, A_log, dt_bias, ssd_D, ssd_norm_w, w_attn_out, w_ssd_out, w_o, w_mlp1, w_mlp2, loss_target, m_w_ada, m_b_ada, m_norm1_w, m_norm2_w, m_w_in, m_q_norm_w, m_k_norm_w, m_conv_w, m_conv_b, m_A_log, m_dt_bias, m_ssd_D, m_ssd_norm_w, m_w_attn_out, m_w_ssd_out, m_w_o, m_w_mlp1, m_w_mlp2, v_w_ada, v_b_ada, v_norm1_w, v_norm2_w, v_w_in, v_q_norm_w, v_k_norm_w, v_conv_w, v_conv_b, v_A_log, v_dt_bias, v_ssd_D, v_ssd_norm_w, v_w_attn_out, v_w_ssd_out, v_w_o, v_w_mlp1, v_w_mlp2):
    args = dict(locals())
    me = _my_index()
    n_ada = 6 * D_MODEL // N_DEV
    n_cw = CONV_DIM // N_DEV

    blk = jnp.zeros((8, D_MODEL), f32)
    blk = blk.at[0:1, :].set(c)
    blk = blk.at[1:1 + D_CONV, :n_cw].set(conv_w[0])
    g0 = _all_gather(blk, "gather_c_convw", in_vmem=True)
    c_all = g0[:, 0, :]
    conv_w_full = g0[:, 1:1 + D_CONV, :n_cw].transpose(1, 0, 2).reshape(D_CONV, CONV_DIM)

    b_shard = lax.dynamic_slice(b_ada, (0, me * n_ada), (1, n_ada))
    mod_cols = _ada_fwd(c_all, w_ada[0], b_shard)
    g1 = _all_gather(mod_cols, "gather_mod", in_vmem=True)
    mod_mine = lax.dynamic_index_in_dim(g1, me, axis=1, keepdims=False)
    mod = mod_mine.reshape(6, 1, D_MODEL)

    big_shapes = [args[n].shape[1:] for n in BIG]
    packed16 = _pack_big([args[n][0] for n in BIG], bf16)
    wfull = _split_gathered(_all_gather(packed16, "gather_weights", in_vmem=False))
    wgrads = {n: jnp.zeros(wfull[n].shape, f32) for n in W_NAMES}

    small = {"norm1_w": norm1_w, "norm2_w": norm2_w, "q_norm_w": q_norm_w, "k_norm_w": k_norm_w,
             "conv_w": conv_w_full, "conv_b": conv_b, "A_log": A_log[0], "dt_bias": dt_bias[0], "ssd_D": ssd_D,
             "ssd_norm_w": ssd_norm_w}

    loss, (gx, gmod, gsmall, gw) = jax.value_and_grad(local_loss, argnums=(0, 1, 2, 3))(
        x[0], mod, small, wgrads, wfull, loss_target[0])

    small_list = [gmod, gsmall["norm1_w"], gsmall["norm2_w"], gsmall["q_norm_w"], gsmall["k_norm_w"], gsmall["conv_w"],
                  gsmall["conv_b"], gsmall["A_log"], gsmall["dt_bias"], gsmall["ssd_D"], gsmall["ssd_norm_w"],
                  loss.reshape(1)]
    small_shapes = [a.shape for a in small_list]
    g2 = _all_gather(_pack_small(small_list), "gather_small_grads", in_vmem=True)
    summed = _unpack_small(_sum_over_mesh(g2), small_shapes)
    loss_total = summed[-1][0]
    g_b_ada = summed[0].reshape(1, 6 * D_MODEL)
    g_small = dict(zip(SMALL, summed[1:-1]))
    g_conv_w = lax.dynamic_slice(g_small["conv_w"], (0, me * n_cw), (D_CONV, n_cw))

    dmod_all = g2[:, :6 * D_MODEL // LANE, :].reshape(N_DEV, 6 * D_MODEL)
    dmod_shard = lax.dynamic_slice(dmod_all, (0, me * n_ada), (N_DEV, n_ada))
    ada = _ada_bwd_adamw(c_all, dmod_shard, w_ada[0], m_w_ada[0], v_w_ada[0])

    small_grads = {"b_ada": g_b_ada, "norm1_w": g_small["norm1_w"], "norm2_w": g_small["norm2_w"],
                   "q_norm_w": g_small["q_norm_w"], "k_norm_w": g_small["k_norm_w"], "conv_w": g_conv_w[None],
                   "conv_b": g_small["conv_b"], "A_log": g_small["A_log"][None], "dt_bias": g_small["dt_bias"][None],
                   "ssd_D": g_small["ssd_D"], "ssd_norm_w": g_small["ssd_norm_w"]}
    sm_names = list(small_grads)
    sm_shapes = [args[n].shape for n in sm_names]
    sm = _adamw_small(_pack_small([args[n] for n in sm_names]), _pack_small([small_grads[n] for n in sm_names]),
                      _pack_small([args["m_" + n] for n in sm_names]), _pack_small([args["v_" + n] for n in sm_names]))
    sm_delta, sm_m, sm_v = [dict(zip(sm_names, _unpack_small(t, sm_shapes))) for t in sm]
    small_grads = {n: small_grads[n].reshape(args[n].shape) for n in sm_names}

    recv = _scatter_blocks(_pack_full_grads(gw), "scatter_grads")
    big = _sum_adamw(recv, _pack_big([args[n][0] for n in BIG], f32), _pack_big([args["m_" + n][0] for n in BIG], f32),
                     _pack_big([args["v_" + n][0] for n in BIG], f32))
    big_g, big_delta, big_m, big_v = [dict(zip(BIG, [t[None] for t in _unpack_big(p, big_shapes)])) for p in big]

    names = ("w_ada", "b_ada", "norm1_w", "norm2_w", "w_in", "q_norm_w", "k_norm_w", "conv_w", "conv_b", "A_log",
             "dt_bias", "ssd_D", "ssd_norm_w", "w_attn_out", "w_ssd_out", "w_o", "w_mlp1", "w_mlp2")
    grads, deltas, new_m, new_v = {}, {}, {}, {}
    for n in names:
        if n == "w_ada":
            grads[n], deltas[n], new_m[n], new_v[n] = [t[None] for t in ada]
        elif n in BIG:
            grads[n], deltas[n], new_m[n], new_v[n] = big_g[n], big_delta[n], big_m[n], big_v[n]
        else:
            grads[n], deltas[n], new_m[n], new_v[n] = small_grads[n], sm_delta[n], sm_m[n], sm_v[n]
    return (loss_total, gx[None], *[grads[n] for n in names], *[deltas[n] for n in names],
            *[new_m[n] for n in names], *[new_v[n] for n in names])
```

```python
import functools
import math

import jax
import jax.numpy as jnp
import numpy as np
from jax import lax
from jax.experimental import pallas as pl
from jax.experimental.pallas import tpu as pltpu

f32 = jnp.float32
bf16 = jnp.bfloat16
HIGHEST = lax.Precision.HIGHEST
MESH = pl.DeviceIdType.MESH

N_DEV = 8
D_MODEL = 1024
GRID_W = 64
N_Q_HEADS = 16
N_KV_HEADS = 4
HEAD_DIM = 64
ROPE_THETA = 10000.0
D_INNER = 2048
SSD_HEAD_DIM = 64
N_SSD_HEADS = 32
N_SSD_GROUPS = 4
D_STATE = 128
D_CONV = 5
CHUNK = 128
D_FF = 4096
EPS = 1e-6
CONV_DIM = D_INNER + 2 * N_SSD_GROUPS * D_STATE
GN = N_SSD_GROUPS * D_STATE
PROJ_NAMES = ("q", "k", "v", "xbc", "z", "dt", "ga", "gs")
PROJ_SIZES = (N_Q_HEADS * HEAD_DIM, N_KV_HEADS * HEAD_DIM, N_KV_HEADS * HEAD_DIM, CONV_DIM, D_INNER,
              2 * N_SSD_HEADS, D_MODEL, D_MODEL)
D_IN_PROJ = sum(PROJ_SIZES)
DT_PAD = 128

ADAM_LR, ADAM_B1, ADAM_B2, ADAM_EPS, ADAM_WD, ADAM_STEP = 0.001, 0.9, 0.999, 1e-08, 0.01, 10

V7X_VMEM_LIMIT = 56 * 1024 * 1024
LANE = 128
PACK_COLS = 1024


def _cparams(**kw):
    return pltpu.CompilerParams(vmem_limit_bytes=V7X_VMEM_LIMIT, **kw)


def _pick(dim, prefs):
    for p in prefs:
        if dim % p == 0:
            return p
    return dim


def _my_index():
    return 4 * lax.axis_index("x") + 2 * lax.axis_index("y") + lax.axis_index("c")


def _all_gather(block, name, in_vmem):
    r, c = block.shape

    def body(x_ref, out_ref, send_sems, recv_sems, local_sem):
        x, y, cc = lax.axis_index("x"), lax.axis_index("y"), lax.axis_index("c")
        me, sibling = (x, y, cc), (x, y, 1 - cc)
        chips = [(1 - x, y), (x, 1 - y), (1 - x, 1 - y)]

        def slot(px, py, pc):
            return out_ref.at[4 * px + 2 * py + pc]

        def copy(k, blk, to, src=None):
            return pltpu.make_async_remote_copy(
                src_ref=slot(*blk) if src is None else src, dst_ref=slot(*blk),
                send_sem=send_sems.at[k], recv_sem=recv_sems.at[k], device_id=to, device_id_type=MESH)

        mine = pltpu.make_async_copy(x_ref, slot(*me), local_sem)
        mine.start()
        first = [copy(0, me, sibling, src=x_ref)]
        first += [copy(1 + j, me, (*chip, cc), src=x_ref) for j, chip in enumerate(chips)]
        for cp in first:
            cp.start()
        passed = [copy(4 + j, (*chip, cc), sibling) for j, chip in enumerate(chips)]
        for j, chip in enumerate(chips):
            copy(1 + j, (*chip, cc), me).wait_recv()
            passed[j].start()
        copy(0, sibling, me).wait_recv()
        for j, chip in enumerate(chips):
            copy(4 + j, (*chip, 1 - cc), me).wait_recv()
        for cp in first + passed:
            cp.wait_send()
        mine.wait()

    space = pltpu.VMEM if in_vmem else pl.ANY
    return pl.pallas_call(
        body, name=name,
        out_shape=jax.ShapeDtypeStruct((N_DEV, r, c), block.dtype),
        in_specs=[pl.BlockSpec(memory_space=space)],
        out_specs=pl.BlockSpec(memory_space=space),
        scratch_shapes=[pltpu.SemaphoreType.DMA((7,)), pltpu.SemaphoreType.DMA((7,)), pltpu.SemaphoreType.DMA],
    )(block)


def _scatter_blocks(g, name):
    _, r, c = g.shape

    def body(g_ref, out_ref, send_sems, recv_sems, local_sem):
        x, y, cc = lax.axis_index("x"), lax.axis_index("y"), lax.axis_index("c")
        me = 4 * x + 2 * y + cc
        mine = pltpu.make_async_copy(g_ref.at[me], out_ref.at[me], local_sem)
        mine.start()

        def copy(k):
            fx, fy, fc = (k >> 2) & 1, (k >> 1) & 1, k & 1
            px = x + fx - 2 * x * fx
            py = y + fy - 2 * y * fy
            pc = cc + fc - 2 * cc * fc
            peer = 4 * px + 2 * py + pc
            send = pltpu.make_async_remote_copy(
                src_ref=g_ref.at[peer], dst_ref=out_ref.at[me],
                send_sem=send_sems.at[k - 1], recv_sem=recv_sems.at[k - 1],
                device_id=(px, py, pc), device_id_type=MESH)
            recv = pltpu.make_async_remote_copy(
                src_ref=g_ref.at[peer], dst_ref=out_ref.at[peer],
                send_sem=send_sems.at[k - 1], recv_sem=recv_sems.at[k - 1],
                device_id=(px, py, pc), device_id_type=MESH)
            return send, recv

        pairs = [copy(k) for k in range(1, N_DEV)]
        for send, _ in pairs:
            send.start()
        for _, recv in pairs:
            recv.wait_recv()
        for send, _ in pairs:
            send.wait_send()
        mine.wait()

    return pl.pallas_call(
        body, name=name,
        out_shape=jax.ShapeDtypeStruct(g.shape, g.dtype),
        in_specs=[pl.BlockSpec(memory_space=pl.ANY)],
        out_specs=pl.BlockSpec(memory_space=pl.ANY),
        scratch_shapes=[pltpu.SemaphoreType.DMA((7,)), pltpu.SemaphoreType.DMA((7,)), pltpu.SemaphoreType.DMA],
    )(g)


_DIMS = {"nn": (((1,), (0,)), ((), ())), "nt": (((1,), (1,)), ((), ())), "tn": (((0,), (0,)), ((), ()))}


def _matmul(a, b, mode, out_dtype, name, epilogue=None, side=None):
    if mode == "nn":
        (m, k), (_, n) = a.shape, b.shape
    elif mode == "nt":
        (m, k), (n, _) = a.shape, b.shape
    else:
        (k, m), (_, n) = a.shape, b.shape
    tm = _pick(m, (1024, 512, 256, 128))
    tn = _pick(n, (512, 384, 256, 128))
    tk = _pick(k, (1024, 512, 256, 128))
    nk = k // tk
    dims = _DIMS[mode]
    n_in = 3 if epilogue == "drelu2" else 2
    n_out = 2 if epilogue == "relu2" else 1

    def body(*refs):
        a_ref, b_ref = refs[:2]
        outs, acc_ref = refs[n_in:n_in + n_out], refs[n_in + n_out]
        kk = pl.program_id(2)

        @pl.when(kk == 0)
        def _():
            acc_ref[...] = jnp.zeros_like(acc_ref)

        acc_ref[...] += lax.dot_general(a_ref[...].astype(bf16), b_ref[...].astype(bf16), dims,
                                        preferred_element_type=f32)

        @pl.when(kk == nk - 1)
        def _():
            acc = acc_ref[...]
            if epilogue == "relu2":
                r = jnp.maximum(acc, 0.0)
                outs[0][...] = acc.astype(out_dtype)
                outs[1][...] = (r * r).astype(out_dtype)
            elif epilogue == "drelu2":
                outs[0][...] = (acc * (2.0 * jnp.maximum(refs[2][...].astype(f32), 0.0))).astype(out_dtype)
            else:
                outs[0][...] = acc.astype(out_dtype)

    if mode == "tn":
        a_spec = pl.BlockSpec((tk, tm), lambda i, j, kk: (kk, i))
    else:
        a_spec = pl.BlockSpec((tm, tk), lambda i, j, kk: (i, kk))
    if mode == "nt":
        b_spec = pl.BlockSpec((tn, tk), lambda i, j, kk: (j, kk))
    else:
        b_spec = pl.BlockSpec((tk, tn), lambda i, j, kk: (kk, j))
    o_spec = pl.BlockSpec((tm, tn), lambda i, j, kk: (i, j))
    o_shape = jax.ShapeDtypeStruct((m, n), out_dtype)
    res = pl.pallas_call(
        body, name=name, grid=(m // tm, n // tn, nk),
        in_specs=[a_spec, b_spec] + ([o_spec] if epilogue == "drelu2" else []),
        out_specs=[o_spec] * n_out, out_shape=[o_shape] * n_out,
        scratch_shapes=[pltpu.VMEM((tm, tn), f32)],
        compiler_params=_cparams(dimension_semantics=("parallel", "parallel", "arbitrary")),
    )(*((a, b, side) if epilogue == "drelu2" else (a, b)))
    return res if n_out == 2 else res[0]


@jax.custom_vjp
def mlp(h, w1, w1grad, w2, w2grad):
    _, r = _matmul(h, w1, "nn", bf16, "mlp1_fwd", epilogue="relu2")
    return _matmul(r, w2, "nn", f32, "mlp2_fwd")


def _mlp_fwd(h, w1, w1grad, w2, w2grad):
    u, r = _matmul(h, w1, "nn", bf16, "mlp1_fwd", epilogue="relu2")
    return _matmul(r, w2, "nn", f32, "mlp2_fwd"), (h, w1, w2, u, r)


def _mlp_bwd(res, dy):
    h, w1, w2, u, r = res
    du = _matmul(dy, w2, "nt", bf16, "mlp2_dgrad", epilogue="drelu2", side=u)
    dw2 = _matmul(r, dy, "tn", f32, "mlp2_wgrad")
    dh = _matmul(du, w1, "nt", h.dtype, "mlp1_dgrad")
    dw1 = _matmul(h, du, "tn", f32, "mlp1_wgrad")
    return dh, jnp.zeros_like(w1), dw1, jnp.zeros_like(w2), dw2


mlp.defvjp(_mlp_fwd, _mlp_bwd)


def make_linear(name):
    @jax.custom_vjp
    def linear(a, w, wgrad):
        return _matmul(a, w, "nn", f32, name + "_fwd")

    def fwd(a, w, wgrad):
        return linear(a, w, wgrad), (a, w)

    def bwd(res, dy):
        a, w = res
        da = _matmul(dy, w, "nt", a.dtype, name + "_dgrad")
        dw = _matmul(a, dy, "tn", f32, name + "_wgrad")
        return da, jnp.zeros_like(w), dw

    linear.defvjp(fwd, bwd)
    return linear


def make_rowwise(name, fn, row_out, sum_out=(), tm_pref=256):
    def specs(rows, gpars, cpars, consts, tm):
        s = [pl.BlockSpec((tm, r.shape[1]), lambda i: (i, 0)) for r in rows]
        s += [pl.BlockSpec(p.shape, lambda i: (0, 0)) for p in gpars]
        s += [pl.BlockSpec(p.shape, lambda i: (0, 0)) for p in cpars]
        for cst in consts:
            nb = cst.shape[0] // tm
            s.append(pl.BlockSpec((tm, cst.shape[1]), lambda i, nb=nb: (i % nb, 0)))
        return s

    def tile_rows(rows, consts):
        r = rows[0].shape[0]
        common = math.gcd(r, *[cst.shape[0] for cst in consts])
        tm = _pick(common, (tm_pref, 512, 256, 128, 64, 32, 16, 8))
        return r, tm

    def forward(rows, gpars, cpars, consts):
        r, tm = tile_rows(rows, consts)
        nr, ng, nc, nk = len(rows), len(gpars), len(cpars), len(consts)

        def body(*refs):
            ins = refs[:nr + ng + nc + nk]
            outs = refs[nr + ng + nc + nk:]
            rv = [t[...].astype(f32) for t in ins[:nr]]
            gv = [t[...].astype(f32) for t in ins[nr:nr + ng]]
            cv = [t[...] for t in ins[nr + ng:nr + ng + nc]]
            kv = [t[...].astype(f32) for t in ins[nr + ng + nc:]]
            ro, so = fn(rv, gv, cv, kv)
            for o_ref, val in zip(outs[:len(row_out)], ro):
                o_ref[...] = val.astype(o_ref.dtype)
            if sum_out:
                @pl.when(pl.program_id(0) == 0)
                def _():
                    for o_ref in outs[len(row_out):]:
                        o_ref[...] = jnp.zeros_like(o_ref)
                for o_ref, val in zip(outs[len(row_out):], so):
                    o_ref[...] += val

        out_specs = [pl.BlockSpec((tm, w), lambda i: (i, 0)) for w, _ in row_out]
        out_specs += [pl.BlockSpec(shp, lambda i: (0, 0)) for shp in sum_out]
        out_shape = [jax.ShapeDtypeStruct((r, w), dt) for w, dt in row_out]
        out_shape += [jax.ShapeDtypeStruct(shp, f32) for shp in sum_out]
        res = pl.pallas_call(
            body, name=name + "_fwd", grid=(r // tm,),
            in_specs=specs(rows, gpars, cpars, consts, tm), out_specs=out_specs, out_shape=out_shape,
            compiler_params=_cparams(dimension_semantics=("arbitrary",)),
        )(*rows, *gpars, *cpars, *consts)
        return tuple(res[:len(row_out)]), tuple(res[len(row_out):])

    def backward(rows, gpars, cpars, consts, d_ro, d_so):
        r, tm = tile_rows(rows, consts)
        nr, ng, nc, nk = len(rows), len(gpars), len(cpars), len(consts)
        n_in = nr + ng + nc + nk + len(row_out) + len(sum_out)

        def body(*refs):
            ins, outs = refs[:n_in], refs[n_in:]
            rv = [t[...].astype(f32) for t in ins[:nr]]
            gv = [t[...].astype(f32) for t in ins[nr:nr + ng]]
            cv = [t[...] for t in ins[nr + ng:nr + ng + nc]]
            kv = [t[...].astype(f32) for t in ins[nr + ng + nc:nr + ng + nc + nk]]
            o = nr + ng + nc + nk
            dro = [t[...].astype(f32) for t in ins[o:o + len(row_out)]]
            dso = [t[...] for t in ins[o + len(row_out):]]
            _, vjp = jax.vjp(lambda a, b: tuple(tuple(t) for t in fn(a, b, cv, kv)), rv, gv)
            drv, dgv = vjp((tuple(dro), tuple(dso)))
            for o_ref, val in zip(outs[:nr], drv):
                o_ref[...] = val.astype(o_ref.dtype)
            if ng:
                @pl.when(pl.program_id(0) == 0)
                def _():
                    for o_ref in outs[nr:]:
                        o_ref[...] = jnp.zeros_like(o_ref)
                for o_ref, val in zip(outs[nr:], dgv):
                    o_ref[...] += val

        in_specs = specs(rows, gpars, cpars, consts, tm)
        in_specs += [pl.BlockSpec((tm, w), lambda i: (i, 0)) for w, _ in row_out]
        in_specs += [pl.BlockSpec(shp, lambda i: (0, 0)) for shp in sum_out]
        out_specs = [pl.BlockSpec((tm, t.shape[1]), lambda i: (i, 0)) for t in rows]
        out_specs += [pl.BlockSpec(p.shape, lambda i: (0, 0)) for p in gpars]
        out_shape = [jax.ShapeDtypeStruct(t.shape, t.dtype) for t in rows]
        out_shape += [jax.ShapeDtypeStruct(p.shape, f32) for p in gpars]
        res = pl.pallas_call(
            body, name=name + "_bwd", grid=(r // tm,),
            in_specs=in_specs, out_specs=out_specs, out_shape=out_shape,
            compiler_params=_cparams(dimension_semantics=("arbitrary",)),
        )(*rows, *gpars, *cpars, *consts, *d_ro, *d_so)
        return tuple(res[:nr]), tuple(res[nr:])

    @jax.custom_vjp
    def op(rows, gpars, cpars, consts):
        return forward(rows, gpars, cpars, consts)

    def op_fwd(rows, gpars, cpars, consts):
        return forward(rows, gpars, cpars, consts), (rows, gpars, cpars, consts)

    def op_bwd(res, cts):
        rows, gpars, cpars, consts = res
        d_ro, d_so = cts
        drows, dg = backward(rows, gpars, cpars, consts, d_ro, d_so)
        dg = tuple(d.astype(p.dtype) for d, p in zip(dg, gpars))
        return (drows, dg, tuple(jnp.zeros_like(p) for p in cpars), tuple(jnp.zeros_like(k) for k in consts))

    op.defvjp(op_fwd, op_bwd)
    return op


def _rms(x):
    return x * lax.rsqrt(jnp.mean(x * x, axis=-1, keepdims=True) + EPS)


def _silu(x):
    return x * jax.nn.sigmoid(x)


def _fn_norm_mod(rows, gp, cp, ks):
    (x,), (nw, sc, sh) = rows, gp
    return ((_rms(x) * nw) * (1.0 + sc) + sh,), ()


def _fn_qk_norm_rope(rows, gp, cp, ks, out_scale=1.0):
    (t,), (w,), (pm,), (cos, sin) = rows, gp, cp, ks
    u = _rms(t) * w
    pu = jnp.dot(u, pm, precision=HIGHEST, preferred_element_type=f32)
    return ((u * cos + pu * sin) * out_scale,), ()


def make_head_rope(name, nh, out_scale, head_major):
    fn = functools.partial(_fn_qk_norm_rope, out_scale=out_scale)
    width = nh * HEAD_DIM

    def out_spec(tm):
        if head_major:
            return pl.BlockSpec((nh, tm, HEAD_DIM), lambda i: (0, i, 0))
        return pl.BlockSpec((tm, width), lambda i: (i, 0))

    def head_of(ref, h):
        return ref.at[h] if head_major else ref.at[:, h * HEAD_DIM:(h + 1) * HEAD_DIM]

    def one_head(t, w, pm, cos, sin):
        return fn([t], [w], [pm], [cos, sin])[0][0]

    def specs(tm):
        return [pl.BlockSpec((tm, width), lambda i: (i, 0)), pl.BlockSpec((1, HEAD_DIM), lambda i: (0, 0)),
                pl.BlockSpec((HEAD_DIM, HEAD_DIM), lambda i: (0, 0)), pl.BlockSpec((tm, HEAD_DIM), lambda i: (i, 0)),
                pl.BlockSpec((tm, HEAD_DIM), lambda i: (i, 0))]

    def forward(t, w, pm, cos, sin):
        s = t.shape[0]
        tm = _pick(s, (512, 256, 128))

        def body(t_ref, w_ref, pm_ref, cos_ref, sin_ref, o_ref):
            for h in range(nh):
                val = one_head(t_ref[:, h * HEAD_DIM:(h + 1) * HEAD_DIM], w_ref[...], pm_ref[...], cos_ref[...],
                               sin_ref[...])
                head_of(o_ref, h)[...] = val.astype(o_ref.dtype)

        return pl.pallas_call(
            body, name=name + "_fwd", grid=(s // tm,), in_specs=specs(tm),
            out_specs=out_spec(tm),
            out_shape=jax.ShapeDtypeStruct((nh, s, HEAD_DIM) if head_major else (s, width), bf16),
            compiler_params=_cparams(dimension_semantics=("arbitrary",)),
        )(t, w, pm, cos, sin)

    def backward(t, w, pm, cos, sin, dout):
        s = t.shape[0]
        tm = _pick(s, (512, 256, 128))

        def body(t_ref, w_ref, pm_ref, cos_ref, sin_ref, do_ref, dt_ref, dw_ref):
            @pl.when(pl.program_id(0) == 0)
            def _():
                dw_ref[...] = jnp.zeros_like(dw_ref)

            pm_v, cos_v, sin_v = pm_ref[...], cos_ref[...], sin_ref[...]
            dw = jnp.zeros((1, HEAD_DIM), f32)
            for h in range(nh):
                sl = slice(h * HEAD_DIM, (h + 1) * HEAD_DIM)
                _, vjp = jax.vjp(lambda a, b: one_head(a, b, pm_v, cos_v, sin_v), t_ref[:, sl], w_ref[...])
                dth, dwh = vjp(head_of(do_ref, h)[...].astype(f32))
                dt_ref[:, sl] = dth
                dw = dw + dwh
            dw_ref[...] += dw

        return pl.pallas_call(
            body, name=name + "_bwd", grid=(s // tm,),
            in_specs=specs(tm) + [out_spec(tm)],
            out_specs=[pl.BlockSpec((tm, width), lambda i: (i, 0)), pl.BlockSpec((1, HEAD_DIM), lambda i: (0, 0))],
            out_shape=[jax.ShapeDtypeStruct((s, width), f32), jax.ShapeDtypeStruct((1, HEAD_DIM), f32)],
            compiler_params=_cparams(dimension_semantics=("arbitrary",)),
        )(t, w, pm, cos, sin, dout)

    @jax.custom_vjp
    def op(t, w, pm, cos, sin):
        return forward(t, w, pm, cos, sin)

    def op_fwd(t, w, pm, cos, sin):
        return forward(t, w, pm, cos, sin), (t, w, pm, cos, sin)

    def op_bwd(res, dout):
        dt, dw = backward(*res, dout)
        return dt, dw, jnp.zeros_like(res[2]), jnp.zeros_like(res[3]), jnp.zeros_like(res[4])

    op.defvjp(op_fwd, op_bwd)
    return op


def _fn_softplus(rows, gp, cp, ks):
    (x,), (b,) = rows, gp
    v = x + b
    return (jnp.maximum(v, 0.0) + jnp.log(1.0 + jnp.exp(-jnp.abs(v))),), ()


def _fn_ssd_gate(rows, gp, cp, ks):
    (y, z), (nw,) = rows, gp
    return (_rms(y * _silu(z)) * nw,), ()


def _fn_merge(rows, gp, cp, ks):
    ao, so, ga, gs = rows
    return (jax.nn.sigmoid(ga) * ao + jax.nn.sigmoid(gs) * so,), ()


def _fn_res_norm(rows, gp, cp, ks):
    (x, mo), (g1, nw, sc, sh) = rows, gp
    x1 = x + g1 * mo
    return (x1, (_rms(x1) * nw) * (1.0 + sc) + sh), ()


def _fn_loss(rows, gp, cp, ks):
    (x1, ff), (g2,), (tgt,) = rows, gp, ks
    err = x1 + g2 * ff - tgt
    return (), (0.5 * jnp.sum(jnp.sum(err * err, axis=-1, keepdims=True), axis=0, keepdims=True) / D_MODEL,)


HALO = 8


def _conv_tiles(s, c):
    return _pick(s, (512, 256, 128)), _pick(c, (512, 256, 128))


def _halo_specs(tm, tc, s):
    nb = tm // HALO
    last = s // HALO - 1
    cur = pl.BlockSpec((tm, tc), lambda j, i: (i, j))
    prev = pl.BlockSpec((HALO, tc), lambda j, i: (jnp.maximum(i * nb - 1, 0), j))
    nxt = pl.BlockSpec((HALO, tc), lambda j, i: (jnp.minimum((i + 1) * nb, last), j))
    return cur, prev, nxt


def _fill_halo(buf, cur, prev, nxt, tm, i, n_i):
    buf[HALO:HALO + tm, :] = cur[...]
    buf[0:HALO, :] = jnp.where(i > 0, prev[...], 0.0)
    buf[HALO + tm:, :] = jnp.where(i < n_i - 1, nxt[...], 0.0)


def _conv_fwd(x, w, b):
    s, c = x.shape
    tm, tc = _conv_tiles(s, c)
    n_i = s // tm

    def body(cur, prev, nxt, w_ref, b_ref, o_ref, buf):
        i = pl.program_id(1)
        _fill_halo(buf, cur, prev, nxt, tm, i, n_i)
        pre = jnp.zeros((tm, tc), f32) + b_ref[...]
        for k in range(D_CONV):
            pre = pre + buf[HALO - 2 + k:HALO - 2 + k + tm, :] * w_ref[k:k + 1, :]
        o_ref[...] = _silu(pre)

    cur, prev, nxt = _halo_specs(tm, tc, s)
    return pl.pallas_call(
        body, name="conv_silu_fwd", grid=(c // tc, n_i),
        in_specs=[cur, prev, nxt, pl.BlockSpec((D_CONV, tc), lambda j, i: (0, j)),
                  pl.BlockSpec((1, tc), lambda j, i: (0, j))],
        out_specs=pl.BlockSpec((tm, tc), lambda j, i: (i, j)),
        out_shape=jax.ShapeDtypeStruct((s, c), f32),
        scratch_shapes=[pltpu.VMEM((tm + 2 * HALO, tc), f32)],
        compiler_params=_cparams(dimension_semantics=("parallel", "arbitrary")),
    )(x, x, x, w, b)


def _conv_bwd(x, w, b, dy):
    s, c = x.shape
    tm, tc = _conv_tiles(s, c)
    n_i = s // tm
    ext = tm + 8

    def body(cur, prev, nxt, dcur, dprev, dnxt, w_ref, b_ref, dx_ref, dw_ref, db_ref, xbuf, dbuf, pbuf):
        i = pl.program_id(1)
        _fill_halo(xbuf, cur, prev, nxt, tm, i, n_i)
        _fill_halo(dbuf, dcur, dprev, dnxt, tm, i, n_i)
        pre = jnp.zeros((ext, tc), f32) + b_ref[...]
        for k in range(D_CONV):
            pre = pre + xbuf[2 + k:2 + k + ext, :] * w_ref[k:k + 1, :]
        sg = jax.nn.sigmoid(pre)
        pbuf[...] = dbuf[4:4 + ext, :] * (sg * (1.0 + pre * (1.0 - sg)))
        dx = jnp.zeros((tm, tc), f32)
        for k in range(D_CONV):
            dx = dx + pbuf[6 - k:6 - k + tm, :] * w_ref[k:k + 1, :]
        dx_ref[...] = dx

        @pl.when(i == 0)
        def _():
            dw_ref[...] = jnp.zeros_like(dw_ref)
            db_ref[...] = jnp.zeros_like(db_ref)

        dpre = pbuf[4:4 + tm, :]
        db_ref[...] += jnp.sum(dpre, axis=0, keepdims=True)
        for k in range(D_CONV):
            dw_ref[k:k + 1, :] += jnp.sum(dpre * xbuf[HALO - 2 + k:HALO - 2 + k + tm, :], axis=0, keepdims=True)

    cur, prev, nxt = _halo_specs(tm, tc, s)
    return pl.pallas_call(
        body, name="conv_silu_bwd", grid=(c // tc, n_i),
        in_specs=[cur, prev, nxt, cur, prev, nxt, pl.BlockSpec((D_CONV, tc), lambda j, i: (0, j)),
                  pl.BlockSpec((1, tc), lambda j, i: (0, j))],
        out_specs=[pl.BlockSpec((tm, tc), lambda j, i: (i, j)), pl.BlockSpec((D_CONV, tc), lambda j, i: (0, j)),
                   pl.BlockSpec((1, tc), lambda j, i: (0, j))],
        out_shape=[jax.ShapeDtypeStruct((s, c), f32), jax.ShapeDtypeStruct((D_CONV, c), f32),
                   jax.ShapeDtypeStruct((1, c), f32)],
        scratch_shapes=[pltpu.VMEM((tm + 2 * HALO, tc), f32), pltpu.VMEM((tm + 2 * HALO, tc), f32),
                        pltpu.VMEM((ext, tc), f32)],
        compiler_params=_cparams(dimension_semantics=("parallel", "arbitrary")),
    )(x, x, x, dy, dy, dy, w, b)


@jax.custom_vjp
def conv_silu(x, w, b):
    return _conv_fwd(x, w, b)


def _conv_silu_fwd(x, w, b):
    return _conv_fwd(x, w, b), (x, w, b)


def _conv_silu_bwd(res, dy):
    return _conv_bwd(*res, dy)


conv_silu.defvjp(_conv_silu_fwd, _conv_silu_bwd)


ATT_SCALE = HEAD_DIM ** -0.5
Q_SCALE = ATT_SCALE * math.log2(math.e)
LN2 = math.log(2.0)
REP = N_Q_HEADS // N_KV_HEADS


HP = 2
assert REP % HP == 0


def _attn_fwd(q, k, v):
    s, dh = q.shape[0], HEAD_DIM
    hq = q.shape[1] // dh
    tq = _pick(s, (256, 128))

    v1 = jnp.concatenate([v, jnp.ones(v.shape[:2] + (1,), v.dtype), jnp.zeros(v.shape[:2] + (dh - 1,), v.dtype)],
                         axis=-1)

    def body(q_ref, k_ref, v_ref, o_ref, p_ref, linv_ref):
        for j in range(HP):
            sl = slice(j * dh, (j + 1) * dh)
            sc = lax.dot_general(q_ref[:, sl], k_ref[0], _DIMS["nt"], preferred_element_type=f32)
            m = jnp.max(sc, axis=-1, keepdims=True)
            p = jnp.exp2(sc - m).astype(bf16)
            p_ref[j] = p
            o1 = jnp.dot(p, v_ref[0], preferred_element_type=f32)
            linv = 1.0 / o1[:, dh:dh + 1]
            o_ref[:, sl] = (o1[:, :dh] * linv).astype(o_ref.dtype)
            linv_ref[j] = linv

    return pl.pallas_call(
        body, name="attn_fwd", grid=(hq // HP, s // tq),
        in_specs=[pl.BlockSpec((tq, HP * dh), lambda h, i: (i, h)),
                  pl.BlockSpec((1, s, dh), lambda h, i: (h * HP // REP, 0, 0)),
                  pl.BlockSpec((1, s, 2 * dh), lambda h, i: (h * HP // REP, 0, 0))],
        out_specs=[pl.BlockSpec((tq, HP * dh), lambda h, i: (i, h)),
                   pl.BlockSpec((HP, tq, s), lambda h, i: (h, i, 0)),
                   pl.BlockSpec((HP, tq, 1), lambda h, i: (h, i, 0))],
        out_shape=[jax.ShapeDtypeStruct((s, hq * dh), bf16), jax.ShapeDtypeStruct((hq, s, s), bf16),
                   jax.ShapeDtypeStruct((hq, s, 1), f32)],
        compiler_params=_cparams(dimension_semantics=("parallel", "arbitrary")),
    )(q, k, v1)


def _attn_bwd(p, do, o, q, k, v, linv):
    hq, s, _ = p.shape
    dh = HEAD_DIM
    tq = _pick(s, (256, 128))

    def body(p_ref, do_ref, o_ref, q_ref, k_ref, v_ref, linv_ref, dq_ref, dkt_ref, dvt_ref):
        @pl.when(pl.program_id(1) == 0)
        def _():
            dkt_ref[...] = jnp.zeros_like(dkt_ref)
            dvt_ref[...] = jnp.zeros_like(dvt_ref)

        for j in range(HP):
            sl = slice(j * dh, (j + 1) * dh)
            pp, doh, li = p_ref[j], do_ref[:, sl], linv_ref[j]
            do32 = doh.astype(f32)
            d = jnp.sum(do32 * o_ref[:, sl].astype(f32), axis=-1, keepdims=True)
            dp = lax.dot_general(doh, v_ref[0], _DIMS["nt"], preferred_element_type=f32)
            ds = (pp.astype(f32) * ((dp - d) * li)).astype(bf16)
            dq_ref[:, sl] = jnp.dot(ds, k_ref[0], preferred_element_type=f32) * LN2
            dvt_ref[j] += lax.dot_general((do32 * li).astype(bf16), pp, _DIMS["tn"], preferred_element_type=f32)
            dkt_ref[j] += lax.dot_general(q_ref[:, sl], ds, _DIMS["tn"], preferred_element_type=f32)

    def row():
        return pl.BlockSpec((tq, HP * dh), lambda h, i: (i, h))

    return pl.pallas_call(
        body, name="attn_bwd", grid=(hq // HP, s // tq),
        in_specs=[pl.BlockSpec((HP, tq, s), lambda h, i: (h, i, 0)), row(), row(), row(),
                  pl.BlockSpec((1, s, dh), lambda h, i: (h * HP // REP, 0, 0)),
                  pl.BlockSpec((1, s, dh), lambda h, i: (h * HP // REP, 0, 0)),
                  pl.BlockSpec((HP, tq, 1), lambda h, i: (h, i, 0))],
        out_specs=[row(), pl.BlockSpec((HP, dh, s), lambda h, i: (h, 0, 0)),
                   pl.BlockSpec((HP, dh, s), lambda h, i: (h, 0, 0))],
        out_shape=[jax.ShapeDtypeStruct((s, hq * dh), f32), jax.ShapeDtypeStruct((hq, dh, s), f32),
                   jax.ShapeDtypeStruct((hq, dh, s), f32)],
        compiler_params=_cparams(dimension_semantics=("parallel", "arbitrary")),
    )(p, do, o, q, k, v, linv)


@jax.custom_vjp
def attention(q, k, v):
    return _attn_fwd(q, k, v)[0]


def _attention_fwd(q, k, v):
    o, p, linv = _attn_fwd(q, k, v)
    return o, (q, k, v, o, p, linv)


def _attention_bwd(res, do):
    q, k, v, o, p, linv = res
    s = q.shape[0]
    dq, dkt, dvt = _attn_bwd(p, do.astype(bf16), o, q, k, v, linv)

    def per_kv_head(t):
        return jnp.swapaxes(t.reshape(N_KV_HEADS, REP, HEAD_DIM, s).sum(axis=1), 1, 2)

    return dq.astype(q.dtype), (per_kv_head(dkt) * LN2).astype(k.dtype), per_kv_head(dvt).astype(v.dtype)


attention.defvjp(_attention_fwd, _attention_bwd)


HPG = N_SSD_HEADS // N_SSD_GROUPS
GW = HPG * SSD_HEAD_DIM
NEG = -1e30
SPLIT_ROWS = 32


def _ssd_consts():
    k = np.arange(SPLIT_ROWS)[:, None]
    live = k < 3 * HPG
    sel_chunk = ((k % HPG) == (np.arange(HPG * CHUNK)[None, :] // CHUNK)) & live
    sel_head = ((k % HPG) == (np.arange(GW)[None, :] // SSD_HEAD_DIM)) & live
    return jnp.asarray(sel_chunk, bf16), jnp.asarray(sel_head, bf16)


def _split3(x):
    hi = x.astype(bf16).astype(f32)
    r1 = x - hi
    mid = r1.astype(bf16).astype(f32)
    lo = (r1 - mid).astype(bf16).astype(f32)
    return jnp.concatenate([hi, mid, lo, jnp.zeros_like(hi)], axis=0).astype(bf16)


def _tn(a, b):
    return lax.dot_general(a, b, _DIMS["tn"], preferred_element_type=f32)


def _nt(a, b):
    return lax.dot_general(a, b, _DIMS["nt"], preferred_element_type=f32)


def _nn(a, b):
    return jnp.dot(a, b, preferred_element_type=f32)


def _head_sum(sel8, x):
    hi = x.astype(bf16)
    lo = (x - hi.astype(f32)).astype(bf16)
    return _nt(sel8, hi) + _nt(sel8, lo)


def _ssd_masks(reverse):
    r = lax.broadcasted_iota(jnp.int32, (CHUNK, CHUNK), 0)
    c = lax.broadcasted_iota(jnp.int32, (CHUNK, CHUNK), 1)
    lower, upper = r >= c, r <= c
    return (upper, lower) if reverse else (lower, upper)


def _ssd_in_specs(cidx):
    return [pl.BlockSpec((CHUNK, D_INNER), lambda c: (cidx(c), 0)),
            pl.BlockSpec((CHUNK, GN), lambda c: (cidx(c), D_INNER // GN)),
            pl.BlockSpec((CHUNK, GN), lambda c: (cidx(c), D_INNER // GN + 1)),
            pl.BlockSpec((N_SSD_HEADS, CHUNK), lambda c: (0, cidx(c))),
            pl.BlockSpec((N_SSD_HEADS, 1), lambda c: (0, 0)),
            pl.BlockSpec((SPLIT_ROWS, HPG * CHUNK), lambda c: (0, 0)),
            pl.BlockSpec((SPLIT_ROWS, GW), lambda c: (0, 0))]


def _ssd_chunk_common(dtt_ref, a_ref, et_ref, mask_t):
    dtt = dtt_ref[...]
    et = jnp.dot(dtt * a_ref[...], mask_t.astype(f32), precision=HIGHEST, preferred_element_type=f32)
    et_ref[...] = et
    return dtt, et


def _ssd_group_common(g, dtt, et, selc_ref, selh_ref, xs_ref, b_ref, c_ref, last):
    gr = slice(g * HPG, (g + 1) * HPG)
    e3 = _split3(et[gr])
    col = _tn(e3, selc_ref[...])
    eb = _tn(e3, selh_ref[...])
    dtb = _tn(_split3(dtt[gr]), selh_ref[...])
    tbc = eb[last:last + 1, :]
    xs = xs_ref[:, g * GW:(g + 1) * GW]
    bg = b_ref[:, g * D_STATE:(g + 1) * D_STATE].astype(bf16)
    cg = c_ref[:, g * D_STATE:(g + 1) * D_STATE].astype(bf16)
    return col, eb, dtb, tbc, xs, bg, cg


def _ssd_fwd(xbc, dtt, a_col, reverse, y_prev=None, dexp=None):
    s = xbc.shape[0]
    nc = s // CHUNK
    cidx = (lambda c: nc - 1 - c) if reverse else (lambda c: c)
    last = 0 if reverse else CHUNK - 1
    selc, selh = _ssd_consts()
    final = y_prev is not None
    n_in = 9 if final else 7

    def body(*refs):
        xs_ref, b_ref, c_ref, dtt_ref, a_ref, selc_ref, selh_ref = refs[:7]
        y_ref, st_ref, ht_ref, et_ref = refs[n_in:]

        @pl.when(pl.program_id(0) == 0)
        def _():
            ht_ref[...] = jnp.zeros_like(ht_ref)

        mask, mask_t = _ssd_masks(reverse)
        dtt_v, et = _ssd_chunk_common(dtt_ref, a_ref, et_ref, mask_t)
        for g in range(N_SSD_GROUPS):
            col, eb, dtb, tbc, xs, bg, cg = _ssd_group_common(g, dtt_v, et, selc_ref, selh_ref, xs_ref, b_ref, c_ref,
                                                              last)
            xd = xs * dtb
            cb = _nt(cg, bg)
            ht = ht_ref[g]
            st_ref[0, g] = ht
            yoff = _nn(cg, ht.astype(bf16)) * jnp.exp(eb)
            for j in range(HPG):
                h = g * HPG + j
                hs = slice(j * SSD_HEAD_DIM, (j + 1) * SSD_HEAD_DIM)
                lam = jnp.exp(jnp.where(mask, col[:, j * CHUNK:(j + 1) * CHUNK] - et_ref[h:h + 1, :], NEG))
                yj = _nn((cb * lam).astype(bf16), xd[:, hs].astype(bf16)) + yoff[:, hs]
                cols = slice(g * GW + j * SSD_HEAD_DIM, g * GW + (j + 1) * SSD_HEAD_DIM)
                if final:
                    yj = yj + refs[7][:, cols] + xs[:, hs] * refs[8][:, cols]
                y_ref[:, cols] = yj
            ht_ref[g] = jnp.exp(tbc) * ht + _tn(bg, (xd * jnp.exp(tbc - eb)).astype(bf16))

    y_spec = pl.BlockSpec((CHUNK, D_INNER), lambda c: (cidx(c), 0))
    extra_specs = [y_spec, pl.BlockSpec((1, D_INNER), lambda c: (0, 0))] if final else []
    return pl.pallas_call(
        body, name="ssd_fwd_rev" if reverse else "ssd_fwd", grid=(nc,),
        in_specs=_ssd_in_specs(cidx) + extra_specs,
        out_specs=[y_spec, pl.BlockSpec((1, N_SSD_GROUPS, D_STATE, GW), lambda c: (cidx(c), 0, 0, 0))],
        out_shape=[jax.ShapeDtypeStruct((s, D_INNER), f32),
                   jax.ShapeDtypeStruct((nc, N_SSD_GROUPS, D_STATE, GW), f32)],
        scratch_shapes=[pltpu.VMEM((N_SSD_GROUPS, D_STATE, GW), f32), pltpu.VMEM((N_SSD_HEADS, CHUNK), f32)],
        compiler_params=_cparams(dimension_semantics=("arbitrary",)),
    )(xbc, xbc, xbc, dtt, a_col, selc, selh, *((y_prev, dexp) if final else ()))


def _ssd_bwd(xbc, dtt, a_col, states, dy, reverse, dxbc_prev=None, dexp=None):
    s = xbc.shape[0]
    nc = s // CHUNK
    cidx = (lambda c: c) if reverse else (lambda c: nc - 1 - c)
    last = 0 if reverse else CHUNK - 1
    selc, selh = _ssd_consts()
    final = dxbc_prev is not None
    n_in = 11 if final else 9
    n_out = 4 if final else 3

    def body(*refs):
        xs_ref, b_ref, c_ref, dtt_ref, a_ref, selc_ref, selh_ref, st_ref, dy_ref = refs[:9]
        dxbc_ref, ddtt_ref, da_ref = refs[n_in:n_in + 3]
        dh_ref, et_ref, det_ref, det2_ref, ddt_ref, q_ref = refs[n_in + n_out:]
        if final:
            prev_ref, dexp_ref, ddexp_ref = refs[9], refs[10], refs[n_in + 3]

        @pl.when(pl.program_id(0) == 0)
        def _():
            dh_ref[...] = jnp.zeros_like(dh_ref)
            da_ref[...] = jnp.zeros_like(da_ref)
            if final:
                ddexp_ref[...] = jnp.zeros_like(ddexp_ref)

        mask, mask_t = _ssd_masks(reverse)
        dtt_v, et = _ssd_chunk_common(dtt_ref, a_ref, et_ref, mask_t)
        sel8 = selh_ref[0:HPG, :]
        is_last = lax.broadcasted_iota(jnp.int32, (CHUNK, GW), 0) == last
        for g in range(N_SSD_GROUPS):
            col, eb, dtb, tbc, xs, bg, cg = _ssd_group_common(g, dtt_v, et, selc_ref, selh_ref, xs_ref, b_ref, c_ref,
                                                              last)
            xd = xs * dtb
            cb = _nt(cg, bg)
            cbt = _nt(bg, cg)
            exp_t = jnp.exp(tbc)
            dfac = jnp.exp(tbc - eb)
            ht = st_ref[0, g]
            dhn = dh_ref[g]
            ht16, dhn16 = ht.astype(bf16), dhn.astype(bf16)
            dy = dy_ref[:, g * GW:(g + 1) * GW]
            dye = dy * jnp.exp(eb)
            dye16 = dye.astype(bf16)
            dc = _nt(dye16, ht16)
            dh_ref[g] = exp_t * dhn + _tn(cg, dye16)
            deb = dye * _nn(cg, ht16)
            xdd = xd * dfac
            dxdd = _nn(bg, dhn16)
            db = _nt(xdd.astype(bf16), dhn16)
            dxd_state = dxdd * dfac
            ddf = dxdd * xdd
            dtbc = jnp.sum(ddf, axis=0, keepdims=True) + exp_t * jnp.sum(dhn * ht, axis=0, keepdims=True)
            deb = deb - ddf + jnp.where(is_last, dtbc, 0.0)
            dcb = jnp.zeros((CHUNK, CHUNK), f32)
            dcbt = jnp.zeros((CHUNK, CHUNK), f32)
            for j in range(HPG):
                h = g * HPG + j
                hs = slice(j * SSD_HEAD_DIM, (j + 1) * SSD_HEAD_DIM)
                colj = col[:, j * CHUNK:(j + 1) * CHUNK]
                row = et_ref[h:h + 1, :]
                lam = jnp.exp(jnp.where(mask, colj - row, NEG))
                lam_t = jnp.exp(jnp.where(mask_t, row - colj, NEG))
                xdj, dyj = xd[:, hs].astype(bf16), dy[:, hs].astype(bf16)
                t1 = _nt(dyj, xdj) * lam
                t2 = _nt(xdj, dyj) * lam_t
                dcb, dcbt = dcb + t1, dcbt + t2
                det_ref[h:h + 1, :] = -jnp.sum(t1 * cb - t2 * cbt, axis=0, keepdims=True)
                dxdj = _nn((cbt * lam_t).astype(bf16), dyj) + dxd_state[:, hs]
                cols = slice(g * GW + j * SSD_HEAD_DIM, g * GW + (j + 1) * SSD_HEAD_DIM)
                dxs = dxdj * dtb[:, hs]
                if final:
                    dxs = dxs + prev_ref[:, cols] + dy[:, hs] * dexp_ref[:, cols]
                dxbc_ref[:, cols] = dxs
                q_ref[:, hs] = dxdj * xs[:, hs]
            b_cols = slice(D_INNER + g * D_STATE, D_INNER + (g + 1) * D_STATE)
            c_cols = slice(D_INNER + GN + g * D_STATE, D_INNER + GN + (g + 1) * D_STATE)
            db = db + _nn(dcbt.astype(bf16), cg)
            dc = dc + _nn(dcb.astype(bf16), bg)
            if final:
                db, dc = db + prev_ref[:, b_cols], dc + prev_ref[:, c_cols]
                ddexp_ref[:, g * GW:(g + 1) * GW] += jnp.sum(dy * xs, axis=0, keepdims=True)
            dxbc_ref[:, b_cols] = db
            dxbc_ref[:, c_cols] = dc
            det2_ref[g * HPG:(g + 1) * HPG, :] = _head_sum(sel8, deb)
            ddt_ref[g * HPG:(g + 1) * HPG, :] = _head_sum(sel8, q_ref[...])
        dat = jnp.dot(det_ref[...] + det2_ref[...], mask.astype(f32), precision=HIGHEST, preferred_element_type=f32)
        ddtt_ref[...] = ddt_ref[...] + dat * a_ref[...]
        da_ref[...] += jnp.sum(dat * dtt_v, axis=1, keepdims=True)

    in_specs = _ssd_in_specs(cidx) + [
        pl.BlockSpec((1, N_SSD_GROUPS, D_STATE, GW), lambda c: (cidx(c), 0, 0, 0)),
        pl.BlockSpec((CHUNK, D_INNER), lambda c: (cidx(c), 0))]
    hl = pltpu.VMEM((N_SSD_HEADS, CHUNK), f32)
    dxbc_spec = pl.BlockSpec((CHUNK, CONV_DIM), lambda c: (cidx(c), 0))
    dexp_spec = pl.BlockSpec((1, D_INNER), lambda c: (0, 0))
    return pl.pallas_call(
        body, name="ssd_bwd_rev" if reverse else "ssd_bwd", grid=(nc,),
        in_specs=in_specs + ([dxbc_spec, dexp_spec] if final else []),
        out_specs=[dxbc_spec, pl.BlockSpec((N_SSD_HEADS, CHUNK), lambda c: (0, cidx(c))),
                   pl.BlockSpec((N_SSD_HEADS, 1), lambda c: (0, 0))] + ([dexp_spec] if final else []),
        out_shape=[jax.ShapeDtypeStruct((s, CONV_DIM), f32), jax.ShapeDtypeStruct((N_SSD_HEADS, s), f32),
                   jax.ShapeDtypeStruct((N_SSD_HEADS, 1), f32)]
        + ([jax.ShapeDtypeStruct((1, D_INNER), f32)] if final else []),
        scratch_shapes=[pltpu.VMEM((N_SSD_GROUPS, D_STATE, GW), f32), hl, hl, hl, hl, pltpu.VMEM((CHUNK, GW), f32)],
        compiler_params=_cparams(dimension_semantics=("arbitrary",)),
    )(xbc, xbc, xbc, dtt, a_col, selc, selh, states, dy, *((dxbc_prev, dexp) if final else ()))


@jax.custom_vjp
def ssd_bidir(xbc, dtt, a_col, dexp):
    y_f, _ = _ssd_fwd(xbc, dtt[:N_SSD_HEADS], a_col[:N_SSD_HEADS], False)
    return _ssd_fwd(xbc, dtt[N_SSD_HEADS:], a_col[N_SSD_HEADS:], True, y_prev=y_f, dexp=dexp)[0]


def _ssd_bidir_fwd(xbc, dtt, a_col, dexp):
    y_f, st_f = _ssd_fwd(xbc, dtt[:N_SSD_HEADS], a_col[:N_SSD_HEADS], False)
    y, st_b = _ssd_fwd(xbc, dtt[N_SSD_HEADS:], a_col[N_SSD_HEADS:], True, y_prev=y_f, dexp=dexp)
    return y, (xbc, dtt, a_col, dexp, st_f, st_b)


def _ssd_bidir_bwd(res, dy):
    xbc, dtt, a_col, dexp, st_f, st_b = res
    dxbc_f, ddtt_f, da_f = _ssd_bwd(xbc, dtt[:N_SSD_HEADS], a_col[:N_SSD_HEADS], st_f, dy, False)
    dxbc, ddtt_b, da_b, ddexp = _ssd_bwd(xbc, dtt[N_SSD_HEADS:], a_col[N_SSD_HEADS:], st_b, dy, True,
                                         dxbc_prev=dxbc_f, dexp=dexp)
    return dxbc, jnp.concatenate([ddtt_f, ddtt_b], axis=0), jnp.concatenate([da_f, da_b], axis=0), ddexp


ssd_bidir.defvjp(_ssd_bidir_fwd, _ssd_bidir_bwd)


W_NAMES = PROJ_NAMES + ("attn_out", "ssd_out", "o", "mlp1", "mlp2")


def _rope_tables(s):
    rows = s // GRID_W
    pos_row = jnp.repeat(jnp.arange(rows, dtype=jnp.int32), GRID_W).astype(f32)
    pos_col = jnp.tile(jnp.arange(GRID_W, dtype=jnp.int32), rows).astype(f32)
    axis_dim = HEAD_DIM // 2
    inv_freq = ROPE_THETA ** (-jnp.arange(0, axis_dim, 2, dtype=f32) / axis_dim)
    ang_r = pos_row[:, None] * inv_freq[None, :]
    ang_c = pos_col[:, None] * inv_freq[None, :]
    cos = jnp.concatenate([jnp.cos(ang_r), jnp.cos(ang_r), jnp.cos(ang_c), jnp.cos(ang_c)], axis=-1)
    sin = jnp.concatenate([jnp.sin(ang_r), jnp.sin(ang_r), jnp.sin(ang_c), jnp.sin(ang_c)], axis=-1)
    return cos, sin


def _rope_perm():
    p = np.zeros((HEAD_DIM, HEAD_DIM), np.float32)
    for j in range(HEAD_DIM):
        if (j % 32) < 16:
            p[j + 16, j] = -1.0
        else:
            p[j - 16, j] = 1.0
    return jnp.asarray(p)


def local_loss(x, mod, small, wgrads, wfull, target):
    s = x.shape[0]
    lin = {n: make_linear("lin_" + n) for n in W_NAMES if not n.startswith("mlp")}
    shift1, scale1, gate1, shift2, scale2, gate2 = [mod[i] for i in range(6)]

    norm_mod = make_rowwise("norm_mod", _fn_norm_mod, [(D_MODEL, bf16)])
    (h,), _ = norm_mod((x,), (small["norm1_w"], scale1, shift1), (), ())

    proj = {n: lin[n](h, wfull[n], wgrads[n]) for n in PROJ_NAMES}

    cos, sin = _rope_tables(s)
    pm = _rope_perm()

    def heads(t, nh):
        return t.reshape(s, nh, HEAD_DIM).transpose(1, 0, 2)

    qr = make_head_rope("q_norm_rope", N_Q_HEADS, Q_SCALE, False)(proj["q"], small["q_norm_w"], pm, cos, sin)
    kr = make_head_rope("k_norm_rope", N_KV_HEADS, 1.0, True)(proj["k"], small["k_norm_w"], pm, cos, sin)
    vh = heads(proj["v"], N_KV_HEADS).astype(bf16)
    att = attention(qr, kr, vh)
    ao = lin["attn_out"](att, wfull["attn_out"], wgrads["attn_out"])

    xbc = conv_silu(proj["xbc"], small["conv_w"], small["conv_b"])
    softplus = make_rowwise("dt_softplus", _fn_softplus, [(2 * N_SSD_HEADS, f32)])
    (dt,), _ = softplus((proj["dt"][:, :2 * N_SSD_HEADS],), (small["dt_bias"].reshape(1, 2 * N_SSD_HEADS),), (), ())
    a_neg = -jnp.exp(small["A_log"])
    dexp = jnp.repeat(small["ssd_D"].reshape(N_SSD_HEADS), SSD_HEAD_DIM).reshape(1, D_INNER)
    y = ssd_bidir(xbc, dt.T, a_neg.reshape(2 * N_SSD_HEADS, 1), dexp)
    ssd_gate = make_rowwise("ssd_gate", _fn_ssd_gate, [(D_INNER, bf16)], tm_pref=128)
    (ssd_out,), _ = ssd_gate((y, proj["z"]), (small["ssd_norm_w"],), (), ())
    so = lin["ssd_out"](ssd_out, wfull["ssd_out"], wgrads["ssd_out"])

    merge = make_rowwise("merge", _fn_merge, [(D_MODEL, bf16)])
    (merged,), _ = merge((ao, so, proj["ga"], proj["gs"]), (), (), ())
    mo = lin["o"](merged, wfull["o"], wgrads["o"])

    res_norm = make_rowwise("res_norm", _fn_res_norm, [(D_MODEL, f32), (D_MODEL, bf16)])
    (x1, h2), _ = res_norm((x, mo), (gate1, small["norm2_w"], scale2, shift2), (), ())
    ff = mlp(h2, wfull["mlp1"], wgrads["mlp1"], wfull["mlp2"], wgrads["mlp2"])
    loss_op = make_rowwise("loss", _fn_loss, [], [(1, 1)])
    _, (loss,) = loss_op((x1, ff), (gate2,), (), (target,))
    return loss[0, 0]


_BC1 = 1.0 - ADAM_B1 ** ADAM_STEP
_BC2 = 1.0 - ADAM_B2 ** ADAM_STEP


def _adamw(w, g, m, v):
    m = ADAM_B1 * m + (1.0 - ADAM_B1) * g
    v = ADAM_B2 * v + (1.0 - ADAM_B2) * (g * g)
    delta = -ADAM_LR * ((m / _BC1) / (jnp.sqrt(v / _BC2) + ADAM_EPS) + ADAM_WD * w)
    return delta, m, v


def _ada_fwd(c_all, w, b):
    n = w.shape[1]

    def body(c_ref, w_ref, b_ref, o_ref):
        o_ref[...] = jnp.dot(_silu(c_ref[...]), w_ref[...], precision=HIGHEST, preferred_element_type=f32) + b_ref[...]

    return pl.pallas_call(body, name="ada_fwd", out_shape=jax.ShapeDtypeStruct((N_DEV, n), f32),
                          compiler_params=_cparams())(c_all, w, b)


def _ada_bwd_adamw(c_all, dmod, w, m, v):
    d, n = w.shape
    tr = _pick(d, (256, 128))

    def body(c_ref, dm_ref, w_ref, m_ref, v_ref, g_ref, dl_ref, mo_ref, vo_ref):
        g = lax.dot_general(_silu(c_ref[...]), dm_ref[...], _DIMS["tn"], precision=HIGHEST,
                            preferred_element_type=f32)
        g_ref[...] = g
        dl_ref[...], mo_ref[...], vo_ref[...] = _adamw(w_ref[...], g, m_ref[...], v_ref[...])

    blk = pl.BlockSpec((tr, n), lambda i: (i, 0))
    return pl.pallas_call(
        body, name="ada_bwd_adamw", grid=(d // tr,),
        in_specs=[pl.BlockSpec((N_DEV, tr), lambda i: (0, i)), pl.BlockSpec((N_DEV, n), lambda i: (0, 0)), blk, blk, blk],
        out_specs=[blk] * 4, out_shape=[jax.ShapeDtypeStruct((d, n), f32)] * 4,
        compiler_params=_cparams(dimension_semantics=("parallel",)),
    )(c_all, dmod, w, m, v)


def _sum_over_mesh(g):
    def body(g_ref, o_ref):
        acc = g_ref[0]
        for d in range(1, N_DEV):
            acc = acc + g_ref[d]
        o_ref[...] = acc

    return pl.pallas_call(body, name="sum_small", out_shape=jax.ShapeDtypeStruct(g.shape[1:], f32),
                          compiler_params=_cparams())(g)


def _adamw_small(w, g, m, v):
    def body(w_ref, g_ref, m_ref, v_ref, dl_ref, mo_ref, vo_ref):
        dl_ref[...], mo_ref[...], vo_ref[...] = _adamw(w_ref[...], g_ref[...], m_ref[...], v_ref[...])

    return pl.pallas_call(body, name="adamw_small", out_shape=[jax.ShapeDtypeStruct(w.shape, f32)] * 3,
                          compiler_params=_cparams())(w, g, m, v)


def _sum_adamw(recv, w, m, v, name):
    _, r, c = recv.shape
    tr = _pick(r, (256, 128, 64, 16))

    def body(g_ref, w_ref, m_ref, v_ref, go_ref, dl_ref, mo_ref, vo_ref):
        g = g_ref[0].astype(f32)
        for d in range(1, N_DEV):
            g = g + g_ref[d].astype(f32)
        go_ref[...] = g
        dl_ref[...], mo_ref[...], vo_ref[...] = _adamw(w_ref[...], g, m_ref[...], v_ref[...])

    blk = pl.BlockSpec((tr, c), lambda i: (i, 0))
    return pl.pallas_call(
        body, name=name, grid=(r // tr,),
        in_specs=[pl.BlockSpec((N_DEV, tr, c), lambda i: (0, i, 0)), blk, blk, blk],
        out_specs=[blk] * 4, out_shape=[jax.ShapeDtypeStruct((r, c), f32)] * 4,
        compiler_params=_cparams(dimension_semantics=("parallel",)),
    )(recv, w, m, v)


def _pack_small(arrs):
    parts = []
    for a in arrs:
        flat = a.reshape(-1).astype(f32)
        parts.append(jnp.pad(flat, (0, (-flat.shape[0]) % LANE)))
    flat = jnp.concatenate(parts)
    flat = jnp.pad(flat, (0, (-flat.shape[0]) % (8 * LANE)))
    return flat.reshape(-1, LANE)


def _unpack_small(packed, shapes):
    flat = packed.reshape(-1)
    out, off = [], 0
    for shp in shapes:
        n = int(np.prod(shp))
        out.append(flat[off:off + n].reshape(shp))
        off += n + (-n) % LANE
    return out


BIG = ("w_attn_out", "w_ssd_out", "w_o", "w_mlp1", "w_mlp2")
BIG_ROWS = (N_Q_HEADS * HEAD_DIM // N_DEV, D_INNER // N_DEV, D_MODEL // N_DEV,
            D_MODEL * (D_FF // N_DEV) // PACK_COLS, D_FF // N_DEV)
N_IN_SHARD = D_IN_PROJ // N_DEV
assert sum(BIG_ROWS) % 16 == 0


def _pack_big(shards, dtype):
    return jnp.concatenate([s.astype(dtype).reshape(-1, PACK_COLS) for s in shards], axis=0)


def _unpack_big(packed, shapes):
    out, off = [], 0
    for rows, shp in zip(BIG_ROWS, shapes):
        out.append(packed[off:off + rows].reshape(shp))
        off += rows
    return out


def _split_gathered(g_in, g):
    offs = np.cumsum((0,) + BIG_ROWS)
    sl = [g[:, offs[i]:offs[i + 1]] for i in range(len(BIG))]
    w_in = g_in.transpose(1, 0, 2).reshape(D_MODEL, D_IN_PROJ)
    w = {}
    off = 0
    for name, size in zip(PROJ_NAMES, PROJ_SIZES):
        w[name] = w_in[:, off:off + size]
        off += size
    w["dt"] = jnp.pad(w["dt"], ((0, 0), (0, DT_PAD - 2 * N_SSD_HEADS)))
    w["attn_out"] = sl[0].reshape(N_Q_HEADS * HEAD_DIM, D_MODEL)
    w["ssd_out"] = sl[1].reshape(D_INNER, D_MODEL)
    w["o"] = sl[2].reshape(D_MODEL, D_MODEL)
    w["mlp1"] = sl[3].reshape(N_DEV, D_MODEL, D_FF // N_DEV).transpose(1, 0, 2).reshape(D_MODEL, D_FF)
    w["mlp2"] = sl[4].reshape(D_FF, D_MODEL)
    return w


def _pack_full_grads(gw):
    gw = {n: g.astype(bf16) for n, g in gw.items()}
    gw["dt"] = gw["dt"][:, :2 * N_SSD_HEADS]
    g_in = jnp.concatenate([gw[n] for n in PROJ_NAMES], axis=1)
    parts = [
        gw["attn_out"].reshape(N_DEV, -1, PACK_COLS),
        gw["ssd_out"].reshape(N_DEV, -1, PACK_COLS),
        gw["o"].reshape(N_DEV, -1, PACK_COLS),
        gw["mlp1"].reshape(D_MODEL, N_DEV, D_FF // N_DEV).transpose(1, 0, 2).reshape(N_DEV, -1, PACK_COLS),
        gw["mlp2"].reshape(N_DEV, -1, PACK_COLS),
    ]
    return g_in.reshape(D_MODEL, N_DEV, N_IN_SHARD).transpose(1, 0, 2), jnp.concatenate(parts, axis=1)


SMALL = ("norm1_w", "norm2_w", "q_norm_w", "k_norm_w", "conv_w", "conv_b", "A_log", "dt_bias", "ssd_D", "ssd_norm_w")


def kernel(x, c, w_ada, b_ada, norm1_w, norm2_w, w_in, q_norm_w, k_norm_w, conv_w, conv_b, A_log, dt_bias, ssd_D, ssd_norm_w, w_attn_out, w_ssd_out, w_o, w_mlp1, w_mlp2, loss_target, m_w_ada, m_b_ada, m_norm1_w, m_norm2_w, m_w_in, m_q_norm_w, m_k_norm_w, m_conv_w, m_conv_b, m_A_log, m_dt_bias, m_ssd_D, m_ssd_norm_w, m_w_attn_out, m_w_ssd_out, m_w_o, m_w_mlp1, m_w_mlp2, v_w_ada, v_b_ada, v_norm1_w, v_norm2_w, v_w_in, v_q_norm_w, v_k_norm_w, v_conv_w, v_conv_b, v_A_log, v_dt_bias, v_ssd_D, v_ssd_norm_w, v_w_attn_out, v_w_ssd_out, v_w_o, v_w_mlp1, v_w_mlp2):
    args = dict(locals())
    me = _my_index()
    n_ada = 6 * D_MODEL // N_DEV
    n_cw = CONV_DIM // N_DEV

    blk = jnp.zeros((8, D_MODEL), f32)
    blk = blk.at[0:1, :].set(c)
    blk = blk.at[1:1 + D_CONV, :n_cw].set(conv_w[0])
    g0 = _all_gather(blk, "gather_c_convw", in_vmem=True)
    c_all = g0[:, 0, :]
    conv_w_full = g0[:, 1:1 + D_CONV, :n_cw].transpose(1, 0, 2).reshape(D_CONV, CONV_DIM)

    b_shard = lax.dynamic_slice(b_ada, (0, me * n_ada), (1, n_ada))
    mod_cols = _ada_fwd(c_all, w_ada[0], b_shard)
    g1 = _all_gather(mod_cols, "gather_mod", in_vmem=True)
    mod_mine = lax.dynamic_index_in_dim(g1, me, axis=1, keepdims=False)
    mod = mod_mine.reshape(6, 1, D_MODEL)

    big_shapes = [args[n].shape[1:] for n in BIG]
    packed16 = _pack_big([args[n][0] for n in BIG], bf16)
    wfull = _split_gathered(_all_gather(w_in[0].astype(bf16), "gather_w_in", in_vmem=False),
                            _all_gather(packed16, "gather_weights", in_vmem=False))
    wgrads = {n: jnp.zeros(wfull[n].shape, f32) for n in W_NAMES}

    small = {"norm1_w": norm1_w, "norm2_w": norm2_w, "q_norm_w": q_norm_w, "k_norm_w": k_norm_w,
             "conv_w": conv_w_full, "conv_b": conv_b, "A_log": A_log[0], "dt_bias": dt_bias[0], "ssd_D": ssd_D,
             "ssd_norm_w": ssd_norm_w}

    loss, (gx, gmod, gsmall, gw) = jax.value_and_grad(local_loss, argnums=(0, 1, 2, 3))(
        x[0], mod, small, wgrads, wfull, loss_target[0])

    small_list = [gmod, gsmall["norm1_w"], gsmall["norm2_w"], gsmall["q_norm_w"], gsmall["k_norm_w"], gsmall["conv_w"],
                  gsmall["conv_b"], gsmall["A_log"], gsmall["dt_bias"], gsmall["ssd_D"], gsmall["ssd_norm_w"],
                  loss.reshape(1)]
    small_shapes = [a.shape for a in small_list]
    g2 = _all_gather(_pack_small(small_list), "gather_small_grads", in_vmem=True)
    summed = _unpack_small(_sum_over_mesh(g2), small_shapes)
    loss_total = summed[-1][0]
    g_b_ada = summed[0].reshape(1, 6 * D_MODEL)
    g_small = dict(zip(SMALL, summed[1:-1]))
    g_conv_w = lax.dynamic_slice(g_small["conv_w"], (0, me * n_cw), (D_CONV, n_cw))

    dmod_all = g2[:, :6 * D_MODEL // LANE, :].reshape(N_DEV, 6 * D_MODEL)
    dmod_shard = lax.dynamic_slice(dmod_all, (0, me * n_ada), (N_DEV, n_ada))
    ada = _ada_bwd_adamw(c_all, dmod_shard, w_ada[0], m_w_ada[0], v_w_ada[0])

    small_grads = {"b_ada": g_b_ada, "norm1_w": g_small["norm1_w"], "norm2_w": g_small["norm2_w"],
                   "q_norm_w": g_small["q_norm_w"], "k_norm_w": g_small["k_norm_w"], "conv_w": g_conv_w[None],
                   "conv_b": g_small["conv_b"], "A_log": g_small["A_log"][None], "dt_bias": g_small["dt_bias"][None],
                   "ssd_D": g_small["ssd_D"], "ssd_norm_w": g_small["ssd_norm_w"]}
    sm_names = list(small_grads)
    sm_shapes = [args[n].shape for n in sm_names]
    sm = _adamw_small(_pack_small([args[n] for n in sm_names]), _pack_small([small_grads[n] for n in sm_names]),
                      _pack_small([args["m_" + n] for n in sm_names]), _pack_small([args["v_" + n] for n in sm_names]))
    sm_delta, sm_m, sm_v = [dict(zip(sm_names, _unpack_small(t, sm_shapes))) for t in sm]
    small_grads = {n: small_grads[n].reshape(args[n].shape) for n in sm_names}

    g_in, g_rest = _pack_full_grads(gw)
    w_in_out = _sum_adamw(_scatter_blocks(g_in, "scatter_grads_w_in"), w_in[0], m_w_in[0], v_w_in[0], "sum_adamw_w_in")
    big = _sum_adamw(_scatter_blocks(g_rest, "scatter_grads"), _pack_big([args[n][0] for n in BIG], f32),
                     _pack_big([args["m_" + n][0] for n in BIG], f32),
                     _pack_big([args["v_" + n][0] for n in BIG], f32), "sum_adamw")
    big_g, big_delta, big_m, big_v = [dict(zip(BIG, [t[None] for t in _unpack_big(p, big_shapes)])) for p in big]
    big_g["w_in"], big_delta["w_in"], big_m["w_in"], big_v["w_in"] = [t[None] for t in w_in_out]

    names = ("w_ada", "b_ada", "norm1_w", "norm2_w", "w_in", "q_norm_w", "k_norm_w", "conv_w", "conv_b", "A_log",
             "dt_bias", "ssd_D", "ssd_norm_w", "w_attn_out", "w_ssd_out", "w_o", "w_mlp1", "w_mlp2")
    grads, deltas, new_m, new_v = {}, {}, {}, {}
    for n in names:
        if n == "w_ada":
            grads[n], deltas[n], new_m[n], new_v[n] = [t[None] for t in ada]
        elif n in big_g:
            grads[n], deltas[n], new_m[n], new_v[n] = big_g[n], big_delta[n], big_m[n], big_v[n]
        else:
            grads[n], deltas[n], new_m[n], new_v[n] = small_grads[n], sm_delta[n], sm_m[n], sm_v[n]
    return (loss_total, gx[None], *[grads[n] for n in names], *[deltas[n] for n in names],
            *[new_m[n] for n in names], *[new_v[n] for n in names])
```

```python
import functools
import math

import jax
import jax.numpy as jnp
import numpy as np
from jax import lax
from jax.experimental import pallas as pl
from jax.experimental.pallas import tpu as pltpu

f32 = jnp.float32
bf16 = jnp.bfloat16
HIGHEST = lax.Precision.HIGHEST
MESH = pl.DeviceIdType.MESH

N_DEV = 8
D_MODEL = 1024
GRID_W = 64
N_Q_HEADS = 16
N_KV_HEADS = 4
HEAD_DIM = 64
ROPE_THETA = 10000.0
D_INNER = 2048
SSD_HEAD_DIM = 64
N_SSD_HEADS = 32
N_SSD_GROUPS = 4
D_STATE = 128
D_CONV = 5
CHUNK = 128
D_FF = 4096
EPS = 1e-6
CONV_DIM = D_INNER + 2 * N_SSD_GROUPS * D_STATE
GN = N_SSD_GROUPS * D_STATE
PROJ_NAMES = ("q", "k", "v", "xbc", "z", "dt", "ga", "gs")
PROJ_SIZES = (N_Q_HEADS * HEAD_DIM, N_KV_HEADS * HEAD_DIM, N_KV_HEADS * HEAD_DIM, CONV_DIM, D_INNER,
              2 * N_SSD_HEADS, D_MODEL, D_MODEL)
D_IN_PROJ = sum(PROJ_SIZES)
DT_PAD = 128

ADAM_LR, ADAM_B1, ADAM_B2, ADAM_EPS, ADAM_WD, ADAM_STEP = 0.001, 0.9, 0.999, 1e-08, 0.01, 10

V7X_VMEM_LIMIT = 56 * 1024 * 1024
LANE = 128
PACK_COLS = 1024


def _cparams(**kw):
    return pltpu.CompilerParams(vmem_limit_bytes=V7X_VMEM_LIMIT, **kw)


def _pick(dim, prefs):
    for p in prefs:
        if dim % p == 0:
            return p
    return dim


def _my_index():
    return 4 * lax.axis_index("x") + 2 * lax.axis_index("y") + lax.axis_index("c")


COMM_SEMS = [pltpu.SemaphoreType.DMA((7,)), pltpu.SemaphoreType.DMA((7,)), pltpu.SemaphoreType.DMA]


def _gather_phases(x_ref, out_ref, send_sems, recv_sems, local_sem):
    x, y, cc = lax.axis_index("x"), lax.axis_index("y"), lax.axis_index("c")
    me, sibling = (x, y, cc), (x, y, 1 - cc)
    chips = [(1 - x, y), (x, 1 - y), (1 - x, 1 - y)]

    def slot(px, py, pc):
        return out_ref.at[4 * px + 2 * py + pc]

    def copy(k, blk, to, src=None):
        return pltpu.make_async_remote_copy(
            src_ref=slot(*blk) if src is None else src, dst_ref=slot(*blk),
            send_sem=send_sems.at[k], recv_sem=recv_sems.at[k], device_id=to, device_id_type=MESH)

    mine = pltpu.make_async_copy(x_ref, slot(*me), local_sem)
    first = [copy(0, me, sibling, src=x_ref)]
    first += [copy(1 + j, me, (*chip, cc), src=x_ref) for j, chip in enumerate(chips)]
    passed = [copy(4 + j, (*chip, cc), sibling) for j, chip in enumerate(chips)]

    def start():
        mine.start()
        for cp in first:
            cp.start()

    def finish():
        for j, chip in enumerate(chips):
            copy(1 + j, (*chip, cc), me).wait_recv()
            passed[j].start()
        copy(0, sibling, me).wait_recv()
        for j, chip in enumerate(chips):
            copy(4 + j, (*chip, 1 - cc), me).wait_recv()
        for cp in first + passed:
            cp.wait_send()
        mine.wait()

    return start, finish


def _scatter_phases(g_ref, out_ref, send_sems, recv_sems, local_sem):
    x, y, cc = lax.axis_index("x"), lax.axis_index("y"), lax.axis_index("c")
    me = 4 * x + 2 * y + cc
    mine = pltpu.make_async_copy(g_ref.at[me], out_ref.at[me], local_sem)

    def copy(k):
        fx, fy, fc = (k >> 2) & 1, (k >> 1) & 1, k & 1
        px = x + fx - 2 * x * fx
        py = y + fy - 2 * y * fy
        pc = cc + fc - 2 * cc * fc
        peer = 4 * px + 2 * py + pc
        send = pltpu.make_async_remote_copy(
            src_ref=g_ref.at[peer], dst_ref=out_ref.at[me],
            send_sem=send_sems.at[k - 1], recv_sem=recv_sems.at[k - 1],
            device_id=(px, py, pc), device_id_type=MESH)
        recv = pltpu.make_async_remote_copy(
            src_ref=g_ref.at[peer], dst_ref=out_ref.at[peer],
            send_sem=send_sems.at[k - 1], recv_sem=recv_sems.at[k - 1],
            device_id=(px, py, pc), device_id_type=MESH)
        return send, recv

    pairs = [copy(k) for k in range(1, N_DEV)]

    def start():
        mine.start()
        for send, _ in pairs:
            send.start()

    def finish():
        for _, recv in pairs:
            recv.wait_recv()
        for send, _ in pairs:
            send.wait_send()
        mine.wait()

    return start, finish


def _all_gather(block, name, in_vmem):
    r, c = block.shape

    def body(x_ref, out_ref, send_sems, recv_sems, local_sem):
        start, finish = _gather_phases(x_ref, out_ref, send_sems, recv_sems, local_sem)
        start()
        finish()

    space = pltpu.VMEM if in_vmem else pl.ANY
    return pl.pallas_call(
        body, name=name,
        out_shape=jax.ShapeDtypeStruct((N_DEV, r, c), block.dtype),
        in_specs=[pl.BlockSpec(memory_space=space)],
        out_specs=pl.BlockSpec(memory_space=space),
        scratch_shapes=[pltpu.SemaphoreType.DMA((7,)), pltpu.SemaphoreType.DMA((7,)), pltpu.SemaphoreType.DMA],
    )(block)


def _scatter_blocks(g, name):
    _, r, c = g.shape

    def body(g_ref, out_ref, send_sems, recv_sems, local_sem):
        start, finish = _scatter_phases(g_ref, out_ref, send_sems, recv_sems, local_sem)
        start()
        finish()

    return pl.pallas_call(
        body, name=name,
        out_shape=jax.ShapeDtypeStruct(g.shape, g.dtype),
        in_specs=[pl.BlockSpec(memory_space=pl.ANY)],
        out_specs=pl.BlockSpec(memory_space=pl.ANY),
        scratch_shapes=[pltpu.SemaphoreType.DMA((7,)), pltpu.SemaphoreType.DMA((7,)), pltpu.SemaphoreType.DMA],
    )(g)


_DIMS = {"nn": (((1,), (0,)), ((), ())), "nt": (((1,), (1,)), ((), ())), "tn": (((0,), (0,)), ((), ()))}


def _matmul(a, b, mode, out_dtype, name, epilogue=None, side=None):
    if mode == "nn":
        (m, k), (_, n) = a.shape, b.shape
    elif mode == "nt":
        (m, k), (n, _) = a.shape, b.shape
    else:
        (k, m), (_, n) = a.shape, b.shape
    tm = _pick(m, (1024, 512, 256, 128))
    tn = _pick(n, (512, 384, 256, 128))
    tk = _pick(k, (1024, 512, 256, 128))
    nk = k // tk
    dims = _DIMS[mode]
    n_in = 3 if epilogue == "drelu2" else 2
    n_out = 2 if epilogue == "relu2" else 1

    def body(*refs):
        a_ref, b_ref = refs[:2]
        outs, acc_ref = refs[n_in:n_in + n_out], refs[n_in + n_out]
        kk = pl.program_id(2)

        @pl.when(kk == 0)
        def _():
            acc_ref[...] = jnp.zeros_like(acc_ref)

        acc_ref[...] += lax.dot_general(a_ref[...].astype(bf16), b_ref[...].astype(bf16), dims,
                                        preferred_element_type=f32)

        @pl.when(kk == nk - 1)
        def _():
            acc = acc_ref[...]
            if epilogue == "relu2":
                r = jnp.maximum(acc, 0.0)
                outs[0][...] = acc.astype(out_dtype)
                outs[1][...] = (r * r).astype(out_dtype)
            elif epilogue == "drelu2":
                outs[0][...] = (acc * (2.0 * jnp.maximum(refs[2][...].astype(f32), 0.0))).astype(out_dtype)
            else:
                outs[0][...] = acc.astype(out_dtype)

    if mode == "tn":
        a_spec = pl.BlockSpec((tk, tm), lambda i, j, kk: (kk, i))
    else:
        a_spec = pl.BlockSpec((tm, tk), lambda i, j, kk: (i, kk))
    if mode == "nt":
        b_spec = pl.BlockSpec((tn, tk), lambda i, j, kk: (j, kk))
    else:
        b_spec = pl.BlockSpec((tk, tn), lambda i, j, kk: (kk, j))
    o_spec = pl.BlockSpec((tm, tn), lambda i, j, kk: (i, j))
    o_shape = jax.ShapeDtypeStruct((m, n), out_dtype)
    res = pl.pallas_call(
        body, name=name, grid=(m // tm, n // tn, nk),
        in_specs=[a_spec, b_spec] + ([o_spec] if epilogue == "drelu2" else []),
        out_specs=[o_spec] * n_out, out_shape=[o_shape] * n_out,
        scratch_shapes=[pltpu.VMEM((tm, tn), f32)],
        compiler_params=_cparams(dimension_semantics=("parallel", "parallel", "arbitrary")),
    )(*((a, b, side) if epilogue == "drelu2" else (a, b)))
    return res if n_out == 2 else res[0]


@jax.custom_vjp
def mlp(h, w1, w1grad, w2, w2grad):
    _, r = _matmul(h, w1, "nn", bf16, "mlp1_fwd", epilogue="relu2")
    return _matmul(r, w2, "nn", f32, "mlp2_fwd")


def _mlp_fwd(h, w1, w1grad, w2, w2grad):
    u, r = _matmul(h, w1, "nn", bf16, "mlp1_fwd", epilogue="relu2")
    return _matmul(r, w2, "nn", f32, "mlp2_fwd"), (h, w1, w2, u, r)


def _mlp_bwd(res, dy):
    h, w1, w2, u, r = res
    du = _matmul(dy, w2, "nt", bf16, "mlp2_dgrad", epilogue="drelu2", side=u)
    dw2 = _matmul(r, dy, "tn", f32, "mlp2_wgrad")
    dh = _matmul(du, w1, "nt", h.dtype, "mlp1_dgrad")
    dw1 = _matmul(h, du, "tn", f32, "mlp1_wgrad")
    return dh, jnp.zeros_like(w1), dw1, jnp.zeros_like(w2), dw2


mlp.defvjp(_mlp_fwd, _mlp_bwd)


def make_linear(name):
    @jax.custom_vjp
    def linear(a, w, wgrad):
        return _matmul(a, w, "nn", f32, name + "_fwd")

    def fwd(a, w, wgrad):
        return linear(a, w, wgrad), (a, w)

    def bwd(res, dy):
        a, w = res
        da = _matmul(dy, w, "nt", a.dtype, name + "_dgrad")
        dw = _matmul(a, dy, "tn", f32, name + "_wgrad")
        return da, jnp.zeros_like(w), dw

    linear.defvjp(fwd, bwd)
    return linear


def make_rowwise(name, fn, row_out, sum_out=(), tm_pref=256):
    def specs(rows, gpars, cpars, consts, tm):
        s = [pl.BlockSpec((tm, r.shape[1]), lambda i: (i, 0)) for r in rows]
        s += [pl.BlockSpec(p.shape, lambda i: (0, 0)) for p in gpars]
        s += [pl.BlockSpec(p.shape, lambda i: (0, 0)) for p in cpars]
        for cst in consts:
            nb = cst.shape[0] // tm
            s.append(pl.BlockSpec((tm, cst.shape[1]), lambda i, nb=nb: (i % nb, 0)))
        return s

    def tile_rows(rows, consts):
        r = rows[0].shape[0]
        common = math.gcd(r, *[cst.shape[0] for cst in consts])
        tm = _pick(common, (tm_pref, 512, 256, 128, 64, 32, 16, 8))
        return r, tm

    def forward(rows, gpars, cpars, consts):
        r, tm = tile_rows(rows, consts)
        nr, ng, nc, nk = len(rows), len(gpars), len(cpars), len(consts)

        def body(*refs):
            ins = refs[:nr + ng + nc + nk]
            outs = refs[nr + ng + nc + nk:]
            rv = [t[...].astype(f32) for t in ins[:nr]]
            gv = [t[...].astype(f32) for t in ins[nr:nr + ng]]
            cv = [t[...] for t in ins[nr + ng:nr + ng + nc]]
            kv = [t[...].astype(f32) for t in ins[nr + ng + nc:]]
            ro, so = fn(rv, gv, cv, kv)
            for o_ref, val in zip(outs[:len(row_out)], ro):
                o_ref[...] = val.astype(o_ref.dtype)
            if sum_out:
                @pl.when(pl.program_id(0) == 0)
                def _():
                    for o_ref in outs[len(row_out):]:
                        o_ref[...] = jnp.zeros_like(o_ref)
                for o_ref, val in zip(outs[len(row_out):], so):
                    o_ref[...] += val

        out_specs = [pl.BlockSpec((tm, w), lambda i: (i, 0)) for w, _ in row_out]
        out_specs += [pl.BlockSpec(shp, lambda i: (0, 0)) for shp in sum_out]
        out_shape = [jax.ShapeDtypeStruct((r, w), dt) for w, dt in row_out]
        out_shape += [jax.ShapeDtypeStruct(shp, f32) for shp in sum_out]
        res = pl.pallas_call(
            body, name=name + "_fwd", grid=(r // tm,),
            in_specs=specs(rows, gpars, cpars, consts, tm), out_specs=out_specs, out_shape=out_shape,
            compiler_params=_cparams(dimension_semantics=("arbitrary",)),
        )(*rows, *gpars, *cpars, *consts)
        return tuple(res[:len(row_out)]), tuple(res[len(row_out):])

    def backward(rows, gpars, cpars, consts, d_ro, d_so):
        r, tm = tile_rows(rows, consts)
        nr, ng, nc, nk = len(rows), len(gpars), len(cpars), len(consts)
        n_in = nr + ng + nc + nk + len(row_out) + len(sum_out)

        def body(*refs):
            ins, outs = refs[:n_in], refs[n_in:]
            rv = [t[...].astype(f32) for t in ins[:nr]]
            gv = [t[...].astype(f32) for t in ins[nr:nr + ng]]
            cv = [t[...] for t in ins[nr + ng:nr + ng + nc]]
            kv = [t[...].astype(f32) for t in ins[nr + ng + nc:nr + ng + nc + nk]]
            o = nr + ng + nc + nk
            dro = [t[...].astype(f32) for t in ins[o:o + len(row_out)]]
            dso = [t[...] for t in ins[o + len(row_out):]]
            _, vjp = jax.vjp(lambda a, b: tuple(tuple(t) for t in fn(a, b, cv, kv)), rv, gv)
            drv, dgv = vjp((tuple(dro), tuple(dso)))
            for o_ref, val in zip(outs[:nr], drv):
                o_ref[...] = val.astype(o_ref.dtype)
            if ng:
                @pl.when(pl.program_id(0) == 0)
                def _():
                    for o_ref in outs[nr:]:
                        o_ref[...] = jnp.zeros_like(o_ref)
                for o_ref, val in zip(outs[nr:], dgv):
                    o_ref[...] += val

        in_specs = specs(rows, gpars, cpars, consts, tm)
        in_specs += [pl.BlockSpec((tm, w), lambda i: (i, 0)) for w, _ in row_out]
        in_specs += [pl.BlockSpec(shp, lambda i: (0, 0)) for shp in sum_out]
        out_specs = [pl.BlockSpec((tm, t.shape[1]), lambda i: (i, 0)) for t in rows]
        out_specs += [pl.BlockSpec(p.shape, lambda i: (0, 0)) for p in gpars]
        out_shape = [jax.ShapeDtypeStruct(t.shape, t.dtype) for t in rows]
        out_shape += [jax.ShapeDtypeStruct(p.shape, f32) for p in gpars]
        res = pl.pallas_call(
            body, name=name + "_bwd", grid=(r // tm,),
            in_specs=in_specs, out_specs=out_specs, out_shape=out_shape,
            compiler_params=_cparams(dimension_semantics=("arbitrary",)),
        )(*rows, *gpars, *cpars, *consts, *d_ro, *d_so)
        return tuple(res[:nr]), tuple(res[nr:])

    @jax.custom_vjp
    def op(rows, gpars, cpars, consts):
        return forward(rows, gpars, cpars, consts)

    def op_fwd(rows, gpars, cpars, consts):
        return forward(rows, gpars, cpars, consts), (rows, gpars, cpars, consts)

    def op_bwd(res, cts):
        rows, gpars, cpars, consts = res
        d_ro, d_so = cts
        drows, dg = backward(rows, gpars, cpars, consts, d_ro, d_so)
        dg = tuple(d.astype(p.dtype) for d, p in zip(dg, gpars))
        return (drows, dg, tuple(jnp.zeros_like(p) for p in cpars), tuple(jnp.zeros_like(k) for k in consts))

    op.defvjp(op_fwd, op_bwd)
    return op


def _rms(x):
    return x * lax.rsqrt(jnp.mean(x * x, axis=-1, keepdims=True) + EPS)


def _silu(x):
    return x * jax.nn.sigmoid(x)


def _fn_norm_mod(rows, gp, cp, ks):
    (x,), (nw, sc, sh) = rows, gp
    return ((_rms(x) * nw) * (1.0 + sc) + sh,), ()


def _fn_qk_norm_rope(rows, gp, cp, ks, out_scale=1.0):
    (t,), (w,), (pm,), (cos, sin) = rows, gp, cp, ks
    u = _rms(t) * w
    pu = jnp.dot(u, pm, precision=HIGHEST, preferred_element_type=f32)
    return ((u * cos + pu * sin) * out_scale,), ()


def make_head_rope(name, nh, out_scale, head_major):
    fn = functools.partial(_fn_qk_norm_rope, out_scale=out_scale)
    width = nh * HEAD_DIM

    def out_spec(tm):
        if head_major:
            return pl.BlockSpec((nh, tm, HEAD_DIM), lambda i: (0, i, 0))
        return pl.BlockSpec((tm, width), lambda i: (i, 0))

    def head_of(ref, h):
        return ref.at[h] if head_major else ref.at[:, h * HEAD_DIM:(h + 1) * HEAD_DIM]

    def one_head(t, w, pm, cos, sin):
        return fn([t], [w], [pm], [cos, sin])[0][0]

    def specs(tm):
        return [pl.BlockSpec((tm, width), lambda i: (i, 0)), pl.BlockSpec((1, HEAD_DIM), lambda i: (0, 0)),
                pl.BlockSpec((HEAD_DIM, HEAD_DIM), lambda i: (0, 0)), pl.BlockSpec((tm, HEAD_DIM), lambda i: (i, 0)),
                pl.BlockSpec((tm, HEAD_DIM), lambda i: (i, 0))]

    def forward(t, w, pm, cos, sin):
        s = t.shape[0]
        tm = _pick(s, (512, 256, 128))

        def body(t_ref, w_ref, pm_ref, cos_ref, sin_ref, o_ref):
            for h in range(nh):
                val = one_head(t_ref[:, h * HEAD_DIM:(h + 1) * HEAD_DIM], w_ref[...], pm_ref[...], cos_ref[...],
                               sin_ref[...])
                head_of(o_ref, h)[...] = val.astype(o_ref.dtype)

        return pl.pallas_call(
            body, name=name + "_fwd", grid=(s // tm,), in_specs=specs(tm),
            out_specs=out_spec(tm),
            out_shape=jax.ShapeDtypeStruct((nh, s, HEAD_DIM) if head_major else (s, width), bf16),
            compiler_params=_cparams(dimension_semantics=("arbitrary",)),
        )(t, w, pm, cos, sin)

    def backward(t, w, pm, cos, sin, dout):
        s = t.shape[0]
        tm = _pick(s, (512, 256, 128))

        def body(t_ref, w_ref, pm_ref, cos_ref, sin_ref, do_ref, dt_ref, dw_ref):
            @pl.when(pl.program_id(0) == 0)
            def _():
                dw_ref[...] = jnp.zeros_like(dw_ref)

            pm_v, cos_v, sin_v = pm_ref[...], cos_ref[...], sin_ref[...]
            dw = jnp.zeros((1, HEAD_DIM), f32)
            for h in range(nh):
                sl = slice(h * HEAD_DIM, (h + 1) * HEAD_DIM)
                _, vjp = jax.vjp(lambda a, b: one_head(a, b, pm_v, cos_v, sin_v), t_ref[:, sl], w_ref[...])
                dth, dwh = vjp(head_of(do_ref, h)[...].astype(f32))
                dt_ref[:, sl] = dth
                dw = dw + dwh
            dw_ref[...] += dw

        return pl.pallas_call(
            body, name=name + "_bwd", grid=(s // tm,),
            in_specs=specs(tm) + [out_spec(tm)],
            out_specs=[pl.BlockSpec((tm, width), lambda i: (i, 0)), pl.BlockSpec((1, HEAD_DIM), lambda i: (0, 0))],
            out_shape=[jax.ShapeDtypeStruct((s, width), f32), jax.ShapeDtypeStruct((1, HEAD_DIM), f32)],
            compiler_params=_cparams(dimension_semantics=("arbitrary",)),
        )(t, w, pm, cos, sin, dout)

    @jax.custom_vjp
    def op(t, w, pm, cos, sin):
        return forward(t, w, pm, cos, sin)

    def op_fwd(t, w, pm, cos, sin):
        return forward(t, w, pm, cos, sin), (t, w, pm, cos, sin)

    def op_bwd(res, dout):
        dt, dw = backward(*res, dout)
        return dt, dw, jnp.zeros_like(res[2]), jnp.zeros_like(res[3]), jnp.zeros_like(res[4])

    op.defvjp(op_fwd, op_bwd)
    return op


def _fn_softplus(rows, gp, cp, ks):
    (x,), (b,) = rows, gp
    v = x + b
    return (jnp.maximum(v, 0.0) + jnp.log(1.0 + jnp.exp(-jnp.abs(v))),), ()


def _fn_ssd_gate(rows, gp, cp, ks):
    (y, z), (nw,) = rows, gp
    return (_rms(y * _silu(z)) * nw,), ()


def _fn_merge(rows, gp, cp, ks):
    ao, so, ga, gs = rows
    return (jax.nn.sigmoid(ga) * ao + jax.nn.sigmoid(gs) * so,), ()


def _fn_res_norm(rows, gp, cp, ks):
    (x, mo), (g1, nw, sc, sh) = rows, gp
    x1 = x + g1 * mo
    return (x1, (_rms(x1) * nw) * (1.0 + sc) + sh), ()


def _fn_loss(rows, gp, cp, ks):
    (x1, ff), (g2,), (tgt,) = rows, gp, ks
    err = x1 + g2 * ff - tgt
    return (), (0.5 * jnp.sum(jnp.sum(err * err, axis=-1, keepdims=True), axis=0, keepdims=True) / D_MODEL,)


HALO = 8


def _conv_tiles(s, c):
    return _pick(s, (512, 256, 128)), _pick(c, (512, 256, 128))


def _halo_specs(tm, tc, s):
    nb = tm // HALO
    last = s // HALO - 1
    cur = pl.BlockSpec((tm, tc), lambda j, i: (i, j))
    prev = pl.BlockSpec((HALO, tc), lambda j, i: (jnp.maximum(i * nb - 1, 0), j))
    nxt = pl.BlockSpec((HALO, tc), lambda j, i: (jnp.minimum((i + 1) * nb, last), j))
    return cur, prev, nxt


def _fill_halo(buf, cur, prev, nxt, tm, i, n_i):
    buf[HALO:HALO + tm, :] = cur[...]
    buf[0:HALO, :] = jnp.where(i > 0, prev[...], 0.0)
    buf[HALO + tm:, :] = jnp.where(i < n_i - 1, nxt[...], 0.0)


def _conv_fwd(x, w, b, shard):
    s, c = x.shape
    tm, tc = _conv_tiles(s, c)
    n_i, n_j = s // tm, c // tc

    def body(cur, prev, nxt, w_ref, b_ref, shard_ref, o_ref, gath_ref, buf, send_sems, recv_sems, local_sem):
        j, i = pl.program_id(0), pl.program_id(1)
        start, finish = _gather_phases(shard_ref, gath_ref, send_sems, recv_sems, local_sem)

        @pl.when((j == 0) & (i == 0))
        def _():
            start()

        _fill_halo(buf, cur, prev, nxt, tm, i, n_i)
        pre = jnp.zeros((tm, tc), f32) + b_ref[...]
        for k in range(D_CONV):
            pre = pre + buf[HALO - 2 + k:HALO - 2 + k + tm, :] * w_ref[k:k + 1, :]
        o_ref[...] = _silu(pre)

        @pl.when((j == n_j - 1) & (i == n_i - 1))
        def _():
            finish()

    cur, prev, nxt = _halo_specs(tm, tc, s)
    hbm = pl.BlockSpec(memory_space=pl.ANY)
    return pl.pallas_call(
        body, name="conv_silu_fwd", grid=(n_j, n_i),
        in_specs=[cur, prev, nxt, pl.BlockSpec((D_CONV, tc), lambda j, i: (0, j)),
                  pl.BlockSpec((1, tc), lambda j, i: (0, j)), hbm],
        out_specs=[pl.BlockSpec((tm, tc), lambda j, i: (i, j)), hbm],
        out_shape=[jax.ShapeDtypeStruct((s, c), f32), jax.ShapeDtypeStruct((N_DEV,) + shard.shape, shard.dtype)],
        scratch_shapes=[pltpu.VMEM((tm + 2 * HALO, tc), f32)] + COMM_SEMS,
        compiler_params=_cparams(dimension_semantics=("arbitrary", "arbitrary")),
    )(x, x, x, w, b, shard)


def _conv_bwd(x, w, b, dy, g):
    s, c = x.shape
    tm, tc = _conv_tiles(s, c)
    n_i, n_j = s // tm, c // tc
    ext = tm + 8

    def body(cur, prev, nxt, dcur, dprev, dnxt, w_ref, b_ref, g_ref, dx_ref, dw_ref, db_ref, recv_ref,
             xbuf, dbuf, pbuf, send_sems, recv_sems, local_sem):
        j, i = pl.program_id(0), pl.program_id(1)
        start, finish = _scatter_phases(g_ref, recv_ref, send_sems, recv_sems, local_sem)

        @pl.when((j == 0) & (i == 0))
        def _():
            start()

        _fill_halo(xbuf, cur, prev, nxt, tm, i, n_i)
        _fill_halo(dbuf, dcur, dprev, dnxt, tm, i, n_i)
        pre = jnp.zeros((ext, tc), f32) + b_ref[...]
        for k in range(D_CONV):
            pre = pre + xbuf[2 + k:2 + k + ext, :] * w_ref[k:k + 1, :]
        sg = jax.nn.sigmoid(pre)
        pbuf[...] = dbuf[4:4 + ext, :] * (sg * (1.0 + pre * (1.0 - sg)))
        dx = jnp.zeros((tm, tc), f32)
        for k in range(D_CONV):
            dx = dx + pbuf[6 - k:6 - k + tm, :] * w_ref[k:k + 1, :]
        dx_ref[...] = dx

        @pl.when(i == 0)
        def _():
            dw_ref[...] = jnp.zeros_like(dw_ref)
            db_ref[...] = jnp.zeros_like(db_ref)

        dpre = pbuf[4:4 + tm, :]
        db_ref[...] += jnp.sum(dpre, axis=0, keepdims=True)
        for k in range(D_CONV):
            dw_ref[k:k + 1, :] += jnp.sum(dpre * xbuf[HALO - 2 + k:HALO - 2 + k + tm, :], axis=0, keepdims=True)

        @pl.when((j == n_j - 1) & (i == n_i - 1))
        def _():
            finish()

    cur, prev, nxt = _halo_specs(tm, tc, s)
    hbm = pl.BlockSpec(memory_space=pl.ANY)
    return pl.pallas_call(
        body, name="conv_silu_bwd", grid=(n_j, n_i),
        in_specs=[cur, prev, nxt, cur, prev, nxt, pl.BlockSpec((D_CONV, tc), lambda j, i: (0, j)),
                  pl.BlockSpec((1, tc), lambda j, i: (0, j)), hbm],
        out_specs=[pl.BlockSpec((tm, tc), lambda j, i: (i, j)), pl.BlockSpec((D_CONV, tc), lambda j, i: (0, j)),
                   pl.BlockSpec((1, tc), lambda j, i: (0, j)), hbm],
        out_shape=[jax.ShapeDtypeStruct((s, c), f32), jax.ShapeDtypeStruct((D_CONV, c), f32),
                   jax.ShapeDtypeStruct((1, c), f32), jax.ShapeDtypeStruct(g.shape, g.dtype)],
        scratch_shapes=[pltpu.VMEM((tm + 2 * HALO, tc), f32), pltpu.VMEM((tm + 2 * HALO, tc), f32),
                        pltpu.VMEM((ext, tc), f32)] + COMM_SEMS,
        compiler_params=_cparams(dimension_semantics=("arbitrary", "arbitrary")),
    )(x, x, x, dy, dy, dy, w, b, g)


@jax.custom_vjp
def conv_silu_comm(x, w, b, shard, recv_like):
    act, gathered = _conv_fwd(x, w, b, shard)
    return (act, gathered) + tuple(jnp.zeros(shp, f32) for shp in LATE_SHAPES)


def _conv_silu_comm_fwd(x, w, b, shard, recv_like):
    return conv_silu_comm(x, w, b, shard, recv_like), (x, w, b, shard)


def _conv_silu_comm_bwd(res, cts):
    x, w, b, shard = res
    dx, dw, db, recv = _conv_bwd(x, w, b, cts[0], _pack_late_grads(dict(zip(LATE, cts[2:]))))
    return dx, dw, db, jnp.zeros_like(shard), recv


conv_silu_comm.defvjp(_conv_silu_comm_fwd, _conv_silu_comm_bwd)


ATT_SCALE = HEAD_DIM ** -0.5
Q_SCALE = ATT_SCALE * math.log2(math.e)
LN2 = math.log(2.0)
REP = N_Q_HEADS // N_KV_HEADS


HP = 2
assert REP % HP == 0


def _attn_fwd(q, k, v):
    s, dh = q.shape[0], HEAD_DIM
    hq = q.shape[1] // dh
    tq = _pick(s, (256, 128))

    v1 = jnp.concatenate([v, jnp.ones(v.shape[:2] + (1,), v.dtype), jnp.zeros(v.shape[:2] + (dh - 1,), v.dtype)],
                         axis=-1)

    def body(q_ref, k_ref, v_ref, o_ref, p_ref, linv_ref):
        for j in range(HP):
            sl = slice(j * dh, (j + 1) * dh)
            sc = lax.dot_general(q_ref[:, sl], k_ref[0], _DIMS["nt"], preferred_element_type=f32)
            m = jnp.max(sc, axis=-1, keepdims=True)
            p = jnp.exp2(sc - m).astype(bf16)
            p_ref[j] = p
            o1 = jnp.dot(p, v_ref[0], preferred_element_type=f32)
            linv = 1.0 / o1[:, dh:dh + 1]
            o_ref[:, sl] = (o1[:, :dh] * linv).astype(o_ref.dtype)
            linv_ref[j] = linv

    return pl.pallas_call(
        body, name="attn_fwd", grid=(hq // HP, s // tq),
        in_specs=[pl.BlockSpec((tq, HP * dh), lambda h, i: (i, h)),
                  pl.BlockSpec((1, s, dh), lambda h, i: (h * HP // REP, 0, 0)),
                  pl.BlockSpec((1, s, 2 * dh), lambda h, i: (h * HP // REP, 0, 0))],
        out_specs=[pl.BlockSpec((tq, HP * dh), lambda h, i: (i, h)),
                   pl.BlockSpec((HP, tq, s), lambda h, i: (h, i, 0)),
                   pl.BlockSpec((HP, tq, 1), lambda h, i: (h, i, 0))],
        out_shape=[jax.ShapeDtypeStruct((s, hq * dh), bf16), jax.ShapeDtypeStruct((hq, s, s), bf16),
                   jax.ShapeDtypeStruct((hq, s, 1), f32)],
        compiler_params=_cparams(dimension_semantics=("parallel", "arbitrary")),
    )(q, k, v1)


def _attn_bwd(p, do, o, q, k, v, linv):
    hq, s, _ = p.shape
    dh = HEAD_DIM
    tq = _pick(s, (256, 128))

    def body(p_ref, do_ref, o_ref, q_ref, k_ref, v_ref, linv_ref, dq_ref, dkt_ref, dvt_ref):
        @pl.when(pl.program_id(1) == 0)
        def _():
            dkt_ref[...] = jnp.zeros_like(dkt_ref)
            dvt_ref[...] = jnp.zeros_like(dvt_ref)

        for j in range(HP):
            sl = slice(j * dh, (j + 1) * dh)
            pp, doh, li = p_ref[j], do_ref[:, sl], linv_ref[j]
            do32 = doh.astype(f32)
            d = jnp.sum(do32 * o_ref[:, sl].astype(f32), axis=-1, keepdims=True)
            dp = lax.dot_general(doh, v_ref[0], _DIMS["nt"], preferred_element_type=f32)
            ds = (pp.astype(f32) * ((dp - d) * li)).astype(bf16)
            dq_ref[:, sl] = jnp.dot(ds, k_ref[0], preferred_element_type=f32) * LN2
            dvt_ref[j] += lax.dot_general((do32 * li).astype(bf16), pp, _DIMS["tn"], preferred_element_type=f32)
            dkt_ref[j] += lax.dot_general(q_ref[:, sl], ds, _DIMS["tn"], preferred_element_type=f32)

    def row():
        return pl.BlockSpec((tq, HP * dh), lambda h, i: (i, h))

    return pl.pallas_call(
        body, name="attn_bwd", grid=(hq // HP, s // tq),
        in_specs=[pl.BlockSpec((HP, tq, s), lambda h, i: (h, i, 0)), row(), row(), row(),
                  pl.BlockSpec((1, s, dh), lambda h, i: (h * HP // REP, 0, 0)),
                  pl.BlockSpec((1, s, dh), lambda h, i: (h * HP // REP, 0, 0)),
                  pl.BlockSpec((HP, tq, 1), lambda h, i: (h, i, 0))],
        out_specs=[row(), pl.BlockSpec((HP, dh, s), lambda h, i: (h, 0, 0)),
                   pl.BlockSpec((HP, dh, s), lambda h, i: (h, 0, 0))],
        out_shape=[jax.ShapeDtypeStruct((s, hq * dh), f32), jax.ShapeDtypeStruct((hq, dh, s), f32),
                   jax.ShapeDtypeStruct((hq, dh, s), f32)],
        compiler_params=_cparams(dimension_semantics=("parallel", "arbitrary")),
    )(p, do, o, q, k, v, linv)


@jax.custom_vjp
def attention(q, k, v):
    return _attn_fwd(q, k, v)[0]


def _attention_fwd(q, k, v):
    o, p, linv = _attn_fwd(q, k, v)
    return o, (q, k, v, o, p, linv)


def _attention_bwd(res, do):
    q, k, v, o, p, linv = res
    s = q.shape[0]
    dq, dkt, dvt = _attn_bwd(p, do.astype(bf16), o, q, k, v, linv)

    def per_kv_head(t):
        return jnp.swapaxes(t.reshape(N_KV_HEADS, REP, HEAD_DIM, s).sum(axis=1), 1, 2)

    return dq.astype(q.dtype), (per_kv_head(dkt) * LN2).astype(k.dtype), per_kv_head(dvt).astype(v.dtype)


attention.defvjp(_attention_fwd, _attention_bwd)


HPG = N_SSD_HEADS // N_SSD_GROUPS
GW = HPG * SSD_HEAD_DIM
NEG = -1e30
SPLIT_ROWS = 32


def _ssd_consts():
    k = np.arange(SPLIT_ROWS)[:, None]
    live = k < 3 * HPG
    sel_chunk = ((k % HPG) == (np.arange(HPG * CHUNK)[None, :] // CHUNK)) & live
    sel_head = ((k % HPG) == (np.arange(GW)[None, :] // SSD_HEAD_DIM)) & live
    return jnp.asarray(sel_chunk, bf16), jnp.asarray(sel_head, bf16)


def _split3(x):
    hi = x.astype(bf16).astype(f32)
    r1 = x - hi
    mid = r1.astype(bf16).astype(f32)
    lo = (r1 - mid).astype(bf16).astype(f32)
    return jnp.concatenate([hi, mid, lo, jnp.zeros_like(hi)], axis=0).astype(bf16)


def _tn(a, b):
    return lax.dot_general(a, b, _DIMS["tn"], preferred_element_type=f32)


def _nt(a, b):
    return lax.dot_general(a, b, _DIMS["nt"], preferred_element_type=f32)


def _nn(a, b):
    return jnp.dot(a, b, preferred_element_type=f32)


def _head_sum(sel8, x):
    hi = x.astype(bf16)
    lo = (x - hi.astype(f32)).astype(bf16)
    return _nt(sel8, hi) + _nt(sel8, lo)


def _ssd_masks(reverse):
    r = lax.broadcasted_iota(jnp.int32, (CHUNK, CHUNK), 0)
    c = lax.broadcasted_iota(jnp.int32, (CHUNK, CHUNK), 1)
    lower, upper = r >= c, r <= c
    return (upper, lower) if reverse else (lower, upper)


def _ssd_in_specs(cidx):
    return [pl.BlockSpec((CHUNK, D_INNER), lambda c: (cidx(c), 0)),
            pl.BlockSpec((CHUNK, GN), lambda c: (cidx(c), D_INNER // GN)),
            pl.BlockSpec((CHUNK, GN), lambda c: (cidx(c), D_INNER // GN + 1)),
            pl.BlockSpec((N_SSD_HEADS, CHUNK), lambda c: (0, cidx(c))),
            pl.BlockSpec((N_SSD_HEADS, 1), lambda c: (0, 0)),
            pl.BlockSpec((SPLIT_ROWS, HPG * CHUNK), lambda c: (0, 0)),
            pl.BlockSpec((SPLIT_ROWS, GW), lambda c: (0, 0))]


def _ssd_chunk_common(dtt_ref, a_ref, et_ref, mask_t):
    dtt = dtt_ref[...]
    et = jnp.dot(dtt * a_ref[...], mask_t.astype(f32), precision=HIGHEST, preferred_element_type=f32)
    et_ref[...] = et
    return dtt, et


def _ssd_group_common(g, dtt, et, selc_ref, selh_ref, xs_ref, b_ref, c_ref, last):
    gr = slice(g * HPG, (g + 1) * HPG)
    e3 = _split3(et[gr])
    col = _tn(e3, selc_ref[...])
    eb = _tn(e3, selh_ref[...])
    dtb = _tn(_split3(dtt[gr]), selh_ref[...])
    tbc = eb[last:last + 1, :]
    xs = xs_ref[:, g * GW:(g + 1) * GW]
    bg = b_ref[:, g * D_STATE:(g + 1) * D_STATE].astype(bf16)
    cg = c_ref[:, g * D_STATE:(g + 1) * D_STATE].astype(bf16)
    return col, eb, dtb, tbc, xs, bg, cg


def _ssd_fwd(xbc, dtt, a_col, reverse, y_prev=None, dexp=None):
    s = xbc.shape[0]
    nc = s // CHUNK
    cidx = (lambda c: nc - 1 - c) if reverse else (lambda c: c)
    last = 0 if reverse else CHUNK - 1
    selc, selh = _ssd_consts()
    final = y_prev is not None
    n_in = 9 if final else 7

    def body(*refs):
        xs_ref, b_ref, c_ref, dtt_ref, a_ref, selc_ref, selh_ref = refs[:7]
        y_ref, st_ref, ht_ref, et_ref = refs[n_in:]

        @pl.when(pl.program_id(0) == 0)
        def _():
            ht_ref[...] = jnp.zeros_like(ht_ref)

        mask, mask_t = _ssd_masks(reverse)
        dtt_v, et = _ssd_chunk_common(dtt_ref, a_ref, et_ref, mask_t)
        for g in range(N_SSD_GROUPS):
            col, eb, dtb, tbc, xs, bg, cg = _ssd_group_common(g, dtt_v, et, selc_ref, selh_ref, xs_ref, b_ref, c_ref,
                                                              last)
            xd = xs * dtb
            cb = _nt(cg, bg)
            ht = ht_ref[g]
            st_ref[0, g] = ht
            yoff = _nn(cg, ht.astype(bf16)) * jnp.exp(eb)
            for j in range(HPG):
                h = g * HPG + j
                hs = slice(j * SSD_HEAD_DIM, (j + 1) * SSD_HEAD_DIM)
                lam = jnp.exp(jnp.where(mask, col[:, j * CHUNK:(j + 1) * CHUNK] - et_ref[h:h + 1, :], NEG))
                yj = _nn((cb * lam).astype(bf16), xd[:, hs].astype(bf16)) + yoff[:, hs]
                cols = slice(g * GW + j * SSD_HEAD_DIM, g * GW + (j + 1) * SSD_HEAD_DIM)
                if final:
                    yj = yj + refs[7][:, cols] + xs[:, hs] * refs[8][:, cols]
                y_ref[:, cols] = yj
            ht_ref[g] = jnp.exp(tbc) * ht + _tn(bg, (xd * jnp.exp(tbc - eb)).astype(bf16))

    y_spec = pl.BlockSpec((CHUNK, D_INNER), lambda c: (cidx(c), 0))
    extra_specs = [y_spec, pl.BlockSpec((1, D_INNER), lambda c: (0, 0))] if final else []
    return pl.pallas_call(
        body, name="ssd_fwd_rev" if reverse else "ssd_fwd", grid=(nc,),
        in_specs=_ssd_in_specs(cidx) + extra_specs,
        out_specs=[y_spec, pl.BlockSpec((1, N_SSD_GROUPS, D_STATE, GW), lambda c: (cidx(c), 0, 0, 0))],
        out_shape=[jax.ShapeDtypeStruct((s, D_INNER), f32),
                   jax.ShapeDtypeStruct((nc, N_SSD_GROUPS, D_STATE, GW), f32)],
        scratch_shapes=[pltpu.VMEM((N_SSD_GROUPS, D_STATE, GW), f32), pltpu.VMEM((N_SSD_HEADS, CHUNK), f32)],
        compiler_params=_cparams(dimension_semantics=("arbitrary",)),
    )(xbc, xbc, xbc, dtt, a_col, selc, selh, *((y_prev, dexp) if final else ()))


def _ssd_bwd(xbc, dtt, a_col, states, dy, reverse, dxbc_prev=None, dexp=None):
    s = xbc.shape[0]
    nc = s // CHUNK
    cidx = (lambda c: c) if reverse else (lambda c: nc - 1 - c)
    last = 0 if reverse else CHUNK - 1
    selc, selh = _ssd_consts()
    final = dxbc_prev is not None
    n_in = 11 if final else 9
    n_out = 4 if final else 3

    def body(*refs):
        xs_ref, b_ref, c_ref, dtt_ref, a_ref, selc_ref, selh_ref, st_ref, dy_ref = refs[:9]
        dxbc_ref, ddtt_ref, da_ref = refs[n_in:n_in + 3]
        dh_ref, et_ref, det_ref, det2_ref, ddt_ref, q_ref = refs[n_in + n_out:]
        if final:
            prev_ref, dexp_ref, ddexp_ref = refs[9], refs[10], refs[n_in + 3]

        @pl.when(pl.program_id(0) == 0)
        def _():
            dh_ref[...] = jnp.zeros_like(dh_ref)
            da_ref[...] = jnp.zeros_like(da_ref)
            if final:
                ddexp_ref[...] = jnp.zeros_like(ddexp_ref)

        mask, mask_t = _ssd_masks(reverse)
        dtt_v, et = _ssd_chunk_common(dtt_ref, a_ref, et_ref, mask_t)
        sel8 = selh_ref[0:HPG, :]
        is_last = lax.broadcasted_iota(jnp.int32, (CHUNK, GW), 0) == last
        for g in range(N_SSD_GROUPS):
            col, eb, dtb, tbc, xs, bg, cg = _ssd_group_common(g, dtt_v, et, selc_ref, selh_ref, xs_ref, b_ref, c_ref,
                                                              last)
            xd = xs * dtb
            cb = _nt(cg, bg)
            cbt = _nt(bg, cg)
            exp_t = jnp.exp(tbc)
            dfac = jnp.exp(tbc - eb)
            ht = st_ref[0, g]
            dhn = dh_ref[g]
            ht16, dhn16 = ht.astype(bf16), dhn.astype(bf16)
            dy = dy_ref[:, g * GW:(g + 1) * GW]
            dye = dy * jnp.exp(eb)
            dye16 = dye.astype(bf16)
            dc = _nt(dye16, ht16)
            dh_ref[g] = exp_t * dhn + _tn(cg, dye16)
            deb = dye * _nn(cg, ht16)
            xdd = xd * dfac
            dxdd = _nn(bg, dhn16)
            db = _nt(xdd.astype(bf16), dhn16)
            dxd_state = dxdd * dfac
            ddf = dxdd * xdd
            dtbc = jnp.sum(ddf, axis=0, keepdims=True) + exp_t * jnp.sum(dhn * ht, axis=0, keepdims=True)
            deb = deb - ddf + jnp.where(is_last, dtbc, 0.0)
            dcb = jnp.zeros((CHUNK, CHUNK), f32)
            dcbt = jnp.zeros((CHUNK, CHUNK), f32)
            for j in range(HPG):
                h = g * HPG + j
                hs = slice(j * SSD_HEAD_DIM, (j + 1) * SSD_HEAD_DIM)
                colj = col[:, j * CHUNK:(j + 1) * CHUNK]
                row = et_ref[h:h + 1, :]
                lam = jnp.exp(jnp.where(mask, colj - row, NEG))
                lam_t = jnp.exp(jnp.where(mask_t, row - colj, NEG))
                xdj, dyj = xd[:, hs].astype(bf16), dy[:, hs].astype(bf16)
                t1 = _nt(dyj, xdj) * lam
                t2 = _nt(xdj, dyj) * lam_t
                dcb, dcbt = dcb + t1, dcbt + t2
                det_ref[h:h + 1, :] = -jnp.sum(t1 * cb - t2 * cbt, axis=0, keepdims=True)
                dxdj = _nn((cbt * lam_t).astype(bf16), dyj) + dxd_state[:, hs]
                cols = slice(g * GW + j * SSD_HEAD_DIM, g * GW + (j + 1) * SSD_HEAD_DIM)
                dxs = dxdj * dtb[:, hs]
                if final:
                    dxs = dxs + prev_ref[:, cols] + dy[:, hs] * dexp_ref[:, cols]
                dxbc_ref[:, cols] = dxs
                q_ref[:, hs] = dxdj * xs[:, hs]
            b_cols = slice(D_INNER + g * D_STATE, D_INNER + (g + 1) * D_STATE)
            c_cols = slice(D_INNER + GN + g * D_STATE, D_INNER + GN + (g + 1) * D_STATE)
            db = db + _nn(dcbt.astype(bf16), cg)
            dc = dc + _nn(dcb.astype(bf16), bg)
            if final:
                db, dc = db + prev_ref[:, b_cols], dc + prev_ref[:, c_cols]
                ddexp_ref[:, g * GW:(g + 1) * GW] += jnp.sum(dy * xs, axis=0, keepdims=True)
            dxbc_ref[:, b_cols] = db
            dxbc_ref[:, c_cols] = dc
            det2_ref[g * HPG:(g + 1) * HPG, :] = _head_sum(sel8, deb)
            ddt_ref[g * HPG:(g + 1) * HPG, :] = _head_sum(sel8, q_ref[...])
        dat = jnp.dot(det_ref[...] + det2_ref[...], mask.astype(f32), precision=HIGHEST, preferred_element_type=f32)
        ddtt_ref[...] = ddt_ref[...] + dat * a_ref[...]
        da_ref[...] += jnp.sum(dat * dtt_v, axis=1, keepdims=True)

    in_specs = _ssd_in_specs(cidx) + [
        pl.BlockSpec((1, N_SSD_GROUPS, D_STATE, GW), lambda c: (cidx(c), 0, 0, 0)),
        pl.BlockSpec((CHUNK, D_INNER), lambda c: (cidx(c), 0))]
    hl = pltpu.VMEM((N_SSD_HEADS, CHUNK), f32)
    dxbc_spec = pl.BlockSpec((CHUNK, CONV_DIM), lambda c: (cidx(c), 0))
    dexp_spec = pl.BlockSpec((1, D_INNER), lambda c: (0, 0))
    return pl.pallas_call(
        body, name="ssd_bwd_rev" if reverse else "ssd_bwd", grid=(nc,),
        in_specs=in_specs + ([dxbc_spec, dexp_spec] if final else []),
        out_specs=[dxbc_spec, pl.BlockSpec((N_SSD_HEADS, CHUNK), lambda c: (0, cidx(c))),
                   pl.BlockSpec((N_SSD_HEADS, 1), lambda c: (0, 0))] + ([dexp_spec] if final else []),
        out_shape=[jax.ShapeDtypeStruct((s, CONV_DIM), f32), jax.ShapeDtypeStruct((N_SSD_HEADS, s), f32),
                   jax.ShapeDtypeStruct((N_SSD_HEADS, 1), f32)]
        + ([jax.ShapeDtypeStruct((1, D_INNER), f32)] if final else []),
        scratch_shapes=[pltpu.VMEM((N_SSD_GROUPS, D_STATE, GW), f32), hl, hl, hl, hl, pltpu.VMEM((CHUNK, GW), f32)],
        compiler_params=_cparams(dimension_semantics=("arbitrary",)),
    )(xbc, xbc, xbc, dtt, a_col, selc, selh, states, dy, *((dxbc_prev, dexp) if final else ()))


@jax.custom_vjp
def ssd_bidir(xbc, dtt, a_col, dexp):
    y_f, _ = _ssd_fwd(xbc, dtt[:N_SSD_HEADS], a_col[:N_SSD_HEADS], False)
    return _ssd_fwd(xbc, dtt[N_SSD_HEADS:], a_col[N_SSD_HEADS:], True, y_prev=y_f, dexp=dexp)[0]


def _ssd_bidir_fwd(xbc, dtt, a_col, dexp):
    y_f, st_f = _ssd_fwd(xbc, dtt[:N_SSD_HEADS], a_col[:N_SSD_HEADS], False)
    y, st_b = _ssd_fwd(xbc, dtt[N_SSD_HEADS:], a_col[N_SSD_HEADS:], True, y_prev=y_f, dexp=dexp)
    return y, (xbc, dtt, a_col, dexp, st_f, st_b)


def _ssd_bidir_bwd(res, dy):
    xbc, dtt, a_col, dexp, st_f, st_b = res
    dxbc_f, ddtt_f, da_f = _ssd_bwd(xbc, dtt[:N_SSD_HEADS], a_col[:N_SSD_HEADS], st_f, dy, False)
    dxbc, ddtt_b, da_b, ddexp = _ssd_bwd(xbc, dtt[N_SSD_HEADS:], a_col[N_SSD_HEADS:], st_b, dy, True,
                                         dxbc_prev=dxbc_f, dexp=dexp)
    return dxbc, jnp.concatenate([ddtt_f, ddtt_b], axis=0), jnp.concatenate([da_f, da_b], axis=0), ddexp


ssd_bidir.defvjp(_ssd_bidir_fwd, _ssd_bidir_bwd)


W_NAMES = PROJ_NAMES + ("attn_out", "ssd_out", "o", "mlp1", "mlp2")


def _rope_tables(s):
    rows = s // GRID_W
    pos_row = jnp.repeat(jnp.arange(rows, dtype=jnp.int32), GRID_W).astype(f32)
    pos_col = jnp.tile(jnp.arange(GRID_W, dtype=jnp.int32), rows).astype(f32)
    axis_dim = HEAD_DIM // 2
    inv_freq = ROPE_THETA ** (-jnp.arange(0, axis_dim, 2, dtype=f32) / axis_dim)
    ang_r = pos_row[:, None] * inv_freq[None, :]
    ang_c = pos_col[:, None] * inv_freq[None, :]
    cos = jnp.concatenate([jnp.cos(ang_r), jnp.cos(ang_r), jnp.cos(ang_c), jnp.cos(ang_c)], axis=-1)
    sin = jnp.concatenate([jnp.sin(ang_r), jnp.sin(ang_r), jnp.sin(ang_c), jnp.sin(ang_c)], axis=-1)
    return cos, sin


def _rope_perm():
    p = np.zeros((HEAD_DIM, HEAD_DIM), np.float32)
    for j in range(HEAD_DIM):
        if (j % 32) < 16:
            p[j + 16, j] = -1.0
        else:
            p[j - 16, j] = 1.0
    return jnp.asarray(p)


def local_loss(x, mod, small, wgrads, recv_like, wfull, late_shard, target):
    s = x.shape[0]
    lin = {n: make_linear("lin_" + n) for n in W_NAMES if not n.startswith("mlp")}
    wfull, wgrads = dict(wfull), dict(wgrads)
    shift1, scale1, gate1, shift2, scale2, gate2 = [mod[i] for i in range(6)]

    norm_mod = make_rowwise("norm_mod", _fn_norm_mod, [(D_MODEL, bf16)])
    (h,), _ = norm_mod((x,), (small["norm1_w"], scale1, shift1), (), ())

    proj = {n: lin[n](h, wfull[n], wgrads[n]) for n in PROJ_NAMES}

    cos, sin = _rope_tables(s)
    pm = _rope_perm()

    def heads(t, nh):
        return t.reshape(s, nh, HEAD_DIM).transpose(1, 0, 2)

    qr = make_head_rope("q_norm_rope", N_Q_HEADS, Q_SCALE, False)(proj["q"], small["q_norm_w"], pm, cos, sin)
    kr = make_head_rope("k_norm_rope", N_KV_HEADS, 1.0, True)(proj["k"], small["k_norm_w"], pm, cos, sin)
    vh = heads(proj["v"], N_KV_HEADS).astype(bf16)
    att = attention(qr, kr, vh)

    xbc, gathered, *carriers = conv_silu_comm(proj["xbc"], small["conv_w"], small["conv_b"], late_shard, recv_like)
    wfull.update(_split_late(gathered))
    wgrads.update(zip(LATE, carriers))
    ao = lin["attn_out"](att, wfull["attn_out"], wgrads["attn_out"])
    softplus = make_rowwise("dt_softplus", _fn_softplus, [(2 * N_SSD_HEADS, f32)])
    (dt,), _ = softplus((proj["dt"][:, :2 * N_SSD_HEADS],), (small["dt_bias"].reshape(1, 2 * N_SSD_HEADS),), (), ())
    a_neg = -jnp.exp(small["A_log"])
    dexp = jnp.repeat(small["ssd_D"].reshape(N_SSD_HEADS), SSD_HEAD_DIM).reshape(1, D_INNER)
    y = ssd_bidir(xbc, dt.T, a_neg.reshape(2 * N_SSD_HEADS, 1), dexp)
    ssd_gate = make_rowwise("ssd_gate", _fn_ssd_gate, [(D_INNER, bf16)], tm_pref=128)
    (ssd_out,), _ = ssd_gate((y, proj["z"]), (small["ssd_norm_w"],), (), ())
    so = lin["ssd_out"](ssd_out, wfull["ssd_out"], wgrads["ssd_out"])

    merge = make_rowwise("merge", _fn_merge, [(D_MODEL, bf16)])
    (merged,), _ = merge((ao, so, proj["ga"], proj["gs"]), (), (), ())
    mo = lin["o"](merged, wfull["o"], wgrads["o"])

    res_norm = make_rowwise("res_norm", _fn_res_norm, [(D_MODEL, f32), (D_MODEL, bf16)])
    (x1, h2), _ = res_norm((x, mo), (gate1, small["norm2_w"], scale2, shift2), (), ())
    ff = mlp(h2, wfull["mlp1"], wgrads["mlp1"], wfull["mlp2"], wgrads["mlp2"])
    loss_op = make_rowwise("loss", _fn_loss, [], [(1, 1)])
    _, (loss,) = loss_op((x1, ff), (gate2,), (), (target,))
    return loss[0, 0]


_BC1 = 1.0 - ADAM_B1 ** ADAM_STEP
_BC2 = 1.0 - ADAM_B2 ** ADAM_STEP


def _adamw(w, g, m, v):
    m = ADAM_B1 * m + (1.0 - ADAM_B1) * g
    v = ADAM_B2 * v + (1.0 - ADAM_B2) * (g * g)
    delta = -ADAM_LR * ((m / _BC1) / (jnp.sqrt(v / _BC2) + ADAM_EPS) + ADAM_WD * w)
    return delta, m, v


def _ada_fwd(c_all, w, b):
    n = w.shape[1]

    def body(c_ref, w_ref, b_ref, o_ref):
        o_ref[...] = jnp.dot(_silu(c_ref[...]), w_ref[...], precision=HIGHEST, preferred_element_type=f32) + b_ref[...]

    return pl.pallas_call(body, name="ada_fwd", out_shape=jax.ShapeDtypeStruct((N_DEV, n), f32),
                          compiler_params=_cparams())(c_all, w, b)


def _ada_bwd_adamw(c_all, dmod, w, m, v):
    d, n = w.shape
    tr = _pick(d, (256, 128))

    def body(c_ref, dm_ref, w_ref, m_ref, v_ref, g_ref, dl_ref, mo_ref, vo_ref):
        g = lax.dot_general(_silu(c_ref[...]), dm_ref[...], _DIMS["tn"], precision=HIGHEST,
                            preferred_element_type=f32)
        g_ref[...] = g
        dl_ref[...], mo_ref[...], vo_ref[...] = _adamw(w_ref[...], g, m_ref[...], v_ref[...])

    blk = pl.BlockSpec((tr, n), lambda i: (i, 0))
    return pl.pallas_call(
        body, name="ada_bwd_adamw", grid=(d // tr,),
        in_specs=[pl.BlockSpec((N_DEV, tr), lambda i: (0, i)), pl.BlockSpec((N_DEV, n), lambda i: (0, 0)), blk, blk, blk],
        out_specs=[blk] * 4, out_shape=[jax.ShapeDtypeStruct((d, n), f32)] * 4,
        compiler_params=_cparams(dimension_semantics=("parallel",)),
    )(c_all, dmod, w, m, v)


def _sum_over_mesh(g):
    def body(g_ref, o_ref):
        acc = g_ref[0]
        for d in range(1, N_DEV):
            acc = acc + g_ref[d]
        o_ref[...] = acc

    return pl.pallas_call(body, name="sum_small", out_shape=jax.ShapeDtypeStruct(g.shape[1:], f32),
                          compiler_params=_cparams())(g)


def _adamw_small(w, g, m, v):
    def body(w_ref, g_ref, m_ref, v_ref, dl_ref, mo_ref, vo_ref):
        dl_ref[...], mo_ref[...], vo_ref[...] = _adamw(w_ref[...], g_ref[...], m_ref[...], v_ref[...])

    return pl.pallas_call(body, name="adamw_small", out_shape=[jax.ShapeDtypeStruct(w.shape, f32)] * 3,
                          compiler_params=_cparams())(w, g, m, v)


def _sum_adamw(recv, w, m, v, name):
    _, r, c = recv.shape
    tr = _pick(r, (256, 128, 64, 16))

    def body(g_ref, w_ref, m_ref, v_ref, go_ref, dl_ref, mo_ref, vo_ref):
        g = g_ref[0].astype(f32)
        for d in range(1, N_DEV):
            g = g + g_ref[d].astype(f32)
        go_ref[...] = g
        dl_ref[...], mo_ref[...], vo_ref[...] = _adamw(w_ref[...], g, m_ref[...], v_ref[...])

    blk = pl.BlockSpec((tr, c), lambda i: (i, 0))
    return pl.pallas_call(
        body, name=name, grid=(r // tr,),
        in_specs=[pl.BlockSpec((N_DEV, tr, c), lambda i: (0, i, 0)), blk, blk, blk],
        out_specs=[blk] * 4, out_shape=[jax.ShapeDtypeStruct((r, c), f32)] * 4,
        compiler_params=_cparams(dimension_semantics=("parallel",)),
    )(recv, w, m, v)


def _pack_small(arrs):
    parts = []
    for a in arrs:
        flat = a.reshape(-1).astype(f32)
        parts.append(jnp.pad(flat, (0, (-flat.shape[0]) % LANE)))
    flat = jnp.concatenate(parts)
    flat = jnp.pad(flat, (0, (-flat.shape[0]) % (8 * LANE)))
    return flat.reshape(-1, LANE)


def _unpack_small(packed, shapes):
    flat = packed.reshape(-1)
    out, off = [], 0
    for shp in shapes:
        n = int(np.prod(shp))
        out.append(flat[off:off + n].reshape(shp))
        off += n + (-n) % LANE
    return out


BIG = ("w_attn_out", "w_ssd_out", "w_o", "w_mlp1", "w_mlp2")
BIG_ROWS = (N_Q_HEADS * HEAD_DIM // N_DEV, D_INNER // N_DEV, D_MODEL // N_DEV,
            D_MODEL * (D_FF // N_DEV) // PACK_COLS, D_FF // N_DEV)
N_IN_SHARD = D_IN_PROJ // N_DEV
assert sum(BIG_ROWS) % 16 == 0


def _pack_big(shards, dtype):
    return jnp.concatenate([s.astype(dtype).reshape(-1, PACK_COLS) for s in shards], axis=0)


def _unpack_big(packed, shapes):
    out, off = [], 0
    for rows, shp in zip(BIG_ROWS, shapes):
        out.append(packed[off:off + rows].reshape(shp))
        off += rows
    return out


LATE = ("attn_out", "ssd_out", "o", "mlp1", "mlp2")
LATE_SHAPES = ((N_Q_HEADS * HEAD_DIM, D_MODEL), (D_INNER, D_MODEL), (D_MODEL, D_MODEL), (D_MODEL, D_FF),
               (D_FF, D_MODEL))


def _split_w_in(g_in):
    w_in = g_in.transpose(1, 0, 2).reshape(D_MODEL, D_IN_PROJ)
    w = {}
    off = 0
    for name, size in zip(PROJ_NAMES, PROJ_SIZES):
        w[name] = w_in[:, off:off + size]
        off += size
    w["dt"] = jnp.pad(w["dt"], ((0, 0), (0, DT_PAD - 2 * N_SSD_HEADS)))
    return w


def _split_late(g):
    offs = np.cumsum((0,) + BIG_ROWS)
    sl = [g[:, offs[i]:offs[i + 1]] for i in range(len(BIG))]
    return {"attn_out": sl[0].reshape(LATE_SHAPES[0]), "ssd_out": sl[1].reshape(LATE_SHAPES[1]),
            "o": sl[2].reshape(LATE_SHAPES[2]),
            "mlp1": sl[3].reshape(N_DEV, D_MODEL, D_FF // N_DEV).transpose(1, 0, 2).reshape(LATE_SHAPES[3]),
            "mlp2": sl[4].reshape(LATE_SHAPES[4])}


def _pack_in_grads(gw):
    gw = {n: g.astype(bf16) for n, g in gw.items()}
    gw["dt"] = gw["dt"][:, :2 * N_SSD_HEADS]
    g_in = jnp.concatenate([gw[n] for n in PROJ_NAMES], axis=1)
    return g_in.reshape(D_MODEL, N_DEV, N_IN_SHARD).transpose(1, 0, 2)


def _pack_late_grads(gw):
    gw = {n: g.astype(bf16) for n, g in gw.items()}
    parts = [
        gw["attn_out"].reshape(N_DEV, -1, PACK_COLS),
        gw["ssd_out"].reshape(N_DEV, -1, PACK_COLS),
        gw["o"].reshape(N_DEV, -1, PACK_COLS),
        gw["mlp1"].reshape(D_MODEL, N_DEV, D_FF // N_DEV).transpose(1, 0, 2).reshape(N_DEV, -1, PACK_COLS),
        gw["mlp2"].reshape(N_DEV, -1, PACK_COLS),
    ]
    return jnp.concatenate(parts, axis=1)


SMALL = ("norm1_w", "norm2_w", "q_norm_w", "k_norm_w", "conv_w", "conv_b", "A_log", "dt_bias", "ssd_D", "ssd_norm_w")


def kernel(x, c, w_ada, b_ada, norm1_w, norm2_w, w_in, q_norm_w, k_norm_w, conv_w, conv_b, A_log, dt_bias, ssd_D, ssd_norm_w, w_attn_out, w_ssd_out, w_o, w_mlp1, w_mlp2, loss_target, m_w_ada, m_b_ada, m_norm1_w, m_norm2_w, m_w_in, m_q_norm_w, m_k_norm_w, m_conv_w, m_conv_b, m_A_log, m_dt_bias, m_ssd_D, m_ssd_norm_w, m_w_attn_out, m_w_ssd_out, m_w_o, m_w_mlp1, m_w_mlp2, v_w_ada, v_b_ada, v_norm1_w, v_norm2_w, v_w_in, v_q_norm_w, v_k_norm_w, v_conv_w, v_conv_b, v_A_log, v_dt_bias, v_ssd_D, v_ssd_norm_w, v_w_attn_out, v_w_ssd_out, v_w_o, v_w_mlp1, v_w_mlp2):
    args = dict(locals())
    me = _my_index()
    n_ada = 6 * D_MODEL // N_DEV
    n_cw = CONV_DIM // N_DEV

    blk = jnp.zeros((8, D_MODEL), f32)
    blk = blk.at[0:1, :].set(c)
    blk = blk.at[1:1 + D_CONV, :n_cw].set(conv_w[0])
    g0 = _all_gather(blk, "gather_c_convw", in_vmem=True)
    c_all = g0[:, 0, :]
    conv_w_full = g0[:, 1:1 + D_CONV, :n_cw].transpose(1, 0, 2).reshape(D_CONV, CONV_DIM)

    b_shard = lax.dynamic_slice(b_ada, (0, me * n_ada), (1, n_ada))
    mod_cols = _ada_fwd(c_all, w_ada[0], b_shard)
    g1 = _all_gather(mod_cols, "gather_mod", in_vmem=True)
    mod_mine = lax.dynamic_index_in_dim(g1, me, axis=1, keepdims=False)
    mod = mod_mine.reshape(6, 1, D_MODEL)

    big_shapes = [args[n].shape[1:] for n in BIG]
    late_shard = _pack_big([args[n][0] for n in BIG], bf16)
    wfull = _split_w_in(_all_gather(w_in[0].astype(bf16), "gather_w_in", in_vmem=False))
    wgrads = {n: jnp.zeros(wfull[n].shape, f32) for n in PROJ_NAMES}
    recv_like = jnp.zeros((N_DEV,) + late_shard.shape, bf16)

    small = {"norm1_w": norm1_w, "norm2_w": norm2_w, "q_norm_w": q_norm_w, "k_norm_w": k_norm_w,
             "conv_w": conv_w_full, "conv_b": conv_b, "A_log": A_log[0], "dt_bias": dt_bias[0], "ssd_D": ssd_D,
             "ssd_norm_w": ssd_norm_w}

    loss, (gx, gmod, gsmall, gw, recv_late) = jax.value_and_grad(local_loss, argnums=(0, 1, 2, 3, 4))(
        x[0], mod, small, wgrads, recv_like, wfull, late_shard, loss_target[0])

    small_list = [gmod, gsmall["norm1_w"], gsmall["norm2_w"], gsmall["q_norm_w"], gsmall["k_norm_w"], gsmall["conv_w"],
                  gsmall["conv_b"], gsmall["A_log"], gsmall["dt_bias"], gsmall["ssd_D"], gsmall["ssd_norm_w"],
                  loss.reshape(1)]
    small_shapes = [a.shape for a in small_list]
    g2 = _all_gather(_pack_small(small_list), "gather_small_grads", in_vmem=True)
    summed = _unpack_small(_sum_over_mesh(g2), small_shapes)
    loss_total = summed[-1][0]
    g_b_ada = summed[0].reshape(1, 6 * D_MODEL)
    g_small = dict(zip(SMALL, summed[1:-1]))
    g_conv_w = lax.dynamic_slice(g_small["conv_w"], (0, me * n_cw), (D_CONV, n_cw))

    dmod_all = g2[:, :6 * D_MODEL // LANE, :].reshape(N_DEV, 6 * D_MODEL)
    dmod_shard = lax.dynamic_slice(dmod_all, (0, me * n_ada), (N_DEV, n_ada))
    ada = _ada_bwd_adamw(c_all, dmod_shard, w_ada[0], m_w_ada[0], v_w_ada[0])

    small_grads = {"b_ada": g_b_ada, "norm1_w": g_small["norm1_w"], "norm2_w": g_small["norm2_w"],
                   "q_norm_w": g_small["q_norm_w"], "k_norm_w": g_small["k_norm_w"], "conv_w": g_conv_w[None],
                   "conv_b": g_small["conv_b"], "A_log": g_small["A_log"][None], "dt_bias": g_small["dt_bias"][None],
                   "ssd_D": g_small["ssd_D"], "ssd_norm_w": g_small["ssd_norm_w"]}
    sm_names = list(small_grads)
    sm_shapes = [args[n].shape for n in sm_names]
    sm = _adamw_small(_pack_small([args[n] for n in sm_names]), _pack_small([small_grads[n] for n in sm_names]),
                      _pack_small([args["m_" + n] for n in sm_names]), _pack_small([args["v_" + n] for n in sm_names]))
    sm_delta, sm_m, sm_v = [dict(zip(sm_names, _unpack_small(t, sm_shapes))) for t in sm]
    small_grads = {n: small_grads[n].reshape(args[n].shape) for n in sm_names}

    w_in_out = _sum_adamw(_scatter_blocks(_pack_in_grads(gw), "scatter_grads_w_in"), w_in[0], m_w_in[0], v_w_in[0],
                          "sum_adamw_w_in")
    big = _sum_adamw(recv_late, _pack_big([args[n][0] for n in BIG], f32),
                     _pack_big([args["m_" + n][0] for n in BIG], f32),
                     _pack_big([args["v_" + n][0] for n in BIG], f32), "sum_adamw")
    big_g, big_delta, big_m, big_v = [dict(zip(BIG, [t[None] for t in _unpack_big(p, big_shapes)])) for p in big]
    big_g["w_in"], big_delta["w_in"], big_m["w_in"], big_v["w_in"] = [t[None] for t in w_in_out]

    names = ("w_ada", "b_ada", "norm1_w", "norm2_w", "w_in", "q_norm_w", "k_norm_w", "conv_w", "conv_b", "A_log",
             "dt_bias", "ssd_D", "ssd_norm_w", "w_attn_out", "w_ssd_out", "w_o", "w_mlp1", "w_mlp2")
    grads, deltas, new_m, new_v = {}, {}, {}, {}
    for n in names:
        if n == "w_ada":
            grads[n], deltas[n], new_m[n], new_v[n] = [t[None] for t in ada]
        elif n in big_g:
            grads[n], deltas[n], new_m[n], new_v[n] = big_g[n], big_delta[n], big_m[n], big_v[n]
        else:
            grads[n], deltas[n], new_m[n], new_v[n] = small_grads[n], sm_delta[n], sm_m[n], sm_v[n]
    return (loss_total, gx[None], *[grads[n] for n in names], *[deltas[n] for n in names],
            *[new_m[n] for n in names], *[new_v[n] for n in names])
```

```python
import functools
import math

import jax
import jax.numpy as jnp
import numpy as np
from jax import lax
from jax.experimental import pallas as pl
from jax.experimental.pallas import tpu as pltpu

f32 = jnp.float32
bf16 = jnp.bfloat16
HIGHEST = lax.Precision.HIGHEST
MESH = pl.DeviceIdType.MESH

N_DEV = 8
D_MODEL = 1024
GRID_W = 64
N_Q_HEADS = 16
N_KV_HEADS = 4
HEAD_DIM = 64
ROPE_THETA = 10000.0
D_INNER = 2048
SSD_HEAD_DIM = 64
N_SSD_HEADS = 32
N_SSD_GROUPS = 4
D_STATE = 128
D_CONV = 5
CHUNK = 128
D_FF = 4096
EPS = 1e-6
CONV_DIM = D_INNER + 2 * N_SSD_GROUPS * D_STATE
GN = N_SSD_GROUPS * D_STATE
PROJ_NAMES = ("q", "k", "v", "xbc", "z", "dt", "ga", "gs")
PROJ_SIZES = (N_Q_HEADS * HEAD_DIM, N_KV_HEADS * HEAD_DIM, N_KV_HEADS * HEAD_DIM, CONV_DIM, D_INNER,
              2 * N_SSD_HEADS, D_MODEL, D_MODEL)
D_IN_PROJ = sum(PROJ_SIZES)
DT_PAD = 128

ADAM_LR, ADAM_B1, ADAM_B2, ADAM_EPS, ADAM_WD, ADAM_STEP = 0.001, 0.9, 0.999, 1e-08, 0.01, 10

V7X_VMEM_LIMIT = 56 * 1024 * 1024
LANE = 128
PACK_COLS = 1024


def _cparams(**kw):
    return pltpu.CompilerParams(vmem_limit_bytes=V7X_VMEM_LIMIT, **kw)


def _pick(dim, prefs):
    for p in prefs:
        if dim % p == 0:
            return p
    return dim


def _my_index():
    return 4 * lax.axis_index("x") + 2 * lax.axis_index("y") + lax.axis_index("c")


COMM_SEMS = [pltpu.SemaphoreType.DMA((7,)), pltpu.SemaphoreType.DMA((7,)), pltpu.SemaphoreType.DMA]


def _gather_phases(x_ref, out_ref, send_sems, recv_sems, local_sem):
    x, y, cc = lax.axis_index("x"), lax.axis_index("y"), lax.axis_index("c")
    me, sibling = (x, y, cc), (x, y, 1 - cc)
    chips = [(1 - x, y), (x, 1 - y), (1 - x, 1 - y)]

    def slot(px, py, pc):
        return out_ref.at[4 * px + 2 * py + pc]

    def copy(k, blk, to, src=None):
        return pltpu.make_async_remote_copy(
            src_ref=slot(*blk) if src is None else src, dst_ref=slot(*blk),
            send_sem=send_sems.at[k], recv_sem=recv_sems.at[k], device_id=to, device_id_type=MESH)

    mine = pltpu.make_async_copy(x_ref, slot(*me), local_sem)
    first = [copy(0, me, sibling, src=x_ref)]
    first += [copy(1 + j, me, (*chip, cc), src=x_ref) for j, chip in enumerate(chips)]
    passed = [copy(4 + j, (*chip, cc), sibling) for j, chip in enumerate(chips)]

    def start():
        mine.start()
        for cp in first:
            cp.start()

    def finish():
        for j, chip in enumerate(chips):
            copy(1 + j, (*chip, cc), me).wait_recv()
            passed[j].start()
        copy(0, sibling, me).wait_recv()
        for j, chip in enumerate(chips):
            copy(4 + j, (*chip, 1 - cc), me).wait_recv()
        for cp in first + passed:
            cp.wait_send()
        mine.wait()

    return start, finish


def _scatter_phases(g_ref, out_ref, send_sems, recv_sems, local_sem):
    x, y, cc = lax.axis_index("x"), lax.axis_index("y"), lax.axis_index("c")
    me = 4 * x + 2 * y + cc
    mine = pltpu.make_async_copy(g_ref.at[me], out_ref.at[me], local_sem)

    def copy(k):
        fx, fy, fc = (k >> 2) & 1, (k >> 1) & 1, k & 1
        px = x + fx - 2 * x * fx
        py = y + fy - 2 * y * fy
        pc = cc + fc - 2 * cc * fc
        peer = 4 * px + 2 * py + pc
        send = pltpu.make_async_remote_copy(
            src_ref=g_ref.at[peer], dst_ref=out_ref.at[me],
            send_sem=send_sems.at[k - 1], recv_sem=recv_sems.at[k - 1],
            device_id=(px, py, pc), device_id_type=MESH)
        recv = pltpu.make_async_remote_copy(
            src_ref=g_ref.at[peer], dst_ref=out_ref.at[peer],
            send_sem=send_sems.at[k - 1], recv_sem=recv_sems.at[k - 1],
            device_id=(px, py, pc), device_id_type=MESH)
        return send, recv

    pairs = [copy(k) for k in range(1, N_DEV)]

    def start():
        mine.start()
        for send, _ in pairs:
            send.start()

    def finish():
        for _, recv in pairs:
            recv.wait_recv()
        for send, _ in pairs:
            send.wait_send()
        mine.wait()

    return start, finish


def _all_gather(block, name, in_vmem):
    r, c = block.shape

    def body(x_ref, out_ref, send_sems, recv_sems, local_sem):
        start, finish = _gather_phases(x_ref, out_ref, send_sems, recv_sems, local_sem)
        start()
        finish()

    space = pltpu.VMEM if in_vmem else pl.ANY
    return pl.pallas_call(
        body, name=name,
        out_shape=jax.ShapeDtypeStruct((N_DEV, r, c), block.dtype),
        in_specs=[pl.BlockSpec(memory_space=space)],
        out_specs=pl.BlockSpec(memory_space=space),
        scratch_shapes=[pltpu.SemaphoreType.DMA((7,)), pltpu.SemaphoreType.DMA((7,)), pltpu.SemaphoreType.DMA],
    )(block)


def _scatter_blocks(g, name):
    _, r, c = g.shape

    def body(g_ref, out_ref, send_sems, recv_sems, local_sem):
        start, finish = _scatter_phases(g_ref, out_ref, send_sems, recv_sems, local_sem)
        start()
        finish()

    return pl.pallas_call(
        body, name=name,
        out_shape=jax.ShapeDtypeStruct(g.shape, g.dtype),
        in_specs=[pl.BlockSpec(memory_space=pl.ANY)],
        out_specs=pl.BlockSpec(memory_space=pl.ANY),
        scratch_shapes=[pltpu.SemaphoreType.DMA((7,)), pltpu.SemaphoreType.DMA((7,)), pltpu.SemaphoreType.DMA],
    )(g)


_DIMS = {"nn": (((1,), (0,)), ((), ())), "nt": (((1,), (1,)), ((), ())), "tn": (((0,), (0,)), ((), ()))}


def _matmul(a, b, mode, out_dtype, name, epilogue=None, side=None):
    if mode == "nn":
        (m, k), (_, n) = a.shape, b.shape
    elif mode == "nt":
        (m, k), (n, _) = a.shape, b.shape
    else:
        (k, m), (_, n) = a.shape, b.shape
    tm = _pick(m, (1024, 512, 256, 128))
    if mode == "tn":
        tn = _pick(n, (1536, 1024, 512, 256, 128))
        tk = _pick(k, (1024, 512, 256, 128))
    else:
        tn = _pick(n, (1024, 512, 384, 256, 128))
        tk = _pick(k, (1024, 512, 256, 128))
    nk = k // tk
    dims = _DIMS[mode]
    n_in = 3 if epilogue == "drelu2" else 2
    n_out = 2 if epilogue == "relu2" else 1

    def body(*refs):
        a_ref, b_ref = refs[:2]
        outs, acc_ref = refs[n_in:n_in + n_out], refs[n_in + n_out]
        kk = pl.program_id(2)
        part = lax.dot_general(a_ref[...].astype(bf16), b_ref[...].astype(bf16), dims, preferred_element_type=f32)

        def finish(acc):
            if epilogue == "relu2":
                r = jnp.maximum(acc, 0.0)
                outs[0][...] = acc.astype(out_dtype)
                outs[1][...] = (r * r).astype(out_dtype)
            elif epilogue == "drelu2":
                outs[0][...] = (acc * (2.0 * jnp.maximum(refs[2][...].astype(f32), 0.0))).astype(out_dtype)
            else:
                outs[0][...] = acc.astype(out_dtype)

        if nk == 1:
            finish(part)
        else:
            @pl.when(kk == 0)
            def _():
                acc_ref[...] = part

            @pl.when(kk > 0)
            def _():
                acc_ref[...] += part

            @pl.when(kk == nk - 1)
            def _():
                finish(acc_ref[...])

    if mode == "tn":
        a_spec = pl.BlockSpec((tk, tm), lambda i, j, kk: (kk, i))
    else:
        a_spec = pl.BlockSpec((tm, tk), lambda i, j, kk: (i, kk))
    if mode == "nt":
        b_spec = pl.BlockSpec((tn, tk), lambda i, j, kk: (j, kk))
    else:
        b_spec = pl.BlockSpec((tk, tn), lambda i, j, kk: (kk, j))
    o_spec = pl.BlockSpec((tm, tn), lambda i, j, kk: (i, j))
    o_shape = jax.ShapeDtypeStruct((m, n), out_dtype)
    res = pl.pallas_call(
        body, name=name, grid=(m // tm, n // tn, nk),
        in_specs=[a_spec, b_spec] + ([o_spec] if epilogue == "drelu2" else []),
        out_specs=[o_spec] * n_out, out_shape=[o_shape] * n_out,
        scratch_shapes=[pltpu.VMEM((tm, tn), f32)],
        compiler_params=_cparams(dimension_semantics=("parallel", "parallel", "arbitrary")),
    )(*((a, b, side) if epilogue == "drelu2" else (a, b)))
    return res if n_out == 2 else res[0]


@jax.custom_vjp
def mlp(h, w1, w1grad, w2, w2grad):
    _, r = _matmul(h, w1, "nn", bf16, "mlp1_fwd", epilogue="relu2")
    return _matmul(r, w2, "nn", f32, "mlp2_fwd")


def _mlp_fwd(h, w1, w1grad, w2, w2grad):
    u, r = _matmul(h, w1, "nn", bf16, "mlp1_fwd", epilogue="relu2")
    return _matmul(r, w2, "nn", f32, "mlp2_fwd"), (h, w1, w2, u, r)


def _mlp_bwd(res, dy):
    h, w1, w2, u, r = res
    du = _matmul(dy, w2, "nt", bf16, "mlp2_dgrad", epilogue="drelu2", side=u)
    dw2 = _matmul(r, dy, "tn", f32, "mlp2_wgrad")
    dh = _matmul(du, w1, "nt", h.dtype, "mlp1_dgrad")
    dw1 = _matmul(h, du, "tn", f32, "mlp1_wgrad")
    return dh, jnp.zeros_like(w1), dw1, jnp.zeros_like(w2), dw2


mlp.defvjp(_mlp_fwd, _mlp_bwd)


def make_linear(name):
    @jax.custom_vjp
    def linear(a, w, wgrad):
        return _matmul(a, w, "nn", f32, name + "_fwd")

    def fwd(a, w, wgrad):
        return linear(a, w, wgrad), (a, w)

    def bwd(res, dy):
        a, w = res
        da = _matmul(dy, w, "nt", a.dtype, name + "_dgrad")
        dw = _matmul(a, dy, "tn", f32, name + "_wgrad")
        return da, jnp.zeros_like(w), dw

    linear.defvjp(fwd, bwd)
    return linear


def make_rowwise(name, fn, row_out, sum_out=(), tm_pref=256):
    def specs(rows, gpars, cpars, consts, tm):
        s = [pl.BlockSpec((tm, r.shape[1]), lambda i: (i, 0)) for r in rows]
        s += [pl.BlockSpec(p.shape, lambda i: (0, 0)) for p in gpars]
        s += [pl.BlockSpec(p.shape, lambda i: (0, 0)) for p in cpars]
        for cst in consts:
            nb = cst.shape[0] // tm
            s.append(pl.BlockSpec((tm, cst.shape[1]), lambda i, nb=nb: (i % nb, 0)))
        return s

    def tile_rows(rows, consts):
        r = rows[0].shape[0]
        common = math.gcd(r, *[cst.shape[0] for cst in consts])
        tm = _pick(common, (tm_pref, 512, 256, 128, 64, 32, 16, 8))
        return r, tm

    def forward(rows, gpars, cpars, consts):
        r, tm = tile_rows(rows, consts)
        nr, ng, nc, nk = len(rows), len(gpars), len(cpars), len(consts)

        def body(*refs):
            ins = refs[:nr + ng + nc + nk]
            outs = refs[nr + ng + nc + nk:]
            rv = [t[...].astype(f32) for t in ins[:nr]]
            gv = [t[...].astype(f32) for t in ins[nr:nr + ng]]
            cv = [t[...] for t in ins[nr + ng:nr + ng + nc]]
            kv = [t[...].astype(f32) for t in ins[nr + ng + nc:]]
            ro, so = fn(rv, gv, cv, kv)
            for o_ref, val in zip(outs[:len(row_out)], ro):
                o_ref[...] = val.astype(o_ref.dtype)
            if sum_out:
                @pl.when(pl.program_id(0) == 0)
                def _():
                    for o_ref in outs[len(row_out):]:
                        o_ref[...] = jnp.zeros_like(o_ref)
                for o_ref, val in zip(outs[len(row_out):], so):
                    o_ref[...] += val

        out_specs = [pl.BlockSpec((tm, w), lambda i: (i, 0)) for w, _ in row_out]
        out_specs += [pl.BlockSpec(shp, lambda i: (0, 0)) for shp in sum_out]
        out_shape = [jax.ShapeDtypeStruct((r, w), dt) for w, dt in row_out]
        out_shape += [jax.ShapeDtypeStruct(shp, f32) for shp in sum_out]
        res = pl.pallas_call(
            body, name=name + "_fwd", grid=(r // tm,),
            in_specs=specs(rows, gpars, cpars, consts, tm), out_specs=out_specs, out_shape=out_shape,
            compiler_params=_cparams(dimension_semantics=("arbitrary",)),
        )(*rows, *gpars, *cpars, *consts)
        return tuple(res[:len(row_out)]), tuple(res[len(row_out):])

    def backward(rows, gpars, cpars, consts, d_ro, d_so):
        r, tm = tile_rows(rows, consts)
        nr, ng, nc, nk = len(rows), len(gpars), len(cpars), len(consts)
        n_in = nr + ng + nc + nk + len(row_out) + len(sum_out)

        def body(*refs):
            ins, outs = refs[:n_in], refs[n_in:]
            rv = [t[...].astype(f32) for t in ins[:nr]]
            gv = [t[...].astype(f32) for t in ins[nr:nr + ng]]
            cv = [t[...] for t in ins[nr + ng:nr + ng + nc]]
            kv = [t[...].astype(f32) for t in ins[nr + ng + nc:nr + ng + nc + nk]]
            o = nr + ng + nc + nk
            dro = [t[...].astype(f32) for t in ins[o:o + len(row_out)]]
            dso = [t[...] for t in ins[o + len(row_out):]]
            _, vjp = jax.vjp(lambda a, b: tuple(tuple(t) for t in fn(a, b, cv, kv)), rv, gv)
            drv, dgv = vjp((tuple(dro), tuple(dso)))
            for o_ref, val in zip(outs[:nr], drv):
                o_ref[...] = val.astype(o_ref.dtype)
            if ng:
                @pl.when(pl.program_id(0) == 0)
                def _():
                    for o_ref in outs[nr:]:
                        o_ref[...] = jnp.zeros_like(o_ref)
                for o_ref, val in zip(outs[nr:], dgv):
                    o_ref[...] += val

        in_specs = specs(rows, gpars, cpars, consts, tm)
        in_specs += [pl.BlockSpec((tm, w), lambda i: (i, 0)) for w, _ in row_out]
        in_specs += [pl.BlockSpec(shp, lambda i: (0, 0)) for shp in sum_out]
        out_specs = [pl.BlockSpec((tm, t.shape[1]), lambda i: (i, 0)) for t in rows]
        out_specs += [pl.BlockSpec(p.shape, lambda i: (0, 0)) for p in gpars]
        out_shape = [jax.ShapeDtypeStruct(t.shape, t.dtype) for t in rows]
        out_shape += [jax.ShapeDtypeStruct(p.shape, f32) for p in gpars]
        res = pl.pallas_call(
            body, name=name + "_bwd", grid=(r // tm,),
            in_specs=in_specs, out_specs=out_specs, out_shape=out_shape,
            compiler_params=_cparams(dimension_semantics=("arbitrary",)),
        )(*rows, *gpars, *cpars, *consts, *d_ro, *d_so)
        return tuple(res[:nr]), tuple(res[nr:])

    @jax.custom_vjp
    def op(rows, gpars, cpars, consts):
        return forward(rows, gpars, cpars, consts)

    def op_fwd(rows, gpars, cpars, consts):
        return forward(rows, gpars, cpars, consts), (rows, gpars, cpars, consts)

    def op_bwd(res, cts):
        rows, gpars, cpars, consts = res
        d_ro, d_so = cts
        drows, dg = backward(rows, gpars, cpars, consts, d_ro, d_so)
        dg = tuple(d.astype(p.dtype) for d, p in zip(dg, gpars))
        return (drows, dg, tuple(jnp.zeros_like(p) for p in cpars), tuple(jnp.zeros_like(k) for k in consts))

    op.defvjp(op_fwd, op_bwd)
    return op


def _rms(x):
    return x * lax.rsqrt(jnp.mean(x * x, axis=-1, keepdims=True) + EPS)


def _silu(x):
    return x * jax.nn.sigmoid(x)


def _fn_norm_mod(rows, gp, cp, ks):
    (x,), (nw, sc, sh) = rows, gp
    return ((_rms(x) * nw) * (1.0 + sc) + sh,), ()


PAIR = 2 * HEAD_DIM


def _exact_dot(a, m):
    hi = a.astype(bf16)
    lo = (a - hi.astype(f32)).astype(bf16)
    return jnp.dot(hi, m, preferred_element_type=f32) + jnp.dot(lo, m, preferred_element_type=f32)


def _make_sel_dot(sign):
    @jax.custom_vjp
    def sel_dot(a, m):
        return _exact_dot(a, m)

    def fwd(a, m):
        return _exact_dot(a, m), m

    def bwd(m, g):
        return sign * _exact_dot(g, m), jnp.zeros_like(m)

    sel_dot.defvjp(fwd, bwd)
    return sel_dot


_head_sum_dot = _make_sel_dot(1.0)
_rope_perm_dot = _make_sel_dot(-1.0)


def _pair_norm_rope(t, w2, gsum, perm, cos2, sin2, out_scale):
    ss = _head_sum_dot(t * t, gsum)
    u = t * lax.rsqrt(ss * (1.0 / HEAD_DIM) + EPS) * w2
    return (u * cos2 + _rope_perm_dot(u, perm) * sin2) * out_scale


def _pair_consts():
    eye = np.eye(2, dtype=np.float32)
    gsum = np.kron(eye, np.ones((HEAD_DIM, HEAD_DIM), np.float32))
    return jnp.asarray(gsum, bf16), jnp.asarray(np.kron(eye, _rope_perm()), bf16)


def make_head_rope(name, nh, out_scale, head_major):
    width = nh * HEAD_DIM
    fn = functools.partial(_pair_norm_rope, out_scale=out_scale)

    def out_spec(tm):
        if head_major:
            return pl.BlockSpec((nh, tm, HEAD_DIM), lambda i: (0, i, 0))
        return pl.BlockSpec((tm, width), lambda i: (i, 0))

    def specs(tm):
        def full(shp):
            return pl.BlockSpec(shp, lambda i: (0, 0))

        return [pl.BlockSpec((tm, width), lambda i: (i, 0)), full((1, PAIR)), full((PAIR, PAIR)), full((PAIR, PAIR)),
                pl.BlockSpec((tm, PAIR), lambda i: (i, 0)), pl.BlockSpec((tm, PAIR), lambda i: (i, 0))]

    def forward(t, w2, gsum, perm, cos2, sin2):
        s = t.shape[0]
        tm = _pick(s, (512, 256, 128))

        def body(t_ref, w_ref, g_ref, p_ref, cos_ref, sin_ref, o_ref):
            for b in range(nh // 2):
                val = fn(t_ref[:, b * PAIR:(b + 1) * PAIR], w_ref[...], g_ref[...], p_ref[...], cos_ref[...],
                         sin_ref[...]).astype(o_ref.dtype)
                if head_major:
                    o_ref[2 * b] = val[:, :HEAD_DIM]
                    o_ref[2 * b + 1] = val[:, HEAD_DIM:]
                else:
                    o_ref[:, b * PAIR:(b + 1) * PAIR] = val

        return pl.pallas_call(
            body, name=name + "_fwd", grid=(s // tm,), in_specs=specs(tm), out_specs=out_spec(tm),
            out_shape=jax.ShapeDtypeStruct((nh, s, HEAD_DIM) if head_major else (s, width), bf16),
            compiler_params=_cparams(dimension_semantics=("arbitrary",)),
        )(t, w2, gsum, perm, cos2, sin2)

    def backward(t, w2, gsum, perm, cos2, sin2, dout):
        s = t.shape[0]
        tm = _pick(s, (512, 256, 128))

        def body(t_ref, w_ref, g_ref, p_ref, cos_ref, sin_ref, do_ref, dt_ref, dw_ref, pair_buf):
            @pl.when(pl.program_id(0) == 0)
            def _():
                dw_ref[...] = jnp.zeros_like(dw_ref)

            g_v, p_v, cos_v, sin_v = g_ref[...], p_ref[...], cos_ref[...], sin_ref[...]
            dw = jnp.zeros((1, PAIR), f32)
            for b in range(nh // 2):
                sl = slice(b * PAIR, (b + 1) * PAIR)
                if head_major:
                    pair_buf[:, :HEAD_DIM] = do_ref[2 * b].astype(f32)
                    pair_buf[:, HEAD_DIM:] = do_ref[2 * b + 1].astype(f32)
                    ct = pair_buf[...]
                else:
                    ct = do_ref[:, sl].astype(f32)
                _, vjp = jax.vjp(lambda a, c: fn(a, c, g_v, p_v, cos_v, sin_v), t_ref[:, sl], w_ref[...])
                dtb, dwb = vjp(ct)
                dt_ref[:, sl] = dtb
                dw = dw + dwb
            dw_ref[...] += dw

        return pl.pallas_call(
            body, name=name + "_bwd", grid=(s // tm,), in_specs=specs(tm) + [out_spec(tm)],
            out_specs=[pl.BlockSpec((tm, width), lambda i: (i, 0)), pl.BlockSpec((1, PAIR), lambda i: (0, 0))],
            out_shape=[jax.ShapeDtypeStruct((s, width), f32), jax.ShapeDtypeStruct((1, PAIR), f32)],
            scratch_shapes=[pltpu.VMEM((tm, PAIR), f32)],
            compiler_params=_cparams(dimension_semantics=("arbitrary",)),
        )(t, w2, gsum, perm, cos2, sin2, dout)

    @jax.custom_vjp
    def op(t, w2, gsum, perm, cos2, sin2):
        return forward(t, w2, gsum, perm, cos2, sin2)

    def op_fwd(*args):
        return forward(*args), args

    def op_bwd(res, dout):
        dt, dw = backward(*res, dout)
        return (dt, dw) + tuple(jnp.zeros_like(r) for r in res[2:])

    op.defvjp(op_fwd, op_bwd)

    def apply(t, w, cos, sin):
        gsum, perm = _pair_consts()

        def two(v):
            return jnp.concatenate([v, v], axis=-1)

        return op(t, two(w), gsum, perm, two(cos), two(sin))

    return apply


def _fn_softplus(rows, gp, cp, ks):
    (x,), (b,) = rows, gp
    v = x + b
    return (jnp.maximum(v, 0.0) + jnp.log(1.0 + jnp.exp(-jnp.abs(v))),), ()


def _fn_ssd_gate(rows, gp, cp, ks):
    (y, z), (nw,) = rows, gp
    return (_rms(y * _silu(z)) * nw,), ()


def _fn_merge(rows, gp, cp, ks):
    ao, so, ga, gs = rows
    return (jax.nn.sigmoid(ga) * ao + jax.nn.sigmoid(gs) * so,), ()


def _fn_res_norm(rows, gp, cp, ks):
    (x, mo), (g1, nw, sc, sh) = rows, gp
    x1 = x + g1 * mo
    return (x1, (_rms(x1) * nw) * (1.0 + sc) + sh), ()


def _fn_loss(rows, gp, cp, ks):
    (x1, ff), (g2,), (tgt,) = rows, gp, ks
    err = x1 + g2 * ff - tgt
    return (), (0.5 * jnp.sum(jnp.sum(err * err, axis=-1, keepdims=True), axis=0, keepdims=True) / D_MODEL,)


HALO = 8


def _conv_tiles(s, c):
    return _pick(s, (512, 256, 128)), _pick(c, (512, 256, 128))


def _halo_specs(tm, tc, s):
    nb = tm // HALO
    last = s // HALO - 1
    cur = pl.BlockSpec((tm, tc), lambda j, i: (i, j))
    prev = pl.BlockSpec((HALO, tc), lambda j, i: (jnp.maximum(i * nb - 1, 0), j))
    nxt = pl.BlockSpec((HALO, tc), lambda j, i: (jnp.minimum((i + 1) * nb, last), j))
    return cur, prev, nxt


def _fill_halo(buf, cur, prev, nxt, tm, i, n_i):
    buf[HALO:HALO + tm, :] = cur[...]
    buf[0:HALO, :] = jnp.where(i > 0, prev[...], 0.0)
    buf[HALO + tm:, :] = jnp.where(i < n_i - 1, nxt[...], 0.0)


def _conv_fwd(x, w, b, shard):
    s, c = x.shape
    tm, tc = _conv_tiles(s, c)
    n_i, n_j = s // tm, c // tc

    def body(cur, prev, nxt, w_ref, b_ref, shard_ref, o_ref, gath_ref, buf, send_sems, recv_sems, local_sem):
        j, i = pl.program_id(0), pl.program_id(1)
        start, finish = _gather_phases(shard_ref, gath_ref, send_sems, recv_sems, local_sem)

        @pl.when((j == 0) & (i == 0))
        def _():
            start()

        _fill_halo(buf, cur, prev, nxt, tm, i, n_i)
        pre = jnp.zeros((tm, tc), f32) + b_ref[...]
        for k in range(D_CONV):
            pre = pre + buf[HALO - 2 + k:HALO - 2 + k + tm, :] * w_ref[k:k + 1, :]
        o_ref[...] = _silu(pre)

        @pl.when((j == n_j - 1) & (i == n_i - 1))
        def _():
            finish()

    cur, prev, nxt = _halo_specs(tm, tc, s)
    hbm = pl.BlockSpec(memory_space=pl.ANY)
    return pl.pallas_call(
        body, name="conv_silu_fwd", grid=(n_j, n_i),
        in_specs=[cur, prev, nxt, pl.BlockSpec((D_CONV, tc), lambda j, i: (0, j)),
                  pl.BlockSpec((1, tc), lambda j, i: (0, j)), hbm],
        out_specs=[pl.BlockSpec((tm, tc), lambda j, i: (i, j)), hbm],
        out_shape=[jax.ShapeDtypeStruct((s, c), f32), jax.ShapeDtypeStruct((N_DEV,) + shard.shape, shard.dtype)],
        scratch_shapes=[pltpu.VMEM((tm + 2 * HALO, tc), f32)] + COMM_SEMS,
        compiler_params=_cparams(dimension_semantics=("arbitrary", "arbitrary")),
    )(x, x, x, w, b, shard)


def _conv_bwd(x, w, b, dy, g):
    s, c = x.shape
    tm, tc = _conv_tiles(s, c)
    n_i, n_j = s // tm, c // tc
    ext = tm + 8

    def body(cur, prev, nxt, dcur, dprev, dnxt, w_ref, b_ref, g_ref, dx_ref, dw_ref, db_ref, recv_ref,
             xbuf, dbuf, pbuf, send_sems, recv_sems, local_sem):
        j, i = pl.program_id(0), pl.program_id(1)
        start, finish = _scatter_phases(g_ref, recv_ref, send_sems, recv_sems, local_sem)

        @pl.when((j == 0) & (i == 0))
        def _():
            start()

        _fill_halo(xbuf, cur, prev, nxt, tm, i, n_i)
        _fill_halo(dbuf, dcur, dprev, dnxt, tm, i, n_i)
        pre = jnp.zeros((ext, tc), f32) + b_ref[...]
        for k in range(D_CONV):
            pre = pre + xbuf[2 + k:2 + k + ext, :] * w_ref[k:k + 1, :]
        sg = jax.nn.sigmoid(pre)
        pbuf[...] = dbuf[4:4 + ext, :] * (sg * (1.0 + pre * (1.0 - sg)))
        dx = jnp.zeros((tm, tc), f32)
        for k in range(D_CONV):
            dx = dx + pbuf[6 - k:6 - k + tm, :] * w_ref[k:k + 1, :]
        dx_ref[...] = dx

        @pl.when(i == 0)
        def _():
            dw_ref[...] = jnp.zeros_like(dw_ref)
            db_ref[...] = jnp.zeros_like(db_ref)

        dpre = pbuf[4:4 + tm, :]
        db_ref[...] += jnp.sum(dpre, axis=0, keepdims=True)
        for k in range(D_CONV):
            dw_ref[k:k + 1, :] += jnp.sum(dpre * xbuf[HALO - 2 + k:HALO - 2 + k + tm, :], axis=0, keepdims=True)

        @pl.when((j == n_j - 1) & (i == n_i - 1))
        def _():
            finish()

    cur, prev, nxt = _halo_specs(tm, tc, s)
    hbm = pl.BlockSpec(memory_space=pl.ANY)
    return pl.pallas_call(
        body, name="conv_silu_bwd", grid=(n_j, n_i),
        in_specs=[cur, prev, nxt, cur, prev, nxt, pl.BlockSpec((D_CONV, tc), lambda j, i: (0, j)),
                  pl.BlockSpec((1, tc), lambda j, i: (0, j)), hbm],
        out_specs=[pl.BlockSpec((tm, tc), lambda j, i: (i, j)), pl.BlockSpec((D_CONV, tc), lambda j, i: (0, j)),
                   pl.BlockSpec((1, tc), lambda j, i: (0, j)), hbm],
        out_shape=[jax.ShapeDtypeStruct((s, c), f32), jax.ShapeDtypeStruct((D_CONV, c), f32),
                   jax.ShapeDtypeStruct((1, c), f32), jax.ShapeDtypeStruct(g.shape, g.dtype)],
        scratch_shapes=[pltpu.VMEM((tm + 2 * HALO, tc), f32), pltpu.VMEM((tm + 2 * HALO, tc), f32),
                        pltpu.VMEM((ext, tc), f32)] + COMM_SEMS,
        compiler_params=_cparams(dimension_semantics=("arbitrary", "arbitrary")),
    )(x, x, x, dy, dy, dy, w, b, g)


@jax.custom_vjp
def conv_silu_comm(x, w, b, shard, recv_like):
    act, gathered = _conv_fwd(x, w, b, shard)
    return (act, gathered) + tuple(jnp.zeros(shp, f32) for shp in LATE_SHAPES)


def _conv_silu_comm_fwd(x, w, b, shard, recv_like):
    return conv_silu_comm(x, w, b, shard, recv_like), (x, w, b, shard)


def _conv_silu_comm_bwd(res, cts):
    x, w, b, shard = res
    dx, dw, db, recv = _conv_bwd(x, w, b, cts[0], _pack_late_grads(dict(zip(LATE, cts[2:]))))
    return dx, dw, db, jnp.zeros_like(shard), recv


conv_silu_comm.defvjp(_conv_silu_comm_fwd, _conv_silu_comm_bwd)


ATT_SCALE = HEAD_DIM ** -0.5
Q_SCALE = ATT_SCALE * math.log2(math.e)
LN2 = math.log(2.0)
REP = N_Q_HEADS // N_KV_HEADS


HP = 2
assert REP % HP == 0


def _attn_fwd(q, k, v):
    s, dh = q.shape[0], HEAD_DIM
    hq = q.shape[1] // dh
    tq = _pick(s, (256, 128))

    v1 = jnp.concatenate([v, jnp.ones(v.shape[:2] + (1,), v.dtype), jnp.zeros(v.shape[:2] + (dh - 1,), v.dtype)],
                         axis=-1)

    def body(q_ref, k_ref, v_ref, o_ref, p_ref, linv_ref):
        for j in range(HP):
            sl = slice(j * dh, (j + 1) * dh)
            sc = lax.dot_general(q_ref[:, sl], k_ref[0], _DIMS["nt"], preferred_element_type=f32)
            m = jnp.max(sc, axis=-1, keepdims=True)
            p = jnp.exp2(sc - m).astype(bf16)
            p_ref[j] = p
            o1 = jnp.dot(p, v_ref[0], preferred_element_type=f32)
            linv = 1.0 / o1[:, dh:dh + 1]
            o_ref[:, sl] = (o1[:, :dh] * linv).astype(o_ref.dtype)
            linv_ref[j] = linv

    return pl.pallas_call(
        body, name="attn_fwd", grid=(hq // HP, s // tq),
        in_specs=[pl.BlockSpec((tq, HP * dh), lambda h, i: (i, h)),
                  pl.BlockSpec((1, s, dh), lambda h, i: (h * HP // REP, 0, 0)),
                  pl.BlockSpec((1, s, 2 * dh), lambda h, i: (h * HP // REP, 0, 0))],
        out_specs=[pl.BlockSpec((tq, HP * dh), lambda h, i: (i, h)),
                   pl.BlockSpec((HP, tq, s), lambda h, i: (h, i, 0)),
                   pl.BlockSpec((HP, tq, 1), lambda h, i: (h, i, 0))],
        out_shape=[jax.ShapeDtypeStruct((s, hq * dh), bf16), jax.ShapeDtypeStruct((hq, s, s), bf16),
                   jax.ShapeDtypeStruct((hq, s, 1), f32)],
        compiler_params=_cparams(dimension_semantics=("parallel", "arbitrary")),
    )(q, k, v1)


def _attn_bwd(p, do, o, q, k, v, linv):
    hq, s, _ = p.shape
    dh = HEAD_DIM
    tq = _pick(s, (256, 128))

    def body(p_ref, do_ref, o_ref, q_ref, k_ref, v_ref, linv_ref, dq_ref, dkt_ref, dvt_ref):
        @pl.when(pl.program_id(1) == 0)
        def _():
            dkt_ref[...] = jnp.zeros_like(dkt_ref)
            dvt_ref[...] = jnp.zeros_like(dvt_ref)

        for j in range(HP):
            sl = slice(j * dh, (j + 1) * dh)
            pp, doh, li = p_ref[j], do_ref[:, sl], linv_ref[j]
            do32 = doh.astype(f32)
            d = jnp.sum(do32 * o_ref[:, sl].astype(f32), axis=-1, keepdims=True)
            dp = lax.dot_general(doh, v_ref[0], _DIMS["nt"], preferred_element_type=f32)
            ds = (pp.astype(f32) * ((dp - d) * li)).astype(bf16)
            dq_ref[:, sl] = jnp.dot(ds, k_ref[0], preferred_element_type=f32) * LN2
            dvt_ref[j] += lax.dot_general((do32 * li).astype(bf16), pp, _DIMS["tn"], preferred_element_type=f32)
            dkt_ref[j] += lax.dot_general(q_ref[:, sl], ds, _DIMS["tn"], preferred_element_type=f32)

    def row():
        return pl.BlockSpec((tq, HP * dh), lambda h, i: (i, h))

    return pl.pallas_call(
        body, name="attn_bwd", grid=(hq // HP, s // tq),
        in_specs=[pl.BlockSpec((HP, tq, s), lambda h, i: (h, i, 0)), row(), row(), row(),
                  pl.BlockSpec((1, s, dh), lambda h, i: (h * HP // REP, 0, 0)),
                  pl.BlockSpec((1, s, dh), lambda h, i: (h * HP // REP, 0, 0)),
                  pl.BlockSpec((HP, tq, 1), lambda h, i: (h, i, 0))],
        out_specs=[row(), pl.BlockSpec((HP, dh, s), lambda h, i: (h, 0, 0)),
                   pl.BlockSpec((HP, dh, s), lambda h, i: (h, 0, 0))],
        out_shape=[jax.ShapeDtypeStruct((s, hq * dh), f32), jax.ShapeDtypeStruct((hq, dh, s), f32),
                   jax.ShapeDtypeStruct((hq, dh, s), f32)],
        compiler_params=_cparams(dimension_semantics=("parallel", "arbitrary")),
    )(p, do, o, q, k, v, linv)


@jax.custom_vjp
def attention(q, k, v):
    return _attn_fwd(q, k, v)[0]


def _attention_fwd(q, k, v):
    o, p, linv = _attn_fwd(q, k, v)
    return o, (q, k, v, o, p, linv)


def _attention_bwd(res, do):
    q, k, v, o, p, linv = res
    s = q.shape[0]
    dq, dkt, dvt = _attn_bwd(p, do.astype(bf16), o, q, k, v, linv)

    def per_kv_head(t):
        return jnp.swapaxes(t.reshape(N_KV_HEADS, REP, HEAD_DIM, s).sum(axis=1), 1, 2)

    return dq.astype(q.dtype), (per_kv_head(dkt) * LN2).astype(k.dtype), per_kv_head(dvt).astype(v.dtype)


attention.defvjp(_attention_fwd, _attention_bwd)


HPG = N_SSD_HEADS // N_SSD_GROUPS
GW = HPG * SSD_HEAD_DIM
NEG = -1e30
SPLIT_ROWS = 32


def _ssd_consts():
    k = np.arange(SPLIT_ROWS)[:, None]
    live = k < 3 * HPG
    sel_chunk = ((k % HPG) == (np.arange(HPG * CHUNK)[None, :] // CHUNK)) & live
    sel_head = ((k % HPG) == (np.arange(GW)[None, :] // SSD_HEAD_DIM)) & live
    return jnp.asarray(sel_chunk, bf16), jnp.asarray(sel_head, bf16)


def _split3(x):
    hi = x.astype(bf16).astype(f32)
    r1 = x - hi
    mid = r1.astype(bf16).astype(f32)
    lo = (r1 - mid).astype(bf16).astype(f32)
    return jnp.concatenate([hi, mid, lo, jnp.zeros_like(hi)], axis=0).astype(bf16)


def _tn(a, b):
    return lax.dot_general(a, b, _DIMS["tn"], preferred_element_type=f32)


def _nt(a, b):
    return lax.dot_general(a, b, _DIMS["nt"], preferred_element_type=f32)


def _nn(a, b):
    return jnp.dot(a, b, preferred_element_type=f32)


def _head_sum(sel8, x):
    hi = x.astype(bf16)
    lo = (x - hi.astype(f32)).astype(bf16)
    return _nt(sel8, hi) + _nt(sel8, lo)


def _ssd_masks(reverse):
    r = lax.broadcasted_iota(jnp.int32, (CHUNK, CHUNK), 0)
    c = lax.broadcasted_iota(jnp.int32, (CHUNK, CHUNK), 1)
    lower, upper = r >= c, r <= c
    return (upper, lower) if reverse else (lower, upper)


def _ssd_in_specs(cidx):
    return [pl.BlockSpec((CHUNK, D_INNER), lambda c: (cidx(c), 0)),
            pl.BlockSpec((CHUNK, GN), lambda c: (cidx(c), D_INNER // GN)),
            pl.BlockSpec((CHUNK, GN), lambda c: (cidx(c), D_INNER // GN + 1)),
            pl.BlockSpec((N_SSD_HEADS, CHUNK), lambda c: (0, cidx(c))),
            pl.BlockSpec((N_SSD_HEADS, 1), lambda c: (0, 0)),
            pl.BlockSpec((SPLIT_ROWS, HPG * CHUNK), lambda c: (0, 0)),
            pl.BlockSpec((SPLIT_ROWS, GW), lambda c: (0, 0))]


def _ssd_chunk_common(dtt_ref, a_ref, et_ref, mask_t):
    dtt = dtt_ref[...]
    et = jnp.dot(dtt * a_ref[...], mask_t.astype(f32), precision=HIGHEST, preferred_element_type=f32)
    et_ref[...] = et
    return dtt, et


def _ssd_group_common(g, dtt, et, selc_ref, selh_ref, xs_ref, b_ref, c_ref, last):
    gr = slice(g * HPG, (g + 1) * HPG)
    e3 = _split3(et[gr])
    col = _tn(e3, selc_ref[...])
    eb = _tn(e3, selh_ref[...])
    dtb = _tn(_split3(dtt[gr]), selh_ref[...])
    tbc = eb[last:last + 1, :]
    xs = xs_ref[:, g * GW:(g + 1) * GW]
    bg = b_ref[:, g * D_STATE:(g + 1) * D_STATE].astype(bf16)
    cg = c_ref[:, g * D_STATE:(g + 1) * D_STATE].astype(bf16)
    return col, eb, dtb, tbc, xs, bg, cg


def _ssd_fwd(xbc, dtt, a_col, reverse, y_prev=None, dexp=None):
    s = xbc.shape[0]
    nc = s // CHUNK
    cidx = (lambda c: nc - 1 - c) if reverse else (lambda c: c)
    last = 0 if reverse else CHUNK - 1
    selc, selh = _ssd_consts()
    final = y_prev is not None
    n_in = 9 if final else 7

    def body(*refs):
        xs_ref, b_ref, c_ref, dtt_ref, a_ref, selc_ref, selh_ref = refs[:7]
        y_ref, st_ref, ht_ref, et_ref = refs[n_in:]

        @pl.when(pl.program_id(0) == 0)
        def _():
            ht_ref[...] = jnp.zeros_like(ht_ref)

        mask, mask_t = _ssd_masks(reverse)
        dtt_v, et = _ssd_chunk_common(dtt_ref, a_ref, et_ref, mask_t)
        for g in range(N_SSD_GROUPS):
            col, eb, dtb, tbc, xs, bg, cg = _ssd_group_common(g, dtt_v, et, selc_ref, selh_ref, xs_ref, b_ref, c_ref,
                                                              last)
            xd = xs * dtb
            cb = _nt(cg, bg)
            ht = ht_ref[g]
            st_ref[0, g] = ht
            yoff = _nn(cg, ht.astype(bf16)) * jnp.exp(eb)
            for j in range(HPG):
                h = g * HPG + j
                hs = slice(j * SSD_HEAD_DIM, (j + 1) * SSD_HEAD_DIM)
                lam = jnp.exp(jnp.where(mask, col[:, j * CHUNK:(j + 1) * CHUNK] - et_ref[h:h + 1, :], NEG))
                yj = _nn((cb * lam).astype(bf16), xd[:, hs].astype(bf16)) + yoff[:, hs]
                cols = slice(g * GW + j * SSD_HEAD_DIM, g * GW + (j + 1) * SSD_HEAD_DIM)
                if final:
                    yj = yj + refs[7][:, cols] + xs[:, hs] * refs[8][:, cols]
                y_ref[:, cols] = yj
            ht_ref[g] = jnp.exp(tbc) * ht + _tn(bg, (xd * jnp.exp(tbc - eb)).astype(bf16))

    y_spec = pl.BlockSpec((CHUNK, D_INNER), lambda c: (cidx(c), 0))
    extra_specs = [y_spec, pl.BlockSpec((1, D_INNER), lambda c: (0, 0))] if final else []
    return pl.pallas_call(
        body, name="ssd_fwd_rev" if reverse else "ssd_fwd", grid=(nc,),
        in_specs=_ssd_in_specs(cidx) + extra_specs,
        out_specs=[y_spec, pl.BlockSpec((1, N_SSD_GROUPS, D_STATE, GW), lambda c: (cidx(c), 0, 0, 0))],
        out_shape=[jax.ShapeDtypeStruct((s, D_INNER), f32),
                   jax.ShapeDtypeStruct((nc, N_SSD_GROUPS, D_STATE, GW), f32)],
        scratch_shapes=[pltpu.VMEM((N_SSD_GROUPS, D_STATE, GW), f32), pltpu.VMEM((N_SSD_HEADS, CHUNK), f32)],
        compiler_params=_cparams(dimension_semantics=("arbitrary",)),
    )(xbc, xbc, xbc, dtt, a_col, selc, selh, *((y_prev, dexp) if final else ()))


def _ssd_bwd(xbc, dtt, a_col, states, dy, reverse, dxbc_prev=None, dexp=None):
    s = xbc.shape[0]
    nc = s // CHUNK
    cidx = (lambda c: c) if reverse else (lambda c: nc - 1 - c)
    last = 0 if reverse else CHUNK - 1
    selc, selh = _ssd_consts()
    final = dxbc_prev is not None
    n_in = 11 if final else 9
    n_out = 4 if final else 3

    def body(*refs):
        xs_ref, b_ref, c_ref, dtt_ref, a_ref, selc_ref, selh_ref, st_ref, dy_ref = refs[:9]
        dxbc_ref, ddtt_ref, da_ref = refs[n_in:n_in + 3]
        dh_ref, et_ref, det_ref, det2_ref, ddt_ref, q_ref = refs[n_in + n_out:]
        if final:
            prev_ref, dexp_ref, ddexp_ref = refs[9], refs[10], refs[n_in + 3]

        @pl.when(pl.program_id(0) == 0)
        def _():
            dh_ref[...] = jnp.zeros_like(dh_ref)
            da_ref[...] = jnp.zeros_like(da_ref)
            if final:
                ddexp_ref[...] = jnp.zeros_like(ddexp_ref)

        mask, mask_t = _ssd_masks(reverse)
        dtt_v, et = _ssd_chunk_common(dtt_ref, a_ref, et_ref, mask_t)
        sel8 = selh_ref[0:HPG, :]
        is_last = lax.broadcasted_iota(jnp.int32, (CHUNK, GW), 0) == last
        for g in range(N_SSD_GROUPS):
            col, eb, dtb, tbc, xs, bg, cg = _ssd_group_common(g, dtt_v, et, selc_ref, selh_ref, xs_ref, b_ref, c_ref,
                                                              last)
            xd = xs * dtb
            cb = _nt(cg, bg)
            cbt = _nt(bg, cg)
            exp_t = jnp.exp(tbc)
            dfac = jnp.exp(tbc - eb)
            ht = st_ref[0, g]
            dhn = dh_ref[g]
            ht16, dhn16 = ht.astype(bf16), dhn.astype(bf16)
            dy = dy_ref[:, g * GW:(g + 1) * GW]
            dye = dy * jnp.exp(eb)
            dye16 = dye.astype(bf16)
            dc = _nt(dye16, ht16)
            dh_ref[g] = exp_t * dhn + _tn(cg, dye16)
            deb = dye * _nn(cg, ht16)
            xdd = xd * dfac
            dxdd = _nn(bg, dhn16)
            db = _nt(xdd.astype(bf16), dhn16)
            dxd_state = dxdd * dfac
            ddf = dxdd * xdd
            dtbc = jnp.sum(ddf, axis=0, keepdims=True) + exp_t * jnp.sum(dhn * ht, axis=0, keepdims=True)
            deb = deb - ddf + jnp.where(is_last, dtbc, 0.0)
            dcb = jnp.zeros((CHUNK, CHUNK), f32)
            dcbt = jnp.zeros((CHUNK, CHUNK), f32)
            for j in range(HPG):
                h = g * HPG + j
                hs = slice(j * SSD_HEAD_DIM, (j + 1) * SSD_HEAD_DIM)
                colj = col[:, j * CHUNK:(j + 1) * CHUNK]
                row = et_ref[h:h + 1, :]
                lam = jnp.exp(jnp.where(mask, colj - row, NEG))
                lam_t = jnp.exp(jnp.where(mask_t, row - colj, NEG))
                xdj, dyj = xd[:, hs].astype(bf16), dy[:, hs].astype(bf16)
                t1 = _nt(dyj, xdj) * lam
                t2 = _nt(xdj, dyj) * lam_t
                dcb, dcbt = dcb + t1, dcbt + t2
                det_ref[h:h + 1, :] = -jnp.sum(t1 * cb - t2 * cbt, axis=0, keepdims=True)
                dxdj = _nn((cbt * lam_t).astype(bf16), dyj) + dxd_state[:, hs]
                cols = slice(g * GW + j * SSD_HEAD_DIM, g * GW + (j + 1) * SSD_HEAD_DIM)
                dxs = dxdj * dtb[:, hs]
                if final:
                    dxs = dxs + prev_ref[:, cols] + dy[:, hs] * dexp_ref[:, cols]
                dxbc_ref[:, cols] = dxs
                q_ref[:, hs] = dxdj * xs[:, hs]
            b_cols = slice(D_INNER + g * D_STATE, D_INNER + (g + 1) * D_STATE)
            c_cols = slice(D_INNER + GN + g * D_STATE, D_INNER + GN + (g + 1) * D_STATE)
            db = db + _nn(dcbt.astype(bf16), cg)
            dc = dc + _nn(dcb.astype(bf16), bg)
            if final:
                db, dc = db + prev_ref[:, b_cols], dc + prev_ref[:, c_cols]
                ddexp_ref[:, g * GW:(g + 1) * GW] += jnp.sum(dy * xs, axis=0, keepdims=True)
            dxbc_ref[:, b_cols] = db
            dxbc_ref[:, c_cols] = dc
            det2_ref[g * HPG:(g + 1) * HPG, :] = _head_sum(sel8, deb)
            ddt_ref[g * HPG:(g + 1) * HPG, :] = _head_sum(sel8, q_ref[...])
        dat = jnp.dot(det_ref[...] + det2_ref[...], mask.astype(f32), precision=HIGHEST, preferred_element_type=f32)
        ddtt_ref[...] = ddt_ref[...] + dat * a_ref[...]
        da_ref[...] += jnp.sum(dat * dtt_v, axis=1, keepdims=True)

    in_specs = _ssd_in_specs(cidx) + [
        pl.BlockSpec((1, N_SSD_GROUPS, D_STATE, GW), lambda c: (cidx(c), 0, 0, 0)),
        pl.BlockSpec((CHUNK, D_INNER), lambda c: (cidx(c), 0))]
    hl = pltpu.VMEM((N_SSD_HEADS, CHUNK), f32)
    dxbc_spec = pl.BlockSpec((CHUNK, CONV_DIM), lambda c: (cidx(c), 0))
    dexp_spec = pl.BlockSpec((1, D_INNER), lambda c: (0, 0))
    return pl.pallas_call(
        body, name="ssd_bwd_rev" if reverse else "ssd_bwd", grid=(nc,),
        in_specs=in_specs + ([dxbc_spec, dexp_spec] if final else []),
        out_specs=[dxbc_spec, pl.BlockSpec((N_SSD_HEADS, CHUNK), lambda c: (0, cidx(c))),
                   pl.BlockSpec((N_SSD_HEADS, 1), lambda c: (0, 0))] + ([dexp_spec] if final else []),
        out_shape=[jax.ShapeDtypeStruct((s, CONV_DIM), f32), jax.ShapeDtypeStruct((N_SSD_HEADS, s), f32),
                   jax.ShapeDtypeStruct((N_SSD_HEADS, 1), f32)]
        + ([jax.ShapeDtypeStruct((1, D_INNER), f32)] if final else []),
        scratch_shapes=[pltpu.VMEM((N_SSD_GROUPS, D_STATE, GW), f32), hl, hl, hl, hl, pltpu.VMEM((CHUNK, GW), f32)],
        compiler_params=_cparams(dimension_semantics=("arbitrary",)),
    )(xbc, xbc, xbc, dtt, a_col, selc, selh, states, dy, *((dxbc_prev, dexp) if final else ()))


@jax.custom_vjp
def ssd_bidir(xbc, dtt, a_col, dexp):
    y_f, _ = _ssd_fwd(xbc, dtt[:N_SSD_HEADS], a_col[:N_SSD_HEADS], False)
    return _ssd_fwd(xbc, dtt[N_SSD_HEADS:], a_col[N_SSD_HEADS:], True, y_prev=y_f, dexp=dexp)[0]


def _ssd_bidir_fwd(xbc, dtt, a_col, dexp):
    y_f, st_f = _ssd_fwd(xbc, dtt[:N_SSD_HEADS], a_col[:N_SSD_HEADS], False)
    y, st_b = _ssd_fwd(xbc, dtt[N_SSD_HEADS:], a_col[N_SSD_HEADS:], True, y_prev=y_f, dexp=dexp)
    return y, (xbc, dtt, a_col, dexp, st_f, st_b)


def _ssd_bidir_bwd(res, dy):
    xbc, dtt, a_col, dexp, st_f, st_b = res
    dxbc_f, ddtt_f, da_f = _ssd_bwd(xbc, dtt[:N_SSD_HEADS], a_col[:N_SSD_HEADS], st_f, dy, False)
    dxbc, ddtt_b, da_b, ddexp = _ssd_bwd(xbc, dtt[N_SSD_HEADS:], a_col[N_SSD_HEADS:], st_b, dy, True,
                                         dxbc_prev=dxbc_f, dexp=dexp)
    return dxbc, jnp.concatenate([ddtt_f, ddtt_b], axis=0), jnp.concatenate([da_f, da_b], axis=0), ddexp


ssd_bidir.defvjp(_ssd_bidir_fwd, _ssd_bidir_bwd)


W_NAMES = PROJ_NAMES + ("attn_out", "ssd_out", "o", "mlp1", "mlp2")


def _rope_tables(s):
    rows = s // GRID_W
    pos_row = jnp.repeat(jnp.arange(rows, dtype=jnp.int32), GRID_W).astype(f32)
    pos_col = jnp.tile(jnp.arange(GRID_W, dtype=jnp.int32), rows).astype(f32)
    axis_dim = HEAD_DIM // 2
    inv_freq = ROPE_THETA ** (-jnp.arange(0, axis_dim, 2, dtype=f32) / axis_dim)
    ang_r = pos_row[:, None] * inv_freq[None, :]
    ang_c = pos_col[:, None] * inv_freq[None, :]
    cos = jnp.concatenate([jnp.cos(ang_r), jnp.cos(ang_r), jnp.cos(ang_c), jnp.cos(ang_c)], axis=-1)
    sin = jnp.concatenate([jnp.sin(ang_r), jnp.sin(ang_r), jnp.sin(ang_c), jnp.sin(ang_c)], axis=-1)
    return cos, sin


def _rope_perm():
    p = np.zeros((HEAD_DIM, HEAD_DIM), np.float32)
    for j in range(HEAD_DIM):
        if (j % 32) < 16:
            p[j + 16, j] = -1.0
        else:
            p[j - 16, j] = 1.0
    return p


def local_loss(x, mod, small, wgrads, recv_like, wfull, late_shard, target):
    s = x.shape[0]
    lin = {n: make_linear("lin_" + n) for n in W_NAMES if not n.startswith("mlp")}
    wfull, wgrads = dict(wfull), dict(wgrads)
    shift1, scale1, gate1, shift2, scale2, gate2 = [mod[i] for i in range(6)]

    norm_mod = make_rowwise("norm_mod", _fn_norm_mod, [(D_MODEL, bf16)])
    (h,), _ = norm_mod((x,), (small["norm1_w"], scale1, shift1), (), ())

    proj = {n: lin[n](h, wfull[n], wgrads[n]) for n in PROJ_NAMES}

    cos, sin = _rope_tables(s)

    def heads(t, nh):
        return t.reshape(s, nh, HEAD_DIM).transpose(1, 0, 2)

    qr = make_head_rope("q_norm_rope", N_Q_HEADS, Q_SCALE, False)(proj["q"], small["q_norm_w"], cos, sin)
    kr = make_head_rope("k_norm_rope", N_KV_HEADS, 1.0, True)(proj["k"], small["k_norm_w"], cos, sin)
    vh = heads(proj["v"], N_KV_HEADS).astype(bf16)
    att = attention(qr, kr, vh)

    xbc, gathered, *carriers = conv_silu_comm(proj["xbc"], small["conv_w"], small["conv_b"], late_shard, recv_like)
    wfull.update(_split_late(gathered))
    wgrads.update(zip(LATE, carriers))
    ao = lin["attn_out"](att, wfull["attn_out"], wgrads["attn_out"])
    softplus = make_rowwise("dt_softplus", _fn_softplus, [(2 * N_SSD_HEADS, f32)])
    (dt,), _ = softplus((proj["dt"][:, :2 * N_SSD_HEADS],), (small["dt_bias"].reshape(1, 2 * N_SSD_HEADS),), (), ())
    a_neg = -jnp.exp(small["A_log"])
    dexp = jnp.repeat(small["ssd_D"].reshape(N_SSD_HEADS), SSD_HEAD_DIM).reshape(1, D_INNER)
    y = ssd_bidir(xbc, dt.T, a_neg.reshape(2 * N_SSD_HEADS, 1), dexp)
    ssd_gate = make_rowwise("ssd_gate", _fn_ssd_gate, [(D_INNER, bf16)], tm_pref=128)
    (ssd_out,), _ = ssd_gate((y, proj["z"]), (small["ssd_norm_w"],), (), ())
    so = lin["ssd_out"](ssd_out, wfull["ssd_out"], wgrads["ssd_out"])

    merge = make_rowwise("merge", _fn_merge, [(D_MODEL, bf16)])
    (merged,), _ = merge((ao, so, proj["ga"], proj["gs"]), (), (), ())
    mo = lin["o"](merged, wfull["o"], wgrads["o"])

    res_norm = make_rowwise("res_norm", _fn_res_norm, [(D_MODEL, f32), (D_MODEL, bf16)])
    (x1, h2), _ = res_norm((x, mo), (gate1, small["norm2_w"], scale2, shift2), (), ())
    ff = mlp(h2, wfull["mlp1"], wgrads["mlp1"], wfull["mlp2"], wgrads["mlp2"])
    loss_op = make_rowwise("loss", _fn_loss, [], [(1, 1)])
    _, (loss,) = loss_op((x1, ff), (gate2,), (), (target,))
    return loss[0, 0]


_BC1 = 1.0 - ADAM_B1 ** ADAM_STEP
_BC2 = 1.0 - ADAM_B2 ** ADAM_STEP


def _adamw(w, g, m, v):
    m = ADAM_B1 * m + (1.0 - ADAM_B1) * g
    v = ADAM_B2 * v + (1.0 - ADAM_B2) * (g * g)
    delta = -ADAM_LR * ((m / _BC1) / (jnp.sqrt(v / _BC2) + ADAM_EPS) + ADAM_WD * w)
    return delta, m, v


def _ada_fwd(c_all, w, b):
    n = w.shape[1]

    def body(c_ref, w_ref, b_ref, o_ref):
        o_ref[...] = jnp.dot(_silu(c_ref[...]), w_ref[...], precision=HIGHEST, preferred_element_type=f32) + b_ref[...]

    return pl.pallas_call(body, name="ada_fwd", out_shape=jax.ShapeDtypeStruct((N_DEV, n), f32),
                          compiler_params=_cparams())(c_all, w, b)


def _ada_bwd_adamw(c_all, dmod, w, m, v):
    d, n = w.shape
    tr = _pick(d, (256, 128))

    def body(c_ref, dm_ref, w_ref, m_ref, v_ref, g_ref, dl_ref, mo_ref, vo_ref):
        g = lax.dot_general(_silu(c_ref[...]), dm_ref[...], _DIMS["tn"], precision=HIGHEST,
                            preferred_element_type=f32)
        g_ref[...] = g
        dl_ref[...], mo_ref[...], vo_ref[...] = _adamw(w_ref[...], g, m_ref[...], v_ref[...])

    blk = pl.BlockSpec((tr, n), lambda i: (i, 0))
    return pl.pallas_call(
        body, name="ada_bwd_adamw", grid=(d // tr,),
        in_specs=[pl.BlockSpec((N_DEV, tr), lambda i: (0, i)), pl.BlockSpec((N_DEV, n), lambda i: (0, 0)), blk, blk, blk],
        out_specs=[blk] * 4, out_shape=[jax.ShapeDtypeStruct((d, n), f32)] * 4,
        compiler_params=_cparams(dimension_semantics=("parallel",)),
    )(c_all, dmod, w, m, v)


def _sum_over_mesh(g):
    def body(g_ref, o_ref):
        acc = g_ref[0]
        for d in range(1, N_DEV):
            acc = acc + g_ref[d]
        o_ref[...] = acc

    return pl.pallas_call(body, name="sum_small", out_shape=jax.ShapeDtypeStruct(g.shape[1:], f32),
                          compiler_params=_cparams())(g)


def _adamw_small(w, g, m, v):
    def body(w_ref, g_ref, m_ref, v_ref, dl_ref, mo_ref, vo_ref):
        dl_ref[...], mo_ref[...], vo_ref[...] = _adamw(w_ref[...], g_ref[...], m_ref[...], v_ref[...])

    return pl.pallas_call(body, name="adamw_small", out_shape=[jax.ShapeDtypeStruct(w.shape, f32)] * 3,
                          compiler_params=_cparams())(w, g, m, v)


def _sum_adamw(recv, w, m, v, name):
    _, r, c = recv.shape
    tr = _pick(r, (256, 128, 64, 16))

    def body(g_ref, w_ref, m_ref, v_ref, go_ref, dl_ref, mo_ref, vo_ref):
        g = g_ref[0].astype(f32)
        for d in range(1, N_DEV):
            g = g + g_ref[d].astype(f32)
        go_ref[...] = g
        dl_ref[...], mo_ref[...], vo_ref[...] = _adamw(w_ref[...], g, m_ref[...], v_ref[...])

    blk = pl.BlockSpec((tr, c), lambda i: (i, 0))
    return pl.pallas_call(
        body, name=name, grid=(r // tr,),
        in_specs=[pl.BlockSpec((N_DEV, tr, c), lambda i: (0, i, 0)), blk, blk, blk],
        out_specs=[blk] * 4, out_shape=[jax.ShapeDtypeStruct((r, c), f32)] * 4,
        compiler_params=_cparams(dimension_semantics=("parallel",)),
    )(recv, w, m, v)


def _pack_small(arrs):
    parts = []
    for a in arrs:
        flat = a.reshape(-1).astype(f32)
        parts.append(jnp.pad(flat, (0, (-flat.shape[0]) % LANE)))
    flat = jnp.concatenate(parts)
    flat = jnp.pad(flat, (0, (-flat.shape[0]) % (8 * LANE)))
    return flat.reshape(-1, LANE)


def _unpack_small(packed, shapes):
    flat = packed.reshape(-1)
    out, off = [], 0
    for shp in shapes:
        n = int(np.prod(shp))
        out.append(flat[off:off + n].reshape(shp))
        off += n + (-n) % LANE
    return out


BIG = ("w_attn_out", "w_ssd_out", "w_o", "w_mlp1", "w_mlp2")
BIG_ROWS = (N_Q_HEADS * HEAD_DIM // N_DEV, D_INNER // N_DEV, D_MODEL // N_DEV,
            D_MODEL * (D_FF // N_DEV) // PACK_COLS, D_FF // N_DEV)
N_IN_SHARD = D_IN_PROJ // N_DEV
assert sum(BIG_ROWS) % 16 == 0


def _pack_big(shards, dtype):
    return jnp.concatenate([s.astype(dtype).reshape(-1, PACK_COLS) for s in shards], axis=0)


def _unpack_big(packed, shapes):
    out, off = [], 0
    for rows, shp in zip(BIG_ROWS, shapes):
        out.append(packed[off:off + rows].reshape(shp))
        off += rows
    return out


LATE = ("attn_out", "ssd_out", "o", "mlp1", "mlp2")
LATE_SHAPES = ((N_Q_HEADS * HEAD_DIM, D_MODEL), (D_INNER, D_MODEL), (D_MODEL, D_MODEL), (D_MODEL, D_FF),
               (D_FF, D_MODEL))


def _split_w_in(g_in):
    w_in = g_in.transpose(1, 0, 2).reshape(D_MODEL, D_IN_PROJ)
    w = {}
    off = 0
    for name, size in zip(PROJ_NAMES, PROJ_SIZES):
        w[name] = w_in[:, off:off + size]
        off += size
    w["dt"] = jnp.pad(w["dt"], ((0, 0), (0, DT_PAD - 2 * N_SSD_HEADS)))
    return w


def _split_late(g):
    offs = np.cumsum((0,) + BIG_ROWS)
    sl = [g[:, offs[i]:offs[i + 1]] for i in range(len(BIG))]
    return {"attn_out": sl[0].reshape(LATE_SHAPES[0]), "ssd_out": sl[1].reshape(LATE_SHAPES[1]),
            "o": sl[2].reshape(LATE_SHAPES[2]),
            "mlp1": sl[3].reshape(N_DEV, D_MODEL, D_FF // N_DEV).transpose(1, 0, 2).reshape(LATE_SHAPES[3]),
            "mlp2": sl[4].reshape(LATE_SHAPES[4])}


def _pack_in_grads(gw):
    gw = {n: g.astype(bf16) for n, g in gw.items()}
    gw["dt"] = gw["dt"][:, :2 * N_SSD_HEADS]
    g_in = jnp.concatenate([gw[n] for n in PROJ_NAMES], axis=1)
    return g_in.reshape(D_MODEL, N_DEV, N_IN_SHARD).transpose(1, 0, 2)


def _pack_late_grads(gw):
    gw = {n: g.astype(bf16) for n, g in gw.items()}
    parts = [
        gw["attn_out"].reshape(N_DEV, -1, PACK_COLS),
        gw["ssd_out"].reshape(N_DEV, -1, PACK_COLS),
        gw["o"].reshape(N_DEV, -1, PACK_COLS),
        gw["mlp1"].reshape(D_MODEL, N_DEV, D_FF // N_DEV).transpose(1, 0, 2).reshape(N_DEV, -1, PACK_COLS),
        gw["mlp2"].reshape(N_DEV, -1, PACK_COLS),
    ]
    return jnp.concatenate(parts, axis=1)


SMALL = ("norm1_w", "norm2_w", "q_norm_w", "k_norm_w", "conv_w", "conv_b", "A_log", "dt_bias", "ssd_D", "ssd_norm_w")


def kernel(x, c, w_ada, b_ada, norm1_w, norm2_w, w_in, q_norm_w, k_norm_w, conv_w, conv_b, A_log, dt_bias, ssd_D, ssd_norm_w, w_attn_out, w_ssd_out, w_o, w_mlp1, w_mlp2, loss_target, m_w_ada, m_b_ada, m_norm1_w, m_norm2_w, m_w_in, m_q_norm_w, m_k_norm_w, m_conv_w, m_conv_b, m_A_log, m_dt_bias, m_ssd_D, m_ssd_norm_w, m_w_attn_out, m_w_ssd_out, m_w_o, m_w_mlp1, m_w_mlp2, v_w_ada, v_b_ada, v_norm1_w, v_norm2_w, v_w_in, v_q_norm_w, v_k_norm_w, v_conv_w, v_conv_b, v_A_log, v_dt_bias, v_ssd_D, v_ssd_norm_w, v_w_attn_out, v_w_ssd_out, v_w_o, v_w_mlp1, v_w_mlp2):
    args = dict(locals())
    me = _my_index()
    n_ada = 6 * D_MODEL // N_DEV
    n_cw = CONV_DIM // N_DEV

    blk = jnp.zeros((8, D_MODEL), f32)
    blk = blk.at[0:1, :].set(c)
    blk = blk.at[1:1 + D_CONV, :n_cw].set(conv_w[0])
    g0 = _all_gather(blk, "gather_c_convw", in_vmem=True)
    c_all = g0[:, 0, :]
    conv_w_full = g0[:, 1:1 + D_CONV, :n_cw].transpose(1, 0, 2).reshape(D_CONV, CONV_DIM)

    b_shard = lax.dynamic_slice(b_ada, (0, me * n_ada), (1, n_ada))
    mod_cols = _ada_fwd(c_all, w_ada[0], b_shard)
    g1 = _all_gather(mod_cols, "gather_mod", in_vmem=True)
    mod_mine = lax.dynamic_index_in_dim(g1, me, axis=1, keepdims=False)
    mod = mod_mine.reshape(6, 1, D_MODEL)

    big_shapes = [args[n].shape[1:] for n in BIG]
    late_shard = _pack_big([args[n][0] for n in BIG], bf16)
    wfull = _split_w_in(_all_gather(w_in[0].astype(bf16), "gather_w_in", in_vmem=False))
    wgrads = {n: jnp.zeros(wfull[n].shape, f32) for n in PROJ_NAMES}
    recv_like = jnp.zeros((N_DEV,) + late_shard.shape, bf16)

    small = {"norm1_w": norm1_w, "norm2_w": norm2_w, "q_norm_w": q_norm_w, "k_norm_w": k_norm_w,
             "conv_w": conv_w_full, "conv_b": conv_b, "A_log": A_log[0], "dt_bias": dt_bias[0], "ssd_D": ssd_D,
             "ssd_norm_w": ssd_norm_w}

    loss, (gx, gmod, gsmall, gw, recv_late) = jax.value_and_grad(local_loss, argnums=(0, 1, 2, 3, 4))(
        x[0], mod, small, wgrads, recv_like, wfull, late_shard, loss_target[0])

    small_list = [gmod, gsmall["norm1_w"], gsmall["norm2_w"], gsmall["q_norm_w"], gsmall["k_norm_w"], gsmall["conv_w"],
                  gsmall["conv_b"], gsmall["A_log"], gsmall["dt_bias"], gsmall["ssd_D"], gsmall["ssd_norm_w"],
                  loss.reshape(1)]
    small_shapes = [a.shape for a in small_list]
    g2 = _all_gather(_pack_small(small_list), "gather_small_grads", in_vmem=True)
    summed = _unpack_small(_sum_over_mesh(g2), small_shapes)
    loss_total = summed[-1][0]
    g_b_ada = summed[0].reshape(1, 6 * D_MODEL)
    g_small = dict(zip(SMALL, summed[1:-1]))
    g_conv_w = lax.dynamic_slice(g_small["conv_w"], (0, me * n_cw), (D_CONV, n_cw))

    dmod_all = g2[:, :6 * D_MODEL // LANE, :].reshape(N_DEV, 6 * D_MODEL)
    dmod_shard = lax.dynamic_slice(dmod_all, (0, me * n_ada), (N_DEV, n_ada))
    ada = _ada_bwd_adamw(c_all, dmod_shard, w_ada[0], m_w_ada[0], v_w_ada[0])

    small_grads = {"b_ada": g_b_ada, "norm1_w": g_small["norm1_w"], "norm2_w": g_small["norm2_w"],
                   "q_norm_w": g_small["q_norm_w"], "k_norm_w": g_small["k_norm_w"], "conv_w": g_conv_w[None],
                   "conv_b": g_small["conv_b"], "A_log": g_small["A_log"][None], "dt_bias": g_small["dt_bias"][None],
                   "ssd_D": g_small["ssd_D"], "ssd_norm_w": g_small["ssd_norm_w"]}
    sm_names = list(small_grads)
    sm_shapes = [args[n].shape for n in sm_names]
    sm = _adamw_small(_pack_small([args[n] for n in sm_names]), _pack_small([small_grads[n] for n in sm_names]),
                      _pack_small([args["m_" + n] for n in sm_names]), _pack_small([args["v_" + n] for n in sm_names]))
    sm_delta, sm_m, sm_v = [dict(zip(sm_names, _unpack_small(t, sm_shapes))) for t in sm]
    small_grads = {n: small_grads[n].reshape(args[n].shape) for n in sm_names}

    w_in_out = _sum_adamw(_scatter_blocks(_pack_in_grads(gw), "scatter_grads_w_in"), w_in[0], m_w_in[0], v_w_in[0],
                          "sum_adamw_w_in")
    big = _sum_adamw(recv_late, _pack_big([args[n][0] for n in BIG], f32),
                     _pack_big([args["m_" + n][0] for n in BIG], f32),
                     _pack_big([args["v_" + n][0] for n in BIG], f32), "sum_adamw")
    big_g, big_delta, big_m, big_v = [dict(zip(BIG, [t[None] for t in _unpack_big(p, big_shapes)])) for p in big]
    big_g["w_in"], big_delta["w_in"], big_m["w_in"], big_v["w_in"] = [t[None] for t in w_in_out]

    names = ("w_ada", "b_ada", "norm1_w", "norm2_w", "w_in", "q_norm_w", "k_norm_w", "conv_w", "conv_b", "A_log",
             "dt_bias", "ssd_D", "ssd_norm_w", "w_attn_out", "w_ssd_out", "w_o", "w_mlp1", "w_mlp2")
    grads, deltas, new_m, new_v = {}, {}, {}, {}
    for n in names:
        if n == "w_ada":
            grads[n], deltas[n], new_m[n], new_v[n] = [t[None] for t in ada]
        elif n in big_g:
            grads[n], deltas[n], new_m[n], new_v[n] = big_g[n], big_delta[n], big_m[n], big_v[n]
        else:
            grads[n], deltas[n], new_m[n], new_v[n] = small_grads[n], sm_delta[n], sm_m[n], sm_v[n]
    return (loss_total, gx[None], *[grads[n] for n in names], *[deltas[n] for n in names],
            *[new_m[n] for n in names], *[new_v[n] for n in names])
```

```python
import functools
import math

import jax
import jax.numpy as jnp
import numpy as np
from jax import lax
from jax.experimental import pallas as pl
from jax.experimental.pallas import tpu as pltpu

f32 = jnp.float32
bf16 = jnp.bfloat16
HIGHEST = lax.Precision.HIGHEST
MESH = pl.DeviceIdType.MESH

N_DEV = 8
D_MODEL = 1024
GRID_W = 64
N_Q_HEADS = 16
N_KV_HEADS = 4
HEAD_DIM = 64
ROPE_THETA = 10000.0
D_INNER = 2048
SSD_HEAD_DIM = 64
N_SSD_HEADS = 32
N_SSD_GROUPS = 4
D_STATE = 128
D_CONV = 5
CHUNK = 128
D_FF = 4096
EPS = 1e-6
CONV_DIM = D_INNER + 2 * N_SSD_GROUPS * D_STATE
GN = N_SSD_GROUPS * D_STATE
PROJ_NAMES = ("q", "k", "v", "xbc", "z", "dt", "ga", "gs")
PROJ_SIZES = (N_Q_HEADS * HEAD_DIM, N_KV_HEADS * HEAD_DIM, N_KV_HEADS * HEAD_DIM, CONV_DIM, D_INNER,
              2 * N_SSD_HEADS, D_MODEL, D_MODEL)
D_IN_PROJ = sum(PROJ_SIZES)
DT_PAD = 128

ADAM_LR, ADAM_B1, ADAM_B2, ADAM_EPS, ADAM_WD, ADAM_STEP = 0.001, 0.9, 0.999, 1e-08, 0.01, 10

V7X_VMEM_LIMIT = 56 * 1024 * 1024
LANE = 128
PACK_COLS = 1024


def _cparams(**kw):
    return pltpu.CompilerParams(vmem_limit_bytes=V7X_VMEM_LIMIT, **kw)


def _pick(dim, prefs):
    for p in prefs:
        if dim % p == 0:
            return p
    return dim


def _my_index():
    return 4 * lax.axis_index("x") + 2 * lax.axis_index("y") + lax.axis_index("c")


COMM_SEMS = [pltpu.SemaphoreType.DMA((7,)), pltpu.SemaphoreType.DMA((7,)), pltpu.SemaphoreType.DMA]


def _gather_phases(x_ref, out_ref, send_sems, recv_sems, local_sem):
    x, y, cc = lax.axis_index("x"), lax.axis_index("y"), lax.axis_index("c")
    me, sibling = (x, y, cc), (x, y, 1 - cc)
    chips = [(1 - x, y), (x, 1 - y), (1 - x, 1 - y)]

    def slot(px, py, pc):
        return out_ref.at[4 * px + 2 * py + pc]

    def copy(k, blk, to, src=None):
        return pltpu.make_async_remote_copy(
            src_ref=slot(*blk) if src is None else src, dst_ref=slot(*blk),
            send_sem=send_sems.at[k], recv_sem=recv_sems.at[k], device_id=to, device_id_type=MESH)

    mine = pltpu.make_async_copy(x_ref, slot(*me), local_sem)
    first = [copy(0, me, sibling, src=x_ref)]
    first += [copy(1 + j, me, (*chip, cc), src=x_ref) for j, chip in enumerate(chips)]
    passed = [copy(4 + j, (*chip, cc), sibling) for j, chip in enumerate(chips)]

    def start():
        mine.start()
        for cp in first:
            cp.start()

    def finish():
        for j, chip in enumerate(chips):
            copy(1 + j, (*chip, cc), me).wait_recv()
            passed[j].start()
        copy(0, sibling, me).wait_recv()
        for j, chip in enumerate(chips):
            copy(4 + j, (*chip, 1 - cc), me).wait_recv()
        for cp in first + passed:
            cp.wait_send()
        mine.wait()

    return start, finish


def _scatter_phases(g_ref, out_ref, send_sems, recv_sems, local_sem):
    x, y, cc = lax.axis_index("x"), lax.axis_index("y"), lax.axis_index("c")
    me = 4 * x + 2 * y + cc
    mine = pltpu.make_async_copy(g_ref.at[me], out_ref.at[me], local_sem)

    def copy(k):
        fx, fy, fc = (k >> 2) & 1, (k >> 1) & 1, k & 1
        px = x + fx - 2 * x * fx
        py = y + fy - 2 * y * fy
        pc = cc + fc - 2 * cc * fc
        peer = 4 * px + 2 * py + pc
        send = pltpu.make_async_remote_copy(
            src_ref=g_ref.at[peer], dst_ref=out_ref.at[me],
            send_sem=send_sems.at[k - 1], recv_sem=recv_sems.at[k - 1],
            device_id=(px, py, pc), device_id_type=MESH)
        recv = pltpu.make_async_remote_copy(
            src_ref=g_ref.at[peer], dst_ref=out_ref.at[peer],
            send_sem=send_sems.at[k - 1], recv_sem=recv_sems.at[k - 1],
            device_id=(px, py, pc), device_id_type=MESH)
        return send, recv

    pairs = [copy(k) for k in range(1, N_DEV)]

    def start():
        mine.start()
        for send, _ in pairs:
            send.start()

    def finish():
        for _, recv in pairs:
            recv.wait_recv()
        for send, _ in pairs:
            send.wait_send()
        mine.wait()

    return start, finish


def _all_gather(block, name, in_vmem):
    r, c = block.shape

    def body(x_ref, out_ref, send_sems, recv_sems, local_sem):
        start, finish = _gather_phases(x_ref, out_ref, send_sems, recv_sems, local_sem)
        start()
        finish()

    space = pltpu.VMEM if in_vmem else pl.ANY
    return pl.pallas_call(
        body, name=name,
        out_shape=jax.ShapeDtypeStruct((N_DEV, r, c), block.dtype),
        in_specs=[pl.BlockSpec(memory_space=space)],
        out_specs=pl.BlockSpec(memory_space=space),
        scratch_shapes=[pltpu.SemaphoreType.DMA((7,)), pltpu.SemaphoreType.DMA((7,)), pltpu.SemaphoreType.DMA],
    )(block)


def _scatter_blocks(g, name):
    _, r, c = g.shape

    def body(g_ref, out_ref, send_sems, recv_sems, local_sem):
        start, finish = _scatter_phases(g_ref, out_ref, send_sems, recv_sems, local_sem)
        start()
        finish()

    return pl.pallas_call(
        body, name=name,
        out_shape=jax.ShapeDtypeStruct(g.shape, g.dtype),
        in_specs=[pl.BlockSpec(memory_space=pl.ANY)],
        out_specs=pl.BlockSpec(memory_space=pl.ANY),
        scratch_shapes=[pltpu.SemaphoreType.DMA((7,)), pltpu.SemaphoreType.DMA((7,)), pltpu.SemaphoreType.DMA],
    )(g)


_DIMS = {"nn": (((1,), (0,)), ((), ())), "nt": (((1,), (1,)), ((), ())), "tn": (((0,), (0,)), ((), ()))}


def _matmul(a, b, mode, out_dtype, name, epilogue=None, side=None):
    if mode == "nn":
        (m, k), (_, n) = a.shape, b.shape
    elif mode == "nt":
        (m, k), (n, _) = a.shape, b.shape
    else:
        (k, m), (_, n) = a.shape, b.shape
    tm = _pick(m, (1024, 512, 256, 128))
    if mode == "tn":
        tn = _pick(n, (1536, 1024, 512, 256, 128))
        tk = _pick(k, (1024, 512, 256, 128))
    else:
        tn = _pick(n, (1024, 512, 384, 256, 128))
        tk = _pick(k, (1024, 512, 256, 128))
    nk = k // tk
    dims = _DIMS[mode]
    n_in = 3 if epilogue == "drelu2" else 2
    n_out = 2 if epilogue == "relu2" else 1

    def body(*refs):
        a_ref, b_ref = refs[:2]
        outs, acc_ref = refs[n_in:n_in + n_out], refs[n_in + n_out]
        kk = pl.program_id(2)
        part = lax.dot_general(a_ref[...].astype(bf16), b_ref[...].astype(bf16), dims, preferred_element_type=f32)

        def finish(acc):
            if epilogue == "relu2":
                r = jnp.maximum(acc, 0.0)
                outs[0][...] = acc.astype(out_dtype)
                outs[1][...] = (r * r).astype(out_dtype)
            elif epilogue == "drelu2":
                outs[0][...] = (acc * (2.0 * jnp.maximum(refs[2][...].astype(f32), 0.0))).astype(out_dtype)
            else:
                outs[0][...] = acc.astype(out_dtype)

        if nk == 1:
            finish(part)
        else:
            @pl.when(kk == 0)
            def _():
                acc_ref[...] = part

            @pl.when(kk > 0)
            def _():
                acc_ref[...] += part

            @pl.when(kk == nk - 1)
            def _():
                finish(acc_ref[...])

    if mode == "tn":
        a_spec = pl.BlockSpec((tk, tm), lambda i, j, kk: (kk, i))
    else:
        a_spec = pl.BlockSpec((tm, tk), lambda i, j, kk: (i, kk))
    if mode == "nt":
        b_spec = pl.BlockSpec((tn, tk), lambda i, j, kk: (j, kk))
    else:
        b_spec = pl.BlockSpec((tk, tn), lambda i, j, kk: (kk, j))
    o_spec = pl.BlockSpec((tm, tn), lambda i, j, kk: (i, j))
    o_shape = jax.ShapeDtypeStruct((m, n), out_dtype)
    res = pl.pallas_call(
        body, name=name, grid=(m // tm, n // tn, nk),
        in_specs=[a_spec, b_spec] + ([o_spec] if epilogue == "drelu2" else []),
        out_specs=[o_spec] * n_out, out_shape=[o_shape] * n_out,
        scratch_shapes=[pltpu.VMEM((tm, tn), f32)],
        compiler_params=_cparams(dimension_semantics=("parallel", "parallel", "arbitrary")),
    )(*((a, b, side) if epilogue == "drelu2" else (a, b)))
    return res if n_out == 2 else res[0]


@jax.custom_vjp
def mlp(h, w1, w1grad, w2, w2grad):
    _, r = _matmul(h, w1, "nn", bf16, "mlp1_fwd", epilogue="relu2")
    return _matmul(r, w2, "nn", f32, "mlp2_fwd")


def _mlp_fwd(h, w1, w1grad, w2, w2grad):
    u, r = _matmul(h, w1, "nn", bf16, "mlp1_fwd", epilogue="relu2")
    return _matmul(r, w2, "nn", f32, "mlp2_fwd"), (h, w1, w2, u, r)


def _mlp_bwd(res, dy):
    h, w1, w2, u, r = res
    du = _matmul(dy, w2, "nt", bf16, "mlp2_dgrad", epilogue="drelu2", side=u)
    dw2 = _matmul(r, dy, "tn", f32, "mlp2_wgrad")
    dh = _matmul(du, w1, "nt", h.dtype, "mlp1_dgrad")
    dw1 = _matmul(h, du, "tn", f32, "mlp1_wgrad")
    return dh, jnp.zeros_like(w1), dw1, jnp.zeros_like(w2), dw2


mlp.defvjp(_mlp_fwd, _mlp_bwd)


def make_linear(name):
    @jax.custom_vjp
    def linear(a, w, wgrad):
        return _matmul(a, w, "nn", f32, name + "_fwd")

    def fwd(a, w, wgrad):
        return linear(a, w, wgrad), (a, w)

    def bwd(res, dy):
        a, w = res
        da = _matmul(dy, w, "nt", a.dtype, name + "_dgrad")
        dw = _matmul(a, dy, "tn", f32, name + "_wgrad")
        return da, jnp.zeros_like(w), dw

    linear.defvjp(fwd, bwd)
    return linear


def _in_proj_dgrad(dys, ws, g):
    s, d = dys[0].shape[0], ws[0].shape[0]
    tm = _pick(s, (512, 256, 128))
    tks = [min(w.shape[1], 1024) for w in ws]
    steps = [w.shape[1] // tk for w, tk in zip(ws, tks)]
    starts = [sum(steps[:p]) for p in range(len(ws))]
    total = sum(steps)
    n_p, n_i = len(ws), s // tm
    assert steps[0] == 1

    def body(*refs):
        dy_refs, w_refs, g_ref = refs[:n_p], refs[n_p:2 * n_p], refs[2 * n_p]
        dh_ref, recv_ref, acc_ref, send_sems, recv_sems, local_sem = refs[2 * n_p + 1:]
        i, t = pl.program_id(0), pl.program_id(1)
        start, finish = _scatter_phases(g_ref, recv_ref, send_sems, recv_sems, local_sem)

        @pl.when((i == 0) & (t == 0))
        def _():
            start()

        for p in range(n_p):
            @pl.when((t >= starts[p]) & (t < starts[p] + steps[p]))
            def _(p=p):
                part = lax.dot_general(dy_refs[p][...].astype(bf16), w_refs[p][...], _DIMS["nt"],
                                       preferred_element_type=f32)
                if p == 0:
                    acc_ref[...] = part
                else:
                    acc_ref[...] += part

        @pl.when(t == total - 1)
        def _():
            dh_ref[...] = acc_ref[...].astype(dh_ref.dtype)

        @pl.when((i == n_i - 1) & (t == total - 1))
        def _():
            finish()

    def piece_map(p, rows):
        def index_map(i, t):
            blk = jnp.clip(t - starts[p], 0, steps[p] - 1)
            return (i, blk) if rows else (0, blk)

        return index_map

    hbm = pl.BlockSpec(memory_space=pl.ANY)
    in_specs = [pl.BlockSpec((tm, tks[p]), piece_map(p, True)) for p in range(n_p)]
    in_specs += [pl.BlockSpec((d, tks[p]), piece_map(p, False)) for p in range(n_p)]
    return pl.pallas_call(
        body, name="in_proj_dgrad", grid=(n_i, total), in_specs=in_specs + [hbm],
        out_specs=[pl.BlockSpec((tm, d), lambda i, t: (i, 0)), hbm],
        out_shape=[jax.ShapeDtypeStruct((s, d), bf16), jax.ShapeDtypeStruct(g.shape, g.dtype)],
        scratch_shapes=[pltpu.VMEM((tm, d), f32)] + COMM_SEMS,
        compiler_params=_cparams(dimension_semantics=("arbitrary", "arbitrary")),
    )(*dys, *ws, g)


@jax.custom_vjp
def in_proj(h, ws, recv_like):
    return tuple(_matmul(h, w, "nn", f32, "lin_" + n + "_fwd") for n, w in zip(PROJ_NAMES, ws))


def _in_proj_fwd(h, ws, recv_like):
    return in_proj(h, ws, recv_like), (h, ws)


def _in_proj_bwd(res, dys):
    h, ws = res
    dws = {n: _matmul(h, dy, "tn", f32, "lin_" + n + "_wgrad") for n, dy in zip(PROJ_NAMES, dys)}
    dh, recv = _in_proj_dgrad(dys, ws, _pack_in_grads(dws))
    return dh.astype(h.dtype), tuple(jnp.zeros_like(w) for w in ws), recv


in_proj.defvjp(_in_proj_fwd, _in_proj_bwd)


def make_rowwise(name, fn, row_out, sum_out=(), tm_pref=256):
    def specs(rows, gpars, cpars, consts, tm):
        s = [pl.BlockSpec((tm, r.shape[1]), lambda i: (i, 0)) for r in rows]
        s += [pl.BlockSpec(p.shape, lambda i: (0, 0)) for p in gpars]
        s += [pl.BlockSpec(p.shape, lambda i: (0, 0)) for p in cpars]
        for cst in consts:
            nb = cst.shape[0] // tm
            s.append(pl.BlockSpec((tm, cst.shape[1]), lambda i, nb=nb: (i % nb, 0)))
        return s

    def tile_rows(rows, consts):
        r = rows[0].shape[0]
        common = math.gcd(r, *[cst.shape[0] for cst in consts])
        tm = _pick(common, (tm_pref, 512, 256, 128, 64, 32, 16, 8))
        return r, tm

    def forward(rows, gpars, cpars, consts):
        r, tm = tile_rows(rows, consts)
        nr, ng, nc, nk = len(rows), len(gpars), len(cpars), len(consts)

        def body(*refs):
            ins = refs[:nr + ng + nc + nk]
            outs = refs[nr + ng + nc + nk:]
            rv = [t[...].astype(f32) for t in ins[:nr]]
            gv = [t[...].astype(f32) for t in ins[nr:nr + ng]]
            cv = [t[...] for t in ins[nr + ng:nr + ng + nc]]
            kv = [t[...].astype(f32) for t in ins[nr + ng + nc:]]
            ro, so = fn(rv, gv, cv, kv)
            for o_ref, val in zip(outs[:len(row_out)], ro):
                o_ref[...] = val.astype(o_ref.dtype)
            if sum_out:
                @pl.when(pl.program_id(0) == 0)
                def _():
                    for o_ref in outs[len(row_out):]:
                        o_ref[...] = jnp.zeros_like(o_ref)
                for o_ref, val in zip(outs[len(row_out):], so):
                    o_ref[...] += val

        out_specs = [pl.BlockSpec((tm, w), lambda i: (i, 0)) for w, _ in row_out]
        out_specs += [pl.BlockSpec(shp, lambda i: (0, 0)) for shp in sum_out]
        out_shape = [jax.ShapeDtypeStruct((r, w), dt) for w, dt in row_out]
        out_shape += [jax.ShapeDtypeStruct(shp, f32) for shp in sum_out]
        res = pl.pallas_call(
            body, name=name + "_fwd", grid=(r // tm,),
            in_specs=specs(rows, gpars, cpars, consts, tm), out_specs=out_specs, out_shape=out_shape,
            compiler_params=_cparams(dimension_semantics=("arbitrary",)),
        )(*rows, *gpars, *cpars, *consts)
        return tuple(res[:len(row_out)]), tuple(res[len(row_out):])

    def backward(rows, gpars, cpars, consts, d_ro, d_so):
        r, tm = tile_rows(rows, consts)
        nr, ng, nc, nk = len(rows), len(gpars), len(cpars), len(consts)
        n_in = nr + ng + nc + nk + len(row_out) + len(sum_out)

        def body(*refs):
            ins, outs = refs[:n_in], refs[n_in:]
            rv = [t[...].astype(f32) for t in ins[:nr]]
            gv = [t[...].astype(f32) for t in ins[nr:nr + ng]]
            cv = [t[...] for t in ins[nr + ng:nr + ng + nc]]
            kv = [t[...].astype(f32) for t in ins[nr + ng + nc:nr + ng + nc + nk]]
            o = nr + ng + nc + nk
            dro = [t[...].astype(f32) for t in ins[o:o + len(row_out)]]
            dso = [t[...] for t in ins[o + len(row_out):]]
            _, vjp = jax.vjp(lambda a, b: tuple(tuple(t) for t in fn(a, b, cv, kv)), rv, gv)
            drv, dgv = vjp((tuple(dro), tuple(dso)))
            for o_ref, val in zip(outs[:nr], drv):
                o_ref[...] = val.astype(o_ref.dtype)
            if ng:
                @pl.when(pl.program_id(0) == 0)
                def _():
                    for o_ref in outs[nr:]:
                        o_ref[...] = jnp.zeros_like(o_ref)
                for o_ref, val in zip(outs[nr:], dgv):
                    o_ref[...] += val

        in_specs = specs(rows, gpars, cpars, consts, tm)
        in_specs += [pl.BlockSpec((tm, w), lambda i: (i, 0)) for w, _ in row_out]
        in_specs += [pl.BlockSpec(shp, lambda i: (0, 0)) for shp in sum_out]
        out_specs = [pl.BlockSpec((tm, t.shape[1]), lambda i: (i, 0)) for t in rows]
        out_specs += [pl.BlockSpec(p.shape, lambda i: (0, 0)) for p in gpars]
        out_shape = [jax.ShapeDtypeStruct(t.shape, t.dtype) for t in rows]
        out_shape += [jax.ShapeDtypeStruct(p.shape, f32) for p in gpars]
        res = pl.pallas_call(
            body, name=name + "_bwd", grid=(r // tm,),
            in_specs=in_specs, out_specs=out_specs, out_shape=out_shape,
            compiler_params=_cparams(dimension_semantics=("arbitrary",)),
        )(*rows, *gpars, *cpars, *consts, *d_ro, *d_so)
        return tuple(res[:nr]), tuple(res[nr:])

    @jax.custom_vjp
    def op(rows, gpars, cpars, consts):
        return forward(rows, gpars, cpars, consts)

    def op_fwd(rows, gpars, cpars, consts):
        return forward(rows, gpars, cpars, consts), (rows, gpars, cpars, consts)

    def op_bwd(res, cts):
        rows, gpars, cpars, consts = res
        d_ro, d_so = cts
        drows, dg = backward(rows, gpars, cpars, consts, d_ro, d_so)
        dg = tuple(d.astype(p.dtype) for d, p in zip(dg, gpars))
        return (drows, dg, tuple(jnp.zeros_like(p) for p in cpars), tuple(jnp.zeros_like(k) for k in consts))

    op.defvjp(op_fwd, op_bwd)
    return op


def _rms(x):
    return x * lax.rsqrt(jnp.mean(x * x, axis=-1, keepdims=True) + EPS)


def _silu(x):
    return x * jax.nn.sigmoid(x)


def _fn_norm_mod(rows, gp, cp, ks):
    (x,), (nw, sc, sh) = rows, gp
    return ((_rms(x) * nw) * (1.0 + sc) + sh,), ()


PAIR = 2 * HEAD_DIM


def _exact_dot(a, m):
    hi = a.astype(bf16)
    lo = (a - hi.astype(f32)).astype(bf16)
    return jnp.dot(hi, m, preferred_element_type=f32) + jnp.dot(lo, m, preferred_element_type=f32)


def _make_sel_dot(sign):
    @jax.custom_vjp
    def sel_dot(a, m):
        return _exact_dot(a, m)

    def fwd(a, m):
        return _exact_dot(a, m), m

    def bwd(m, g):
        return sign * _exact_dot(g, m), jnp.zeros_like(m)

    sel_dot.defvjp(fwd, bwd)
    return sel_dot


_head_sum_dot = _make_sel_dot(1.0)
_rope_perm_dot = _make_sel_dot(-1.0)


def _pair_norm_rope(t, w2, gsum, perm, cos2, sin2, out_scale):
    ss = _head_sum_dot(t * t, gsum)
    u = t * lax.rsqrt(ss * (1.0 / HEAD_DIM) + EPS) * w2
    return (u * cos2 + _rope_perm_dot(u, perm) * sin2) * out_scale


def _pair_consts():
    eye = np.eye(2, dtype=np.float32)
    gsum = np.kron(eye, np.ones((HEAD_DIM, HEAD_DIM), np.float32))
    return jnp.asarray(gsum, bf16), jnp.asarray(np.kron(eye, _rope_perm()), bf16)


def make_head_rope(name, nh, out_scale, head_major):
    width = nh * HEAD_DIM
    fn = functools.partial(_pair_norm_rope, out_scale=out_scale)

    def out_spec(tm):
        if head_major:
            return pl.BlockSpec((nh, tm, HEAD_DIM), lambda i: (0, i, 0))
        return pl.BlockSpec((tm, width), lambda i: (i, 0))

    def specs(tm):
        def full(shp):
            return pl.BlockSpec(shp, lambda i: (0, 0))

        return [pl.BlockSpec((tm, width), lambda i: (i, 0)), full((1, PAIR)), full((PAIR, PAIR)), full((PAIR, PAIR)),
                pl.BlockSpec((tm, PAIR), lambda i: (i, 0)), pl.BlockSpec((tm, PAIR), lambda i: (i, 0))]

    def forward(t, w2, gsum, perm, cos2, sin2):
        s = t.shape[0]
        tm = _pick(s, (512, 256, 128))

        def body(t_ref, w_ref, g_ref, p_ref, cos_ref, sin_ref, o_ref):
            for b in range(nh // 2):
                val = fn(t_ref[:, b * PAIR:(b + 1) * PAIR], w_ref[...], g_ref[...], p_ref[...], cos_ref[...],
                         sin_ref[...]).astype(o_ref.dtype)
                if head_major:
                    o_ref[2 * b] = val[:, :HEAD_DIM]
                    o_ref[2 * b + 1] = val[:, HEAD_DIM:]
                else:
                    o_ref[:, b * PAIR:(b + 1) * PAIR] = val

        return pl.pallas_call(
            body, name=name + "_fwd", grid=(s // tm,), in_specs=specs(tm), out_specs=out_spec(tm),
            out_shape=jax.ShapeDtypeStruct((nh, s, HEAD_DIM) if head_major else (s, width), bf16),
            compiler_params=_cparams(dimension_semantics=("arbitrary",)),
        )(t, w2, gsum, perm, cos2, sin2)

    def backward(t, w2, gsum, perm, cos2, sin2, dout):
        s = t.shape[0]
        tm = _pick(s, (512, 256, 128))

        def body(t_ref, w_ref, g_ref, p_ref, cos_ref, sin_ref, do_ref, dt_ref, dw_ref, pair_buf):
            @pl.when(pl.program_id(0) == 0)
            def _():
                dw_ref[...] = jnp.zeros_like(dw_ref)

            g_v, p_v, cos_v, sin_v = g_ref[...], p_ref[...], cos_ref[...], sin_ref[...]
            dw = jnp.zeros((1, PAIR), f32)
            for b in range(nh // 2):
                sl = slice(b * PAIR, (b + 1) * PAIR)
                if head_major:
                    pair_buf[:, :HEAD_DIM] = do_ref[2 * b].astype(f32)
                    pair_buf[:, HEAD_DIM:] = do_ref[2 * b + 1].astype(f32)
                    ct = pair_buf[...]
                else:
                    ct = do_ref[:, sl].astype(f32)
                _, vjp = jax.vjp(lambda a, c: fn(a, c, g_v, p_v, cos_v, sin_v), t_ref[:, sl], w_ref[...])
                dtb, dwb = vjp(ct)
                dt_ref[:, sl] = dtb
                dw = dw + dwb
            dw_ref[...] += dw

        return pl.pallas_call(
            body, name=name + "_bwd", grid=(s // tm,), in_specs=specs(tm) + [out_spec(tm)],
            out_specs=[pl.BlockSpec((tm, width), lambda i: (i, 0)), pl.BlockSpec((1, PAIR), lambda i: (0, 0))],
            out_shape=[jax.ShapeDtypeStruct((s, width), f32), jax.ShapeDtypeStruct((1, PAIR), f32)],
            scratch_shapes=[pltpu.VMEM((tm, PAIR), f32)],
            compiler_params=_cparams(dimension_semantics=("arbitrary",)),
        )(t, w2, gsum, perm, cos2, sin2, dout)

    @jax.custom_vjp
    def op(t, w2, gsum, perm, cos2, sin2):
        return forward(t, w2, gsum, perm, cos2, sin2)

    def op_fwd(*args):
        return forward(*args), args

    def op_bwd(res, dout):
        dt, dw = backward(*res, dout)
        return (dt, dw) + tuple(jnp.zeros_like(r) for r in res[2:])

    op.defvjp(op_fwd, op_bwd)

    def apply(t, w, cos2, sin2):
        gsum, perm = _pair_consts()
        return op(t, jnp.concatenate([w, w], axis=-1), gsum, perm, cos2, sin2)

    return apply


def _fn_softplus(rows, gp, cp, ks):
    (x,), (b,) = rows, gp
    v = x + b
    return (jnp.maximum(v, 0.0) + jnp.log(1.0 + jnp.exp(-jnp.abs(v))),), ()


def _fn_ssd_gate(rows, gp, cp, ks):
    (y, z), (nw,) = rows, gp
    return (_rms(y * _silu(z)) * nw,), ()


def _fn_merge(rows, gp, cp, ks):
    ao, so, ga, gs = rows
    return (jax.nn.sigmoid(ga) * ao + jax.nn.sigmoid(gs) * so,), ()


def _fn_res_norm(rows, gp, cp, ks):
    (x, mo), (g1, nw, sc, sh) = rows, gp
    x1 = x + g1 * mo
    return (x1, (_rms(x1) * nw) * (1.0 + sc) + sh), ()


def _fn_loss(rows, gp, cp, ks):
    (x1, ff), (g2,), (tgt,) = rows, gp, ks
    err = x1 + g2 * ff - tgt
    return (), (0.5 * jnp.sum(jnp.sum(err * err, axis=-1, keepdims=True), axis=0, keepdims=True) / D_MODEL,)


HALO = 8


def _conv_tiles(s, c):
    return _pick(s, (512, 256, 128)), _pick(c, (512, 256, 128))


def _halo_specs(tm, tc, s):
    nb = tm // HALO
    last = s // HALO - 1
    cur = pl.BlockSpec((tm, tc), lambda j, i: (i, j))
    prev = pl.BlockSpec((HALO, tc), lambda j, i: (jnp.maximum(i * nb - 1, 0), j))
    nxt = pl.BlockSpec((HALO, tc), lambda j, i: (jnp.minimum((i + 1) * nb, last), j))
    return cur, prev, nxt


def _fill_halo(buf, cur, prev, nxt, tm, i, n_i):
    buf[HALO:HALO + tm, :] = cur[...]
    buf[0:HALO, :] = jnp.where(i > 0, prev[...], 0.0)
    buf[HALO + tm:, :] = jnp.where(i < n_i - 1, nxt[...], 0.0)


def _conv_fwd(x, w, b, shard):
    s, c = x.shape
    tm, tc = _conv_tiles(s, c)
    n_i, n_j = s // tm, c // tc

    def body(cur, prev, nxt, w_ref, b_ref, shard_ref, o_ref, gath_ref, buf, send_sems, recv_sems, local_sem):
        j, i = pl.program_id(0), pl.program_id(1)
        start, finish = _gather_phases(shard_ref, gath_ref, send_sems, recv_sems, local_sem)

        @pl.when((j == 0) & (i == 0))
        def _():
            start()

        _fill_halo(buf, cur, prev, nxt, tm, i, n_i)
        pre = jnp.zeros((tm, tc), f32) + b_ref[...]
        for k in range(D_CONV):
            pre = pre + buf[HALO - 2 + k:HALO - 2 + k + tm, :] * w_ref[k:k + 1, :]
        o_ref[...] = _silu(pre)

        @pl.when((j == n_j - 1) & (i == n_i - 1))
        def _():
            finish()

    cur, prev, nxt = _halo_specs(tm, tc, s)
    hbm = pl.BlockSpec(memory_space=pl.ANY)
    return pl.pallas_call(
        body, name="conv_silu_fwd", grid=(n_j, n_i),
        in_specs=[cur, prev, nxt, pl.BlockSpec((D_CONV, tc), lambda j, i: (0, j)),
                  pl.BlockSpec((1, tc), lambda j, i: (0, j)), hbm],
        out_specs=[pl.BlockSpec((tm, tc), lambda j, i: (i, j)), hbm],
        out_shape=[jax.ShapeDtypeStruct((s, c), f32), jax.ShapeDtypeStruct((N_DEV,) + shard.shape, shard.dtype)],
        scratch_shapes=[pltpu.VMEM((tm + 2 * HALO, tc), f32)] + COMM_SEMS,
        compiler_params=_cparams(dimension_semantics=("arbitrary", "arbitrary")),
    )(x, x, x, w, b, shard)


def _conv_bwd(x, w, b, dy, g):
    s, c = x.shape
    tm, tc = _conv_tiles(s, c)
    n_i, n_j = s // tm, c // tc
    ext = tm + 8

    def body(cur, prev, nxt, dcur, dprev, dnxt, w_ref, b_ref, g_ref, dx_ref, dw_ref, db_ref, recv_ref,
             xbuf, dbuf, pbuf, send_sems, recv_sems, local_sem):
        j, i = pl.program_id(0), pl.program_id(1)
        start, finish = _scatter_phases(g_ref, recv_ref, send_sems, recv_sems, local_sem)

        @pl.when((j == 0) & (i == 0))
        def _():
            start()

        _fill_halo(xbuf, cur, prev, nxt, tm, i, n_i)
        _fill_halo(dbuf, dcur, dprev, dnxt, tm, i, n_i)
        pre = jnp.zeros((ext, tc), f32) + b_ref[...]
        for k in range(D_CONV):
            pre = pre + xbuf[2 + k:2 + k + ext, :] * w_ref[k:k + 1, :]
        sg = jax.nn.sigmoid(pre)
        pbuf[...] = dbuf[4:4 + ext, :] * (sg * (1.0 + pre * (1.0 - sg)))
        dx = jnp.zeros((tm, tc), f32)
        for k in range(D_CONV):
            dx = dx + pbuf[6 - k:6 - k + tm, :] * w_ref[k:k + 1, :]
        dx_ref[...] = dx

        @pl.when(i == 0)
        def _():
            dw_ref[...] = jnp.zeros_like(dw_ref)
            db_ref[...] = jnp.zeros_like(db_ref)

        dpre = pbuf[4:4 + tm, :]
        db_ref[...] += jnp.sum(dpre, axis=0, keepdims=True)
        for k in range(D_CONV):
            dw_ref[k:k + 1, :] += jnp.sum(dpre * xbuf[HALO - 2 + k:HALO - 2 + k + tm, :], axis=0, keepdims=True)

        @pl.when((j == n_j - 1) & (i == n_i - 1))
        def _():
            finish()

    cur, prev, nxt = _halo_specs(tm, tc, s)
    hbm = pl.BlockSpec(memory_space=pl.ANY)
    return pl.pallas_call(
        body, name="conv_silu_bwd", grid=(n_j, n_i),
        in_specs=[cur, prev, nxt, cur, prev, nxt, pl.BlockSpec((D_CONV, tc), lambda j, i: (0, j)),
                  pl.BlockSpec((1, tc), lambda j, i: (0, j)), hbm],
        out_specs=[pl.BlockSpec((tm, tc), lambda j, i: (i, j)), pl.BlockSpec((D_CONV, tc), lambda j, i: (0, j)),
                   pl.BlockSpec((1, tc), lambda j, i: (0, j)), hbm],
        out_shape=[jax.ShapeDtypeStruct((s, c), f32), jax.ShapeDtypeStruct((D_CONV, c), f32),
                   jax.ShapeDtypeStruct((1, c), f32), jax.ShapeDtypeStruct(g.shape, g.dtype)],
        scratch_shapes=[pltpu.VMEM((tm + 2 * HALO, tc), f32), pltpu.VMEM((tm + 2 * HALO, tc), f32),
                        pltpu.VMEM((ext, tc), f32)] + COMM_SEMS,
        compiler_params=_cparams(dimension_semantics=("arbitrary", "arbitrary")),
    )(x, x, x, dy, dy, dy, w, b, g)


@jax.custom_vjp
def conv_silu_comm(x, w, b, shard, recv_like):
    act, gathered = _conv_fwd(x, w, b, shard)
    return (act, gathered) + tuple(jnp.zeros(shp, f32) for shp in LATE_SHAPES)


def _conv_silu_comm_fwd(x, w, b, shard, recv_like):
    return conv_silu_comm(x, w, b, shard, recv_like), (x, w, b, shard)


def _conv_silu_comm_bwd(res, cts):
    x, w, b, shard = res
    dx, dw, db, recv = _conv_bwd(x, w, b, cts[0], _pack_late_grads(dict(zip(LATE, cts[2:]))))
    return dx, dw, db, jnp.zeros_like(shard), recv


conv_silu_comm.defvjp(_conv_silu_comm_fwd, _conv_silu_comm_bwd)


ATT_SCALE = HEAD_DIM ** -0.5
Q_SCALE = ATT_SCALE * math.log2(math.e)
LN2 = math.log(2.0)
REP = N_Q_HEADS // N_KV_HEADS


HP = 2
assert REP % HP == 0


def _attn_fwd(q, k, v):
    s, dh = q.shape[0], HEAD_DIM
    hq = q.shape[1] // dh
    tq = _pick(s, (256, 128))

    v1 = jnp.concatenate([v, jnp.ones(v.shape[:2] + (1,), v.dtype), jnp.zeros(v.shape[:2] + (dh - 1,), v.dtype)],
                         axis=-1)

    def body(q_ref, k_ref, v_ref, o_ref, p_ref, linv_ref):
        for j in range(HP):
            sl = slice(j * dh, (j + 1) * dh)
            sc = lax.dot_general(q_ref[:, sl], k_ref[0], _DIMS["nt"], preferred_element_type=f32)
            m = jnp.max(sc, axis=-1, keepdims=True)
            p = jnp.exp2(sc - m).astype(bf16)
            p_ref[j] = p
            o1 = jnp.dot(p, v_ref[0], preferred_element_type=f32)
            linv = 1.0 / o1[:, dh:dh + 1]
            o_ref[:, sl] = (o1[:, :dh] * linv).astype(o_ref.dtype)
            linv_ref[j] = linv

    return pl.pallas_call(
        body, name="attn_fwd", grid=(hq // HP, s // tq),
        in_specs=[pl.BlockSpec((tq, HP * dh), lambda h, i: (i, h)),
                  pl.BlockSpec((1, s, dh), lambda h, i: (h * HP // REP, 0, 0)),
                  pl.BlockSpec((1, s, 2 * dh), lambda h, i: (h * HP // REP, 0, 0))],
        out_specs=[pl.BlockSpec((tq, HP * dh), lambda h, i: (i, h)),
                   pl.BlockSpec((HP, tq, s), lambda h, i: (h, i, 0)),
                   pl.BlockSpec((HP, tq, 1), lambda h, i: (h, i, 0))],
        out_shape=[jax.ShapeDtypeStruct((s, hq * dh), bf16), jax.ShapeDtypeStruct((hq, s, s), bf16),
                   jax.ShapeDtypeStruct((hq, s, 1), f32)],
        compiler_params=_cparams(dimension_semantics=("parallel", "arbitrary")),
    )(q, k, v1)


def _attn_bwd(p, do, o, q, k, v, linv):
    hq, s, _ = p.shape
    dh = HEAD_DIM
    tq = _pick(s, (256, 128))

    def body(p_ref, do_ref, o_ref, q_ref, k_ref, v_ref, linv_ref, dq_ref, dkt_ref, dvt_ref):
        @pl.when(pl.program_id(1) == 0)
        def _():
            dkt_ref[...] = jnp.zeros_like(dkt_ref)
            dvt_ref[...] = jnp.zeros_like(dvt_ref)

        for j in range(HP):
            sl = slice(j * dh, (j + 1) * dh)
            pp, doh, li = p_ref[j], do_ref[:, sl], linv_ref[j]
            do32 = doh.astype(f32)
            d = jnp.sum(do32 * o_ref[:, sl].astype(f32), axis=-1, keepdims=True)
            dp = lax.dot_general(doh, v_ref[0], _DIMS["nt"], preferred_element_type=f32)
            ds = (pp.astype(f32) * ((dp - d) * li)).astype(bf16)
            dq_ref[:, sl] = jnp.dot(ds, k_ref[0], preferred_element_type=f32) * LN2
            dvt_ref[j] += lax.dot_general((do32 * li).astype(bf16), pp, _DIMS["tn"], preferred_element_type=f32)
            dkt_ref[j] += lax.dot_general(q_ref[:, sl], ds, _DIMS["tn"], preferred_element_type=f32)

    def row():
        return pl.BlockSpec((tq, HP * dh), lambda h, i: (i, h))

    return pl.pallas_call(
        body, name="attn_bwd", grid=(hq // HP, s // tq),
        in_specs=[pl.BlockSpec((HP, tq, s), lambda h, i: (h, i, 0)), row(), row(), row(),
                  pl.BlockSpec((1, s, dh), lambda h, i: (h * HP // REP, 0, 0)),
                  pl.BlockSpec((1, s, dh), lambda h, i: (h * HP // REP, 0, 0)),
                  pl.BlockSpec((HP, tq, 1), lambda h, i: (h, i, 0))],
        out_specs=[row(), pl.BlockSpec((HP, dh, s), lambda h, i: (h, 0, 0)),
                   pl.BlockSpec((HP, dh, s), lambda h, i: (h, 0, 0))],
        out_shape=[jax.ShapeDtypeStruct((s, hq * dh), f32), jax.ShapeDtypeStruct((hq, dh, s), f32),
                   jax.ShapeDtypeStruct((hq, dh, s), f32)],
        compiler_params=_cparams(dimension_semantics=("parallel", "arbitrary")),
    )(p, do, o, q, k, v, linv)


@jax.custom_vjp
def attention(q, k, v):
    return _attn_fwd(q, k, v)[0]


def _attention_fwd(q, k, v):
    o, p, linv = _attn_fwd(q, k, v)
    return o, (q, k, v, o, p, linv)


def _attention_bwd(res, do):
    q, k, v, o, p, linv = res
    s = q.shape[0]
    dq, dkt, dvt = _attn_bwd(p, do.astype(bf16), o, q, k, v, linv)

    def per_kv_head(t):
        return jnp.swapaxes(t.reshape(N_KV_HEADS, REP, HEAD_DIM, s).sum(axis=1), 1, 2)

    return dq.astype(q.dtype), (per_kv_head(dkt) * LN2).astype(k.dtype), per_kv_head(dvt).astype(v.dtype)


attention.defvjp(_attention_fwd, _attention_bwd)


HPG = N_SSD_HEADS // N_SSD_GROUPS
GW = HPG * SSD_HEAD_DIM
NEG = -1e30
SPLIT_ROWS = 32


def _ssd_consts():
    k = np.arange(SPLIT_ROWS)[:, None]
    live = k < 3 * HPG
    sel_chunk = ((k % HPG) == (np.arange(HPG * CHUNK)[None, :] // CHUNK)) & live
    sel_head = ((k % HPG) == (np.arange(GW)[None, :] // SSD_HEAD_DIM)) & live
    return jnp.asarray(sel_chunk, bf16), jnp.asarray(sel_head, bf16)


def _split3(x):
    hi = x.astype(bf16).astype(f32)
    r1 = x - hi
    mid = r1.astype(bf16).astype(f32)
    lo = (r1 - mid).astype(bf16).astype(f32)
    return jnp.concatenate([hi, mid, lo, jnp.zeros_like(hi)], axis=0).astype(bf16)


def _tn(a, b):
    return lax.dot_general(a, b, _DIMS["tn"], preferred_element_type=f32)


def _nt(a, b):
    return lax.dot_general(a, b, _DIMS["nt"], preferred_element_type=f32)


def _nn(a, b):
    return jnp.dot(a, b, preferred_element_type=f32)


def _head_sum(sel8, x):
    hi = x.astype(bf16)
    lo = (x - hi.astype(f32)).astype(bf16)
    return _nt(sel8, hi) + _nt(sel8, lo)


def _ssd_masks(reverse):
    r = lax.broadcasted_iota(jnp.int32, (CHUNK, CHUNK), 0)
    c = lax.broadcasted_iota(jnp.int32, (CHUNK, CHUNK), 1)
    lower, upper = r >= c, r <= c
    return (upper, lower) if reverse else (lower, upper)


def _ssd_in_specs(cidx):
    return [pl.BlockSpec((CHUNK, D_INNER), lambda c: (cidx(c), 0)),
            pl.BlockSpec((CHUNK, GN), lambda c: (cidx(c), D_INNER // GN)),
            pl.BlockSpec((CHUNK, GN), lambda c: (cidx(c), D_INNER // GN + 1)),
            pl.BlockSpec((N_SSD_HEADS, CHUNK), lambda c: (0, cidx(c))),
            pl.BlockSpec((N_SSD_HEADS, 1), lambda c: (0, 0)),
            pl.BlockSpec((SPLIT_ROWS, HPG * CHUNK), lambda c: (0, 0)),
            pl.BlockSpec((SPLIT_ROWS, GW), lambda c: (0, 0))]


def _ssd_chunk_common(dtt_ref, a_ref, et_ref, mask_t):
    dtt = dtt_ref[...]
    et = jnp.dot(dtt * a_ref[...], mask_t.astype(f32), precision=HIGHEST, preferred_element_type=f32)
    et_ref[...] = et
    return dtt, et


def _ssd_group_common(g, dtt, et, selc_ref, selh_ref, xs_ref, b_ref, c_ref, last):
    gr = slice(g * HPG, (g + 1) * HPG)
    e3 = _split3(et[gr])
    col = _tn(e3, selc_ref[...])
    eb = _tn(e3, selh_ref[...])
    dtb = _tn(_split3(dtt[gr]), selh_ref[...])
    tbc = eb[last:last + 1, :]
    xs = xs_ref[:, g * GW:(g + 1) * GW]
    bg = b_ref[:, g * D_STATE:(g + 1) * D_STATE].astype(bf16)
    cg = c_ref[:, g * D_STATE:(g + 1) * D_STATE].astype(bf16)
    return col, eb, dtb, tbc, xs, bg, cg


def _ssd_fwd(xbc, dtt, a_col, reverse, y_prev=None, dexp=None):
    s = xbc.shape[0]
    nc = s // CHUNK
    cidx = (lambda c: nc - 1 - c) if reverse else (lambda c: c)
    last = 0 if reverse else CHUNK - 1
    selc, selh = _ssd_consts()
    final = y_prev is not None
    n_in = 9 if final else 7

    def body(*refs):
        xs_ref, b_ref, c_ref, dtt_ref, a_ref, selc_ref, selh_ref = refs[:7]
        y_ref, st_ref, ht_ref, et_ref = refs[n_in:]

        @pl.when(pl.program_id(0) == 0)
        def _():
            ht_ref[...] = jnp.zeros_like(ht_ref)

        mask, mask_t = _ssd_masks(reverse)
        dtt_v, et = _ssd_chunk_common(dtt_ref, a_ref, et_ref, mask_t)
        for g in range(N_SSD_GROUPS):
            col, eb, dtb, tbc, xs, bg, cg = _ssd_group_common(g, dtt_v, et, selc_ref, selh_ref, xs_ref, b_ref, c_ref,
                                                              last)
            xd = xs * dtb
            cb = _nt(cg, bg)
            ht = ht_ref[g]
            st_ref[0, g] = ht
            yoff = _nn(cg, ht.astype(bf16)) * jnp.exp(eb)
            for j in range(HPG):
                h = g * HPG + j
                hs = slice(j * SSD_HEAD_DIM, (j + 1) * SSD_HEAD_DIM)
                lam = jnp.exp(jnp.where(mask, col[:, j * CHUNK:(j + 1) * CHUNK] - et_ref[h:h + 1, :], NEG))
                yj = _nn((cb * lam).astype(bf16), xd[:, hs].astype(bf16)) + yoff[:, hs]
                cols = slice(g * GW + j * SSD_HEAD_DIM, g * GW + (j + 1) * SSD_HEAD_DIM)
                if final:
                    yj = yj + refs[7][:, cols] + xs[:, hs] * refs[8][:, cols]
                y_ref[:, cols] = yj
            ht_ref[g] = jnp.exp(tbc) * ht + _tn(bg, (xd * jnp.exp(tbc - eb)).astype(bf16))

    y_spec = pl.BlockSpec((CHUNK, D_INNER), lambda c: (cidx(c), 0))
    extra_specs = [y_spec, pl.BlockSpec((1, D_INNER), lambda c: (0, 0))] if final else []
    return pl.pallas_call(
        body, name="ssd_fwd_rev" if reverse else "ssd_fwd", grid=(nc,),
        in_specs=_ssd_in_specs(cidx) + extra_specs,
        out_specs=[y_spec, pl.BlockSpec((1, N_SSD_GROUPS, D_STATE, GW), lambda c: (cidx(c), 0, 0, 0))],
        out_shape=[jax.ShapeDtypeStruct((s, D_INNER), f32),
                   jax.ShapeDtypeStruct((nc, N_SSD_GROUPS, D_STATE, GW), f32)],
        scratch_shapes=[pltpu.VMEM((N_SSD_GROUPS, D_STATE, GW), f32), pltpu.VMEM((N_SSD_HEADS, CHUNK), f32)],
        compiler_params=_cparams(dimension_semantics=("arbitrary",)),
    )(xbc, xbc, xbc, dtt, a_col, selc, selh, *((y_prev, dexp) if final else ()))


def _ssd_bwd(xbc, dtt, a_col, states, dy, reverse, dxbc_prev=None, dexp=None):
    s = xbc.shape[0]
    nc = s // CHUNK
    cidx = (lambda c: c) if reverse else (lambda c: nc - 1 - c)
    last = 0 if reverse else CHUNK - 1
    selc, selh = _ssd_consts()
    final = dxbc_prev is not None
    n_in = 11 if final else 9
    n_out = 4 if final else 3

    def body(*refs):
        xs_ref, b_ref, c_ref, dtt_ref, a_ref, selc_ref, selh_ref, st_ref, dy_ref = refs[:9]
        dxbc_ref, ddtt_ref, da_ref = refs[n_in:n_in + 3]
        dh_ref, et_ref, det_ref, det2_ref, ddt_ref, q_ref = refs[n_in + n_out:]
        if final:
            prev_ref, dexp_ref, ddexp_ref = refs[9], refs[10], refs[n_in + 3]

        @pl.when(pl.program_id(0) == 0)
        def _():
            dh_ref[...] = jnp.zeros_like(dh_ref)
            da_ref[...] = jnp.zeros_like(da_ref)
            if final:
                ddexp_ref[...] = jnp.zeros_like(ddexp_ref)

        mask, mask_t = _ssd_masks(reverse)
        dtt_v, et = _ssd_chunk_common(dtt_ref, a_ref, et_ref, mask_t)
        sel8 = selh_ref[0:HPG, :]
        is_last = lax.broadcasted_iota(jnp.int32, (CHUNK, GW), 0) == last
        for g in range(N_SSD_GROUPS):
            col, eb, dtb, tbc, xs, bg, cg = _ssd_group_common(g, dtt_v, et, selc_ref, selh_ref, xs_ref, b_ref, c_ref,
                                                              last)
            xd = xs * dtb
            cb = _nt(cg, bg)
            cbt = _nt(bg, cg)
            exp_t = jnp.exp(tbc)
            dfac = jnp.exp(tbc - eb)
            ht = st_ref[0, g]
            dhn = dh_ref[g]
            ht16, dhn16 = ht.astype(bf16), dhn.astype(bf16)
            dy = dy_ref[:, g * GW:(g + 1) * GW]
            dye = dy * jnp.exp(eb)
            dye16 = dye.astype(bf16)
            dc = _nt(dye16, ht16)
            dh_ref[g] = exp_t * dhn + _tn(cg, dye16)
            deb = dye * _nn(cg, ht16)
            xdd = xd * dfac
            dxdd = _nn(bg, dhn16)
            db = _nt(xdd.astype(bf16), dhn16)
            dxd_state = dxdd * dfac
            ddf = dxdd * xdd
            dtbc = jnp.sum(ddf, axis=0, keepdims=True) + exp_t * jnp.sum(dhn * ht, axis=0, keepdims=True)
            deb = deb - ddf + jnp.where(is_last, dtbc, 0.0)
            dcb = jnp.zeros((CHUNK, CHUNK), f32)
            dcbt = jnp.zeros((CHUNK, CHUNK), f32)
            for j in range(HPG):
                h = g * HPG + j
                hs = slice(j * SSD_HEAD_DIM, (j + 1) * SSD_HEAD_DIM)
                colj = col[:, j * CHUNK:(j + 1) * CHUNK]
                row = et_ref[h:h + 1, :]
                lam = jnp.exp(jnp.where(mask, colj - row, NEG))
                lam_t = jnp.exp(jnp.where(mask_t, row - colj, NEG))
                xdj, dyj = xd[:, hs].astype(bf16), dy[:, hs].astype(bf16)
                t1 = _nt(dyj, xdj) * lam
                t2 = _nt(xdj, dyj) * lam_t
                dcb, dcbt = dcb + t1, dcbt + t2
                det_ref[h:h + 1, :] = -jnp.sum(t1 * cb - t2 * cbt, axis=0, keepdims=True)
                dxdj = _nn((cbt * lam_t).astype(bf16), dyj) + dxd_state[:, hs]
                cols = slice(g * GW + j * SSD_HEAD_DIM, g * GW + (j + 1) * SSD_HEAD_DIM)
                dxs = dxdj * dtb[:, hs]
                if final:
                    dxs = dxs + prev_ref[:, cols] + dy[:, hs] * dexp_ref[:, cols]
                dxbc_ref[:, cols] = dxs
                q_ref[:, hs] = dxdj * xs[:, hs]
            b_cols = slice(D_INNER + g * D_STATE, D_INNER + (g + 1) * D_STATE)
            c_cols = slice(D_INNER + GN + g * D_STATE, D_INNER + GN + (g + 1) * D_STATE)
            db = db + _nn(dcbt.astype(bf16), cg)
            dc = dc + _nn(dcb.astype(bf16), bg)
            if final:
                db, dc = db + prev_ref[:, b_cols], dc + prev_ref[:, c_cols]
                ddexp_ref[:, g * GW:(g + 1) * GW] += jnp.sum(dy * xs, axis=0, keepdims=True)
            dxbc_ref[:, b_cols] = db
            dxbc_ref[:, c_cols] = dc
            det2_ref[g * HPG:(g + 1) * HPG, :] = _head_sum(sel8, deb)
            ddt_ref[g * HPG:(g + 1) * HPG, :] = _head_sum(sel8, q_ref[...])
        dat = jnp.dot(det_ref[...] + det2_ref[...], mask.astype(f32), precision=HIGHEST, preferred_element_type=f32)
        ddtt_ref[...] = ddt_ref[...] + dat * a_ref[...]
        da_ref[...] += jnp.sum(dat * dtt_v, axis=1, keepdims=True)

    in_specs = _ssd_in_specs(cidx) + [
        pl.BlockSpec((1, N_SSD_GROUPS, D_STATE, GW), lambda c: (cidx(c), 0, 0, 0)),
        pl.BlockSpec((CHUNK, D_INNER), lambda c: (cidx(c), 0))]
    hl = pltpu.VMEM((N_SSD_HEADS, CHUNK), f32)
    dxbc_spec = pl.BlockSpec((CHUNK, CONV_DIM), lambda c: (cidx(c), 0))
    dexp_spec = pl.BlockSpec((1, D_INNER), lambda c: (0, 0))
    return pl.pallas_call(
        body, name="ssd_bwd_rev" if reverse else "ssd_bwd", grid=(nc,),
        in_specs=in_specs + ([dxbc_spec, dexp_spec] if final else []),
        out_specs=[dxbc_spec, pl.BlockSpec((N_SSD_HEADS, CHUNK), lambda c: (0, cidx(c))),
                   pl.BlockSpec((N_SSD_HEADS, 1), lambda c: (0, 0))] + ([dexp_spec] if final else []),
        out_shape=[jax.ShapeDtypeStruct((s, CONV_DIM), f32), jax.ShapeDtypeStruct((N_SSD_HEADS, s), f32),
                   jax.ShapeDtypeStruct((N_SSD_HEADS, 1), f32)]
        + ([jax.ShapeDtypeStruct((1, D_INNER), f32)] if final else []),
        scratch_shapes=[pltpu.VMEM((N_SSD_GROUPS, D_STATE, GW), f32), hl, hl, hl, hl, pltpu.VMEM((CHUNK, GW), f32)],
        compiler_params=_cparams(dimension_semantics=("arbitrary",)),
    )(xbc, xbc, xbc, dtt, a_col, selc, selh, states, dy, *((dxbc_prev, dexp) if final else ()))


@jax.custom_vjp
def ssd_bidir(xbc, dtt, a_col, dexp):
    y_f, _ = _ssd_fwd(xbc, dtt[:N_SSD_HEADS], a_col[:N_SSD_HEADS], False)
    return _ssd_fwd(xbc, dtt[N_SSD_HEADS:], a_col[N_SSD_HEADS:], True, y_prev=y_f, dexp=dexp)[0]


def _ssd_bidir_fwd(xbc, dtt, a_col, dexp):
    y_f, st_f = _ssd_fwd(xbc, dtt[:N_SSD_HEADS], a_col[:N_SSD_HEADS], False)
    y, st_b = _ssd_fwd(xbc, dtt[N_SSD_HEADS:], a_col[N_SSD_HEADS:], True, y_prev=y_f, dexp=dexp)
    return y, (xbc, dtt, a_col, dexp, st_f, st_b)


def _ssd_bidir_bwd(res, dy):
    xbc, dtt, a_col, dexp, st_f, st_b = res
    dxbc_f, ddtt_f, da_f = _ssd_bwd(xbc, dtt[:N_SSD_HEADS], a_col[:N_SSD_HEADS], st_f, dy, False)
    dxbc, ddtt_b, da_b, ddexp = _ssd_bwd(xbc, dtt[N_SSD_HEADS:], a_col[N_SSD_HEADS:], st_b, dy, True,
                                         dxbc_prev=dxbc_f, dexp=dexp)
    return dxbc, jnp.concatenate([ddtt_f, ddtt_b], axis=0), jnp.concatenate([da_f, da_b], axis=0), ddexp


ssd_bidir.defvjp(_ssd_bidir_fwd, _ssd_bidir_bwd)


W_NAMES = PROJ_NAMES + ("attn_out", "ssd_out", "o", "mlp1", "mlp2")


def _rope_tables(s):
    rows = s // GRID_W
    pos_row = np.repeat(np.arange(rows), GRID_W).astype(np.float32)
    pos_col = np.tile(np.arange(GRID_W), rows).astype(np.float32)
    axis_dim = HEAD_DIM // 2
    inv_freq = np.float32(ROPE_THETA) ** (-np.arange(0, axis_dim, 2, dtype=np.float32) / np.float32(axis_dim))
    ang_r = pos_row[:, None] * inv_freq[None, :].astype(np.float32)
    ang_c = pos_col[:, None] * inv_freq[None, :].astype(np.float32)
    cos = np.concatenate([np.cos(ang_r), np.cos(ang_r), np.cos(ang_c), np.cos(ang_c)] * 2, axis=-1)
    sin = np.concatenate([np.sin(ang_r), np.sin(ang_r), np.sin(ang_c), np.sin(ang_c)] * 2, axis=-1)
    return jnp.asarray(cos, f32), jnp.asarray(sin, f32)


def _rope_perm():
    p = np.zeros((HEAD_DIM, HEAD_DIM), np.float32)
    for j in range(HEAD_DIM):
        if (j % 32) < 16:
            p[j + 16, j] = -1.0
        else:
            p[j - 16, j] = 1.0
    return p


def local_loss(x, mod, small, recv_in_like, recv_late_like, wfull, late_shard, target):
    s = x.shape[0]
    lin = {n: make_linear("lin_" + n) for n in LATE if not n.startswith("mlp")}
    wfull, wgrads = dict(wfull), {}
    shift1, scale1, gate1, shift2, scale2, gate2 = [mod[i] for i in range(6)]

    norm_mod = make_rowwise("norm_mod", _fn_norm_mod, [(D_MODEL, bf16)])
    (h,), _ = norm_mod((x,), (small["norm1_w"], scale1, shift1), (), ())

    proj = dict(zip(PROJ_NAMES, in_proj(h, tuple(wfull[n] for n in PROJ_NAMES), recv_in_like)))

    cos, sin = _rope_tables(s)

    def heads(t, nh):
        return t.reshape(s, nh, HEAD_DIM).transpose(1, 0, 2)

    qr = make_head_rope("q_norm_rope", N_Q_HEADS, Q_SCALE, False)(proj["q"], small["q_norm_w"], cos, sin)
    kr = make_head_rope("k_norm_rope", N_KV_HEADS, 1.0, True)(proj["k"], small["k_norm_w"], cos, sin)
    vh = heads(proj["v"], N_KV_HEADS).astype(bf16)
    att = attention(qr, kr, vh)

    xbc, gathered, *carriers = conv_silu_comm(proj["xbc"], small["conv_w"], small["conv_b"], late_shard,
                                              recv_late_like)
    wfull.update(_split_late(gathered))
    wgrads.update(zip(LATE, carriers))
    ao = lin["attn_out"](att, wfull["attn_out"], wgrads["attn_out"])
    softplus = make_rowwise("dt_softplus", _fn_softplus, [(2 * N_SSD_HEADS, f32)])
    (dt,), _ = softplus((proj["dt"][:, :2 * N_SSD_HEADS],), (small["dt_bias"].reshape(1, 2 * N_SSD_HEADS),), (), ())
    a_neg = -jnp.exp(small["A_log"])
    dexp = jnp.repeat(small["ssd_D"].reshape(N_SSD_HEADS), SSD_HEAD_DIM).reshape(1, D_INNER)
    y = ssd_bidir(xbc, dt.T, a_neg.reshape(2 * N_SSD_HEADS, 1), dexp)
    ssd_gate = make_rowwise("ssd_gate", _fn_ssd_gate, [(D_INNER, bf16)], tm_pref=128)
    (ssd_out,), _ = ssd_gate((y, proj["z"]), (small["ssd_norm_w"],), (), ())
    so = lin["ssd_out"](ssd_out, wfull["ssd_out"], wgrads["ssd_out"])

    merge = make_rowwise("merge", _fn_merge, [(D_MODEL, bf16)])
    (merged,), _ = merge((ao, so, proj["ga"], proj["gs"]), (), (), ())
    mo = lin["o"](merged, wfull["o"], wgrads["o"])

    res_norm = make_rowwise("res_norm", _fn_res_norm, [(D_MODEL, f32), (D_MODEL, bf16)])
    (x1, h2), _ = res_norm((x, mo), (gate1, small["norm2_w"], scale2, shift2), (), ())
    ff = mlp(h2, wfull["mlp1"], wgrads["mlp1"], wfull["mlp2"], wgrads["mlp2"])
    loss_op = make_rowwise("loss", _fn_loss, [], [(1, 1)])
    _, (loss,) = loss_op((x1, ff), (gate2,), (), (target,))
    return loss[0, 0]


_BC1 = 1.0 - ADAM_B1 ** ADAM_STEP
_BC2 = 1.0 - ADAM_B2 ** ADAM_STEP


def _adamw(w, g, m, v):
    m = ADAM_B1 * m + (1.0 - ADAM_B1) * g
    v = ADAM_B2 * v + (1.0 - ADAM_B2) * (g * g)
    delta = -ADAM_LR * ((m / _BC1) / (jnp.sqrt(v / _BC2) + ADAM_EPS) + ADAM_WD * w)
    return delta, m, v


def _ada_fwd(c_all, w, b):
    n = w.shape[1]

    def body(c_ref, w_ref, b_ref, o_ref):
        o_ref[...] = jnp.dot(_silu(c_ref[...]), w_ref[...], precision=HIGHEST, preferred_element_type=f32) + b_ref[...]

    return pl.pallas_call(body, name="ada_fwd", out_shape=jax.ShapeDtypeStruct((N_DEV, n), f32),
                          compiler_params=_cparams())(c_all, w, b)


def _ada_bwd_adamw(c_all, dmod, w, m, v):
    d, n = w.shape
    tr = _pick(d, (256, 128))

    def body(c_ref, dm_ref, w_ref, m_ref, v_ref, g_ref, dl_ref, mo_ref, vo_ref):
        g = lax.dot_general(_silu(c_ref[...]), dm_ref[...], _DIMS["tn"], precision=HIGHEST,
                            preferred_element_type=f32)
        g_ref[...] = g
        dl_ref[...], mo_ref[...], vo_ref[...] = _adamw(w_ref[...], g, m_ref[...], v_ref[...])

    blk = pl.BlockSpec((tr, n), lambda i: (i, 0))
    return pl.pallas_call(
        body, name="ada_bwd_adamw", grid=(d // tr,),
        in_specs=[pl.BlockSpec((N_DEV, tr), lambda i: (0, i)), pl.BlockSpec((N_DEV, n), lambda i: (0, 0)), blk, blk, blk],
        out_specs=[blk] * 4, out_shape=[jax.ShapeDtypeStruct((d, n), f32)] * 4,
        compiler_params=_cparams(dimension_semantics=("parallel",)),
    )(c_all, dmod, w, m, v)


def _sum_over_mesh(g):
    def body(g_ref, o_ref):
        acc = g_ref[0]
        for d in range(1, N_DEV):
            acc = acc + g_ref[d]
        o_ref[...] = acc

    return pl.pallas_call(body, name="sum_small", out_shape=jax.ShapeDtypeStruct(g.shape[1:], f32),
                          compiler_params=_cparams())(g)


def _adamw_small(w, g, m, v):
    def body(w_ref, g_ref, m_ref, v_ref, dl_ref, mo_ref, vo_ref):
        dl_ref[...], mo_ref[...], vo_ref[...] = _adamw(w_ref[...], g_ref[...], m_ref[...], v_ref[...])

    return pl.pallas_call(body, name="adamw_small", out_shape=[jax.ShapeDtypeStruct(w.shape, f32)] * 3,
                          compiler_params=_cparams())(w, g, m, v)


def _sum_adamw(recv, w, m, v, name):
    _, r, c = recv.shape
    tr = _pick(r, (256, 128, 64, 16))

    def body(g_ref, w_ref, m_ref, v_ref, go_ref, dl_ref, mo_ref, vo_ref):
        g = g_ref[0].astype(f32)
        for d in range(1, N_DEV):
            g = g + g_ref[d].astype(f32)
        go_ref[...] = g
        dl_ref[...], mo_ref[...], vo_ref[...] = _adamw(w_ref[...], g, m_ref[...], v_ref[...])

    blk = pl.BlockSpec((tr, c), lambda i: (i, 0))
    return pl.pallas_call(
        body, name=name, grid=(r // tr,),
        in_specs=[pl.BlockSpec((N_DEV, tr, c), lambda i: (0, i, 0)), blk, blk, blk],
        out_specs=[blk] * 4, out_shape=[jax.ShapeDtypeStruct((r, c), f32)] * 4,
        compiler_params=_cparams(dimension_semantics=("parallel",)),
    )(recv, w, m, v)


def _pack_small(arrs):
    parts = []
    for a in arrs:
        flat = a.reshape(-1).astype(f32)
        parts.append(jnp.pad(flat, (0, (-flat.shape[0]) % LANE)))
    flat = jnp.concatenate(parts)
    flat = jnp.pad(flat, (0, (-flat.shape[0]) % (8 * LANE)))
    return flat.reshape(-1, LANE)


def _unpack_small(packed, shapes):
    flat = packed.reshape(-1)
    out, off = [], 0
    for shp in shapes:
        n = int(np.prod(shp))
        out.append(flat[off:off + n].reshape(shp))
        off += n + (-n) % LANE
    return out


BIG = ("w_attn_out", "w_ssd_out", "w_o", "w_mlp1", "w_mlp2")
BIG_ROWS = (N_Q_HEADS * HEAD_DIM // N_DEV, D_INNER // N_DEV, D_MODEL // N_DEV,
            D_MODEL * (D_FF // N_DEV) // PACK_COLS, D_FF // N_DEV)
N_IN_SHARD = D_IN_PROJ // N_DEV
assert sum(BIG_ROWS) % 16 == 0


def _pack_big(shards, dtype):
    return jnp.concatenate([s.astype(dtype).reshape(-1, PACK_COLS) for s in shards], axis=0)


def _unpack_big(packed, shapes):
    out, off = [], 0
    for rows, shp in zip(BIG_ROWS, shapes):
        out.append(packed[off:off + rows].reshape(shp))
        off += rows
    return out


LATE = ("attn_out", "ssd_out", "o", "mlp1", "mlp2")
LATE_SHAPES = ((N_Q_HEADS * HEAD_DIM, D_MODEL), (D_INNER, D_MODEL), (D_MODEL, D_MODEL), (D_MODEL, D_FF),
               (D_FF, D_MODEL))


def _split_w_in(g_in):
    w_in = g_in.transpose(1, 0, 2).reshape(D_MODEL, D_IN_PROJ)
    w = {}
    off = 0
    for name, size in zip(PROJ_NAMES, PROJ_SIZES):
        w[name] = w_in[:, off:off + size]
        off += size
    w["dt"] = jnp.pad(w["dt"], ((0, 0), (0, DT_PAD - 2 * N_SSD_HEADS)))
    return w


def _split_late(g):
    offs = np.cumsum((0,) + BIG_ROWS)
    sl = [g[:, offs[i]:offs[i + 1]] for i in range(len(BIG))]
    return {"attn_out": sl[0].reshape(LATE_SHAPES[0]), "ssd_out": sl[1].reshape(LATE_SHAPES[1]),
            "o": sl[2].reshape(LATE_SHAPES[2]),
            "mlp1": sl[3].reshape(N_DEV, D_MODEL, D_FF // N_DEV).transpose(1, 0, 2).reshape(LATE_SHAPES[3]),
            "mlp2": sl[4].reshape(LATE_SHAPES[4])}


def _pack_in_grads(gw):
    gw = {n: g.astype(bf16) for n, g in gw.items()}
    gw["dt"] = gw["dt"][:, :2 * N_SSD_HEADS]
    g_in = jnp.concatenate([gw[n] for n in PROJ_NAMES], axis=1)
    return g_in.reshape(D_MODEL, N_DEV, N_IN_SHARD).transpose(1, 0, 2)


def _pack_late_grads(gw):
    gw = {n: g.astype(bf16) for n, g in gw.items()}
    parts = [
        gw["attn_out"].reshape(N_DEV, -1, PACK_COLS),
        gw["ssd_out"].reshape(N_DEV, -1, PACK_COLS),
        gw["o"].reshape(N_DEV, -1, PACK_COLS),
        gw["mlp1"].reshape(D_MODEL, N_DEV, D_FF // N_DEV).transpose(1, 0, 2).reshape(N_DEV, -1, PACK_COLS),
        gw["mlp2"].reshape(N_DEV, -1, PACK_COLS),
    ]
    return jnp.concatenate(parts, axis=1)


SMALL = ("norm1_w", "norm2_w", "q_norm_w", "k_norm_w", "conv_w", "conv_b", "A_log", "dt_bias", "ssd_D", "ssd_norm_w")


def kernel(x, c, w_ada, b_ada, norm1_w, norm2_w, w_in, q_norm_w, k_norm_w, conv_w, conv_b, A_log, dt_bias, ssd_D, ssd_norm_w, w_attn_out, w_ssd_out, w_o, w_mlp1, w_mlp2, loss_target, m_w_ada, m_b_ada, m_norm1_w, m_norm2_w, m_w_in, m_q_norm_w, m_k_norm_w, m_conv_w, m_conv_b, m_A_log, m_dt_bias, m_ssd_D, m_ssd_norm_w, m_w_attn_out, m_w_ssd_out, m_w_o, m_w_mlp1, m_w_mlp2, v_w_ada, v_b_ada, v_norm1_w, v_norm2_w, v_w_in, v_q_norm_w, v_k_norm_w, v_conv_w, v_conv_b, v_A_log, v_dt_bias, v_ssd_D, v_ssd_norm_w, v_w_attn_out, v_w_ssd_out, v_w_o, v_w_mlp1, v_w_mlp2):
    args = dict(locals())
    me = _my_index()
    n_ada = 6 * D_MODEL // N_DEV
    n_cw = CONV_DIM // N_DEV

    blk = jnp.zeros((8, D_MODEL), f32)
    blk = blk.at[0:1, :].set(c)
    blk = blk.at[1:1 + D_CONV, :n_cw].set(conv_w[0])
    g0 = _all_gather(blk, "gather_c_convw", in_vmem=True)
    c_all = g0[:, 0, :]
    conv_w_full = g0[:, 1:1 + D_CONV, :n_cw].transpose(1, 0, 2).reshape(D_CONV, CONV_DIM)

    b_shard = lax.dynamic_slice(b_ada, (0, me * n_ada), (1, n_ada))
    mod_cols = _ada_fwd(c_all, w_ada[0], b_shard)
    g1 = _all_gather(mod_cols, "gather_mod", in_vmem=True)
    mod_mine = lax.dynamic_index_in_dim(g1, me, axis=1, keepdims=False)
    mod = mod_mine.reshape(6, 1, D_MODEL)

    big_shapes = [args[n].shape[1:] for n in BIG]
    late_shard = _pack_big([args[n][0] for n in BIG], bf16)
    wfull = _split_w_in(_all_gather(w_in[0].astype(bf16), "gather_w_in", in_vmem=False))
    recv_in_like = jnp.zeros((N_DEV,) + w_in.shape[1:], bf16)
    recv_late_like = jnp.zeros((N_DEV,) + late_shard.shape, bf16)

    small = {"norm1_w": norm1_w, "norm2_w": norm2_w, "q_norm_w": q_norm_w, "k_norm_w": k_norm_w,
             "conv_w": conv_w_full, "conv_b": conv_b, "A_log": A_log[0], "dt_bias": dt_bias[0], "ssd_D": ssd_D,
             "ssd_norm_w": ssd_norm_w}

    loss, (gx, gmod, gsmall, recv_in, recv_late) = jax.value_and_grad(local_loss, argnums=(0, 1, 2, 3, 4))(
        x[0], mod, small, recv_in_like, recv_late_like, wfull, late_shard, loss_target[0])

    small_list = [gmod, gsmall["norm1_w"], gsmall["norm2_w"], gsmall["q_norm_w"], gsmall["k_norm_w"], gsmall["conv_w"],
                  gsmall["conv_b"], gsmall["A_log"], gsmall["dt_bias"], gsmall["ssd_D"], gsmall["ssd_norm_w"],
                  loss.reshape(1)]
    small_shapes = [a.shape for a in small_list]
    g2 = _all_gather(_pack_small(small_list), "gather_small_grads", in_vmem=True)
    summed = _unpack_small(_sum_over_mesh(g2), small_shapes)
    loss_total = summed[-1][0]
    g_b_ada = summed[0].reshape(1, 6 * D_MODEL)
    g_small = dict(zip(SMALL, summed[1:-1]))
    g_conv_w = lax.dynamic_slice(g_small["conv_w"], (0, me * n_cw), (D_CONV, n_cw))

    dmod_all = g2[:, :6 * D_MODEL // LANE, :].reshape(N_DEV, 6 * D_MODEL)
    dmod_shard = lax.dynamic_slice(dmod_all, (0, me * n_ada), (N_DEV, n_ada))
    ada = _ada_bwd_adamw(c_all, dmod_shard, w_ada[0], m_w_ada[0], v_w_ada[0])

    small_grads = {"b_ada": g_b_ada, "norm1_w": g_small["norm1_w"], "norm2_w": g_small["norm2_w"],
                   "q_norm_w": g_small["q_norm_w"], "k_norm_w": g_small["k_norm_w"], "conv_w": g_conv_w[None],
                   "conv_b": g_small["conv_b"], "A_log": g_small["A_log"][None], "dt_bias": g_small["dt_bias"][None],
                   "ssd_D": g_small["ssd_D"], "ssd_norm_w": g_small["ssd_norm_w"]}
    sm_names = list(small_grads)
    sm_shapes = [args[n].shape for n in sm_names]
    sm = _adamw_small(_pack_small([args[n] for n in sm_names]), _pack_small([small_grads[n] for n in sm_names]),
                      _pack_small([args["m_" + n] for n in sm_names]), _pack_small([args["v_" + n] for n in sm_names]))
    sm_delta, sm_m, sm_v = [dict(zip(sm_names, _unpack_small(t, sm_shapes))) for t in sm]
    small_grads = {n: small_grads[n].reshape(args[n].shape) for n in sm_names}

    w_in_out = _sum_adamw(recv_in, w_in[0], m_w_in[0], v_w_in[0], "sum_adamw_w_in")
    big = _sum_adamw(recv_late, _pack_big([args[n][0] for n in BIG], f32),
                     _pack_big([args["m_" + n][0] for n in BIG], f32),
                     _pack_big([args["v_" + n][0] for n in BIG], f32), "sum_adamw")
    big_g, big_delta, big_m, big_v = [dict(zip(BIG, [t[None] for t in _unpack_big(p, big_shapes)])) for p in big]
    big_g["w_in"], big_delta["w_in"], big_m["w_in"], big_v["w_in"] = [t[None] for t in w_in_out]

    names = ("w_ada", "b_ada", "norm1_w", "norm2_w", "w_in", "q_norm_w", "k_norm_w", "conv_w", "conv_b", "A_log",
             "dt_bias", "ssd_D", "ssd_norm_w", "w_attn_out", "w_ssd_out", "w_o", "w_mlp1", "w_mlp2")
    grads, deltas, new_m, new_v = {}, {}, {}, {}
    for n in names:
        if n == "w_ada":
            grads[n], deltas[n], new_m[n], new_v[n] = [t[None] for t in ada]
        elif n in big_g:
            grads[n], deltas[n], new_m[n], new_v[n] = big_g[n], big_delta[n], big_m[n], big_v[n]
        else:
            grads[n], deltas[n], new_m[n], new_v[n] = small_grads[n], sm_delta[n], sm_m[n], sm_v[n]
    return (loss_total, gx[None], *[grads[n] for n in names], *[deltas[n] for n in names],
            *[new_m[n] for n in names], *[new_v[n] for n in names])
```

```python
import functools
import math

import jax
import jax.numpy as jnp
import numpy as np
from jax import lax
from jax.experimental import pallas as pl
from jax.experimental.pallas import tpu as pltpu

f32 = jnp.float32
bf16 = jnp.bfloat16
HIGHEST = lax.Precision.HIGHEST
MESH = pl.DeviceIdType.MESH

N_DEV = 8
D_MODEL = 1024
GRID_W = 64
N_Q_HEADS = 16
N_KV_HEADS = 4
HEAD_DIM = 64
ROPE_THETA = 10000.0
D_INNER = 2048
SSD_HEAD_DIM = 64
N_SSD_HEADS = 32
N_SSD_GROUPS = 4
D_STATE = 128
D_CONV = 5
CHUNK = 128
D_FF = 4096
EPS = 1e-6
CONV_DIM = D_INNER + 2 * N_SSD_GROUPS * D_STATE
GN = N_SSD_GROUPS * D_STATE
PROJ_NAMES = ("q", "k", "v", "xbc", "z", "dt", "ga", "gs")
PROJ_SIZES = (N_Q_HEADS * HEAD_DIM, N_KV_HEADS * HEAD_DIM, N_KV_HEADS * HEAD_DIM, CONV_DIM, D_INNER,
              2 * N_SSD_HEADS, D_MODEL, D_MODEL)
D_IN_PROJ = sum(PROJ_SIZES)
PROJ_DTYPES = (jnp.bfloat16, jnp.bfloat16, jnp.bfloat16, jnp.float32, jnp.bfloat16, jnp.float32, jnp.bfloat16,
               jnp.bfloat16)
DT_PAD = 128

ADAM_LR, ADAM_B1, ADAM_B2, ADAM_EPS, ADAM_WD, ADAM_STEP = 0.001, 0.9, 0.999, 1e-08, 0.01, 10

V7X_VMEM_LIMIT = 56 * 1024 * 1024
LANE = 128
PACK_COLS = 1024


def _cparams(**kw):
    return pltpu.CompilerParams(vmem_limit_bytes=V7X_VMEM_LIMIT, **kw)


def _pick(dim, prefs):
    for p in prefs:
        if dim % p == 0:
            return p
    return dim


def _my_index():
    return 4 * lax.axis_index("x") + 2 * lax.axis_index("y") + lax.axis_index("c")


COMM_SEMS = [pltpu.SemaphoreType.DMA((7,)), pltpu.SemaphoreType.DMA((7,)), pltpu.SemaphoreType.DMA]


def _gather_phases(x_ref, out_ref, send_sems, recv_sems, local_sem):
    x, y, cc = lax.axis_index("x"), lax.axis_index("y"), lax.axis_index("c")
    me, sibling = (x, y, cc), (x, y, 1 - cc)
    chips = [(1 - x, y), (x, 1 - y), (1 - x, 1 - y)]

    def slot(px, py, pc):
        return out_ref.at[4 * px + 2 * py + pc]

    def copy(k, blk, to, src=None):
        return pltpu.make_async_remote_copy(
            src_ref=slot(*blk) if src is None else src, dst_ref=slot(*blk),
            send_sem=send_sems.at[k], recv_sem=recv_sems.at[k], device_id=to, device_id_type=MESH)

    mine = pltpu.make_async_copy(x_ref, slot(*me), local_sem)
    first = [copy(0, me, sibling, src=x_ref)]
    first += [copy(1 + j, me, (*chip, cc), src=x_ref) for j, chip in enumerate(chips)]
    passed = [copy(4 + j, (*chip, cc), sibling) for j, chip in enumerate(chips)]

    def start():
        mine.start()
        for cp in first:
            cp.start()

    def finish():
        for j, chip in enumerate(chips):
            copy(1 + j, (*chip, cc), me).wait_recv()
            passed[j].start()
        copy(0, sibling, me).wait_recv()
        for j, chip in enumerate(chips):
            copy(4 + j, (*chip, 1 - cc), me).wait_recv()
        for cp in first + passed:
            cp.wait_send()
        mine.wait()

    return start, finish


def _scatter_phases(g_ref, out_ref, send_sems, recv_sems, local_sem):
    x, y, cc = lax.axis_index("x"), lax.axis_index("y"), lax.axis_index("c")
    me = 4 * x + 2 * y + cc
    mine = pltpu.make_async_copy(g_ref.at[me], out_ref.at[me], local_sem)

    def copy(k):
        fx, fy, fc = (k >> 2) & 1, (k >> 1) & 1, k & 1
        px = x + fx - 2 * x * fx
        py = y + fy - 2 * y * fy
        pc = cc + fc - 2 * cc * fc
        peer = 4 * px + 2 * py + pc
        send = pltpu.make_async_remote_copy(
            src_ref=g_ref.at[peer], dst_ref=out_ref.at[me],
            send_sem=send_sems.at[k - 1], recv_sem=recv_sems.at[k - 1],
            device_id=(px, py, pc), device_id_type=MESH)
        recv = pltpu.make_async_remote_copy(
            src_ref=g_ref.at[peer], dst_ref=out_ref.at[peer],
            send_sem=send_sems.at[k - 1], recv_sem=recv_sems.at[k - 1],
            device_id=(px, py, pc), device_id_type=MESH)
        return send, recv

    pairs = [copy(k) for k in range(1, N_DEV)]

    def start():
        mine.start()
        for send, _ in pairs:
            send.start()

    def finish():
        for _, recv in pairs:
            recv.wait_recv()
        for send, _ in pairs:
            send.wait_send()
        mine.wait()

    return start, finish


def _all_gather(block, name, in_vmem):
    r, c = block.shape

    def body(x_ref, out_ref, send_sems, recv_sems, local_sem):
        start, finish = _gather_phases(x_ref, out_ref, send_sems, recv_sems, local_sem)
        start()
        finish()

    space = pltpu.VMEM if in_vmem else pl.ANY
    return pl.pallas_call(
        body, name=name,
        out_shape=jax.ShapeDtypeStruct((N_DEV, r, c), block.dtype),
        in_specs=[pl.BlockSpec(memory_space=space)],
        out_specs=pl.BlockSpec(memory_space=space),
        scratch_shapes=[pltpu.SemaphoreType.DMA((7,)), pltpu.SemaphoreType.DMA((7,)), pltpu.SemaphoreType.DMA],
    )(block)


def _scatter_blocks(g, name):
    _, r, c = g.shape

    def body(g_ref, out_ref, send_sems, recv_sems, local_sem):
        start, finish = _scatter_phases(g_ref, out_ref, send_sems, recv_sems, local_sem)
        start()
        finish()

    return pl.pallas_call(
        body, name=name,
        out_shape=jax.ShapeDtypeStruct(g.shape, g.dtype),
        in_specs=[pl.BlockSpec(memory_space=pl.ANY)],
        out_specs=pl.BlockSpec(memory_space=pl.ANY),
        scratch_shapes=[pltpu.SemaphoreType.DMA((7,)), pltpu.SemaphoreType.DMA((7,)), pltpu.SemaphoreType.DMA],
    )(g)


_DIMS = {"nn": (((1,), (0,)), ((), ())), "nt": (((1,), (1,)), ((), ())), "tn": (((0,), (0,)), ((), ()))}


def _matmul(a, b, mode, out_dtype, name, epilogue=None, side=None):
    if mode == "nn":
        (m, k), (_, n) = a.shape, b.shape
    elif mode == "nt":
        (m, k), (n, _) = a.shape, b.shape
    else:
        (k, m), (_, n) = a.shape, b.shape
    tm = _pick(m, (1024, 512, 256, 128))
    if mode == "tn":
        tn = _pick(n, (1536, 1024, 512, 256, 128))
        tk = _pick(k, (1024, 512, 256, 128))
    else:
        tn = _pick(n, (1024, 512, 384, 256, 128))
        tk = _pick(k, (1024, 512, 256, 128))
    nk = k // tk
    dims = _DIMS[mode]
    n_in = 3 if epilogue == "drelu2" else 2
    n_out = 2 if epilogue == "relu2" else 1

    def body(*refs):
        a_ref, b_ref = refs[:2]
        outs, acc_ref = refs[n_in:n_in + n_out], refs[n_in + n_out]
        kk = pl.program_id(2)
        part = lax.dot_general(a_ref[...].astype(bf16), b_ref[...].astype(bf16), dims, preferred_element_type=f32)

        def finish(acc):
            if epilogue == "relu2":
                r = jnp.maximum(acc, 0.0)
                outs[0][...] = acc.astype(out_dtype)
                outs[1][...] = (r * r).astype(out_dtype)
            elif epilogue == "drelu2":
                outs[0][...] = (acc * (2.0 * jnp.maximum(refs[2][...].astype(f32), 0.0))).astype(out_dtype)
            else:
                outs[0][...] = acc.astype(out_dtype)

        if nk == 1:
            finish(part)
        else:
            @pl.when(kk == 0)
            def _():
                acc_ref[...] = part

            @pl.when(kk > 0)
            def _():
                acc_ref[...] += part

            @pl.when(kk == nk - 1)
            def _():
                finish(acc_ref[...])

    if mode == "tn":
        a_spec = pl.BlockSpec((tk, tm), lambda i, j, kk: (kk, i))
    else:
        a_spec = pl.BlockSpec((tm, tk), lambda i, j, kk: (i, kk))
    if mode == "nt":
        b_spec = pl.BlockSpec((tn, tk), lambda i, j, kk: (j, kk))
    else:
        b_spec = pl.BlockSpec((tk, tn), lambda i, j, kk: (kk, j))
    o_spec = pl.BlockSpec((tm, tn), lambda i, j, kk: (i, j))
    o_shape = jax.ShapeDtypeStruct((m, n), out_dtype)
    res = pl.pallas_call(
        body, name=name, grid=(m // tm, n // tn, nk),
        in_specs=[a_spec, b_spec] + ([o_spec] if epilogue == "drelu2" else []),
        out_specs=[o_spec] * n_out, out_shape=[o_shape] * n_out,
        scratch_shapes=[pltpu.VMEM((tm, tn), f32)],
        compiler_params=_cparams(dimension_semantics=("parallel", "parallel", "arbitrary")),
    )(*((a, b, side) if epilogue == "drelu2" else (a, b)))
    return res if n_out == 2 else res[0]


@jax.custom_vjp
def mlp(h, w1, w1grad, w2, w2grad):
    _, r = _matmul(h, w1, "nn", bf16, "mlp1_fwd", epilogue="relu2")
    return _matmul(r, w2, "nn", f32, "mlp2_fwd")


def _mlp_fwd(h, w1, w1grad, w2, w2grad):
    u, r = _matmul(h, w1, "nn", bf16, "mlp1_fwd", epilogue="relu2")
    return _matmul(r, w2, "nn", f32, "mlp2_fwd"), (h, w1, w2, u, r)


def _mlp_bwd(res, dy):
    h, w1, w2, u, r = res
    du = _matmul(dy, w2, "nt", bf16, "mlp2_dgrad", epilogue="drelu2", side=u)
    dw2 = _matmul(r, dy, "tn", f32, "mlp2_wgrad")
    dh = _matmul(du, w1, "nt", h.dtype, "mlp1_dgrad")
    dw1 = _matmul(h, du, "tn", f32, "mlp1_wgrad")
    return dh, jnp.zeros_like(w1), dw1, jnp.zeros_like(w2), dw2


mlp.defvjp(_mlp_fwd, _mlp_bwd)


def make_linear(name):
    @jax.custom_vjp
    def linear(a, w, wgrad):
        return _matmul(a, w, "nn", f32, name + "_fwd")

    def fwd(a, w, wgrad):
        return linear(a, w, wgrad), (a, w)

    def bwd(res, dy):
        a, w = res
        da = _matmul(dy, w, "nt", a.dtype, name + "_dgrad")
        dw = _matmul(a, dy, "tn", f32, name + "_wgrad")
        return da, jnp.zeros_like(w), dw

    linear.defvjp(fwd, bwd)
    return linear


def _in_proj_dgrad(dys, ws, g):
    s, d = dys[0].shape[0], ws[0].shape[0]
    tm = _pick(s, (512, 256, 128))
    tks = [min(w.shape[1], 1024) for w in ws]
    steps = [w.shape[1] // tk for w, tk in zip(ws, tks)]
    starts = [sum(steps[:p]) for p in range(len(ws))]
    total = sum(steps)
    n_p, n_i = len(ws), s // tm
    assert steps[0] == 1

    def body(*refs):
        dy_refs, w_refs, g_ref = refs[:n_p], refs[n_p:2 * n_p], refs[2 * n_p]
        dh_ref, recv_ref, acc_ref, send_sems, recv_sems, local_sem = refs[2 * n_p + 1:]
        i, t = pl.program_id(0), pl.program_id(1)
        start, finish = _scatter_phases(g_ref, recv_ref, send_sems, recv_sems, local_sem)

        @pl.when((i == 0) & (t == 0))
        def _():
            start()

        for p in range(n_p):
            @pl.when((t >= starts[p]) & (t < starts[p] + steps[p]))
            def _(p=p):
                part = lax.dot_general(dy_refs[p][...].astype(bf16), w_refs[p][...], _DIMS["nt"],
                                       preferred_element_type=f32)
                if p == 0:
                    acc_ref[...] = part
                else:
                    acc_ref[...] += part

        @pl.when(t == total - 1)
        def _():
            dh_ref[...] = acc_ref[...].astype(dh_ref.dtype)

        @pl.when((i == n_i - 1) & (t == total - 1))
        def _():
            finish()

    def piece_map(p, rows):
        def index_map(i, t):
            blk = jnp.clip(t - starts[p], 0, steps[p] - 1)
            return (i, blk) if rows else (0, blk)

        return index_map

    hbm = pl.BlockSpec(memory_space=pl.ANY)
    in_specs = [pl.BlockSpec((tm, tks[p]), piece_map(p, True)) for p in range(n_p)]
    in_specs += [pl.BlockSpec((d, tks[p]), piece_map(p, False)) for p in range(n_p)]
    return pl.pallas_call(
        body, name="in_proj_dgrad", grid=(n_i, total), in_specs=in_specs + [hbm],
        out_specs=[pl.BlockSpec((tm, d), lambda i, t: (i, 0)), hbm],
        out_shape=[jax.ShapeDtypeStruct((s, d), bf16), jax.ShapeDtypeStruct(g.shape, g.dtype)],
        scratch_shapes=[pltpu.VMEM((tm, d), f32)] + COMM_SEMS,
        compiler_params=_cparams(dimension_semantics=("arbitrary", "arbitrary")),
    )(*dys, *ws, g)


@jax.custom_vjp
def in_proj(h, ws, recv_like):
    return tuple(_matmul(h, w, "nn", dt, "lin_" + n + "_fwd") for n, w, dt in zip(PROJ_NAMES, ws, PROJ_DTYPES))


def _in_proj_fwd(h, ws, recv_like):
    return in_proj(h, ws, recv_like), (h, ws)


def _in_proj_bwd(res, dys):
    h, ws = res
    dws = {n: _matmul(h, dy, "tn", f32, "lin_" + n + "_wgrad") for n, dy in zip(PROJ_NAMES, dys)}
    dh, recv = _in_proj_dgrad(dys, ws, _pack_in_grads(dws))
    return dh.astype(h.dtype), tuple(jnp.zeros_like(w) for w in ws), recv


in_proj.defvjp(_in_proj_fwd, _in_proj_bwd)


def make_rowwise(name, fn, row_out, sum_out=(), tm_pref=256):
    def specs(rows, gpars, cpars, consts, tm):
        s = [pl.BlockSpec((tm, r.shape[1]), lambda i: (i, 0)) for r in rows]
        s += [pl.BlockSpec(p.shape, lambda i: (0, 0)) for p in gpars]
        s += [pl.BlockSpec(p.shape, lambda i: (0, 0)) for p in cpars]
        for cst in consts:
            nb = cst.shape[0] // tm
            s.append(pl.BlockSpec((tm, cst.shape[1]), lambda i, nb=nb: (i % nb, 0)))
        return s

    def tile_rows(rows, consts):
        r = rows[0].shape[0]
        common = math.gcd(r, *[cst.shape[0] for cst in consts])
        tm = _pick(common, (tm_pref, 512, 256, 128, 64, 32, 16, 8))
        return r, tm

    def forward(rows, gpars, cpars, consts):
        r, tm = tile_rows(rows, consts)
        nr, ng, nc, nk = len(rows), len(gpars), len(cpars), len(consts)

        def body(*refs):
            ins = refs[:nr + ng + nc + nk]
            outs = refs[nr + ng + nc + nk:]
            rv = [t[...].astype(f32) for t in ins[:nr]]
            gv = [t[...].astype(f32) for t in ins[nr:nr + ng]]
            cv = [t[...] for t in ins[nr + ng:nr + ng + nc]]
            kv = [t[...].astype(f32) for t in ins[nr + ng + nc:]]
            ro, so = fn(rv, gv, cv, kv)
            for o_ref, val in zip(outs[:len(row_out)], ro):
                o_ref[...] = val.astype(o_ref.dtype)
            if sum_out:
                @pl.when(pl.program_id(0) == 0)
                def _():
                    for o_ref in outs[len(row_out):]:
                        o_ref[...] = jnp.zeros_like(o_ref)
                for o_ref, val in zip(outs[len(row_out):], so):
                    o_ref[...] += val

        out_specs = [pl.BlockSpec((tm, w), lambda i: (i, 0)) for w, _ in row_out]
        out_specs += [pl.BlockSpec(shp, lambda i: (0, 0)) for shp in sum_out]
        out_shape = [jax.ShapeDtypeStruct((r, w), dt) for w, dt in row_out]
        out_shape += [jax.ShapeDtypeStruct(shp, f32) for shp in sum_out]
        res = pl.pallas_call(
            body, name=name + "_fwd", grid=(r // tm,),
            in_specs=specs(rows, gpars, cpars, consts, tm), out_specs=out_specs, out_shape=out_shape,
            compiler_params=_cparams(dimension_semantics=("arbitrary",)),
        )(*rows, *gpars, *cpars, *consts)
        return tuple(res[:len(row_out)]), tuple(res[len(row_out):])

    def backward(rows, gpars, cpars, consts, d_ro, d_so):
        r, tm = tile_rows(rows, consts)
        nr, ng, nc, nk = len(rows), len(gpars), len(cpars), len(consts)
        n_in = nr + ng + nc + nk + len(row_out) + len(sum_out)

        def body(*refs):
            ins, outs = refs[:n_in], refs[n_in:]
            rv = [t[...].astype(f32) for t in ins[:nr]]
            gv = [t[...].astype(f32) for t in ins[nr:nr + ng]]
            cv = [t[...] for t in ins[nr + ng:nr + ng + nc]]
            kv = [t[...].astype(f32) for t in ins[nr + ng + nc:nr + ng + nc + nk]]
            o = nr + ng + nc + nk
            dro = [t[...].astype(f32) for t in ins[o:o + len(row_out)]]
            dso = [t[...] for t in ins[o + len(row_out):]]
            _, vjp = jax.vjp(lambda a, b: tuple(tuple(t) for t in fn(a, b, cv, kv)), rv, gv)
            drv, dgv = vjp((tuple(dro), tuple(dso)))
            for o_ref, val in zip(outs[:nr], drv):
                o_ref[...] = val.astype(o_ref.dtype)
            if ng:
                @pl.when(pl.program_id(0) == 0)
                def _():
                    for o_ref in outs[nr:]:
                        o_ref[...] = jnp.zeros_like(o_ref)
                for o_ref, val in zip(outs[nr:], dgv):
                    o_ref[...] += val

        in_specs = specs(rows, gpars, cpars, consts, tm)
        in_specs += [pl.BlockSpec((tm, w), lambda i: (i, 0)) for w, _ in row_out]
        in_specs += [pl.BlockSpec(shp, lambda i: (0, 0)) for shp in sum_out]
        out_specs = [pl.BlockSpec((tm, t.shape[1]), lambda i: (i, 0)) for t in rows]
        out_specs += [pl.BlockSpec(p.shape, lambda i: (0, 0)) for p in gpars]
        out_shape = [jax.ShapeDtypeStruct(t.shape, t.dtype) for t in rows]
        out_shape += [jax.ShapeDtypeStruct(p.shape, f32) for p in gpars]
        res = pl.pallas_call(
            body, name=name + "_bwd", grid=(r // tm,),
            in_specs=in_specs, out_specs=out_specs, out_shape=out_shape,
            compiler_params=_cparams(dimension_semantics=("arbitrary",)),
        )(*rows, *gpars, *cpars, *consts, *d_ro, *d_so)
        return tuple(res[:nr]), tuple(res[nr:])

    @jax.custom_vjp
    def op(rows, gpars, cpars, consts):
        return forward(rows, gpars, cpars, consts)

    def op_fwd(rows, gpars, cpars, consts):
        return forward(rows, gpars, cpars, consts), (rows, gpars, cpars, consts)

    def op_bwd(res, cts):
        rows, gpars, cpars, consts = res
        d_ro, d_so = cts
        drows, dg = backward(rows, gpars, cpars, consts, d_ro, d_so)
        dg = tuple(d.astype(p.dtype) for d, p in zip(dg, gpars))
        return (drows, dg, tuple(jnp.zeros_like(p) for p in cpars), tuple(jnp.zeros_like(k) for k in consts))

    op.defvjp(op_fwd, op_bwd)
    return op


def _rms(x):
    return x * lax.rsqrt(jnp.mean(x * x, axis=-1, keepdims=True) + EPS)


def _silu(x):
    return x * jax.nn.sigmoid(x)


def _fn_norm_mod(rows, gp, cp, ks):
    (x,), (nw, sc, sh) = rows, gp
    return ((_rms(x) * nw) * (1.0 + sc) + sh,), ()


PAIR = 2 * HEAD_DIM


def _exact_dot(a, m):
    hi = a.astype(bf16)
    lo = (a - hi.astype(f32)).astype(bf16)
    return jnp.dot(hi, m, preferred_element_type=f32) + jnp.dot(lo, m, preferred_element_type=f32)


def _make_sel_dot(sign):
    @jax.custom_vjp
    def sel_dot(a, m):
        return _exact_dot(a, m)

    def fwd(a, m):
        return _exact_dot(a, m), m

    def bwd(m, g):
        return sign * _exact_dot(g, m), jnp.zeros_like(m)

    sel_dot.defvjp(fwd, bwd)
    return sel_dot


_head_sum_dot = _make_sel_dot(1.0)
_rope_perm_dot = _make_sel_dot(-1.0)


def _pair_norm_rope(t, w2, gsum, perm, cos2, sin2, out_scale):
    ss = _head_sum_dot(t * t, gsum)
    u = t * lax.rsqrt(ss * (1.0 / HEAD_DIM) + EPS) * w2
    return (u * cos2 + _rope_perm_dot(u, perm) * sin2) * out_scale


def _pair_consts():
    eye = np.eye(2, dtype=np.float32)
    gsum = np.kron(eye, np.ones((HEAD_DIM, HEAD_DIM), np.float32))
    return jnp.asarray(gsum, bf16), jnp.asarray(np.kron(eye, _rope_perm()), bf16)


def make_head_rope(name, nh, out_scale, head_major):
    width = nh * HEAD_DIM
    fn = functools.partial(_pair_norm_rope, out_scale=out_scale)

    def out_spec(tm):
        if head_major:
            return pl.BlockSpec((nh, tm, HEAD_DIM), lambda i: (0, i, 0))
        return pl.BlockSpec((tm, width), lambda i: (i, 0))

    def specs(tm):
        def full(shp):
            return pl.BlockSpec(shp, lambda i: (0, 0))

        return [pl.BlockSpec((tm, width), lambda i: (i, 0)), full((1, PAIR)), full((PAIR, PAIR)), full((PAIR, PAIR)),
                pl.BlockSpec((tm, PAIR), lambda i: (i, 0)), pl.BlockSpec((tm, PAIR), lambda i: (i, 0))]

    def forward(t, w2, gsum, perm, cos2, sin2):
        s = t.shape[0]
        tm = _pick(s, (512, 256, 128))

        def body(t_ref, w_ref, g_ref, p_ref, cos_ref, sin_ref, o_ref):
            for b in range(nh // 2):
                val = fn(t_ref[:, b * PAIR:(b + 1) * PAIR].astype(f32), w_ref[...], g_ref[...], p_ref[...], cos_ref[...],
                         sin_ref[...]).astype(o_ref.dtype)
                if head_major:
                    o_ref[2 * b] = val[:, :HEAD_DIM]
                    o_ref[2 * b + 1] = val[:, HEAD_DIM:]
                else:
                    o_ref[:, b * PAIR:(b + 1) * PAIR] = val

        return pl.pallas_call(
            body, name=name + "_fwd", grid=(s // tm,), in_specs=specs(tm), out_specs=out_spec(tm),
            out_shape=jax.ShapeDtypeStruct((nh, s, HEAD_DIM) if head_major else (s, width), bf16),
            compiler_params=_cparams(dimension_semantics=("arbitrary",)),
        )(t, w2, gsum, perm, cos2, sin2)

    def backward(t, w2, gsum, perm, cos2, sin2, dout):
        s = t.shape[0]
        tm = _pick(s, (512, 256, 128))

        def body(t_ref, w_ref, g_ref, p_ref, cos_ref, sin_ref, do_ref, dt_ref, dw_ref, pair_buf):
            @pl.when(pl.program_id(0) == 0)
            def _():
                dw_ref[...] = jnp.zeros_like(dw_ref)

            g_v, p_v, cos_v, sin_v = g_ref[...], p_ref[...], cos_ref[...], sin_ref[...]
            dw = jnp.zeros((1, PAIR), f32)
            for b in range(nh // 2):
                sl = slice(b * PAIR, (b + 1) * PAIR)
                if head_major:
                    pair_buf[:, :HEAD_DIM] = do_ref[2 * b].astype(f32)
                    pair_buf[:, HEAD_DIM:] = do_ref[2 * b + 1].astype(f32)
                    ct = pair_buf[...]
                else:
                    ct = do_ref[:, sl].astype(f32)
                _, vjp = jax.vjp(lambda a, c: fn(a, c, g_v, p_v, cos_v, sin_v), t_ref[:, sl].astype(f32), w_ref[...])
                dtb, dwb = vjp(ct)
                dt_ref[:, sl] = dtb.astype(dt_ref.dtype)
                dw = dw + dwb
            dw_ref[...] += dw

        return pl.pallas_call(
            body, name=name + "_bwd", grid=(s // tm,), in_specs=specs(tm) + [out_spec(tm)],
            out_specs=[pl.BlockSpec((tm, width), lambda i: (i, 0)), pl.BlockSpec((1, PAIR), lambda i: (0, 0))],
            out_shape=[jax.ShapeDtypeStruct((s, width), t.dtype), jax.ShapeDtypeStruct((1, PAIR), f32)],
            scratch_shapes=[pltpu.VMEM((tm, PAIR), f32)],
            compiler_params=_cparams(dimension_semantics=("arbitrary",)),
        )(t, w2, gsum, perm, cos2, sin2, dout)

    @jax.custom_vjp
    def op(t, w2, gsum, perm, cos2, sin2):
        return forward(t, w2, gsum, perm, cos2, sin2)

    def op_fwd(*args):
        return forward(*args), args

    def op_bwd(res, dout):
        dt, dw = backward(*res, dout)
        return (dt, dw) + tuple(jnp.zeros_like(r) for r in res[2:])

    op.defvjp(op_fwd, op_bwd)

    def apply(t, w, cos2, sin2):
        gsum, perm = _pair_consts()
        return op(t, jnp.concatenate([w, w], axis=-1), gsum, perm, cos2, sin2)

    return apply


def _fn_softplus(rows, gp, cp, ks):
    (x,), (b,) = rows, gp
    v = x + b
    return (jnp.maximum(v, 0.0) + jnp.log(1.0 + jnp.exp(-jnp.abs(v))),), ()


def _fn_ssd_gate(rows, gp, cp, ks):
    (y, z), (nw,) = rows, gp
    return (_rms(y * _silu(z)) * nw,), ()


def _fn_merge(rows, gp, cp, ks):
    ao, so, ga, gs = rows
    return (jax.nn.sigmoid(ga) * ao + jax.nn.sigmoid(gs) * so,), ()


def _fn_res_norm(rows, gp, cp, ks):
    (x, mo), (g1, nw, sc, sh) = rows, gp
    x1 = x + g1 * mo
    return (x1, (_rms(x1) * nw) * (1.0 + sc) + sh), ()


def _fn_loss(rows, gp, cp, ks):
    (x1, ff), (g2,), (tgt,) = rows, gp, ks
    err = x1 + g2 * ff - tgt
    return (), (0.5 * jnp.sum(jnp.sum(err * err, axis=-1, keepdims=True), axis=0, keepdims=True) / D_MODEL,)


HALO = 8


def _conv_tiles(s, c):
    return _pick(s, (512, 256, 128)), _pick(c, (512, 256, 128))


def _halo_specs(tm, tc, s):
    nb = tm // HALO
    last = s // HALO - 1
    cur = pl.BlockSpec((tm, tc), lambda j, i: (i, j))
    prev = pl.BlockSpec((HALO, tc), lambda j, i: (jnp.maximum(i * nb - 1, 0), j))
    nxt = pl.BlockSpec((HALO, tc), lambda j, i: (jnp.minimum((i + 1) * nb, last), j))
    return cur, prev, nxt


def _fill_halo(buf, cur, prev, nxt, tm, i, n_i):
    buf[HALO:HALO + tm, :] = cur[...]
    buf[0:HALO, :] = jnp.where(i > 0, prev[...], 0.0)
    buf[HALO + tm:, :] = jnp.where(i < n_i - 1, nxt[...], 0.0)


def _conv_fwd(x, w, b, shard):
    s, c = x.shape
    tm, tc = _conv_tiles(s, c)
    n_i, n_j = s // tm, c // tc

    def body(cur, prev, nxt, w_ref, b_ref, shard_ref, o_ref, gath_ref, buf, send_sems, recv_sems, local_sem):
        j, i = pl.program_id(0), pl.program_id(1)
        start, finish = _gather_phases(shard_ref, gath_ref, send_sems, recv_sems, local_sem)

        @pl.when((j == 0) & (i == 0))
        def _():
            start()

        _fill_halo(buf, cur, prev, nxt, tm, i, n_i)
        pre = jnp.zeros((tm, tc), f32) + b_ref[...]
        for k in range(D_CONV):
            pre = pre + buf[HALO - 2 + k:HALO - 2 + k + tm, :] * w_ref[k:k + 1, :]
        o_ref[...] = _silu(pre)

        @pl.when((j == n_j - 1) & (i == n_i - 1))
        def _():
            finish()

    cur, prev, nxt = _halo_specs(tm, tc, s)
    hbm = pl.BlockSpec(memory_space=pl.ANY)
    return pl.pallas_call(
        body, name="conv_silu_fwd", grid=(n_j, n_i),
        in_specs=[cur, prev, nxt, pl.BlockSpec((D_CONV, tc), lambda j, i: (0, j)),
                  pl.BlockSpec((1, tc), lambda j, i: (0, j)), hbm],
        out_specs=[pl.BlockSpec((tm, tc), lambda j, i: (i, j)), hbm],
        out_shape=[jax.ShapeDtypeStruct((s, c), f32), jax.ShapeDtypeStruct((N_DEV,) + shard.shape, shard.dtype)],
        scratch_shapes=[pltpu.VMEM((tm + 2 * HALO, tc), f32)] + COMM_SEMS,
        compiler_params=_cparams(dimension_semantics=("arbitrary", "arbitrary")),
    )(x, x, x, w, b, shard)


def _conv_bwd(x, w, b, dy, g):
    s, c = x.shape
    tm, tc = _conv_tiles(s, c)
    n_i, n_j = s // tm, c // tc
    ext = tm + 8

    def body(cur, prev, nxt, dcur, dprev, dnxt, w_ref, b_ref, g_ref, dx_ref, dw_ref, db_ref, recv_ref,
             xbuf, dbuf, pbuf, send_sems, recv_sems, local_sem):
        j, i = pl.program_id(0), pl.program_id(1)
        start, finish = _scatter_phases(g_ref, recv_ref, send_sems, recv_sems, local_sem)

        @pl.when((j == 0) & (i == 0))
        def _():
            start()

        _fill_halo(xbuf, cur, prev, nxt, tm, i, n_i)
        _fill_halo(dbuf, dcur, dprev, dnxt, tm, i, n_i)
        pre = jnp.zeros((ext, tc), f32) + b_ref[...]
        for k in range(D_CONV):
            pre = pre + xbuf[2 + k:2 + k + ext, :] * w_ref[k:k + 1, :]
        sg = jax.nn.sigmoid(pre)
        pbuf[...] = dbuf[4:4 + ext, :] * (sg * (1.0 + pre * (1.0 - sg)))
        dx = jnp.zeros((tm, tc), f32)
        for k in range(D_CONV):
            dx = dx + pbuf[6 - k:6 - k + tm, :] * w_ref[k:k + 1, :]
        dx_ref[...] = dx

        @pl.when(i == 0)
        def _():
            dw_ref[...] = jnp.zeros_like(dw_ref)
            db_ref[...] = jnp.zeros_like(db_ref)

        dpre = pbuf[4:4 + tm, :]
        db_ref[...] += jnp.sum(dpre, axis=0, keepdims=True)
        for k in range(D_CONV):
            dw_ref[k:k + 1, :] += jnp.sum(dpre * xbuf[HALO - 2 + k:HALO - 2 + k + tm, :], axis=0, keepdims=True)

        @pl.when((j == n_j - 1) & (i == n_i - 1))
        def _():
            finish()

    cur, prev, nxt = _halo_specs(tm, tc, s)
    hbm = pl.BlockSpec(memory_space=pl.ANY)
    return pl.pallas_call(
        body, name="conv_silu_bwd", grid=(n_j, n_i),
        in_specs=[cur, prev, nxt, cur, prev, nxt, pl.BlockSpec((D_CONV, tc), lambda j, i: (0, j)),
                  pl.BlockSpec((1, tc), lambda j, i: (0, j)), hbm],
        out_specs=[pl.BlockSpec((tm, tc), lambda j, i: (i, j)), pl.BlockSpec((D_CONV, tc), lambda j, i: (0, j)),
                   pl.BlockSpec((1, tc), lambda j, i: (0, j)), hbm],
        out_shape=[jax.ShapeDtypeStruct((s, c), f32), jax.ShapeDtypeStruct((D_CONV, c), f32),
                   jax.ShapeDtypeStruct((1, c), f32), jax.ShapeDtypeStruct(g.shape, g.dtype)],
        scratch_shapes=[pltpu.VMEM((tm + 2 * HALO, tc), f32), pltpu.VMEM((tm + 2 * HALO, tc), f32),
                        pltpu.VMEM((ext, tc), f32)] + COMM_SEMS,
        compiler_params=_cparams(dimension_semantics=("arbitrary", "arbitrary")),
    )(x, x, x, dy, dy, dy, w, b, g)


@jax.custom_vjp
def conv_silu_comm(x, w, b, shard, recv_like):
    act, gathered = _conv_fwd(x, w, b, shard)
    return (act, gathered) + tuple(jnp.zeros(shp, f32) for shp in LATE_SHAPES)


def _conv_silu_comm_fwd(x, w, b, shard, recv_like):
    return conv_silu_comm(x, w, b, shard, recv_like), (x, w, b, shard)


def _conv_silu_comm_bwd(res, cts):
    x, w, b, shard = res
    dx, dw, db, recv = _conv_bwd(x, w, b, cts[0], _pack_late_grads(dict(zip(LATE, cts[2:]))))
    return dx, dw, db, jnp.zeros_like(shard), recv


conv_silu_comm.defvjp(_conv_silu_comm_fwd, _conv_silu_comm_bwd)


ATT_SCALE = HEAD_DIM ** -0.5
Q_SCALE = ATT_SCALE * math.log2(math.e)
LN2 = math.log(2.0)
REP = N_Q_HEADS // N_KV_HEADS


HP = 2
assert REP % HP == 0


def _attn_fwd(q, k, v):
    s, dh = q.shape[0], HEAD_DIM
    hq = q.shape[1] // dh
    tq = _pick(s, (256, 128))

    v1 = jnp.concatenate([v, jnp.ones(v.shape[:2] + (1,), v.dtype), jnp.zeros(v.shape[:2] + (dh - 1,), v.dtype)],
                         axis=-1)

    def body(q_ref, k_ref, v_ref, o_ref, p_ref, linv_ref):
        for j in range(HP):
            sl = slice(j * dh, (j + 1) * dh)
            sc = lax.dot_general(q_ref[:, sl], k_ref[0], _DIMS["nt"], preferred_element_type=f32)
            m = jnp.max(sc, axis=-1, keepdims=True)
            p = jnp.exp2(sc - m).astype(bf16)
            p_ref[j] = p
            o1 = jnp.dot(p, v_ref[0], preferred_element_type=f32)
            linv = 1.0 / o1[:, dh:dh + 1]
            o_ref[:, sl] = (o1[:, :dh] * linv).astype(o_ref.dtype)
            linv_ref[j] = linv

    return pl.pallas_call(
        body, name="attn_fwd", grid=(hq // HP, s // tq),
        in_specs=[pl.BlockSpec((tq, HP * dh), lambda h, i: (i, h)),
                  pl.BlockSpec((1, s, dh), lambda h, i: (h * HP // REP, 0, 0)),
                  pl.BlockSpec((1, s, 2 * dh), lambda h, i: (h * HP // REP, 0, 0))],
        out_specs=[pl.BlockSpec((tq, HP * dh), lambda h, i: (i, h)),
                   pl.BlockSpec((HP, tq, s), lambda h, i: (h, i, 0)),
                   pl.BlockSpec((HP, tq, 1), lambda h, i: (h, i, 0))],
        out_shape=[jax.ShapeDtypeStruct((s, hq * dh), bf16), jax.ShapeDtypeStruct((hq, s, s), bf16),
                   jax.ShapeDtypeStruct((hq, s, 1), f32)],
        compiler_params=_cparams(dimension_semantics=("parallel", "arbitrary")),
    )(q, k, v1)


def _attn_bwd(p, do, o, q, k, v, linv):
    hq, s, _ = p.shape
    dh = HEAD_DIM
    tq = _pick(s, (256, 128))

    def body(p_ref, do_ref, o_ref, q_ref, k_ref, v_ref, linv_ref, dq_ref, dkt_ref, dvt_ref):
        @pl.when(pl.program_id(1) == 0)
        def _():
            dkt_ref[...] = jnp.zeros_like(dkt_ref)
            dvt_ref[...] = jnp.zeros_like(dvt_ref)

        for j in range(HP):
            sl = slice(j * dh, (j + 1) * dh)
            pp, doh, li = p_ref[j], do_ref[:, sl], linv_ref[j]
            do32 = doh.astype(f32)
            d = jnp.sum(do32 * o_ref[:, sl].astype(f32), axis=-1, keepdims=True)
            dp = lax.dot_general(doh, v_ref[0], _DIMS["nt"], preferred_element_type=f32)
            ds = (pp.astype(f32) * ((dp - d) * li)).astype(bf16)
            dq_ref[:, sl] = jnp.dot(ds, k_ref[0], preferred_element_type=f32) * LN2
            dvt_ref[j] += lax.dot_general((do32 * li).astype(bf16), pp, _DIMS["tn"], preferred_element_type=f32)
            dkt_ref[j] += lax.dot_general(q_ref[:, sl], ds, _DIMS["tn"], preferred_element_type=f32)

    def row():
        return pl.BlockSpec((tq, HP * dh), lambda h, i: (i, h))

    return pl.pallas_call(
        body, name="attn_bwd", grid=(hq // HP, s // tq),
        in_specs=[pl.BlockSpec((HP, tq, s), lambda h, i: (h, i, 0)), row(), row(), row(),
                  pl.BlockSpec((1, s, dh), lambda h, i: (h * HP // REP, 0, 0)),
                  pl.BlockSpec((1, s, dh), lambda h, i: (h * HP // REP, 0, 0)),
                  pl.BlockSpec((HP, tq, 1), lambda h, i: (h, i, 0))],
        out_specs=[row(), pl.BlockSpec((HP, dh, s), lambda h, i: (h, 0, 0)),
                   pl.BlockSpec((HP, dh, s), lambda h, i: (h, 0, 0))],
        out_shape=[jax.ShapeDtypeStruct((s, hq * dh), f32), jax.ShapeDtypeStruct((hq, dh, s), f32),
                   jax.ShapeDtypeStruct((hq, dh, s), f32)],
        compiler_params=_cparams(dimension_semantics=("parallel", "arbitrary")),
    )(p, do, o, q, k, v, linv)


@jax.custom_vjp
def attention(q, k, v):
    return _attn_fwd(q, k, v)[0]


def _attention_fwd(q, k, v):
    o, p, linv = _attn_fwd(q, k, v)
    return o, (q, k, v, o, p, linv)


def _attention_bwd(res, do):
    q, k, v, o, p, linv = res
    s = q.shape[0]
    dq, dkt, dvt = _attn_bwd(p, do.astype(bf16), o, q, k, v, linv)

    def per_kv_head(t):
        return jnp.swapaxes(t.reshape(N_KV_HEADS, REP, HEAD_DIM, s).sum(axis=1), 1, 2)

    return dq.astype(q.dtype), (per_kv_head(dkt) * LN2).astype(k.dtype), per_kv_head(dvt).astype(v.dtype)


attention.defvjp(_attention_fwd, _attention_bwd)


HPG = N_SSD_HEADS // N_SSD_GROUPS
GW = HPG * SSD_HEAD_DIM
NEG = -1e30
SPLIT_ROWS = 32


def _ssd_consts():
    k = np.arange(SPLIT_ROWS)[:, None]
    live = k < 3 * HPG
    sel_chunk = ((k % HPG) == (np.arange(HPG * CHUNK)[None, :] // CHUNK)) & live
    sel_head = ((k % HPG) == (np.arange(GW)[None, :] // SSD_HEAD_DIM)) & live
    return jnp.asarray(sel_chunk, bf16), jnp.asarray(sel_head, bf16)


def _split3(x):
    hi = x.astype(bf16).astype(f32)
    r1 = x - hi
    mid = r1.astype(bf16).astype(f32)
    lo = (r1 - mid).astype(bf16).astype(f32)
    return jnp.concatenate([hi, mid, lo, jnp.zeros_like(hi)], axis=0).astype(bf16)


def _tn(a, b):
    return lax.dot_general(a, b, _DIMS["tn"], preferred_element_type=f32)


def _nt(a, b):
    return lax.dot_general(a, b, _DIMS["nt"], preferred_element_type=f32)


def _nn(a, b):
    return jnp.dot(a, b, preferred_element_type=f32)


def _head_sum(sel8, x):
    hi = x.astype(bf16)
    lo = (x - hi.astype(f32)).astype(bf16)
    return _nt(sel8, hi) + _nt(sel8, lo)


def _ssd_masks(reverse):
    r = lax.broadcasted_iota(jnp.int32, (CHUNK, CHUNK), 0)
    c = lax.broadcasted_iota(jnp.int32, (CHUNK, CHUNK), 1)
    lower, upper = r >= c, r <= c
    return (upper, lower) if reverse else (lower, upper)


def _ssd_in_specs(cidx):
    return [pl.BlockSpec((CHUNK, D_INNER), lambda c: (cidx(c), 0)),
            pl.BlockSpec((CHUNK, GN), lambda c: (cidx(c), D_INNER // GN)),
            pl.BlockSpec((CHUNK, GN), lambda c: (cidx(c), D_INNER // GN + 1)),
            pl.BlockSpec((N_SSD_HEADS, CHUNK), lambda c: (0, cidx(c))),
            pl.BlockSpec((N_SSD_HEADS, 1), lambda c: (0, 0)),
            pl.BlockSpec((SPLIT_ROWS, HPG * CHUNK), lambda c: (0, 0)),
            pl.BlockSpec((SPLIT_ROWS, GW), lambda c: (0, 0))]


def _ssd_chunk_common(dtt_ref, a_ref, et_ref, mask_t):
    dtt = dtt_ref[...]
    et = jnp.dot(dtt * a_ref[...], mask_t.astype(f32), precision=HIGHEST, preferred_element_type=f32)
    et_ref[...] = et
    return dtt, et


def _ssd_group_common(g, dtt, et, selc_ref, selh_ref, xs_ref, b_ref, c_ref, last):
    gr = slice(g * HPG, (g + 1) * HPG)
    e3 = _split3(et[gr])
    col = _tn(e3, selc_ref[...])
    eb = _tn(e3, selh_ref[...])
    dtb = _tn(_split3(dtt[gr]), selh_ref[...])
    tbc = eb[last:last + 1, :]
    xs = xs_ref[:, g * GW:(g + 1) * GW]
    bg = b_ref[:, g * D_STATE:(g + 1) * D_STATE].astype(bf16)
    cg = c_ref[:, g * D_STATE:(g + 1) * D_STATE].astype(bf16)
    return col, eb, dtb, tbc, xs, bg, cg


def _ssd_fwd(xbc, dtt, a_col, reverse, y_prev=None, dexp=None):
    s = xbc.shape[0]
    nc = s // CHUNK
    cidx = (lambda c: nc - 1 - c) if reverse else (lambda c: c)
    last = 0 if reverse else CHUNK - 1
    selc, selh = _ssd_consts()
    final = y_prev is not None
    n_in = 9 if final else 7

    def body(*refs):
        xs_ref, b_ref, c_ref, dtt_ref, a_ref, selc_ref, selh_ref = refs[:7]
        y_ref, st_ref, ht_ref, et_ref = refs[n_in:]

        @pl.when(pl.program_id(0) == 0)
        def _():
            ht_ref[...] = jnp.zeros_like(ht_ref)

        mask, mask_t = _ssd_masks(reverse)
        dtt_v, et = _ssd_chunk_common(dtt_ref, a_ref, et_ref, mask_t)
        for g in range(N_SSD_GROUPS):
            col, eb, dtb, tbc, xs, bg, cg = _ssd_group_common(g, dtt_v, et, selc_ref, selh_ref, xs_ref, b_ref, c_ref,
                                                              last)
            xd = xs * dtb
            cb = _nt(cg, bg)
            ht = ht_ref[g]
            st_ref[0, g] = ht
            yoff = _nn(cg, ht.astype(bf16)) * jnp.exp(eb)
            for j in range(HPG):
                h = g * HPG + j
                hs = slice(j * SSD_HEAD_DIM, (j + 1) * SSD_HEAD_DIM)
                lam = jnp.exp(jnp.where(mask, col[:, j * CHUNK:(j + 1) * CHUNK] - et_ref[h:h + 1, :], NEG))
                yj = _nn((cb * lam).astype(bf16), xd[:, hs].astype(bf16)) + yoff[:, hs]
                cols = slice(g * GW + j * SSD_HEAD_DIM, g * GW + (j + 1) * SSD_HEAD_DIM)
                if final:
                    yj = yj + refs[7][:, cols] + xs[:, hs] * refs[8][:, cols]
                y_ref[:, cols] = yj
            ht_ref[g] = jnp.exp(tbc) * ht + _tn(bg, (xd * jnp.exp(tbc - eb)).astype(bf16))

    y_spec = pl.BlockSpec((CHUNK, D_INNER), lambda c: (cidx(c), 0))
    extra_specs = [y_spec, pl.BlockSpec((1, D_INNER), lambda c: (0, 0))] if final else []
    return pl.pallas_call(
        body, name="ssd_fwd_rev" if reverse else "ssd_fwd", grid=(nc,),
        in_specs=_ssd_in_specs(cidx) + extra_specs,
        out_specs=[y_spec, pl.BlockSpec((1, N_SSD_GROUPS, D_STATE, GW), lambda c: (cidx(c), 0, 0, 0))],
        out_shape=[jax.ShapeDtypeStruct((s, D_INNER), f32),
                   jax.ShapeDtypeStruct((nc, N_SSD_GROUPS, D_STATE, GW), f32)],
        scratch_shapes=[pltpu.VMEM((N_SSD_GROUPS, D_STATE, GW), f32), pltpu.VMEM((N_SSD_HEADS, CHUNK), f32)],
        compiler_params=_cparams(dimension_semantics=("arbitrary",)),
    )(xbc, xbc, xbc, dtt, a_col, selc, selh, *((y_prev, dexp) if final else ()))


def _ssd_bwd(xbc, dtt, a_col, states, dy, reverse, dxbc_prev=None, dexp=None):
    s = xbc.shape[0]
    nc = s // CHUNK
    cidx = (lambda c: c) if reverse else (lambda c: nc - 1 - c)
    last = 0 if reverse else CHUNK - 1
    selc, selh = _ssd_consts()
    final = dxbc_prev is not None
    n_in = 11 if final else 9
    n_out = 4 if final else 3

    def body(*refs):
        xs_ref, b_ref, c_ref, dtt_ref, a_ref, selc_ref, selh_ref, st_ref, dy_ref = refs[:9]
        dxbc_ref, ddtt_ref, da_ref = refs[n_in:n_in + 3]
        dh_ref, et_ref, det_ref, det2_ref, ddt_ref, q_ref = refs[n_in + n_out:]
        if final:
            prev_ref, dexp_ref, ddexp_ref = refs[9], refs[10], refs[n_in + 3]

        @pl.when(pl.program_id(0) == 0)
        def _():
            dh_ref[...] = jnp.zeros_like(dh_ref)
            da_ref[...] = jnp.zeros_like(da_ref)
            if final:
                ddexp_ref[...] = jnp.zeros_like(ddexp_ref)

        mask, mask_t = _ssd_masks(reverse)
        dtt_v, et = _ssd_chunk_common(dtt_ref, a_ref, et_ref, mask_t)
        sel8 = selh_ref[0:HPG, :]
        is_last = lax.broadcasted_iota(jnp.int32, (CHUNK, GW), 0) == last
        for g in range(N_SSD_GROUPS):
            col, eb, dtb, tbc, xs, bg, cg = _ssd_group_common(g, dtt_v, et, selc_ref, selh_ref, xs_ref, b_ref, c_ref,
                                                              last)
            xd = xs * dtb
            cb = _nt(cg, bg)
            cbt = _nt(bg, cg)
            exp_t = jnp.exp(tbc)
            dfac = jnp.exp(tbc - eb)
            ht = st_ref[0, g]
            dhn = dh_ref[g]
            ht16, dhn16 = ht.astype(bf16), dhn.astype(bf16)
            dy = dy_ref[:, g * GW:(g + 1) * GW]
            dye = dy * jnp.exp(eb)
            dye16 = dye.astype(bf16)
            dc = _nt(dye16, ht16)
            dh_ref[g] = exp_t * dhn + _tn(cg, dye16)
            deb = dye * _nn(cg, ht16)
            xdd = xd * dfac
            dxdd = _nn(bg, dhn16)
            db = _nt(xdd.astype(bf16), dhn16)
            dxd_state = dxdd * dfac
            ddf = dxdd * xdd
            dtbc = jnp.sum(ddf, axis=0, keepdims=True) + exp_t * jnp.sum(dhn * ht, axis=0, keepdims=True)
            deb = deb - ddf + jnp.where(is_last, dtbc, 0.0)
            dcb = jnp.zeros((CHUNK, CHUNK), f32)
            dcbt = jnp.zeros((CHUNK, CHUNK), f32)
            for j in range(HPG):
                h = g * HPG + j
                hs = slice(j * SSD_HEAD_DIM, (j + 1) * SSD_HEAD_DIM)
                colj = col[:, j * CHUNK:(j + 1) * CHUNK]
                row = et_ref[h:h + 1, :]
                lam = jnp.exp(jnp.where(mask, colj - row, NEG))
                lam_t = jnp.exp(jnp.where(mask_t, row - colj, NEG))
                xdj, dyj = xd[:, hs].astype(bf16), dy[:, hs].astype(bf16)
                t1 = _nt(dyj, xdj) * lam
                t2 = _nt(xdj, dyj) * lam_t
                dcb, dcbt = dcb + t1, dcbt + t2
                det_ref[h:h + 1, :] = -jnp.sum(t1 * cb - t2 * cbt, axis=0, keepdims=True)
                dxdj = _nn((cbt * lam_t).astype(bf16), dyj) + dxd_state[:, hs]
                cols = slice(g * GW + j * SSD_HEAD_DIM, g * GW + (j + 1) * SSD_HEAD_DIM)
                dxs = dxdj * dtb[:, hs]
                if final:
                    dxs = dxs + prev_ref[:, cols] + dy[:, hs] * dexp_ref[:, cols]
                dxbc_ref[:, cols] = dxs
                q_ref[:, hs] = dxdj * xs[:, hs]
            b_cols = slice(D_INNER + g * D_STATE, D_INNER + (g + 1) * D_STATE)
            c_cols = slice(D_INNER + GN + g * D_STATE, D_INNER + GN + (g + 1) * D_STATE)
            db = db + _nn(dcbt.astype(bf16), cg)
            dc = dc + _nn(dcb.astype(bf16), bg)
            if final:
                db, dc = db + prev_ref[:, b_cols], dc + prev_ref[:, c_cols]
                ddexp_ref[:, g * GW:(g + 1) * GW] += jnp.sum(dy * xs, axis=0, keepdims=True)
            dxbc_ref[:, b_cols] = db
            dxbc_ref[:, c_cols] = dc
            det2_ref[g * HPG:(g + 1) * HPG, :] = _head_sum(sel8, deb)
            ddt_ref[g * HPG:(g + 1) * HPG, :] = _head_sum(sel8, q_ref[...])
        dat = jnp.dot(det_ref[...] + det2_ref[...], mask.astype(f32), precision=HIGHEST, preferred_element_type=f32)
        ddtt_ref[...] = ddt_ref[...] + dat * a_ref[...]
        da_ref[...] += jnp.sum(dat * dtt_v, axis=1, keepdims=True)

    in_specs = _ssd_in_specs(cidx) + [
        pl.BlockSpec((1, N_SSD_GROUPS, D_STATE, GW), lambda c: (cidx(c), 0, 0, 0)),
        pl.BlockSpec((CHUNK, D_INNER), lambda c: (cidx(c), 0))]
    hl = pltpu.VMEM((N_SSD_HEADS, CHUNK), f32)
    dxbc_spec = pl.BlockSpec((CHUNK, CONV_DIM), lambda c: (cidx(c), 0))
    dexp_spec = pl.BlockSpec((1, D_INNER), lambda c: (0, 0))
    return pl.pallas_call(
        body, name="ssd_bwd_rev" if reverse else "ssd_bwd", grid=(nc,),
        in_specs=in_specs + ([dxbc_spec, dexp_spec] if final else []),
        out_specs=[dxbc_spec, pl.BlockSpec((N_SSD_HEADS, CHUNK), lambda c: (0, cidx(c))),
                   pl.BlockSpec((N_SSD_HEADS, 1), lambda c: (0, 0))] + ([dexp_spec] if final else []),
        out_shape=[jax.ShapeDtypeStruct((s, CONV_DIM), f32), jax.ShapeDtypeStruct((N_SSD_HEADS, s), f32),
                   jax.ShapeDtypeStruct((N_SSD_HEADS, 1), f32)]
        + ([jax.ShapeDtypeStruct((1, D_INNER), f32)] if final else []),
        scratch_shapes=[pltpu.VMEM((N_SSD_GROUPS, D_STATE, GW), f32), hl, hl, hl, hl, pltpu.VMEM((CHUNK, GW), f32)],
        compiler_params=_cparams(dimension_semantics=("arbitrary",)),
    )(xbc, xbc, xbc, dtt, a_col, selc, selh, states, dy, *((dxbc_prev, dexp) if final else ()))


@jax.custom_vjp
def ssd_bidir(xbc, dtt, a_col, dexp):
    y_f, _ = _ssd_fwd(xbc, dtt[:N_SSD_HEADS], a_col[:N_SSD_HEADS], False)
    return _ssd_fwd(xbc, dtt[N_SSD_HEADS:], a_col[N_SSD_HEADS:], True, y_prev=y_f, dexp=dexp)[0]


def _ssd_bidir_fwd(xbc, dtt, a_col, dexp):
    y_f, st_f = _ssd_fwd(xbc, dtt[:N_SSD_HEADS], a_col[:N_SSD_HEADS], False)
    y, st_b = _ssd_fwd(xbc, dtt[N_SSD_HEADS:], a_col[N_SSD_HEADS:], True, y_prev=y_f, dexp=dexp)
    return y, (xbc, dtt, a_col, dexp, st_f, st_b)


def _ssd_bidir_bwd(res, dy):
    xbc, dtt, a_col, dexp, st_f, st_b = res
    dxbc_f, ddtt_f, da_f = _ssd_bwd(xbc, dtt[:N_SSD_HEADS], a_col[:N_SSD_HEADS], st_f, dy, False)
    dxbc, ddtt_b, da_b, ddexp = _ssd_bwd(xbc, dtt[N_SSD_HEADS:], a_col[N_SSD_HEADS:], st_b, dy, True,
                                         dxbc_prev=dxbc_f, dexp=dexp)
    return dxbc, jnp.concatenate([ddtt_f, ddtt_b], axis=0), jnp.concatenate([da_f, da_b], axis=0), ddexp


ssd_bidir.defvjp(_ssd_bidir_fwd, _ssd_bidir_bwd)


W_NAMES = PROJ_NAMES + ("attn_out", "ssd_out", "o", "mlp1", "mlp2")


def _rope_tables(s):
    rows = s // GRID_W
    pos_row = np.repeat(np.arange(rows), GRID_W).astype(np.float32)
    pos_col = np.tile(np.arange(GRID_W), rows).astype(np.float32)
    axis_dim = HEAD_DIM // 2
    inv_freq = np.float32(ROPE_THETA) ** (-np.arange(0, axis_dim, 2, dtype=np.float32) / np.float32(axis_dim))
    ang_r = pos_row[:, None] * inv_freq[None, :].astype(np.float32)
    ang_c = pos_col[:, None] * inv_freq[None, :].astype(np.float32)
    cos = np.concatenate([np.cos(ang_r), np.cos(ang_r), np.cos(ang_c), np.cos(ang_c)] * 2, axis=-1)
    sin = np.concatenate([np.sin(ang_r), np.sin(ang_r), np.sin(ang_c), np.sin(ang_c)] * 2, axis=-1)
    return jnp.asarray(cos, f32), jnp.asarray(sin, f32)


def _rope_perm():
    p = np.zeros((HEAD_DIM, HEAD_DIM), np.float32)
    for j in range(HEAD_DIM):
        if (j % 32) < 16:
            p[j + 16, j] = -1.0
        else:
            p[j - 16, j] = 1.0
    return p


def local_loss(x, mod, small, recv_in_like, recv_late_like, wfull, late_shard, target):
    s = x.shape[0]
    lin = {n: make_linear("lin_" + n) for n in LATE if not n.startswith("mlp")}
    wfull, wgrads = dict(wfull), {}
    shift1, scale1, gate1, shift2, scale2, gate2 = [mod[i] for i in range(6)]

    norm_mod = make_rowwise("norm_mod", _fn_norm_mod, [(D_MODEL, bf16)])
    (h,), _ = norm_mod((x,), (small["norm1_w"], scale1, shift1), (), ())

    proj = dict(zip(PROJ_NAMES, in_proj(h, tuple(wfull[n] for n in PROJ_NAMES), recv_in_like)))

    cos, sin = _rope_tables(s)

    def heads(t, nh):
        return t.reshape(s, nh, HEAD_DIM).transpose(1, 0, 2)

    qr = make_head_rope("q_norm_rope", N_Q_HEADS, Q_SCALE, False)(proj["q"], small["q_norm_w"], cos, sin)
    kr = make_head_rope("k_norm_rope", N_KV_HEADS, 1.0, True)(proj["k"], small["k_norm_w"], cos, sin)
    vh = heads(proj["v"], N_KV_HEADS).astype(bf16)
    att = attention(qr, kr, vh)

    xbc, gathered, *carriers = conv_silu_comm(proj["xbc"], small["conv_w"], small["conv_b"], late_shard,
                                              recv_late_like)
    wfull.update(_split_late(gathered))
    wgrads.update(zip(LATE, carriers))
    ao = lin["attn_out"](att, wfull["attn_out"], wgrads["attn_out"])
    softplus = make_rowwise("dt_softplus", _fn_softplus, [(2 * N_SSD_HEADS, f32)])
    (dt,), _ = softplus((proj["dt"][:, :2 * N_SSD_HEADS],), (small["dt_bias"].reshape(1, 2 * N_SSD_HEADS),), (), ())
    a_neg = -jnp.exp(small["A_log"])
    dexp = jnp.repeat(small["ssd_D"].reshape(N_SSD_HEADS), SSD_HEAD_DIM).reshape(1, D_INNER)
    y = ssd_bidir(xbc, dt.T, a_neg.reshape(2 * N_SSD_HEADS, 1), dexp)
    ssd_gate = make_rowwise("ssd_gate", _fn_ssd_gate, [(D_INNER, bf16)], tm_pref=128)
    (ssd_out,), _ = ssd_gate((y, proj["z"]), (small["ssd_norm_w"],), (), ())
    so = lin["ssd_out"](ssd_out, wfull["ssd_out"], wgrads["ssd_out"])

    merge = make_rowwise("merge", _fn_merge, [(D_MODEL, bf16)])
    (merged,), _ = merge((ao, so, proj["ga"], proj["gs"]), (), (), ())
    mo = lin["o"](merged, wfull["o"], wgrads["o"])

    res_norm = make_rowwise("res_norm", _fn_res_norm, [(D_MODEL, f32), (D_MODEL, bf16)])
    (x1, h2), _ = res_norm((x, mo), (gate1, small["norm2_w"], scale2, shift2), (), ())
    ff = mlp(h2, wfull["mlp1"], wgrads["mlp1"], wfull["mlp2"], wgrads["mlp2"])
    loss_op = make_rowwise("loss", _fn_loss, [], [(1, 1)])
    _, (loss,) = loss_op((x1, ff), (gate2,), (), (target,))
    return loss[0, 0]


_BC1 = 1.0 - ADAM_B1 ** ADAM_STEP
_BC2 = 1.0 - ADAM_B2 ** ADAM_STEP


def _adamw(w, g, m, v):
    m = ADAM_B1 * m + (1.0 - ADAM_B1) * g
    v = ADAM_B2 * v + (1.0 - ADAM_B2) * (g * g)
    delta = -ADAM_LR * ((m / _BC1) / (jnp.sqrt(v / _BC2) + ADAM_EPS) + ADAM_WD * w)
    return delta, m, v


def _ada_fwd(c_all, w, b):
    n = w.shape[1]

    def body(c_ref, w_ref, b_ref, o_ref):
        o_ref[...] = jnp.dot(_silu(c_ref[...]), w_ref[...], precision=HIGHEST, preferred_element_type=f32) + b_ref[...]

    return pl.pallas_call(body, name="ada_fwd", out_shape=jax.ShapeDtypeStruct((N_DEV, n), f32),
                          compiler_params=_cparams())(c_all, w, b)


def _ada_bwd_adamw(c_all, dmod, w, m, v):
    d, n = w.shape
    tr = _pick(d, (256, 128))

    def body(c_ref, dm_ref, w_ref, m_ref, v_ref, g_ref, dl_ref, mo_ref, vo_ref):
        g = lax.dot_general(_silu(c_ref[...]), dm_ref[...], _DIMS["tn"], precision=HIGHEST,
                            preferred_element_type=f32)
        g_ref[...] = g
        dl_ref[...], mo_ref[...], vo_ref[...] = _adamw(w_ref[...], g, m_ref[...], v_ref[...])

    blk = pl.BlockSpec((tr, n), lambda i: (i, 0))
    return pl.pallas_call(
        body, name="ada_bwd_adamw", grid=(d // tr,),
        in_specs=[pl.BlockSpec((N_DEV, tr), lambda i: (0, i)), pl.BlockSpec((N_DEV, n), lambda i: (0, 0)), blk, blk, blk],
        out_specs=[blk] * 4, out_shape=[jax.ShapeDtypeStruct((d, n), f32)] * 4,
        compiler_params=_cparams(dimension_semantics=("parallel",)),
    )(c_all, dmod, w, m, v)


def _sum_over_mesh(g):
    def body(g_ref, o_ref):
        acc = g_ref[0]
        for d in range(1, N_DEV):
            acc = acc + g_ref[d]
        o_ref[...] = acc

    return pl.pallas_call(body, name="sum_small", out_shape=jax.ShapeDtypeStruct(g.shape[1:], f32),
                          compiler_params=_cparams())(g)


def _adamw_small(w, g, m, v):
    def body(w_ref, g_ref, m_ref, v_ref, dl_ref, mo_ref, vo_ref):
        dl_ref[...], mo_ref[...], vo_ref[...] = _adamw(w_ref[...], g_ref[...], m_ref[...], v_ref[...])

    return pl.pallas_call(body, name="adamw_small", out_shape=[jax.ShapeDtypeStruct(w.shape, f32)] * 3,
                          compiler_params=_cparams())(w, g, m, v)


def _sum_adamw(recv, w, m, v, name):
    _, r, c = recv.shape
    tr = _pick(r, (256, 128, 64, 16))

    def body(g_ref, w_ref, m_ref, v_ref, go_ref, dl_ref, mo_ref, vo_ref):
        g = g_ref[0].astype(f32)
        for d in range(1, N_DEV):
            g = g + g_ref[d].astype(f32)
        go_ref[...] = g
        dl_ref[...], mo_ref[...], vo_ref[...] = _adamw(w_ref[...], g, m_ref[...], v_ref[...])

    blk = pl.BlockSpec((tr, c), lambda i: (i, 0))
    return pl.pallas_call(
        body, name=name, grid=(r // tr,),
        in_specs=[pl.BlockSpec((N_DEV, tr, c), lambda i: (0, i, 0)), blk, blk, blk],
        out_specs=[blk] * 4, out_shape=[jax.ShapeDtypeStruct((r, c), f32)] * 4,
        compiler_params=_cparams(dimension_semantics=("parallel",)),
    )(recv, w, m, v)


def _pack_small(arrs):
    parts = []
    for a in arrs:
        flat = a.reshape(-1).astype(f32)
        parts.append(jnp.pad(flat, (0, (-flat.shape[0]) % LANE)))
    flat = jnp.concatenate(parts)
    flat = jnp.pad(flat, (0, (-flat.shape[0]) % (8 * LANE)))
    return flat.reshape(-1, LANE)


def _unpack_small(packed, shapes):
    flat = packed.reshape(-1)
    out, off = [], 0
    for shp in shapes:
        n = int(np.prod(shp))
        out.append(flat[off:off + n].reshape(shp))
        off += n + (-n) % LANE
    return out


BIG = ("w_attn_out", "w_ssd_out", "w_o", "w_mlp1", "w_mlp2")
BIG_ROWS = (N_Q_HEADS * HEAD_DIM // N_DEV, D_INNER // N_DEV, D_MODEL // N_DEV,
            D_MODEL * (D_FF // N_DEV) // PACK_COLS, D_FF // N_DEV)
N_IN_SHARD = D_IN_PROJ // N_DEV
assert sum(BIG_ROWS) % 16 == 0


def _pack_big(shards, dtype):
    return jnp.concatenate([s.astype(dtype).reshape(-1, PACK_COLS) for s in shards], axis=0)


def _unpack_big(packed, shapes):
    out, off = [], 0
    for rows, shp in zip(BIG_ROWS, shapes):
        out.append(packed[off:off + rows].reshape(shp))
        off += rows
    return out


LATE = ("attn_out", "ssd_out", "o", "mlp1", "mlp2")
LATE_SHAPES = ((N_Q_HEADS * HEAD_DIM, D_MODEL), (D_INNER, D_MODEL), (D_MODEL, D_MODEL), (D_MODEL, D_FF),
               (D_FF, D_MODEL))


def _split_w_in(g_in):
    w_in = g_in.transpose(1, 0, 2).reshape(D_MODEL, D_IN_PROJ)
    w = {}
    off = 0
    for name, size in zip(PROJ_NAMES, PROJ_SIZES):
        w[name] = w_in[:, off:off + size]
        off += size
    w["dt"] = jnp.pad(w["dt"], ((0, 0), (0, DT_PAD - 2 * N_SSD_HEADS)))
    return w


def _split_late(g):
    offs = np.cumsum((0,) + BIG_ROWS)
    sl = [g[:, offs[i]:offs[i + 1]] for i in range(len(BIG))]
    return {"attn_out": sl[0].reshape(LATE_SHAPES[0]), "ssd_out": sl[1].reshape(LATE_SHAPES[1]),
            "o": sl[2].reshape(LATE_SHAPES[2]),
            "mlp1": sl[3].reshape(N_DEV, D_MODEL, D_FF // N_DEV).transpose(1, 0, 2).reshape(LATE_SHAPES[3]),
            "mlp2": sl[4].reshape(LATE_SHAPES[4])}


def _pack_in_grads(gw):
    gw = {n: g.astype(bf16) for n, g in gw.items()}
    gw["dt"] = gw["dt"][:, :2 * N_SSD_HEADS]
    g_in = jnp.concatenate([gw[n] for n in PROJ_NAMES], axis=1)
    return g_in.reshape(D_MODEL, N_DEV, N_IN_SHARD).transpose(1, 0, 2)


def _pack_late_grads(gw):
    gw = {n: g.astype(bf16) for n, g in gw.items()}
    parts = [
        gw["attn_out"].reshape(N_DEV, -1, PACK_COLS),
        gw["ssd_out"].reshape(N_DEV, -1, PACK_COLS),
        gw["o"].reshape(N_DEV, -1, PACK_COLS),
        gw["mlp1"].reshape(D_MODEL, N_DEV, D_FF // N_DEV).transpose(1, 0, 2).reshape(N_DEV, -1, PACK_COLS),
        gw["mlp2"].reshape(N_DEV, -1, PACK_COLS),
    ]
    return jnp.concatenate(parts, axis=1)


SMALL = ("norm1_w", "norm2_w", "q_norm_w", "k_norm_w", "conv_w", "conv_b", "A_log", "dt_bias", "ssd_D", "ssd_norm_w")


def kernel(x, c, w_ada, b_ada, norm1_w, norm2_w, w_in, q_norm_w, k_norm_w, conv_w, conv_b, A_log, dt_bias, ssd_D, ssd_norm_w, w_attn_out, w_ssd_out, w_o, w_mlp1, w_mlp2, loss_target, m_w_ada, m_b_ada, m_norm1_w, m_norm2_w, m_w_in, m_q_norm_w, m_k_norm_w, m_conv_w, m_conv_b, m_A_log, m_dt_bias, m_ssd_D, m_ssd_norm_w, m_w_attn_out, m_w_ssd_out, m_w_o, m_w_mlp1, m_w_mlp2, v_w_ada, v_b_ada, v_norm1_w, v_norm2_w, v_w_in, v_q_norm_w, v_k_norm_w, v_conv_w, v_conv_b, v_A_log, v_dt_bias, v_ssd_D, v_ssd_norm_w, v_w_attn_out, v_w_ssd_out, v_w_o, v_w_mlp1, v_w_mlp2):
    args = dict(locals())
    me = _my_index()
    n_ada = 6 * D_MODEL // N_DEV
    n_cw = CONV_DIM // N_DEV

    blk = jnp.zeros((8, D_MODEL), f32)
    blk = blk.at[0:1, :].set(c)
    blk = blk.at[1:1 + D_CONV, :n_cw].set(conv_w[0])
    g0 = _all_gather(blk, "gather_c_convw", in_vmem=True)
    c_all = g0[:, 0, :]
    conv_w_full = g0[:, 1:1 + D_CONV, :n_cw].transpose(1, 0, 2).reshape(D_CONV, CONV_DIM)

    b_shard = lax.dynamic_slice(b_ada, (0, me * n_ada), (1, n_ada))
    mod_cols = _ada_fwd(c_all, w_ada[0], b_shard)
    g1 = _all_gather(mod_cols, "gather_mod", in_vmem=True)
    mod_mine = lax.dynamic_index_in_dim(g1, me, axis=1, keepdims=False)
    mod = mod_mine.reshape(6, 1, D_MODEL)

    big_shapes = [args[n].shape[1:] for n in BIG]
    late_shard = _pack_big([args[n][0] for n in BIG], bf16)
    wfull = _split_w_in(_all_gather(w_in[0].astype(bf16), "gather_w_in", in_vmem=False))
    recv_in_like = jnp.zeros((N_DEV,) + w_in.shape[1:], bf16)
    recv_late_like = jnp.zeros((N_DEV,) + late_shard.shape, bf16)

    small = {"norm1_w": norm1_w, "norm2_w": norm2_w, "q_norm_w": q_norm_w, "k_norm_w": k_norm_w,
             "conv_w": conv_w_full, "conv_b": conv_b, "A_log": A_log[0], "dt_bias": dt_bias[0], "ssd_D": ssd_D,
             "ssd_norm_w": ssd_norm_w}

    loss, (gx, gmod, gsmall, recv_in, recv_late) = jax.value_and_grad(local_loss, argnums=(0, 1, 2, 3, 4))(
        x[0], mod, small, recv_in_like, recv_late_like, wfull, late_shard, loss_target[0])

    small_list = [gmod, gsmall["norm1_w"], gsmall["norm2_w"], gsmall["q_norm_w"], gsmall["k_norm_w"], gsmall["conv_w"],
                  gsmall["conv_b"], gsmall["A_log"], gsmall["dt_bias"], gsmall["ssd_D"], gsmall["ssd_norm_w"],
                  loss.reshape(1)]
    small_shapes = [a.shape for a in small_list]
    g2 = _all_gather(_pack_small(small_list), "gather_small_grads", in_vmem=True)
    summed = _unpack_small(_sum_over_mesh(g2), small_shapes)
    loss_total = summed[-1][0]
    g_b_ada = summed[0].reshape(1, 6 * D_MODEL)
    g_small = dict(zip(SMALL, summed[1:-1]))
    g_conv_w = lax.dynamic_slice(g_small["conv_w"], (0, me * n_cw), (D_CONV, n_cw))

    dmod_all = g2[:, :6 * D_MODEL // LANE, :].reshape(N_DEV, 6 * D_MODEL)
    dmod_shard = lax.dynamic_slice(dmod_all, (0, me * n_ada), (N_DEV, n_ada))
    ada = _ada_bwd_adamw(c_all, dmod_shard, w_ada[0], m_w_ada[0], v_w_ada[0])

    small_grads = {"b_ada": g_b_ada, "norm1_w": g_small["norm1_w"], "norm2_w": g_small["norm2_w"],
                   "q_norm_w": g_small["q_norm_w"], "k_norm_w": g_small["k_norm_w"], "conv_w": g_conv_w[None],
                   "conv_b": g_small["conv_b"], "A_log": g_small["A_log"][None], "dt_bias": g_small["dt_bias"][None],
                   "ssd_D": g_small["ssd_D"], "ssd_norm_w": g_small["ssd_norm_w"]}
    sm_names = list(small_grads)
    sm_shapes = [args[n].shape for n in sm_names]
    sm = _adamw_small(_pack_small([args[n] for n in sm_names]), _pack_small([small_grads[n] for n in sm_names]),
                      _pack_small([args["m_" + n] for n in sm_names]), _pack_small([args["v_" + n] for n in sm_names]))
    sm_delta, sm_m, sm_v = [dict(zip(sm_names, _unpack_small(t, sm_shapes))) for t in sm]
    small_grads = {n: small_grads[n].reshape(args[n].shape) for n in sm_names}

    w_in_out = _sum_adamw(recv_in, w_in[0], m_w_in[0], v_w_in[0], "sum_adamw_w_in")
    big = _sum_adamw(recv_late, _pack_big([args[n][0] for n in BIG], f32),
                     _pack_big([args["m_" + n][0] for n in BIG], f32),
                     _pack_big([args["v_" + n][0] for n in BIG], f32), "sum_adamw")
    big_g, big_delta, big_m, big_v = [dict(zip(BIG, [t[None] for t in _unpack_big(p, big_shapes)])) for p in big]
    big_g["w_in"], big_delta["w_in"], big_m["w_in"], big_v["w_in"] = [t[None] for t in w_in_out]

    names = ("w_ada", "b_ada", "norm1_w", "norm2_w", "w_in", "q_norm_w", "k_norm_w", "conv_w", "conv_b", "A_log",
             "dt_bias", "ssd_D", "ssd_norm_w", "w_attn_out", "w_ssd_out", "w_o", "w_mlp1", "w_mlp2")
    grads, deltas, new_m, new_v = {}, {}, {}, {}
    for n in names:
        if n == "w_ada":
            grads[n], deltas[n], new_m[n], new_v[n] = [t[None] for t in ada]
        elif n in big_g:
            grads[n], deltas[n], new_m[n], new_v[n] = big_g[n], big_delta[n], big_m[n], big_v[n]
        else:
            grads[n], deltas[n], new_m[n], new_v[n] = small_grads[n], sm_delta[n], sm_m[n], sm_v[n]
    return (loss_total, gx[None], *[grads[n] for n in names], *[deltas[n] for n in names],
            *[new_m[n] for n in names], *[new_v[n] for n in names])
```

```python
import functools
import math

import jax
import jax.numpy as jnp
import numpy as np
from jax import lax
from jax.experimental import pallas as pl
from jax.experimental.pallas import tpu as pltpu

f32 = jnp.float32
bf16 = jnp.bfloat16
HIGHEST = lax.Precision.HIGHEST
MESH = pl.DeviceIdType.MESH

N_DEV = 8
D_MODEL = 1024
GRID_W = 64
N_Q_HEADS = 16
N_KV_HEADS = 4
HEAD_DIM = 64
ROPE_THETA = 10000.0
D_INNER = 2048
SSD_HEAD_DIM = 64
N_SSD_HEADS = 32
N_SSD_GROUPS = 4
D_STATE = 128
D_CONV = 5
CHUNK = 128
D_FF = 4096
EPS = 1e-6
CONV_DIM = D_INNER + 2 * N_SSD_GROUPS * D_STATE
GN = N_SSD_GROUPS * D_STATE
PROJ_NAMES = ("q", "k", "v", "xbc", "z", "dt", "ga", "gs")
PROJ_SIZES = (N_Q_HEADS * HEAD_DIM, N_KV_HEADS * HEAD_DIM, N_KV_HEADS * HEAD_DIM, CONV_DIM, D_INNER,
              2 * N_SSD_HEADS, D_MODEL, D_MODEL)
D_IN_PROJ = sum(PROJ_SIZES)
PROJ_DTYPES = (jnp.bfloat16, jnp.bfloat16, jnp.bfloat16, jnp.float32, jnp.bfloat16, jnp.float32, jnp.bfloat16,
               jnp.bfloat16)
DT_PAD = 128

ADAM_LR, ADAM_B1, ADAM_B2, ADAM_EPS, ADAM_WD, ADAM_STEP = 0.001, 0.9, 0.999, 1e-08, 0.01, 10

V7X_VMEM_LIMIT = 56 * 1024 * 1024
LANE = 128
PACK_COLS = 1024


def _cparams(**kw):
    return pltpu.CompilerParams(vmem_limit_bytes=V7X_VMEM_LIMIT, **kw)


def _pick(dim, prefs):
    for p in prefs:
        if dim % p == 0:
            return p
    return dim


def _my_index():
    return 4 * lax.axis_index("x") + 2 * lax.axis_index("y") + lax.axis_index("c")


COMM_SEMS = [pltpu.SemaphoreType.DMA((7,)), pltpu.SemaphoreType.DMA((7,)), pltpu.SemaphoreType.DMA]


def _gather_phases(x_ref, out_ref, send_sems, recv_sems, local_sem):
    x, y, cc = lax.axis_index("x"), lax.axis_index("y"), lax.axis_index("c")
    me, sibling = (x, y, cc), (x, y, 1 - cc)
    chips = [(1 - x, y), (x, 1 - y), (1 - x, 1 - y)]

    def slot(px, py, pc):
        return out_ref.at[4 * px + 2 * py + pc]

    def copy(k, blk, to, src=None):
        return pltpu.make_async_remote_copy(
            src_ref=slot(*blk) if src is None else src, dst_ref=slot(*blk),
            send_sem=send_sems.at[k], recv_sem=recv_sems.at[k], device_id=to, device_id_type=MESH)

    mine = pltpu.make_async_copy(x_ref, slot(*me), local_sem)
    first = [copy(0, me, sibling, src=x_ref)]
    first += [copy(1 + j, me, (*chip, cc), src=x_ref) for j, chip in enumerate(chips)]
    passed = [copy(4 + j, (*chip, cc), sibling) for j, chip in enumerate(chips)]

    def start():
        mine.start()
        for cp in first:
            cp.start()

    def finish():
        for j, chip in enumerate(chips):
            copy(1 + j, (*chip, cc), me).wait_recv()
            passed[j].start()
        copy(0, sibling, me).wait_recv()
        for j, chip in enumerate(chips):
            copy(4 + j, (*chip, 1 - cc), me).wait_recv()
        for cp in first + passed:
            cp.wait_send()
        mine.wait()

    return start, finish


def _scatter_phases(g_ref, out_ref, send_sems, recv_sems, local_sem):
    x, y, cc = lax.axis_index("x"), lax.axis_index("y"), lax.axis_index("c")
    me = 4 * x + 2 * y + cc
    mine = pltpu.make_async_copy(g_ref.at[me], out_ref.at[me], local_sem)

    def copy(k):
        fx, fy, fc = (k >> 2) & 1, (k >> 1) & 1, k & 1
        px = x + fx - 2 * x * fx
        py = y + fy - 2 * y * fy
        pc = cc + fc - 2 * cc * fc
        peer = 4 * px + 2 * py + pc
        send = pltpu.make_async_remote_copy(
            src_ref=g_ref.at[peer], dst_ref=out_ref.at[me],
            send_sem=send_sems.at[k - 1], recv_sem=recv_sems.at[k - 1],
            device_id=(px, py, pc), device_id_type=MESH)
        recv = pltpu.make_async_remote_copy(
            src_ref=g_ref.at[peer], dst_ref=out_ref.at[peer],
            send_sem=send_sems.at[k - 1], recv_sem=recv_sems.at[k - 1],
            device_id=(px, py, pc), device_id_type=MESH)
        return send, recv

    pairs = [copy(k) for k in range(1, N_DEV)]

    def start():
        mine.start()
        for send, _ in pairs:
            send.start()

    def finish():
        for _, recv in pairs:
            recv.wait_recv()
        for send, _ in pairs:
            send.wait_send()
        mine.wait()

    return start, finish


def _all_gather(block, name, in_vmem):
    r, c = block.shape

    def body(x_ref, out_ref, send_sems, recv_sems, local_sem):
        start, finish = _gather_phases(x_ref, out_ref, send_sems, recv_sems, local_sem)
        start()
        finish()

    space = pltpu.VMEM if in_vmem else pl.ANY
    return pl.pallas_call(
        body, name=name,
        out_shape=jax.ShapeDtypeStruct((N_DEV, r, c), block.dtype),
        in_specs=[pl.BlockSpec(memory_space=space)],
        out_specs=pl.BlockSpec(memory_space=space),
        scratch_shapes=[pltpu.SemaphoreType.DMA((7,)), pltpu.SemaphoreType.DMA((7,)), pltpu.SemaphoreType.DMA],
    )(block)


def _scatter_blocks(g, name):
    _, r, c = g.shape

    def body(g_ref, out_ref, send_sems, recv_sems, local_sem):
        start, finish = _scatter_phases(g_ref, out_ref, send_sems, recv_sems, local_sem)
        start()
        finish()

    return pl.pallas_call(
        body, name=name,
        out_shape=jax.ShapeDtypeStruct(g.shape, g.dtype),
        in_specs=[pl.BlockSpec(memory_space=pl.ANY)],
        out_specs=pl.BlockSpec(memory_space=pl.ANY),
        scratch_shapes=[pltpu.SemaphoreType.DMA((7,)), pltpu.SemaphoreType.DMA((7,)), pltpu.SemaphoreType.DMA],
    )(g)


_DIMS = {"nn": (((1,), (0,)), ((), ())), "nt": (((1,), (1,)), ((), ())), "tn": (((0,), (0,)), ((), ()))}


def _matmul(a, b, mode, out_dtype, name, epilogue=None, side=None):
    if mode == "nn":
        (m, k), (_, n) = a.shape, b.shape
    elif mode == "nt":
        (m, k), (n, _) = a.shape, b.shape
    else:
        (k, m), (_, n) = a.shape, b.shape
    tm = _pick(m, (1024, 512, 256, 128))
    if mode == "tn":
        tn = _pick(n, (1536, 1024, 512, 256, 128))
        tk = _pick(k, (1024, 512, 256, 128))
    else:
        tn = _pick(n, (1024, 512, 384, 256, 128))
        tk = _pick(k, (1024, 512, 256, 128))
    nk = k // tk
    dims = _DIMS[mode]
    n_in = 3 if epilogue == "drelu2" else 2
    n_out = 2 if epilogue == "relu2" else 1

    def body(*refs):
        a_ref, b_ref = refs[:2]
        outs, acc_ref = refs[n_in:n_in + n_out], refs[n_in + n_out]
        kk = pl.program_id(2)
        part = lax.dot_general(a_ref[...].astype(bf16), b_ref[...].astype(bf16), dims, preferred_element_type=f32)

        def finish(acc):
            if epilogue == "relu2":
                r = jnp.maximum(acc, 0.0)
                outs[0][...] = acc.astype(out_dtype)
                outs[1][...] = (r * r).astype(out_dtype)
            elif epilogue == "drelu2":
                outs[0][...] = (acc * (2.0 * jnp.maximum(refs[2][...].astype(f32), 0.0))).astype(out_dtype)
            else:
                outs[0][...] = acc.astype(out_dtype)

        if nk == 1:
            finish(part)
        else:
            @pl.when(kk == 0)
            def _():
                acc_ref[...] = part

            @pl.when(kk > 0)
            def _():
                acc_ref[...] += part

            @pl.when(kk == nk - 1)
            def _():
                finish(acc_ref[...])

    if mode == "tn":
        a_spec = pl.BlockSpec((tk, tm), lambda i, j, kk: (kk, i))
    else:
        a_spec = pl.BlockSpec((tm, tk), lambda i, j, kk: (i, kk))
    if mode == "nt":
        b_spec = pl.BlockSpec((tn, tk), lambda i, j, kk: (j, kk))
    else:
        b_spec = pl.BlockSpec((tk, tn), lambda i, j, kk: (kk, j))
    o_spec = pl.BlockSpec((tm, tn), lambda i, j, kk: (i, j))
    o_shape = jax.ShapeDtypeStruct((m, n), out_dtype)
    res = pl.pallas_call(
        body, name=name, grid=(m // tm, n // tn, nk),
        in_specs=[a_spec, b_spec] + ([o_spec] if epilogue == "drelu2" else []),
        out_specs=[o_spec] * n_out, out_shape=[o_shape] * n_out,
        scratch_shapes=[pltpu.VMEM((tm, tn), f32)],
        compiler_params=_cparams(dimension_semantics=("parallel", "parallel", "arbitrary")),
    )(*((a, b, side) if epilogue == "drelu2" else (a, b)))
    return res if n_out == 2 else res[0]


@jax.custom_vjp
def mlp(h, w1, w1grad, w2, w2grad):
    _, r = _matmul(h, w1, "nn", bf16, "mlp1_fwd", epilogue="relu2")
    return _matmul(r, w2, "nn", f32, "mlp2_fwd")


def _mlp_fwd(h, w1, w1grad, w2, w2grad):
    u, r = _matmul(h, w1, "nn", bf16, "mlp1_fwd", epilogue="relu2")
    return _matmul(r, w2, "nn", f32, "mlp2_fwd"), (h, w1, w2, u, r)


def _mlp_bwd(res, dy):
    h, w1, w2, u, r = res
    du = _matmul(dy, w2, "nt", bf16, "mlp2_dgrad", epilogue="drelu2", side=u)
    dw2 = _matmul(r, dy, "tn", f32, "mlp2_wgrad")
    dh = _matmul(du, w1, "nt", h.dtype, "mlp1_dgrad")
    dw1 = _matmul(h, du, "tn", f32, "mlp1_wgrad")
    return dh, jnp.zeros_like(w1), dw1, jnp.zeros_like(w2), dw2


mlp.defvjp(_mlp_fwd, _mlp_bwd)


def make_linear(name):
    @jax.custom_vjp
    def linear(a, w, wgrad):
        return _matmul(a, w, "nn", f32, name + "_fwd")

    def fwd(a, w, wgrad):
        return linear(a, w, wgrad), (a, w)

    def bwd(res, dy):
        a, w = res
        da = _matmul(dy, w, "nt", a.dtype, name + "_dgrad")
        dw = _matmul(a, dy, "tn", f32, name + "_wgrad")
        return da, jnp.zeros_like(w), dw

    linear.defvjp(fwd, bwd)
    return linear


def _in_proj_dgrad(dys, ws, g):
    s, d = dys[0].shape[0], ws[0].shape[0]
    tm = _pick(s, (512, 256, 128))
    tks = [min(w.shape[1], 1024) for w in ws]
    steps = [w.shape[1] // tk for w, tk in zip(ws, tks)]
    starts = [sum(steps[:p]) for p in range(len(ws))]
    total = sum(steps)
    n_p, n_i = len(ws), s // tm
    assert steps[0] == 1

    def body(*refs):
        dy_refs, w_refs, g_ref = refs[:n_p], refs[n_p:2 * n_p], refs[2 * n_p]
        dh_ref, recv_ref, acc_ref, send_sems, recv_sems, local_sem = refs[2 * n_p + 1:]
        i, t = pl.program_id(0), pl.program_id(1)
        start, finish = _scatter_phases(g_ref, recv_ref, send_sems, recv_sems, local_sem)

        @pl.when((i == 0) & (t == 0))
        def _():
            start()

        for p in range(n_p):
            @pl.when((t >= starts[p]) & (t < starts[p] + steps[p]))
            def _(p=p):
                part = lax.dot_general(dy_refs[p][...].astype(bf16), w_refs[p][...], _DIMS["nt"],
                                       preferred_element_type=f32)
                if p == 0:
                    acc_ref[...] = part
                else:
                    acc_ref[...] += part

        @pl.when(t == total - 1)
        def _():
            dh_ref[...] = acc_ref[...].astype(dh_ref.dtype)

        @pl.when((i == n_i - 1) & (t == total - 1))
        def _():
            finish()

    def piece_map(p, rows):
        def index_map(i, t):
            blk = jnp.clip(t - starts[p], 0, steps[p] - 1)
            return (i, blk) if rows else (0, blk)

        return index_map

    hbm = pl.BlockSpec(memory_space=pl.ANY)
    in_specs = [pl.BlockSpec((tm, tks[p]), piece_map(p, True)) for p in range(n_p)]
    in_specs += [pl.BlockSpec((d, tks[p]), piece_map(p, False)) for p in range(n_p)]
    return pl.pallas_call(
        body, name="in_proj_dgrad", grid=(n_i, total), in_specs=in_specs + [hbm],
        out_specs=[pl.BlockSpec((tm, d), lambda i, t: (i, 0)), hbm],
        out_shape=[jax.ShapeDtypeStruct((s, d), bf16), jax.ShapeDtypeStruct(g.shape, g.dtype)],
        scratch_shapes=[pltpu.VMEM((tm, d), f32)] + COMM_SEMS,
        compiler_params=_cparams(dimension_semantics=("arbitrary", "arbitrary")),
    )(*dys, *ws, g)


@jax.custom_vjp
def in_proj(h, ws, recv_like):
    return tuple(_matmul(h, w, "nn", dt, "lin_" + n + "_fwd") for n, w, dt in zip(PROJ_NAMES, ws, PROJ_DTYPES))


def _in_proj_fwd(h, ws, recv_like):
    return in_proj(h, ws, recv_like), (h, ws)


def _in_proj_bwd(res, dys):
    h, ws = res
    dws = {n: _matmul(h, dy, "tn", f32, "lin_" + n + "_wgrad") for n, dy in zip(PROJ_NAMES, dys)}
    dh, recv = _in_proj_dgrad(dys, ws, _pack_in_grads(dws))
    return dh.astype(h.dtype), tuple(jnp.zeros_like(w) for w in ws), recv


in_proj.defvjp(_in_proj_fwd, _in_proj_bwd)


def make_rowwise(name, fn, row_out, sum_out=(), tm_pref=256):
    def specs(rows, gpars, cpars, consts, tm):
        s = [pl.BlockSpec((tm, r.shape[1]), lambda i: (i, 0)) for r in rows]
        s += [pl.BlockSpec(p.shape, lambda i: (0, 0)) for p in gpars]
        s += [pl.BlockSpec(p.shape, lambda i: (0, 0)) for p in cpars]
        for cst in consts:
            nb = cst.shape[0] // tm
            s.append(pl.BlockSpec((tm, cst.shape[1]), lambda i, nb=nb: (i % nb, 0)))
        return s

    def tile_rows(rows, consts):
        r = rows[0].shape[0]
        common = math.gcd(r, *[cst.shape[0] for cst in consts])
        tm = _pick(common, (tm_pref, 512, 256, 128, 64, 32, 16, 8))
        return r, tm

    def forward(rows, gpars, cpars, consts):
        r, tm = tile_rows(rows, consts)
        nr, ng, nc, nk = len(rows), len(gpars), len(cpars), len(consts)

        def body(*refs):
            ins = refs[:nr + ng + nc + nk]
            outs = refs[nr + ng + nc + nk:]
            rv = [t[...].astype(f32) for t in ins[:nr]]
            gv = [t[...].astype(f32) for t in ins[nr:nr + ng]]
            cv = [t[...] for t in ins[nr + ng:nr + ng + nc]]
            kv = [t[...].astype(f32) for t in ins[nr + ng + nc:]]
            ro, so = fn(rv, gv, cv, kv)
            for o_ref, val in zip(outs[:len(row_out)], ro):
                o_ref[...] = val.astype(o_ref.dtype)
            if sum_out:
                @pl.when(pl.program_id(0) == 0)
                def _():
                    for o_ref in outs[len(row_out):]:
                        o_ref[...] = jnp.zeros_like(o_ref)
                for o_ref, val in zip(outs[len(row_out):], so):
                    o_ref[...] += val

        out_specs = [pl.BlockSpec((tm, w), lambda i: (i, 0)) for w, _ in row_out]
        out_specs += [pl.BlockSpec(shp, lambda i: (0, 0)) for shp in sum_out]
        out_shape = [jax.ShapeDtypeStruct((r, w), dt) for w, dt in row_out]
        out_shape += [jax.ShapeDtypeStruct(shp, f32) for shp in sum_out]
        res = pl.pallas_call(
            body, name=name + "_fwd", grid=(r // tm,),
            in_specs=specs(rows, gpars, cpars, consts, tm), out_specs=out_specs, out_shape=out_shape,
            compiler_params=_cparams(dimension_semantics=("arbitrary",)),
        )(*rows, *gpars, *cpars, *consts)
        return tuple(res[:len(row_out)]), tuple(res[len(row_out):])

    def backward(rows, gpars, cpars, consts, d_ro, d_so):
        r, tm = tile_rows(rows, consts)
        nr, ng, nc, nk = len(rows), len(gpars), len(cpars), len(consts)
        n_in = nr + ng + nc + nk + len(row_out) + len(sum_out)

        def body(*refs):
            ins, outs = refs[:n_in], refs[n_in:]
            rv = [t[...].astype(f32) for t in ins[:nr]]
            gv = [t[...].astype(f32) for t in ins[nr:nr + ng]]
            cv = [t[...] for t in ins[nr + ng:nr + ng + nc]]
            kv = [t[...].astype(f32) for t in ins[nr + ng + nc:nr + ng + nc + nk]]
            o = nr + ng + nc + nk
            dro = [t[...].astype(f32) for t in ins[o:o + len(row_out)]]
            dso = [t[...] for t in ins[o + len(row_out):]]
            _, vjp = jax.vjp(lambda a, b: tuple(tuple(t) for t in fn(a, b, cv, kv)), rv, gv)
            drv, dgv = vjp((tuple(dro), tuple(dso)))
            for o_ref, val in zip(outs[:nr], drv):
                o_ref[...] = val.astype(o_ref.dtype)
            if ng:
                @pl.when(pl.program_id(0) == 0)
                def _():
                    for o_ref in outs[nr:]:
                        o_ref[...] = jnp.zeros_like(o_ref)
                for o_ref, val in zip(outs[nr:], dgv):
                    o_ref[...] += val

        in_specs = specs(rows, gpars, cpars, consts, tm)
        in_specs += [pl.BlockSpec((tm, w), lambda i: (i, 0)) for w, _ in row_out]
        in_specs += [pl.BlockSpec(shp, lambda i: (0, 0)) for shp in sum_out]
        out_specs = [pl.BlockSpec((tm, t.shape[1]), lambda i: (i, 0)) for t in rows]
        out_specs += [pl.BlockSpec(p.shape, lambda i: (0, 0)) for p in gpars]
        out_shape = [jax.ShapeDtypeStruct(t.shape, t.dtype) for t in rows]
        out_shape += [jax.ShapeDtypeStruct(p.shape, f32) for p in gpars]
        res = pl.pallas_call(
            body, name=name + "_bwd", grid=(r // tm,),
            in_specs=in_specs, out_specs=out_specs, out_shape=out_shape,
            compiler_params=_cparams(dimension_semantics=("arbitrary",)),
        )(*rows, *gpars, *cpars, *consts, *d_ro, *d_so)
        return tuple(res[:nr]), tuple(res[nr:])

    @jax.custom_vjp
    def op(rows, gpars, cpars, consts):
        return forward(rows, gpars, cpars, consts)

    def op_fwd(rows, gpars, cpars, consts):
        return forward(rows, gpars, cpars, consts), (rows, gpars, cpars, consts)

    def op_bwd(res, cts):
        rows, gpars, cpars, consts = res
        d_ro, d_so = cts
        drows, dg = backward(rows, gpars, cpars, consts, d_ro, d_so)
        dg = tuple(d.astype(p.dtype) for d, p in zip(dg, gpars))
        return (drows, dg, tuple(jnp.zeros_like(p) for p in cpars), tuple(jnp.zeros_like(k) for k in consts))

    op.defvjp(op_fwd, op_bwd)
    return op


def _rms(x):
    return x * lax.rsqrt(jnp.mean(x * x, axis=-1, keepdims=True) + EPS)


def _silu(x):
    return x * jax.nn.sigmoid(x)


def _fn_norm_mod(rows, gp, cp, ks):
    (x,), (nw, sc, sh) = rows, gp
    return ((_rms(x) * nw) * (1.0 + sc) + sh, x), ()


PAIR = 2 * HEAD_DIM


def _exact_dot(a, m):
    hi = a.astype(bf16)
    lo = (a - hi.astype(f32)).astype(bf16)
    return jnp.dot(hi, m, preferred_element_type=f32) + jnp.dot(lo, m, preferred_element_type=f32)


def _make_sel_dot(sign):
    @jax.custom_vjp
    def sel_dot(a, m):
        return _exact_dot(a, m)

    def fwd(a, m):
        return _exact_dot(a, m), m

    def bwd(m, g):
        return sign * _exact_dot(g, m), jnp.zeros_like(m)

    sel_dot.defvjp(fwd, bwd)
    return sel_dot


_head_sum_dot = _make_sel_dot(1.0)
_rope_perm_dot = _make_sel_dot(-1.0)


def _pair_norm_rope(t, w2, gsum, perm, cos2, sin2, out_scale):
    ss = _head_sum_dot(t * t, gsum)
    u = t * lax.rsqrt(ss * (1.0 / HEAD_DIM) + EPS) * w2
    return (u * cos2 + _rope_perm_dot(u, perm) * sin2) * out_scale


def _pair_consts():
    eye = np.eye(2, dtype=np.float32)
    gsum = np.kron(eye, np.ones((HEAD_DIM, HEAD_DIM), np.float32))
    return jnp.asarray(gsum, bf16), jnp.asarray(np.kron(eye, _rope_perm()), bf16)


def make_head_rope(name, nh, out_scale, head_major):
    width = nh * HEAD_DIM
    fn = functools.partial(_pair_norm_rope, out_scale=out_scale)

    def out_spec(tm):
        if head_major:
            return pl.BlockSpec((nh, tm, HEAD_DIM), lambda i: (0, i, 0))
        return pl.BlockSpec((tm, width), lambda i: (i, 0))

    def specs(tm):
        def full(shp):
            return pl.BlockSpec(shp, lambda i: (0, 0))

        return [pl.BlockSpec((tm, width), lambda i: (i, 0)), full((1, PAIR)), full((PAIR, PAIR)), full((PAIR, PAIR)),
                pl.BlockSpec((tm, PAIR), lambda i: (i, 0)), pl.BlockSpec((tm, PAIR), lambda i: (i, 0))]

    def forward(t, w2, gsum, perm, cos2, sin2):
        s = t.shape[0]
        tm = _pick(s, (512, 256, 128))

        def body(t_ref, w_ref, g_ref, p_ref, cos_ref, sin_ref, o_ref):
            for b in range(nh // 2):
                val = fn(t_ref[:, b * PAIR:(b + 1) * PAIR].astype(f32), w_ref[...], g_ref[...], p_ref[...], cos_ref[...],
                         sin_ref[...]).astype(o_ref.dtype)
                if head_major:
                    o_ref[2 * b] = val[:, :HEAD_DIM]
                    o_ref[2 * b + 1] = val[:, HEAD_DIM:]
                else:
                    o_ref[:, b * PAIR:(b + 1) * PAIR] = val

        return pl.pallas_call(
            body, name=name + "_fwd", grid=(s // tm,), in_specs=specs(tm), out_specs=out_spec(tm),
            out_shape=jax.ShapeDtypeStruct((nh, s, HEAD_DIM) if head_major else (s, width), bf16),
            compiler_params=_cparams(dimension_semantics=("arbitrary",)),
        )(t, w2, gsum, perm, cos2, sin2)

    def backward(t, w2, gsum, perm, cos2, sin2, dout):
        s = t.shape[0]
        tm = _pick(s, (512, 256, 128))

        def body(t_ref, w_ref, g_ref, p_ref, cos_ref, sin_ref, do_ref, dt_ref, dw_ref, pair_buf):
            @pl.when(pl.program_id(0) == 0)
            def _():
                dw_ref[...] = jnp.zeros_like(dw_ref)

            g_v, p_v, cos_v, sin_v = g_ref[...], p_ref[...], cos_ref[...], sin_ref[...]
            dw = jnp.zeros((1, PAIR), f32)
            for b in range(nh // 2):
                sl = slice(b * PAIR, (b + 1) * PAIR)
                if head_major:
                    pair_buf[:, :HEAD_DIM] = do_ref[2 * b].astype(f32)
                    pair_buf[:, HEAD_DIM:] = do_ref[2 * b + 1].astype(f32)
                    ct = pair_buf[...]
                else:
                    ct = do_ref[:, sl].astype(f32)
                _, vjp = jax.vjp(lambda a, c: fn(a, c, g_v, p_v, cos_v, sin_v), t_ref[:, sl].astype(f32), w_ref[...])
                dtb, dwb = vjp(ct)
                dt_ref[:, sl] = dtb.astype(dt_ref.dtype)
                dw = dw + dwb
            dw_ref[...] += dw

        return pl.pallas_call(
            body, name=name + "_bwd", grid=(s // tm,), in_specs=specs(tm) + [out_spec(tm)],
            out_specs=[pl.BlockSpec((tm, width), lambda i: (i, 0)), pl.BlockSpec((1, PAIR), lambda i: (0, 0))],
            out_shape=[jax.ShapeDtypeStruct((s, width), t.dtype), jax.ShapeDtypeStruct((1, PAIR), f32)],
            scratch_shapes=[pltpu.VMEM((tm, PAIR), f32)],
            compiler_params=_cparams(dimension_semantics=("arbitrary",)),
        )(t, w2, gsum, perm, cos2, sin2, dout)

    @jax.custom_vjp
    def op(t, w2, gsum, perm, cos2, sin2):
        return forward(t, w2, gsum, perm, cos2, sin2)

    def op_fwd(*args):
        return forward(*args), args

    def op_bwd(res, dout):
        dt, dw = backward(*res, dout)
        return (dt, dw) + tuple(jnp.zeros_like(r) for r in res[2:])

    op.defvjp(op_fwd, op_bwd)

    def apply(t, w, cos2, sin2):
        gsum, perm = _pair_consts()
        return op(t, jnp.concatenate([w, w], axis=-1), gsum, perm, cos2, sin2)

    return apply


def _fn_softplus(rows, gp, cp, ks):
    (x,), (b,) = rows, gp
    v = x + b
    return (jnp.maximum(v, 0.0) + jnp.log(1.0 + jnp.exp(-jnp.abs(v))),), ()


def _fn_ssd_gate(rows, gp, cp, ks):
    (y, z), (nw,) = rows, gp
    return (_rms(y * _silu(z)) * nw,), ()


def _fn_merge(rows, gp, cp, ks):
    ao, so, ga, gs = rows
    return (jax.nn.sigmoid(ga) * ao + jax.nn.sigmoid(gs) * so,), ()


def _fn_res_norm(rows, gp, cp, ks):
    (x, mo), (g1, nw, sc, sh) = rows, gp
    x1 = x + g1 * mo
    return (x1, (_rms(x1) * nw) * (1.0 + sc) + sh), ()


def _fn_loss(rows, gp, cp, ks):
    (x1, ff), (g2,), (tgt,) = rows, gp, ks
    err = x1 + g2 * ff - tgt
    return (), (0.5 * jnp.sum(jnp.sum(err * err, axis=-1, keepdims=True), axis=0, keepdims=True) / D_MODEL,)


HALO = 8
HALO_BWD = 16


def _conv_tiles(s, c):
    return _pick(s, (512, 256, 128)), _pick(c, (512, 256, 128))


def _halo_specs(tm, tc, s, halo=HALO):
    nb = tm // halo
    last = s // halo - 1
    cur = pl.BlockSpec((tm, tc), lambda j, i: (i, j))
    prev = pl.BlockSpec((halo, tc), lambda j, i: (jnp.maximum(i * nb - 1, 0), j))
    nxt = pl.BlockSpec((halo, tc), lambda j, i: (jnp.minimum((i + 1) * nb, last), j))
    return cur, prev, nxt


def _fill_halo(buf, cur, prev, nxt, tm, i, n_i, halo=HALO):
    buf[halo:halo + tm, :] = cur[...]
    buf[0:halo, :] = jnp.where(i > 0, prev[...], 0.0)
    buf[halo + tm:, :] = jnp.where(i < n_i - 1, nxt[...], 0.0)


def _conv_fwd(x, w, b, shard):
    s, c = x.shape
    tm, tc = _conv_tiles(s, c)
    n_i, n_j = s // tm, c // tc

    def body(cur, prev, nxt, w_ref, b_ref, shard_ref, o_ref, gath_ref, buf, send_sems, recv_sems, local_sem):
        j, i = pl.program_id(0), pl.program_id(1)
        start, finish = _gather_phases(shard_ref, gath_ref, send_sems, recv_sems, local_sem)

        @pl.when((j == 0) & (i == 0))
        def _():
            start()

        _fill_halo(buf, cur, prev, nxt, tm, i, n_i)
        pre = jnp.zeros((tm, tc), f32) + b_ref[...]
        for k in range(D_CONV):
            pre = pre + buf[HALO - 2 + k:HALO - 2 + k + tm, :] * w_ref[k:k + 1, :]
        o_ref[...] = _silu(pre)

        @pl.when((j == n_j - 1) & (i == n_i - 1))
        def _():
            finish()

    cur, prev, nxt = _halo_specs(tm, tc, s)
    hbm = pl.BlockSpec(memory_space=pl.ANY)
    return pl.pallas_call(
        body, name="conv_silu_fwd", grid=(n_j, n_i),
        in_specs=[cur, prev, nxt, pl.BlockSpec((D_CONV, tc), lambda j, i: (0, j)),
                  pl.BlockSpec((1, tc), lambda j, i: (0, j)), hbm],
        out_specs=[pl.BlockSpec((tm, tc), lambda j, i: (i, j)), hbm],
        out_shape=[jax.ShapeDtypeStruct((s, c), f32), jax.ShapeDtypeStruct((N_DEV,) + shard.shape, shard.dtype)],
        scratch_shapes=[pltpu.VMEM((tm + 2 * HALO, tc), f32)] + COMM_SEMS,
        compiler_params=_cparams(dimension_semantics=("arbitrary", "arbitrary")),
    )(x, x, x, w, b, shard)


def _conv_bwd(x, w, b, dy, g):
    s, c = x.shape
    tm, tc = _conv_tiles(s, c)
    n_i, n_j = s // tm, c // tc
    ext = tm + 16

    def body(cur, prev, nxt, dcur, dprev, dnxt, w_ref, b_ref, g_ref, dx_ref, dw_ref, db_ref, recv_ref,
             xbuf, dbuf, pbuf, send_sems, recv_sems, local_sem):
        j, i = pl.program_id(0), pl.program_id(1)
        start, finish = _scatter_phases(g_ref, recv_ref, send_sems, recv_sems, local_sem)

        @pl.when((j == 0) & (i == 0))
        def _():
            start()

        _fill_halo(xbuf, cur, prev, nxt, tm, i, n_i, HALO_BWD)
        _fill_halo(dbuf, dcur, dprev, dnxt, tm, i, n_i, HALO_BWD)
        xs = [xbuf[6 + k:6 + k + ext, :] for k in range(D_CONV)]
        pre = jnp.zeros((ext, tc), f32) + b_ref[...]
        for k in range(D_CONV):
            pre = pre + xs[k] * w_ref[k:k + 1, :]
        sg = jax.nn.sigmoid(pre)
        pbuf[...] = dbuf[8:8 + ext, :] * (sg * (1.0 + pre * (1.0 - sg)))
        dx = jnp.zeros((tm, tc), f32)
        for k in range(D_CONV):
            dx = dx + pbuf[10 - k:10 - k + tm, :] * w_ref[k:k + 1, :]
        dx_ref[...] = dx

        @pl.when(i == 0)
        def _():
            dw_ref[...] = jnp.zeros_like(dw_ref)
            db_ref[...] = jnp.zeros_like(db_ref)

        dpre = pbuf[8:8 + tm, :]
        db_ref[...] += jnp.sum(dpre, axis=0, keepdims=True)
        for k in range(D_CONV):
            dw_ref[k:k + 1, :] += jnp.sum(dpre * xs[k][8:8 + tm, :], axis=0, keepdims=True)

        @pl.when((j == n_j - 1) & (i == n_i - 1))
        def _():
            finish()

    cur, prev, nxt = _halo_specs(tm, tc, s, HALO_BWD)
    hbm = pl.BlockSpec(memory_space=pl.ANY)
    return pl.pallas_call(
        body, name="conv_silu_bwd", grid=(n_j, n_i),
        in_specs=[cur, prev, nxt, cur, prev, nxt, pl.BlockSpec((D_CONV, tc), lambda j, i: (0, j)),
                  pl.BlockSpec((1, tc), lambda j, i: (0, j)), hbm],
        out_specs=[pl.BlockSpec((tm, tc), lambda j, i: (i, j)), pl.BlockSpec((D_CONV, tc), lambda j, i: (0, j)),
                   pl.BlockSpec((1, tc), lambda j, i: (0, j)), hbm],
        out_shape=[jax.ShapeDtypeStruct((s, c), f32), jax.ShapeDtypeStruct((D_CONV, c), f32),
                   jax.ShapeDtypeStruct((1, c), f32), jax.ShapeDtypeStruct(g.shape, g.dtype)],
        scratch_shapes=[pltpu.VMEM((tm + 2 * HALO_BWD, tc), f32), pltpu.VMEM((tm + 2 * HALO_BWD, tc), f32),
                        pltpu.VMEM((ext, tc), f32)] + COMM_SEMS,
        compiler_params=_cparams(dimension_semantics=("arbitrary", "arbitrary")),
    )(x, x, x, dy, dy, dy, w, b, g)


@jax.custom_vjp
def conv_silu_comm(x, w, b, shard, recv_like):
    act, gathered = _conv_fwd(x, w, b, shard)
    return (act, gathered) + tuple(jnp.zeros(shp, f32) for shp in LATE_SHAPES)


def _conv_silu_comm_fwd(x, w, b, shard, recv_like):
    return conv_silu_comm(x, w, b, shard, recv_like), (x, w, b, shard)


def _conv_silu_comm_bwd(res, cts):
    x, w, b, shard = res
    dx, dw, db, recv = _conv_bwd(x, w, b, cts[0], _pack_late_grads(dict(zip(LATE, cts[2:]))))
    return dx, dw, db, jnp.zeros_like(shard), recv


conv_silu_comm.defvjp(_conv_silu_comm_fwd, _conv_silu_comm_bwd)


ATT_SCALE = HEAD_DIM ** -0.5
Q_SCALE = ATT_SCALE * math.log2(math.e)
LN2 = math.log(2.0)
REP = N_Q_HEADS // N_KV_HEADS


HP = 2
assert REP % HP == 0


def _attn_fwd(q, k, v):
    s, dh = q.shape[0], HEAD_DIM
    hq = q.shape[1] // dh
    tq = _pick(s, (256, 128))

    v1 = jnp.concatenate([v, jnp.ones(v.shape[:2] + (1,), v.dtype), jnp.zeros(v.shape[:2] + (dh - 1,), v.dtype)],
                         axis=-1)

    def body(q_ref, k_ref, v_ref, o_ref, p_ref, linv_ref):
        for j in range(HP):
            sl = slice(j * dh, (j + 1) * dh)
            sc = lax.dot_general(q_ref[:, sl], k_ref[0], _DIMS["nt"], preferred_element_type=f32)
            m = jnp.max(sc, axis=-1, keepdims=True)
            p = jnp.exp2(sc - m).astype(bf16)
            p_ref[j] = p
            o1 = jnp.dot(p, v_ref[0], preferred_element_type=f32)
            linv = 1.0 / o1[:, dh:dh + 1]
            o_ref[:, sl] = (o1[:, :dh] * linv).astype(o_ref.dtype)
            linv_ref[j] = linv

    return pl.pallas_call(
        body, name="attn_fwd", grid=(hq // HP, s // tq),
        in_specs=[pl.BlockSpec((tq, HP * dh), lambda h, i: (i, h)),
                  pl.BlockSpec((1, s, dh), lambda h, i: (h * HP // REP, 0, 0)),
                  pl.BlockSpec((1, s, 2 * dh), lambda h, i: (h * HP // REP, 0, 0))],
        out_specs=[pl.BlockSpec((tq, HP * dh), lambda h, i: (i, h)),
                   pl.BlockSpec((HP, tq, s), lambda h, i: (h, i, 0)),
                   pl.BlockSpec((HP, tq, 1), lambda h, i: (h, i, 0))],
        out_shape=[jax.ShapeDtypeStruct((s, hq * dh), bf16), jax.ShapeDtypeStruct((hq, s, s), bf16),
                   jax.ShapeDtypeStruct((hq, s, 1), f32)],
        compiler_params=_cparams(dimension_semantics=("parallel", "arbitrary")),
    )(q, k, v1)


def _attn_bwd(p, do, o, q, k, v, linv):
    hq, s, _ = p.shape
    dh = HEAD_DIM
    tq = _pick(s, (256, 128))

    def body(p_ref, do_ref, o_ref, q_ref, k_ref, v_ref, linv_ref, dq_ref, dkt_ref, dvt_ref):
        @pl.when(pl.program_id(1) == 0)
        def _():
            dkt_ref[...] = jnp.zeros_like(dkt_ref)
            dvt_ref[...] = jnp.zeros_like(dvt_ref)

        for j in range(HP):
            sl = slice(j * dh, (j + 1) * dh)
            pp, doh, li = p_ref[j], do_ref[:, sl], linv_ref[j]
            do32 = doh.astype(f32)
            d = jnp.sum(do32 * o_ref[:, sl].astype(f32), axis=-1, keepdims=True)
            dp = lax.dot_general(doh, v_ref[0], _DIMS["nt"], preferred_element_type=f32)
            ds = (pp.astype(f32) * ((dp - d) * li)).astype(bf16)
            dq_ref[:, sl] = (jnp.dot(ds, k_ref[0], preferred_element_type=f32) * LN2).astype(dq_ref.dtype)
            dvt_ref[j] += lax.dot_general((do32 * li).astype(bf16), pp, _DIMS["tn"], preferred_element_type=f32)
            dkt_ref[j] += lax.dot_general(q_ref[:, sl], ds, _DIMS["tn"], preferred_element_type=f32)

    def row():
        return pl.BlockSpec((tq, HP * dh), lambda h, i: (i, h))

    return pl.pallas_call(
        body, name="attn_bwd", grid=(hq // HP, s // tq),
        in_specs=[pl.BlockSpec((HP, tq, s), lambda h, i: (h, i, 0)), row(), row(), row(),
                  pl.BlockSpec((1, s, dh), lambda h, i: (h * HP // REP, 0, 0)),
                  pl.BlockSpec((1, s, dh), lambda h, i: (h * HP // REP, 0, 0)),
                  pl.BlockSpec((HP, tq, 1), lambda h, i: (h, i, 0))],
        out_specs=[row(), pl.BlockSpec((HP, dh, s), lambda h, i: (h, 0, 0)),
                   pl.BlockSpec((HP, dh, s), lambda h, i: (h, 0, 0))],
        out_shape=[jax.ShapeDtypeStruct((s, hq * dh), q.dtype), jax.ShapeDtypeStruct((hq, dh, s), f32),
                   jax.ShapeDtypeStruct((hq, dh, s), f32)],
        compiler_params=_cparams(dimension_semantics=("parallel", "arbitrary")),
    )(p, do, o, q, k, v, linv)


@jax.custom_vjp
def attention(q, k, v):
    return _attn_fwd(q, k, v)[0]


def _attention_fwd(q, k, v):
    o, p, linv = _attn_fwd(q, k, v)
    return o, (q, k, v, o, p, linv)


def _attention_bwd(res, do):
    q, k, v, o, p, linv = res
    s = q.shape[0]
    dq, dkt, dvt = _attn_bwd(p, do.astype(bf16), o, q, k, v, linv)

    def per_kv_head(t):
        return jnp.swapaxes(t.reshape(N_KV_HEADS, REP, HEAD_DIM, s).sum(axis=1), 1, 2)

    return dq, (per_kv_head(dkt) * LN2).astype(k.dtype), per_kv_head(dvt).astype(v.dtype)


attention.defvjp(_attention_fwd, _attention_bwd)


HPG = N_SSD_HEADS // N_SSD_GROUPS
GW = HPG * SSD_HEAD_DIM
NEG = -1e30
SPLIT_ROWS = 32


def _ssd_consts():
    k = np.arange(SPLIT_ROWS)[:, None]
    live = k < 3 * HPG
    sel_chunk = ((k % HPG) == (np.arange(HPG * CHUNK)[None, :] // CHUNK)) & live
    sel_head = ((k % HPG) == (np.arange(GW)[None, :] // SSD_HEAD_DIM)) & live
    return jnp.asarray(sel_chunk, bf16), jnp.asarray(sel_head, bf16)


def _split3(x):
    hi = x.astype(bf16).astype(f32)
    r1 = x - hi
    mid = r1.astype(bf16).astype(f32)
    lo = (r1 - mid).astype(bf16).astype(f32)
    return jnp.concatenate([hi, mid, lo, jnp.zeros_like(hi)], axis=0).astype(bf16)


def _tn(a, b):
    return lax.dot_general(a, b, _DIMS["tn"], preferred_element_type=f32)


def _nt(a, b):
    return lax.dot_general(a, b, _DIMS["nt"], preferred_element_type=f32)


def _nn(a, b):
    return jnp.dot(a, b, preferred_element_type=f32)


def _head_sum(sel8, x):
    hi = x.astype(bf16)
    lo = (x - hi.astype(f32)).astype(bf16)
    return _nt(sel8, hi) + _nt(sel8, lo)


def _ssd_masks(reverse):
    r = lax.broadcasted_iota(jnp.int32, (CHUNK, CHUNK), 0)
    c = lax.broadcasted_iota(jnp.int32, (CHUNK, CHUNK), 1)
    lower, upper = r >= c, r <= c
    return (upper, lower) if reverse else (lower, upper)


def _ssd_in_specs(cidx):
    return [pl.BlockSpec((CHUNK, D_INNER), lambda c: (cidx(c), 0)),
            pl.BlockSpec((CHUNK, GN), lambda c: (cidx(c), D_INNER // GN)),
            pl.BlockSpec((CHUNK, GN), lambda c: (cidx(c), D_INNER // GN + 1)),
            pl.BlockSpec((N_SSD_HEADS, CHUNK), lambda c: (0, cidx(c))),
            pl.BlockSpec((N_SSD_HEADS, 1), lambda c: (0, 0)),
            pl.BlockSpec((SPLIT_ROWS, HPG * CHUNK), lambda c: (0, 0)),
            pl.BlockSpec((SPLIT_ROWS, GW), lambda c: (0, 0))]


def _ssd_chunk_common(dtt_ref, a_ref, et_ref, mask_t):
    dtt = dtt_ref[...]
    et = jnp.dot(dtt * a_ref[...], mask_t.astype(f32), precision=HIGHEST, preferred_element_type=f32)
    et_ref[...] = et
    return dtt, et


def _ssd_group_common(g, dtt, et, selc_ref, selh_ref, xs_ref, b_ref, c_ref, last):
    gr = slice(g * HPG, (g + 1) * HPG)
    e3 = _split3(et[gr])
    col = _tn(e3, selc_ref[...])
    eb = _tn(e3, selh_ref[...])
    dtb = _tn(_split3(dtt[gr]), selh_ref[...])
    tbc = eb[last:last + 1, :]
    xs = xs_ref[:, g * GW:(g + 1) * GW]
    bg = b_ref[:, g * D_STATE:(g + 1) * D_STATE].astype(bf16)
    cg = c_ref[:, g * D_STATE:(g + 1) * D_STATE].astype(bf16)
    return col, eb, dtb, tbc, xs, bg, cg


def _ssd_fwd(xbc, dtt, a_col, reverse, y_prev=None, dexp=None):
    s = xbc.shape[0]
    nc = s // CHUNK
    cidx = (lambda c: nc - 1 - c) if reverse else (lambda c: c)
    last = 0 if reverse else CHUNK - 1
    selc, selh = _ssd_consts()
    final = y_prev is not None
    n_in = 9 if final else 7

    def body(*refs):
        xs_ref, b_ref, c_ref, dtt_ref, a_ref, selc_ref, selh_ref = refs[:7]
        y_ref, st_ref, ht_ref, et_ref = refs[n_in:]

        @pl.when(pl.program_id(0) == 0)
        def _():
            ht_ref[...] = jnp.zeros_like(ht_ref)

        mask, mask_t = _ssd_masks(reverse)
        dtt_v, et = _ssd_chunk_common(dtt_ref, a_ref, et_ref, mask_t)
        for g in range(N_SSD_GROUPS):
            col, eb, dtb, tbc, xs, bg, cg = _ssd_group_common(g, dtt_v, et, selc_ref, selh_ref, xs_ref, b_ref, c_ref,
                                                              last)
            xd = xs * dtb
            cb = _nt(cg, bg)
            ht = ht_ref[g]
            st_ref[0, g] = ht
            yoff = _nn(cg, ht.astype(bf16)) * jnp.exp(eb)
            for j in range(HPG):
                h = g * HPG + j
                hs = slice(j * SSD_HEAD_DIM, (j + 1) * SSD_HEAD_DIM)
                lam = jnp.exp(jnp.where(mask, col[:, j * CHUNK:(j + 1) * CHUNK] - et_ref[h:h + 1, :], NEG))
                yj = _nn((cb * lam).astype(bf16), xd[:, hs].astype(bf16)) + yoff[:, hs]
                cols = slice(g * GW + j * SSD_HEAD_DIM, g * GW + (j + 1) * SSD_HEAD_DIM)
                if final:
                    yj = yj + refs[7][:, cols] + xs[:, hs] * refs[8][:, cols]
                y_ref[:, cols] = yj
            ht_ref[g] = jnp.exp(tbc) * ht + _tn(bg, (xd * jnp.exp(tbc - eb)).astype(bf16))

    y_spec = pl.BlockSpec((CHUNK, D_INNER), lambda c: (cidx(c), 0))
    extra_specs = [y_spec, pl.BlockSpec((1, D_INNER), lambda c: (0, 0))] if final else []
    return pl.pallas_call(
        body, name="ssd_fwd_rev" if reverse else "ssd_fwd", grid=(nc,),
        in_specs=_ssd_in_specs(cidx) + extra_specs,
        out_specs=[y_spec, pl.BlockSpec((1, N_SSD_GROUPS, D_STATE, GW), lambda c: (cidx(c), 0, 0, 0))],
        out_shape=[jax.ShapeDtypeStruct((s, D_INNER), f32),
                   jax.ShapeDtypeStruct((nc, N_SSD_GROUPS, D_STATE, GW), f32)],
        scratch_shapes=[pltpu.VMEM((N_SSD_GROUPS, D_STATE, GW), f32), pltpu.VMEM((N_SSD_HEADS, CHUNK), f32)],
        compiler_params=_cparams(dimension_semantics=("arbitrary",)),
    )(xbc, xbc, xbc, dtt, a_col, selc, selh, *((y_prev, dexp) if final else ()))


def _ssd_bwd(xbc, dtt, a_col, states, dy, reverse, dxbc_prev=None, dexp=None):
    s = xbc.shape[0]
    nc = s // CHUNK
    cidx = (lambda c: c) if reverse else (lambda c: nc - 1 - c)
    last = 0 if reverse else CHUNK - 1
    selc, selh = _ssd_consts()
    final = dxbc_prev is not None
    n_in = 11 if final else 9
    n_out = 4 if final else 3

    def body(*refs):
        xs_ref, b_ref, c_ref, dtt_ref, a_ref, selc_ref, selh_ref, st_ref, dy_ref = refs[:9]
        dxbc_ref, ddtt_ref, da_ref = refs[n_in:n_in + 3]
        dh_ref, et_ref, det_ref, det2_ref, ddt_ref, q_ref = refs[n_in + n_out:]
        if final:
            prev_ref, dexp_ref, ddexp_ref = refs[9], refs[10], refs[n_in + 3]

        @pl.when(pl.program_id(0) == 0)
        def _():
            dh_ref[...] = jnp.zeros_like(dh_ref)
            da_ref[...] = jnp.zeros_like(da_ref)
            if final:
                ddexp_ref[...] = jnp.zeros_like(ddexp_ref)

        mask, mask_t = _ssd_masks(reverse)
        dtt_v, et = _ssd_chunk_common(dtt_ref, a_ref, et_ref, mask_t)
        sel8 = selh_ref[0:HPG, :]
        is_last = lax.broadcasted_iota(jnp.int32, (CHUNK, GW), 0) == last
        for g in range(N_SSD_GROUPS):
            col, eb, dtb, tbc, xs, bg, cg = _ssd_group_common(g, dtt_v, et, selc_ref, selh_ref, xs_ref, b_ref, c_ref,
                                                              last)
            xd = xs * dtb
            cb = _nt(cg, bg)
            cbt = _nt(bg, cg)
            exp_t = jnp.exp(tbc)
            dfac = jnp.exp(tbc - eb)
            ht = st_ref[0, g]
            dhn = dh_ref[g]
            ht16, dhn16 = ht.astype(bf16), dhn.astype(bf16)
            dy = dy_ref[:, g * GW:(g + 1) * GW]
            dye = dy * jnp.exp(eb)
            dye16 = dye.astype(bf16)
            dc = _nt(dye16, ht16)
            dh_ref[g] = exp_t * dhn + _tn(cg, dye16)
            deb = dye * _nn(cg, ht16)
            xdd = xd * dfac
            dxdd = _nn(bg, dhn16)
            db = _nt(xdd.astype(bf16), dhn16)
            dxd_state = dxdd * dfac
            ddf = dxdd * xdd
            dtbc = jnp.sum(ddf, axis=0, keepdims=True) + exp_t * jnp.sum(dhn * ht, axis=0, keepdims=True)
            deb = deb - ddf + jnp.where(is_last, dtbc, 0.0)
            dcb = jnp.zeros((CHUNK, CHUNK), f32)
            dcbt = jnp.zeros((CHUNK, CHUNK), f32)
            for j in range(HPG):
                h = g * HPG + j
                hs = slice(j * SSD_HEAD_DIM, (j + 1) * SSD_HEAD_DIM)
                colj = col[:, j * CHUNK:(j + 1) * CHUNK]
                row = et_ref[h:h + 1, :]
                lam = jnp.exp(jnp.where(mask, colj - row, NEG))
                lam_t = jnp.exp(jnp.where(mask_t, row - colj, NEG))
                xdj, dyj = xd[:, hs].astype(bf16), dy[:, hs].astype(bf16)
                t1 = _nt(dyj, xdj) * lam
                t2 = _nt(xdj, dyj) * lam_t
                dcb, dcbt = dcb + t1, dcbt + t2
                det_ref[h:h + 1, :] = -jnp.sum(t1 * cb - t2 * cbt, axis=0, keepdims=True)
                dxdj = _nn((cbt * lam_t).astype(bf16), dyj) + dxd_state[:, hs]
                cols = slice(g * GW + j * SSD_HEAD_DIM, g * GW + (j + 1) * SSD_HEAD_DIM)
                dxs = dxdj * dtb[:, hs]
                if final:
                    dxs = dxs + prev_ref[:, cols] + dy[:, hs] * dexp_ref[:, cols]
                dxbc_ref[:, cols] = dxs
                q_ref[:, hs] = dxdj * xs[:, hs]
            b_cols = slice(D_INNER + g * D_STATE, D_INNER + (g + 1) * D_STATE)
            c_cols = slice(D_INNER + GN + g * D_STATE, D_INNER + GN + (g + 1) * D_STATE)
            db = db + _nn(dcbt.astype(bf16), cg)
            dc = dc + _nn(dcb.astype(bf16), bg)
            if final:
                db, dc = db + prev_ref[:, b_cols], dc + prev_ref[:, c_cols]
                ddexp_ref[:, g * GW:(g + 1) * GW] += jnp.sum(dy * xs, axis=0, keepdims=True)
            dxbc_ref[:, b_cols] = db
            dxbc_ref[:, c_cols] = dc
            det2_ref[g * HPG:(g + 1) * HPG, :] = _head_sum(sel8, deb)
            ddt_ref[g * HPG:(g + 1) * HPG, :] = _head_sum(sel8, q_ref[...])
        dat = jnp.dot(det_ref[...] + det2_ref[...], mask.astype(f32), precision=HIGHEST, preferred_element_type=f32)
        ddtt_ref[...] = ddt_ref[...] + dat * a_ref[...]
        da_ref[...] += jnp.sum(dat * dtt_v, axis=1, keepdims=True)

    in_specs = _ssd_in_specs(cidx) + [
        pl.BlockSpec((1, N_SSD_GROUPS, D_STATE, GW), lambda c: (cidx(c), 0, 0, 0)),
        pl.BlockSpec((CHUNK, D_INNER), lambda c: (cidx(c), 0))]
    hl = pltpu.VMEM((N_SSD_HEADS, CHUNK), f32)
    dxbc_spec = pl.BlockSpec((CHUNK, CONV_DIM), lambda c: (cidx(c), 0))
    dexp_spec = pl.BlockSpec((1, D_INNER), lambda c: (0, 0))
    return pl.pallas_call(
        body, name="ssd_bwd_rev" if reverse else "ssd_bwd", grid=(nc,),
        in_specs=in_specs + ([dxbc_spec, dexp_spec] if final else []),
        out_specs=[dxbc_spec, pl.BlockSpec((N_SSD_HEADS, CHUNK), lambda c: (0, cidx(c))),
                   pl.BlockSpec((N_SSD_HEADS, 1), lambda c: (0, 0))] + ([dexp_spec] if final else []),
        out_shape=[jax.ShapeDtypeStruct((s, CONV_DIM), f32), jax.ShapeDtypeStruct((N_SSD_HEADS, s), f32),
                   jax.ShapeDtypeStruct((N_SSD_HEADS, 1), f32)]
        + ([jax.ShapeDtypeStruct((1, D_INNER), f32)] if final else []),
        scratch_shapes=[pltpu.VMEM((N_SSD_GROUPS, D_STATE, GW), f32), hl, hl, hl, hl, pltpu.VMEM((CHUNK, GW), f32)],
        compiler_params=_cparams(dimension_semantics=("arbitrary",)),
    )(xbc, xbc, xbc, dtt, a_col, selc, selh, states, dy, *((dxbc_prev, dexp) if final else ()))


@jax.custom_vjp
def ssd_bidir(xbc, dtt, a_col, dexp):
    y_f, _ = _ssd_fwd(xbc, dtt[:N_SSD_HEADS], a_col[:N_SSD_HEADS], False)
    return _ssd_fwd(xbc, dtt[N_SSD_HEADS:], a_col[N_SSD_HEADS:], True, y_prev=y_f, dexp=dexp)[0]


def _ssd_bidir_fwd(xbc, dtt, a_col, dexp):
    y_f, st_f = _ssd_fwd(xbc, dtt[:N_SSD_HEADS], a_col[:N_SSD_HEADS], False)
    y, st_b = _ssd_fwd(xbc, dtt[N_SSD_HEADS:], a_col[N_SSD_HEADS:], True, y_prev=y_f, dexp=dexp)
    return y, (xbc, dtt, a_col, dexp, st_f, st_b)


def _ssd_bidir_bwd(res, dy):
    xbc, dtt, a_col, dexp, st_f, st_b = res
    dxbc_f, ddtt_f, da_f = _ssd_bwd(xbc, dtt[:N_SSD_HEADS], a_col[:N_SSD_HEADS], st_f, dy, False)
    dxbc, ddtt_b, da_b, ddexp = _ssd_bwd(xbc, dtt[N_SSD_HEADS:], a_col[N_SSD_HEADS:], st_b, dy, True,
                                         dxbc_prev=dxbc_f, dexp=dexp)
    return dxbc, jnp.concatenate([ddtt_f, ddtt_b], axis=0), jnp.concatenate([da_f, da_b], axis=0), ddexp


ssd_bidir.defvjp(_ssd_bidir_fwd, _ssd_bidir_bwd)


W_NAMES = PROJ_NAMES + ("attn_out", "ssd_out", "o", "mlp1", "mlp2")


def _rope_tables(s):
    rows = s // GRID_W
    pos_row = np.repeat(np.arange(rows), GRID_W).astype(np.float32)
    pos_col = np.tile(np.arange(GRID_W), rows).astype(np.float32)
    axis_dim = HEAD_DIM // 2
    inv_freq = np.float32(ROPE_THETA) ** (-np.arange(0, axis_dim, 2, dtype=np.float32) / np.float32(axis_dim))
    ang_r = pos_row[:, None] * inv_freq[None, :].astype(np.float32)
    ang_c = pos_col[:, None] * inv_freq[None, :].astype(np.float32)
    cos = np.concatenate([np.cos(ang_r), np.cos(ang_r), np.cos(ang_c), np.cos(ang_c)] * 2, axis=-1)
    sin = np.concatenate([np.sin(ang_r), np.sin(ang_r), np.sin(ang_c), np.sin(ang_c)] * 2, axis=-1)
    return jnp.asarray(cos, f32), jnp.asarray(sin, f32)


def _rope_perm():
    p = np.zeros((HEAD_DIM, HEAD_DIM), np.float32)
    for j in range(HEAD_DIM):
        if (j % 32) < 16:
            p[j + 16, j] = -1.0
        else:
            p[j - 16, j] = 1.0
    return p


def local_loss(x, mod, small, recv_in_like, recv_late_like, wfull, late_shard, target):
    s = x.shape[0]
    lin = {n: make_linear("lin_" + n) for n in LATE if not n.startswith("mlp")}
    wfull, wgrads = dict(wfull), {}
    shift1, scale1, gate1, shift2, scale2, gate2 = [mod[i] for i in range(6)]

    norm_mod = make_rowwise("norm_mod", _fn_norm_mod, [(D_MODEL, bf16), (D_MODEL, f32)])
    (h, x_res), _ = norm_mod((x,), (small["norm1_w"], scale1, shift1), (), ())

    proj = dict(zip(PROJ_NAMES, in_proj(h, tuple(wfull[n] for n in PROJ_NAMES), recv_in_like)))

    cos, sin = _rope_tables(s)

    def heads(t, nh):
        return t.reshape(s, nh, HEAD_DIM).transpose(1, 0, 2)

    qr = make_head_rope("q_norm_rope", N_Q_HEADS, Q_SCALE, False)(proj["q"], small["q_norm_w"], cos, sin)
    kr = make_head_rope("k_norm_rope", N_KV_HEADS, 1.0, True)(proj["k"], small["k_norm_w"], cos, sin)
    vh = heads(proj["v"], N_KV_HEADS).astype(bf16)
    att = attention(qr, kr, vh)

    xbc, gathered, *carriers = conv_silu_comm(proj["xbc"], small["conv_w"], small["conv_b"], late_shard,
                                              recv_late_like)
    wfull.update(_split_late(gathered))
    wgrads.update(zip(LATE, carriers))
    ao = lin["attn_out"](att, wfull["attn_out"], wgrads["attn_out"])
    softplus = make_rowwise("dt_softplus", _fn_softplus, [(2 * N_SSD_HEADS, f32)])
    (dt,), _ = softplus((proj["dt"][:, :2 * N_SSD_HEADS],), (small["dt_bias"].reshape(1, 2 * N_SSD_HEADS),), (), ())
    a_neg = -jnp.exp(small["A_log"])
    dexp = jnp.repeat(small["ssd_D"].reshape(N_SSD_HEADS), SSD_HEAD_DIM).reshape(1, D_INNER)
    y = ssd_bidir(xbc, dt.T, a_neg.reshape(2 * N_SSD_HEADS, 1), dexp)
    ssd_gate = make_rowwise("ssd_gate", _fn_ssd_gate, [(D_INNER, bf16)], tm_pref=128)
    (ssd_out,), _ = ssd_gate((y, proj["z"]), (small["ssd_norm_w"],), (), ())
    so = lin["ssd_out"](ssd_out, wfull["ssd_out"], wgrads["ssd_out"])

    merge = make_rowwise("merge", _fn_merge, [(D_MODEL, bf16)])
    (merged,), _ = merge((ao, so, proj["ga"], proj["gs"]), (), (), ())
    mo = lin["o"](merged, wfull["o"], wgrads["o"])

    res_norm = make_rowwise("res_norm", _fn_res_norm, [(D_MODEL, f32), (D_MODEL, bf16)])
    (x1, h2), _ = res_norm((x_res, mo), (gate1, small["norm2_w"], scale2, shift2), (), ())
    ff = mlp(h2, wfull["mlp1"], wgrads["mlp1"], wfull["mlp2"], wgrads["mlp2"])
    loss_op = make_rowwise("loss", _fn_loss, [], [(1, 1)])
    _, (loss,) = loss_op((x1, ff), (gate2,), (), (target,))
    return loss[0, 0]


_BC1 = 1.0 - ADAM_B1 ** ADAM_STEP
_BC2 = 1.0 - ADAM_B2 ** ADAM_STEP


def _adamw(w, g, m, v):
    m = ADAM_B1 * m + (1.0 - ADAM_B1) * g
    v = ADAM_B2 * v + (1.0 - ADAM_B2) * (g * g)
    delta = -ADAM_LR * ((m / _BC1) / (jnp.sqrt(v / _BC2) + ADAM_EPS) + ADAM_WD * w)
    return delta, m, v


def _ada_fwd(c_all, w, b):
    n = w.shape[1]

    def body(c_ref, w_ref, b_ref, o_ref):
        o_ref[...] = jnp.dot(_silu(c_ref[...]), w_ref[...], precision=HIGHEST, preferred_element_type=f32) + b_ref[...]

    return pl.pallas_call(body, name="ada_fwd", out_shape=jax.ShapeDtypeStruct((N_DEV, n), f32),
                          compiler_params=_cparams())(c_all, w, b)


def _ada_bwd_adamw(c_all, dmod, w, m, v):
    d, n = w.shape
    tr = _pick(d, (256, 128))

    def body(c_ref, dm_ref, w_ref, m_ref, v_ref, g_ref, dl_ref, mo_ref, vo_ref):
        g = lax.dot_general(_silu(c_ref[...]), dm_ref[...], _DIMS["tn"], precision=HIGHEST,
                            preferred_element_type=f32)
        g_ref[...] = g
        dl_ref[...], mo_ref[...], vo_ref[...] = _adamw(w_ref[...], g, m_ref[...], v_ref[...])

    blk = pl.BlockSpec((tr, n), lambda i: (i, 0))
    return pl.pallas_call(
        body, name="ada_bwd_adamw", grid=(d // tr,),
        in_specs=[pl.BlockSpec((N_DEV, tr), lambda i: (0, i)), pl.BlockSpec((N_DEV, n), lambda i: (0, 0)), blk, blk, blk],
        out_specs=[blk] * 4, out_shape=[jax.ShapeDtypeStruct((d, n), f32)] * 4,
        compiler_params=_cparams(dimension_semantics=("parallel",)),
    )(c_all, dmod, w, m, v)


def _sum_over_mesh(g):
    def body(g_ref, o_ref):
        acc = g_ref[0]
        for d in range(1, N_DEV):
            acc = acc + g_ref[d]
        o_ref[...] = acc

    return pl.pallas_call(body, name="sum_small", out_shape=jax.ShapeDtypeStruct(g.shape[1:], f32),
                          compiler_params=_cparams())(g)


def _adamw_small(w, g, m, v):
    def body(w_ref, g_ref, m_ref, v_ref, dl_ref, mo_ref, vo_ref):
        dl_ref[...], mo_ref[...], vo_ref[...] = _adamw(w_ref[...], g_ref[...], m_ref[...], v_ref[...])

    return pl.pallas_call(body, name="adamw_small", out_shape=[jax.ShapeDtypeStruct(w.shape, f32)] * 3,
                          compiler_params=_cparams())(w, g, m, v)


def _sum_adamw(recv, w, m, v, name):
    _, r, c = recv.shape
    tr = _pick(r, (256, 128, 64, 16))

    def body(g_ref, w_ref, m_ref, v_ref, go_ref, dl_ref, mo_ref, vo_ref):
        g = g_ref[0].astype(f32)
        for d in range(1, N_DEV):
            g = g + g_ref[d].astype(f32)
        go_ref[...] = g
        dl_ref[...], mo_ref[...], vo_ref[...] = _adamw(w_ref[...], g, m_ref[...], v_ref[...])

    blk = pl.BlockSpec((tr, c), lambda i: (i, 0))
    return pl.pallas_call(
        body, name=name, grid=(r // tr,),
        in_specs=[pl.BlockSpec((N_DEV, tr, c), lambda i: (0, i, 0)), blk, blk, blk],
        out_specs=[blk] * 4, out_shape=[jax.ShapeDtypeStruct((r, c), f32)] * 4,
        compiler_params=_cparams(dimension_semantics=("parallel",)),
    )(recv, w, m, v)


def _pack_small(arrs):
    parts = []
    for a in arrs:
        flat = a.reshape(-1).astype(f32)
        parts.append(jnp.pad(flat, (0, (-flat.shape[0]) % LANE)))
    flat = jnp.concatenate(parts)
    flat = jnp.pad(flat, (0, (-flat.shape[0]) % (8 * LANE)))
    return flat.reshape(-1, LANE)


def _unpack_small(packed, shapes):
    flat = packed.reshape(-1)
    out, off = [], 0
    for shp in shapes:
        n = int(np.prod(shp))
        out.append(flat[off:off + n].reshape(shp))
        off += n + (-n) % LANE
    return out


BIG = ("w_attn_out", "w_ssd_out", "w_o", "w_mlp1", "w_mlp2")
BIG_ROWS = (N_Q_HEADS * HEAD_DIM // N_DEV, D_INNER // N_DEV, D_MODEL // N_DEV,
            D_MODEL * (D_FF // N_DEV) // PACK_COLS, D_FF // N_DEV)
N_IN_SHARD = D_IN_PROJ // N_DEV
assert sum(BIG_ROWS) % 16 == 0


def _pack_big(shards, dtype):
    return jnp.concatenate([s.astype(dtype).reshape(-1, PACK_COLS) for s in shards], axis=0)


def _unpack_big(packed, shapes):
    out, off = [], 0
    for rows, shp in zip(BIG_ROWS, shapes):
        out.append(packed[off:off + rows].reshape(shp))
        off += rows
    return out


LATE = ("attn_out", "ssd_out", "o", "mlp1", "mlp2")
LATE_SHAPES = ((N_Q_HEADS * HEAD_DIM, D_MODEL), (D_INNER, D_MODEL), (D_MODEL, D_MODEL), (D_MODEL, D_FF),
               (D_FF, D_MODEL))


def _split_w_in(g_in):
    w_in = g_in.transpose(1, 0, 2).reshape(D_MODEL, D_IN_PROJ)
    w = {}
    off = 0
    for name, size in zip(PROJ_NAMES, PROJ_SIZES):
        w[name] = w_in[:, off:off + size]
        off += size
    w["dt"] = jnp.pad(w["dt"], ((0, 0), (0, DT_PAD - 2 * N_SSD_HEADS)))
    return w


def _split_late(g):
    offs = np.cumsum((0,) + BIG_ROWS)
    sl = [g[:, offs[i]:offs[i + 1]] for i in range(len(BIG))]
    return {"attn_out": sl[0].reshape(LATE_SHAPES[0]), "ssd_out": sl[1].reshape(LATE_SHAPES[1]),
            "o": sl[2].reshape(LATE_SHAPES[2]),
            "mlp1": sl[3].reshape(N_DEV, D_MODEL, D_FF // N_DEV).transpose(1, 0, 2).reshape(LATE_SHAPES[3]),
            "mlp2": sl[4].reshape(LATE_SHAPES[4])}


def _pack_in_grads(gw):
    gw = {n: g.astype(bf16) for n, g in gw.items()}
    gw["dt"] = gw["dt"][:, :2 * N_SSD_HEADS]
    g_in = jnp.concatenate([gw[n] for n in PROJ_NAMES], axis=1)
    return g_in.reshape(D_MODEL, N_DEV, N_IN_SHARD).transpose(1, 0, 2)


def _pack_late_grads(gw):
    gw = {n: g.astype(bf16) for n, g in gw.items()}
    parts = [
        gw["attn_out"].reshape(N_DEV, -1, PACK_COLS),
        gw["ssd_out"].reshape(N_DEV, -1, PACK_COLS),
        gw["o"].reshape(N_DEV, -1, PACK_COLS),
        gw["mlp1"].reshape(D_MODEL, N_DEV, D_FF // N_DEV).transpose(1, 0, 2).reshape(N_DEV, -1, PACK_COLS),
        gw["mlp2"].reshape(N_DEV, -1, PACK_COLS),
    ]
    return jnp.concatenate(parts, axis=1)


SMALL = ("norm1_w", "norm2_w", "q_norm_w", "k_norm_w", "conv_w", "conv_b", "A_log", "dt_bias", "ssd_D", "ssd_norm_w")


def kernel(x, c, w_ada, b_ada, norm1_w, norm2_w, w_in, q_norm_w, k_norm_w, conv_w, conv_b, A_log, dt_bias, ssd_D, ssd_norm_w, w_attn_out, w_ssd_out, w_o, w_mlp1, w_mlp2, loss_target, m_w_ada, m_b_ada, m_norm1_w, m_norm2_w, m_w_in, m_q_norm_w, m_k_norm_w, m_conv_w, m_conv_b, m_A_log, m_dt_bias, m_ssd_D, m_ssd_norm_w, m_w_attn_out, m_w_ssd_out, m_w_o, m_w_mlp1, m_w_mlp2, v_w_ada, v_b_ada, v_norm1_w, v_norm2_w, v_w_in, v_q_norm_w, v_k_norm_w, v_conv_w, v_conv_b, v_A_log, v_dt_bias, v_ssd_D, v_ssd_norm_w, v_w_attn_out, v_w_ssd_out, v_w_o, v_w_mlp1, v_w_mlp2):
    args = dict(locals())
    me = _my_index()
    n_ada = 6 * D_MODEL // N_DEV
    n_cw = CONV_DIM // N_DEV

    blk = jnp.zeros((8, D_MODEL), f32)
    blk = blk.at[0:1, :].set(c)
    blk = blk.at[1:1 + D_CONV, :n_cw].set(conv_w[0])
    g0 = _all_gather(blk, "gather_c_convw", in_vmem=True)
    c_all = g0[:, 0, :]
    conv_w_full = g0[:, 1:1 + D_CONV, :n_cw].transpose(1, 0, 2).reshape(D_CONV, CONV_DIM)

    b_shard = lax.dynamic_slice(b_ada, (0, me * n_ada), (1, n_ada))
    mod_cols = _ada_fwd(c_all, w_ada[0], b_shard)
    g1 = _all_gather(mod_cols, "gather_mod", in_vmem=True)
    mod_mine = lax.dynamic_index_in_dim(g1, me, axis=1, keepdims=False)
    mod = mod_mine.reshape(6, 1, D_MODEL)

    big_shapes = [args[n].shape[1:] for n in BIG]
    late_shard = _pack_big([args[n][0] for n in BIG], bf16)
    wfull = _split_w_in(_all_gather(w_in[0].astype(bf16), "gather_w_in", in_vmem=False))
    recv_in_like = jnp.zeros((N_DEV,) + w_in.shape[1:], bf16)
    recv_late_like = jnp.zeros((N_DEV,) + late_shard.shape, bf16)

    small = {"norm1_w": norm1_w, "norm2_w": norm2_w, "q_norm_w": q_norm_w, "k_norm_w": k_norm_w,
             "conv_w": conv_w_full, "conv_b": conv_b, "A_log": A_log[0], "dt_bias": dt_bias[0], "ssd_D": ssd_D,
             "ssd_norm_w": ssd_norm_w}

    loss, (gx, gmod, gsmall, recv_in, recv_late) = jax.value_and_grad(local_loss, argnums=(0, 1, 2, 3, 4))(
        x[0], mod, small, recv_in_like, recv_late_like, wfull, late_shard, loss_target[0])

    small_list = [gmod, gsmall["norm1_w"], gsmall["norm2_w"], gsmall["q_norm_w"], gsmall["k_norm_w"], gsmall["conv_w"],
                  gsmall["conv_b"], gsmall["A_log"], gsmall["dt_bias"], gsmall["ssd_D"], gsmall["ssd_norm_w"],
                  loss.reshape(1)]
    small_shapes = [a.shape for a in small_list]
    g2 = _all_gather(_pack_small(small_list), "gather_small_grads", in_vmem=True)
    summed = _unpack_small(_sum_over_mesh(g2), small_shapes)
    loss_total = summed[-1][0]
    g_b_ada = summed[0].reshape(1, 6 * D_MODEL)
    g_small = dict(zip(SMALL, summed[1:-1]))
    g_conv_w = lax.dynamic_slice(g_small["conv_w"], (0, me * n_cw), (D_CONV, n_cw))

    dmod_all = g2[:, :6 * D_MODEL // LANE, :].reshape(N_DEV, 6 * D_MODEL)
    dmod_shard = lax.dynamic_slice(dmod_all, (0, me * n_ada), (N_DEV, n_ada))
    ada = _ada_bwd_adamw(c_all, dmod_shard, w_ada[0], m_w_ada[0], v_w_ada[0])

    small_grads = {"b_ada": g_b_ada, "norm1_w": g_small["norm1_w"], "norm2_w": g_small["norm2_w"],
                   "q_norm_w": g_small["q_norm_w"], "k_norm_w": g_small["k_norm_w"], "conv_w": g_conv_w[None],
                   "conv_b": g_small["conv_b"], "A_log": g_small["A_log"][None], "dt_bias": g_small["dt_bias"][None],
                   "ssd_D": g_small["ssd_D"], "ssd_norm_w": g_small["ssd_norm_w"]}
    sm_names = list(small_grads)
    sm_shapes = [args[n].shape for n in sm_names]
    sm = _adamw_small(_pack_small([args[n] for n in sm_names]), _pack_small([small_grads[n] for n in sm_names]),
                      _pack_small([args["m_" + n] for n in sm_names]), _pack_small([args["v_" + n] for n in sm_names]))
    sm_delta, sm_m, sm_v = [dict(zip(sm_names, _unpack_small(t, sm_shapes))) for t in sm]
    small_grads = {n: small_grads[n].reshape(args[n].shape) for n in sm_names}

    w_in_out = _sum_adamw(recv_in, w_in[0], m_w_in[0], v_w_in[0], "sum_adamw_w_in")
    big = _sum_adamw(recv_late, _pack_big([args[n][0] for n in BIG], f32),
                     _pack_big([args["m_" + n][0] for n in BIG], f32),
                     _pack_big([args["v_" + n][0] for n in BIG], f32), "sum_adamw")
    big_g, big_delta, big_m, big_v = [dict(zip(BIG, [t[None] for t in _unpack_big(p, big_shapes)])) for p in big]
    big_g["w_in"], big_delta["w_in"], big_m["w_in"], big_v["w_in"] = [t[None] for t in w_in_out]

    names = ("w_ada", "b_ada", "norm1_w", "norm2_w", "w_in", "q_norm_w", "k_norm_w", "conv_w", "conv_b", "A_log",
             "dt_bias", "ssd_D", "ssd_norm_w", "w_attn_out", "w_ssd_out", "w_o", "w_mlp1", "w_mlp2")
    grads, deltas, new_m, new_v = {}, {}, {}, {}
    for n in names:
        if n == "w_ada":
            grads[n], deltas[n], new_m[n], new_v[n] = [t[None] for t in ada]
        elif n in big_g:
            grads[n], deltas[n], new_m[n], new_v[n] = big_g[n], big_delta[n], big_m[n], big_v[n]
        else:
            grads[n], deltas[n], new_m[n], new_v[n] = small_grads[n], sm_delta[n], sm_m[n], sm_v[n]
    return (loss_total, gx[None], *[grads[n] for n in names], *[deltas[n] for n in names],
            *[new_m[n] for n in names], *[new_v[n] for n in names])
```

```python
import functools
import math

import jax
import jax.numpy as jnp
import numpy as np
from jax import lax
from jax.experimental import pallas as pl
from jax.experimental.pallas import tpu as pltpu

f32 = jnp.float32
bf16 = jnp.bfloat16
HIGHEST = lax.Precision.HIGHEST
MESH = pl.DeviceIdType.MESH

N_DEV = 8
D_MODEL = 1024
GRID_W = 64
N_Q_HEADS = 16
N_KV_HEADS = 4
HEAD_DIM = 64
ROPE_THETA = 10000.0
D_INNER = 2048
SSD_HEAD_DIM = 64
N_SSD_HEADS = 32
N_SSD_GROUPS = 4
D_STATE = 128
D_CONV = 5
CHUNK = 128
D_FF = 4096
EPS = 1e-6
CONV_DIM = D_INNER + 2 * N_SSD_GROUPS * D_STATE
GN = N_SSD_GROUPS * D_STATE
PROJ_NAMES = ("q", "k", "v", "xbc", "z", "dt", "ga", "gs")
PROJ_SIZES = (N_Q_HEADS * HEAD_DIM, N_KV_HEADS * HEAD_DIM, N_KV_HEADS * HEAD_DIM, CONV_DIM, D_INNER,
              2 * N_SSD_HEADS, D_MODEL, D_MODEL)
D_IN_PROJ = sum(PROJ_SIZES)
PROJ_DTYPES = (jnp.bfloat16, jnp.bfloat16, jnp.bfloat16, jnp.float32, jnp.bfloat16, jnp.float32, jnp.bfloat16,
               jnp.bfloat16)
DT_PAD = 128

ADAM_LR, ADAM_B1, ADAM_B2, ADAM_EPS, ADAM_WD, ADAM_STEP = 0.001, 0.9, 0.999, 1e-08, 0.01, 10

V7X_VMEM_LIMIT = 56 * 1024 * 1024
LANE = 128
PACK_COLS = 1024


def _cparams(**kw):
    return pltpu.CompilerParams(vmem_limit_bytes=V7X_VMEM_LIMIT, **kw)


def _pick(dim, prefs):
    for p in prefs:
        if dim % p == 0:
            return p
    return dim


def _my_index():
    return 4 * lax.axis_index("x") + 2 * lax.axis_index("y") + lax.axis_index("c")


COMM_SEMS = [pltpu.SemaphoreType.DMA((7,)), pltpu.SemaphoreType.DMA((7,)), pltpu.SemaphoreType.DMA]


def _gather_phases(x_ref, out_ref, send_sems, recv_sems, local_sem):
    x, y, cc = lax.axis_index("x"), lax.axis_index("y"), lax.axis_index("c")
    me, sibling = (x, y, cc), (x, y, 1 - cc)
    chips = [(1 - x, y), (x, 1 - y), (1 - x, 1 - y)]

    def slot(px, py, pc):
        return out_ref.at[4 * px + 2 * py + pc]

    def copy(k, blk, to, src=None):
        return pltpu.make_async_remote_copy(
            src_ref=slot(*blk) if src is None else src, dst_ref=slot(*blk),
            send_sem=send_sems.at[k], recv_sem=recv_sems.at[k], device_id=to, device_id_type=MESH)

    mine = pltpu.make_async_copy(x_ref, slot(*me), local_sem)
    first = [copy(0, me, sibling, src=x_ref)]
    first += [copy(1 + j, me, (*chip, cc), src=x_ref) for j, chip in enumerate(chips)]
    passed = [copy(4 + j, (*chip, cc), sibling) for j, chip in enumerate(chips)]

    def start():
        mine.start()
        for cp in first:
            cp.start()

    def finish():
        for j, chip in enumerate(chips):
            copy(1 + j, (*chip, cc), me).wait_recv()
            passed[j].start()
        copy(0, sibling, me).wait_recv()
        for j, chip in enumerate(chips):
            copy(4 + j, (*chip, 1 - cc), me).wait_recv()
        for cp in first + passed:
            cp.wait_send()
        mine.wait()

    return start, finish


def _scatter_phases(g_ref, out_ref, send_sems, recv_sems, local_sem):
    x, y, cc = lax.axis_index("x"), lax.axis_index("y"), lax.axis_index("c")
    me = 4 * x + 2 * y + cc
    mine = pltpu.make_async_copy(g_ref.at[me], out_ref.at[me], local_sem)

    def copy(k):
        fx, fy, fc = (k >> 2) & 1, (k >> 1) & 1, k & 1
        px = x + fx - 2 * x * fx
        py = y + fy - 2 * y * fy
        pc = cc + fc - 2 * cc * fc
        peer = 4 * px + 2 * py + pc
        send = pltpu.make_async_remote_copy(
            src_ref=g_ref.at[peer], dst_ref=out_ref.at[me],
            send_sem=send_sems.at[k - 1], recv_sem=recv_sems.at[k - 1],
            device_id=(px, py, pc), device_id_type=MESH)
        recv = pltpu.make_async_remote_copy(
            src_ref=g_ref.at[peer], dst_ref=out_ref.at[peer],
            send_sem=send_sems.at[k - 1], recv_sem=recv_sems.at[k - 1],
            device_id=(px, py, pc), device_id_type=MESH)
        return send, recv

    pairs = [copy(k) for k in range(1, N_DEV)]

    def start():
        mine.start()
        for send, _ in pairs:
            send.start()

    def finish():
        for _, recv in pairs:
            recv.wait_recv()
        for send, _ in pairs:
            send.wait_send()
        mine.wait()

    return start, finish


def _all_gather(block, name, in_vmem):
    r, c = block.shape

    def body(x_ref, out_ref, send_sems, recv_sems, local_sem):
        start, finish = _gather_phases(x_ref, out_ref, send_sems, recv_sems, local_sem)
        start()
        finish()

    space = pltpu.VMEM if in_vmem else pl.ANY
    return pl.pallas_call(
        body, name=name,
        out_shape=jax.ShapeDtypeStruct((N_DEV, r, c), block.dtype),
        in_specs=[pl.BlockSpec(memory_space=space)],
        out_specs=pl.BlockSpec(memory_space=space),
        scratch_shapes=[pltpu.SemaphoreType.DMA((7,)), pltpu.SemaphoreType.DMA((7,)), pltpu.SemaphoreType.DMA],
    )(block)


def _scatter_blocks(g, name):
    _, r, c = g.shape

    def body(g_ref, out_ref, send_sems, recv_sems, local_sem):
        start, finish = _scatter_phases(g_ref, out_ref, send_sems, recv_sems, local_sem)
        start()
        finish()

    return pl.pallas_call(
        body, name=name,
        out_shape=jax.ShapeDtypeStruct(g.shape, g.dtype),
        in_specs=[pl.BlockSpec(memory_space=pl.ANY)],
        out_specs=pl.BlockSpec(memory_space=pl.ANY),
        scratch_shapes=[pltpu.SemaphoreType.DMA((7,)), pltpu.SemaphoreType.DMA((7,)), pltpu.SemaphoreType.DMA],
    )(g)


_DIMS = {"nn": (((1,), (0,)), ((), ())), "nt": (((1,), (1,)), ((), ())), "tn": (((0,), (0,)), ((), ()))}


def _matmul(a, b, mode, out_dtype, name, epilogue=None, side=None):
    if mode == "nn":
        (m, k), (_, n) = a.shape, b.shape
    elif mode == "nt":
        (m, k), (n, _) = a.shape, b.shape
    else:
        (k, m), (_, n) = a.shape, b.shape
    tm = _pick(m, (1024, 512, 256, 128))
    if mode == "tn":
        tn = _pick(n, (1536, 1024, 512, 256, 128))
        tk = _pick(k, (1024, 512, 256, 128))
    else:
        tn = _pick(n, (1024, 512, 384, 256, 128))
        tk = _pick(k, (2048, 1024, 512, 256, 128)) if a.dtype == bf16 else _pick(k, (1024, 512, 256, 128))
    nk = k // tk
    dims = _DIMS[mode]
    n_in = 3 if epilogue == "drelu2" else 2
    n_out = 2 if epilogue == "relu2" else 1

    def body(*refs):
        a_ref, b_ref = refs[:2]
        outs, acc_ref = refs[n_in:n_in + n_out], refs[n_in + n_out]
        kk = pl.program_id(2)
        part = lax.dot_general(a_ref[...].astype(bf16), b_ref[...].astype(bf16), dims, preferred_element_type=f32)

        def finish(acc):
            if epilogue == "relu2":
                r = jnp.maximum(acc, 0.0)
                outs[0][...] = acc.astype(out_dtype)
                outs[1][...] = (r * r).astype(out_dtype)
            elif epilogue == "drelu2":
                outs[0][...] = (acc * (2.0 * jnp.maximum(refs[2][...].astype(f32), 0.0))).astype(out_dtype)
            else:
                outs[0][...] = acc.astype(out_dtype)

        if nk == 1:
            finish(part)
        else:
            @pl.when(kk == 0)
            def _():
                acc_ref[...] = part

            @pl.when(kk > 0)
            def _():
                acc_ref[...] += part

            @pl.when(kk == nk - 1)
            def _():
                finish(acc_ref[...])

    if mode == "tn":
        a_spec = pl.BlockSpec((tk, tm), lambda i, j, kk: (kk, i))
    else:
        a_spec = pl.BlockSpec((tm, tk), lambda i, j, kk: (i, kk))
    if mode == "nt":
        b_spec = pl.BlockSpec((tn, tk), lambda i, j, kk: (j, kk))
    else:
        b_spec = pl.BlockSpec((tk, tn), lambda i, j, kk: (kk, j))
    o_spec = pl.BlockSpec((tm, tn), lambda i, j, kk: (i, j))
    o_shape = jax.ShapeDtypeStruct((m, n), out_dtype)
    res = pl.pallas_call(
        body, name=name, grid=(m // tm, n // tn, nk),
        in_specs=[a_spec, b_spec] + ([o_spec] if epilogue == "drelu2" else []),
        out_specs=[o_spec] * n_out, out_shape=[o_shape] * n_out,
        scratch_shapes=[pltpu.VMEM((tm, tn), f32)],
        compiler_params=_cparams(dimension_semantics=("parallel", "parallel", "arbitrary")),
    )(*((a, b, side) if epilogue == "drelu2" else (a, b)))
    return res if n_out == 2 else res[0]


@jax.custom_vjp
def mlp(h, w1, w1grad, w2, w2grad):
    _, r = _matmul(h, w1, "nn", bf16, "mlp1_fwd", epilogue="relu2")
    return _matmul(r, w2, "nn", f32, "mlp2_fwd")


def _mlp_fwd(h, w1, w1grad, w2, w2grad):
    u, r = _matmul(h, w1, "nn", bf16, "mlp1_fwd", epilogue="relu2")
    return _matmul(r, w2, "nn", f32, "mlp2_fwd"), (h, w1, w2, u, r)


def _mlp_bwd(res, dy):
    h, w1, w2, u, r = res
    du = _matmul(dy, w2, "nt", bf16, "mlp2_dgrad", epilogue="drelu2", side=u)
    dw2 = _matmul(r, dy, "tn", f32, "mlp2_wgrad")
    dh = _matmul(du, w1, "nt", h.dtype, "mlp1_dgrad")
    dw1 = _matmul(h, du, "tn", f32, "mlp1_wgrad")
    return dh, jnp.zeros_like(w1), dw1, jnp.zeros_like(w2), dw2


mlp.defvjp(_mlp_fwd, _mlp_bwd)


def make_linear(name):
    @jax.custom_vjp
    def linear(a, w, wgrad):
        return _matmul(a, w, "nn", f32, name + "_fwd")

    def fwd(a, w, wgrad):
        return linear(a, w, wgrad), (a, w)

    def bwd(res, dy):
        a, w = res
        da = _matmul(dy, w, "nt", a.dtype, name + "_dgrad")
        dw = _matmul(a, dy, "tn", f32, name + "_wgrad")
        return da, jnp.zeros_like(w), dw

    linear.defvjp(fwd, bwd)
    return linear


def _in_proj_dgrad(dys, ws, g):
    s, d = dys[0].shape[0], ws[0].shape[0]
    tm = _pick(s, (512, 256, 128))
    tks = [min(w.shape[1], 1024) for w in ws]
    steps = [w.shape[1] // tk for w, tk in zip(ws, tks)]
    starts = [sum(steps[:p]) for p in range(len(ws))]
    total = sum(steps)
    n_p, n_i = len(ws), s // tm
    assert steps[0] == 1

    def body(*refs):
        dy_refs, w_refs, g_ref = refs[:n_p], refs[n_p:2 * n_p], refs[2 * n_p]
        dh_ref, recv_ref, acc_ref, send_sems, recv_sems, local_sem = refs[2 * n_p + 1:]
        i, t = pl.program_id(0), pl.program_id(1)
        start, finish = _scatter_phases(g_ref, recv_ref, send_sems, recv_sems, local_sem)

        @pl.when((i == 0) & (t == 0))
        def _():
            start()

        for p in range(n_p):
            @pl.when((t >= starts[p]) & (t < starts[p] + steps[p]))
            def _(p=p):
                part = lax.dot_general(dy_refs[p][...].astype(bf16), w_refs[p][...], _DIMS["nt"],
                                       preferred_element_type=f32)
                if p == 0:
                    acc_ref[...] = part
                else:
                    acc_ref[...] += part

        @pl.when(t == total - 1)
        def _():
            dh_ref[...] = acc_ref[...].astype(dh_ref.dtype)

        @pl.when((i == n_i - 1) & (t == total - 1))
        def _():
            finish()

    def piece_map(p, rows):
        def index_map(i, t):
            blk = jnp.clip(t - starts[p], 0, steps[p] - 1)
            return (i, blk) if rows else (0, blk)

        return index_map

    hbm = pl.BlockSpec(memory_space=pl.ANY)
    in_specs = [pl.BlockSpec((tm, tks[p]), piece_map(p, True)) for p in range(n_p)]
    in_specs += [pl.BlockSpec((d, tks[p]), piece_map(p, False)) for p in range(n_p)]
    return pl.pallas_call(
        body, name="in_proj_dgrad", grid=(n_i, total), in_specs=in_specs + [hbm],
        out_specs=[pl.BlockSpec((tm, d), lambda i, t: (i, 0)), hbm],
        out_shape=[jax.ShapeDtypeStruct((s, d), bf16), jax.ShapeDtypeStruct(g.shape, g.dtype)],
        scratch_shapes=[pltpu.VMEM((tm, d), f32)] + COMM_SEMS,
        compiler_params=_cparams(dimension_semantics=("arbitrary", "arbitrary")),
    )(*dys, *ws, g)


@jax.custom_vjp
def in_proj(h, ws, recv_like):
    return tuple(_matmul(h, w, "nn", dt, "lin_" + n + "_fwd") for n, w, dt in zip(PROJ_NAMES, ws, PROJ_DTYPES))


def _in_proj_fwd(h, ws, recv_like):
    return in_proj(h, ws, recv_like), (h, ws)


def _in_proj_bwd(res, dys):
    h, ws = res
    dws = {n: _matmul(h, dy, "tn", f32, "lin_" + n + "_wgrad") for n, dy in zip(PROJ_NAMES, dys)}
    dh, recv = _in_proj_dgrad(dys, ws, _pack_in_grads(dws))
    return dh.astype(h.dtype), tuple(jnp.zeros_like(w) for w in ws), recv


in_proj.defvjp(_in_proj_fwd, _in_proj_bwd)


def make_rowwise(name, fn, row_out, sum_out=(), tm_pref=256):
    def specs(rows, gpars, cpars, consts, tm):
        s = [pl.BlockSpec((tm, r.shape[1]), lambda i: (i, 0)) for r in rows]
        s += [pl.BlockSpec(p.shape, lambda i: (0, 0)) for p in gpars]
        s += [pl.BlockSpec(p.shape, lambda i: (0, 0)) for p in cpars]
        for cst in consts:
            nb = cst.shape[0] // tm
            s.append(pl.BlockSpec((tm, cst.shape[1]), lambda i, nb=nb: (i % nb, 0)))
        return s

    def tile_rows(rows, consts):
        r = rows[0].shape[0]
        common = math.gcd(r, *[cst.shape[0] for cst in consts])
        tm = _pick(common, (tm_pref, 512, 256, 128, 64, 32, 16, 8))
        return r, tm

    def forward(rows, gpars, cpars, consts):
        r, tm = tile_rows(rows, consts)
        nr, ng, nc, nk = len(rows), len(gpars), len(cpars), len(consts)

        def body(*refs):
            ins = refs[:nr + ng + nc + nk]
            outs = refs[nr + ng + nc + nk:]
            rv = [t[...].astype(f32) for t in ins[:nr]]
            gv = [t[...].astype(f32) for t in ins[nr:nr + ng]]
            cv = [t[...] for t in ins[nr + ng:nr + ng + nc]]
            kv = [t[...].astype(f32) for t in ins[nr + ng + nc:]]
            ro, so = fn(rv, gv, cv, kv)
            for o_ref, val in zip(outs[:len(row_out)], ro):
                o_ref[...] = val.astype(o_ref.dtype)
            if sum_out:
                @pl.when(pl.program_id(0) == 0)
                def _():
                    for o_ref in outs[len(row_out):]:
                        o_ref[...] = jnp.zeros_like(o_ref)
                for o_ref, val in zip(outs[len(row_out):], so):
                    o_ref[...] += val

        out_specs = [pl.BlockSpec((tm, w), lambda i: (i, 0)) for w, _ in row_out]
        out_specs += [pl.BlockSpec(shp, lambda i: (0, 0)) for shp in sum_out]
        out_shape = [jax.ShapeDtypeStruct((r, w), dt) for w, dt in row_out]
        out_shape += [jax.ShapeDtypeStruct(shp, f32) for shp in sum_out]
        res = pl.pallas_call(
            body, name=name + "_fwd", grid=(r // tm,),
            in_specs=specs(rows, gpars, cpars, consts, tm), out_specs=out_specs, out_shape=out_shape,
            compiler_params=_cparams(dimension_semantics=("arbitrary",)),
        )(*rows, *gpars, *cpars, *consts)
        return tuple(res[:len(row_out)]), tuple(res[len(row_out):])

    def backward(rows, gpars, cpars, consts, d_ro, d_so):
        r, tm = tile_rows(rows, consts)
        nr, ng, nc, nk = len(rows), len(gpars), len(cpars), len(consts)
        n_in = nr + ng + nc + nk + len(row_out) + len(sum_out)

        def body(*refs):
            ins, outs = refs[:n_in], refs[n_in:]
            rv = [t[...].astype(f32) for t in ins[:nr]]
            gv = [t[...].astype(f32) for t in ins[nr:nr + ng]]
            cv = [t[...] for t in ins[nr + ng:nr + ng + nc]]
            kv = [t[...].astype(f32) for t in ins[nr + ng + nc:nr + ng + nc + nk]]
            o = nr + ng + nc + nk
            dro = [t[...].astype(f32) for t in ins[o:o + len(row_out)]]
            dso = [t[...] for t in ins[o + len(row_out):]]
            _, vjp = jax.vjp(lambda a, b: tuple(tuple(t) for t in fn(a, b, cv, kv)), rv, gv)
            drv, dgv = vjp((tuple(dro), tuple(dso)))
            for o_ref, val in zip(outs[:nr], drv):
                o_ref[...] = val.astype(o_ref.dtype)
            if ng:
                @pl.when(pl.program_id(0) == 0)
                def _():
                    for o_ref in outs[nr:]:
                        o_ref[...] = jnp.zeros_like(o_ref)
                for o_ref, val in zip(outs[nr:], dgv):
                    o_ref[...] += val

        in_specs = specs(rows, gpars, cpars, consts, tm)
        in_specs += [pl.BlockSpec((tm, w), lambda i: (i, 0)) for w, _ in row_out]
        in_specs += [pl.BlockSpec(shp, lambda i: (0, 0)) for shp in sum_out]
        out_specs = [pl.BlockSpec((tm, t.shape[1]), lambda i: (i, 0)) for t in rows]
        out_specs += [pl.BlockSpec(p.shape, lambda i: (0, 0)) for p in gpars]
        out_shape = [jax.ShapeDtypeStruct(t.shape, t.dtype) for t in rows]
        out_shape += [jax.ShapeDtypeStruct(p.shape, f32) for p in gpars]
        res = pl.pallas_call(
            body, name=name + "_bwd", grid=(r // tm,),
            in_specs=in_specs, out_specs=out_specs, out_shape=out_shape,
            compiler_params=_cparams(dimension_semantics=("arbitrary",)),
        )(*rows, *gpars, *cpars, *consts, *d_ro, *d_so)
        return tuple(res[:nr]), tuple(res[nr:])

    @jax.custom_vjp
    def op(rows, gpars, cpars, consts):
        return forward(rows, gpars, cpars, consts)

    def op_fwd(rows, gpars, cpars, consts):
        return forward(rows, gpars, cpars, consts), (rows, gpars, cpars, consts)

    def op_bwd(res, cts):
        rows, gpars, cpars, consts = res
        d_ro, d_so = cts
        drows, dg = backward(rows, gpars, cpars, consts, d_ro, d_so)
        dg = tuple(d.astype(p.dtype) for d, p in zip(dg, gpars))
        return (drows, dg, tuple(jnp.zeros_like(p) for p in cpars), tuple(jnp.zeros_like(k) for k in consts))

    op.defvjp(op_fwd, op_bwd)
    return op


def _rms(x):
    return x * lax.rsqrt(jnp.mean(x * x, axis=-1, keepdims=True) + EPS)


def _silu(x):
    return x * jax.nn.sigmoid(x)


def _fn_norm_mod(rows, gp, cp, ks):
    (x,), (nw, sc, sh) = rows, gp
    return ((_rms(x) * nw) * (1.0 + sc) + sh, x), ()


PAIR = 2 * HEAD_DIM


def _exact_dot(a, m):
    hi = a.astype(bf16)
    lo = (a - hi.astype(f32)).astype(bf16)
    return jnp.dot(hi, m, preferred_element_type=f32) + jnp.dot(lo, m, preferred_element_type=f32)


def _make_sel_dot(sign):
    @jax.custom_vjp
    def sel_dot(a, m):
        return _exact_dot(a, m)

    def fwd(a, m):
        return _exact_dot(a, m), m

    def bwd(m, g):
        return sign * _exact_dot(g, m), jnp.zeros_like(m)

    sel_dot.defvjp(fwd, bwd)
    return sel_dot


_head_sum_dot = _make_sel_dot(1.0)
_rope_perm_dot = _make_sel_dot(-1.0)


def _pair_norm_rope(t, w2, gsum, perm, cos2, sin2, out_scale):
    ss = _head_sum_dot(t * t, gsum)
    u = t * lax.rsqrt(ss * (1.0 / HEAD_DIM) + EPS) * w2
    return (u * cos2 + _rope_perm_dot(u, perm) * sin2) * out_scale


def _pair_consts():
    eye = np.eye(2, dtype=np.float32)
    gsum = np.kron(eye, np.ones((HEAD_DIM, HEAD_DIM), np.float32))
    return jnp.asarray(gsum, bf16), jnp.asarray(np.kron(eye, _rope_perm()), bf16)


def make_head_rope(name, nh, out_scale, head_major):
    width = nh * HEAD_DIM
    fn = functools.partial(_pair_norm_rope, out_scale=out_scale)

    def out_spec(tm):
        if head_major:
            return pl.BlockSpec((nh, tm, HEAD_DIM), lambda i: (0, i, 0))
        return pl.BlockSpec((tm, width), lambda i: (i, 0))

    def specs(tm):
        def full(shp):
            return pl.BlockSpec(shp, lambda i: (0, 0))

        return [pl.BlockSpec((tm, width), lambda i: (i, 0)), full((1, PAIR)), full((PAIR, PAIR)), full((PAIR, PAIR)),
                pl.BlockSpec((tm, PAIR), lambda i: (i, 0)), pl.BlockSpec((tm, PAIR), lambda i: (i, 0))]

    def forward(t, w2, gsum, perm, cos2, sin2):
        s = t.shape[0]
        tm = _pick(s, (512, 256, 128))

        def body(t_ref, w_ref, g_ref, p_ref, cos_ref, sin_ref, o_ref):
            for b in range(nh // 2):
                val = fn(t_ref[:, b * PAIR:(b + 1) * PAIR].astype(f32), w_ref[...], g_ref[...], p_ref[...], cos_ref[...],
                         sin_ref[...]).astype(o_ref.dtype)
                if head_major:
                    o_ref[2 * b] = val[:, :HEAD_DIM]
                    o_ref[2 * b + 1] = val[:, HEAD_DIM:]
                else:
                    o_ref[:, b * PAIR:(b + 1) * PAIR] = val

        return pl.pallas_call(
            body, name=name + "_fwd", grid=(s // tm,), in_specs=specs(tm), out_specs=out_spec(tm),
            out_shape=jax.ShapeDtypeStruct((nh, s, HEAD_DIM) if head_major else (s, width), bf16),
            compiler_params=_cparams(dimension_semantics=("arbitrary",)),
        )(t, w2, gsum, perm, cos2, sin2)

    def backward(t, w2, gsum, perm, cos2, sin2, dout):
        s = t.shape[0]
        tm = _pick(s, (512, 256, 128))

        def body(t_ref, w_ref, g_ref, p_ref, cos_ref, sin_ref, do_ref, dt_ref, dw_ref, pair_buf):
            @pl.when(pl.program_id(0) == 0)
            def _():
                dw_ref[...] = jnp.zeros_like(dw_ref)

            g_v, p_v, cos_v, sin_v = g_ref[...], p_ref[...], cos_ref[...], sin_ref[...]
            dw = jnp.zeros((1, PAIR), f32)
            for b in range(nh // 2):
                sl = slice(b * PAIR, (b + 1) * PAIR)
                if head_major:
                    pair_buf[:, :HEAD_DIM] = do_ref[2 * b].astype(f32)
                    pair_buf[:, HEAD_DIM:] = do_ref[2 * b + 1].astype(f32)
                    ct = pair_buf[...]
                else:
                    ct = do_ref[:, sl].astype(f32)
                _, vjp = jax.vjp(lambda a, c: fn(a, c, g_v, p_v, cos_v, sin_v), t_ref[:, sl].astype(f32), w_ref[...])
                dtb, dwb = vjp(ct)
                dt_ref[:, sl] = dtb.astype(dt_ref.dtype)
                dw = dw + dwb
            dw_ref[...] += dw

        return pl.pallas_call(
            body, name=name + "_bwd", grid=(s // tm,), in_specs=specs(tm) + [out_spec(tm)],
            out_specs=[pl.BlockSpec((tm, width), lambda i: (i, 0)), pl.BlockSpec((1, PAIR), lambda i: (0, 0))],
            out_shape=[jax.ShapeDtypeStruct((s, width), t.dtype), jax.ShapeDtypeStruct((1, PAIR), f32)],
            scratch_shapes=[pltpu.VMEM((tm, PAIR), f32)],
            compiler_params=_cparams(dimension_semantics=("arbitrary",)),
        )(t, w2, gsum, perm, cos2, sin2, dout)

    @jax.custom_vjp
    def op(t, w2, gsum, perm, cos2, sin2):
        return forward(t, w2, gsum, perm, cos2, sin2)

    def op_fwd(*args):
        return forward(*args), args

    def op_bwd(res, dout):
        dt, dw = backward(*res, dout)
        return (dt, dw) + tuple(jnp.zeros_like(r) for r in res[2:])

    op.defvjp(op_fwd, op_bwd)

    def apply(t, w, cos2, sin2):
        gsum, perm = _pair_consts()
        return op(t, jnp.concatenate([w, w], axis=-1), gsum, perm, cos2, sin2)

    return apply


def _fn_softplus(rows, gp, cp, ks):
    (x,), (b,) = rows, gp
    v = x + b
    return (jnp.maximum(v, 0.0) + jnp.log(1.0 + jnp.exp(-jnp.abs(v))),), ()


def _fn_ssd_gate(rows, gp, cp, ks):
    (y, z), (nw,) = rows, gp
    return (_rms(y * _silu(z)) * nw,), ()


def _fn_merge(rows, gp, cp, ks):
    ao, so, ga, gs = rows
    return (jax.nn.sigmoid(ga) * ao + jax.nn.sigmoid(gs) * so,), ()


def _fn_res_norm(rows, gp, cp, ks):
    (x, mo), (g1, nw, sc, sh) = rows, gp
    x1 = x + g1 * mo
    return (x1, (_rms(x1) * nw) * (1.0 + sc) + sh), ()


def _fn_loss(rows, gp, cp, ks):
    (x1, ff), (g2,), (tgt,) = rows, gp, ks
    err = x1 + g2 * ff - tgt
    return (), (0.5 * jnp.sum(jnp.sum(err * err, axis=-1, keepdims=True), axis=0, keepdims=True) / D_MODEL,)


HALO = 8
HALO_BWD = 16


def _conv_tiles(s, c):
    return _pick(s, (512, 256, 128)), _pick(c, (512, 256, 128))


def _halo_specs(tm, tc, s, halo=HALO):
    nb = tm // halo
    last = s // halo - 1
    cur = pl.BlockSpec((tm, tc), lambda j, i: (i, j))
    prev = pl.BlockSpec((halo, tc), lambda j, i: (jnp.maximum(i * nb - 1, 0), j))
    nxt = pl.BlockSpec((halo, tc), lambda j, i: (jnp.minimum((i + 1) * nb, last), j))
    return cur, prev, nxt


def _fill_halo(buf, cur, prev, nxt, tm, i, n_i, halo=HALO):
    buf[halo:halo + tm, :] = cur[...]
    buf[0:halo, :] = jnp.where(i > 0, prev[...], 0.0)
    buf[halo + tm:, :] = jnp.where(i < n_i - 1, nxt[...], 0.0)


def _conv_fwd(x, w, b, shard):
    s, c = x.shape
    tm, tc = _conv_tiles(s, c)
    n_i, n_j = s // tm, c // tc

    def body(cur, prev, nxt, w_ref, b_ref, shard_ref, o_ref, gath_ref, buf, send_sems, recv_sems, local_sem):
        j, i = pl.program_id(0), pl.program_id(1)
        start, finish = _gather_phases(shard_ref, gath_ref, send_sems, recv_sems, local_sem)

        @pl.when((j == 0) & (i == 0))
        def _():
            start()

        _fill_halo(buf, cur, prev, nxt, tm, i, n_i)
        pre = jnp.zeros((tm, tc), f32) + b_ref[...]
        for k in range(D_CONV):
            pre = pre + buf[HALO - 2 + k:HALO - 2 + k + tm, :] * w_ref[k:k + 1, :]
        o_ref[...] = _silu(pre)

        @pl.when((j == n_j - 1) & (i == n_i - 1))
        def _():
            finish()

    cur, prev, nxt = _halo_specs(tm, tc, s)
    hbm = pl.BlockSpec(memory_space=pl.ANY)
    return pl.pallas_call(
        body, name="conv_silu_fwd", grid=(n_j, n_i),
        in_specs=[cur, prev, nxt, pl.BlockSpec((D_CONV, tc), lambda j, i: (0, j)),
                  pl.BlockSpec((1, tc), lambda j, i: (0, j)), hbm],
        out_specs=[pl.BlockSpec((tm, tc), lambda j, i: (i, j)), hbm],
        out_shape=[jax.ShapeDtypeStruct((s, c), f32), jax.ShapeDtypeStruct((N_DEV,) + shard.shape, shard.dtype)],
        scratch_shapes=[pltpu.VMEM((tm + 2 * HALO, tc), f32)] + COMM_SEMS,
        compiler_params=_cparams(dimension_semantics=("arbitrary", "arbitrary")),
    )(x, x, x, w, b, shard)


def _conv_bwd(x, w, b, dy, g):
    s, c = x.shape
    tm, tc = _conv_tiles(s, c)
    n_i, n_j = s // tm, c // tc
    ext = tm + 16

    def body(cur, prev, nxt, dcur, dprev, dnxt, w_ref, b_ref, g_ref, dx_ref, dw_ref, db_ref, recv_ref,
             xbuf, dbuf, pbuf, send_sems, recv_sems, local_sem):
        j, i = pl.program_id(0), pl.program_id(1)
        start, finish = _scatter_phases(g_ref, recv_ref, send_sems, recv_sems, local_sem)

        @pl.when((j == 0) & (i == 0))
        def _():
            start()

        _fill_halo(xbuf, cur, prev, nxt, tm, i, n_i, HALO_BWD)
        _fill_halo(dbuf, dcur, dprev, dnxt, tm, i, n_i, HALO_BWD)
        xs = [xbuf[6 + k:6 + k + ext, :] for k in range(D_CONV)]
        pre = jnp.zeros((ext, tc), f32) + b_ref[...]
        for k in range(D_CONV):
            pre = pre + xs[k] * w_ref[k:k + 1, :]
        sg = jax.nn.sigmoid(pre)
        pbuf[...] = dbuf[8:8 + ext, :] * (sg * (1.0 + pre * (1.0 - sg)))
        dx = jnp.zeros((tm, tc), f32)
        for k in range(D_CONV):
            dx = dx + pbuf[10 - k:10 - k + tm, :] * w_ref[k:k + 1, :]
        dx_ref[...] = dx

        @pl.when(i == 0)
        def _():
            dw_ref[...] = jnp.zeros_like(dw_ref)
            db_ref[...] = jnp.zeros_like(db_ref)

        dpre = pbuf[8:8 + tm, :]
        db_ref[...] += jnp.sum(dpre, axis=0, keepdims=True)
        for k in range(D_CONV):
            dw_ref[k:k + 1, :] += jnp.sum(dpre * xs[k][8:8 + tm, :], axis=0, keepdims=True)

        @pl.when((j == n_j - 1) & (i == n_i - 1))
        def _():
            finish()

    cur, prev, nxt = _halo_specs(tm, tc, s, HALO_BWD)
    hbm = pl.BlockSpec(memory_space=pl.ANY)
    return pl.pallas_call(
        body, name="conv_silu_bwd", grid=(n_j, n_i),
        in_specs=[cur, prev, nxt, cur, prev, nxt, pl.BlockSpec((D_CONV, tc), lambda j, i: (0, j)),
                  pl.BlockSpec((1, tc), lambda j, i: (0, j)), hbm],
        out_specs=[pl.BlockSpec((tm, tc), lambda j, i: (i, j)), pl.BlockSpec((D_CONV, tc), lambda j, i: (0, j)),
                   pl.BlockSpec((1, tc), lambda j, i: (0, j)), hbm],
        out_shape=[jax.ShapeDtypeStruct((s, c), f32), jax.ShapeDtypeStruct((D_CONV, c), f32),
                   jax.ShapeDtypeStruct((1, c), f32), jax.ShapeDtypeStruct(g.shape, g.dtype)],
        scratch_shapes=[pltpu.VMEM((tm + 2 * HALO_BWD, tc), f32), pltpu.VMEM((tm + 2 * HALO_BWD, tc), f32),
                        pltpu.VMEM((ext, tc), f32)] + COMM_SEMS,
        compiler_params=_cparams(dimension_semantics=("arbitrary", "arbitrary")),
    )(x, x, x, dy, dy, dy, w, b, g)


@jax.custom_vjp
def conv_silu_comm(x, w, b, shard, recv_like):
    act, gathered = _conv_fwd(x, w, b, shard)
    return (act, gathered) + tuple(jnp.zeros(shp, f32) for shp in LATE_SHAPES)


def _conv_silu_comm_fwd(x, w, b, shard, recv_like):
    return conv_silu_comm(x, w, b, shard, recv_like), (x, w, b, shard)


def _conv_silu_comm_bwd(res, cts):
    x, w, b, shard = res
    dx, dw, db, recv = _conv_bwd(x, w, b, cts[0], _pack_late_grads(dict(zip(LATE, cts[2:]))))
    return dx, dw, db, jnp.zeros_like(shard), recv


conv_silu_comm.defvjp(_conv_silu_comm_fwd, _conv_silu_comm_bwd)


ATT_SCALE = HEAD_DIM ** -0.5
Q_SCALE = ATT_SCALE * math.log2(math.e)
LN2 = math.log(2.0)
REP = N_Q_HEADS // N_KV_HEADS


HP = 2
assert REP % HP == 0


def _attn_fwd(q, k, v):
    s, dh = q.shape[0], HEAD_DIM
    hq = q.shape[1] // dh
    tq = _pick(s, (256, 128))

    v1 = jnp.concatenate([v, jnp.ones(v.shape[:2] + (1,), v.dtype), jnp.zeros(v.shape[:2] + (dh - 1,), v.dtype)],
                         axis=-1)

    def body(q_ref, k_ref, v_ref, o_ref, p_ref, linv_ref):
        for j in range(HP):
            sl = slice(j * dh, (j + 1) * dh)
            sc = lax.dot_general(q_ref[:, sl], k_ref[0], _DIMS["nt"], preferred_element_type=f32)
            m = jnp.max(sc, axis=-1, keepdims=True)
            p = jnp.exp2(sc - m).astype(bf16)
            p_ref[j] = p
            o1 = jnp.dot(p, v_ref[0], preferred_element_type=f32)
            linv = 1.0 / o1[:, dh:dh + 1]
            o_ref[:, sl] = (o1[:, :dh] * linv).astype(o_ref.dtype)
            linv_ref[j] = linv

    return pl.pallas_call(
        body, name="attn_fwd", grid=(hq // HP, s // tq),
        in_specs=[pl.BlockSpec((tq, HP * dh), lambda h, i: (i, h)),
                  pl.BlockSpec((1, s, dh), lambda h, i: (h * HP // REP, 0, 0)),
                  pl.BlockSpec((1, s, 2 * dh), lambda h, i: (h * HP // REP, 0, 0))],
        out_specs=[pl.BlockSpec((tq, HP * dh), lambda h, i: (i, h)),
                   pl.BlockSpec((HP, tq, s), lambda h, i: (h, i, 0)),
                   pl.BlockSpec((HP, tq, 1), lambda h, i: (h, i, 0))],
        out_shape=[jax.ShapeDtypeStruct((s, hq * dh), bf16), jax.ShapeDtypeStruct((hq, s, s), bf16),
                   jax.ShapeDtypeStruct((hq, s, 1), f32)],
        compiler_params=_cparams(dimension_semantics=("parallel", "arbitrary")),
    )(q, k, v1)


def _attn_bwd(p, do, o, q, k, v, linv):
    hq, s, _ = p.shape
    dh = HEAD_DIM
    tq = _pick(s, (256, 128))

    def body(p_ref, do_ref, o_ref, q_ref, k_ref, v_ref, linv_ref, dq_ref, dkt_ref, dvt_ref):
        @pl.when(pl.program_id(1) == 0)
        def _():
            dkt_ref[...] = jnp.zeros_like(dkt_ref)
            dvt_ref[...] = jnp.zeros_like(dvt_ref)

        for j in range(HP):
            sl = slice(j * dh, (j + 1) * dh)
            pp, doh, li = p_ref[j], do_ref[:, sl], linv_ref[j]
            do32 = doh.astype(f32)
            d = jnp.sum(do32 * o_ref[:, sl].astype(f32), axis=-1, keepdims=True)
            dp = lax.dot_general(doh, v_ref[0], _DIMS["nt"], preferred_element_type=f32)
            ds = (pp.astype(f32) * ((dp - d) * li)).astype(bf16)
            dq_ref[:, sl] = (jnp.dot(ds, k_ref[0], preferred_element_type=f32) * LN2).astype(dq_ref.dtype)
            dvt_ref[j] += lax.dot_general((do32 * li).astype(bf16), pp, _DIMS["tn"], preferred_element_type=f32)
            dkt_ref[j] += lax.dot_general(q_ref[:, sl], ds, _DIMS["tn"], preferred_element_type=f32)

    def row():
        return pl.BlockSpec((tq, HP * dh), lambda h, i: (i, h))

    return pl.pallas_call(
        body, name="attn_bwd", grid=(hq // HP, s // tq),
        in_specs=[pl.BlockSpec((HP, tq, s), lambda h, i: (h, i, 0)), row(), row(), row(),
                  pl.BlockSpec((1, s, dh), lambda h, i: (h * HP // REP, 0, 0)),
                  pl.BlockSpec((1, s, dh), lambda h, i: (h * HP // REP, 0, 0)),
                  pl.BlockSpec((HP, tq, 1), lambda h, i: (h, i, 0))],
        out_specs=[row(), pl.BlockSpec((HP, dh, s), lambda h, i: (h, 0, 0)),
                   pl.BlockSpec((HP, dh, s), lambda h, i: (h, 0, 0))],
        out_shape=[jax.ShapeDtypeStruct((s, hq * dh), q.dtype), jax.ShapeDtypeStruct((hq, dh, s), f32),
                   jax.ShapeDtypeStruct((hq, dh, s), f32)],
        compiler_params=_cparams(dimension_semantics=("parallel", "arbitrary")),
    )(p, do, o, q, k, v, linv)


@jax.custom_vjp
def attention(q, k, v):
    return _attn_fwd(q, k, v)[0]


def _attention_fwd(q, k, v):
    o, p, linv = _attn_fwd(q, k, v)
    return o, (q, k, v, o, p, linv)


def _attention_bwd(res, do):
    q, k, v, o, p, linv = res
    s = q.shape[0]
    dq, dkt, dvt = _attn_bwd(p, do.astype(bf16), o, q, k, v, linv)

    def per_kv_head(t):
        return jnp.swapaxes(t.reshape(N_KV_HEADS, REP, HEAD_DIM, s).sum(axis=1), 1, 2)

    return dq, (per_kv_head(dkt) * LN2).astype(k.dtype), per_kv_head(dvt).astype(v.dtype)


attention.defvjp(_attention_fwd, _attention_bwd)


HPG = N_SSD_HEADS // N_SSD_GROUPS
GW = HPG * SSD_HEAD_DIM
NEG = -1e30
SPLIT_ROWS = 32


def _ssd_consts():
    k = np.arange(SPLIT_ROWS)[:, None]
    live = k < 3 * HPG
    sel_chunk = ((k % HPG) == (np.arange(HPG * CHUNK)[None, :] // CHUNK)) & live
    sel_head = ((k % HPG) == (np.arange(GW)[None, :] // SSD_HEAD_DIM)) & live
    return jnp.asarray(sel_chunk, bf16), jnp.asarray(sel_head, bf16)


def _split3(x):
    hi = x.astype(bf16).astype(f32)
    r1 = x - hi
    mid = r1.astype(bf16).astype(f32)
    lo = (r1 - mid).astype(bf16).astype(f32)
    return jnp.concatenate([hi, mid, lo, jnp.zeros_like(hi)], axis=0).astype(bf16)


def _tn(a, b):
    return lax.dot_general(a, b, _DIMS["tn"], preferred_element_type=f32)


def _nt(a, b):
    return lax.dot_general(a, b, _DIMS["nt"], preferred_element_type=f32)


def _nn(a, b):
    return jnp.dot(a, b, preferred_element_type=f32)


def _head_sum(sel8, x):
    hi = x.astype(bf16)
    lo = (x - hi.astype(f32)).astype(bf16)
    return _nt(sel8, hi) + _nt(sel8, lo)


def _ssd_masks(reverse):
    r = lax.broadcasted_iota(jnp.int32, (CHUNK, CHUNK), 0)
    c = lax.broadcasted_iota(jnp.int32, (CHUNK, CHUNK), 1)
    lower, upper = r >= c, r <= c
    return (upper, lower) if reverse else (lower, upper)


def _ssd_in_specs(cidx):
    return [pl.BlockSpec((CHUNK, D_INNER), lambda c: (cidx(c), 0)),
            pl.BlockSpec((CHUNK, GN), lambda c: (cidx(c), D_INNER // GN)),
            pl.BlockSpec((CHUNK, GN), lambda c: (cidx(c), D_INNER // GN + 1)),
            pl.BlockSpec((N_SSD_HEADS, CHUNK), lambda c: (0, cidx(c))),
            pl.BlockSpec((N_SSD_HEADS, 1), lambda c: (0, 0)),
            pl.BlockSpec((SPLIT_ROWS, HPG * CHUNK), lambda c: (0, 0)),
            pl.BlockSpec((SPLIT_ROWS, GW), lambda c: (0, 0))]


def _ssd_chunk_common(dtt_ref, a_ref, et_ref, mask_t):
    dtt = dtt_ref[...]
    et = jnp.dot(dtt * a_ref[...], mask_t.astype(f32), precision=HIGHEST, preferred_element_type=f32)
    et_ref[...] = et
    return dtt, et


def _ssd_group_common(g, dtt, et, selc_ref, selh_ref, xs_ref, b_ref, c_ref, last):
    gr = slice(g * HPG, (g + 1) * HPG)
    e3 = _split3(et[gr])
    col = _tn(e3, selc_ref[...])
    eb = _tn(e3, selh_ref[...])
    dtb = _tn(_split3(dtt[gr]), selh_ref[...])
    tbc = eb[last:last + 1, :]
    xs = xs_ref[:, g * GW:(g + 1) * GW]
    bg = b_ref[:, g * D_STATE:(g + 1) * D_STATE].astype(bf16)
    cg = c_ref[:, g * D_STATE:(g + 1) * D_STATE].astype(bf16)
    return col, eb, dtb, tbc, xs, bg, cg


def _ssd_fwd(xbc, dtt, a_col, reverse, y_prev=None, dexp=None):
    s = xbc.shape[0]
    nc = s // CHUNK
    cidx = (lambda c: nc - 1 - c) if reverse else (lambda c: c)
    last = 0 if reverse else CHUNK - 1
    selc, selh = _ssd_consts()
    final = y_prev is not None
    n_in = 9 if final else 7

    def body(*refs):
        xs_ref, b_ref, c_ref, dtt_ref, a_ref, selc_ref, selh_ref = refs[:7]
        y_ref, st_ref, ht_ref, et_ref = refs[n_in:]

        @pl.when(pl.program_id(0) == 0)
        def _():
            ht_ref[...] = jnp.zeros_like(ht_ref)

        mask, mask_t = _ssd_masks(reverse)
        dtt_v, et = _ssd_chunk_common(dtt_ref, a_ref, et_ref, mask_t)
        for g in range(N_SSD_GROUPS):
            col, eb, dtb, tbc, xs, bg, cg = _ssd_group_common(g, dtt_v, et, selc_ref, selh_ref, xs_ref, b_ref, c_ref,
                                                              last)
            xd = xs * dtb
            cb = _nt(cg, bg)
            ht = ht_ref[g]
            st_ref[0, g] = ht
            yoff = _nn(cg, ht.astype(bf16)) * jnp.exp(eb)
            for j in range(HPG):
                h = g * HPG + j
                hs = slice(j * SSD_HEAD_DIM, (j + 1) * SSD_HEAD_DIM)
                lam = jnp.exp(jnp.where(mask, col[:, j * CHUNK:(j + 1) * CHUNK] - et_ref[h:h + 1, :], NEG))
                yj = _nn((cb * lam).astype(bf16), xd[:, hs].astype(bf16)) + yoff[:, hs]
                cols = slice(g * GW + j * SSD_HEAD_DIM, g * GW + (j + 1) * SSD_HEAD_DIM)
                if final:
                    yj = yj + refs[7][:, cols] + xs[:, hs] * refs[8][:, cols]
                y_ref[:, cols] = yj
            ht_ref[g] = jnp.exp(tbc) * ht + _tn(bg, (xd * jnp.exp(tbc - eb)).astype(bf16))

    y_spec = pl.BlockSpec((CHUNK, D_INNER), lambda c: (cidx(c), 0))
    extra_specs = [y_spec, pl.BlockSpec((1, D_INNER), lambda c: (0, 0))] if final else []
    return pl.pallas_call(
        body, name="ssd_fwd_rev" if reverse else "ssd_fwd", grid=(nc,),
        in_specs=_ssd_in_specs(cidx) + extra_specs,
        out_specs=[y_spec, pl.BlockSpec((1, N_SSD_GROUPS, D_STATE, GW), lambda c: (cidx(c), 0, 0, 0))],
        out_shape=[jax.ShapeDtypeStruct((s, D_INNER), f32),
                   jax.ShapeDtypeStruct((nc, N_SSD_GROUPS, D_STATE, GW), f32)],
        scratch_shapes=[pltpu.VMEM((N_SSD_GROUPS, D_STATE, GW), f32), pltpu.VMEM((N_SSD_HEADS, CHUNK), f32)],
        compiler_params=_cparams(dimension_semantics=("arbitrary",)),
    )(xbc, xbc, xbc, dtt, a_col, selc, selh, *((y_prev, dexp) if final else ()))


def _ssd_bwd(xbc, dtt, a_col, states, dy, reverse, dxbc_prev=None, dexp=None):
    s = xbc.shape[0]
    nc = s // CHUNK
    cidx = (lambda c: c) if reverse else (lambda c: nc - 1 - c)
    last = 0 if reverse else CHUNK - 1
    selc, selh = _ssd_consts()
    final = dxbc_prev is not None
    n_in = 11 if final else 9
    n_out = 4 if final else 3

    def body(*refs):
        xs_ref, b_ref, c_ref, dtt_ref, a_ref, selc_ref, selh_ref, st_ref, dy_ref = refs[:9]
        dxbc_ref, ddtt_ref, da_ref = refs[n_in:n_in + 3]
        dh_ref, et_ref, det_ref, det2_ref, ddt_ref, q_ref = refs[n_in + n_out:]
        if final:
            prev_ref, dexp_ref, ddexp_ref = refs[9], refs[10], refs[n_in + 3]

        @pl.when(pl.program_id(0) == 0)
        def _():
            dh_ref[...] = jnp.zeros_like(dh_ref)
            da_ref[...] = jnp.zeros_like(da_ref)
            if final:
                ddexp_ref[...] = jnp.zeros_like(ddexp_ref)

        mask, mask_t = _ssd_masks(reverse)
        dtt_v, et = _ssd_chunk_common(dtt_ref, a_ref, et_ref, mask_t)
        sel8 = selh_ref[0:HPG, :]
        is_last = lax.broadcasted_iota(jnp.int32, (CHUNK, GW), 0) == last
        for g in range(N_SSD_GROUPS):
            col, eb, dtb, tbc, xs, bg, cg = _ssd_group_common(g, dtt_v, et, selc_ref, selh_ref, xs_ref, b_ref, c_ref,
                                                              last)
            xd = xs * dtb
            cb = _nt(cg, bg)
            cbt = _nt(bg, cg)
            exp_t = jnp.exp(tbc)
            dfac = jnp.exp(tbc - eb)
            ht = st_ref[0, g]
            dhn = dh_ref[g]
            ht16, dhn16 = ht.astype(bf16), dhn.astype(bf16)
            dy = dy_ref[:, g * GW:(g + 1) * GW]
            dye = dy * jnp.exp(eb)
            dye16 = dye.astype(bf16)
            dc = _nt(dye16, ht16)
            dh_ref[g] = exp_t * dhn + _tn(cg, dye16)
            deb = dye * _nn(cg, ht16)
            xdd = xd * dfac
            dxdd = _nn(bg, dhn16)
            db = _nt(xdd.astype(bf16), dhn16)
            dxd_state = dxdd * dfac
            ddf = dxdd * xdd
            dtbc = jnp.sum(ddf, axis=0, keepdims=True) + exp_t * jnp.sum(dhn * ht, axis=0, keepdims=True)
            deb = deb - ddf + jnp.where(is_last, dtbc, 0.0)
            dcb = jnp.zeros((CHUNK, CHUNK), f32)
            dcbt = jnp.zeros((CHUNK, CHUNK), f32)
            for j in range(HPG):
                h = g * HPG + j
                hs = slice(j * SSD_HEAD_DIM, (j + 1) * SSD_HEAD_DIM)
                colj = col[:, j * CHUNK:(j + 1) * CHUNK]
                row = et_ref[h:h + 1, :]
                lam = jnp.exp(jnp.where(mask, colj - row, NEG))
                lam_t = lam.T
                xdj, dyj = xd[:, hs].astype(bf16), dy[:, hs].astype(bf16)
                t1 = _nt(dyj, xdj) * lam
                t2 = _nt(xdj, dyj) * lam_t
                dcb, dcbt = dcb + t1, dcbt + t2
                det_ref[h:h + 1, :] = -jnp.sum(t1 * cb - t2 * cbt, axis=0, keepdims=True)
                dxdj = _nn((cbt * lam_t).astype(bf16), dyj) + dxd_state[:, hs]
                cols = slice(g * GW + j * SSD_HEAD_DIM, g * GW + (j + 1) * SSD_HEAD_DIM)
                dxs = dxdj * dtb[:, hs]
                if final:
                    dxs = dxs + prev_ref[:, cols] + dy[:, hs] * dexp_ref[:, cols]
                dxbc_ref[:, cols] = dxs
                q_ref[:, hs] = dxdj * xs[:, hs]
            b_cols = slice(D_INNER + g * D_STATE, D_INNER + (g + 1) * D_STATE)
            c_cols = slice(D_INNER + GN + g * D_STATE, D_INNER + GN + (g + 1) * D_STATE)
            db = db + _nn(dcbt.astype(bf16), cg)
            dc = dc + _nn(dcb.astype(bf16), bg)
            if final:
                db, dc = db + prev_ref[:, b_cols], dc + prev_ref[:, c_cols]
                ddexp_ref[:, g * GW:(g + 1) * GW] += jnp.sum(dy * xs, axis=0, keepdims=True)
            dxbc_ref[:, b_cols] = db
            dxbc_ref[:, c_cols] = dc
            det2_ref[g * HPG:(g + 1) * HPG, :] = _head_sum(sel8, deb)
            ddt_ref[g * HPG:(g + 1) * HPG, :] = _head_sum(sel8, q_ref[...])
        dat = jnp.dot(det_ref[...] + det2_ref[...], mask.astype(f32), precision=HIGHEST, preferred_element_type=f32)
        ddtt_ref[...] = ddt_ref[...] + dat * a_ref[...]
        da_ref[...] += jnp.sum(dat * dtt_v, axis=1, keepdims=True)

    in_specs = _ssd_in_specs(cidx) + [
        pl.BlockSpec((1, N_SSD_GROUPS, D_STATE, GW), lambda c: (cidx(c), 0, 0, 0)),
        pl.BlockSpec((CHUNK, D_INNER), lambda c: (cidx(c), 0))]
    hl = pltpu.VMEM((N_SSD_HEADS, CHUNK), f32)
    dxbc_spec = pl.BlockSpec((CHUNK, CONV_DIM), lambda c: (cidx(c), 0))
    dexp_spec = pl.BlockSpec((1, D_INNER), lambda c: (0, 0))
    return pl.pallas_call(
        body, name="ssd_bwd_rev" if reverse else "ssd_bwd", grid=(nc,),
        in_specs=in_specs + ([dxbc_spec, dexp_spec] if final else []),
        out_specs=[dxbc_spec, pl.BlockSpec((N_SSD_HEADS, CHUNK), lambda c: (0, cidx(c))),
                   pl.BlockSpec((N_SSD_HEADS, 1), lambda c: (0, 0))] + ([dexp_spec] if final else []),
        out_shape=[jax.ShapeDtypeStruct((s, CONV_DIM), f32), jax.ShapeDtypeStruct((N_SSD_HEADS, s), f32),
                   jax.ShapeDtypeStruct((N_SSD_HEADS, 1), f32)]
        + ([jax.ShapeDtypeStruct((1, D_INNER), f32)] if final else []),
        scratch_shapes=[pltpu.VMEM((N_SSD_GROUPS, D_STATE, GW), f32), hl, hl, hl, hl, pltpu.VMEM((CHUNK, GW), f32)],
        compiler_params=_cparams(dimension_semantics=("arbitrary",)),
    )(xbc, xbc, xbc, dtt, a_col, selc, selh, states, dy, *((dxbc_prev, dexp) if final else ()))


@jax.custom_vjp
def ssd_bidir(xbc, dtt, a_col, dexp):
    y_f, _ = _ssd_fwd(xbc, dtt[:N_SSD_HEADS], a_col[:N_SSD_HEADS], False)
    return _ssd_fwd(xbc, dtt[N_SSD_HEADS:], a_col[N_SSD_HEADS:], True, y_prev=y_f, dexp=dexp)[0]


def _ssd_bidir_fwd(xbc, dtt, a_col, dexp):
    y_f, st_f = _ssd_fwd(xbc, dtt[:N_SSD_HEADS], a_col[:N_SSD_HEADS], False)
    y, st_b = _ssd_fwd(xbc, dtt[N_SSD_HEADS:], a_col[N_SSD_HEADS:], True, y_prev=y_f, dexp=dexp)
    return y, (xbc, dtt, a_col, dexp, st_f, st_b)


def _ssd_bidir_bwd(res, dy):
    xbc, dtt, a_col, dexp, st_f, st_b = res
    dxbc_f, ddtt_f, da_f = _ssd_bwd(xbc, dtt[:N_SSD_HEADS], a_col[:N_SSD_HEADS], st_f, dy, False)
    dxbc, ddtt_b, da_b, ddexp = _ssd_bwd(xbc, dtt[N_SSD_HEADS:], a_col[N_SSD_HEADS:], st_b, dy, True,
                                         dxbc_prev=dxbc_f, dexp=dexp)
    return dxbc, jnp.concatenate([ddtt_f, ddtt_b], axis=0), jnp.concatenate([da_f, da_b], axis=0), ddexp


ssd_bidir.defvjp(_ssd_bidir_fwd, _ssd_bidir_bwd)


W_NAMES = PROJ_NAMES + ("attn_out", "ssd_out", "o", "mlp1", "mlp2")


def _rope_tables(s):
    rows = s // GRID_W
    pos_row = np.repeat(np.arange(rows), GRID_W).astype(np.float32)
    pos_col = np.tile(np.arange(GRID_W), rows).astype(np.float32)
    axis_dim = HEAD_DIM // 2
    inv_freq = np.float32(ROPE_THETA) ** (-np.arange(0, axis_dim, 2, dtype=np.float32) / np.float32(axis_dim))
    ang_r = pos_row[:, None] * inv_freq[None, :].astype(np.float32)
    ang_c = pos_col[:, None] * inv_freq[None, :].astype(np.float32)
    cos = np.concatenate([np.cos(ang_r), np.cos(ang_r), np.cos(ang_c), np.cos(ang_c)] * 2, axis=-1)
    sin = np.concatenate([np.sin(ang_r), np.sin(ang_r), np.sin(ang_c), np.sin(ang_c)] * 2, axis=-1)
    return jnp.asarray(cos, f32), jnp.asarray(sin, f32)


def _rope_perm():
    p = np.zeros((HEAD_DIM, HEAD_DIM), np.float32)
    for j in range(HEAD_DIM):
        if (j % 32) < 16:
            p[j + 16, j] = -1.0
        else:
            p[j - 16, j] = 1.0
    return p


def local_loss(x, mod, small, recv_in_like, recv_late_like, wfull, late_shard, target):
    s = x.shape[0]
    lin = {n: make_linear("lin_" + n) for n in LATE if not n.startswith("mlp")}
    wfull, wgrads = dict(wfull), {}
    shift1, scale1, gate1, shift2, scale2, gate2 = [mod[i] for i in range(6)]

    norm_mod = make_rowwise("norm_mod", _fn_norm_mod, [(D_MODEL, bf16), (D_MODEL, f32)])
    (h, x_res), _ = norm_mod((x,), (small["norm1_w"], scale1, shift1), (), ())

    proj = dict(zip(PROJ_NAMES, in_proj(h, tuple(wfull[n] for n in PROJ_NAMES), recv_in_like)))

    cos, sin = _rope_tables(s)

    def heads(t, nh):
        return t.reshape(s, nh, HEAD_DIM).transpose(1, 0, 2)

    qr = make_head_rope("q_norm_rope", N_Q_HEADS, Q_SCALE, False)(proj["q"], small["q_norm_w"], cos, sin)
    kr = make_head_rope("k_norm_rope", N_KV_HEADS, 1.0, True)(proj["k"], small["k_norm_w"], cos, sin)
    vh = heads(proj["v"], N_KV_HEADS).astype(bf16)
    att = attention(qr, kr, vh)

    xbc, gathered, *carriers = conv_silu_comm(proj["xbc"], small["conv_w"], small["conv_b"], late_shard,
                                              recv_late_like)
    wfull.update(_split_late(gathered))
    wgrads.update(zip(LATE, carriers))
    ao = lin["attn_out"](att, wfull["attn_out"], wgrads["attn_out"])
    softplus = make_rowwise("dt_softplus", _fn_softplus, [(2 * N_SSD_HEADS, f32)])
    (dt,), _ = softplus((proj["dt"][:, :2 * N_SSD_HEADS],), (small["dt_bias"].reshape(1, 2 * N_SSD_HEADS),), (), ())
    a_neg = -jnp.exp(small["A_log"])
    dexp = jnp.repeat(small["ssd_D"].reshape(N_SSD_HEADS), SSD_HEAD_DIM).reshape(1, D_INNER)
    y = ssd_bidir(xbc, dt.T, a_neg.reshape(2 * N_SSD_HEADS, 1), dexp)
    ssd_gate = make_rowwise("ssd_gate", _fn_ssd_gate, [(D_INNER, bf16)], tm_pref=128)
    (ssd_out,), _ = ssd_gate((y, proj["z"]), (small["ssd_norm_w"],), (), ())
    so = lin["ssd_out"](ssd_out, wfull["ssd_out"], wgrads["ssd_out"])

    merge = make_rowwise("merge", _fn_merge, [(D_MODEL, bf16)])
    (merged,), _ = merge((ao, so, proj["ga"], proj["gs"]), (), (), ())
    mo = lin["o"](merged, wfull["o"], wgrads["o"])

    res_norm = make_rowwise("res_norm", _fn_res_norm, [(D_MODEL, f32), (D_MODEL, bf16)])
    (x1, h2), _ = res_norm((x_res, mo), (gate1, small["norm2_w"], scale2, shift2), (), ())
    ff = mlp(h2, wfull["mlp1"], wgrads["mlp1"], wfull["mlp2"], wgrads["mlp2"])
    loss_op = make_rowwise("loss", _fn_loss, [], [(1, 1)])
    _, (loss,) = loss_op((x1, ff), (gate2,), (), (target,))
    return loss[0, 0]


_BC1 = 1.0 - ADAM_B1 ** ADAM_STEP
_BC2 = 1.0 - ADAM_B2 ** ADAM_STEP


def _adamw(w, g, m, v):
    m = ADAM_B1 * m + (1.0 - ADAM_B1) * g
    v = ADAM_B2 * v + (1.0 - ADAM_B2) * (g * g)
    delta = -ADAM_LR * ((m / _BC1) / (jnp.sqrt(v / _BC2) + ADAM_EPS) + ADAM_WD * w)
    return delta, m, v


def _ada_fwd(c_all, w, b):
    n = w.shape[1]

    def body(c_ref, w_ref, b_ref, o_ref):
        o_ref[...] = jnp.dot(_silu(c_ref[...]), w_ref[...], precision=HIGHEST, preferred_element_type=f32) + b_ref[...]

    return pl.pallas_call(body, name="ada_fwd", out_shape=jax.ShapeDtypeStruct((N_DEV, n), f32),
                          compiler_params=_cparams())(c_all, w, b)


def _ada_bwd_adamw(c_all, dmod, w, m, v):
    d, n = w.shape
    tr = _pick(d, (256, 128))

    def body(c_ref, dm_ref, w_ref, m_ref, v_ref, g_ref, dl_ref, mo_ref, vo_ref):
        g = lax.dot_general(_silu(c_ref[...]), dm_ref[...], _DIMS["tn"], precision=HIGHEST,
                            preferred_element_type=f32)
        g_ref[...] = g
        dl_ref[...], mo_ref[...], vo_ref[...] = _adamw(w_ref[...], g, m_ref[...], v_ref[...])

    blk = pl.BlockSpec((tr, n), lambda i: (i, 0))
    return pl.pallas_call(
        body, name="ada_bwd_adamw", grid=(d // tr,),
        in_specs=[pl.BlockSpec((N_DEV, tr), lambda i: (0, i)), pl.BlockSpec((N_DEV, n), lambda i: (0, 0)), blk, blk, blk],
        out_specs=[blk] * 4, out_shape=[jax.ShapeDtypeStruct((d, n), f32)] * 4,
        compiler_params=_cparams(dimension_semantics=("parallel",)),
    )(c_all, dmod, w, m, v)


def _sum_over_mesh(g):
    def body(g_ref, o_ref):
        acc = g_ref[0]
        for d in range(1, N_DEV):
            acc = acc + g_ref[d]
        o_ref[...] = acc

    return pl.pallas_call(body, name="sum_small", out_shape=jax.ShapeDtypeStruct(g.shape[1:], f32),
                          compiler_params=_cparams())(g)


def _adamw_small(w, g, m, v):
    def body(w_ref, g_ref, m_ref, v_ref, dl_ref, mo_ref, vo_ref):
        dl_ref[...], mo_ref[...], vo_ref[...] = _adamw(w_ref[...], g_ref[...], m_ref[...], v_ref[...])

    return pl.pallas_call(body, name="adamw_small", out_shape=[jax.ShapeDtypeStruct(w.shape, f32)] * 3,
                          compiler_params=_cparams())(w, g, m, v)


def _sum_adamw(recv, w, m, v, name):
    _, r, c = recv.shape
    tr = _pick(r, (256, 128, 64, 16))

    def body(g_ref, w_ref, m_ref, v_ref, go_ref, dl_ref, mo_ref, vo_ref):
        g = g_ref[0].astype(f32)
        for d in range(1, N_DEV):
            g = g + g_ref[d].astype(f32)
        go_ref[...] = g
        dl_ref[...], mo_ref[...], vo_ref[...] = _adamw(w_ref[...], g, m_ref[...], v_ref[...])

    blk = pl.BlockSpec((tr, c), lambda i: (i, 0))
    return pl.pallas_call(
        body, name=name, grid=(r // tr,),
        in_specs=[pl.BlockSpec((N_DEV, tr, c), lambda i: (0, i, 0)), blk, blk, blk],
        out_specs=[blk] * 4, out_shape=[jax.ShapeDtypeStruct((r, c), f32)] * 4,
        compiler_params=_cparams(dimension_semantics=("parallel",)),
    )(recv, w, m, v)


def _pack_small(arrs):
    parts = []
    for a in arrs:
        flat = a.reshape(-1).astype(f32)
        parts.append(jnp.pad(flat, (0, (-flat.shape[0]) % LANE)))
    flat = jnp.concatenate(parts)
    flat = jnp.pad(flat, (0, (-flat.shape[0]) % (8 * LANE)))
    return flat.reshape(-1, LANE)


def _unpack_small(packed, shapes):
    flat = packed.reshape(-1)
    out, off = [], 0
    for shp in shapes:
        n = int(np.prod(shp))
        out.append(flat[off:off + n].reshape(shp))
        off += n + (-n) % LANE
    return out


BIG = ("w_attn_out", "w_ssd_out", "w_o", "w_mlp1", "w_mlp2")
BIG_ROWS = (N_Q_HEADS * HEAD_DIM // N_DEV, D_INNER // N_DEV, D_MODEL // N_DEV,
            D_MODEL * (D_FF // N_DEV) // PACK_COLS, D_FF // N_DEV)
N_IN_SHARD = D_IN_PROJ // N_DEV
assert sum(BIG_ROWS) % 16 == 0


def _pack_big(shards, dtype):
    return jnp.concatenate([s.astype(dtype).reshape(-1, PACK_COLS) for s in shards], axis=0)


def _unpack_big(packed, shapes):
    out, off = [], 0
    for rows, shp in zip(BIG_ROWS, shapes):
        out.append(packed[off:off + rows].reshape(shp))
        off += rows
    return out


LATE = ("attn_out", "ssd_out", "o", "mlp1", "mlp2")
LATE_SHAPES = ((N_Q_HEADS * HEAD_DIM, D_MODEL), (D_INNER, D_MODEL), (D_MODEL, D_MODEL), (D_MODEL, D_FF),
               (D_FF, D_MODEL))


def _split_w_in(g_in):
    w_in = g_in.transpose(1, 0, 2).reshape(D_MODEL, D_IN_PROJ)
    w = {}
    off = 0
    for name, size in zip(PROJ_NAMES, PROJ_SIZES):
        w[name] = w_in[:, off:off + size]
        off += size
    w["dt"] = jnp.pad(w["dt"], ((0, 0), (0, DT_PAD - 2 * N_SSD_HEADS)))
    return w


def _split_late(g):
    offs = np.cumsum((0,) + BIG_ROWS)
    sl = [g[:, offs[i]:offs[i + 1]] for i in range(len(BIG))]
    return {"attn_out": sl[0].reshape(LATE_SHAPES[0]), "ssd_out": sl[1].reshape(LATE_SHAPES[1]),
            "o": sl[2].reshape(LATE_SHAPES[2]),
            "mlp1": sl[3].reshape(N_DEV, D_MODEL, D_FF // N_DEV).transpose(1, 0, 2).reshape(LATE_SHAPES[3]),
            "mlp2": sl[4].reshape(LATE_SHAPES[4])}


def _pack_in_grads(gw):
    gw = {n: g.astype(bf16) for n, g in gw.items()}
    gw["dt"] = gw["dt"][:, :2 * N_SSD_HEADS]
    g_in = jnp.concatenate([gw[n] for n in PROJ_NAMES], axis=1)
    return g_in.reshape(D_MODEL, N_DEV, N_IN_SHARD).transpose(1, 0, 2)


def _pack_late_grads(gw):
    gw = {n: g.astype(bf16) for n, g in gw.items()}
    parts = [
        gw["attn_out"].reshape(N_DEV, -1, PACK_COLS),
        gw["ssd_out"].reshape(N_DEV, -1, PACK_COLS),
        gw["o"].reshape(N_DEV, -1, PACK_COLS),
        gw["mlp1"].reshape(D_MODEL, N_DEV, D_FF // N_DEV).transpose(1, 0, 2).reshape(N_DEV, -1, PACK_COLS),
        gw["mlp2"].reshape(N_DEV, -1, PACK_COLS),
    ]
    return jnp.concatenate(parts, axis=1)


SMALL = ("norm1_w", "norm2_w", "q_norm_w", "k_norm_w", "conv_w", "conv_b", "A_log", "dt_bias", "ssd_D", "ssd_norm_w")


def kernel(x, c, w_ada, b_ada, norm1_w, norm2_w, w_in, q_norm_w, k_norm_w, conv_w, conv_b, A_log, dt_bias, ssd_D, ssd_norm_w, w_attn_out, w_ssd_out, w_o, w_mlp1, w_mlp2, loss_target, m_w_ada, m_b_ada, m_norm1_w, m_norm2_w, m_w_in, m_q_norm_w, m_k_norm_w, m_conv_w, m_conv_b, m_A_log, m_dt_bias, m_ssd_D, m_ssd_norm_w, m_w_attn_out, m_w_ssd_out, m_w_o, m_w_mlp1, m_w_mlp2, v_w_ada, v_b_ada, v_norm1_w, v_norm2_w, v_w_in, v_q_norm_w, v_k_norm_w, v_conv_w, v_conv_b, v_A_log, v_dt_bias, v_ssd_D, v_ssd_norm_w, v_w_attn_out, v_w_ssd_out, v_w_o, v_w_mlp1, v_w_mlp2):
    args = dict(locals())
    me = _my_index()
    n_ada = 6 * D_MODEL // N_DEV
    n_cw = CONV_DIM // N_DEV

    blk = jnp.zeros((8, D_MODEL), f32)
    blk = blk.at[0:1, :].set(c)
    blk = blk.at[1:1 + D_CONV, :n_cw].set(conv_w[0])
    g0 = _all_gather(blk, "gather_c_convw", in_vmem=True)
    c_all = g0[:, 0, :]
    conv_w_full = g0[:, 1:1 + D_CONV, :n_cw].transpose(1, 0, 2).reshape(D_CONV, CONV_DIM)

    b_shard = lax.dynamic_slice(b_ada, (0, me * n_ada), (1, n_ada))
    mod_cols = _ada_fwd(c_all, w_ada[0], b_shard)
    g1 = _all_gather(mod_cols, "gather_mod", in_vmem=True)
    mod_mine = lax.dynamic_index_in_dim(g1, me, axis=1, keepdims=False)
    mod = mod_mine.reshape(6, 1, D_MODEL)

    big_shapes = [args[n].shape[1:] for n in BIG]
    late_shard = _pack_big([args[n][0] for n in BIG], bf16)
    wfull = _split_w_in(_all_gather(w_in[0].astype(bf16), "gather_w_in", in_vmem=False))
    recv_in_like = jnp.zeros((N_DEV,) + w_in.shape[1:], bf16)
    recv_late_like = jnp.zeros((N_DEV,) + late_shard.shape, bf16)

    small = {"norm1_w": norm1_w, "norm2_w": norm2_w, "q_norm_w": q_norm_w, "k_norm_w": k_norm_w,
             "conv_w": conv_w_full, "conv_b": conv_b, "A_log": A_log[0], "dt_bias": dt_bias[0], "ssd_D": ssd_D,
             "ssd_norm_w": ssd_norm_w}

    loss, (gx, gmod, gsmall, recv_in, recv_late) = jax.value_and_grad(local_loss, argnums=(0, 1, 2, 3, 4))(
        x[0], mod, small, recv_in_like, recv_late_like, wfull, late_shard, loss_target[0])

    small_list = [gmod, gsmall["norm1_w"], gsmall["norm2_w"], gsmall["q_norm_w"], gsmall["k_norm_w"], gsmall["conv_w"],
                  gsmall["conv_b"], gsmall["A_log"], gsmall["dt_bias"], gsmall["ssd_D"], gsmall["ssd_norm_w"],
                  loss.reshape(1)]
    small_shapes = [a.shape for a in small_list]
    g2 = _all_gather(_pack_small(small_list), "gather_small_grads", in_vmem=True)
    summed = _unpack_small(_sum_over_mesh(g2), small_shapes)
    loss_total = summed[-1][0]
    g_b_ada = summed[0].reshape(1, 6 * D_MODEL)
    g_small = dict(zip(SMALL, summed[1:-1]))
    g_conv_w = lax.dynamic_slice(g_small["conv_w"], (0, me * n_cw), (D_CONV, n_cw))

    dmod_all = g2[:, :6 * D_MODEL // LANE, :].reshape(N_DEV, 6 * D_MODEL)
    dmod_shard = lax.dynamic_slice(dmod_all, (0, me * n_ada), (N_DEV, n_ada))
    ada = _ada_bwd_adamw(c_all, dmod_shard, w_ada[0], m_w_ada[0], v_w_ada[0])

    small_grads = {"b_ada": g_b_ada, "norm1_w": g_small["norm1_w"], "norm2_w": g_small["norm2_w"],
                   "q_norm_w": g_small["q_norm_w"], "k_norm_w": g_small["k_norm_w"], "conv_w": g_conv_w[None],
                   "conv_b": g_small["conv_b"], "A_log": g_small["A_log"][None], "dt_bias": g_small["dt_bias"][None],
                   "ssd_D": g_small["ssd_D"], "ssd_norm_w": g_small["ssd_norm_w"]}
    sm_names = list(small_grads)
    sm_shapes = [args[n].shape for n in sm_names]
    sm = _adamw_small(_pack_small([args[n] for n in sm_names]), _pack_small([small_grads[n] for n in sm_names]),
                      _pack_small([args["m_" + n] for n in sm_names]), _pack_small([args["v_" + n] for n in sm_names]))
    sm_delta, sm_m, sm_v = [dict(zip(sm_names, _unpack_small(t, sm_shapes))) for t in sm]
    small_grads = {n: small_grads[n].reshape(args[n].shape) for n in sm_names}

    w_in_out = _sum_adamw(recv_in, w_in[0], m_w_in[0], v_w_in[0], "sum_adamw_w_in")
    big = _sum_adamw(recv_late, _pack_big([args[n][0] for n in BIG], f32),
                     _pack_big([args["m_" + n][0] for n in BIG], f32),
                     _pack_big([args["v_" + n][0] for n in BIG], f32), "sum_adamw")
    big_g, big_delta, big_m, big_v = [dict(zip(BIG, [t[None] for t in _unpack_big(p, big_shapes)])) for p in big]
    big_g["w_in"], big_delta["w_in"], big_m["w_in"], big_v["w_in"] = [t[None] for t in w_in_out]

    names = ("w_ada", "b_ada", "norm1_w", "norm2_w", "w_in", "q_norm_w", "k_norm_w", "conv_w", "conv_b", "A_log",
             "dt_bias", "ssd_D", "ssd_norm_w", "w_attn_out", "w_ssd_out", "w_o", "w_mlp1", "w_mlp2")
    grads, deltas, new_m, new_v = {}, {}, {}, {}
    for n in names:
        if n == "w_ada":
            grads[n], deltas[n], new_m[n], new_v[n] = [t[None] for t in ada]
        elif n in big_g:
            grads[n], deltas[n], new_m[n], new_v[n] = big_g[n], big_delta[n], big_m[n], big_v[n]
        else:
            grads[n], deltas[n], new_m[n], new_v[n] = small_grads[n], sm_delta[n], sm_m[n], sm_v[n]
    return (loss_total, gx[None], *[grads[n] for n in names], *[deltas[n] for n in names],
            *[new_m[n] for n in names], *[new_v[n] for n in names])
```

```python
import functools
import math

import jax
import jax.numpy as jnp
import numpy as np
from jax import lax
from jax.experimental import pallas as pl
from jax.experimental.pallas import tpu as pltpu

f32 = jnp.float32
bf16 = jnp.bfloat16
HIGHEST = lax.Precision.HIGHEST
MESH = pl.DeviceIdType.MESH

N_DEV = 8
D_MODEL = 1024
GRID_W = 64
N_Q_HEADS = 16
N_KV_HEADS = 4
HEAD_DIM = 64
ROPE_THETA = 10000.0
D_INNER = 2048
SSD_HEAD_DIM = 64
N_SSD_HEADS = 32
N_SSD_GROUPS = 4
D_STATE = 128
D_CONV = 5
CHUNK = 128
D_FF = 4096
EPS = 1e-6
CONV_DIM = D_INNER + 2 * N_SSD_GROUPS * D_STATE
GN = N_SSD_GROUPS * D_STATE
PROJ_NAMES = ("q", "k", "v", "xbc", "z", "dt", "ga", "gs")
PROJ_SIZES = (N_Q_HEADS * HEAD_DIM, N_KV_HEADS * HEAD_DIM, N_KV_HEADS * HEAD_DIM, CONV_DIM, D_INNER,
              2 * N_SSD_HEADS, D_MODEL, D_MODEL)
D_IN_PROJ = sum(PROJ_SIZES)
PROJ_DTYPES = (jnp.bfloat16, jnp.bfloat16, jnp.bfloat16, jnp.float32, jnp.bfloat16, jnp.float32, jnp.bfloat16,
               jnp.bfloat16)
DT_PAD = 128

ADAM_LR, ADAM_B1, ADAM_B2, ADAM_EPS, ADAM_WD, ADAM_STEP = 0.001, 0.9, 0.999, 1e-08, 0.01, 10

V7X_VMEM_LIMIT = 56 * 1024 * 1024
LANE = 128
PACK_COLS = 1024


def _cparams(**kw):
    return pltpu.CompilerParams(vmem_limit_bytes=V7X_VMEM_LIMIT, **kw)


def _pick(dim, prefs):
    for p in prefs:
        if dim % p == 0:
            return p
    return dim


def _my_index():
    return 4 * lax.axis_index("x") + 2 * lax.axis_index("y") + lax.axis_index("c")


COMM_SEMS = [pltpu.SemaphoreType.DMA((7,)), pltpu.SemaphoreType.DMA((7,)), pltpu.SemaphoreType.DMA]


def _gather_phases(x_ref, out_ref, send_sems, recv_sems, local_sem):
    x, y, cc = lax.axis_index("x"), lax.axis_index("y"), lax.axis_index("c")
    me, sibling = (x, y, cc), (x, y, 1 - cc)
    chips = [(1 - x, y), (x, 1 - y), (1 - x, 1 - y)]

    def slot(px, py, pc):
        return out_ref.at[4 * px + 2 * py + pc]

    def copy(k, blk, to, src=None):
        return pltpu.make_async_remote_copy(
            src_ref=slot(*blk) if src is None else src, dst_ref=slot(*blk),
            send_sem=send_sems.at[k], recv_sem=recv_sems.at[k], device_id=to, device_id_type=MESH)

    mine = pltpu.make_async_copy(x_ref, slot(*me), local_sem)
    first = [copy(0, me, sibling, src=x_ref)]
    first += [copy(1 + j, me, (*chip, cc), src=x_ref) for j, chip in enumerate(chips)]
    passed = [copy(4 + j, (*chip, cc), sibling) for j, chip in enumerate(chips)]

    def start():
        mine.start()
        for cp in first:
            cp.start()

    def finish():
        for j, chip in enumerate(chips):
            copy(1 + j, (*chip, cc), me).wait_recv()
            passed[j].start()
        copy(0, sibling, me).wait_recv()
        for j, chip in enumerate(chips):
            copy(4 + j, (*chip, 1 - cc), me).wait_recv()
        for cp in first + passed:
            cp.wait_send()
        mine.wait()

    return start, finish


def _scatter_phases(g_ref, out_ref, send_sems, recv_sems, local_sem):
    x, y, cc = lax.axis_index("x"), lax.axis_index("y"), lax.axis_index("c")
    me = 4 * x + 2 * y + cc
    mine = pltpu.make_async_copy(g_ref.at[me], out_ref.at[me], local_sem)

    def copy(k):
        fx, fy, fc = (k >> 2) & 1, (k >> 1) & 1, k & 1
        px = x + fx - 2 * x * fx
        py = y + fy - 2 * y * fy
        pc = cc + fc - 2 * cc * fc
        peer = 4 * px + 2 * py + pc
        send = pltpu.make_async_remote_copy(
            src_ref=g_ref.at[peer], dst_ref=out_ref.at[me],
            send_sem=send_sems.at[k - 1], recv_sem=recv_sems.at[k - 1],
            device_id=(px, py, pc), device_id_type=MESH)
        recv = pltpu.make_async_remote_copy(
            src_ref=g_ref.at[peer], dst_ref=out_ref.at[peer],
            send_sem=send_sems.at[k - 1], recv_sem=recv_sems.at[k - 1],
            device_id=(px, py, pc), device_id_type=MESH)
        return send, recv

    pairs = [copy(k) for k in range(1, N_DEV)]

    def start():
        mine.start()
        for send, _ in pairs:
            send.start()

    def finish():
        for _, recv in pairs:
            recv.wait_recv()
        for send, _ in pairs:
            send.wait_send()
        mine.wait()

    return start, finish


def _all_gather(block, name, in_vmem):
    r, c = block.shape

    def body(x_ref, out_ref, send_sems, recv_sems, local_sem):
        start, finish = _gather_phases(x_ref, out_ref, send_sems, recv_sems, local_sem)
        start()
        finish()

    space = pltpu.VMEM if in_vmem else pl.ANY
    return pl.pallas_call(
        body, name=name,
        out_shape=jax.ShapeDtypeStruct((N_DEV, r, c), block.dtype),
        in_specs=[pl.BlockSpec(memory_space=space)],
        out_specs=pl.BlockSpec(memory_space=space),
        scratch_shapes=[pltpu.SemaphoreType.DMA((7,)), pltpu.SemaphoreType.DMA((7,)), pltpu.SemaphoreType.DMA],
    )(block)


def _scatter_blocks(g, name):
    _, r, c = g.shape

    def body(g_ref, out_ref, send_sems, recv_sems, local_sem):
        start, finish = _scatter_phases(g_ref, out_ref, send_sems, recv_sems, local_sem)
        start()
        finish()

    return pl.pallas_call(
        body, name=name,
        out_shape=jax.ShapeDtypeStruct(g.shape, g.dtype),
        in_specs=[pl.BlockSpec(memory_space=pl.ANY)],
        out_specs=pl.BlockSpec(memory_space=pl.ANY),
        scratch_shapes=[pltpu.SemaphoreType.DMA((7,)), pltpu.SemaphoreType.DMA((7,)), pltpu.SemaphoreType.DMA],
    )(g)


_DIMS = {"nn": (((1,), (0,)), ((), ())), "nt": (((1,), (1,)), ((), ())), "tn": (((0,), (0,)), ((), ()))}


def _matmul(a, b, mode, out_dtype, name, epilogue=None, side=None):
    if mode == "nn":
        (m, k), (_, n) = a.shape, b.shape
    elif mode == "nt":
        (m, k), (n, _) = a.shape, b.shape
    else:
        (k, m), (_, n) = a.shape, b.shape
    tm = _pick(m, (1024, 512, 256, 128))
    if mode == "tn":
        tn = _pick(n, (1536, 1024, 512, 256, 128))
        tk = _pick(k, (2048, 1024, 512, 256, 128)) if b.dtype == bf16 else _pick(k, (1024, 512, 256, 128))
    else:
        tn = _pick(n, (1024, 512, 384, 256, 128))
        tk = _pick(k, (2048, 1024, 512, 256, 128)) if a.dtype == bf16 else _pick(k, (1024, 512, 256, 128))
    nk = k // tk
    dims = _DIMS[mode]
    n_in = 3 if epilogue == "drelu2" else 2

    def body(*refs):
        a_ref, b_ref = refs[:2]
        o_ref, acc_ref = refs[n_in], refs[n_in + 1]
        kk = pl.program_id(2)
        part = lax.dot_general(a_ref[...].astype(bf16), b_ref[...].astype(bf16), dims, preferred_element_type=f32)

        def finish(acc):
            if epilogue == "relu2":
                r = jnp.maximum(acc, 0.0)
                o_ref[...] = (r * r).astype(out_dtype)
            elif epilogue == "drelu2":
                o_ref[...] = (acc * (2.0 * jnp.sqrt(refs[2][...].astype(f32)))).astype(out_dtype)
            else:
                o_ref[...] = acc.astype(out_dtype)

        if nk == 1:
            finish(part)
        else:
            @pl.when(kk == 0)
            def _():
                acc_ref[...] = part

            @pl.when(kk > 0)
            def _():
                acc_ref[...] += part

            @pl.when(kk == nk - 1)
            def _():
                finish(acc_ref[...])

    if mode == "tn":
        a_spec = pl.BlockSpec((tk, tm), lambda i, j, kk: (kk, i))
    else:
        a_spec = pl.BlockSpec((tm, tk), lambda i, j, kk: (i, kk))
    if mode == "nt":
        b_spec = pl.BlockSpec((tn, tk), lambda i, j, kk: (j, kk))
    else:
        b_spec = pl.BlockSpec((tk, tn), lambda i, j, kk: (kk, j))
    o_spec = pl.BlockSpec((tm, tn), lambda i, j, kk: (i, j))
    o_shape = jax.ShapeDtypeStruct((m, n), out_dtype)
    return pl.pallas_call(
        body, name=name, grid=(m // tm, n // tn, nk),
        in_specs=[a_spec, b_spec] + ([o_spec] if epilogue == "drelu2" else []),
        out_specs=o_spec, out_shape=o_shape,
        scratch_shapes=[pltpu.VMEM((tm, tn), f32)],
        compiler_params=_cparams(dimension_semantics=("parallel", "parallel", "arbitrary")),
    )(*((a, b, side) if epilogue == "drelu2" else (a, b)))


@jax.custom_vjp
def mlp(h, w1, w1grad, w2, w2grad):
    r = _matmul(h, w1, "nn", bf16, "mlp1_fwd", epilogue="relu2")
    return _matmul(r, w2, "nn", f32, "mlp2_fwd")


def _mlp_fwd(h, w1, w1grad, w2, w2grad):
    r = _matmul(h, w1, "nn", bf16, "mlp1_fwd", epilogue="relu2")
    return _matmul(r, w2, "nn", f32, "mlp2_fwd"), (h, w1, w2, r)


def _mlp_bwd(res, dy):
    h, w1, w2, r = res
    du = _matmul(dy, w2, "nt", bf16, "mlp2_dgrad", epilogue="drelu2", side=r)
    dw2 = _matmul(r, dy, "tn", f32, "mlp2_wgrad")
    dh = _matmul(du, w1, "nt", h.dtype, "mlp1_dgrad")
    dw1 = _matmul(h, du, "tn", f32, "mlp1_wgrad")
    return dh, jnp.zeros_like(w1), dw1, jnp.zeros_like(w2), dw2


mlp.defvjp(_mlp_fwd, _mlp_bwd)


def make_linear(name):
    @jax.custom_vjp
    def linear(a, w, wgrad):
        return _matmul(a, w, "nn", f32, name + "_fwd")

    def fwd(a, w, wgrad):
        return linear(a, w, wgrad), (a, w)

    def bwd(res, dy):
        a, w = res
        da = _matmul(dy, w, "nt", a.dtype, name + "_dgrad")
        dw = _matmul(a, dy, "tn", f32, name + "_wgrad")
        return da, jnp.zeros_like(w), dw

    linear.defvjp(fwd, bwd)
    return linear


def _in_proj_dgrad(dys, ws, g):
    s, d = dys[0].shape[0], ws[0].shape[0]
    tm = _pick(s, (512, 256, 128))
    tks = [min(w.shape[1], 1024) for w in ws]
    steps = [w.shape[1] // tk for w, tk in zip(ws, tks)]
    starts = [sum(steps[:p]) for p in range(len(ws))]
    total = sum(steps)
    n_p, n_i = len(ws), s // tm
    assert steps[0] == 1

    def body(*refs):
        dy_refs, w_refs, g_ref = refs[:n_p], refs[n_p:2 * n_p], refs[2 * n_p]
        dh_ref, recv_ref, acc_ref, send_sems, recv_sems, local_sem = refs[2 * n_p + 1:]
        i, t = pl.program_id(0), pl.program_id(1)
        start, finish = _scatter_phases(g_ref, recv_ref, send_sems, recv_sems, local_sem)

        @pl.when((i == 0) & (t == 0))
        def _():
            start()

        for p in range(n_p):
            @pl.when((t >= starts[p]) & (t < starts[p] + steps[p]))
            def _(p=p):
                part = lax.dot_general(dy_refs[p][...].astype(bf16), w_refs[p][...], _DIMS["nt"],
                                       preferred_element_type=f32)
                if p == 0:
                    acc_ref[...] = part
                else:
                    acc_ref[...] += part

        @pl.when(t == total - 1)
        def _():
            dh_ref[...] = acc_ref[...].astype(dh_ref.dtype)

        @pl.when((i == n_i - 1) & (t == total - 1))
        def _():
            finish()

    def piece_map(p, rows):
        def index_map(i, t):
            blk = jnp.clip(t - starts[p], 0, steps[p] - 1)
            return (i, blk) if rows else (0, blk)

        return index_map

    hbm = pl.BlockSpec(memory_space=pl.ANY)
    in_specs = [pl.BlockSpec((tm, tks[p]), piece_map(p, True)) for p in range(n_p)]
    in_specs += [pl.BlockSpec((d, tks[p]), piece_map(p, False)) for p in range(n_p)]
    return pl.pallas_call(
        body, name="in_proj_dgrad", grid=(n_i, total), in_specs=in_specs + [hbm],
        out_specs=[pl.BlockSpec((tm, d), lambda i, t: (i, 0)), hbm],
        out_shape=[jax.ShapeDtypeStruct((s, d), bf16), jax.ShapeDtypeStruct(g.shape, g.dtype)],
        scratch_shapes=[pltpu.VMEM((tm, d), f32)] + COMM_SEMS,
        compiler_params=_cparams(dimension_semantics=("arbitrary", "arbitrary")),
    )(*dys, *ws, g)


@jax.custom_vjp
def in_proj(h, ws, recv_like):
    return tuple(_matmul(h, w, "nn", dt, "lin_" + n + "_fwd") for n, w, dt in zip(PROJ_NAMES, ws, PROJ_DTYPES))


def _in_proj_fwd(h, ws, recv_like):
    return in_proj(h, ws, recv_like), (h, ws)


def _in_proj_bwd(res, dys):
    h, ws = res
    dws = {n: _matmul(h, dy, "tn", f32, "lin_" + n + "_wgrad") for n, dy in zip(PROJ_NAMES, dys)}
    dh, recv = _in_proj_dgrad(dys, ws, _pack_in_grads(dws))
    return dh.astype(h.dtype), tuple(jnp.zeros_like(w) for w in ws), recv


in_proj.defvjp(_in_proj_fwd, _in_proj_bwd)


def make_rowwise(name, fn, row_out, sum_out=(), tm_pref=256):
    def specs(rows, gpars, cpars, consts, tm):
        s = [pl.BlockSpec((tm, r.shape[1]), lambda i: (i, 0)) for r in rows]
        s += [pl.BlockSpec(p.shape, lambda i: (0, 0)) for p in gpars]
        s += [pl.BlockSpec(p.shape, lambda i: (0, 0)) for p in cpars]
        for cst in consts:
            nb = cst.shape[0] // tm
            s.append(pl.BlockSpec((tm, cst.shape[1]), lambda i, nb=nb: (i % nb, 0)))
        return s

    def tile_rows(rows, consts):
        r = rows[0].shape[0]
        common = math.gcd(r, *[cst.shape[0] for cst in consts])
        tm = _pick(common, (tm_pref, 512, 256, 128, 64, 32, 16, 8))
        return r, tm

    def forward(rows, gpars, cpars, consts):
        r, tm = tile_rows(rows, consts)
        nr, ng, nc, nk = len(rows), len(gpars), len(cpars), len(consts)

        def body(*refs):
            ins = refs[:nr + ng + nc + nk]
            outs = refs[nr + ng + nc + nk:]
            rv = [t[...].astype(f32) for t in ins[:nr]]
            gv = [t[...].astype(f32) for t in ins[nr:nr + ng]]
            cv = [t[...] for t in ins[nr + ng:nr + ng + nc]]
            kv = [t[...].astype(f32) for t in ins[nr + ng + nc:]]
            ro, so = fn(rv, gv, cv, kv)
            for o_ref, val in zip(outs[:len(row_out)], ro):
                o_ref[...] = val.astype(o_ref.dtype)
            if sum_out:
                @pl.when(pl.program_id(0) == 0)
                def _():
                    for o_ref in outs[len(row_out):]:
                        o_ref[...] = jnp.zeros_like(o_ref)
                for o_ref, val in zip(outs[len(row_out):], so):
                    o_ref[...] += val

        out_specs = [pl.BlockSpec((tm, w), lambda i: (i, 0)) for w, _ in row_out]
        out_specs += [pl.BlockSpec(shp, lambda i: (0, 0)) for shp in sum_out]
        out_shape = [jax.ShapeDtypeStruct((r, w), dt) for w, dt in row_out]
        out_shape += [jax.ShapeDtypeStruct(shp, f32) for shp in sum_out]
        res = pl.pallas_call(
            body, name=name + "_fwd", grid=(r // tm,),
            in_specs=specs(rows, gpars, cpars, consts, tm), out_specs=out_specs, out_shape=out_shape,
            compiler_params=_cparams(dimension_semantics=("arbitrary",)),
        )(*rows, *gpars, *cpars, *consts)
        return tuple(res[:len(row_out)]), tuple(res[len(row_out):])

    def backward(rows, gpars, cpars, consts, d_ro, d_so):
        r, tm = tile_rows(rows, consts)
        nr, ng, nc, nk = len(rows), len(gpars), len(cpars), len(consts)
        n_in = nr + ng + nc + nk + len(row_out) + len(sum_out)

        def body(*refs):
            ins, outs = refs[:n_in], refs[n_in:]
            rv = [t[...].astype(f32) for t in ins[:nr]]
            gv = [t[...].astype(f32) for t in ins[nr:nr + ng]]
            cv = [t[...] for t in ins[nr + ng:nr + ng + nc]]
            kv = [t[...].astype(f32) for t in ins[nr + ng + nc:nr + ng + nc + nk]]
            o = nr + ng + nc + nk
            dro = [t[...].astype(f32) for t in ins[o:o + len(row_out)]]
            dso = [t[...] for t in ins[o + len(row_out):]]
            _, vjp = jax.vjp(lambda a, b: tuple(tuple(t) for t in fn(a, b, cv, kv)), rv, gv)
            drv, dgv = vjp((tuple(dro), tuple(dso)))
            for o_ref, val in zip(outs[:nr], drv):
                o_ref[...] = val.astype(o_ref.dtype)
            if ng:
                @pl.when(pl.program_id(0) == 0)
                def _():
                    for o_ref in outs[nr:]:
                        o_ref[...] = jnp.zeros_like(o_ref)
                for o_ref, val in zip(outs[nr:], dgv):
                    o_ref[...] += val

        in_specs = specs(rows, gpars, cpars, consts, tm)
        in_specs += [pl.BlockSpec((tm, w), lambda i: (i, 0)) for w, _ in row_out]
        in_specs += [pl.BlockSpec(shp, lambda i: (0, 0)) for shp in sum_out]
        out_specs = [pl.BlockSpec((tm, t.shape[1]), lambda i: (i, 0)) for t in rows]
        out_specs += [pl.BlockSpec(p.shape, lambda i: (0, 0)) for p in gpars]
        out_shape = [jax.ShapeDtypeStruct(t.shape, t.dtype) for t in rows]
        out_shape += [jax.ShapeDtypeStruct(p.shape, f32) for p in gpars]
        res = pl.pallas_call(
            body, name=name + "_bwd", grid=(r // tm,),
            in_specs=in_specs, out_specs=out_specs, out_shape=out_shape,
            compiler_params=_cparams(dimension_semantics=("arbitrary",)),
        )(*rows, *gpars, *cpars, *consts, *d_ro, *d_so)
        return tuple(res[:nr]), tuple(res[nr:])

    @jax.custom_vjp
    def op(rows, gpars, cpars, consts):
        return forward(rows, gpars, cpars, consts)

    def op_fwd(rows, gpars, cpars, consts):
        return forward(rows, gpars, cpars, consts), (rows, gpars, cpars, consts)

    def op_bwd(res, cts):
        rows, gpars, cpars, consts = res
        d_ro, d_so = cts
        drows, dg = backward(rows, gpars, cpars, consts, d_ro, d_so)
        dg = tuple(d.astype(p.dtype) for d, p in zip(dg, gpars))
        return (drows, dg, tuple(jnp.zeros_like(p) for p in cpars), tuple(jnp.zeros_like(k) for k in consts))

    op.defvjp(op_fwd, op_bwd)
    return op


def _rms(x):
    return x * lax.rsqrt(jnp.mean(x * x, axis=-1, keepdims=True) + EPS)


def _silu(x):
    return x * jax.nn.sigmoid(x)


def _fn_norm_mod(rows, gp, cp, ks):
    (x,), (nw, sc, sh) = rows, gp
    return ((_rms(x) * nw) * (1.0 + sc) + sh, x), ()


PAIR = 2 * HEAD_DIM


def _exact_dot(a, m):
    hi = a.astype(bf16)
    lo = (a - hi.astype(f32)).astype(bf16)
    return jnp.dot(hi, m, preferred_element_type=f32) + jnp.dot(lo, m, preferred_element_type=f32)


def _make_sel_dot(sign):
    @jax.custom_vjp
    def sel_dot(a, m):
        return _exact_dot(a, m)

    def fwd(a, m):
        return _exact_dot(a, m), m

    def bwd(m, g):
        return sign * _exact_dot(g, m), jnp.zeros_like(m)

    sel_dot.defvjp(fwd, bwd)
    return sel_dot


_head_sum_dot = _make_sel_dot(1.0)
_rope_perm_dot = _make_sel_dot(-1.0)


def _pair_norm_rope(t, w2, gsum, perm, cos2, sin2, out_scale):
    ss = _head_sum_dot(t * t, gsum)
    u = t * lax.rsqrt(ss * (1.0 / HEAD_DIM) + EPS) * w2
    return (u * cos2 + _rope_perm_dot(u, perm) * sin2) * out_scale


def _pair_consts():
    eye = np.eye(2, dtype=np.float32)
    gsum = np.kron(eye, np.ones((HEAD_DIM, HEAD_DIM), np.float32))
    return jnp.asarray(gsum, bf16), jnp.asarray(np.kron(eye, _rope_perm()), bf16)


def make_head_rope(name, nh, out_scale, head_major):
    width = nh * HEAD_DIM
    fn = functools.partial(_pair_norm_rope, out_scale=out_scale)

    def out_spec(tm):
        if head_major:
            return pl.BlockSpec((nh, tm, HEAD_DIM), lambda i: (0, i, 0))
        return pl.BlockSpec((tm, width), lambda i: (i, 0))

    def specs(tm):
        def full(shp):
            return pl.BlockSpec(shp, lambda i: (0, 0))

        return [pl.BlockSpec((tm, width), lambda i: (i, 0)), full((1, PAIR)), full((PAIR, PAIR)), full((PAIR, PAIR)),
                pl.BlockSpec((tm, PAIR), lambda i: (i, 0)), pl.BlockSpec((tm, PAIR), lambda i: (i, 0))]

    def forward(t, w2, gsum, perm, cos2, sin2):
        s = t.shape[0]
        tm = _pick(s, (512, 256, 128))

        def body(t_ref, w_ref, g_ref, p_ref, cos_ref, sin_ref, o_ref):
            for b in range(nh // 2):
                val = fn(t_ref[:, b * PAIR:(b + 1) * PAIR].astype(f32), w_ref[...], g_ref[...], p_ref[...], cos_ref[...],
                         sin_ref[...]).astype(o_ref.dtype)
                if head_major:
                    o_ref[2 * b] = val[:, :HEAD_DIM]
                    o_ref[2 * b + 1] = val[:, HEAD_DIM:]
                else:
                    o_ref[:, b * PAIR:(b + 1) * PAIR] = val

        return pl.pallas_call(
            body, name=name + "_fwd", grid=(s // tm,), in_specs=specs(tm), out_specs=out_spec(tm),
            out_shape=jax.ShapeDtypeStruct((nh, s, HEAD_DIM) if head_major else (s, width), bf16),
            compiler_params=_cparams(dimension_semantics=("arbitrary",)),
        )(t, w2, gsum, perm, cos2, sin2)

    def backward(t, w2, gsum, perm, cos2, sin2, dout):
        s = t.shape[0]
        tm = _pick(s, (512, 256, 128))

        def body(t_ref, w_ref, g_ref, p_ref, cos_ref, sin_ref, do_ref, dt_ref, dw_ref, pair_buf):
            @pl.when(pl.program_id(0) == 0)
            def _():
                dw_ref[...] = jnp.zeros_like(dw_ref)

            g_v, p_v, cos_v, sin_v = g_ref[...], p_ref[...], cos_ref[...], sin_ref[...]
            dw = jnp.zeros((1, PAIR), f32)
            for b in range(nh // 2):
                sl = slice(b * PAIR, (b + 1) * PAIR)
                if head_major:
                    pair_buf[:, :HEAD_DIM] = do_ref[2 * b].astype(f32)
                    pair_buf[:, HEAD_DIM:] = do_ref[2 * b + 1].astype(f32)
                    ct = pair_buf[...]
                else:
                    ct = do_ref[:, sl].astype(f32)
                _, vjp = jax.vjp(lambda a, c: fn(a, c, g_v, p_v, cos_v, sin_v), t_ref[:, sl].astype(f32), w_ref[...])
                dtb, dwb = vjp(ct)
                dt_ref[:, sl] = dtb.astype(dt_ref.dtype)
                dw = dw + dwb
            dw_ref[...] += dw

        return pl.pallas_call(
            body, name=name + "_bwd", grid=(s // tm,), in_specs=specs(tm) + [out_spec(tm)],
            out_specs=[pl.BlockSpec((tm, width), lambda i: (i, 0)), pl.BlockSpec((1, PAIR), lambda i: (0, 0))],
            out_shape=[jax.ShapeDtypeStruct((s, width), t.dtype), jax.ShapeDtypeStruct((1, PAIR), f32)],
            scratch_shapes=[pltpu.VMEM((tm, PAIR), f32)],
            compiler_params=_cparams(dimension_semantics=("arbitrary",)),
        )(t, w2, gsum, perm, cos2, sin2, dout)

    @jax.custom_vjp
    def op(t, w2, gsum, perm, cos2, sin2):
        return forward(t, w2, gsum, perm, cos2, sin2)

    def op_fwd(*args):
        return forward(*args), args

    def op_bwd(res, dout):
        dt, dw = backward(*res, dout)
        return (dt, dw) + tuple(jnp.zeros_like(r) for r in res[2:])

    op.defvjp(op_fwd, op_bwd)

    def apply(t, w, cos2, sin2):
        gsum, perm = _pair_consts()
        return op(t, jnp.concatenate([w, w], axis=-1), gsum, perm, cos2, sin2)

    return apply


def _fn_softplus(rows, gp, cp, ks):
    (x,), (b,) = rows, gp
    v = x + b
    return (jnp.maximum(v, 0.0) + jnp.log(1.0 + jnp.exp(-jnp.abs(v))),), ()


def _fn_ssd_gate(rows, gp, cp, ks):
    (y, z), (nw,) = rows, gp
    return (_rms(y * _silu(z)) * nw,), ()


def _fn_merge(rows, gp, cp, ks):
    ao, so, ga, gs = rows
    return (jax.nn.sigmoid(ga) * ao + jax.nn.sigmoid(gs) * so,), ()


def _fn_res_norm(rows, gp, cp, ks):
    (x, mo), (g1, nw, sc, sh) = rows, gp
    x1 = x + g1 * mo
    return (x1, (_rms(x1) * nw) * (1.0 + sc) + sh), ()


def _fn_loss(rows, gp, cp, ks):
    (x1, ff), (g2,), (tgt,) = rows, gp, ks
    err = x1 + g2 * ff - tgt
    return (), (0.5 * jnp.sum(jnp.sum(err * err, axis=-1, keepdims=True), axis=0, keepdims=True) / D_MODEL,)


HALO = 8
HALO_BWD = 16


def _conv_tiles(s, c):
    return _pick(s, (512, 256, 128)), _pick(c, (512, 256, 128))


def _halo_specs(tm, tc, s, halo=HALO):
    nb = tm // halo
    last = s // halo - 1
    cur = pl.BlockSpec((tm, tc), lambda j, i: (i, j))
    prev = pl.BlockSpec((halo, tc), lambda j, i: (jnp.maximum(i * nb - 1, 0), j))
    nxt = pl.BlockSpec((halo, tc), lambda j, i: (jnp.minimum((i + 1) * nb, last), j))
    return cur, prev, nxt


def _fill_halo(buf, cur, prev, nxt, tm, i, n_i, halo=HALO):
    buf[halo:halo + tm, :] = cur[...]
    buf[0:halo, :] = jnp.where(i > 0, prev[...], 0.0)
    buf[halo + tm:, :] = jnp.where(i < n_i - 1, nxt[...], 0.0)


def _conv_fwd(x, w, b, shard):
    s, c = x.shape
    tm, tc = _conv_tiles(s, c)
    n_i, n_j = s // tm, c // tc

    def body(cur, prev, nxt, w_ref, b_ref, shard_ref, o_ref, gath_ref, buf, send_sems, recv_sems, local_sem):
        j, i = pl.program_id(0), pl.program_id(1)
        start, finish = _gather_phases(shard_ref, gath_ref, send_sems, recv_sems, local_sem)

        @pl.when((j == 0) & (i == 0))
        def _():
            start()

        _fill_halo(buf, cur, prev, nxt, tm, i, n_i)
        pre = jnp.zeros((tm, tc), f32) + b_ref[...]
        for k in range(D_CONV):
            pre = pre + buf[HALO - 2 + k:HALO - 2 + k + tm, :] * w_ref[k:k + 1, :]
        o_ref[...] = _silu(pre)

        @pl.when((j == n_j - 1) & (i == n_i - 1))
        def _():
            finish()

    cur, prev, nxt = _halo_specs(tm, tc, s)
    hbm = pl.BlockSpec(memory_space=pl.ANY)
    return pl.pallas_call(
        body, name="conv_silu_fwd", grid=(n_j, n_i),
        in_specs=[cur, prev, nxt, pl.BlockSpec((D_CONV, tc), lambda j, i: (0, j)),
                  pl.BlockSpec((1, tc), lambda j, i: (0, j)), hbm],
        out_specs=[pl.BlockSpec((tm, tc), lambda j, i: (i, j)), hbm],
        out_shape=[jax.ShapeDtypeStruct((s, c), f32), jax.ShapeDtypeStruct((N_DEV,) + shard.shape, shard.dtype)],
        scratch_shapes=[pltpu.VMEM((tm + 2 * HALO, tc), f32)] + COMM_SEMS,
        compiler_params=_cparams(dimension_semantics=("arbitrary", "arbitrary")),
    )(x, x, x, w, b, shard)


def _conv_bwd(x, w, b, dy, g):
    s, c = x.shape
    tm, tc = _conv_tiles(s, c)
    n_i, n_j = s // tm, c // tc
    ext = tm + 16

    def body(cur, prev, nxt, dcur, dprev, dnxt, w_ref, b_ref, g_ref, dx_ref, dw_ref, db_ref, recv_ref,
             xbuf, dbuf, pbuf, send_sems, recv_sems, local_sem):
        j, i = pl.program_id(0), pl.program_id(1)
        start, finish = _scatter_phases(g_ref, recv_ref, send_sems, recv_sems, local_sem)

        @pl.when((j == 0) & (i == 0))
        def _():
            start()

        _fill_halo(xbuf, cur, prev, nxt, tm, i, n_i, HALO_BWD)
        _fill_halo(dbuf, dcur, dprev, dnxt, tm, i, n_i, HALO_BWD)
        xs = [xbuf[6 + k:6 + k + ext, :] for k in range(D_CONV)]
        pre = jnp.zeros((ext, tc), f32) + b_ref[...]
        for k in range(D_CONV):
            pre = pre + xs[k] * w_ref[k:k + 1, :]
        sg = jax.nn.sigmoid(pre)
        pbuf[...] = dbuf[8:8 + ext, :] * (sg * (1.0 + pre * (1.0 - sg)))
        dx = jnp.zeros((tm, tc), f32)
        for k in range(D_CONV):
            dx = dx + pbuf[10 - k:10 - k + tm, :] * w_ref[k:k + 1, :]
        dx_ref[...] = dx

        @pl.when(i == 0)
        def _():
            dw_ref[...] = jnp.zeros_like(dw_ref)
            db_ref[...] = jnp.zeros_like(db_ref)

        dpre = pbuf[8:8 + tm, :]
        db_ref[...] += jnp.sum(dpre, axis=0, keepdims=True)
        for k in range(D_CONV):
            dw_ref[k:k + 1, :] += jnp.sum(dpre * xs[k][8:8 + tm, :], axis=0, keepdims=True)

        @pl.when((j == n_j - 1) & (i == n_i - 1))
        def _():
            finish()

    cur, prev, nxt = _halo_specs(tm, tc, s, HALO_BWD)
    hbm = pl.BlockSpec(memory_space=pl.ANY)
    return pl.pallas_call(
        body, name="conv_silu_bwd", grid=(n_j, n_i),
        in_specs=[cur, prev, nxt, cur, prev, nxt, pl.BlockSpec((D_CONV, tc), lambda j, i: (0, j)),
                  pl.BlockSpec((1, tc), lambda j, i: (0, j)), hbm],
        out_specs=[pl.BlockSpec((tm, tc), lambda j, i: (i, j)), pl.BlockSpec((D_CONV, tc), lambda j, i: (0, j)),
                   pl.BlockSpec((1, tc), lambda j, i: (0, j)), hbm],
        out_shape=[jax.ShapeDtypeStruct((s, c), f32), jax.ShapeDtypeStruct((D_CONV, c), f32),
                   jax.ShapeDtypeStruct((1, c), f32), jax.ShapeDtypeStruct(g.shape, g.dtype)],
        scratch_shapes=[pltpu.VMEM((tm + 2 * HALO_BWD, tc), f32), pltpu.VMEM((tm + 2 * HALO_BWD, tc), f32),
                        pltpu.VMEM((ext, tc), f32)] + COMM_SEMS,
        compiler_params=_cparams(dimension_semantics=("arbitrary", "arbitrary")),
    )(x, x, x, dy, dy, dy, w, b, g)


@jax.custom_vjp
def conv_silu_comm(x, w, b, shard, recv_like):
    act, gathered = _conv_fwd(x, w, b, shard)
    return (act, gathered) + tuple(jnp.zeros(shp, f32) for shp in LATE_SHAPES)


def _conv_silu_comm_fwd(x, w, b, shard, recv_like):
    return conv_silu_comm(x, w, b, shard, recv_like), (x, w, b, shard)


def _conv_silu_comm_bwd(res, cts):
    x, w, b, shard = res
    dx, dw, db, recv = _conv_bwd(x, w, b, cts[0], _pack_late_grads(dict(zip(LATE, cts[2:]))))
    return dx, dw, db, jnp.zeros_like(shard), recv


conv_silu_comm.defvjp(_conv_silu_comm_fwd, _conv_silu_comm_bwd)


ATT_SCALE = HEAD_DIM ** -0.5
Q_SCALE = ATT_SCALE * math.log2(math.e)
LN2 = math.log(2.0)
REP = N_Q_HEADS // N_KV_HEADS


HP = 2
assert REP % HP == 0


def _attn_fwd(q, k, v):
    s, dh = q.shape[0], HEAD_DIM
    hq = q.shape[1] // dh
    tq = _pick(s, (256, 128))

    v1 = jnp.concatenate([v, jnp.ones(v.shape[:2] + (1,), v.dtype), jnp.zeros(v.shape[:2] + (dh - 1,), v.dtype)],
                         axis=-1)

    def body(q_ref, k_ref, v_ref, o_ref, p_ref, linv_ref):
        for j in range(HP):
            sl = slice(j * dh, (j + 1) * dh)
            sc = lax.dot_general(q_ref[:, sl], k_ref[0], _DIMS["nt"], preferred_element_type=f32)
            m = jnp.max(sc, axis=-1, keepdims=True)
            p = jnp.exp2(sc - m).astype(bf16)
            p_ref[j] = p
            o1 = jnp.dot(p, v_ref[0], preferred_element_type=f32)
            linv = 1.0 / o1[:, dh:dh + 1]
            o_ref[:, sl] = (o1[:, :dh] * linv).astype(o_ref.dtype)
            linv_ref[j] = linv

    return pl.pallas_call(
        body, name="attn_fwd", grid=(hq // HP, s // tq),
        in_specs=[pl.BlockSpec((tq, HP * dh), lambda h, i: (i, h)),
                  pl.BlockSpec((1, s, dh), lambda h, i: (h * HP // REP, 0, 0)),
                  pl.BlockSpec((1, s, 2 * dh), lambda h, i: (h * HP // REP, 0, 0))],
        out_specs=[pl.BlockSpec((tq, HP * dh), lambda h, i: (i, h)),
                   pl.BlockSpec((HP, tq, s), lambda h, i: (h, i, 0)),
                   pl.BlockSpec((HP, tq, 1), lambda h, i: (h, i, 0))],
        out_shape=[jax.ShapeDtypeStruct((s, hq * dh), bf16), jax.ShapeDtypeStruct((hq, s, s), bf16),
                   jax.ShapeDtypeStruct((hq, s, 1), f32)],
        compiler_params=_cparams(dimension_semantics=("parallel", "arbitrary")),
    )(q, k, v1)


def _attn_bwd(p, do, o, q, k, v, linv):
    hq, s, _ = p.shape
    dh = HEAD_DIM
    tq = _pick(s, (256, 128))

    def body(p_ref, do_ref, o_ref, q_ref, k_ref, v_ref, linv_ref, dq_ref, dkt_ref, dvt_ref):
        @pl.when(pl.program_id(1) == 0)
        def _():
            dkt_ref[...] = jnp.zeros_like(dkt_ref)
            dvt_ref[...] = jnp.zeros_like(dvt_ref)

        for j in range(HP):
            sl = slice(j * dh, (j + 1) * dh)
            pp, doh, li = p_ref[j], do_ref[:, sl], linv_ref[j]
            do32 = doh.astype(f32)
            d = jnp.sum(do32 * o_ref[:, sl].astype(f32), axis=-1, keepdims=True)
            dp = lax.dot_general(doh, v_ref[0], _DIMS["nt"], preferred_element_type=f32)
            ds = (pp.astype(f32) * ((dp - d) * li)).astype(bf16)
            dq_ref[:, sl] = (jnp.dot(ds, k_ref[0], preferred_element_type=f32) * LN2).astype(dq_ref.dtype)
            dvt_ref[j] += lax.dot_general((do32 * li).astype(bf16), pp, _DIMS["tn"], preferred_element_type=f32)
            dkt_ref[j] += lax.dot_general(q_ref[:, sl], ds, _DIMS["tn"], preferred_element_type=f32)

    def row():
        return pl.BlockSpec((tq, HP * dh), lambda h, i: (i, h))

    return pl.pallas_call(
        body, name="attn_bwd", grid=(hq // HP, s // tq),
        in_specs=[pl.BlockSpec((HP, tq, s), lambda h, i: (h, i, 0)), row(), row(), row(),
                  pl.BlockSpec((1, s, dh), lambda h, i: (h * HP // REP, 0, 0)),
                  pl.BlockSpec((1, s, dh), lambda h, i: (h * HP // REP, 0, 0)),
                  pl.BlockSpec((HP, tq, 1), lambda h, i: (h, i, 0))],
        out_specs=[row(), pl.BlockSpec((HP, dh, s), lambda h, i: (h, 0, 0)),
                   pl.BlockSpec((HP, dh, s), lambda h, i: (h, 0, 0))],
        out_shape=[jax.ShapeDtypeStruct((s, hq * dh), q.dtype), jax.ShapeDtypeStruct((hq, dh, s), f32),
                   jax.ShapeDtypeStruct((hq, dh, s), f32)],
        compiler_params=_cparams(dimension_semantics=("parallel", "arbitrary")),
    )(p, do, o, q, k, v, linv)


@jax.custom_vjp
def attention(q, k, v):
    return _attn_fwd(q, k, v)[0]


def _attention_fwd(q, k, v):
    o, p, linv = _attn_fwd(q, k, v)
    return o, (q, k, v, o, p, linv)


def _attention_bwd(res, do):
    q, k, v, o, p, linv = res
    s = q.shape[0]
    dq, dkt, dvt = _attn_bwd(p, do.astype(bf16), o, q, k, v, linv)

    def per_kv_head(t):
        return jnp.swapaxes(t.reshape(N_KV_HEADS, REP, HEAD_DIM, s).sum(axis=1), 1, 2)

    return dq, (per_kv_head(dkt) * LN2).astype(k.dtype), per_kv_head(dvt).astype(v.dtype)


attention.defvjp(_attention_fwd, _attention_bwd)


HPG = N_SSD_HEADS // N_SSD_GROUPS
GW = HPG * SSD_HEAD_DIM
NEG = -1e30
SPLIT_ROWS = 32


def _ssd_consts():
    k = np.arange(SPLIT_ROWS)[:, None]
    live = k < 3 * HPG
    sel_chunk = ((k % HPG) == (np.arange(HPG * CHUNK)[None, :] // CHUNK)) & live
    sel_head = ((k % HPG) == (np.arange(GW)[None, :] // SSD_HEAD_DIM)) & live
    return jnp.asarray(sel_chunk, bf16), jnp.asarray(sel_head, bf16)


def _split3(x):
    hi = x.astype(bf16).astype(f32)
    r1 = x - hi
    mid = r1.astype(bf16).astype(f32)
    lo = (r1 - mid).astype(bf16).astype(f32)
    return jnp.concatenate([hi, mid, lo, jnp.zeros_like(hi)], axis=0).astype(bf16)


def _tn(a, b):
    return lax.dot_general(a, b, _DIMS["tn"], preferred_element_type=f32)


def _nt(a, b):
    return lax.dot_general(a, b, _DIMS["nt"], preferred_element_type=f32)


def _nn(a, b):
    return jnp.dot(a, b, preferred_element_type=f32)


def _head_sum(sel8, x):
    hi = x.astype(bf16)
    lo = (x - hi.astype(f32)).astype(bf16)
    return _nt(sel8, hi) + _nt(sel8, lo)


def _ssd_masks(reverse):
    r = lax.broadcasted_iota(jnp.int32, (CHUNK, CHUNK), 0)
    c = lax.broadcasted_iota(jnp.int32, (CHUNK, CHUNK), 1)
    lower, upper = r >= c, r <= c
    return (upper, lower) if reverse else (lower, upper)


def _ssd_in_specs(cidx):
    return [pl.BlockSpec((CHUNK, D_INNER), lambda c: (cidx(c), 0)),
            pl.BlockSpec((CHUNK, GN), lambda c: (cidx(c), D_INNER // GN)),
            pl.BlockSpec((CHUNK, GN), lambda c: (cidx(c), D_INNER // GN + 1)),
            pl.BlockSpec((N_SSD_HEADS, CHUNK), lambda c: (0, cidx(c))),
            pl.BlockSpec((N_SSD_HEADS, 1), lambda c: (0, 0)),
            pl.BlockSpec((SPLIT_ROWS, HPG * CHUNK), lambda c: (0, 0)),
            pl.BlockSpec((SPLIT_ROWS, GW), lambda c: (0, 0))]


def _ssd_chunk_common(dtt_ref, a_ref, et_ref, mask_t):
    dtt = dtt_ref[...]
    et = jnp.dot(dtt * a_ref[...], mask_t.astype(f32), precision=HIGHEST, preferred_element_type=f32)
    et_ref[...] = et
    return dtt, et


def _ssd_group_common(g, dtt, et, selc_ref, selh_ref, xs_ref, b_ref, c_ref, last):
    gr = slice(g * HPG, (g + 1) * HPG)
    e3 = _split3(et[gr])
    col = _tn(e3, selc_ref[...])
    eb = _tn(e3, selh_ref[...])
    dtb = _tn(_split3(dtt[gr]), selh_ref[...])
    tbc = eb[last:last + 1, :]
    xs = xs_ref[:, g * GW:(g + 1) * GW]
    bg = b_ref[:, g * D_STATE:(g + 1) * D_STATE].astype(bf16)
    cg = c_ref[:, g * D_STATE:(g + 1) * D_STATE].astype(bf16)
    return col, eb, dtb, tbc, xs, bg, cg


def _ssd_fwd(xbc, dtt, a_col, reverse, y_prev=None, dexp=None):
    s = xbc.shape[0]
    nc = s // CHUNK
    cidx = (lambda c: nc - 1 - c) if reverse else (lambda c: c)
    last = 0 if reverse else CHUNK - 1
    selc, selh = _ssd_consts()
    final = y_prev is not None
    n_in = 9 if final else 7

    def body(*refs):
        xs_ref, b_ref, c_ref, dtt_ref, a_ref, selc_ref, selh_ref = refs[:7]
        y_ref, st_ref, ht_ref, et_ref = refs[n_in:]

        @pl.when(pl.program_id(0) == 0)
        def _():
            ht_ref[...] = jnp.zeros_like(ht_ref)

        mask, mask_t = _ssd_masks(reverse)
        dtt_v, et = _ssd_chunk_common(dtt_ref, a_ref, et_ref, mask_t)
        for g in range(N_SSD_GROUPS):
            col, eb, dtb, tbc, xs, bg, cg = _ssd_group_common(g, dtt_v, et, selc_ref, selh_ref, xs_ref, b_ref, c_ref,
                                                              last)
            xd = xs * dtb
            cb = _nt(cg, bg)
            ht = ht_ref[g]
            st_ref[0, g] = ht
            yoff = _nn(cg, ht.astype(bf16)) * jnp.exp(eb)
            for j in range(HPG):
                h = g * HPG + j
                hs = slice(j * SSD_HEAD_DIM, (j + 1) * SSD_HEAD_DIM)
                lam = jnp.exp(jnp.where(mask, col[:, j * CHUNK:(j + 1) * CHUNK] - et_ref[h:h + 1, :], NEG))
                yj = _nn((cb * lam).astype(bf16), xd[:, hs].astype(bf16)) + yoff[:, hs]
                cols = slice(g * GW + j * SSD_HEAD_DIM, g * GW + (j + 1) * SSD_HEAD_DIM)
                if final:
                    yj = yj + refs[7][:, cols] + xs[:, hs] * refs[8][:, cols]
                y_ref[:, cols] = yj
            ht_ref[g] = jnp.exp(tbc) * ht + _tn(bg, (xd * jnp.exp(tbc - eb)).astype(bf16))

    y_spec = pl.BlockSpec((CHUNK, D_INNER), lambda c: (cidx(c), 0))
    extra_specs = [y_spec, pl.BlockSpec((1, D_INNER), lambda c: (0, 0))] if final else []
    return pl.pallas_call(
        body, name="ssd_fwd_rev" if reverse else "ssd_fwd", grid=(nc,),
        in_specs=_ssd_in_specs(cidx) + extra_specs,
        out_specs=[y_spec, pl.BlockSpec((1, N_SSD_GROUPS, D_STATE, GW), lambda c: (cidx(c), 0, 0, 0))],
        out_shape=[jax.ShapeDtypeStruct((s, D_INNER), f32),
                   jax.ShapeDtypeStruct((nc, N_SSD_GROUPS, D_STATE, GW), f32)],
        scratch_shapes=[pltpu.VMEM((N_SSD_GROUPS, D_STATE, GW), f32), pltpu.VMEM((N_SSD_HEADS, CHUNK), f32)],
        compiler_params=_cparams(dimension_semantics=("arbitrary",)),
    )(xbc, xbc, xbc, dtt, a_col, selc, selh, *((y_prev, dexp) if final else ()))


def _ssd_bwd(xbc, dtt, a_col, states, dy, reverse, dxbc_prev=None, dexp=None):
    s = xbc.shape[0]
    nc = s // CHUNK
    cidx = (lambda c: c) if reverse else (lambda c: nc - 1 - c)
    last = 0 if reverse else CHUNK - 1
    selc, selh = _ssd_consts()
    final = dxbc_prev is not None
    n_in = 11 if final else 9
    n_out = 4 if final else 3

    def body(*refs):
        xs_ref, b_ref, c_ref, dtt_ref, a_ref, selc_ref, selh_ref, st_ref, dy_ref = refs[:9]
        dxbc_ref, ddtt_ref, da_ref = refs[n_in:n_in + 3]
        dh_ref, et_ref, det_ref, det2_ref, ddt_ref, q_ref = refs[n_in + n_out:]
        if final:
            prev_ref, dexp_ref, ddexp_ref = refs[9], refs[10], refs[n_in + 3]

        @pl.when(pl.program_id(0) == 0)
        def _():
            dh_ref[...] = jnp.zeros_like(dh_ref)
            da_ref[...] = jnp.zeros_like(da_ref)
            if final:
                ddexp_ref[...] = jnp.zeros_like(ddexp_ref)

        mask, mask_t = _ssd_masks(reverse)
        dtt_v, et = _ssd_chunk_common(dtt_ref, a_ref, et_ref, mask_t)
        sel8 = selh_ref[0:HPG, :]
        is_last = lax.broadcasted_iota(jnp.int32, (CHUNK, GW), 0) == last
        for g in range(N_SSD_GROUPS):
            col, eb, dtb, tbc, xs, bg, cg = _ssd_group_common(g, dtt_v, et, selc_ref, selh_ref, xs_ref, b_ref, c_ref,
                                                              last)
            xd = xs * dtb
            cb = _nt(cg, bg)
            cbt = _nt(bg, cg)
            exp_t = jnp.exp(tbc)
            dfac = jnp.exp(tbc - eb)
            ht = st_ref[0, g]
            dhn = dh_ref[g]
            ht16, dhn16 = ht.astype(bf16), dhn.astype(bf16)
            dy = dy_ref[:, g * GW:(g + 1) * GW]
            dye = dy * jnp.exp(eb)
            dye16 = dye.astype(bf16)
            dc = _nt(dye16, ht16)
            dh_ref[g] = exp_t * dhn + _tn(cg, dye16)
            deb = dye * _nn(cg, ht16)
            xdd = xd * dfac
            dxdd = _nn(bg, dhn16)
            db = _nt(xdd.astype(bf16), dhn16)
            dxd_state = dxdd * dfac
            ddf = dxdd * xdd
            dtbc = jnp.sum(ddf, axis=0, keepdims=True) + exp_t * jnp.sum(dhn * ht, axis=0, keepdims=True)
            deb = deb - ddf + jnp.where(is_last, dtbc, 0.0)
            dcb = jnp.zeros((CHUNK, CHUNK), f32)
            dcbt = jnp.zeros((CHUNK, CHUNK), f32)
            for j in range(HPG):
                h = g * HPG + j
                hs = slice(j * SSD_HEAD_DIM, (j + 1) * SSD_HEAD_DIM)
                colj = col[:, j * CHUNK:(j + 1) * CHUNK]
                row = et_ref[h:h + 1, :]
                lam = jnp.exp(jnp.where(mask, colj - row, NEG))
                lam_t = lam.T
                xdj, dyj = xd[:, hs].astype(bf16), dy[:, hs].astype(bf16)
                t1 = _nt(dyj, xdj) * lam
                t2 = _nt(xdj, dyj) * lam_t
                dcb, dcbt = dcb + t1, dcbt + t2
                det_ref[h:h + 1, :] = -jnp.sum(t1 * cb - t2 * cbt, axis=0, keepdims=True)
                dxdj = _nn((cbt * lam_t).astype(bf16), dyj) + dxd_state[:, hs]
                cols = slice(g * GW + j * SSD_HEAD_DIM, g * GW + (j + 1) * SSD_HEAD_DIM)
                dxs = dxdj * dtb[:, hs]
                if final:
                    dxs = dxs + prev_ref[:, cols] + dy[:, hs] * dexp_ref[:, cols]
                dxbc_ref[:, cols] = dxs
                q_ref[:, hs] = dxdj * xs[:, hs]
            b_cols = slice(D_INNER + g * D_STATE, D_INNER + (g + 1) * D_STATE)
            c_cols = slice(D_INNER + GN + g * D_STATE, D_INNER + GN + (g + 1) * D_STATE)
            db = db + _nn(dcbt.astype(bf16), cg)
            dc = dc + _nn(dcb.astype(bf16), bg)
            if final:
                db, dc = db + prev_ref[:, b_cols], dc + prev_ref[:, c_cols]
                ddexp_ref[:, g * GW:(g + 1) * GW] += jnp.sum(dy * xs, axis=0, keepdims=True)
            dxbc_ref[:, b_cols] = db
            dxbc_ref[:, c_cols] = dc
            det2_ref[g * HPG:(g + 1) * HPG, :] = _head_sum(sel8, deb)
            ddt_ref[g * HPG:(g + 1) * HPG, :] = _head_sum(sel8, q_ref[...])
        dat = jnp.dot(det_ref[...] + det2_ref[...], mask.astype(f32), precision=HIGHEST, preferred_element_type=f32)
        ddtt_ref[...] = ddt_ref[...] + dat * a_ref[...]
        da_ref[...] += jnp.sum(dat * dtt_v, axis=1, keepdims=True)

    in_specs = _ssd_in_specs(cidx) + [
        pl.BlockSpec((1, N_SSD_GROUPS, D_STATE, GW), lambda c: (cidx(c), 0, 0, 0)),
        pl.BlockSpec((CHUNK, D_INNER), lambda c: (cidx(c), 0))]
    hl = pltpu.VMEM((N_SSD_HEADS, CHUNK), f32)
    dxbc_spec = pl.BlockSpec((CHUNK, CONV_DIM), lambda c: (cidx(c), 0))
    dexp_spec = pl.BlockSpec((1, D_INNER), lambda c: (0, 0))
    return pl.pallas_call(
        body, name="ssd_bwd_rev" if reverse else "ssd_bwd", grid=(nc,),
        in_specs=in_specs + ([dxbc_spec, dexp_spec] if final else []),
        out_specs=[dxbc_spec, pl.BlockSpec((N_SSD_HEADS, CHUNK), lambda c: (0, cidx(c))),
                   pl.BlockSpec((N_SSD_HEADS, 1), lambda c: (0, 0))] + ([dexp_spec] if final else []),
        out_shape=[jax.ShapeDtypeStruct((s, CONV_DIM), f32), jax.ShapeDtypeStruct((N_SSD_HEADS, s), f32),
                   jax.ShapeDtypeStruct((N_SSD_HEADS, 1), f32)]
        + ([jax.ShapeDtypeStruct((1, D_INNER), f32)] if final else []),
        scratch_shapes=[pltpu.VMEM((N_SSD_GROUPS, D_STATE, GW), f32), hl, hl, hl, hl, pltpu.VMEM((CHUNK, GW), f32)],
        compiler_params=_cparams(dimension_semantics=("arbitrary",)),
    )(xbc, xbc, xbc, dtt, a_col, selc, selh, states, dy, *((dxbc_prev, dexp) if final else ()))


@jax.custom_vjp
def ssd_bidir(xbc, dtt, a_col, dexp):
    y_f, _ = _ssd_fwd(xbc, dtt[:N_SSD_HEADS], a_col[:N_SSD_HEADS], False)
    return _ssd_fwd(xbc, dtt[N_SSD_HEADS:], a_col[N_SSD_HEADS:], True, y_prev=y_f, dexp=dexp)[0]


def _ssd_bidir_fwd(xbc, dtt, a_col, dexp):
    y_f, st_f = _ssd_fwd(xbc, dtt[:N_SSD_HEADS], a_col[:N_SSD_HEADS], False)
    y, st_b = _ssd_fwd(xbc, dtt[N_SSD_HEADS:], a_col[N_SSD_HEADS:], True, y_prev=y_f, dexp=dexp)
    return y, (xbc, dtt, a_col, dexp, st_f, st_b)


def _ssd_bidir_bwd(res, dy):
    xbc, dtt, a_col, dexp, st_f, st_b = res
    dxbc_f, ddtt_f, da_f = _ssd_bwd(xbc, dtt[:N_SSD_HEADS], a_col[:N_SSD_HEADS], st_f, dy, False)
    dxbc, ddtt_b, da_b, ddexp = _ssd_bwd(xbc, dtt[N_SSD_HEADS:], a_col[N_SSD_HEADS:], st_b, dy, True,
                                         dxbc_prev=dxbc_f, dexp=dexp)
    return dxbc, jnp.concatenate([ddtt_f, ddtt_b], axis=0), jnp.concatenate([da_f, da_b], axis=0), ddexp


ssd_bidir.defvjp(_ssd_bidir_fwd, _ssd_bidir_bwd)


def _rope_tables(s):
    rows = s // GRID_W
    pos_row = np.repeat(np.arange(rows), GRID_W).astype(np.float32)
    pos_col = np.tile(np.arange(GRID_W), rows).astype(np.float32)
    axis_dim = HEAD_DIM // 2
    inv_freq = np.float32(ROPE_THETA) ** (-np.arange(0, axis_dim, 2, dtype=np.float32) / np.float32(axis_dim))
    ang_r = pos_row[:, None] * inv_freq[None, :].astype(np.float32)
    ang_c = pos_col[:, None] * inv_freq[None, :].astype(np.float32)
    cos = np.concatenate([np.cos(ang_r), np.cos(ang_r), np.cos(ang_c), np.cos(ang_c)] * 2, axis=-1)
    sin = np.concatenate([np.sin(ang_r), np.sin(ang_r), np.sin(ang_c), np.sin(ang_c)] * 2, axis=-1)
    return jnp.asarray(cos, f32), jnp.asarray(sin, f32)


def _rope_perm():
    p = np.zeros((HEAD_DIM, HEAD_DIM), np.float32)
    for j in range(HEAD_DIM):
        if (j % 32) < 16:
            p[j + 16, j] = -1.0
        else:
            p[j - 16, j] = 1.0
    return p


def local_loss(x, mod, small, recv_in_like, recv_late_like, wfull, late_shard, target):
    s = x.shape[0]
    lin = {n: make_linear("lin_" + n) for n in LATE if not n.startswith("mlp")}
    wfull, wgrads = dict(wfull), {}
    shift1, scale1, gate1, shift2, scale2, gate2 = [mod[i] for i in range(6)]

    norm_mod = make_rowwise("norm_mod", _fn_norm_mod, [(D_MODEL, bf16), (D_MODEL, f32)])
    (h, x_res), _ = norm_mod((x,), (small["norm1_w"], scale1, shift1), (), ())

    proj = dict(zip(PROJ_NAMES, in_proj(h, tuple(wfull[n] for n in PROJ_NAMES), recv_in_like)))

    cos, sin = _rope_tables(s)

    def heads(t, nh):
        return t.reshape(s, nh, HEAD_DIM).transpose(1, 0, 2)

    qr = make_head_rope("q_norm_rope", N_Q_HEADS, Q_SCALE, False)(proj["q"], small["q_norm_w"], cos, sin)
    kr = make_head_rope("k_norm_rope", N_KV_HEADS, 1.0, True)(proj["k"], small["k_norm_w"], cos, sin)
    vh = heads(proj["v"], N_KV_HEADS).astype(bf16)
    att = attention(qr, kr, vh)

    xbc, gathered, *carriers = conv_silu_comm(proj["xbc"], small["conv_w"], small["conv_b"], late_shard,
                                              recv_late_like)
    wfull.update(_split_late(gathered))
    wgrads.update(zip(LATE, carriers))
    ao = lin["attn_out"](att, wfull["attn_out"], wgrads["attn_out"])
    softplus = make_rowwise("dt_softplus", _fn_softplus, [(2 * N_SSD_HEADS, f32)])
    (dt,), _ = softplus((proj["dt"][:, :2 * N_SSD_HEADS],), (small["dt_bias"].reshape(1, 2 * N_SSD_HEADS),), (), ())
    a_neg = -jnp.exp(small["A_log"])
    dexp = jnp.repeat(small["ssd_D"].reshape(N_SSD_HEADS), SSD_HEAD_DIM).reshape(1, D_INNER)
    y = ssd_bidir(xbc, dt.T, a_neg.reshape(2 * N_SSD_HEADS, 1), dexp)
    ssd_gate = make_rowwise("ssd_gate", _fn_ssd_gate, [(D_INNER, bf16)], tm_pref=128)
    (ssd_out,), _ = ssd_gate((y, proj["z"]), (small["ssd_norm_w"],), (), ())
    so = lin["ssd_out"](ssd_out, wfull["ssd_out"], wgrads["ssd_out"])

    merge = make_rowwise("merge", _fn_merge, [(D_MODEL, bf16)])
    (merged,), _ = merge((ao, so, proj["ga"], proj["gs"]), (), (), ())
    mo = lin["o"](merged, wfull["o"], wgrads["o"])

    res_norm = make_rowwise("res_norm", _fn_res_norm, [(D_MODEL, f32), (D_MODEL, bf16)])
    (x1, h2), _ = res_norm((x_res, mo), (gate1, small["norm2_w"], scale2, shift2), (), ())
    ff = mlp(h2, wfull["mlp1"], wgrads["mlp1"], wfull["mlp2"], wgrads["mlp2"])
    loss_op = make_rowwise("loss", _fn_loss, [], [(1, 1)])
    _, (loss,) = loss_op((x1, ff), (gate2,), (), (target,))
    return loss[0, 0]


_BC1 = 1.0 - ADAM_B1 ** ADAM_STEP
_BC2 = 1.0 - ADAM_B2 ** ADAM_STEP


def _adamw(w, g, m, v):
    m = ADAM_B1 * m + (1.0 - ADAM_B1) * g
    v = ADAM_B2 * v + (1.0 - ADAM_B2) * (g * g)
    delta = -ADAM_LR * ((m / _BC1) / (jnp.sqrt(v / _BC2) + ADAM_EPS) + ADAM_WD * w)
    return delta, m, v


def _ada_fwd(c_all, w, b):
    n = w.shape[1]

    def body(c_ref, w_ref, b_ref, o_ref):
        o_ref[...] = jnp.dot(_silu(c_ref[...]), w_ref[...], precision=HIGHEST, preferred_element_type=f32) + b_ref[...]

    return pl.pallas_call(body, name="ada_fwd", out_shape=jax.ShapeDtypeStruct((N_DEV, n), f32),
                          compiler_params=_cparams())(c_all, w, b)


def _ada_bwd_adamw(c_all, dmod, w, m, v):
    d, n = w.shape
    tr = _pick(d, (256, 128))

    def body(c_ref, dm_ref, w_ref, m_ref, v_ref, g_ref, dl_ref, mo_ref, vo_ref):
        g = lax.dot_general(_silu(c_ref[...]), dm_ref[...], _DIMS["tn"], precision=HIGHEST,
                            preferred_element_type=f32)
        g_ref[...] = g
        dl_ref[...], mo_ref[...], vo_ref[...] = _adamw(w_ref[...], g, m_ref[...], v_ref[...])

    blk = pl.BlockSpec((tr, n), lambda i: (i, 0))
    return pl.pallas_call(
        body, name="ada_bwd_adamw", grid=(d // tr,),
        in_specs=[pl.BlockSpec((N_DEV, tr), lambda i: (0, i)), pl.BlockSpec((N_DEV, n), lambda i: (0, 0)), blk, blk, blk],
        out_specs=[blk] * 4, out_shape=[jax.ShapeDtypeStruct((d, n), f32)] * 4,
        compiler_params=_cparams(dimension_semantics=("parallel",)),
    )(c_all, dmod, w, m, v)


def _sum_over_mesh(g):
    def body(g_ref, o_ref):
        acc = g_ref[0]
        for d in range(1, N_DEV):
            acc = acc + g_ref[d]
        o_ref[...] = acc

    return pl.pallas_call(body, name="sum_small", out_shape=jax.ShapeDtypeStruct(g.shape[1:], f32),
                          compiler_params=_cparams())(g)


def _adamw_small(w, g, m, v):
    def body(w_ref, g_ref, m_ref, v_ref, dl_ref, mo_ref, vo_ref):
        dl_ref[...], mo_ref[...], vo_ref[...] = _adamw(w_ref[...], g_ref[...], m_ref[...], v_ref[...])

    return pl.pallas_call(body, name="adamw_small", out_shape=[jax.ShapeDtypeStruct(w.shape, f32)] * 3,
                          compiler_params=_cparams())(w, g, m, v)


def _sum_adamw(recv, w, m, v, name):
    _, r, c = recv.shape
    tr = _pick(r, (256, 128, 64, 16))

    def body(g_ref, w_ref, m_ref, v_ref, go_ref, dl_ref, mo_ref, vo_ref):
        g = g_ref[0].astype(f32)
        for d in range(1, N_DEV):
            g = g + g_ref[d].astype(f32)
        go_ref[...] = g
        dl_ref[...], mo_ref[...], vo_ref[...] = _adamw(w_ref[...], g, m_ref[...], v_ref[...])

    blk = pl.BlockSpec((tr, c), lambda i: (i, 0))
    return pl.pallas_call(
        body, name=name, grid=(r // tr,),
        in_specs=[pl.BlockSpec((N_DEV, tr, c), lambda i: (0, i, 0)), blk, blk, blk],
        out_specs=[blk] * 4, out_shape=[jax.ShapeDtypeStruct((r, c), f32)] * 4,
        compiler_params=_cparams(dimension_semantics=("parallel",)),
    )(recv, w, m, v)


def _pack_small(arrs):
    parts = []
    for a in arrs:
        flat = a.reshape(-1).astype(f32)
        parts.append(jnp.pad(flat, (0, (-flat.shape[0]) % LANE)))
    flat = jnp.concatenate(parts)
    flat = jnp.pad(flat, (0, (-flat.shape[0]) % (8 * LANE)))
    return flat.reshape(-1, LANE)


def _unpack_small(packed, shapes):
    flat = packed.reshape(-1)
    out, off = [], 0
    for shp in shapes:
        n = int(np.prod(shp))
        out.append(flat[off:off + n].reshape(shp))
        off += n + (-n) % LANE
    return out


BIG = ("w_attn_out", "w_ssd_out", "w_o", "w_mlp1", "w_mlp2")
BIG_ROWS = (N_Q_HEADS * HEAD_DIM // N_DEV, D_INNER // N_DEV, D_MODEL // N_DEV,
            D_MODEL * (D_FF // N_DEV) // PACK_COLS, D_FF // N_DEV)
N_IN_SHARD = D_IN_PROJ // N_DEV
assert sum(BIG_ROWS) % 16 == 0


def _pack_big(shards, dtype):
    return jnp.concatenate([s.astype(dtype).reshape(-1, PACK_COLS) for s in shards], axis=0)


def _unpack_big(packed, shapes):
    out, off = [], 0
    for rows, shp in zip(BIG_ROWS, shapes):
        out.append(packed[off:off + rows].reshape(shp))
        off += rows
    return out


LATE = ("attn_out", "ssd_out", "o", "mlp1", "mlp2")
LATE_SHAPES = ((N_Q_HEADS * HEAD_DIM, D_MODEL), (D_INNER, D_MODEL), (D_MODEL, D_MODEL), (D_MODEL, D_FF),
               (D_FF, D_MODEL))


def _split_w_in(g_in):
    w_in = g_in.transpose(1, 0, 2).reshape(D_MODEL, D_IN_PROJ)
    w = {}
    off = 0
    for name, size in zip(PROJ_NAMES, PROJ_SIZES):
        w[name] = w_in[:, off:off + size]
        off += size
    w["dt"] = jnp.pad(w["dt"], ((0, 0), (0, DT_PAD - 2 * N_SSD_HEADS)))
    return w


def _split_late(g):
    offs = np.cumsum((0,) + BIG_ROWS)
    sl = [g[:, offs[i]:offs[i + 1]] for i in range(len(BIG))]
    return {"attn_out": sl[0].reshape(LATE_SHAPES[0]), "ssd_out": sl[1].reshape(LATE_SHAPES[1]),
            "o": sl[2].reshape(LATE_SHAPES[2]),
            "mlp1": sl[3].reshape(N_DEV, D_MODEL, D_FF // N_DEV).transpose(1, 0, 2).reshape(LATE_SHAPES[3]),
            "mlp2": sl[4].reshape(LATE_SHAPES[4])}


def _pack_in_grads(gw):
    gw = {n: g.astype(bf16) for n, g in gw.items()}
    gw["dt"] = gw["dt"][:, :2 * N_SSD_HEADS]
    g_in = jnp.concatenate([gw[n] for n in PROJ_NAMES], axis=1)
    return g_in.reshape(D_MODEL, N_DEV, N_IN_SHARD).transpose(1, 0, 2)


def _pack_late_grads(gw):
    gw = {n: g.astype(bf16) for n, g in gw.items()}
    parts = [
        gw["attn_out"].reshape(N_DEV, -1, PACK_COLS),
        gw["ssd_out"].reshape(N_DEV, -1, PACK_COLS),
        gw["o"].reshape(N_DEV, -1, PACK_COLS),
        gw["mlp1"].reshape(D_MODEL, N_DEV, D_FF // N_DEV).transpose(1, 0, 2).reshape(N_DEV, -1, PACK_COLS),
        gw["mlp2"].reshape(N_DEV, -1, PACK_COLS),
    ]
    return jnp.concatenate(parts, axis=1)


SMALL = ("norm1_w", "norm2_w", "q_norm_w", "k_norm_w", "conv_w", "conv_b", "A_log", "dt_bias", "ssd_D", "ssd_norm_w")


def kernel(x, c, w_ada, b_ada, norm1_w, norm2_w, w_in, q_norm_w, k_norm_w, conv_w, conv_b, A_log, dt_bias, ssd_D, ssd_norm_w, w_attn_out, w_ssd_out, w_o, w_mlp1, w_mlp2, loss_target, m_w_ada, m_b_ada, m_norm1_w, m_norm2_w, m_w_in, m_q_norm_w, m_k_norm_w, m_conv_w, m_conv_b, m_A_log, m_dt_bias, m_ssd_D, m_ssd_norm_w, m_w_attn_out, m_w_ssd_out, m_w_o, m_w_mlp1, m_w_mlp2, v_w_ada, v_b_ada, v_norm1_w, v_norm2_w, v_w_in, v_q_norm_w, v_k_norm_w, v_conv_w, v_conv_b, v_A_log, v_dt_bias, v_ssd_D, v_ssd_norm_w, v_w_attn_out, v_w_ssd_out, v_w_o, v_w_mlp1, v_w_mlp2):
    args = dict(locals())
    me = _my_index()
    n_ada = 6 * D_MODEL // N_DEV
    n_cw = CONV_DIM // N_DEV

    blk = jnp.zeros((8, D_MODEL), f32)
    blk = blk.at[0:1, :].set(c)
    blk = blk.at[1:1 + D_CONV, :n_cw].set(conv_w[0])
    g0 = _all_gather(blk, "gather_c_convw", in_vmem=True)
    c_all = g0[:, 0, :]
    conv_w_full = g0[:, 1:1 + D_CONV, :n_cw].transpose(1, 0, 2).reshape(D_CONV, CONV_DIM)

    b_shard = lax.dynamic_slice(b_ada, (0, me * n_ada), (1, n_ada))
    mod_cols = _ada_fwd(c_all, w_ada[0], b_shard)
    g1 = _all_gather(mod_cols, "gather_mod", in_vmem=True)
    mod_mine = lax.dynamic_index_in_dim(g1, me, axis=1, keepdims=False)
    mod = mod_mine.reshape(6, 1, D_MODEL)

    big_shapes = [args[n].shape[1:] for n in BIG]
    late_shard = _pack_big([args[n][0] for n in BIG], bf16)
    wfull = _split_w_in(_all_gather(w_in[0].astype(bf16), "gather_w_in", in_vmem=False))
    recv_in_like = jnp.zeros((N_DEV,) + w_in.shape[1:], bf16)
    recv_late_like = jnp.zeros((N_DEV,) + late_shard.shape, bf16)

    small = {"norm1_w": norm1_w, "norm2_w": norm2_w, "q_norm_w": q_norm_w, "k_norm_w": k_norm_w,
             "conv_w": conv_w_full, "conv_b": conv_b, "A_log": A_log[0], "dt_bias": dt_bias[0], "ssd_D": ssd_D,
             "ssd_norm_w": ssd_norm_w}

    loss, (gx, gmod, gsmall, recv_in, recv_late) = jax.value_and_grad(local_loss, argnums=(0, 1, 2, 3, 4))(
        x[0], mod, small, recv_in_like, recv_late_like, wfull, late_shard, loss_target[0])

    small_list = [gmod, gsmall["norm1_w"], gsmall["norm2_w"], gsmall["q_norm_w"], gsmall["k_norm_w"], gsmall["conv_w"],
                  gsmall["conv_b"], gsmall["A_log"], gsmall["dt_bias"], gsmall["ssd_D"], gsmall["ssd_norm_w"],
                  loss.reshape(1)]
    small_shapes = [a.shape for a in small_list]
    g2 = _all_gather(_pack_small(small_list), "gather_small_grads", in_vmem=True)
    summed = _unpack_small(_sum_over_mesh(g2), small_shapes)
    loss_total = summed[-1][0]
    g_b_ada = summed[0].reshape(1, 6 * D_MODEL)
    g_small = dict(zip(SMALL, summed[1:-1]))
    g_conv_w = lax.dynamic_slice(g_small["conv_w"], (0, me * n_cw), (D_CONV, n_cw))

    dmod_all = g2[:, :6 * D_MODEL // LANE, :].reshape(N_DEV, 6 * D_MODEL)
    dmod_shard = lax.dynamic_slice(dmod_all, (0, me * n_ada), (N_DEV, n_ada))
    ada = _ada_bwd_adamw(c_all, dmod_shard, w_ada[0], m_w_ada[0], v_w_ada[0])

    small_grads = {"b_ada": g_b_ada, "norm1_w": g_small["norm1_w"], "norm2_w": g_small["norm2_w"],
                   "q_norm_w": g_small["q_norm_w"], "k_norm_w": g_small["k_norm_w"], "conv_w": g_conv_w[None],
                   "conv_b": g_small["conv_b"], "A_log": g_small["A_log"][None], "dt_bias": g_small["dt_bias"][None],
                   "ssd_D": g_small["ssd_D"], "ssd_norm_w": g_small["ssd_norm_w"]}
    sm_names = list(small_grads)
    sm_shapes = [args[n].shape for n in sm_names]
    sm = _adamw_small(_pack_small([args[n] for n in sm_names]), _pack_small([small_grads[n] for n in sm_names]),
                      _pack_small([args["m_" + n] for n in sm_names]), _pack_small([args["v_" + n] for n in sm_names]))
    sm_delta, sm_m, sm_v = [dict(zip(sm_names, _unpack_small(t, sm_shapes))) for t in sm]
    small_grads = {n: small_grads[n].reshape(args[n].shape) for n in sm_names}

    w_in_out = _sum_adamw(recv_in, w_in[0], m_w_in[0], v_w_in[0], "sum_adamw_w_in")
    big = _sum_adamw(recv_late, _pack_big([args[n][0] for n in BIG], f32),
                     _pack_big([args["m_" + n][0] for n in BIG], f32),
                     _pack_big([args["v_" + n][0] for n in BIG], f32), "sum_adamw")
    big_g, big_delta, big_m, big_v = [dict(zip(BIG, [t[None] for t in _unpack_big(p, big_shapes)])) for p in big]
    big_g["w_in"], big_delta["w_in"], big_m["w_in"], big_v["w_in"] = [t[None] for t in w_in_out]

    names = ("w_ada", "b_ada", "norm1_w", "norm2_w", "w_in", "q_norm_w", "k_norm_w", "conv_w", "conv_b", "A_log",
             "dt_bias", "ssd_D", "ssd_norm_w", "w_attn_out", "w_ssd_out", "w_o", "w_mlp1", "w_mlp2")
    grads, deltas, new_m, new_v = {}, {}, {}, {}
    for n in names:
        if n == "w_ada":
            grads[n], deltas[n], new_m[n], new_v[n] = [t[None] for t in ada]
        elif n in big_g:
            grads[n], deltas[n], new_m[n], new_v[n] = big_g[n], big_delta[n], big_m[n], big_v[n]
        else:
            grads[n], deltas[n], new_m[n], new_v[n] = small_grads[n], sm_delta[n], sm_m[n], sm_v[n]
    return (loss_total, gx[None], *[grads[n] for n in names], *[deltas[n] for n in names],
            *[new_m[n] for n in names], *[new_v[n] for n in names])
```

```python
import functools
import math

import jax
import jax.numpy as jnp
import numpy as np
from jax import lax
from jax.experimental import pallas as pl
from jax.experimental.pallas import tpu as pltpu

f32 = jnp.float32
bf16 = jnp.bfloat16
HIGHEST = lax.Precision.HIGHEST
MESH = pl.DeviceIdType.MESH

N_DEV = 8
D_MODEL = 1024
GRID_W = 64
N_Q_HEADS = 16
N_KV_HEADS = 4
HEAD_DIM = 64
ROPE_THETA = 10000.0
D_INNER = 2048
SSD_HEAD_DIM = 64
N_SSD_HEADS = 32
N_SSD_GROUPS = 4
D_STATE = 128
D_CONV = 5
CHUNK = 128
D_FF = 4096
EPS = 1e-6
CONV_DIM = D_INNER + 2 * N_SSD_GROUPS * D_STATE
GN = N_SSD_GROUPS * D_STATE
PROJ_NAMES = ("q", "k", "v", "xbc", "z", "dt", "ga", "gs")
PROJ_SIZES = (N_Q_HEADS * HEAD_DIM, N_KV_HEADS * HEAD_DIM, N_KV_HEADS * HEAD_DIM, CONV_DIM, D_INNER,
              2 * N_SSD_HEADS, D_MODEL, D_MODEL)
D_IN_PROJ = sum(PROJ_SIZES)
PROJ_DTYPES = (jnp.bfloat16, jnp.bfloat16, jnp.bfloat16, jnp.float32, jnp.bfloat16, jnp.float32, jnp.bfloat16,
               jnp.bfloat16)
DT_PAD = 128

ADAM_LR, ADAM_B1, ADAM_B2, ADAM_EPS, ADAM_WD, ADAM_STEP = 0.001, 0.9, 0.999, 1e-08, 0.01, 10

V7X_VMEM_LIMIT = 56 * 1024 * 1024
LANE = 128
PACK_COLS = 1024


def _cparams(**kw):
    return pltpu.CompilerParams(vmem_limit_bytes=V7X_VMEM_LIMIT, **kw)


def _pick(dim, prefs):
    for p in prefs:
        if dim % p == 0:
            return p
    return dim


def _my_index():
    return 4 * lax.axis_index("x") + 2 * lax.axis_index("y") + lax.axis_index("c")


COMM_SEMS = [pltpu.SemaphoreType.DMA((7,)), pltpu.SemaphoreType.DMA((7,)), pltpu.SemaphoreType.DMA]


def _gather_phases(x_ref, out_ref, send_sems, recv_sems, local_sem):
    x, y, cc = lax.axis_index("x"), lax.axis_index("y"), lax.axis_index("c")
    me, sibling = (x, y, cc), (x, y, 1 - cc)
    chips = [(1 - x, y), (x, 1 - y), (1 - x, 1 - y)]

    def slot(px, py, pc):
        return out_ref.at[4 * px + 2 * py + pc]

    def copy(k, blk, to, src=None):
        return pltpu.make_async_remote_copy(
            src_ref=slot(*blk) if src is None else src, dst_ref=slot(*blk),
            send_sem=send_sems.at[k], recv_sem=recv_sems.at[k], device_id=to, device_id_type=MESH)

    mine = pltpu.make_async_copy(x_ref, slot(*me), local_sem)
    first = [copy(0, me, sibling, src=x_ref)]
    first += [copy(1 + j, me, (*chip, cc), src=x_ref) for j, chip in enumerate(chips)]
    passed = [copy(4 + j, (*chip, cc), sibling) for j, chip in enumerate(chips)]

    def start():
        mine.start()
        for cp in first:
            cp.start()

    def finish():
        for j, chip in enumerate(chips):
            copy(1 + j, (*chip, cc), me).wait_recv()
            passed[j].start()
        copy(0, sibling, me).wait_recv()
        for j, chip in enumerate(chips):
            copy(4 + j, (*chip, 1 - cc), me).wait_recv()
        for cp in first + passed:
            cp.wait_send()
        mine.wait()

    return start, finish


def _scatter_phases(g_ref, out_ref, send_sems, recv_sems, local_sem):
    x, y, cc = lax.axis_index("x"), lax.axis_index("y"), lax.axis_index("c")
    me = 4 * x + 2 * y + cc
    mine = pltpu.make_async_copy(g_ref.at[me], out_ref.at[me], local_sem)

    def copy(k):
        fx, fy, fc = (k >> 2) & 1, (k >> 1) & 1, k & 1
        px = x + fx - 2 * x * fx
        py = y + fy - 2 * y * fy
        pc = cc + fc - 2 * cc * fc
        peer = 4 * px + 2 * py + pc
        send = pltpu.make_async_remote_copy(
            src_ref=g_ref.at[peer], dst_ref=out_ref.at[me],
            send_sem=send_sems.at[k - 1], recv_sem=recv_sems.at[k - 1],
            device_id=(px, py, pc), device_id_type=MESH)
        recv = pltpu.make_async_remote_copy(
            src_ref=g_ref.at[peer], dst_ref=out_ref.at[peer],
            send_sem=send_sems.at[k - 1], recv_sem=recv_sems.at[k - 1],
            device_id=(px, py, pc), device_id_type=MESH)
        return send, recv

    pairs = [copy(k) for k in range(1, N_DEV)]

    def start():
        mine.start()
        for send, _ in pairs:
            send.start()

    def finish():
        for _, recv in pairs:
            recv.wait_recv()
        for send, _ in pairs:
            send.wait_send()
        mine.wait()

    return start, finish


def _all_gather(block, name, in_vmem):
    r, c = block.shape

    def body(x_ref, out_ref, send_sems, recv_sems, local_sem):
        start, finish = _gather_phases(x_ref, out_ref, send_sems, recv_sems, local_sem)
        start()
        finish()

    space = pltpu.VMEM if in_vmem else pl.ANY
    return pl.pallas_call(
        body, name=name,
        out_shape=jax.ShapeDtypeStruct((N_DEV, r, c), block.dtype),
        in_specs=[pl.BlockSpec(memory_space=space)],
        out_specs=pl.BlockSpec(memory_space=space),
        scratch_shapes=[pltpu.SemaphoreType.DMA((7,)), pltpu.SemaphoreType.DMA((7,)), pltpu.SemaphoreType.DMA],
    )(block)


def _scatter_blocks(g, name):
    _, r, c = g.shape

    def body(g_ref, out_ref, send_sems, recv_sems, local_sem):
        start, finish = _scatter_phases(g_ref, out_ref, send_sems, recv_sems, local_sem)
        start()
        finish()

    return pl.pallas_call(
        body, name=name,
        out_shape=jax.ShapeDtypeStruct(g.shape, g.dtype),
        in_specs=[pl.BlockSpec(memory_space=pl.ANY)],
        out_specs=pl.BlockSpec(memory_space=pl.ANY),
        scratch_shapes=[pltpu.SemaphoreType.DMA((7,)), pltpu.SemaphoreType.DMA((7,)), pltpu.SemaphoreType.DMA],
    )(g)


_DIMS = {"nn": (((1,), (0,)), ((), ())), "nt": (((1,), (1,)), ((), ())), "tn": (((0,), (0,)), ((), ()))}


def _matmul(a, b, mode, out_dtype, name, epilogue=None, side=None):
    if mode == "nn":
        (m, k), (_, n) = a.shape, b.shape
    elif mode == "nt":
        (m, k), (n, _) = a.shape, b.shape
    else:
        (k, m), (_, n) = a.shape, b.shape
    tm = _pick(m, (1024, 512, 256, 128))
    if mode == "tn":
        tn = _pick(n, (1536, 1024, 512, 256, 128))
        tk = _pick(k, (2048, 1024, 512, 256, 128)) if b.dtype == bf16 else _pick(k, (1024, 512, 256, 128))
    else:
        tn = _pick(n, (1024, 512, 384, 256, 128))
        tk = _pick(k, (2048, 1024, 512, 256, 128)) if a.dtype == bf16 else _pick(k, (1024, 512, 256, 128))
    nk = k // tk
    dims = _DIMS[mode]
    n_in = 3 if epilogue == "drelu2" else 2

    def body(*refs):
        a_ref, b_ref = refs[:2]
        o_ref, acc_ref = refs[n_in], refs[n_in + 1]
        kk = pl.program_id(2)
        part = lax.dot_general(a_ref[...].astype(bf16), b_ref[...].astype(bf16), dims, preferred_element_type=f32)

        def finish(acc):
            if epilogue == "relu2":
                r = jnp.maximum(acc, 0.0)
                o_ref[...] = (r * r).astype(out_dtype)
            elif epilogue == "drelu2":
                o_ref[...] = (acc * (2.0 * jnp.sqrt(refs[2][...].astype(f32)))).astype(out_dtype)
            else:
                o_ref[...] = acc.astype(out_dtype)

        if nk == 1:
            finish(part)
        else:
            @pl.when(kk == 0)
            def _():
                acc_ref[...] = part

            @pl.when(kk > 0)
            def _():
                acc_ref[...] += part

            @pl.when(kk == nk - 1)
            def _():
                finish(acc_ref[...])

    if mode == "tn":
        a_spec = pl.BlockSpec((tk, tm), lambda i, j, kk: (kk, i))
    else:
        a_spec = pl.BlockSpec((tm, tk), lambda i, j, kk: (i, kk))
    if mode == "nt":
        b_spec = pl.BlockSpec((tn, tk), lambda i, j, kk: (j, kk))
    else:
        b_spec = pl.BlockSpec((tk, tn), lambda i, j, kk: (kk, j))
    o_spec = pl.BlockSpec((tm, tn), lambda i, j, kk: (i, j))
    o_shape = jax.ShapeDtypeStruct((m, n), out_dtype)
    return pl.pallas_call(
        body, name=name, grid=(m // tm, n // tn, nk),
        in_specs=[a_spec, b_spec] + ([o_spec] if epilogue == "drelu2" else []),
        out_specs=o_spec, out_shape=o_shape,
        scratch_shapes=[pltpu.VMEM((tm, tn), f32)],
        compiler_params=_cparams(dimension_semantics=("parallel", "parallel", "arbitrary")),
    )(*((a, b, side) if epilogue == "drelu2" else (a, b)))


@jax.custom_vjp
def mlp(h, w1, w1grad, w2, w2grad):
    r = _matmul(h, w1, "nn", bf16, "mlp1_fwd", epilogue="relu2")
    return _matmul(r, w2, "nn", f32, "mlp2_fwd")


def _mlp_fwd(h, w1, w1grad, w2, w2grad):
    r = _matmul(h, w1, "nn", bf16, "mlp1_fwd", epilogue="relu2")
    return _matmul(r, w2, "nn", f32, "mlp2_fwd"), (h, w1, w2, r)


def _mlp_bwd(res, dy):
    h, w1, w2, r = res
    du = _matmul(dy, w2, "nt", bf16, "mlp2_dgrad", epilogue="drelu2", side=r)
    dw2 = _matmul(r, dy, "tn", f32, "mlp2_wgrad")
    dh = _matmul(du, w1, "nt", h.dtype, "mlp1_dgrad")
    dw1 = _matmul(h, du, "tn", f32, "mlp1_wgrad")
    return dh, jnp.zeros_like(w1), dw1, jnp.zeros_like(w2), dw2


mlp.defvjp(_mlp_fwd, _mlp_bwd)


def make_linear(name):
    @jax.custom_vjp
    def linear(a, w, wgrad):
        return _matmul(a, w, "nn", f32, name + "_fwd")

    def fwd(a, w, wgrad):
        return linear(a, w, wgrad), (a, w)

    def bwd(res, dy):
        a, w = res
        da = _matmul(dy, w, "nt", a.dtype, name + "_dgrad")
        dw = _matmul(a, dy, "tn", f32, name + "_wgrad")
        return da, jnp.zeros_like(w), dw

    linear.defvjp(fwd, bwd)
    return linear


def _in_proj_dgrad(dys, ws, g):
    s, d = dys[0].shape[0], ws[0].shape[0]
    tm = _pick(s, (1024, 512, 256, 128))
    tks = [w.shape[1] if w.shape[1] <= 1024 else 512 for w in ws]
    steps = [w.shape[1] // tk for w, tk in zip(ws, tks)]
    starts = [sum(steps[:p]) for p in range(len(ws))]
    total = sum(steps)
    n_p, n_i = len(ws), s // tm
    assert steps[0] == 1

    def body(*refs):
        dy_refs, w_refs, g_ref = refs[:n_p], refs[n_p:2 * n_p], refs[2 * n_p]
        dh_ref, recv_ref, acc_ref, send_sems, recv_sems, local_sem = refs[2 * n_p + 1:]
        i, t = pl.program_id(0), pl.program_id(1)
        start, finish = _scatter_phases(g_ref, recv_ref, send_sems, recv_sems, local_sem)

        @pl.when((i == 0) & (t == 0))
        def _():
            start()

        for p in range(n_p):
            @pl.when((t >= starts[p]) & (t < starts[p] + steps[p]))
            def _(p=p):
                part = lax.dot_general(dy_refs[p][...].astype(bf16), w_refs[p][...], _DIMS["nt"],
                                       preferred_element_type=f32)
                if p == 0:
                    acc_ref[...] = part
                else:
                    acc_ref[...] += part

        @pl.when(t == total - 1)
        def _():
            dh_ref[...] = acc_ref[...].astype(dh_ref.dtype)

        @pl.when((i == n_i - 1) & (t == total - 1))
        def _():
            finish()

    def piece_map(p, rows):
        def index_map(i, t):
            blk = jnp.clip(t - starts[p], 0, steps[p] - 1)
            return (i, blk) if rows else (0, blk)

        return index_map

    hbm = pl.BlockSpec(memory_space=pl.ANY)
    in_specs = [pl.BlockSpec((tm, tks[p]), piece_map(p, True)) for p in range(n_p)]
    in_specs += [pl.BlockSpec((d, tks[p]), piece_map(p, False)) for p in range(n_p)]
    return pl.pallas_call(
        body, name="in_proj_dgrad", grid=(n_i, total), in_specs=in_specs + [hbm],
        out_specs=[pl.BlockSpec((tm, d), lambda i, t: (i, 0)), hbm],
        out_shape=[jax.ShapeDtypeStruct((s, d), bf16), jax.ShapeDtypeStruct(g.shape, g.dtype)],
        scratch_shapes=[pltpu.VMEM((tm, d), f32)] + COMM_SEMS,
        compiler_params=_cparams(dimension_semantics=("arbitrary", "arbitrary")),
    )(*dys, *ws, g)


@jax.custom_vjp
def in_proj(h, ws, recv_like):
    return tuple(_matmul(h, w, "nn", dt, "lin_" + n + "_fwd") for n, w, dt in zip(PROJ_NAMES, ws, PROJ_DTYPES))


def _in_proj_fwd(h, ws, recv_like):
    return in_proj(h, ws, recv_like), (h, ws)


def _in_proj_bwd(res, dys):
    h, ws = res
    dws = {n: _matmul(h, dy, "tn", f32, "lin_" + n + "_wgrad") for n, dy in zip(PROJ_NAMES, dys)}
    dh, recv = _in_proj_dgrad(dys, ws, _pack_in_grads(dws))
    return dh.astype(h.dtype), tuple(jnp.zeros_like(w) for w in ws), recv


in_proj.defvjp(_in_proj_fwd, _in_proj_bwd)


def make_rowwise(name, fn, row_out, sum_out=(), tm_pref=256):
    def specs(rows, gpars, cpars, consts, tm):
        s = [pl.BlockSpec((tm, r.shape[1]), lambda i: (i, 0)) for r in rows]
        s += [pl.BlockSpec(p.shape, lambda i: (0, 0)) for p in gpars]
        s += [pl.BlockSpec(p.shape, lambda i: (0, 0)) for p in cpars]
        for cst in consts:
            nb = cst.shape[0] // tm
            s.append(pl.BlockSpec((tm, cst.shape[1]), lambda i, nb=nb: (i % nb, 0)))
        return s

    def tile_rows(rows, consts):
        r = rows[0].shape[0]
        common = math.gcd(r, *[cst.shape[0] for cst in consts])
        tm = _pick(common, (tm_pref, 512, 256, 128, 64, 32, 16, 8))
        return r, tm

    def forward(rows, gpars, cpars, consts):
        r, tm = tile_rows(rows, consts)
        nr, ng, nc, nk = len(rows), len(gpars), len(cpars), len(consts)

        def body(*refs):
            ins = refs[:nr + ng + nc + nk]
            outs = refs[nr + ng + nc + nk:]
            rv = [t[...].astype(f32) for t in ins[:nr]]
            gv = [t[...].astype(f32) for t in ins[nr:nr + ng]]
            cv = [t[...] for t in ins[nr + ng:nr + ng + nc]]
            kv = [t[...].astype(f32) for t in ins[nr + ng + nc:]]
            ro, so = fn(rv, gv, cv, kv)
            for o_ref, val in zip(outs[:len(row_out)], ro):
                o_ref[...] = val.astype(o_ref.dtype)
            if sum_out:
                @pl.when(pl.program_id(0) == 0)
                def _():
                    for o_ref in outs[len(row_out):]:
                        o_ref[...] = jnp.zeros_like(o_ref)
                for o_ref, val in zip(outs[len(row_out):], so):
                    o_ref[...] += val

        out_specs = [pl.BlockSpec((tm, w), lambda i: (i, 0)) for w, _ in row_out]
        out_specs += [pl.BlockSpec(shp, lambda i: (0, 0)) for shp in sum_out]
        out_shape = [jax.ShapeDtypeStruct((r, w), dt) for w, dt in row_out]
        out_shape += [jax.ShapeDtypeStruct(shp, f32) for shp in sum_out]
        res = pl.pallas_call(
            body, name=name + "_fwd", grid=(r // tm,),
            in_specs=specs(rows, gpars, cpars, consts, tm), out_specs=out_specs, out_shape=out_shape,
            compiler_params=_cparams(dimension_semantics=("arbitrary",)),
        )(*rows, *gpars, *cpars, *consts)
        return tuple(res[:len(row_out)]), tuple(res[len(row_out):])

    def backward(rows, gpars, cpars, consts, d_ro, d_so):
        r, tm = tile_rows(rows, consts)
        nr, ng, nc, nk = len(rows), len(gpars), len(cpars), len(consts)
        n_in = nr + ng + nc + nk + len(row_out) + len(sum_out)

        def body(*refs):
            ins, outs = refs[:n_in], refs[n_in:]
            rv = [t[...].astype(f32) for t in ins[:nr]]
            gv = [t[...].astype(f32) for t in ins[nr:nr + ng]]
            cv = [t[...] for t in ins[nr + ng:nr + ng + nc]]
            kv = [t[...].astype(f32) for t in ins[nr + ng + nc:nr + ng + nc + nk]]
            o = nr + ng + nc + nk
            dro = [t[...].astype(f32) for t in ins[o:o + len(row_out)]]
            dso = [t[...] for t in ins[o + len(row_out):]]
            _, vjp = jax.vjp(lambda a, b: tuple(tuple(t) for t in fn(a, b, cv, kv)), rv, gv)
            drv, dgv = vjp((tuple(dro), tuple(dso)))
            for o_ref, val in zip(outs[:nr], drv):
                o_ref[...] = val.astype(o_ref.dtype)
            if ng:
                @pl.when(pl.program_id(0) == 0)
                def _():
                    for o_ref in outs[nr:]:
                        o_ref[...] = jnp.zeros_like(o_ref)
                for o_ref, val in zip(outs[nr:], dgv):
                    o_ref[...] += val

        in_specs = specs(rows, gpars, cpars, consts, tm)
        in_specs += [pl.BlockSpec((tm, w), lambda i: (i, 0)) for w, _ in row_out]
        in_specs += [pl.BlockSpec(shp, lambda i: (0, 0)) for shp in sum_out]
        out_specs = [pl.BlockSpec((tm, t.shape[1]), lambda i: (i, 0)) for t in rows]
        out_specs += [pl.BlockSpec(p.shape, lambda i: (0, 0)) for p in gpars]
        out_shape = [jax.ShapeDtypeStruct(t.shape, t.dtype) for t in rows]
        out_shape += [jax.ShapeDtypeStruct(p.shape, f32) for p in gpars]
        res = pl.pallas_call(
            body, name=name + "_bwd", grid=(r // tm,),
            in_specs=in_specs, out_specs=out_specs, out_shape=out_shape,
            compiler_params=_cparams(dimension_semantics=("arbitrary",)),
        )(*rows, *gpars, *cpars, *consts, *d_ro, *d_so)
        return tuple(res[:nr]), tuple(res[nr:])

    @jax.custom_vjp
    def op(rows, gpars, cpars, consts):
        return forward(rows, gpars, cpars, consts)

    def op_fwd(rows, gpars, cpars, consts):
        return forward(rows, gpars, cpars, consts), (rows, gpars, cpars, consts)

    def op_bwd(res, cts):
        rows, gpars, cpars, consts = res
        d_ro, d_so = cts
        drows, dg = backward(rows, gpars, cpars, consts, d_ro, d_so)
        dg = tuple(d.astype(p.dtype) for d, p in zip(dg, gpars))
        return (drows, dg, tuple(jnp.zeros_like(p) for p in cpars), tuple(jnp.zeros_like(k) for k in consts))

    op.defvjp(op_fwd, op_bwd)
    return op


def _rms(x):
    return x * lax.rsqrt(jnp.mean(x * x, axis=-1, keepdims=True) + EPS)


def _silu(x):
    return x * jax.nn.sigmoid(x)


def _fn_norm_mod(rows, gp, cp, ks):
    (x,), (nw, sc, sh) = rows, gp
    return ((_rms(x) * nw) * (1.0 + sc) + sh, x), ()


PAIR = 2 * HEAD_DIM


def _exact_dot(a, m):
    hi = a.astype(bf16)
    lo = (a - hi.astype(f32)).astype(bf16)
    return jnp.dot(hi, m, preferred_element_type=f32) + jnp.dot(lo, m, preferred_element_type=f32)


def _make_sel_dot(sign):
    @jax.custom_vjp
    def sel_dot(a, m):
        return _exact_dot(a, m)

    def fwd(a, m):
        return _exact_dot(a, m), m

    def bwd(m, g):
        return sign * _exact_dot(g, m), jnp.zeros_like(m)

    sel_dot.defvjp(fwd, bwd)
    return sel_dot


_head_sum_dot = _make_sel_dot(1.0)
_rope_perm_dot = _make_sel_dot(-1.0)


def _pair_norm_rope(t, w2, gsum, perm, cos2, sin2, out_scale):
    ss = _head_sum_dot(t * t, gsum)
    u = t * lax.rsqrt(ss * (1.0 / HEAD_DIM) + EPS) * w2
    return (u * cos2 + _rope_perm_dot(u, perm) * sin2) * out_scale


def _pair_consts():
    eye = np.eye(2, dtype=np.float32)
    gsum = np.kron(eye, np.ones((HEAD_DIM, HEAD_DIM), np.float32))
    return jnp.asarray(gsum, bf16), jnp.asarray(np.kron(eye, _rope_perm()), bf16)


def make_head_rope(name, nh, out_scale, head_major):
    width = nh * HEAD_DIM
    fn = functools.partial(_pair_norm_rope, out_scale=out_scale)

    def out_spec(tm):
        if head_major:
            return pl.BlockSpec((nh, tm, HEAD_DIM), lambda i: (0, i, 0))
        return pl.BlockSpec((tm, width), lambda i: (i, 0))

    def specs(tm):
        def full(shp):
            return pl.BlockSpec(shp, lambda i: (0, 0))

        return [pl.BlockSpec((tm, width), lambda i: (i, 0)), full((1, PAIR)), full((PAIR, PAIR)), full((PAIR, PAIR)),
                pl.BlockSpec((tm, PAIR), lambda i: (i, 0)), pl.BlockSpec((tm, PAIR), lambda i: (i, 0))]

    def forward(t, w2, gsum, perm, cos2, sin2):
        s = t.shape[0]
        tm = _pick(s, (512, 256, 128))

        def body(t_ref, w_ref, g_ref, p_ref, cos_ref, sin_ref, o_ref):
            for b in range(nh // 2):
                val = fn(t_ref[:, b * PAIR:(b + 1) * PAIR].astype(f32), w_ref[...], g_ref[...], p_ref[...], cos_ref[...],
                         sin_ref[...]).astype(o_ref.dtype)
                if head_major:
                    o_ref[2 * b] = val[:, :HEAD_DIM]
                    o_ref[2 * b + 1] = val[:, HEAD_DIM:]
                else:
                    o_ref[:, b * PAIR:(b + 1) * PAIR] = val

        return pl.pallas_call(
            body, name=name + "_fwd", grid=(s // tm,), in_specs=specs(tm), out_specs=out_spec(tm),
            out_shape=jax.ShapeDtypeStruct((nh, s, HEAD_DIM) if head_major else (s, width), bf16),
            compiler_params=_cparams(dimension_semantics=("arbitrary",)),
        )(t, w2, gsum, perm, cos2, sin2)

    def backward(t, w2, gsum, perm, cos2, sin2, dout):
        s = t.shape[0]
        tm = _pick(s, (512, 256, 128))

        def body(t_ref, w_ref, g_ref, p_ref, cos_ref, sin_ref, do_ref, dt_ref, dw_ref, pair_buf):
            @pl.when(pl.program_id(0) == 0)
            def _():
                dw_ref[...] = jnp.zeros_like(dw_ref)

            g_v, p_v, cos_v, sin_v = g_ref[...], p_ref[...], cos_ref[...], sin_ref[...]
            dw = jnp.zeros((1, PAIR), f32)
            for b in range(nh // 2):
                sl = slice(b * PAIR, (b + 1) * PAIR)
                if head_major:
                    pair_buf[:, :HEAD_DIM] = do_ref[2 * b].astype(f32)
                    pair_buf[:, HEAD_DIM:] = do_ref[2 * b + 1].astype(f32)
                    ct = pair_buf[...]
                else:
                    ct = do_ref[:, sl].astype(f32)
                _, vjp = jax.vjp(lambda a, c: fn(a, c, g_v, p_v, cos_v, sin_v), t_ref[:, sl].astype(f32), w_ref[...])
                dtb, dwb = vjp(ct)
                dt_ref[:, sl] = dtb.astype(dt_ref.dtype)
                dw = dw + dwb
            dw_ref[...] += dw

        return pl.pallas_call(
            body, name=name + "_bwd", grid=(s // tm,), in_specs=specs(tm) + [out_spec(tm)],
            out_specs=[pl.BlockSpec((tm, width), lambda i: (i, 0)), pl.BlockSpec((1, PAIR), lambda i: (0, 0))],
            out_shape=[jax.ShapeDtypeStruct((s, width), t.dtype), jax.ShapeDtypeStruct((1, PAIR), f32)],
            scratch_shapes=[pltpu.VMEM((tm, PAIR), f32)],
            compiler_params=_cparams(dimension_semantics=("arbitrary",)),
        )(t, w2, gsum, perm, cos2, sin2, dout)

    @jax.custom_vjp
    def op(t, w2, gsum, perm, cos2, sin2):
        return forward(t, w2, gsum, perm, cos2, sin2)

    def op_fwd(*args):
        return forward(*args), args

    def op_bwd(res, dout):
        dt, dw = backward(*res, dout)
        return (dt, dw) + tuple(jnp.zeros_like(r) for r in res[2:])

    op.defvjp(op_fwd, op_bwd)

    def apply(t, w, cos2, sin2):
        gsum, perm = _pair_consts()
        return op(t, jnp.concatenate([w, w], axis=-1), gsum, perm, cos2, sin2)

    return apply


def _fn_softplus(rows, gp, cp, ks):
    (x,), (b,) = rows, gp
    v = x + b
    return (jnp.maximum(v, 0.0) + jnp.log(1.0 + jnp.exp(-jnp.abs(v))),), ()


def _fn_ssd_gate(rows, gp, cp, ks):
    (y, z), (nw,) = rows, gp
    return (_rms(y * _silu(z)) * nw,), ()


def _fn_merge(rows, gp, cp, ks):
    ao, so, ga, gs = rows
    return (jax.nn.sigmoid(ga) * ao + jax.nn.sigmoid(gs) * so,), ()


def _fn_res_norm(rows, gp, cp, ks):
    (x, mo), (g1, nw, sc, sh) = rows, gp
    x1 = x + g1 * mo
    return (x1, (_rms(x1) * nw) * (1.0 + sc) + sh), ()


def _fn_loss(rows, gp, cp, ks):
    (x1, ff), (g2,), (tgt,) = rows, gp, ks
    err = x1 + g2 * ff - tgt
    return (), (0.5 * jnp.sum(jnp.sum(err * err, axis=-1, keepdims=True), axis=0, keepdims=True) / D_MODEL,)


HALO = 8
HALO_BWD = 16


def _conv_tiles(s, c):
    return _pick(s, (512, 256, 128)), _pick(c, (512, 256, 128))


def _halo_specs(tm, tc, s, halo=HALO):
    nb = tm // halo
    last = s // halo - 1
    cur = pl.BlockSpec((tm, tc), lambda j, i: (i, j))
    prev = pl.BlockSpec((halo, tc), lambda j, i: (jnp.maximum(i * nb - 1, 0), j))
    nxt = pl.BlockSpec((halo, tc), lambda j, i: (jnp.minimum((i + 1) * nb, last), j))
    return cur, prev, nxt


def _fill_halo(buf, cur, prev, nxt, tm, i, n_i, halo=HALO):
    buf[halo:halo + tm, :] = cur[...]
    buf[0:halo, :] = jnp.where(i > 0, prev[...], 0.0)
    buf[halo + tm:, :] = jnp.where(i < n_i - 1, nxt[...], 0.0)


def _conv_fwd(x, w, b, shard):
    s, c = x.shape
    tm, tc = _conv_tiles(s, c)
    n_i, n_j = s // tm, c // tc

    def body(cur, prev, nxt, w_ref, b_ref, shard_ref, o_ref, gath_ref, buf, send_sems, recv_sems, local_sem):
        j, i = pl.program_id(0), pl.program_id(1)
        start, finish = _gather_phases(shard_ref, gath_ref, send_sems, recv_sems, local_sem)

        @pl.when((j == 0) & (i == 0))
        def _():
            start()

        _fill_halo(buf, cur, prev, nxt, tm, i, n_i)
        pre = jnp.zeros((tm, tc), f32) + b_ref[...]
        for k in range(D_CONV):
            pre = pre + buf[HALO - 2 + k:HALO - 2 + k + tm, :] * w_ref[k:k + 1, :]
        o_ref[...] = _silu(pre)

        @pl.when((j == n_j - 1) & (i == n_i - 1))
        def _():
            finish()

    cur, prev, nxt = _halo_specs(tm, tc, s)
    hbm = pl.BlockSpec(memory_space=pl.ANY)
    return pl.pallas_call(
        body, name="conv_silu_fwd", grid=(n_j, n_i),
        in_specs=[cur, prev, nxt, pl.BlockSpec((D_CONV, tc), lambda j, i: (0, j)),
                  pl.BlockSpec((1, tc), lambda j, i: (0, j)), hbm],
        out_specs=[pl.BlockSpec((tm, tc), lambda j, i: (i, j)), hbm],
        out_shape=[jax.ShapeDtypeStruct((s, c), f32), jax.ShapeDtypeStruct((N_DEV,) + shard.shape, shard.dtype)],
        scratch_shapes=[pltpu.VMEM((tm + 2 * HALO, tc), f32)] + COMM_SEMS,
        compiler_params=_cparams(dimension_semantics=("arbitrary", "arbitrary")),
    )(x, x, x, w, b, shard)


def _conv_bwd(x, w, b, dy, g):
    s, c = x.shape
    tm, tc = _conv_tiles(s, c)
    n_i, n_j = s // tm, c // tc
    ext = tm + 16

    def body(cur, prev, nxt, dcur, dprev, dnxt, w_ref, b_ref, g_ref, dx_ref, dw_ref, db_ref, recv_ref,
             xbuf, dbuf, pbuf, send_sems, recv_sems, local_sem):
        j, i = pl.program_id(0), pl.program_id(1)
        start, finish = _scatter_phases(g_ref, recv_ref, send_sems, recv_sems, local_sem)

        @pl.when((j == 0) & (i == 0))
        def _():
            start()

        _fill_halo(xbuf, cur, prev, nxt, tm, i, n_i, HALO_BWD)
        _fill_halo(dbuf, dcur, dprev, dnxt, tm, i, n_i, HALO_BWD)
        xs = [xbuf[6 + k:6 + k + ext, :] for k in range(D_CONV)]
        pre = jnp.zeros((ext, tc), f32) + b_ref[...]
        for k in range(D_CONV):
            pre = pre + xs[k] * w_ref[k:k + 1, :]
        sg = jax.nn.sigmoid(pre)
        pbuf[...] = dbuf[8:8 + ext, :] * (sg * (1.0 + pre * (1.0 - sg)))
        dx = jnp.zeros((tm, tc), f32)
        for k in range(D_CONV):
            dx = dx + pbuf[10 - k:10 - k + tm, :] * w_ref[k:k + 1, :]
        dx_ref[...] = dx

        @pl.when(i == 0)
        def _():
            dw_ref[...] = jnp.zeros_like(dw_ref)
            db_ref[...] = jnp.zeros_like(db_ref)

        dpre = pbuf[8:8 + tm, :]
        db_ref[...] += jnp.sum(dpre, axis=0, keepdims=True)
        for k in range(D_CONV):
            dw_ref[k:k + 1, :] += jnp.sum(dpre * xs[k][8:8 + tm, :], axis=0, keepdims=True)

        @pl.when((j == n_j - 1) & (i == n_i - 1))
        def _():
            finish()

    cur, prev, nxt = _halo_specs(tm, tc, s, HALO_BWD)
    hbm = pl.BlockSpec(memory_space=pl.ANY)
    return pl.pallas_call(
        body, name="conv_silu_bwd", grid=(n_j, n_i),
        in_specs=[cur, prev, nxt, cur, prev, nxt, pl.BlockSpec((D_CONV, tc), lambda j, i: (0, j)),
                  pl.BlockSpec((1, tc), lambda j, i: (0, j)), hbm],
        out_specs=[pl.BlockSpec((tm, tc), lambda j, i: (i, j)), pl.BlockSpec((D_CONV, tc), lambda j, i: (0, j)),
                   pl.BlockSpec((1, tc), lambda j, i: (0, j)), hbm],
        out_shape=[jax.ShapeDtypeStruct((s, c), f32), jax.ShapeDtypeStruct((D_CONV, c), f32),
                   jax.ShapeDtypeStruct((1, c), f32), jax.ShapeDtypeStruct(g.shape, g.dtype)],
        scratch_shapes=[pltpu.VMEM((tm + 2 * HALO_BWD, tc), f32), pltpu.VMEM((tm + 2 * HALO_BWD, tc), f32),
                        pltpu.VMEM((ext, tc), f32)] + COMM_SEMS,
        compiler_params=_cparams(dimension_semantics=("arbitrary", "arbitrary")),
    )(x, x, x, dy, dy, dy, w, b, g)


@jax.custom_vjp
def conv_silu_comm(x, w, b, shard, recv_like):
    act, gathered = _conv_fwd(x, w, b, shard)
    return (act, gathered) + tuple(jnp.zeros(shp, f32) for shp in LATE_SHAPES)


def _conv_silu_comm_fwd(x, w, b, shard, recv_like):
    return conv_silu_comm(x, w, b, shard, recv_like), (x, w, b, shard)


def _conv_silu_comm_bwd(res, cts):
    x, w, b, shard = res
    dx, dw, db, recv = _conv_bwd(x, w, b, cts[0], _pack_late_grads(dict(zip(LATE, cts[2:]))))
    return dx, dw, db, jnp.zeros_like(shard), recv


conv_silu_comm.defvjp(_conv_silu_comm_fwd, _conv_silu_comm_bwd)


ATT_SCALE = HEAD_DIM ** -0.5
Q_SCALE = ATT_SCALE * math.log2(math.e)
LN2 = math.log(2.0)
REP = N_Q_HEADS // N_KV_HEADS


HP = 2
assert REP % HP == 0


def _attn_fwd(q, k, v):
    s, dh = q.shape[0], HEAD_DIM
    hq = q.shape[1] // dh
    tq = _pick(s, (256, 128))

    v1 = jnp.concatenate([v, jnp.ones(v.shape[:2] + (1,), v.dtype), jnp.zeros(v.shape[:2] + (dh - 1,), v.dtype)],
                         axis=-1)

    def body(q_ref, k_ref, v_ref, o_ref, p_ref, linv_ref):
        for j in range(HP):
            sl = slice(j * dh, (j + 1) * dh)
            sc = lax.dot_general(q_ref[:, sl], k_ref[0], _DIMS["nt"], preferred_element_type=f32)
            m = jnp.max(sc, axis=-1, keepdims=True)
            p = jnp.exp2(sc - m).astype(bf16)
            p_ref[j] = p
            o1 = jnp.dot(p, v_ref[0], preferred_element_type=f32)
            linv = 1.0 / o1[:, dh:dh + 1]
            o_ref[:, sl] = (o1[:, :dh] * linv).astype(o_ref.dtype)
            linv_ref[j] = linv

    return pl.pallas_call(
        body, name="attn_fwd", grid=(hq // HP, s // tq),
        in_specs=[pl.BlockSpec((tq, HP * dh), lambda h, i: (i, h)),
                  pl.BlockSpec((1, s, dh), lambda h, i: (h * HP // REP, 0, 0)),
                  pl.BlockSpec((1, s, 2 * dh), lambda h, i: (h * HP // REP, 0, 0))],
        out_specs=[pl.BlockSpec((tq, HP * dh), lambda h, i: (i, h)),
                   pl.BlockSpec((HP, tq, s), lambda h, i: (h, i, 0)),
                   pl.BlockSpec((HP, tq, 1), lambda h, i: (h, i, 0))],
        out_shape=[jax.ShapeDtypeStruct((s, hq * dh), bf16), jax.ShapeDtypeStruct((hq, s, s), bf16),
                   jax.ShapeDtypeStruct((hq, s, 1), f32)],
        compiler_params=_cparams(dimension_semantics=("parallel", "arbitrary")),
    )(q, k, v1)


def _attn_bwd(p, do, o, q, k, v, linv):
    hq, s, _ = p.shape
    dh = HEAD_DIM
    tq = _pick(s, (256, 128))

    def body(p_ref, do_ref, o_ref, q_ref, k_ref, v_ref, linv_ref, dq_ref, dkt_ref, dvt_ref):
        @pl.when(pl.program_id(1) == 0)
        def _():
            dkt_ref[...] = jnp.zeros_like(dkt_ref)
            dvt_ref[...] = jnp.zeros_like(dvt_ref)

        for j in range(HP):
            sl = slice(j * dh, (j + 1) * dh)
            pp, doh, li = p_ref[j], do_ref[:, sl], linv_ref[j]
            do32 = doh.astype(f32)
            d = jnp.sum(do32 * o_ref[:, sl].astype(f32), axis=-1, keepdims=True)
            dp = lax.dot_general(doh, v_ref[0], _DIMS["nt"], preferred_element_type=f32)
            ds = (pp.astype(f32) * ((dp - d) * li)).astype(bf16)
            dq_ref[:, sl] = (jnp.dot(ds, k_ref[0], preferred_element_type=f32) * LN2).astype(dq_ref.dtype)
            dvt_ref[j] += lax.dot_general((do32 * li).astype(bf16), pp, _DIMS["tn"], preferred_element_type=f32)
            dkt_ref[j] += lax.dot_general(q_ref[:, sl], ds, _DIMS["tn"], preferred_element_type=f32)

    def row():
        return pl.BlockSpec((tq, HP * dh), lambda h, i: (i, h))

    return pl.pallas_call(
        body, name="attn_bwd", grid=(hq // HP, s // tq),
        in_specs=[pl.BlockSpec((HP, tq, s), lambda h, i: (h, i, 0)), row(), row(), row(),
                  pl.BlockSpec((1, s, dh), lambda h, i: (h * HP // REP, 0, 0)),
                  pl.BlockSpec((1, s, dh), lambda h, i: (h * HP // REP, 0, 0)),
                  pl.BlockSpec((HP, tq, 1), lambda h, i: (h, i, 0))],
        out_specs=[row(), pl.BlockSpec((HP, dh, s), lambda h, i: (h, 0, 0)),
                   pl.BlockSpec((HP, dh, s), lambda h, i: (h, 0, 0))],
        out_shape=[jax.ShapeDtypeStruct((s, hq * dh), q.dtype), jax.ShapeDtypeStruct((hq, dh, s), f32),
                   jax.ShapeDtypeStruct((hq, dh, s), f32)],
        compiler_params=_cparams(dimension_semantics=("parallel", "arbitrary")),
    )(p, do, o, q, k, v, linv)


@jax.custom_vjp
def attention(q, k, v):
    return _attn_fwd(q, k, v)[0]


def _attention_fwd(q, k, v):
    o, p, linv = _attn_fwd(q, k, v)
    return o, (q, k, v, o, p, linv)


def _attention_bwd(res, do):
    q, k, v, o, p, linv = res
    s = q.shape[0]
    dq, dkt, dvt = _attn_bwd(p, do.astype(bf16), o, q, k, v, linv)

    def per_kv_head(t):
        return jnp.swapaxes(t.reshape(N_KV_HEADS, REP, HEAD_DIM, s).sum(axis=1), 1, 2)

    return dq, (per_kv_head(dkt) * LN2).astype(k.dtype), per_kv_head(dvt).astype(v.dtype)


attention.defvjp(_attention_fwd, _attention_bwd)


HPG = N_SSD_HEADS // N_SSD_GROUPS
GW = HPG * SSD_HEAD_DIM
NEG = -1e30
SPLIT_ROWS = 32


def _ssd_consts():
    k = np.arange(SPLIT_ROWS)[:, None]
    live = k < 3 * HPG
    sel_chunk = ((k % HPG) == (np.arange(HPG * CHUNK)[None, :] // CHUNK)) & live
    sel_head = ((k % HPG) == (np.arange(GW)[None, :] // SSD_HEAD_DIM)) & live
    return jnp.asarray(sel_chunk, bf16), jnp.asarray(sel_head, bf16)


def _split3(x):
    hi = x.astype(bf16).astype(f32)
    r1 = x - hi
    mid = r1.astype(bf16).astype(f32)
    lo = (r1 - mid).astype(bf16).astype(f32)
    return jnp.concatenate([hi, mid, lo, jnp.zeros_like(hi)], axis=0).astype(bf16)


def _tn(a, b):
    return lax.dot_general(a, b, _DIMS["tn"], preferred_element_type=f32)


def _nt(a, b):
    return lax.dot_general(a, b, _DIMS["nt"], preferred_element_type=f32)


def _nn(a, b):
    return jnp.dot(a, b, preferred_element_type=f32)


def _head_sum(sel8, x):
    hi = x.astype(bf16)
    lo = (x - hi.astype(f32)).astype(bf16)
    return _nt(sel8, hi) + _nt(sel8, lo)


def _ssd_masks(reverse):
    r = lax.broadcasted_iota(jnp.int32, (CHUNK, CHUNK), 0)
    c = lax.broadcasted_iota(jnp.int32, (CHUNK, CHUNK), 1)
    lower, upper = r >= c, r <= c
    return (upper, lower) if reverse else (lower, upper)


def _ssd_in_specs(cidx):
    return [pl.BlockSpec((CHUNK, D_INNER), lambda c: (cidx(c), 0)),
            pl.BlockSpec((CHUNK, GN), lambda c: (cidx(c), D_INNER // GN)),
            pl.BlockSpec((CHUNK, GN), lambda c: (cidx(c), D_INNER // GN + 1)),
            pl.BlockSpec((N_SSD_HEADS, CHUNK), lambda c: (0, cidx(c))),
            pl.BlockSpec((N_SSD_HEADS, 1), lambda c: (0, 0)),
            pl.BlockSpec((SPLIT_ROWS, HPG * CHUNK), lambda c: (0, 0)),
            pl.BlockSpec((SPLIT_ROWS, GW), lambda c: (0, 0))]


def _ssd_chunk_common(dtt_ref, a_ref, et_ref, mask_t):
    dtt = dtt_ref[...]
    et = jnp.dot(dtt * a_ref[...], mask_t.astype(f32), precision=HIGHEST, preferred_element_type=f32)
    et_ref[...] = et
    return dtt, et


def _ssd_group_common(g, dtt, et, selc_ref, selh_ref, xs_ref, b_ref, c_ref, last):
    gr = slice(g * HPG, (g + 1) * HPG)
    e3 = _split3(et[gr])
    col = _tn(e3, selc_ref[...])
    eb = _tn(e3, selh_ref[...])
    dtb = _tn(_split3(dtt[gr]), selh_ref[...])
    tbc = eb[last:last + 1, :]
    xs = xs_ref[:, g * GW:(g + 1) * GW]
    bg = b_ref[:, g * D_STATE:(g + 1) * D_STATE].astype(bf16)
    cg = c_ref[:, g * D_STATE:(g + 1) * D_STATE].astype(bf16)
    return col, eb, dtb, tbc, xs, bg, cg


def _ssd_fwd(xbc, dtt, a_col, reverse, y_prev=None, dexp=None):
    s = xbc.shape[0]
    nc = s // CHUNK
    cidx = (lambda c: nc - 1 - c) if reverse else (lambda c: c)
    last = 0 if reverse else CHUNK - 1
    selc, selh = _ssd_consts()
    final = y_prev is not None
    n_in = 9 if final else 7

    def body(*refs):
        xs_ref, b_ref, c_ref, dtt_ref, a_ref, selc_ref, selh_ref = refs[:7]
        y_ref, st_ref, ht_ref, et_ref = refs[n_in:]

        @pl.when(pl.program_id(0) == 0)
        def _():
            ht_ref[...] = jnp.zeros_like(ht_ref)

        mask, mask_t = _ssd_masks(reverse)
        dtt_v, et = _ssd_chunk_common(dtt_ref, a_ref, et_ref, mask_t)
        for g in range(N_SSD_GROUPS):
            col, eb, dtb, tbc, xs, bg, cg = _ssd_group_common(g, dtt_v, et, selc_ref, selh_ref, xs_ref, b_ref, c_ref,
                                                              last)
            xd = xs * dtb
            cb = _nt(cg, bg)
            ht = ht_ref[g]
            st_ref[0, g] = ht
            yoff = _nn(cg, ht.astype(bf16)) * jnp.exp(eb)
            for j in range(HPG):
                h = g * HPG + j
                hs = slice(j * SSD_HEAD_DIM, (j + 1) * SSD_HEAD_DIM)
                lam = jnp.exp(jnp.where(mask, col[:, j * CHUNK:(j + 1) * CHUNK] - et_ref[h:h + 1, :], NEG))
                yj = _nn((cb * lam).astype(bf16), xd[:, hs].astype(bf16)) + yoff[:, hs]
                cols = slice(g * GW + j * SSD_HEAD_DIM, g * GW + (j + 1) * SSD_HEAD_DIM)
                if final:
                    yj = yj + refs[7][:, cols] + xs[:, hs] * refs[8][:, cols]
                y_ref[:, cols] = yj.astype(y_ref.dtype)
            ht_ref[g] = jnp.exp(tbc) * ht + _tn(bg, (xd * jnp.exp(tbc - eb)).astype(bf16))

    y_spec = pl.BlockSpec((CHUNK, D_INNER), lambda c: (cidx(c), 0))
    extra_specs = [y_spec, pl.BlockSpec((1, D_INNER), lambda c: (0, 0))] if final else []
    return pl.pallas_call(
        body, name="ssd_fwd_rev" if reverse else "ssd_fwd", grid=(nc,),
        in_specs=_ssd_in_specs(cidx) + extra_specs,
        out_specs=[y_spec, pl.BlockSpec((1, N_SSD_GROUPS, D_STATE, GW), lambda c: (cidx(c), 0, 0, 0))],
        out_shape=[jax.ShapeDtypeStruct((s, D_INNER), bf16 if final else f32),
                   jax.ShapeDtypeStruct((nc, N_SSD_GROUPS, D_STATE, GW), f32)],
        scratch_shapes=[pltpu.VMEM((N_SSD_GROUPS, D_STATE, GW), f32), pltpu.VMEM((N_SSD_HEADS, CHUNK), f32)],
        compiler_params=_cparams(dimension_semantics=("arbitrary",)),
    )(xbc, xbc, xbc, dtt, a_col, selc, selh, *((y_prev, dexp) if final else ()))


def _ssd_bwd(xbc, dtt, a_col, states, dy, reverse, dxbc_prev=None, dexp=None):
    s = xbc.shape[0]
    nc = s // CHUNK
    cidx = (lambda c: c) if reverse else (lambda c: nc - 1 - c)
    last = 0 if reverse else CHUNK - 1
    selc, selh = _ssd_consts()
    final = dxbc_prev is not None
    n_in = 11 if final else 9
    n_out = 4 if final else 3

    def body(*refs):
        xs_ref, b_ref, c_ref, dtt_ref, a_ref, selc_ref, selh_ref, st_ref, dy_ref = refs[:9]
        dxbc_ref, ddtt_ref, da_ref = refs[n_in:n_in + 3]
        dh_ref, et_ref, det_ref, det2_ref, ddt_ref, q_ref = refs[n_in + n_out:]
        if final:
            prev_ref, dexp_ref, ddexp_ref = refs[9], refs[10], refs[n_in + 3]

        @pl.when(pl.program_id(0) == 0)
        def _():
            dh_ref[...] = jnp.zeros_like(dh_ref)
            da_ref[...] = jnp.zeros_like(da_ref)
            if final:
                ddexp_ref[...] = jnp.zeros_like(ddexp_ref)

        mask, mask_t = _ssd_masks(reverse)
        dtt_v, et = _ssd_chunk_common(dtt_ref, a_ref, et_ref, mask_t)
        sel8 = selh_ref[0:HPG, :]
        is_last = lax.broadcasted_iota(jnp.int32, (CHUNK, GW), 0) == last
        for g in range(N_SSD_GROUPS):
            col, eb, dtb, tbc, xs, bg, cg = _ssd_group_common(g, dtt_v, et, selc_ref, selh_ref, xs_ref, b_ref, c_ref,
                                                              last)
            xd = xs * dtb
            cb = _nt(cg, bg)
            cbt = _nt(bg, cg)
            exp_t = jnp.exp(tbc)
            dfac = jnp.exp(tbc - eb)
            ht = st_ref[0, g]
            dhn = dh_ref[g]
            ht16, dhn16 = ht.astype(bf16), dhn.astype(bf16)
            dy = dy_ref[:, g * GW:(g + 1) * GW].astype(f32)
            dye = dy * jnp.exp(eb)
            dye16 = dye.astype(bf16)
            dc = _nt(dye16, ht16)
            dh_ref[g] = exp_t * dhn + _tn(cg, dye16)
            deb = dye * _nn(cg, ht16)
            xdd = xd * dfac
            dxdd = _nn(bg, dhn16)
            db = _nt(xdd.astype(bf16), dhn16)
            dxd_state = dxdd * dfac
            ddf = dxdd * xdd
            dtbc = jnp.sum(ddf, axis=0, keepdims=True) + exp_t * jnp.sum(dhn * ht, axis=0, keepdims=True)
            deb = deb - ddf + jnp.where(is_last, dtbc, 0.0)
            dcb = jnp.zeros((CHUNK, CHUNK), f32)
            dcbt = jnp.zeros((CHUNK, CHUNK), f32)
            for j in range(HPG):
                h = g * HPG + j
                hs = slice(j * SSD_HEAD_DIM, (j + 1) * SSD_HEAD_DIM)
                colj = col[:, j * CHUNK:(j + 1) * CHUNK]
                row = et_ref[h:h + 1, :]
                lam = jnp.exp(jnp.where(mask, colj - row, NEG))
                lam_t = lam.T
                xdj, dyj = xd[:, hs].astype(bf16), dy[:, hs].astype(bf16)
                t1 = _nt(dyj, xdj) * lam
                t2 = _nt(xdj, dyj) * lam_t
                dcb, dcbt = dcb + t1, dcbt + t2
                det_ref[h:h + 1, :] = -jnp.sum(t1 * cb - t2 * cbt, axis=0, keepdims=True)
                dxdj = _nn((cbt * lam_t).astype(bf16), dyj) + dxd_state[:, hs]
                cols = slice(g * GW + j * SSD_HEAD_DIM, g * GW + (j + 1) * SSD_HEAD_DIM)
                dxs = dxdj * dtb[:, hs]
                if final:
                    dxs = dxs + prev_ref[:, cols] + dy[:, hs] * dexp_ref[:, cols]
                dxbc_ref[:, cols] = dxs
                q_ref[:, hs] = dxdj * xs[:, hs]
            b_cols = slice(D_INNER + g * D_STATE, D_INNER + (g + 1) * D_STATE)
            c_cols = slice(D_INNER + GN + g * D_STATE, D_INNER + GN + (g + 1) * D_STATE)
            db = db + _nn(dcbt.astype(bf16), cg)
            dc = dc + _nn(dcb.astype(bf16), bg)
            if final:
                db, dc = db + prev_ref[:, b_cols], dc + prev_ref[:, c_cols]
                ddexp_ref[:, g * GW:(g + 1) * GW] += jnp.sum(dy * xs, axis=0, keepdims=True)
            dxbc_ref[:, b_cols] = db
            dxbc_ref[:, c_cols] = dc
            det2_ref[g * HPG:(g + 1) * HPG, :] = _head_sum(sel8, deb)
            ddt_ref[g * HPG:(g + 1) * HPG, :] = _head_sum(sel8, q_ref[...])
        dat = jnp.dot(det_ref[...] + det2_ref[...], mask.astype(f32), precision=HIGHEST, preferred_element_type=f32)
        ddtt_ref[...] = ddt_ref[...] + dat * a_ref[...]
        da_ref[...] += jnp.sum(dat * dtt_v, axis=1, keepdims=True)

    in_specs = _ssd_in_specs(cidx) + [
        pl.BlockSpec((1, N_SSD_GROUPS, D_STATE, GW), lambda c: (cidx(c), 0, 0, 0)),
        pl.BlockSpec((CHUNK, D_INNER), lambda c: (cidx(c), 0))]
    hl = pltpu.VMEM((N_SSD_HEADS, CHUNK), f32)
    dxbc_spec = pl.BlockSpec((CHUNK, CONV_DIM), lambda c: (cidx(c), 0))
    dexp_spec = pl.BlockSpec((1, D_INNER), lambda c: (0, 0))
    return pl.pallas_call(
        body, name="ssd_bwd_rev" if reverse else "ssd_bwd", grid=(nc,),
        in_specs=in_specs + ([dxbc_spec, dexp_spec] if final else []),
        out_specs=[dxbc_spec, pl.BlockSpec((N_SSD_HEADS, CHUNK), lambda c: (0, cidx(c))),
                   pl.BlockSpec((N_SSD_HEADS, 1), lambda c: (0, 0))] + ([dexp_spec] if final else []),
        out_shape=[jax.ShapeDtypeStruct((s, CONV_DIM), f32), jax.ShapeDtypeStruct((N_SSD_HEADS, s), f32),
                   jax.ShapeDtypeStruct((N_SSD_HEADS, 1), f32)]
        + ([jax.ShapeDtypeStruct((1, D_INNER), f32)] if final else []),
        scratch_shapes=[pltpu.VMEM((N_SSD_GROUPS, D_STATE, GW), f32), hl, hl, hl, hl, pltpu.VMEM((CHUNK, GW), f32)],
        compiler_params=_cparams(dimension_semantics=("arbitrary",)),
    )(xbc, xbc, xbc, dtt, a_col, selc, selh, states, dy, *((dxbc_prev, dexp) if final else ()))


@jax.custom_vjp
def ssd_bidir(xbc, dtt, a_col, dexp):
    y_f, _ = _ssd_fwd(xbc, dtt[:N_SSD_HEADS], a_col[:N_SSD_HEADS], False)
    return _ssd_fwd(xbc, dtt[N_SSD_HEADS:], a_col[N_SSD_HEADS:], True, y_prev=y_f, dexp=dexp)[0]


def _ssd_bidir_fwd(xbc, dtt, a_col, dexp):
    y_f, st_f = _ssd_fwd(xbc, dtt[:N_SSD_HEADS], a_col[:N_SSD_HEADS], False)
    y, st_b = _ssd_fwd(xbc, dtt[N_SSD_HEADS:], a_col[N_SSD_HEADS:], True, y_prev=y_f, dexp=dexp)
    return y, (xbc, dtt, a_col, dexp, st_f, st_b)


def _ssd_bidir_bwd(res, dy):
    xbc, dtt, a_col, dexp, st_f, st_b = res
    dxbc_f, ddtt_f, da_f = _ssd_bwd(xbc, dtt[:N_SSD_HEADS], a_col[:N_SSD_HEADS], st_f, dy, False)
    dxbc, ddtt_b, da_b, ddexp = _ssd_bwd(xbc, dtt[N_SSD_HEADS:], a_col[N_SSD_HEADS:], st_b, dy, True,
                                         dxbc_prev=dxbc_f, dexp=dexp)
    return dxbc, jnp.concatenate([ddtt_f, ddtt_b], axis=0), jnp.concatenate([da_f, da_b], axis=0), ddexp


ssd_bidir.defvjp(_ssd_bidir_fwd, _ssd_bidir_bwd)


def _rope_tables(s):
    rows = s // GRID_W
    pos_row = np.repeat(np.arange(rows), GRID_W).astype(np.float32)
    pos_col = np.tile(np.arange(GRID_W), rows).astype(np.float32)
    axis_dim = HEAD_DIM // 2
    inv_freq = np.float32(ROPE_THETA) ** (-np.arange(0, axis_dim, 2, dtype=np.float32) / np.float32(axis_dim))
    ang_r = pos_row[:, None] * inv_freq[None, :].astype(np.float32)
    ang_c = pos_col[:, None] * inv_freq[None, :].astype(np.float32)
    cos = np.concatenate([np.cos(ang_r), np.cos(ang_r), np.cos(ang_c), np.cos(ang_c)] * 2, axis=-1)
    sin = np.concatenate([np.sin(ang_r), np.sin(ang_r), np.sin(ang_c), np.sin(ang_c)] * 2, axis=-1)
    return jnp.asarray(cos, f32), jnp.asarray(sin, f32)


def _rope_perm():
    p = np.zeros((HEAD_DIM, HEAD_DIM), np.float32)
    for j in range(HEAD_DIM):
        if (j % 32) < 16:
            p[j + 16, j] = -1.0
        else:
            p[j - 16, j] = 1.0
    return p


def local_loss(x, mod, small, recv_in_like, recv_late_like, wfull, late_shard, target):
    s = x.shape[0]
    lin = {n: make_linear("lin_" + n) for n in LATE if not n.startswith("mlp")}
    wfull, wgrads = dict(wfull), {}
    shift1, scale1, gate1, shift2, scale2, gate2 = [mod[i] for i in range(6)]

    norm_mod = make_rowwise("norm_mod", _fn_norm_mod, [(D_MODEL, bf16), (D_MODEL, f32)])
    (h, x_res), _ = norm_mod((x,), (small["norm1_w"], scale1, shift1), (), ())

    proj = dict(zip(PROJ_NAMES, in_proj(h, tuple(wfull[n] for n in PROJ_NAMES), recv_in_like)))

    cos, sin = _rope_tables(s)

    def heads(t, nh):
        return t.reshape(s, nh, HEAD_DIM).transpose(1, 0, 2)

    qr = make_head_rope("q_norm_rope", N_Q_HEADS, Q_SCALE, False)(proj["q"], small["q_norm_w"], cos, sin)
    kr = make_head_rope("k_norm_rope", N_KV_HEADS, 1.0, True)(proj["k"], small["k_norm_w"], cos, sin)
    vh = heads(proj["v"], N_KV_HEADS).astype(bf16)
    att = attention(qr, kr, vh)

    xbc, gathered, *carriers = conv_silu_comm(proj["xbc"], small["conv_w"], small["conv_b"], late_shard,
                                              recv_late_like)
    wfull.update(_split_late(gathered))
    wgrads.update(zip(LATE, carriers))
    ao = lin["attn_out"](att, wfull["attn_out"], wgrads["attn_out"])
    softplus = make_rowwise("dt_softplus", _fn_softplus, [(2 * N_SSD_HEADS, f32)])
    (dt,), _ = softplus((proj["dt"][:, :2 * N_SSD_HEADS],), (small["dt_bias"].reshape(1, 2 * N_SSD_HEADS),), (), ())
    a_neg = -jnp.exp(small["A_log"])
    dexp = jnp.repeat(small["ssd_D"].reshape(N_SSD_HEADS), SSD_HEAD_DIM).reshape(1, D_INNER)
    y = ssd_bidir(xbc, dt.T, a_neg.reshape(2 * N_SSD_HEADS, 1), dexp)
    ssd_gate = make_rowwise("ssd_gate", _fn_ssd_gate, [(D_INNER, bf16)], tm_pref=128)
    (ssd_out,), _ = ssd_gate((y, proj["z"]), (small["ssd_norm_w"],), (), ())
    so = lin["ssd_out"](ssd_out, wfull["ssd_out"], wgrads["ssd_out"])

    merge = make_rowwise("merge", _fn_merge, [(D_MODEL, bf16)])
    (merged,), _ = merge((ao, so, proj["ga"], proj["gs"]), (), (), ())
    mo = lin["o"](merged, wfull["o"], wgrads["o"])

    res_norm = make_rowwise("res_norm", _fn_res_norm, [(D_MODEL, f32), (D_MODEL, bf16)])
    (x1, h2), _ = res_norm((x_res, mo), (gate1, small["norm2_w"], scale2, shift2), (), ())
    ff = mlp(h2, wfull["mlp1"], wgrads["mlp1"], wfull["mlp2"], wgrads["mlp2"])
    loss_op = make_rowwise("loss", _fn_loss, [], [(1, 1)])
    _, (loss,) = loss_op((x1, ff), (gate2,), (), (target,))
    return loss[0, 0]


_BC1 = 1.0 - ADAM_B1 ** ADAM_STEP
_BC2 = 1.0 - ADAM_B2 ** ADAM_STEP


def _adamw(w, g, m, v):
    m = ADAM_B1 * m + (1.0 - ADAM_B1) * g
    v = ADAM_B2 * v + (1.0 - ADAM_B2) * (g * g)
    delta = -ADAM_LR * ((m / _BC1) / (jnp.sqrt(v / _BC2) + ADAM_EPS) + ADAM_WD * w)
    return delta, m, v


def _ada_fwd(c_all, w, b):
    n = w.shape[1]

    def body(c_ref, w_ref, b_ref, o_ref):
        o_ref[...] = jnp.dot(_silu(c_ref[...]), w_ref[...], precision=HIGHEST, preferred_element_type=f32) + b_ref[...]

    return pl.pallas_call(body, name="ada_fwd", out_shape=jax.ShapeDtypeStruct((N_DEV, n), f32),
                          compiler_params=_cparams())(c_all, w, b)


def _ada_bwd_adamw(c_all, dmod, w, m, v):
    d, n = w.shape
    tr = _pick(d, (256, 128))

    def body(c_ref, dm_ref, w_ref, m_ref, v_ref, g_ref, dl_ref, mo_ref, vo_ref):
        g = lax.dot_general(_silu(c_ref[...]), dm_ref[...], _DIMS["tn"], precision=HIGHEST,
                            preferred_element_type=f32)
        g_ref[...] = g
        dl_ref[...], mo_ref[...], vo_ref[...] = _adamw(w_ref[...], g, m_ref[...], v_ref[...])

    blk = pl.BlockSpec((tr, n), lambda i: (i, 0))
    return pl.pallas_call(
        body, name="ada_bwd_adamw", grid=(d // tr,),
        in_specs=[pl.BlockSpec((N_DEV, tr), lambda i: (0, i)), pl.BlockSpec((N_DEV, n), lambda i: (0, 0)), blk, blk, blk],
        out_specs=[blk] * 4, out_shape=[jax.ShapeDtypeStruct((d, n), f32)] * 4,
        compiler_params=_cparams(dimension_semantics=("parallel",)),
    )(c_all, dmod, w, m, v)


def _sum_over_mesh(g):
    def body(g_ref, o_ref):
        acc = g_ref[0]
        for d in range(1, N_DEV):
            acc = acc + g_ref[d]
        o_ref[...] = acc

    return pl.pallas_call(body, name="sum_small", out_shape=jax.ShapeDtypeStruct(g.shape[1:], f32),
                          compiler_params=_cparams())(g)


def _adamw_small(w, g, m, v):
    def body(w_ref, g_ref, m_ref, v_ref, dl_ref, mo_ref, vo_ref):
        dl_ref[...], mo_ref[...], vo_ref[...] = _adamw(w_ref[...], g_ref[...], m_ref[...], v_ref[...])

    return pl.pallas_call(body, name="adamw_small", out_shape=[jax.ShapeDtypeStruct(w.shape, f32)] * 3,
                          compiler_params=_cparams())(w, g, m, v)


def _sum_adamw(recv, w, m, v, name):
    _, r, c = recv.shape
    tr = _pick(r, (256, 128, 64, 16))

    def body(g_ref, w_ref, m_ref, v_ref, go_ref, dl_ref, mo_ref, vo_ref):
        g = g_ref[0].astype(f32)
        for d in range(1, N_DEV):
            g = g + g_ref[d].astype(f32)
        go_ref[...] = g
        dl_ref[...], mo_ref[...], vo_ref[...] = _adamw(w_ref[...], g, m_ref[...], v_ref[...])

    blk = pl.BlockSpec((tr, c), lambda i: (i, 0))
    return pl.pallas_call(
        body, name=name, grid=(r // tr,),
        in_specs=[pl.BlockSpec((N_DEV, tr, c), lambda i: (0, i, 0)), blk, blk, blk],
        out_specs=[blk] * 4, out_shape=[jax.ShapeDtypeStruct((r, c), f32)] * 4,
        compiler_params=_cparams(dimension_semantics=("parallel",)),
    )(recv, w, m, v)


def _pack_small(arrs):
    parts = []
    for a in arrs:
        flat = a.reshape(-1).astype(f32)
        parts.append(jnp.pad(flat, (0, (-flat.shape[0]) % LANE)))
    flat = jnp.concatenate(parts)
    flat = jnp.pad(flat, (0, (-flat.shape[0]) % (8 * LANE)))
    return flat.reshape(-1, LANE)


def _unpack_small(packed, shapes):
    flat = packed.reshape(-1)
    out, off = [], 0
    for shp in shapes:
        n = int(np.prod(shp))
        out.append(flat[off:off + n].reshape(shp))
        off += n + (-n) % LANE
    return out


BIG = ("w_attn_out", "w_ssd_out", "w_o", "w_mlp1", "w_mlp2")
BIG_ROWS = (N_Q_HEADS * HEAD_DIM // N_DEV, D_INNER // N_DEV, D_MODEL // N_DEV,
            D_MODEL * (D_FF // N_DEV) // PACK_COLS, D_FF // N_DEV)
N_IN_SHARD = D_IN_PROJ // N_DEV
assert sum(BIG_ROWS) % 16 == 0


def _pack_big(shards, dtype):
    return jnp.concatenate([s.astype(dtype).reshape(-1, PACK_COLS) for s in shards], axis=0)


def _unpack_big(packed, shapes):
    out, off = [], 0
    for rows, shp in zip(BIG_ROWS, shapes):
        out.append(packed[off:off + rows].reshape(shp))
        off += rows
    return out


LATE = ("attn_out", "ssd_out", "o", "mlp1", "mlp2")
LATE_SHAPES = ((N_Q_HEADS * HEAD_DIM, D_MODEL), (D_INNER, D_MODEL), (D_MODEL, D_MODEL), (D_MODEL, D_FF),
               (D_FF, D_MODEL))


def _split_w_in(g_in):
    w_in = g_in.transpose(1, 0, 2).reshape(D_MODEL, D_IN_PROJ)
    w = {}
    off = 0
    for name, size in zip(PROJ_NAMES, PROJ_SIZES):
        w[name] = w_in[:, off:off + size]
        off += size
    w["dt"] = jnp.pad(w["dt"], ((0, 0), (0, DT_PAD - 2 * N_SSD_HEADS)))
    return w


def _split_late(g):
    offs = np.cumsum((0,) + BIG_ROWS)
    sl = [g[:, offs[i]:offs[i + 1]] for i in range(len(BIG))]
    return {"attn_out": sl[0].reshape(LATE_SHAPES[0]), "ssd_out": sl[1].reshape(LATE_SHAPES[1]),
            "o": sl[2].reshape(LATE_SHAPES[2]),
            "mlp1": sl[3].reshape(N_DEV, D_MODEL, D_FF // N_DEV).transpose(1, 0, 2).reshape(LATE_SHAPES[3]),
            "mlp2": sl[4].reshape(LATE_SHAPES[4])}


def _pack_in_grads(gw):
    gw = {n: g.astype(bf16) for n, g in gw.items()}
    gw["dt"] = gw["dt"][:, :2 * N_SSD_HEADS]
    g_in = jnp.concatenate([gw[n] for n in PROJ_NAMES], axis=1)
    return g_in.reshape(D_MODEL, N_DEV, N_IN_SHARD).transpose(1, 0, 2)


def _pack_late_grads(gw):
    gw = {n: g.astype(bf16) for n, g in gw.items()}
    parts = [
        gw["attn_out"].reshape(N_DEV, -1, PACK_COLS),
        gw["ssd_out"].reshape(N_DEV, -1, PACK_COLS),
        gw["o"].reshape(N_DEV, -1, PACK_COLS),
        gw["mlp1"].reshape(D_MODEL, N_DEV, D_FF // N_DEV).transpose(1, 0, 2).reshape(N_DEV, -1, PACK_COLS),
        gw["mlp2"].reshape(N_DEV, -1, PACK_COLS),
    ]
    return jnp.concatenate(parts, axis=1)


SMALL = ("norm1_w", "norm2_w", "q_norm_w", "k_norm_w", "conv_w", "conv_b", "A_log", "dt_bias", "ssd_D", "ssd_norm_w")


def kernel(x, c, w_ada, b_ada, norm1_w, norm2_w, w_in, q_norm_w, k_norm_w, conv_w, conv_b, A_log, dt_bias, ssd_D, ssd_norm_w, w_attn_out, w_ssd_out, w_o, w_mlp1, w_mlp2, loss_target, m_w_ada, m_b_ada, m_norm1_w, m_norm2_w, m_w_in, m_q_norm_w, m_k_norm_w, m_conv_w, m_conv_b, m_A_log, m_dt_bias, m_ssd_D, m_ssd_norm_w, m_w_attn_out, m_w_ssd_out, m_w_o, m_w_mlp1, m_w_mlp2, v_w_ada, v_b_ada, v_norm1_w, v_norm2_w, v_w_in, v_q_norm_w, v_k_norm_w, v_conv_w, v_conv_b, v_A_log, v_dt_bias, v_ssd_D, v_ssd_norm_w, v_w_attn_out, v_w_ssd_out, v_w_o, v_w_mlp1, v_w_mlp2):
    args = dict(locals())
    me = _my_index()
    n_ada = 6 * D_MODEL // N_DEV
    n_cw = CONV_DIM // N_DEV

    blk = jnp.zeros((8, D_MODEL), f32)
    blk = blk.at[0:1, :].set(c)
    blk = blk.at[1:1 + D_CONV, :n_cw].set(conv_w[0])
    g0 = _all_gather(blk, "gather_c_convw", in_vmem=True)
    c_all = g0[:, 0, :]
    conv_w_full = g0[:, 1:1 + D_CONV, :n_cw].transpose(1, 0, 2).reshape(D_CONV, CONV_DIM)

    b_shard = lax.dynamic_slice(b_ada, (0, me * n_ada), (1, n_ada))
    mod_cols = _ada_fwd(c_all, w_ada[0], b_shard)
    g1 = _all_gather(mod_cols, "gather_mod", in_vmem=True)
    mod_mine = lax.dynamic_index_in_dim(g1, me, axis=1, keepdims=False)
    mod = mod_mine.reshape(6, 1, D_MODEL)

    big_shapes = [args[n].shape[1:] for n in BIG]
    late_shard = _pack_big([args[n][0] for n in BIG], bf16)
    wfull = _split_w_in(_all_gather(w_in[0].astype(bf16), "gather_w_in", in_vmem=False))
    recv_in_like = jnp.zeros((N_DEV,) + w_in.shape[1:], bf16)
    recv_late_like = jnp.zeros((N_DEV,) + late_shard.shape, bf16)

    small = {"norm1_w": norm1_w, "norm2_w": norm2_w, "q_norm_w": q_norm_w, "k_norm_w": k_norm_w,
             "conv_w": conv_w_full, "conv_b": conv_b, "A_log": A_log[0], "dt_bias": dt_bias[0], "ssd_D": ssd_D,
             "ssd_norm_w": ssd_norm_w}

    loss, (gx, gmod, gsmall, recv_in, recv_late) = jax.value_and_grad(local_loss, argnums=(0, 1, 2, 3, 4))(
        x[0], mod, small, recv_in_like, recv_late_like, wfull, late_shard, loss_target[0])

    small_list = [gmod, gsmall["norm1_w"], gsmall["norm2_w"], gsmall["q_norm_w"], gsmall["k_norm_w"], gsmall["conv_w"],
                  gsmall["conv_b"], gsmall["A_log"], gsmall["dt_bias"], gsmall["ssd_D"], gsmall["ssd_norm_w"],
                  loss.reshape(1)]
    small_shapes = [a.shape for a in small_list]
    g2 = _all_gather(_pack_small(small_list), "gather_small_grads", in_vmem=True)
    summed = _unpack_small(_sum_over_mesh(g2), small_shapes)
    loss_total = summed[-1][0]
    g_b_ada = summed[0].reshape(1, 6 * D_MODEL)
    g_small = dict(zip(SMALL, summed[1:-1]))
    g_conv_w = lax.dynamic_slice(g_small["conv_w"], (0, me * n_cw), (D_CONV, n_cw))

    dmod_all = g2[:, :6 * D_MODEL // LANE, :].reshape(N_DEV, 6 * D_MODEL)
    dmod_shard = lax.dynamic_slice(dmod_all, (0, me * n_ada), (N_DEV, n_ada))
    ada = _ada_bwd_adamw(c_all, dmod_shard, w_ada[0], m_w_ada[0], v_w_ada[0])

    small_grads = {"b_ada": g_b_ada, "norm1_w": g_small["norm1_w"], "norm2_w": g_small["norm2_w"],
                   "q_norm_w": g_small["q_norm_w"], "k_norm_w": g_small["k_norm_w"], "conv_w": g_conv_w[None],
                   "conv_b": g_small["conv_b"], "A_log": g_small["A_log"][None], "dt_bias": g_small["dt_bias"][None],
                   "ssd_D": g_small["ssd_D"], "ssd_norm_w": g_small["ssd_norm_w"]}
    sm_names = list(small_grads)
    sm_shapes = [args[n].shape for n in sm_names]
    sm = _adamw_small(_pack_small([args[n] for n in sm_names]), _pack_small([small_grads[n] for n in sm_names]),
                      _pack_small([args["m_" + n] for n in sm_names]), _pack_small([args["v_" + n] for n in sm_names]))
    sm_delta, sm_m, sm_v = [dict(zip(sm_names, _unpack_small(t, sm_shapes))) for t in sm]
    small_grads = {n: small_grads[n].reshape(args[n].shape) for n in sm_names}

    w_in_out = _sum_adamw(recv_in, w_in[0], m_w_in[0], v_w_in[0], "sum_adamw_w_in")
    big = _sum_adamw(recv_late, _pack_big([args[n][0] for n in BIG], f32),
                     _pack_big([args["m_" + n][0] for n in BIG], f32),
                     _pack_big([args["v_" + n][0] for n in BIG], f32), "sum_adamw")
    big_g, big_delta, big_m, big_v = [dict(zip(BIG, [t[None] for t in _unpack_big(p, big_shapes)])) for p in big]
    big_g["w_in"], big_delta["w_in"], big_m["w_in"], big_v["w_in"] = [t[None] for t in w_in_out]

    names = ("w_ada", "b_ada", "norm1_w", "norm2_w", "w_in", "q_norm_w", "k_norm_w", "conv_w", "conv_b", "A_log",
             "dt_bias", "ssd_D", "ssd_norm_w", "w_attn_out", "w_ssd_out", "w_o", "w_mlp1", "w_mlp2")
    grads, deltas, new_m, new_v = {}, {}, {}, {}
    for n in names:
        if n == "w_ada":
            grads[n], deltas[n], new_m[n], new_v[n] = [t[None] for t in ada]
        elif n in big_g:
            grads[n], deltas[n], new_m[n], new_v[n] = big_g[n], big_delta[n], big_m[n], big_v[n]
        else:
            grads[n], deltas[n], new_m[n], new_v[n] = small_grads[n], sm_delta[n], sm_m[n], sm_v[n]
    return (loss_total, gx[None], *[grads[n] for n in names], *[deltas[n] for n in names],
            *[new_m[n] for n in names], *[new_v[n] for n in names])
```

```python
import functools
import math

import jax
import jax.numpy as jnp
import numpy as np
from jax import lax
from jax.experimental import pallas as pl
from jax.experimental.pallas import tpu as pltpu

f32 = jnp.float32
bf16 = jnp.bfloat16
HIGHEST = lax.Precision.HIGHEST
MESH = pl.DeviceIdType.MESH

N_DEV = 8
D_MODEL = 1024
GRID_W = 64
N_Q_HEADS = 16
N_KV_HEADS = 4
HEAD_DIM = 64
ROPE_THETA = 10000.0
D_INNER = 2048
SSD_HEAD_DIM = 64
N_SSD_HEADS = 32
N_SSD_GROUPS = 4
D_STATE = 128
D_CONV = 5
CHUNK = 128
D_FF = 4096
EPS = 1e-6
CONV_DIM = D_INNER + 2 * N_SSD_GROUPS * D_STATE
GN = N_SSD_GROUPS * D_STATE
PROJ_NAMES = ("q", "k", "v", "xbc", "z", "dt", "ga", "gs")
PROJ_SIZES = (N_Q_HEADS * HEAD_DIM, N_KV_HEADS * HEAD_DIM, N_KV_HEADS * HEAD_DIM, CONV_DIM, D_INNER,
              2 * N_SSD_HEADS, D_MODEL, D_MODEL)
D_IN_PROJ = sum(PROJ_SIZES)
PROJ_DTYPES = (jnp.bfloat16, jnp.bfloat16, jnp.bfloat16, jnp.float32, jnp.bfloat16, jnp.float32, jnp.bfloat16,
               jnp.bfloat16)
DT_PAD = 128

ADAM_LR, ADAM_B1, ADAM_B2, ADAM_EPS, ADAM_WD, ADAM_STEP = 0.001, 0.9, 0.999, 1e-08, 0.01, 10

V7X_VMEM_LIMIT = 56 * 1024 * 1024
LANE = 128
PACK_COLS = 1024


def _cparams(**kw):
    return pltpu.CompilerParams(vmem_limit_bytes=V7X_VMEM_LIMIT, **kw)


def _pick(dim, prefs):
    for p in prefs:
        if dim % p == 0:
            return p
    return dim


def _my_index():
    return 4 * lax.axis_index("x") + 2 * lax.axis_index("y") + lax.axis_index("c")


COMM_SEMS = [pltpu.SemaphoreType.DMA((7,)), pltpu.SemaphoreType.DMA((7,)), pltpu.SemaphoreType.DMA]


def _gather_phases(x_ref, out_ref, send_sems, recv_sems, local_sem):
    x, y, cc = lax.axis_index("x"), lax.axis_index("y"), lax.axis_index("c")
    me, sibling = (x, y, cc), (x, y, 1 - cc)
    chips = [(1 - x, y), (x, 1 - y), (1 - x, 1 - y)]

    def slot(px, py, pc):
        return out_ref.at[4 * px + 2 * py + pc]

    def copy(k, blk, to, src=None):
        return pltpu.make_async_remote_copy(
            src_ref=slot(*blk) if src is None else src, dst_ref=slot(*blk),
            send_sem=send_sems.at[k], recv_sem=recv_sems.at[k], device_id=to, device_id_type=MESH)

    mine = pltpu.make_async_copy(x_ref, slot(*me), local_sem)
    first = [copy(0, me, sibling, src=x_ref)]
    first += [copy(1 + j, me, (*chip, cc), src=x_ref) for j, chip in enumerate(chips)]
    passed = [copy(4 + j, (*chip, cc), sibling) for j, chip in enumerate(chips)]

    def start():
        mine.start()
        for cp in first:
            cp.start()

    def finish():
        for j, chip in enumerate(chips):
            copy(1 + j, (*chip, cc), me).wait_recv()
            passed[j].start()
        copy(0, sibling, me).wait_recv()
        for j, chip in enumerate(chips):
            copy(4 + j, (*chip, 1 - cc), me).wait_recv()
        for cp in first + passed:
            cp.wait_send()
        mine.wait()

    return start, finish


def _scatter_phases(g_ref, out_ref, send_sems, recv_sems, local_sem):
    x, y, cc = lax.axis_index("x"), lax.axis_index("y"), lax.axis_index("c")
    me = 4 * x + 2 * y + cc
    mine = pltpu.make_async_copy(g_ref.at[me], out_ref.at[me], local_sem)

    def copy(k):
        fx, fy, fc = (k >> 2) & 1, (k >> 1) & 1, k & 1
        px = x + fx - 2 * x * fx
        py = y + fy - 2 * y * fy
        pc = cc + fc - 2 * cc * fc
        peer = 4 * px + 2 * py + pc
        send = pltpu.make_async_remote_copy(
            src_ref=g_ref.at[peer], dst_ref=out_ref.at[me],
            send_sem=send_sems.at[k - 1], recv_sem=recv_sems.at[k - 1],
            device_id=(px, py, pc), device_id_type=MESH)
        recv = pltpu.make_async_remote_copy(
            src_ref=g_ref.at[peer], dst_ref=out_ref.at[peer],
            send_sem=send_sems.at[k - 1], recv_sem=recv_sems.at[k - 1],
            device_id=(px, py, pc), device_id_type=MESH)
        return send, recv

    pairs = [copy(k) for k in range(1, N_DEV)]

    def start():
        mine.start()
        for send, _ in pairs:
            send.start()

    def finish():
        for _, recv in pairs:
            recv.wait_recv()
        for send, _ in pairs:
            send.wait_send()
        mine.wait()

    return start, finish


def _all_gather(block, name, in_vmem):
    r, c = block.shape

    def body(x_ref, out_ref, send_sems, recv_sems, local_sem):
        start, finish = _gather_phases(x_ref, out_ref, send_sems, recv_sems, local_sem)
        start()
        finish()

    space = pltpu.VMEM if in_vmem else pl.ANY
    return pl.pallas_call(
        body, name=name,
        out_shape=jax.ShapeDtypeStruct((N_DEV, r, c), block.dtype),
        in_specs=[pl.BlockSpec(memory_space=space)],
        out_specs=pl.BlockSpec(memory_space=space),
        scratch_shapes=[pltpu.SemaphoreType.DMA((7,)), pltpu.SemaphoreType.DMA((7,)), pltpu.SemaphoreType.DMA],
    )(block)


def _scatter_blocks(g, name):
    _, r, c = g.shape

    def body(g_ref, out_ref, send_sems, recv_sems, local_sem):
        start, finish = _scatter_phases(g_ref, out_ref, send_sems, recv_sems, local_sem)
        start()
        finish()

    return pl.pallas_call(
        body, name=name,
        out_shape=jax.ShapeDtypeStruct(g.shape, g.dtype),
        in_specs=[pl.BlockSpec(memory_space=pl.ANY)],
        out_specs=pl.BlockSpec(memory_space=pl.ANY),
        scratch_shapes=[pltpu.SemaphoreType.DMA((7,)), pltpu.SemaphoreType.DMA((7,)), pltpu.SemaphoreType.DMA],
    )(g)


_DIMS = {"nn": (((1,), (0,)), ((), ())), "nt": (((1,), (1,)), ((), ())), "tn": (((0,), (0,)), ((), ()))}


def _matmul(a, b, mode, out_dtype, name, epilogue=None, side=None):
    if mode == "nn":
        (m, k), (_, n) = a.shape, b.shape
    elif mode == "nt":
        (m, k), (n, _) = a.shape, b.shape
    else:
        (k, m), (_, n) = a.shape, b.shape
    tm = _pick(m, (1024, 512, 256, 128))
    if mode == "tn":
        tn = _pick(n, (1536, 1024, 512, 256, 128))
        tk = _pick(k, (2048, 1024, 512, 256, 128)) if b.dtype == bf16 else _pick(k, (1024, 512, 256, 128))
    else:
        tn = _pick(n, (1024, 512, 384, 256, 128))
        tk = _pick(k, (2048, 1024, 512, 256, 128)) if a.dtype == bf16 else _pick(k, (1024, 512, 256, 128))
    nk = k // tk
    dims = _DIMS[mode]
    n_in = 3 if epilogue == "drelu2" else 2

    def body(*refs):
        a_ref, b_ref = refs[:2]
        o_ref, acc_ref = refs[n_in], refs[n_in + 1]
        kk = pl.program_id(2)
        part = lax.dot_general(a_ref[...].astype(bf16), b_ref[...].astype(bf16), dims, preferred_element_type=f32)

        def finish(acc):
            if epilogue == "relu2":
                r = jnp.maximum(acc, 0.0)
                o_ref[...] = (r * r).astype(out_dtype)
            elif epilogue == "drelu2":
                o_ref[...] = (acc * (2.0 * jnp.sqrt(refs[2][...].astype(f32)))).astype(out_dtype)
            else:
                o_ref[...] = acc.astype(out_dtype)

        if nk == 1:
            finish(part)
        else:
            @pl.when(kk == 0)
            def _():
                acc_ref[...] = part

            @pl.when(kk > 0)
            def _():
                acc_ref[...] += part

            @pl.when(kk == nk - 1)
            def _():
                finish(acc_ref[...])

    if mode == "tn":
        a_spec = pl.BlockSpec((tk, tm), lambda i, j, kk: (kk, i))
    else:
        a_spec = pl.BlockSpec((tm, tk), lambda i, j, kk: (i, kk))
    if mode == "nt":
        b_spec = pl.BlockSpec((tn, tk), lambda i, j, kk: (j, kk))
    else:
        b_spec = pl.BlockSpec((tk, tn), lambda i, j, kk: (kk, j))
    o_spec = pl.BlockSpec((tm, tn), lambda i, j, kk: (i, j))
    o_shape = jax.ShapeDtypeStruct((m, n), out_dtype)
    return pl.pallas_call(
        body, name=name, grid=(m // tm, n // tn, nk),
        in_specs=[a_spec, b_spec] + ([o_spec] if epilogue == "drelu2" else []),
        out_specs=o_spec, out_shape=o_shape,
        scratch_shapes=[pltpu.VMEM((tm, tn), f32)],
        compiler_params=_cparams(dimension_semantics=("parallel", "parallel", "arbitrary")),
    )(*((a, b, side) if epilogue == "drelu2" else (a, b)))


@jax.custom_vjp
def mlp(h, w1, w1grad, w2, w2grad):
    r = _matmul(h, w1, "nn", bf16, "mlp1_fwd", epilogue="relu2")
    return _matmul(r, w2, "nn", f32, "mlp2_fwd")


def _mlp_fwd(h, w1, w1grad, w2, w2grad):
    r = _matmul(h, w1, "nn", bf16, "mlp1_fwd", epilogue="relu2")
    return _matmul(r, w2, "nn", f32, "mlp2_fwd"), (h, w1, w2, r)


def _mlp_bwd(res, dy):
    h, w1, w2, r = res
    du = _matmul(dy, w2, "nt", bf16, "mlp2_dgrad", epilogue="drelu2", side=r)
    dw2 = _matmul(r, dy, "tn", f32, "mlp2_wgrad")
    dh = _matmul(du, w1, "nt", h.dtype, "mlp1_dgrad")
    dw1 = _matmul(h, du, "tn", f32, "mlp1_wgrad")
    return dh, jnp.zeros_like(w1), dw1, jnp.zeros_like(w2), dw2


mlp.defvjp(_mlp_fwd, _mlp_bwd)


def make_linear(name):
    @jax.custom_vjp
    def linear(a, w, wgrad):
        return _matmul(a, w, "nn", f32, name + "_fwd")

    def fwd(a, w, wgrad):
        return linear(a, w, wgrad), (a, w)

    def bwd(res, dy):
        a, w = res
        da = _matmul(dy, w, "nt", a.dtype, name + "_dgrad")
        dw = _matmul(a, dy, "tn", f32, name + "_wgrad")
        return da, jnp.zeros_like(w), dw

    linear.defvjp(fwd, bwd)
    return linear


def _in_proj_dgrad(dys, ws, g):
    s, d = dys[0].shape[0], ws[0].shape[0]
    tm = _pick(s, (1024, 512, 256, 128))
    tks = [w.shape[1] if w.shape[1] <= 1024 else 512 for w in ws]
    steps = [w.shape[1] // tk for w, tk in zip(ws, tks)]
    starts = [sum(steps[:p]) for p in range(len(ws))]
    total = sum(steps)
    n_p, n_i = len(ws), s // tm
    assert steps[0] == 1

    def body(*refs):
        dy_refs, w_refs, g_ref = refs[:n_p], refs[n_p:2 * n_p], refs[2 * n_p]
        dh_ref, recv_ref, acc_ref, send_sems, recv_sems, local_sem = refs[2 * n_p + 1:]
        i, t = pl.program_id(0), pl.program_id(1)
        start, finish = _scatter_phases(g_ref, recv_ref, send_sems, recv_sems, local_sem)

        @pl.when((i == 0) & (t == 0))
        def _():
            start()

        for p in range(n_p):
            @pl.when((t >= starts[p]) & (t < starts[p] + steps[p]))
            def _(p=p):
                part = lax.dot_general(dy_refs[p][...].astype(bf16), w_refs[p][...], _DIMS["nt"],
                                       preferred_element_type=f32)
                if p == 0:
                    acc_ref[...] = part
                else:
                    acc_ref[...] += part

        @pl.when(t == total - 1)
        def _():
            dh_ref[...] = acc_ref[...].astype(dh_ref.dtype)

        @pl.when((i == n_i - 1) & (t == total - 1))
        def _():
            finish()

    def piece_map(p, rows):
        def index_map(i, t):
            blk = jnp.clip(t - starts[p], 0, steps[p] - 1)
            return (i, blk) if rows else (0, blk)

        return index_map

    hbm = pl.BlockSpec(memory_space=pl.ANY)
    in_specs = [pl.BlockSpec((tm, tks[p]), piece_map(p, True)) for p in range(n_p)]
    in_specs += [pl.BlockSpec((d, tks[p]), piece_map(p, False)) for p in range(n_p)]
    return pl.pallas_call(
        body, name="in_proj_dgrad", grid=(n_i, total), in_specs=in_specs + [hbm],
        out_specs=[pl.BlockSpec((tm, d), lambda i, t: (i, 0)), hbm],
        out_shape=[jax.ShapeDtypeStruct((s, d), bf16), jax.ShapeDtypeStruct(g.shape, g.dtype)],
        scratch_shapes=[pltpu.VMEM((tm, d), f32)] + COMM_SEMS,
        compiler_params=_cparams(dimension_semantics=("arbitrary", "arbitrary")),
    )(*dys, *ws, g)


@jax.custom_vjp
def in_proj(h, ws, recv_like):
    return tuple(_matmul(h, w, "nn", dt, "lin_" + n + "_fwd") for n, w, dt in zip(PROJ_NAMES, ws, PROJ_DTYPES))


def _in_proj_fwd(h, ws, recv_like):
    return in_proj(h, ws, recv_like), (h, ws)


def _in_proj_bwd(res, dys):
    h, ws = res
    dws = {n: _matmul(h, dy, "tn", f32, "lin_" + n + "_wgrad") for n, dy in zip(PROJ_NAMES, dys)}
    dh, recv = _in_proj_dgrad(dys, ws, _pack_in_grads(dws))
    return dh.astype(h.dtype), tuple(jnp.zeros_like(w) for w in ws), recv


in_proj.defvjp(_in_proj_fwd, _in_proj_bwd)


def make_rowwise(name, fn, row_out, sum_out=(), tm_pref=256):
    def specs(rows, gpars, cpars, consts, tm):
        s = [pl.BlockSpec((tm, r.shape[1]), lambda i: (i, 0)) for r in rows]
        s += [pl.BlockSpec(p.shape, lambda i: (0, 0)) for p in gpars]
        s += [pl.BlockSpec(p.shape, lambda i: (0, 0)) for p in cpars]
        for cst in consts:
            nb = cst.shape[0] // tm
            s.append(pl.BlockSpec((tm, cst.shape[1]), lambda i, nb=nb: (i % nb, 0)))
        return s

    def tile_rows(rows, consts):
        r = rows[0].shape[0]
        common = math.gcd(r, *[cst.shape[0] for cst in consts])
        tm = _pick(common, (tm_pref, 512, 256, 128, 64, 32, 16, 8))
        return r, tm

    def forward(rows, gpars, cpars, consts):
        r, tm = tile_rows(rows, consts)
        nr, ng, nc, nk = len(rows), len(gpars), len(cpars), len(consts)

        def body(*refs):
            ins = refs[:nr + ng + nc + nk]
            outs = refs[nr + ng + nc + nk:]
            rv = [t[...].astype(f32) for t in ins[:nr]]
            gv = [t[...].astype(f32) for t in ins[nr:nr + ng]]
            cv = [t[...] for t in ins[nr + ng:nr + ng + nc]]
            kv = [t[...].astype(f32) for t in ins[nr + ng + nc:]]
            ro, so = fn(rv, gv, cv, kv)
            for o_ref, val in zip(outs[:len(row_out)], ro):
                o_ref[...] = val.astype(o_ref.dtype)
            if sum_out:
                @pl.when(pl.program_id(0) == 0)
                def _():
                    for o_ref in outs[len(row_out):]:
                        o_ref[...] = jnp.zeros_like(o_ref)
                for o_ref, val in zip(outs[len(row_out):], so):
                    o_ref[...] += val

        out_specs = [pl.BlockSpec((tm, w), lambda i: (i, 0)) for w, _ in row_out]
        out_specs += [pl.BlockSpec(shp, lambda i: (0, 0)) for shp in sum_out]
        out_shape = [jax.ShapeDtypeStruct((r, w), dt) for w, dt in row_out]
        out_shape += [jax.ShapeDtypeStruct(shp, f32) for shp in sum_out]
        res = pl.pallas_call(
            body, name=name + "_fwd", grid=(r // tm,),
            in_specs=specs(rows, gpars, cpars, consts, tm), out_specs=out_specs, out_shape=out_shape,
            compiler_params=_cparams(dimension_semantics=("arbitrary",)),
        )(*rows, *gpars, *cpars, *consts)
        return tuple(res[:len(row_out)]), tuple(res[len(row_out):])

    def backward(rows, gpars, cpars, consts, d_ro, d_so):
        r, tm = tile_rows(rows, consts)
        nr, ng, nc, nk = len(rows), len(gpars), len(cpars), len(consts)
        n_in = nr + ng + nc + nk + len(row_out) + len(sum_out)

        def body(*refs):
            ins, outs = refs[:n_in], refs[n_in:]
            rv = [t[...].astype(f32) for t in ins[:nr]]
            gv = [t[...].astype(f32) for t in ins[nr:nr + ng]]
            cv = [t[...] for t in ins[nr + ng:nr + ng + nc]]
            kv = [t[...].astype(f32) for t in ins[nr + ng + nc:nr + ng + nc + nk]]
            o = nr + ng + nc + nk
            dro = [t[...].astype(f32) for t in ins[o:o + len(row_out)]]
            dso = [t[...] for t in ins[o + len(row_out):]]
            _, vjp = jax.vjp(lambda a, b: tuple(tuple(t) for t in fn(a, b, cv, kv)), rv, gv)
            drv, dgv = vjp((tuple(dro), tuple(dso)))
            for o_ref, val in zip(outs[:nr], drv):
                o_ref[...] = val.astype(o_ref.dtype)
            if ng:
                @pl.when(pl.program_id(0) == 0)
                def _():
                    for o_ref in outs[nr:]:
                        o_ref[...] = jnp.zeros_like(o_ref)
                for o_ref, val in zip(outs[nr:], dgv):
                    o_ref[...] += val

        in_specs = specs(rows, gpars, cpars, consts, tm)
        in_specs += [pl.BlockSpec((tm, w), lambda i: (i, 0)) for w, _ in row_out]
        in_specs += [pl.BlockSpec(shp, lambda i: (0, 0)) for shp in sum_out]
        out_specs = [pl.BlockSpec((tm, t.shape[1]), lambda i: (i, 0)) for t in rows]
        out_specs += [pl.BlockSpec(p.shape, lambda i: (0, 0)) for p in gpars]
        out_shape = [jax.ShapeDtypeStruct(t.shape, t.dtype) for t in rows]
        out_shape += [jax.ShapeDtypeStruct(p.shape, f32) for p in gpars]
        res = pl.pallas_call(
            body, name=name + "_bwd", grid=(r // tm,),
            in_specs=in_specs, out_specs=out_specs, out_shape=out_shape,
            compiler_params=_cparams(dimension_semantics=("arbitrary",)),
        )(*rows, *gpars, *cpars, *consts, *d_ro, *d_so)
        return tuple(res[:nr]), tuple(res[nr:])

    @jax.custom_vjp
    def op(rows, gpars, cpars, consts):
        return forward(rows, gpars, cpars, consts)

    def op_fwd(rows, gpars, cpars, consts):
        return forward(rows, gpars, cpars, consts), (rows, gpars, cpars, consts)

    def op_bwd(res, cts):
        rows, gpars, cpars, consts = res
        d_ro, d_so = cts
        drows, dg = backward(rows, gpars, cpars, consts, d_ro, d_so)
        dg = tuple(d.astype(p.dtype) for d, p in zip(dg, gpars))
        return (drows, dg, tuple(jnp.zeros_like(p) for p in cpars), tuple(jnp.zeros_like(k) for k in consts))

    op.defvjp(op_fwd, op_bwd)
    return op


def _rms(x):
    return x * lax.rsqrt(jnp.mean(x * x, axis=-1, keepdims=True) + EPS)


def _silu(x):
    return x * jax.nn.sigmoid(x)


def _fn_norm_mod(rows, gp, cp, ks):
    (x,), (nw, sc, sh) = rows, gp
    return ((_rms(x) * nw) * (1.0 + sc) + sh, x), ()


PAIR = 2 * HEAD_DIM


def _exact_dot(a, m):
    hi = a.astype(bf16)
    lo = (a - hi.astype(f32)).astype(bf16)
    return jnp.dot(hi, m, preferred_element_type=f32) + jnp.dot(lo, m, preferred_element_type=f32)


def _make_sel_dot(sign):
    @jax.custom_vjp
    def sel_dot(a, m):
        return _exact_dot(a, m)

    def fwd(a, m):
        return _exact_dot(a, m), m

    def bwd(m, g):
        return sign * _exact_dot(g, m), jnp.zeros_like(m)

    sel_dot.defvjp(fwd, bwd)
    return sel_dot


_head_sum_dot = _make_sel_dot(1.0)
_rope_perm_dot = _make_sel_dot(-1.0)


def _pair_norm_rope(t, w2, gsum, perm, cos2, sin2, out_scale):
    ss = _head_sum_dot(t * t, gsum)
    u = t * lax.rsqrt(ss * (1.0 / HEAD_DIM) + EPS) * w2
    return (u * cos2 + _rope_perm_dot(u, perm) * sin2) * out_scale


def _pair_consts():
    eye = np.eye(2, dtype=np.float32)
    gsum = np.kron(eye, np.ones((HEAD_DIM, HEAD_DIM), np.float32))
    return jnp.asarray(gsum, bf16), jnp.asarray(np.kron(eye, _rope_perm()), bf16)


def make_head_rope(name, nh, out_scale, head_major):
    width = nh * HEAD_DIM
    fn = functools.partial(_pair_norm_rope, out_scale=out_scale)

    def out_spec(tm):
        if head_major:
            return pl.BlockSpec((nh, tm, HEAD_DIM), lambda i: (0, i, 0))
        return pl.BlockSpec((tm, width), lambda i: (i, 0))

    def specs(tm):
        def full(shp):
            return pl.BlockSpec(shp, lambda i: (0, 0))

        return [pl.BlockSpec((tm, width), lambda i: (i, 0)), full((1, PAIR)), full((PAIR, PAIR)), full((PAIR, PAIR)),
                pl.BlockSpec((tm, PAIR), lambda i: (i, 0)), pl.BlockSpec((tm, PAIR), lambda i: (i, 0))]

    def forward(t, w2, gsum, perm, cos2, sin2):
        s = t.shape[0]
        tm = _pick(s, (512, 256, 128))

        def body(t_ref, w_ref, g_ref, p_ref, cos_ref, sin_ref, o_ref):
            for b in range(nh // 2):
                val = fn(t_ref[:, b * PAIR:(b + 1) * PAIR].astype(f32), w_ref[...], g_ref[...], p_ref[...], cos_ref[...],
                         sin_ref[...]).astype(o_ref.dtype)
                if head_major:
                    o_ref[2 * b] = val[:, :HEAD_DIM]
                    o_ref[2 * b + 1] = val[:, HEAD_DIM:]
                else:
                    o_ref[:, b * PAIR:(b + 1) * PAIR] = val

        return pl.pallas_call(
            body, name=name + "_fwd", grid=(s // tm,), in_specs=specs(tm), out_specs=out_spec(tm),
            out_shape=jax.ShapeDtypeStruct((nh, s, HEAD_DIM) if head_major else (s, width), bf16),
            compiler_params=_cparams(dimension_semantics=("arbitrary",)),
        )(t, w2, gsum, perm, cos2, sin2)

    def backward(t, w2, gsum, perm, cos2, sin2, dout):
        s = t.shape[0]
        tm = _pick(s, (512, 256, 128))

        def body(t_ref, w_ref, g_ref, p_ref, cos_ref, sin_ref, do_ref, dt_ref, dw_ref, pair_buf):
            @pl.when(pl.program_id(0) == 0)
            def _():
                dw_ref[...] = jnp.zeros_like(dw_ref)

            g_v, p_v, cos_v, sin_v = g_ref[...], p_ref[...], cos_ref[...], sin_ref[...]
            dw = jnp.zeros((1, PAIR), f32)
            for b in range(nh // 2):
                sl = slice(b * PAIR, (b + 1) * PAIR)
                if head_major:
                    pair_buf[:, :HEAD_DIM] = do_ref[2 * b].astype(f32)
                    pair_buf[:, HEAD_DIM:] = do_ref[2 * b + 1].astype(f32)
                    ct = pair_buf[...]
                else:
                    ct = do_ref[:, sl].astype(f32)
                _, vjp = jax.vjp(lambda a, c: fn(a, c, g_v, p_v, cos_v, sin_v), t_ref[:, sl].astype(f32), w_ref[...])
                dtb, dwb = vjp(ct)
                dt_ref[:, sl] = dtb.astype(dt_ref.dtype)
                dw = dw + dwb
            dw_ref[...] += dw

        return pl.pallas_call(
            body, name=name + "_bwd", grid=(s // tm,), in_specs=specs(tm) + [out_spec(tm)],
            out_specs=[pl.BlockSpec((tm, width), lambda i: (i, 0)), pl.BlockSpec((1, PAIR), lambda i: (0, 0))],
            out_shape=[jax.ShapeDtypeStruct((s, width), t.dtype), jax.ShapeDtypeStruct((1, PAIR), f32)],
            scratch_shapes=[pltpu.VMEM((tm, PAIR), f32)],
            compiler_params=_cparams(dimension_semantics=("arbitrary",)),
        )(t, w2, gsum, perm, cos2, sin2, dout)

    @jax.custom_vjp
    def op(t, w2, gsum, perm, cos2, sin2):
        return forward(t, w2, gsum, perm, cos2, sin2)

    def op_fwd(*args):
        return forward(*args), args

    def op_bwd(res, dout):
        dt, dw = backward(*res, dout)
        return (dt, dw) + tuple(jnp.zeros_like(r) for r in res[2:])

    op.defvjp(op_fwd, op_bwd)

    def apply(t, w, cos2, sin2):
        gsum, perm = _pair_consts()
        return op(t, jnp.concatenate([w, w], axis=-1), gsum, perm, cos2, sin2)

    return apply


def _fn_softplus(rows, gp, cp, ks):
    (x,), (b,) = rows, gp
    v = x + b
    return (jnp.maximum(v, 0.0) + jnp.log(1.0 + jnp.exp(-jnp.abs(v))),), ()


def _fn_ssd_gate(rows, gp, cp, ks):
    (y, z), (nw,) = rows, gp
    return (_rms(y * _silu(z)) * nw,), ()


def _fn_merge(rows, gp, cp, ks):
    ao, so, ga, gs = rows
    return (jax.nn.sigmoid(ga) * ao + jax.nn.sigmoid(gs) * so,), ()


def _fn_res_norm(rows, gp, cp, ks):
    (x, mo), (g1, nw, sc, sh) = rows, gp
    x1 = x + g1 * mo
    return (x1, (_rms(x1) * nw) * (1.0 + sc) + sh), ()


def _fn_loss(rows, gp, cp, ks):
    (x1, ff), (g2,), (tgt,) = rows, gp, ks
    err = x1 + g2 * ff - tgt
    return (), (0.5 * jnp.sum(jnp.sum(err * err, axis=-1, keepdims=True), axis=0, keepdims=True) / D_MODEL,)


HALO = 8
HALO_BWD = 16


def _conv_tiles(s, c):
    return _pick(s, (512, 256, 128)), _pick(c, (512, 256, 128))


def _halo_specs(tm, tc, s, halo=HALO):
    nb = tm // halo
    last = s // halo - 1
    cur = pl.BlockSpec((tm, tc), lambda j, i: (i, j))
    prev = pl.BlockSpec((halo, tc), lambda j, i: (jnp.maximum(i * nb - 1, 0), j))
    nxt = pl.BlockSpec((halo, tc), lambda j, i: (jnp.minimum((i + 1) * nb, last), j))
    return cur, prev, nxt


def _fill_halo(buf, cur, prev, nxt, tm, i, n_i, halo=HALO):
    buf[halo:halo + tm, :] = cur[...]
    buf[0:halo, :] = jnp.where(i > 0, prev[...], 0.0)
    buf[halo + tm:, :] = jnp.where(i < n_i - 1, nxt[...], 0.0)


def _conv_fwd(x, w, b, shard):
    s, c = x.shape
    tm, tc = _conv_tiles(s, c)
    n_i, n_j = s // tm, c // tc

    def body(cur, prev, nxt, w_ref, b_ref, shard_ref, o_ref, gath_ref, buf, send_sems, recv_sems, local_sem):
        j, i = pl.program_id(0), pl.program_id(1)
        start, finish = _gather_phases(shard_ref, gath_ref, send_sems, recv_sems, local_sem)

        @pl.when((j == 0) & (i == 0))
        def _():
            start()

        _fill_halo(buf, cur, prev, nxt, tm, i, n_i)
        pre = jnp.zeros((tm, tc), f32) + b_ref[...]
        for k in range(D_CONV):
            pre = pre + buf[HALO - 2 + k:HALO - 2 + k + tm, :] * w_ref[k:k + 1, :]
        o_ref[...] = _silu(pre)

        @pl.when((j == n_j - 1) & (i == n_i - 1))
        def _():
            finish()

    cur, prev, nxt = _halo_specs(tm, tc, s)
    hbm = pl.BlockSpec(memory_space=pl.ANY)
    return pl.pallas_call(
        body, name="conv_silu_fwd", grid=(n_j, n_i),
        in_specs=[cur, prev, nxt, pl.BlockSpec((D_CONV, tc), lambda j, i: (0, j)),
                  pl.BlockSpec((1, tc), lambda j, i: (0, j)), hbm],
        out_specs=[pl.BlockSpec((tm, tc), lambda j, i: (i, j)), hbm],
        out_shape=[jax.ShapeDtypeStruct((s, c), f32), jax.ShapeDtypeStruct((N_DEV,) + shard.shape, shard.dtype)],
        scratch_shapes=[pltpu.VMEM((tm + 2 * HALO, tc), f32)] + COMM_SEMS,
        compiler_params=_cparams(dimension_semantics=("arbitrary", "arbitrary")),
    )(x, x, x, w, b, shard)


def _conv_bwd(x, w, b, dy, g):
    s, c = x.shape
    tm, tc = _conv_tiles(s, c)
    n_i, n_j = s // tm, c // tc
    ext = tm + 16

    def body(cur, prev, nxt, dcur, dprev, dnxt, w_ref, b_ref, g_ref, dx_ref, dw_ref, db_ref, recv_ref,
             xbuf, dbuf, pbuf, send_sems, recv_sems, local_sem):
        j, i = pl.program_id(0), pl.program_id(1)
        start, finish = _scatter_phases(g_ref, recv_ref, send_sems, recv_sems, local_sem)

        @pl.when((j == 0) & (i == 0))
        def _():
            start()

        _fill_halo(xbuf, cur, prev, nxt, tm, i, n_i, HALO_BWD)
        _fill_halo(dbuf, dcur, dprev, dnxt, tm, i, n_i, HALO_BWD)
        xs = [xbuf[6 + k:6 + k + ext, :] for k in range(D_CONV)]
        pre = jnp.zeros((ext, tc), f32) + b_ref[...]
        for k in range(D_CONV):
            pre = pre + xs[k] * w_ref[k:k + 1, :]
        sg = jax.nn.sigmoid(pre)
        pbuf[...] = dbuf[8:8 + ext, :] * (sg * (1.0 + pre * (1.0 - sg)))
        dx = jnp.zeros((tm, tc), f32)
        for k in range(D_CONV):
            dx = dx + pbuf[10 - k:10 - k + tm, :] * w_ref[k:k + 1, :]
        dx_ref[...] = dx

        @pl.when(i == 0)
        def _():
            dw_ref[...] = jnp.zeros_like(dw_ref)
            db_ref[...] = jnp.zeros_like(db_ref)

        dpre = pbuf[8:8 + tm, :]
        db_ref[...] += jnp.sum(dpre, axis=0, keepdims=True)
        for k in range(D_CONV):
            dw_ref[k:k + 1, :] += jnp.sum(dpre * xs[k][8:8 + tm, :], axis=0, keepdims=True)

        @pl.when((j == n_j - 1) & (i == n_i - 1))
        def _():
            finish()

    cur, prev, nxt = _halo_specs(tm, tc, s, HALO_BWD)
    hbm = pl.BlockSpec(memory_space=pl.ANY)
    return pl.pallas_call(
        body, name="conv_silu_bwd", grid=(n_j, n_i),
        in_specs=[cur, prev, nxt, cur, prev, nxt, pl.BlockSpec((D_CONV, tc), lambda j, i: (0, j)),
                  pl.BlockSpec((1, tc), lambda j, i: (0, j)), hbm],
        out_specs=[pl.BlockSpec((tm, tc), lambda j, i: (i, j)), pl.BlockSpec((D_CONV, tc), lambda j, i: (0, j)),
                   pl.BlockSpec((1, tc), lambda j, i: (0, j)), hbm],
        out_shape=[jax.ShapeDtypeStruct((s, c), f32), jax.ShapeDtypeStruct((D_CONV, c), f32),
                   jax.ShapeDtypeStruct((1, c), f32), jax.ShapeDtypeStruct(g.shape, g.dtype)],
        scratch_shapes=[pltpu.VMEM((tm + 2 * HALO_BWD, tc), f32), pltpu.VMEM((tm + 2 * HALO_BWD, tc), f32),
                        pltpu.VMEM((ext, tc), f32)] + COMM_SEMS,
        compiler_params=_cparams(dimension_semantics=("arbitrary", "arbitrary")),
    )(x, x, x, dy, dy, dy, w, b, g)


@jax.custom_vjp
def conv_silu_comm(x, w, b, shard, recv_like):
    act, gathered = _conv_fwd(x, w, b, shard)
    return (act, gathered) + tuple(jnp.zeros(shp, f32) for shp in LATE_SHAPES)


def _conv_silu_comm_fwd(x, w, b, shard, recv_like):
    return conv_silu_comm(x, w, b, shard, recv_like), (x, w, b, shard)


def _conv_silu_comm_bwd(res, cts):
    x, w, b, shard = res
    dx, dw, db, recv = _conv_bwd(x, w, b, cts[0], _pack_late_grads(dict(zip(LATE, cts[2:]))))
    return dx, dw, db, jnp.zeros_like(shard), recv


conv_silu_comm.defvjp(_conv_silu_comm_fwd, _conv_silu_comm_bwd)


ATT_SCALE = HEAD_DIM ** -0.5
Q_SCALE = ATT_SCALE * math.log2(math.e)
LN2 = math.log(2.0)
REP = N_Q_HEADS // N_KV_HEADS


HP = 2
assert REP % HP == 0


def _attn_fwd(q, k, v):
    s, dh = q.shape[0], HEAD_DIM
    hq = q.shape[1] // dh
    tq = _pick(s, (256, 128))

    v1 = jnp.concatenate([v, jnp.ones(v.shape[:2] + (1,), v.dtype), jnp.zeros(v.shape[:2] + (dh - 1,), v.dtype)],
                         axis=-1)

    def body(q_ref, k_ref, v_ref, o_ref, p_ref, linv_ref):
        for j in range(HP):
            sl = slice(j * dh, (j + 1) * dh)
            sc = lax.dot_general(q_ref[:, sl], k_ref[0], _DIMS["nt"], preferred_element_type=f32)
            m = jnp.max(sc, axis=-1, keepdims=True)
            p = jnp.exp2(sc - m).astype(bf16)
            p_ref[j] = p
            o1 = jnp.dot(p, v_ref[0], preferred_element_type=f32)
            linv = 1.0 / o1[:, dh:dh + 1]
            o_ref[:, sl] = (o1[:, :dh] * linv).astype(o_ref.dtype)
            linv_ref[j] = linv

    return pl.pallas_call(
        body, name="attn_fwd", grid=(hq // HP, s // tq),
        in_specs=[pl.BlockSpec((tq, HP * dh), lambda h, i: (i, h)),
                  pl.BlockSpec((1, s, dh), lambda h, i: (h * HP // REP, 0, 0)),
                  pl.BlockSpec((1, s, 2 * dh), lambda h, i: (h * HP // REP, 0, 0))],
        out_specs=[pl.BlockSpec((tq, HP * dh), lambda h, i: (i, h)),
                   pl.BlockSpec((HP, tq, s), lambda h, i: (h, i, 0)),
                   pl.BlockSpec((HP, tq, 1), lambda h, i: (h, i, 0))],
        out_shape=[jax.ShapeDtypeStruct((s, hq * dh), bf16), jax.ShapeDtypeStruct((hq, s, s), bf16),
                   jax.ShapeDtypeStruct((hq, s, 1), f32)],
        compiler_params=_cparams(dimension_semantics=("parallel", "arbitrary")),
    )(q, k, v1)


def _attn_bwd(p, do, o, q, k, v, linv):
    hq, s, _ = p.shape
    dh = HEAD_DIM
    tq = _pick(s, (256, 128))

    def body(p_ref, do_ref, o_ref, q_ref, k_ref, v_ref, linv_ref, dq_ref, dkt_ref, dvt_ref):
        @pl.when(pl.program_id(1) == 0)
        def _():
            dkt_ref[...] = jnp.zeros_like(dkt_ref)
            dvt_ref[...] = jnp.zeros_like(dvt_ref)

        for j in range(HP):
            sl = slice(j * dh, (j + 1) * dh)
            pp, doh, li = p_ref[j], do_ref[:, sl], linv_ref[j]
            do32 = doh.astype(f32)
            d = jnp.sum(do32 * o_ref[:, sl].astype(f32), axis=-1, keepdims=True)
            dp = lax.dot_general(doh, v_ref[0], _DIMS["nt"], preferred_element_type=f32)
            ds = (pp.astype(f32) * ((dp - d) * li)).astype(bf16)
            dq_ref[:, sl] = (jnp.dot(ds, k_ref[0], preferred_element_type=f32) * LN2).astype(dq_ref.dtype)
            dvt_ref[j] += lax.dot_general((do32 * li).astype(bf16), pp, _DIMS["tn"], preferred_element_type=f32)
            dkt_ref[j] += lax.dot_general(q_ref[:, sl], ds, _DIMS["tn"], preferred_element_type=f32)

    def row():
        return pl.BlockSpec((tq, HP * dh), lambda h, i: (i, h))

    return pl.pallas_call(
        body, name="attn_bwd", grid=(hq // HP, s // tq),
        in_specs=[pl.BlockSpec((HP, tq, s), lambda h, i: (h, i, 0)), row(), row(), row(),
                  pl.BlockSpec((1, s, dh), lambda h, i: (h * HP // REP, 0, 0)),
                  pl.BlockSpec((1, s, dh), lambda h, i: (h * HP // REP, 0, 0)),
                  pl.BlockSpec((HP, tq, 1), lambda h, i: (h, i, 0))],
        out_specs=[row(), pl.BlockSpec((HP, dh, s), lambda h, i: (h, 0, 0)),
                   pl.BlockSpec((HP, dh, s), lambda h, i: (h, 0, 0))],
        out_shape=[jax.ShapeDtypeStruct((s, hq * dh), q.dtype), jax.ShapeDtypeStruct((hq, dh, s), f32),
                   jax.ShapeDtypeStruct((hq, dh, s), f32)],
        compiler_params=_cparams(dimension_semantics=("parallel", "arbitrary")),
    )(p, do, o, q, k, v, linv)


@jax.custom_vjp
def attention(q, k, v):
    return _attn_fwd(q, k, v)[0]


def _attention_fwd(q, k, v):
    o, p, linv = _attn_fwd(q, k, v)
    return o, (q, k, v, o, p, linv)


def _attention_bwd(res, do):
    q, k, v, o, p, linv = res
    s = q.shape[0]
    dq, dkt, dvt = _attn_bwd(p, do.astype(bf16), o, q, k, v, linv)

    def per_kv_head(t):
        return jnp.swapaxes(t.reshape(N_KV_HEADS, REP, HEAD_DIM, s).sum(axis=1), 1, 2)

    return dq, (per_kv_head(dkt) * LN2).astype(k.dtype), per_kv_head(dvt).astype(v.dtype)


attention.defvjp(_attention_fwd, _attention_bwd)


HPG = N_SSD_HEADS // N_SSD_GROUPS
GW = HPG * SSD_HEAD_DIM
NEG = -1e30
SPLIT_ROWS = 32


def _ssd_consts():
    k = np.arange(SPLIT_ROWS)[:, None]
    live = k < 3 * HPG
    sel_chunk = ((k % HPG) == (np.arange(HPG * CHUNK)[None, :] // CHUNK)) & live
    sel_head = ((k % HPG) == (np.arange(GW)[None, :] // SSD_HEAD_DIM)) & live
    return jnp.asarray(sel_chunk, bf16), jnp.asarray(sel_head, bf16)


def _split3(x):
    hi = x.astype(bf16).astype(f32)
    r1 = x - hi
    mid = r1.astype(bf16).astype(f32)
    lo = (r1 - mid).astype(bf16).astype(f32)
    return jnp.concatenate([hi, mid, lo, jnp.zeros_like(hi)], axis=0).astype(bf16)


def _tn(a, b):
    return lax.dot_general(a, b, _DIMS["tn"], preferred_element_type=f32)


def _nt(a, b):
    return lax.dot_general(a, b, _DIMS["nt"], preferred_element_type=f32)


def _nn(a, b):
    return jnp.dot(a, b, preferred_element_type=f32)


def _head_sum(sel8, x):
    hi = x.astype(bf16)
    lo = (x - hi.astype(f32)).astype(bf16)
    return _nt(sel8, hi) + _nt(sel8, lo)


def _ssd_masks(reverse):
    r = lax.broadcasted_iota(jnp.int32, (CHUNK, CHUNK), 0)
    c = lax.broadcasted_iota(jnp.int32, (CHUNK, CHUNK), 1)
    lower, upper = r >= c, r <= c
    return (upper, lower) if reverse else (lower, upper)


def _ssd_in_specs(cidx):
    return [pl.BlockSpec((CHUNK, D_INNER), lambda c: (cidx(c), 0)),
            pl.BlockSpec((CHUNK, GN), lambda c: (cidx(c), D_INNER // GN)),
            pl.BlockSpec((CHUNK, GN), lambda c: (cidx(c), D_INNER // GN + 1)),
            pl.BlockSpec((N_SSD_HEADS, CHUNK), lambda c: (0, cidx(c))),
            pl.BlockSpec((N_SSD_HEADS, 1), lambda c: (0, 0)),
            pl.BlockSpec((SPLIT_ROWS, HPG * CHUNK), lambda c: (0, 0)),
            pl.BlockSpec((SPLIT_ROWS, GW), lambda c: (0, 0))]


def _ssd_chunk_common(dtt_ref, a_ref, et_ref, mask_t):
    dtt = dtt_ref[...]
    et = jnp.dot(dtt * a_ref[...], mask_t.astype(f32), precision=HIGHEST, preferred_element_type=f32)
    et_ref[...] = et
    return dtt, et


def _ssd_group_common(g, dtt, et, selc_ref, selh_ref, xs_ref, b_ref, c_ref, last):
    gr = slice(g * HPG, (g + 1) * HPG)
    e3 = _split3(et[gr])
    col = _tn(e3, selc_ref[...])
    eb = _tn(e3, selh_ref[...])
    dtb = _tn(_split3(dtt[gr]), selh_ref[...])
    tbc = eb[last:last + 1, :]
    xs = xs_ref[:, g * GW:(g + 1) * GW]
    bg = b_ref[:, g * D_STATE:(g + 1) * D_STATE].astype(bf16)
    cg = c_ref[:, g * D_STATE:(g + 1) * D_STATE].astype(bf16)
    return col, eb, dtb, tbc, xs, bg, cg


def _ssd_fwd(xbc, dtt, a_col, reverse, y_prev=None, dexp=None):
    s = xbc.shape[0]
    nc = s // CHUNK
    cidx = (lambda c: nc - 1 - c) if reverse else (lambda c: c)
    last = 0 if reverse else CHUNK - 1
    selc, selh = _ssd_consts()
    final = y_prev is not None
    n_in = 9 if final else 7

    def body(*refs):
        xs_ref, b_ref, c_ref, dtt_ref, a_ref, selc_ref, selh_ref = refs[:7]
        y_ref, st_ref, ht_ref, et_ref, yg_ref = refs[n_in:]

        @pl.when(pl.program_id(0) == 0)
        def _():
            ht_ref[...] = jnp.zeros_like(ht_ref)

        mask, mask_t = _ssd_masks(reverse)
        dtt_v, et = _ssd_chunk_common(dtt_ref, a_ref, et_ref, mask_t)
        for g in range(N_SSD_GROUPS):
            col, eb, dtb, tbc, xs, bg, cg = _ssd_group_common(g, dtt_v, et, selc_ref, selh_ref, xs_ref, b_ref, c_ref,
                                                              last)
            xd = xs * dtb
            cb = _nt(cg, bg)
            ht = ht_ref[g]
            st_ref[0, g] = ht
            yoff = _nn(cg, ht.astype(bf16)) * jnp.exp(eb)
            for j in range(HPG):
                h = g * HPG + j
                hs = slice(j * SSD_HEAD_DIM, (j + 1) * SSD_HEAD_DIM)
                lam = jnp.exp(jnp.where(mask, col[:, j * CHUNK:(j + 1) * CHUNK] - et_ref[h:h + 1, :], NEG))
                yg_ref[:, hs] = _nn((cb * lam).astype(bf16), xd[:, hs].astype(bf16))
            cols = slice(g * GW, (g + 1) * GW)
            yg = yg_ref[...] + yoff
            if final:
                yg = yg + refs[7][:, cols] + xs * refs[8][:, cols]
            y_ref[:, cols] = yg.astype(y_ref.dtype)
            ht_ref[g] = jnp.exp(tbc) * ht + _tn(bg, (xd * jnp.exp(tbc - eb)).astype(bf16))

    y_spec = pl.BlockSpec((CHUNK, D_INNER), lambda c: (cidx(c), 0))
    extra_specs = [y_spec, pl.BlockSpec((1, D_INNER), lambda c: (0, 0))] if final else []
    return pl.pallas_call(
        body, name="ssd_fwd_rev" if reverse else "ssd_fwd", grid=(nc,),
        in_specs=_ssd_in_specs(cidx) + extra_specs,
        out_specs=[y_spec, pl.BlockSpec((1, N_SSD_GROUPS, D_STATE, GW), lambda c: (cidx(c), 0, 0, 0))],
        out_shape=[jax.ShapeDtypeStruct((s, D_INNER), bf16 if final else f32),
                   jax.ShapeDtypeStruct((nc, N_SSD_GROUPS, D_STATE, GW), f32)],
        scratch_shapes=[pltpu.VMEM((N_SSD_GROUPS, D_STATE, GW), f32), pltpu.VMEM((N_SSD_HEADS, CHUNK), f32),
                        pltpu.VMEM((CHUNK, GW), f32)],
        compiler_params=_cparams(dimension_semantics=("arbitrary",)),
    )(xbc, xbc, xbc, dtt, a_col, selc, selh, *((y_prev, dexp) if final else ()))


def _ssd_bwd(xbc, dtt, a_col, states, dy, reverse, dxbc_prev=None, dexp=None):
    s = xbc.shape[0]
    nc = s // CHUNK
    cidx = (lambda c: c) if reverse else (lambda c: nc - 1 - c)
    last = 0 if reverse else CHUNK - 1
    selc, selh = _ssd_consts()
    final = dxbc_prev is not None
    n_in = 11 if final else 9
    n_out = 4 if final else 3

    def body(*refs):
        xs_ref, b_ref, c_ref, dtt_ref, a_ref, selc_ref, selh_ref, st_ref, dy_ref = refs[:9]
        dxbc_ref, ddtt_ref, da_ref = refs[n_in:n_in + 3]
        dh_ref, et_ref, det_ref, det2_ref, ddt_ref, q_ref = refs[n_in + n_out:]
        if final:
            prev_ref, dexp_ref, ddexp_ref = refs[9], refs[10], refs[n_in + 3]

        @pl.when(pl.program_id(0) == 0)
        def _():
            dh_ref[...] = jnp.zeros_like(dh_ref)
            da_ref[...] = jnp.zeros_like(da_ref)
            if final:
                ddexp_ref[...] = jnp.zeros_like(ddexp_ref)

        mask, mask_t = _ssd_masks(reverse)
        dtt_v, et = _ssd_chunk_common(dtt_ref, a_ref, et_ref, mask_t)
        sel8 = selh_ref[0:HPG, :]
        is_last = lax.broadcasted_iota(jnp.int32, (CHUNK, GW), 0) == last
        for g in range(N_SSD_GROUPS):
            col, eb, dtb, tbc, xs, bg, cg = _ssd_group_common(g, dtt_v, et, selc_ref, selh_ref, xs_ref, b_ref, c_ref,
                                                              last)
            xd = xs * dtb
            cb = _nt(cg, bg)
            cbt = _nt(bg, cg)
            exp_t = jnp.exp(tbc)
            dfac = jnp.exp(tbc - eb)
            ht = st_ref[0, g]
            dhn = dh_ref[g]
            ht16, dhn16 = ht.astype(bf16), dhn.astype(bf16)
            dy = dy_ref[:, g * GW:(g + 1) * GW].astype(f32)
            dye = dy * jnp.exp(eb)
            dye16 = dye.astype(bf16)
            dc = _nt(dye16, ht16)
            dh_ref[g] = exp_t * dhn + _tn(cg, dye16)
            deb = dye * _nn(cg, ht16)
            xdd = xd * dfac
            dxdd = _nn(bg, dhn16)
            db = _nt(xdd.astype(bf16), dhn16)
            dxd_state = dxdd * dfac
            ddf = dxdd * xdd
            dtbc = jnp.sum(ddf, axis=0, keepdims=True) + exp_t * jnp.sum(dhn * ht, axis=0, keepdims=True)
            deb = deb - ddf + jnp.where(is_last, dtbc, 0.0)
            dcb = jnp.zeros((CHUNK, CHUNK), f32)
            dcbt = jnp.zeros((CHUNK, CHUNK), f32)
            for j in range(HPG):
                h = g * HPG + j
                hs = slice(j * SSD_HEAD_DIM, (j + 1) * SSD_HEAD_DIM)
                colj = col[:, j * CHUNK:(j + 1) * CHUNK]
                row = et_ref[h:h + 1, :]
                lam = jnp.exp(jnp.where(mask, colj - row, NEG))
                lam_t = lam.T
                xdj, dyj = xd[:, hs].astype(bf16), dy[:, hs].astype(bf16)
                t1 = _nt(dyj, xdj) * lam
                t2 = _nt(xdj, dyj) * lam_t
                dcb, dcbt = dcb + t1, dcbt + t2
                det_ref[h:h + 1, :] = -jnp.sum(t1 * cb - t2 * cbt, axis=0, keepdims=True)
                q_ref[:, hs] = _nn((cbt * lam_t).astype(bf16), dyj)
            x_cols = slice(g * GW, (g + 1) * GW)
            dxd = q_ref[...] + dxd_state
            dxs = dxd * dtb
            if final:
                dxs = dxs + prev_ref[:, x_cols] + dy * dexp_ref[:, x_cols]
            dxbc_ref[:, x_cols] = dxs
            b_cols = slice(D_INNER + g * D_STATE, D_INNER + (g + 1) * D_STATE)
            c_cols = slice(D_INNER + GN + g * D_STATE, D_INNER + GN + (g + 1) * D_STATE)
            db = db + _nn(dcbt.astype(bf16), cg)
            dc = dc + _nn(dcb.astype(bf16), bg)
            if final:
                db, dc = db + prev_ref[:, b_cols], dc + prev_ref[:, c_cols]
                ddexp_ref[:, g * GW:(g + 1) * GW] += jnp.sum(dy * xs, axis=0, keepdims=True)
            dxbc_ref[:, b_cols] = db
            dxbc_ref[:, c_cols] = dc
            det2_ref[g * HPG:(g + 1) * HPG, :] = _head_sum(sel8, deb)
            ddt_ref[g * HPG:(g + 1) * HPG, :] = _head_sum(sel8, dxd * xs)
        dat = jnp.dot(det_ref[...] + det2_ref[...], mask.astype(f32), precision=HIGHEST, preferred_element_type=f32)
        ddtt_ref[...] = ddt_ref[...] + dat * a_ref[...]
        da_ref[...] += jnp.sum(dat * dtt_v, axis=1, keepdims=True)

    in_specs = _ssd_in_specs(cidx) + [
        pl.BlockSpec((1, N_SSD_GROUPS, D_STATE, GW), lambda c: (cidx(c), 0, 0, 0)),
        pl.BlockSpec((CHUNK, D_INNER), lambda c: (cidx(c), 0))]
    hl = pltpu.VMEM((N_SSD_HEADS, CHUNK), f32)
    dxbc_spec = pl.BlockSpec((CHUNK, CONV_DIM), lambda c: (cidx(c), 0))
    dexp_spec = pl.BlockSpec((1, D_INNER), lambda c: (0, 0))
    return pl.pallas_call(
        body, name="ssd_bwd_rev" if reverse else "ssd_bwd", grid=(nc,),
        in_specs=in_specs + ([dxbc_spec, dexp_spec] if final else []),
        out_specs=[dxbc_spec, pl.BlockSpec((N_SSD_HEADS, CHUNK), lambda c: (0, cidx(c))),
                   pl.BlockSpec((N_SSD_HEADS, 1), lambda c: (0, 0))] + ([dexp_spec] if final else []),
        out_shape=[jax.ShapeDtypeStruct((s, CONV_DIM), f32), jax.ShapeDtypeStruct((N_SSD_HEADS, s), f32),
                   jax.ShapeDtypeStruct((N_SSD_HEADS, 1), f32)]
        + ([jax.ShapeDtypeStruct((1, D_INNER), f32)] if final else []),
        scratch_shapes=[pltpu.VMEM((N_SSD_GROUPS, D_STATE, GW), f32), hl, hl, hl, hl, pltpu.VMEM((CHUNK, GW), f32)],
        compiler_params=_cparams(dimension_semantics=("arbitrary",)),
    )(xbc, xbc, xbc, dtt, a_col, selc, selh, states, dy, *((dxbc_prev, dexp) if final else ()))


@jax.custom_vjp
def ssd_bidir(xbc, dtt, a_col, dexp):
    y_f, _ = _ssd_fwd(xbc, dtt[:N_SSD_HEADS], a_col[:N_SSD_HEADS], False)
    return _ssd_fwd(xbc, dtt[N_SSD_HEADS:], a_col[N_SSD_HEADS:], True, y_prev=y_f, dexp=dexp)[0]


def _ssd_bidir_fwd(xbc, dtt, a_col, dexp):
    y_f, st_f = _ssd_fwd(xbc, dtt[:N_SSD_HEADS], a_col[:N_SSD_HEADS], False)
    y, st_b = _ssd_fwd(xbc, dtt[N_SSD_HEADS:], a_col[N_SSD_HEADS:], True, y_prev=y_f, dexp=dexp)
    return y, (xbc, dtt, a_col, dexp, st_f, st_b)


def _ssd_bidir_bwd(res, dy):
    xbc, dtt, a_col, dexp, st_f, st_b = res
    dxbc_f, ddtt_f, da_f = _ssd_bwd(xbc, dtt[:N_SSD_HEADS], a_col[:N_SSD_HEADS], st_f, dy, False)
    dxbc, ddtt_b, da_b, ddexp = _ssd_bwd(xbc, dtt[N_SSD_HEADS:], a_col[N_SSD_HEADS:], st_b, dy, True,
                                         dxbc_prev=dxbc_f, dexp=dexp)
    return dxbc, jnp.concatenate([ddtt_f, ddtt_b], axis=0), jnp.concatenate([da_f, da_b], axis=0), ddexp


ssd_bidir.defvjp(_ssd_bidir_fwd, _ssd_bidir_bwd)


def _rope_tables(s):
    rows = s // GRID_W
    pos_row = np.repeat(np.arange(rows), GRID_W).astype(np.float32)
    pos_col = np.tile(np.arange(GRID_W), rows).astype(np.float32)
    axis_dim = HEAD_DIM // 2
    inv_freq = np.float32(ROPE_THETA) ** (-np.arange(0, axis_dim, 2, dtype=np.float32) / np.float32(axis_dim))
    ang_r = pos_row[:, None] * inv_freq[None, :].astype(np.float32)
    ang_c = pos_col[:, None] * inv_freq[None, :].astype(np.float32)
    cos = np.concatenate([np.cos(ang_r), np.cos(ang_r), np.cos(ang_c), np.cos(ang_c)] * 2, axis=-1)
    sin = np.concatenate([np.sin(ang_r), np.sin(ang_r), np.sin(ang_c), np.sin(ang_c)] * 2, axis=-1)
    return jnp.asarray(cos, f32), jnp.asarray(sin, f32)


def _rope_perm():
    p = np.zeros((HEAD_DIM, HEAD_DIM), np.float32)
    for j in range(HEAD_DIM):
        if (j % 32) < 16:
            p[j + 16, j] = -1.0
        else:
            p[j - 16, j] = 1.0
    return p


def local_loss(x, mod, small, recv_in_like, recv_late_like, wfull, late_shard, target):
    s = x.shape[0]
    lin = {n: make_linear("lin_" + n) for n in LATE if not n.startswith("mlp")}
    wfull, wgrads = dict(wfull), {}
    shift1, scale1, gate1, shift2, scale2, gate2 = [mod[i] for i in range(6)]

    norm_mod = make_rowwise("norm_mod", _fn_norm_mod, [(D_MODEL, bf16), (D_MODEL, f32)])
    (h, x_res), _ = norm_mod((x,), (small["norm1_w"], scale1, shift1), (), ())

    proj = dict(zip(PROJ_NAMES, in_proj(h, tuple(wfull[n] for n in PROJ_NAMES), recv_in_like)))

    cos, sin = _rope_tables(s)

    def heads(t, nh):
        return t.reshape(s, nh, HEAD_DIM).transpose(1, 0, 2)

    qr = make_head_rope("q_norm_rope", N_Q_HEADS, Q_SCALE, False)(proj["q"], small["q_norm_w"], cos, sin)
    kr = make_head_rope("k_norm_rope", N_KV_HEADS, 1.0, True)(proj["k"], small["k_norm_w"], cos, sin)
    vh = heads(proj["v"], N_KV_HEADS).astype(bf16)
    att = attention(qr, kr, vh)

    xbc, gathered, *carriers = conv_silu_comm(proj["xbc"], small["conv_w"], small["conv_b"], late_shard,
                                              recv_late_like)
    wfull.update(_split_late(gathered))
    wgrads.update(zip(LATE, carriers))
    ao = lin["attn_out"](att, wfull["attn_out"], wgrads["attn_out"])
    softplus = make_rowwise("dt_softplus", _fn_softplus, [(2 * N_SSD_HEADS, f32)])
    (dt,), _ = softplus((proj["dt"][:, :2 * N_SSD_HEADS],), (small["dt_bias"].reshape(1, 2 * N_SSD_HEADS),), (), ())
    a_neg = -jnp.exp(small["A_log"])
    dexp = jnp.repeat(small["ssd_D"].reshape(N_SSD_HEADS), SSD_HEAD_DIM).reshape(1, D_INNER)
    y = ssd_bidir(xbc, dt.T, a_neg.reshape(2 * N_SSD_HEADS, 1), dexp)
    ssd_gate = make_rowwise("ssd_gate", _fn_ssd_gate, [(D_INNER, bf16)], tm_pref=128)
    (ssd_out,), _ = ssd_gate((y, proj["z"]), (small["ssd_norm_w"],), (), ())
    so = lin["ssd_out"](ssd_out, wfull["ssd_out"], wgrads["ssd_out"])

    merge = make_rowwise("merge", _fn_merge, [(D_MODEL, bf16)])
    (merged,), _ = merge((ao, so, proj["ga"], proj["gs"]), (), (), ())
    mo = lin["o"](merged, wfull["o"], wgrads["o"])

    res_norm = make_rowwise("res_norm", _fn_res_norm, [(D_MODEL, f32), (D_MODEL, bf16)])
    (x1, h2), _ = res_norm((x_res, mo), (gate1, small["norm2_w"], scale2, shift2), (), ())
    ff = mlp(h2, wfull["mlp1"], wgrads["mlp1"], wfull["mlp2"], wgrads["mlp2"])
    loss_op = make_rowwise("loss", _fn_loss, [], [(1, 1)])
    _, (loss,) = loss_op((x1, ff), (gate2,), (), (target,))
    return loss[0, 0]


_BC1 = 1.0 - ADAM_B1 ** ADAM_STEP
_BC2 = 1.0 - ADAM_B2 ** ADAM_STEP


def _adamw(w, g, m, v):
    m = ADAM_B1 * m + (1.0 - ADAM_B1) * g
    v = ADAM_B2 * v + (1.0 - ADAM_B2) * (g * g)
    delta = -ADAM_LR * ((m / _BC1) / (jnp.sqrt(v / _BC2) + ADAM_EPS) + ADAM_WD * w)
    return delta, m, v


def _ada_fwd(c_all, w, b):
    n = w.shape[1]

    def body(c_ref, w_ref, b_ref, o_ref):
        o_ref[...] = jnp.dot(_silu(c_ref[...]), w_ref[...], precision=HIGHEST, preferred_element_type=f32) + b_ref[...]

    return pl.pallas_call(body, name="ada_fwd", out_shape=jax.ShapeDtypeStruct((N_DEV, n), f32),
                          compiler_params=_cparams())(c_all, w, b)


def _ada_bwd_adamw(c_all, dmod, w, m, v):
    d, n = w.shape
    tr = _pick(d, (256, 128))

    def body(c_ref, dm_ref, w_ref, m_ref, v_ref, g_ref, dl_ref, mo_ref, vo_ref):
        g = lax.dot_general(_silu(c_ref[...]), dm_ref[...], _DIMS["tn"], precision=HIGHEST,
                            preferred_element_type=f32)
        g_ref[...] = g
        dl_ref[...], mo_ref[...], vo_ref[...] = _adamw(w_ref[...], g, m_ref[...], v_ref[...])

    blk = pl.BlockSpec((tr, n), lambda i: (i, 0))
    return pl.pallas_call(
        body, name="ada_bwd_adamw", grid=(d // tr,),
        in_specs=[pl.BlockSpec((N_DEV, tr), lambda i: (0, i)), pl.BlockSpec((N_DEV, n), lambda i: (0, 0)), blk, blk, blk],
        out_specs=[blk] * 4, out_shape=[jax.ShapeDtypeStruct((d, n), f32)] * 4,
        compiler_params=_cparams(dimension_semantics=("parallel",)),
    )(c_all, dmod, w, m, v)


def _sum_over_mesh(g):
    def body(g_ref, o_ref):
        acc = g_ref[0]
        for d in range(1, N_DEV):
            acc = acc + g_ref[d]
        o_ref[...] = acc

    return pl.pallas_call(body, name="sum_small", out_shape=jax.ShapeDtypeStruct(g.shape[1:], f32),
                          compiler_params=_cparams())(g)


def _adamw_small(w, g, m, v):
    def body(w_ref, g_ref, m_ref, v_ref, dl_ref, mo_ref, vo_ref):
        dl_ref[...], mo_ref[...], vo_ref[...] = _adamw(w_ref[...], g_ref[...], m_ref[...], v_ref[...])

    return pl.pallas_call(body, name="adamw_small", out_shape=[jax.ShapeDtypeStruct(w.shape, f32)] * 3,
                          compiler_params=_cparams())(w, g, m, v)


def _sum_adamw(recv, w, m, v, name):
    _, r, c = recv.shape
    tr = _pick(r, (256, 128, 64, 16))

    def body(g_ref, w_ref, m_ref, v_ref, go_ref, dl_ref, mo_ref, vo_ref):
        g = g_ref[0].astype(f32)
        for d in range(1, N_DEV):
            g = g + g_ref[d].astype(f32)
        go_ref[...] = g
        dl_ref[...], mo_ref[...], vo_ref[...] = _adamw(w_ref[...], g, m_ref[...], v_ref[...])

    blk = pl.BlockSpec((tr, c), lambda i: (i, 0))
    return pl.pallas_call(
        body, name=name, grid=(r // tr,),
        in_specs=[pl.BlockSpec((N_DEV, tr, c), lambda i: (0, i, 0)), blk, blk, blk],
        out_specs=[blk] * 4, out_shape=[jax.ShapeDtypeStruct((r, c), f32)] * 4,
        compiler_params=_cparams(dimension_semantics=("parallel",)),
    )(recv, w, m, v)


def _pack_small(arrs):
    parts = []
    for a in arrs:
        flat = a.reshape(-1).astype(f32)
        parts.append(jnp.pad(flat, (0, (-flat.shape[0]) % LANE)))
    flat = jnp.concatenate(parts)
    flat = jnp.pad(flat, (0, (-flat.shape[0]) % (8 * LANE)))
    return flat.reshape(-1, LANE)


def _unpack_small(packed, shapes):
    flat = packed.reshape(-1)
    out, off = [], 0
    for shp in shapes:
        n = int(np.prod(shp))
        out.append(flat[off:off + n].reshape(shp))
        off += n + (-n) % LANE
    return out


BIG = ("w_attn_out", "w_ssd_out", "w_o", "w_mlp1", "w_mlp2")
BIG_ROWS = (N_Q_HEADS * HEAD_DIM // N_DEV, D_INNER // N_DEV, D_MODEL // N_DEV,
            D_MODEL * (D_FF // N_DEV) // PACK_COLS, D_FF // N_DEV)
N_IN_SHARD = D_IN_PROJ // N_DEV
assert sum(BIG_ROWS) % 16 == 0


def _pack_big(shards, dtype):
    return jnp.concatenate([s.astype(dtype).reshape(-1, PACK_COLS) for s in shards], axis=0)


def _unpack_big(packed, shapes):
    out, off = [], 0
    for rows, shp in zip(BIG_ROWS, shapes):
        out.append(packed[off:off + rows].reshape(shp))
        off += rows
    return out


LATE = ("attn_out", "ssd_out", "o", "mlp1", "mlp2")
LATE_SHAPES = ((N_Q_HEADS * HEAD_DIM, D_MODEL), (D_INNER, D_MODEL), (D_MODEL, D_MODEL), (D_MODEL, D_FF),
               (D_FF, D_MODEL))


def _split_w_in(g_in):
    w_in = g_in.transpose(1, 0, 2).reshape(D_MODEL, D_IN_PROJ)
    w = {}
    off = 0
    for name, size in zip(PROJ_NAMES, PROJ_SIZES):
        w[name] = w_in[:, off:off + size]
        off += size
    w["dt"] = jnp.pad(w["dt"], ((0, 0), (0, DT_PAD - 2 * N_SSD_HEADS)))
    return w


def _split_late(g):
    offs = np.cumsum((0,) + BIG_ROWS)
    sl = [g[:, offs[i]:offs[i + 1]] for i in range(len(BIG))]
    return {"attn_out": sl[0].reshape(LATE_SHAPES[0]), "ssd_out": sl[1].reshape(LATE_SHAPES[1]),
            "o": sl[2].reshape(LATE_SHAPES[2]),
            "mlp1": sl[3].reshape(N_DEV, D_MODEL, D_FF // N_DEV).transpose(1, 0, 2).reshape(LATE_SHAPES[3]),
            "mlp2": sl[4].reshape(LATE_SHAPES[4])}


def _pack_in_grads(gw):
    gw = {n: g.astype(bf16) for n, g in gw.items()}
    gw["dt"] = gw["dt"][:, :2 * N_SSD_HEADS]
    g_in = jnp.concatenate([gw[n] for n in PROJ_NAMES], axis=1)
    return g_in.reshape(D_MODEL, N_DEV, N_IN_SHARD).transpose(1, 0, 2)


def _pack_late_grads(gw):
    gw = {n: g.astype(bf16) for n, g in gw.items()}
    parts = [
        gw["attn_out"].reshape(N_DEV, -1, PACK_COLS),
        gw["ssd_out"].reshape(N_DEV, -1, PACK_COLS),
        gw["o"].reshape(N_DEV, -1, PACK_COLS),
        gw["mlp1"].reshape(D_MODEL, N_DEV, D_FF // N_DEV).transpose(1, 0, 2).reshape(N_DEV, -1, PACK_COLS),
        gw["mlp2"].reshape(N_DEV, -1, PACK_COLS),
    ]
    return jnp.concatenate(parts, axis=1)


SMALL = ("norm1_w", "norm2_w", "q_norm_w", "k_norm_w", "conv_w", "conv_b", "A_log", "dt_bias", "ssd_D", "ssd_norm_w")


def kernel(x, c, w_ada, b_ada, norm1_w, norm2_w, w_in, q_norm_w, k_norm_w, conv_w, conv_b, A_log, dt_bias, ssd_D, ssd_norm_w, w_attn_out, w_ssd_out, w_o, w_mlp1, w_mlp2, loss_target, m_w_ada, m_b_ada, m_norm1_w, m_norm2_w, m_w_in, m_q_norm_w, m_k_norm_w, m_conv_w, m_conv_b, m_A_log, m_dt_bias, m_ssd_D, m_ssd_norm_w, m_w_attn_out, m_w_ssd_out, m_w_o, m_w_mlp1, m_w_mlp2, v_w_ada, v_b_ada, v_norm1_w, v_norm2_w, v_w_in, v_q_norm_w, v_k_norm_w, v_conv_w, v_conv_b, v_A_log, v_dt_bias, v_ssd_D, v_ssd_norm_w, v_w_attn_out, v_w_ssd_out, v_w_o, v_w_mlp1, v_w_mlp2):
    args = dict(locals())
    me = _my_index()
    n_ada = 6 * D_MODEL // N_DEV
    n_cw = CONV_DIM // N_DEV

    blk = jnp.zeros((8, D_MODEL), f32)
    blk = blk.at[0:1, :].set(c)
    blk = blk.at[1:1 + D_CONV, :n_cw].set(conv_w[0])
    g0 = _all_gather(blk, "gather_c_convw", in_vmem=True)
    c_all = g0[:, 0, :]
    conv_w_full = g0[:, 1:1 + D_CONV, :n_cw].transpose(1, 0, 2).reshape(D_CONV, CONV_DIM)

    b_shard = lax.dynamic_slice(b_ada, (0, me * n_ada), (1, n_ada))
    mod_cols = _ada_fwd(c_all, w_ada[0], b_shard)
    g1 = _all_gather(mod_cols, "gather_mod", in_vmem=True)
    mod_mine = lax.dynamic_index_in_dim(g1, me, axis=1, keepdims=False)
    mod = mod_mine.reshape(6, 1, D_MODEL)

    big_shapes = [args[n].shape[1:] for n in BIG]
    late_shard = _pack_big([args[n][0] for n in BIG], bf16)
    wfull = _split_w_in(_all_gather(w_in[0].astype(bf16), "gather_w_in", in_vmem=False))
    recv_in_like = jnp.zeros((N_DEV,) + w_in.shape[1:], bf16)
    recv_late_like = jnp.zeros((N_DEV,) + late_shard.shape, bf16)

    small = {"norm1_w": norm1_w, "norm2_w": norm2_w, "q_norm_w": q_norm_w, "k_norm_w": k_norm_w,
             "conv_w": conv_w_full, "conv_b": conv_b, "A_log": A_log[0], "dt_bias": dt_bias[0], "ssd_D": ssd_D,
             "ssd_norm_w": ssd_norm_w}

    loss, (gx, gmod, gsmall, recv_in, recv_late) = jax.value_and_grad(local_loss, argnums=(0, 1, 2, 3, 4))(
        x[0], mod, small, recv_in_like, recv_late_like, wfull, late_shard, loss_target[0])

    small_list = [gmod, gsmall["norm1_w"], gsmall["norm2_w"], gsmall["q_norm_w"], gsmall["k_norm_w"], gsmall["conv_w"],
                  gsmall["conv_b"], gsmall["A_log"], gsmall["dt_bias"], gsmall["ssd_D"], gsmall["ssd_norm_w"],
                  loss.reshape(1)]
    small_shapes = [a.shape for a in small_list]
    g2 = _all_gather(_pack_small(small_list), "gather_small_grads", in_vmem=True)
    summed = _unpack_small(_sum_over_mesh(g2), small_shapes)
    loss_total = summed[-1][0]
    g_b_ada = summed[0].reshape(1, 6 * D_MODEL)
    g_small = dict(zip(SMALL, summed[1:-1]))
    g_conv_w = lax.dynamic_slice(g_small["conv_w"], (0, me * n_cw), (D_CONV, n_cw))

    dmod_all = g2[:, :6 * D_MODEL // LANE, :].reshape(N_DEV, 6 * D_MODEL)
    dmod_shard = lax.dynamic_slice(dmod_all, (0, me * n_ada), (N_DEV, n_ada))
    ada = _ada_bwd_adamw(c_all, dmod_shard, w_ada[0], m_w_ada[0], v_w_ada[0])

    small_grads = {"b_ada": g_b_ada, "norm1_w": g_small["norm1_w"], "norm2_w": g_small["norm2_w"],
                   "q_norm_w": g_small["q_norm_w"], "k_norm_w": g_small["k_norm_w"], "conv_w": g_conv_w[None],
                   "conv_b": g_small["conv_b"], "A_log": g_small["A_log"][None], "dt_bias": g_small["dt_bias"][None],
                   "ssd_D": g_small["ssd_D"], "ssd_norm_w": g_small["ssd_norm_w"]}
    sm_names = list(small_grads)
    sm_shapes = [args[n].shape for n in sm_names]
    sm = _adamw_small(_pack_small([args[n] for n in sm_names]), _pack_small([small_grads[n] for n in sm_names]),
                      _pack_small([args["m_" + n] for n in sm_names]), _pack_small([args["v_" + n] for n in sm_names]))
    sm_delta, sm_m, sm_v = [dict(zip(sm_names, _unpack_small(t, sm_shapes))) for t in sm]
    small_grads = {n: small_grads[n].reshape(args[n].shape) for n in sm_names}

    w_in_out = _sum_adamw(recv_in, w_in[0], m_w_in[0], v_w_in[0], "sum_adamw_w_in")
    big = _sum_adamw(recv_late, _pack_big([args[n][0] for n in BIG], f32),
                     _pack_big([args["m_" + n][0] for n in BIG], f32),
                     _pack_big([args["v_" + n][0] for n in BIG], f32), "sum_adamw")
    big_g, big_delta, big_m, big_v = [dict(zip(BIG, [t[None] for t in _unpack_big(p, big_shapes)])) for p in big]
    big_g["w_in"], big_delta["w_in"], big_m["w_in"], big_v["w_in"] = [t[None] for t in w_in_out]

    names = ("w_ada", "b_ada", "norm1_w", "norm2_w", "w_in", "q_norm_w", "k_norm_w", "conv_w", "conv_b", "A_log",
             "dt_bias", "ssd_D", "ssd_norm_w", "w_attn_out", "w_ssd_out", "w_o", "w_mlp1", "w_mlp2")
    grads, deltas, new_m, new_v = {}, {}, {}, {}
    for n in names:
        if n == "w_ada":
            grads[n], deltas[n], new_m[n], new_v[n] = [t[None] for t in ada]
        elif n in big_g:
            grads[n], deltas[n], new_m[n], new_v[n] = big_g[n], big_delta[n], big_m[n], big_v[n]
        else:
            grads[n], deltas[n], new_m[n], new_v[n] = small_grads[n], sm_delta[n], sm_m[n], sm_v[n]
    return (loss_total, gx[None], *[grads[n] for n in names], *[deltas[n] for n in names],
            *[new_m[n] for n in names], *[new_v[n] for n in names])
```

```python
import functools
import math

import jax
import jax.numpy as jnp
import numpy as np
from jax import lax
from jax.experimental import pallas as pl
from jax.experimental.pallas import tpu as pltpu

f32 = jnp.float32
bf16 = jnp.bfloat16
HIGHEST = lax.Precision.HIGHEST
MESH = pl.DeviceIdType.MESH

N_DEV = 8
D_MODEL = 1024
GRID_W = 64
N_Q_HEADS = 16
N_KV_HEADS = 4
HEAD_DIM = 64
ROPE_THETA = 10000.0
D_INNER = 2048
SSD_HEAD_DIM = 64
N_SSD_HEADS = 32
N_SSD_GROUPS = 4
D_STATE = 128
D_CONV = 5
CHUNK = 128
D_FF = 4096
EPS = 1e-6
CONV_DIM = D_INNER + 2 * N_SSD_GROUPS * D_STATE
GN = N_SSD_GROUPS * D_STATE
PROJ_NAMES = ("q", "k", "v", "xbc", "z", "dt", "ga", "gs")
PROJ_SIZES = (N_Q_HEADS * HEAD_DIM, N_KV_HEADS * HEAD_DIM, N_KV_HEADS * HEAD_DIM, CONV_DIM, D_INNER,
              2 * N_SSD_HEADS, D_MODEL, D_MODEL)
D_IN_PROJ = sum(PROJ_SIZES)
PROJ_DTYPES = (jnp.bfloat16, jnp.bfloat16, jnp.bfloat16, jnp.float32, jnp.bfloat16, jnp.float32, jnp.bfloat16,
               jnp.bfloat16)
DT_PAD = 128

ADAM_LR, ADAM_B1, ADAM_B2, ADAM_EPS, ADAM_WD, ADAM_STEP = 0.001, 0.9, 0.999, 1e-08, 0.01, 10

V7X_VMEM_LIMIT = 56 * 1024 * 1024
LANE = 128
PACK_COLS = 1024


def _cparams(**kw):
    return pltpu.CompilerParams(vmem_limit_bytes=V7X_VMEM_LIMIT, **kw)


def _pick(dim, prefs):
    for p in prefs:
        if dim % p == 0:
            return p
    return dim


def _my_index():
    return 4 * lax.axis_index("x") + 2 * lax.axis_index("y") + lax.axis_index("c")


COMM_SEMS = [pltpu.SemaphoreType.DMA((7,)), pltpu.SemaphoreType.DMA((7,)), pltpu.SemaphoreType.DMA]


def _gather_phases(x_ref, out_ref, send_sems, recv_sems, local_sem):
    x, y, cc = lax.axis_index("x"), lax.axis_index("y"), lax.axis_index("c")
    me, sibling = (x, y, cc), (x, y, 1 - cc)
    chips = [(1 - x, y), (x, 1 - y), (1 - x, 1 - y)]

    def slot(px, py, pc):
        return out_ref.at[4 * px + 2 * py + pc]

    def copy(k, blk, to, src=None):
        return pltpu.make_async_remote_copy(
            src_ref=slot(*blk) if src is None else src, dst_ref=slot(*blk),
            send_sem=send_sems.at[k], recv_sem=recv_sems.at[k], device_id=to, device_id_type=MESH)

    mine = pltpu.make_async_copy(x_ref, slot(*me), local_sem)
    first = [copy(0, me, sibling, src=x_ref)]
    first += [copy(1 + j, me, (*chip, cc), src=x_ref) for j, chip in enumerate(chips)]
    passed = [copy(4 + j, (*chip, cc), sibling) for j, chip in enumerate(chips)]

    def start():
        mine.start()
        for cp in first:
            cp.start()

    def finish():
        for j, chip in enumerate(chips):
            copy(1 + j, (*chip, cc), me).wait_recv()
            passed[j].start()
        copy(0, sibling, me).wait_recv()
        for j, chip in enumerate(chips):
            copy(4 + j, (*chip, 1 - cc), me).wait_recv()
        for cp in first + passed:
            cp.wait_send()
        mine.wait()

    return start, finish


def _scatter_phases(g_ref, out_ref, send_sems, recv_sems, local_sem):
    x, y, cc = lax.axis_index("x"), lax.axis_index("y"), lax.axis_index("c")
    me = 4 * x + 2 * y + cc
    mine = pltpu.make_async_copy(g_ref.at[me], out_ref.at[me], local_sem)

    def copy(k):
        fx, fy, fc = (k >> 2) & 1, (k >> 1) & 1, k & 1
        px = x + fx - 2 * x * fx
        py = y + fy - 2 * y * fy
        pc = cc + fc - 2 * cc * fc
        peer = 4 * px + 2 * py + pc
        send = pltpu.make_async_remote_copy(
            src_ref=g_ref.at[peer], dst_ref=out_ref.at[me],
            send_sem=send_sems.at[k - 1], recv_sem=recv_sems.at[k - 1],
            device_id=(px, py, pc), device_id_type=MESH)
        recv = pltpu.make_async_remote_copy(
            src_ref=g_ref.at[peer], dst_ref=out_ref.at[peer],
            send_sem=send_sems.at[k - 1], recv_sem=recv_sems.at[k - 1],
            device_id=(px, py, pc), device_id_type=MESH)
        return send, recv

    pairs = [copy(k) for k in range(1, N_DEV)]

    def start():
        mine.start()
        for send, _ in pairs:
            send.start()

    def finish():
        for _, recv in pairs:
            recv.wait_recv()
        for send, _ in pairs:
            send.wait_send()
        mine.wait()

    return start, finish


def _all_gather(block, name, in_vmem):
    r, c = block.shape

    def body(x_ref, out_ref, send_sems, recv_sems, local_sem):
        start, finish = _gather_phases(x_ref, out_ref, send_sems, recv_sems, local_sem)
        start()
        finish()

    space = pltpu.VMEM if in_vmem else pl.ANY
    return pl.pallas_call(
        body, name=name,
        out_shape=jax.ShapeDtypeStruct((N_DEV, r, c), block.dtype),
        in_specs=[pl.BlockSpec(memory_space=space)],
        out_specs=pl.BlockSpec(memory_space=space),
        scratch_shapes=[pltpu.SemaphoreType.DMA((7,)), pltpu.SemaphoreType.DMA((7,)), pltpu.SemaphoreType.DMA],
    )(block)


def _scatter_blocks(g, name):
    _, r, c = g.shape

    def body(g_ref, out_ref, send_sems, recv_sems, local_sem):
        start, finish = _scatter_phases(g_ref, out_ref, send_sems, recv_sems, local_sem)
        start()
        finish()

    return pl.pallas_call(
        body, name=name,
        out_shape=jax.ShapeDtypeStruct(g.shape, g.dtype),
        in_specs=[pl.BlockSpec(memory_space=pl.ANY)],
        out_specs=pl.BlockSpec(memory_space=pl.ANY),
        scratch_shapes=[pltpu.SemaphoreType.DMA((7,)), pltpu.SemaphoreType.DMA((7,)), pltpu.SemaphoreType.DMA],
    )(g)


_DIMS = {"nn": (((1,), (0,)), ((), ())), "nt": (((1,), (1,)), ((), ())), "tn": (((0,), (0,)), ((), ()))}


def _matmul(a, b, mode, out_dtype, name, epilogue=None, side=None):
    if mode == "nn":
        (m, k), (_, n) = a.shape, b.shape
    elif mode == "nt":
        (m, k), (n, _) = a.shape, b.shape
    else:
        (k, m), (_, n) = a.shape, b.shape
    tm = _pick(m, (1024, 512, 256, 128))
    if mode == "tn":
        tn = _pick(n, (1536, 1024, 512, 256, 128))
        tk = _pick(k, (2048, 1024, 512, 256, 128)) if b.dtype == bf16 else _pick(k, (1024, 512, 256, 128))
    else:
        tn = _pick(n, (1024, 512, 384, 256, 128))
        tk = _pick(k, (2048, 1024, 512, 256, 128)) if a.dtype == bf16 else _pick(k, (1024, 512, 256, 128))
    nk = k // tk
    dims = _DIMS[mode]
    n_in = 3 if epilogue == "drelu2" else 2

    def body(*refs):
        a_ref, b_ref = refs[:2]
        o_ref, acc_ref = refs[n_in], refs[n_in + 1]
        kk = pl.program_id(2)
        part = lax.dot_general(a_ref[...].astype(bf16), b_ref[...].astype(bf16), dims, preferred_element_type=f32)

        def finish(acc):
            if epilogue == "relu2":
                r = jnp.maximum(acc, 0.0)
                o_ref[...] = (r * r).astype(out_dtype)
            elif epilogue == "drelu2":
                o_ref[...] = (acc * (2.0 * jnp.sqrt(refs[2][...].astype(f32)))).astype(out_dtype)
            else:
                o_ref[...] = acc.astype(out_dtype)

        if nk == 1:
            finish(part)
        else:
            @pl.when(kk == 0)
            def _():
                acc_ref[...] = part

            @pl.when(kk > 0)
            def _():
                acc_ref[...] += part

            @pl.when(kk == nk - 1)
            def _():
                finish(acc_ref[...])

    if mode == "tn":
        a_spec = pl.BlockSpec((tk, tm), lambda i, j, kk: (kk, i))
    else:
        a_spec = pl.BlockSpec((tm, tk), lambda i, j, kk: (i, kk))
    if mode == "nt":
        b_spec = pl.BlockSpec((tn, tk), lambda i, j, kk: (j, kk))
    else:
        b_spec = pl.BlockSpec((tk, tn), lambda i, j, kk: (kk, j))
    o_spec = pl.BlockSpec((tm, tn), lambda i, j, kk: (i, j))
    o_shape = jax.ShapeDtypeStruct((m, n), out_dtype)
    return pl.pallas_call(
        body, name=name, grid=(m // tm, n // tn, nk),
        in_specs=[a_spec, b_spec] + ([o_spec] if epilogue == "drelu2" else []),
        out_specs=o_spec, out_shape=o_shape,
        scratch_shapes=[pltpu.VMEM((tm, tn), f32)],
        compiler_params=_cparams(dimension_semantics=("parallel", "parallel", "arbitrary")),
    )(*((a, b, side) if epilogue == "drelu2" else (a, b)))


@jax.custom_vjp
def mlp(h, w1, w1grad, w2, w2grad):
    r = _matmul(h, w1, "nn", bf16, "mlp1_fwd", epilogue="relu2")
    return _matmul(r, w2, "nn", f32, "mlp2_fwd")


def _mlp_fwd(h, w1, w1grad, w2, w2grad):
    r = _matmul(h, w1, "nn", bf16, "mlp1_fwd", epilogue="relu2")
    return _matmul(r, w2, "nn", f32, "mlp2_fwd"), (h, w1, w2, r)


def _mlp_bwd(res, dy):
    h, w1, w2, r = res
    du = _matmul(dy, w2, "nt", bf16, "mlp2_dgrad", epilogue="drelu2", side=r)
    dw2 = _matmul(r, dy, "tn", f32, "mlp2_wgrad")
    dh = _matmul(du, w1, "nt", h.dtype, "mlp1_dgrad")
    dw1 = _matmul(h, du, "tn", f32, "mlp1_wgrad")
    return dh, jnp.zeros_like(w1), dw1, jnp.zeros_like(w2), dw2


mlp.defvjp(_mlp_fwd, _mlp_bwd)


def make_linear(name):
    @jax.custom_vjp
    def linear(a, w, wgrad):
        return _matmul(a, w, "nn", f32, name + "_fwd")

    def fwd(a, w, wgrad):
        return linear(a, w, wgrad), (a, w)

    def bwd(res, dy):
        a, w = res
        da = _matmul(dy, w, "nt", a.dtype, name + "_dgrad")
        dw = _matmul(a, dy, "tn", f32, name + "_wgrad")
        return da, jnp.zeros_like(w), dw

    linear.defvjp(fwd, bwd)
    return linear


def _in_proj_dgrad(dys, ws, g):
    s, d = dys[0].shape[0], ws[0].shape[0]
    tm = _pick(s, (1024, 512, 256, 128))
    tks = [w.shape[1] if w.shape[1] <= 1024 else 512 for w in ws]
    steps = [w.shape[1] // tk for w, tk in zip(ws, tks)]
    starts = [sum(steps[:p]) for p in range(len(ws))]
    total = sum(steps)
    n_p, n_i = len(ws), s // tm
    assert steps[0] == 1

    def body(*refs):
        dy_refs, w_refs, g_ref = refs[:n_p], refs[n_p:2 * n_p], refs[2 * n_p]
        dh_ref, recv_ref, acc_ref, send_sems, recv_sems, local_sem = refs[2 * n_p + 1:]
        i, t = pl.program_id(0), pl.program_id(1)
        start, finish = _scatter_phases(g_ref, recv_ref, send_sems, recv_sems, local_sem)

        @pl.when((i == 0) & (t == 0))
        def _():
            start()

        for p in range(n_p):
            @pl.when((t >= starts[p]) & (t < starts[p] + steps[p]))
            def _(p=p):
                part = lax.dot_general(dy_refs[p][...].astype(bf16), w_refs[p][...], _DIMS["nt"],
                                       preferred_element_type=f32)
                if p == 0:
                    acc_ref[...] = part
                else:
                    acc_ref[...] += part

        @pl.when(t == total - 1)
        def _():
            dh_ref[...] = acc_ref[...].astype(dh_ref.dtype)

        @pl.when((i == n_i - 1) & (t == total - 1))
        def _():
            finish()

    def piece_map(p, rows):
        def index_map(i, t):
            blk = jnp.clip(t - starts[p], 0, steps[p] - 1)
            return (i, blk) if rows else (0, blk)

        return index_map

    hbm = pl.BlockSpec(memory_space=pl.ANY)
    in_specs = [pl.BlockSpec((tm, tks[p]), piece_map(p, True)) for p in range(n_p)]
    in_specs += [pl.BlockSpec((d, tks[p]), piece_map(p, False)) for p in range(n_p)]
    return pl.pallas_call(
        body, name="in_proj_dgrad", grid=(n_i, total), in_specs=in_specs + [hbm],
        out_specs=[pl.BlockSpec((tm, d), lambda i, t: (i, 0)), hbm],
        out_shape=[jax.ShapeDtypeStruct((s, d), bf16), jax.ShapeDtypeStruct(g.shape, g.dtype)],
        scratch_shapes=[pltpu.VMEM((tm, d), f32)] + COMM_SEMS,
        compiler_params=_cparams(dimension_semantics=("arbitrary", "arbitrary")),
    )(*dys, *ws, g)


@jax.custom_vjp
def in_proj(h, ws, recv_like):
    return tuple(_matmul(h, w, "nn", dt, "lin_" + n + "_fwd") for n, w, dt in zip(PROJ_NAMES, ws, PROJ_DTYPES))


def _in_proj_fwd(h, ws, recv_like):
    return in_proj(h, ws, recv_like), (h, ws)


def _in_proj_bwd(res, dys):
    h, ws = res
    dws = {n: _matmul(h, dy, "tn", f32, "lin_" + n + "_wgrad") for n, dy in zip(PROJ_NAMES, dys)}
    dh, recv = _in_proj_dgrad(dys, ws, _pack_in_grads(dws))
    return dh.astype(h.dtype), tuple(jnp.zeros_like(w) for w in ws), recv


in_proj.defvjp(_in_proj_fwd, _in_proj_bwd)


def make_rowwise(name, fn, row_out, sum_out=(), tm_pref=512):
    def specs(rows, gpars, cpars, consts, tm):
        s = [pl.BlockSpec((tm, r.shape[1]), lambda i: (i, 0)) for r in rows]
        s += [pl.BlockSpec(p.shape, lambda i: (0, 0)) for p in gpars]
        s += [pl.BlockSpec(p.shape, lambda i: (0, 0)) for p in cpars]
        for cst in consts:
            nb = cst.shape[0] // tm
            s.append(pl.BlockSpec((tm, cst.shape[1]), lambda i, nb=nb: (i % nb, 0)))
        return s

    def tile_rows(rows, consts):
        r = rows[0].shape[0]
        common = math.gcd(r, *[cst.shape[0] for cst in consts])
        tm = _pick(common, (tm_pref, 512, 256, 128, 64, 32, 16, 8))
        return r, tm

    def forward(rows, gpars, cpars, consts):
        r, tm = tile_rows(rows, consts)
        nr, ng, nc, nk = len(rows), len(gpars), len(cpars), len(consts)

        def body(*refs):
            ins = refs[:nr + ng + nc + nk]
            outs = refs[nr + ng + nc + nk:]
            rv = [t[...].astype(f32) for t in ins[:nr]]
            gv = [t[...].astype(f32) for t in ins[nr:nr + ng]]
            cv = [t[...] for t in ins[nr + ng:nr + ng + nc]]
            kv = [t[...].astype(f32) for t in ins[nr + ng + nc:]]
            ro, so = fn(rv, gv, cv, kv)
            for o_ref, val in zip(outs[:len(row_out)], ro):
                o_ref[...] = val.astype(o_ref.dtype)
            if sum_out:
                @pl.when(pl.program_id(0) == 0)
                def _():
                    for o_ref in outs[len(row_out):]:
                        o_ref[...] = jnp.zeros_like(o_ref)
                for o_ref, val in zip(outs[len(row_out):], so):
                    o_ref[...] += val

        out_specs = [pl.BlockSpec((tm, w), lambda i: (i, 0)) for w, _ in row_out]
        out_specs += [pl.BlockSpec(shp, lambda i: (0, 0)) for shp in sum_out]
        out_shape = [jax.ShapeDtypeStruct((r, w), dt) for w, dt in row_out]
        out_shape += [jax.ShapeDtypeStruct(shp, f32) for shp in sum_out]
        res = pl.pallas_call(
            body, name=name + "_fwd", grid=(r // tm,),
            in_specs=specs(rows, gpars, cpars, consts, tm), out_specs=out_specs, out_shape=out_shape,
            compiler_params=_cparams(dimension_semantics=("arbitrary",)),
        )(*rows, *gpars, *cpars, *consts)
        return tuple(res[:len(row_out)]), tuple(res[len(row_out):])

    def backward(rows, gpars, cpars, consts, d_ro, d_so):
        r, tm = tile_rows(rows, consts)
        nr, ng, nc, nk = len(rows), len(gpars), len(cpars), len(consts)
        n_in = nr + ng + nc + nk + len(row_out) + len(sum_out)

        def body(*refs):
            ins, outs = refs[:n_in], refs[n_in:]
            rv = [t[...].astype(f32) for t in ins[:nr]]
            gv = [t[...].astype(f32) for t in ins[nr:nr + ng]]
            cv = [t[...] for t in ins[nr + ng:nr + ng + nc]]
            kv = [t[...].astype(f32) for t in ins[nr + ng + nc:nr + ng + nc + nk]]
            o = nr + ng + nc + nk
            dro = [t[...].astype(f32) for t in ins[o:o + len(row_out)]]
            dso = [t[...] for t in ins[o + len(row_out):]]
            _, vjp = jax.vjp(lambda a, b: tuple(tuple(t) for t in fn(a, b, cv, kv)), rv, gv)
            drv, dgv = vjp((tuple(dro), tuple(dso)))
            for o_ref, val in zip(outs[:nr], drv):
                o_ref[...] = val.astype(o_ref.dtype)
            if ng:
                @pl.when(pl.program_id(0) == 0)
                def _():
                    for o_ref in outs[nr:]:
                        o_ref[...] = jnp.zeros_like(o_ref)
                for o_ref, val in zip(outs[nr:], dgv):
                    o_ref[...] += val

        in_specs = specs(rows, gpars, cpars, consts, tm)
        in_specs += [pl.BlockSpec((tm, w), lambda i: (i, 0)) for w, _ in row_out]
        in_specs += [pl.BlockSpec(shp, lambda i: (0, 0)) for shp in sum_out]
        out_specs = [pl.BlockSpec((tm, t.shape[1]), lambda i: (i, 0)) for t in rows]
        out_specs += [pl.BlockSpec(p.shape, lambda i: (0, 0)) for p in gpars]
        out_shape = [jax.ShapeDtypeStruct(t.shape, t.dtype) for t in rows]
        out_shape += [jax.ShapeDtypeStruct(p.shape, f32) for p in gpars]
        res = pl.pallas_call(
            body, name=name + "_bwd", grid=(r // tm,),
            in_specs=in_specs, out_specs=out_specs, out_shape=out_shape,
            compiler_params=_cparams(dimension_semantics=("arbitrary",)),
        )(*rows, *gpars, *cpars, *consts, *d_ro, *d_so)
        return tuple(res[:nr]), tuple(res[nr:])

    @jax.custom_vjp
    def op(rows, gpars, cpars, consts):
        return forward(rows, gpars, cpars, consts)

    def op_fwd(rows, gpars, cpars, consts):
        return forward(rows, gpars, cpars, consts), (rows, gpars, cpars, consts)

    def op_bwd(res, cts):
        rows, gpars, cpars, consts = res
        d_ro, d_so = cts
        drows, dg = backward(rows, gpars, cpars, consts, d_ro, d_so)
        dg = tuple(d.astype(p.dtype) for d, p in zip(dg, gpars))
        return (drows, dg, tuple(jnp.zeros_like(p) for p in cpars), tuple(jnp.zeros_like(k) for k in consts))

    op.defvjp(op_fwd, op_bwd)
    return op


def _rms(x):
    return x * lax.rsqrt(jnp.mean(x * x, axis=-1, keepdims=True) + EPS)


def _silu(x):
    return x * jax.nn.sigmoid(x)


def _fn_norm_mod(rows, gp, cp, ks):
    (x,), (nw, sc, sh) = rows, gp
    return ((_rms(x) * nw) * (1.0 + sc) + sh, x), ()


PAIR = 2 * HEAD_DIM


def _exact_dot(a, m):
    hi = a.astype(bf16)
    lo = (a - hi.astype(f32)).astype(bf16)
    return jnp.dot(hi, m, preferred_element_type=f32) + jnp.dot(lo, m, preferred_element_type=f32)


def _make_sel_dot(sign):
    @jax.custom_vjp
    def sel_dot(a, m):
        return _exact_dot(a, m)

    def fwd(a, m):
        return _exact_dot(a, m), m

    def bwd(m, g):
        return sign * _exact_dot(g, m), jnp.zeros_like(m)

    sel_dot.defvjp(fwd, bwd)
    return sel_dot


_head_sum_dot = _make_sel_dot(1.0)
_rope_perm_dot = _make_sel_dot(-1.0)


def _pair_norm_rope(t, w2, gsum, perm, cos2, sin2, out_scale):
    ss = _head_sum_dot(t * t, gsum)
    u = t * lax.rsqrt(ss * (1.0 / HEAD_DIM) + EPS) * w2
    return (u * cos2 + _rope_perm_dot(u, perm) * sin2) * out_scale


def _pair_consts():
    eye = np.eye(2, dtype=np.float32)
    gsum = np.kron(eye, np.ones((HEAD_DIM, HEAD_DIM), np.float32))
    return jnp.asarray(gsum, bf16), jnp.asarray(np.kron(eye, _rope_perm()), bf16)


def make_head_rope(name, nh, out_scale, head_major):
    width = nh * HEAD_DIM
    fn = functools.partial(_pair_norm_rope, out_scale=out_scale)

    def out_spec(tm):
        if head_major:
            return pl.BlockSpec((nh, tm, HEAD_DIM), lambda i: (0, i, 0))
        return pl.BlockSpec((tm, width), lambda i: (i, 0))

    def specs(tm):
        def full(shp):
            return pl.BlockSpec(shp, lambda i: (0, 0))

        return [pl.BlockSpec((tm, width), lambda i: (i, 0)), full((1, PAIR)), full((PAIR, PAIR)), full((PAIR, PAIR)),
                pl.BlockSpec((tm, PAIR), lambda i: (i, 0)), pl.BlockSpec((tm, PAIR), lambda i: (i, 0))]

    def forward(t, w2, gsum, perm, cos2, sin2):
        s = t.shape[0]
        tm = _pick(s, (512, 256, 128))

        def body(t_ref, w_ref, g_ref, p_ref, cos_ref, sin_ref, o_ref):
            for b in range(nh // 2):
                val = fn(t_ref[:, b * PAIR:(b + 1) * PAIR].astype(f32), w_ref[...], g_ref[...], p_ref[...], cos_ref[...],
                         sin_ref[...]).astype(o_ref.dtype)
                if head_major:
                    o_ref[2 * b] = val[:, :HEAD_DIM]
                    o_ref[2 * b + 1] = val[:, HEAD_DIM:]
                else:
                    o_ref[:, b * PAIR:(b + 1) * PAIR] = val

        return pl.pallas_call(
            body, name=name + "_fwd", grid=(s // tm,), in_specs=specs(tm), out_specs=out_spec(tm),
            out_shape=jax.ShapeDtypeStruct((nh, s, HEAD_DIM) if head_major else (s, width), bf16),
            compiler_params=_cparams(dimension_semantics=("arbitrary",)),
        )(t, w2, gsum, perm, cos2, sin2)

    def backward(t, w2, gsum, perm, cos2, sin2, dout):
        s = t.shape[0]
        tm = _pick(s, (512, 256, 128))

        def body(t_ref, w_ref, g_ref, p_ref, cos_ref, sin_ref, do_ref, dt_ref, dw_ref, pair_buf):
            @pl.when(pl.program_id(0) == 0)
            def _():
                dw_ref[...] = jnp.zeros_like(dw_ref)

            g_v, p_v, cos_v, sin_v = g_ref[...], p_ref[...], cos_ref[...], sin_ref[...]
            dw = jnp.zeros((1, PAIR), f32)
            for b in range(nh // 2):
                sl = slice(b * PAIR, (b + 1) * PAIR)
                if head_major:
                    pair_buf[:, :HEAD_DIM] = do_ref[2 * b].astype(f32)
                    pair_buf[:, HEAD_DIM:] = do_ref[2 * b + 1].astype(f32)
                    ct = pair_buf[...]
                else:
                    ct = do_ref[:, sl].astype(f32)
                _, vjp = jax.vjp(lambda a, c: fn(a, c, g_v, p_v, cos_v, sin_v), t_ref[:, sl].astype(f32), w_ref[...])
                dtb, dwb = vjp(ct)
                dt_ref[:, sl] = dtb.astype(dt_ref.dtype)
                dw = dw + dwb
            dw_ref[...] += dw

        return pl.pallas_call(
            body, name=name + "_bwd", grid=(s // tm,), in_specs=specs(tm) + [out_spec(tm)],
            out_specs=[pl.BlockSpec((tm, width), lambda i: (i, 0)), pl.BlockSpec((1, PAIR), lambda i: (0, 0))],
            out_shape=[jax.ShapeDtypeStruct((s, width), t.dtype), jax.ShapeDtypeStruct((1, PAIR), f32)],
            scratch_shapes=[pltpu.VMEM((tm, PAIR), f32)],
            compiler_params=_cparams(dimension_semantics=("arbitrary",)),
        )(t, w2, gsum, perm, cos2, sin2, dout)

    @jax.custom_vjp
    def op(t, w2, gsum, perm, cos2, sin2):
        return forward(t, w2, gsum, perm, cos2, sin2)

    def op_fwd(*args):
        return forward(*args), args

    def op_bwd(res, dout):
        dt, dw = backward(*res, dout)
        return (dt, dw) + tuple(jnp.zeros_like(r) for r in res[2:])

    op.defvjp(op_fwd, op_bwd)

    def apply(t, w, cos2, sin2):
        gsum, perm = _pair_consts()
        return op(t, jnp.concatenate([w, w], axis=-1), gsum, perm, cos2, sin2)

    return apply


def _fn_softplus(rows, gp, cp, ks):
    (x,), (b,) = rows, gp
    v = x + b
    return (jnp.maximum(v, 0.0) + jnp.log(1.0 + jnp.exp(-jnp.abs(v))),), ()


def _fn_ssd_gate(rows, gp, cp, ks):
    (y, z), (nw,) = rows, gp
    return (_rms(y * _silu(z)) * nw,), ()


def _fn_merge(rows, gp, cp, ks):
    ao, so, ga, gs = rows
    return (jax.nn.sigmoid(ga) * ao + jax.nn.sigmoid(gs) * so,), ()


def _fn_res_norm(rows, gp, cp, ks):
    (x, mo), (g1, nw, sc, sh) = rows, gp
    x1 = x + g1 * mo
    return (x1, (_rms(x1) * nw) * (1.0 + sc) + sh), ()


def _fn_loss(rows, gp, cp, ks):
    (x1, ff), (g2,), (tgt,) = rows, gp, ks
    err = x1 + g2 * ff - tgt
    return (), (0.5 * jnp.sum(jnp.sum(err * err, axis=-1, keepdims=True), axis=0, keepdims=True) / D_MODEL,)


HALO = 8
HALO_BWD = 16


def _conv_tiles(s, c):
    return _pick(s, (512, 256, 128)), _pick(c, (512, 256, 128))


def _halo_specs(tm, tc, s, halo=HALO):
    nb = tm // halo
    last = s // halo - 1
    cur = pl.BlockSpec((tm, tc), lambda j, i: (i, j))
    prev = pl.BlockSpec((halo, tc), lambda j, i: (jnp.maximum(i * nb - 1, 0), j))
    nxt = pl.BlockSpec((halo, tc), lambda j, i: (jnp.minimum((i + 1) * nb, last), j))
    return cur, prev, nxt


def _fill_halo(buf, cur, prev, nxt, tm, i, n_i, halo=HALO):
    buf[halo:halo + tm, :] = cur[...]
    buf[0:halo, :] = jnp.where(i > 0, prev[...], 0.0)
    buf[halo + tm:, :] = jnp.where(i < n_i - 1, nxt[...], 0.0)


def _conv_fwd(x, w, b, shard):
    s, c = x.shape
    tm, tc = _conv_tiles(s, c)
    n_i, n_j = s // tm, c // tc

    def body(cur, prev, nxt, w_ref, b_ref, shard_ref, o_ref, gath_ref, buf, send_sems, recv_sems, local_sem):
        j, i = pl.program_id(0), pl.program_id(1)
        start, finish = _gather_phases(shard_ref, gath_ref, send_sems, recv_sems, local_sem)

        @pl.when((j == 0) & (i == 0))
        def _():
            start()

        _fill_halo(buf, cur, prev, nxt, tm, i, n_i)
        pre = jnp.zeros((tm, tc), f32) + b_ref[...]
        for k in range(D_CONV):
            pre = pre + buf[HALO - 2 + k:HALO - 2 + k + tm, :] * w_ref[k:k + 1, :]
        o_ref[...] = _silu(pre)

        @pl.when((j == n_j - 1) & (i == n_i - 1))
        def _():
            finish()

    cur, prev, nxt = _halo_specs(tm, tc, s)
    hbm = pl.BlockSpec(memory_space=pl.ANY)
    return pl.pallas_call(
        body, name="conv_silu_fwd", grid=(n_j, n_i),
        in_specs=[cur, prev, nxt, pl.BlockSpec((D_CONV, tc), lambda j, i: (0, j)),
                  pl.BlockSpec((1, tc), lambda j, i: (0, j)), hbm],
        out_specs=[pl.BlockSpec((tm, tc), lambda j, i: (i, j)), hbm],
        out_shape=[jax.ShapeDtypeStruct((s, c), f32), jax.ShapeDtypeStruct((N_DEV,) + shard.shape, shard.dtype)],
        scratch_shapes=[pltpu.VMEM((tm + 2 * HALO, tc), f32)] + COMM_SEMS,
        compiler_params=_cparams(dimension_semantics=("arbitrary", "arbitrary")),
    )(x, x, x, w, b, shard)


def _conv_bwd(x, w, b, dy, g):
    s, c = x.shape
    tm, tc = _conv_tiles(s, c)
    n_i, n_j = s // tm, c // tc
    ext = tm + 16

    def body(cur, prev, nxt, dcur, dprev, dnxt, w_ref, b_ref, g_ref, dx_ref, dw_ref, db_ref, recv_ref,
             xbuf, dbuf, pbuf, send_sems, recv_sems, local_sem):
        j, i = pl.program_id(0), pl.program_id(1)
        start, finish = _scatter_phases(g_ref, recv_ref, send_sems, recv_sems, local_sem)

        @pl.when((j == 0) & (i == 0))
        def _():
            start()

        _fill_halo(xbuf, cur, prev, nxt, tm, i, n_i, HALO_BWD)
        _fill_halo(dbuf, dcur, dprev, dnxt, tm, i, n_i, HALO_BWD)
        xs = [xbuf[6 + k:6 + k + ext, :] for k in range(D_CONV)]
        pre = jnp.zeros((ext, tc), f32) + b_ref[...]
        for k in range(D_CONV):
            pre = pre + xs[k] * w_ref[k:k + 1, :]
        sg = jax.nn.sigmoid(pre)
        pbuf[...] = dbuf[8:8 + ext, :] * (sg * (1.0 + pre * (1.0 - sg)))
        dx = jnp.zeros((tm, tc), f32)
        for k in range(D_CONV):
            dx = dx + pbuf[10 - k:10 - k + tm, :] * w_ref[k:k + 1, :]
        dx_ref[...] = dx

        @pl.when(i == 0)
        def _():
            dw_ref[...] = jnp.zeros_like(dw_ref)
            db_ref[...] = jnp.zeros_like(db_ref)

        dpre = pbuf[8:8 + tm, :]
        db_ref[...] += jnp.sum(dpre, axis=0, keepdims=True)
        for k in range(D_CONV):
            dw_ref[k:k + 1, :] += jnp.sum(dpre * xs[k][8:8 + tm, :], axis=0, keepdims=True)

        @pl.when((j == n_j - 1) & (i == n_i - 1))
        def _():
            finish()

    cur, prev, nxt = _halo_specs(tm, tc, s, HALO_BWD)
    hbm = pl.BlockSpec(memory_space=pl.ANY)
    return pl.pallas_call(
        body, name="conv_silu_bwd", grid=(n_j, n_i),
        in_specs=[cur, prev, nxt, cur, prev, nxt, pl.BlockSpec((D_CONV, tc), lambda j, i: (0, j)),
                  pl.BlockSpec((1, tc), lambda j, i: (0, j)), hbm],
        out_specs=[pl.BlockSpec((tm, tc), lambda j, i: (i, j)), pl.BlockSpec((D_CONV, tc), lambda j, i: (0, j)),
                   pl.BlockSpec((1, tc), lambda j, i: (0, j)), hbm],
        out_shape=[jax.ShapeDtypeStruct((s, c), f32), jax.ShapeDtypeStruct((D_CONV, c), f32),
                   jax.ShapeDtypeStruct((1, c), f32), jax.ShapeDtypeStruct(g.shape, g.dtype)],
        scratch_shapes=[pltpu.VMEM((tm + 2 * HALO_BWD, tc), f32), pltpu.VMEM((tm + 2 * HALO_BWD, tc), f32),
                        pltpu.VMEM((ext, tc), f32)] + COMM_SEMS,
        compiler_params=_cparams(dimension_semantics=("arbitrary", "arbitrary")),
    )(x, x, x, dy, dy, dy, w, b, g)


@jax.custom_vjp
def conv_silu_comm(x, w, b, shard, recv_like):
    act, gathered = _conv_fwd(x, w, b, shard)
    return (act, gathered) + tuple(jnp.zeros(shp, f32) for shp in LATE_SHAPES)


def _conv_silu_comm_fwd(x, w, b, shard, recv_like):
    return conv_silu_comm(x, w, b, shard, recv_like), (x, w, b, shard)


def _conv_silu_comm_bwd(res, cts):
    x, w, b, shard = res
    dx, dw, db, recv = _conv_bwd(x, w, b, cts[0], _pack_late_grads(dict(zip(LATE, cts[2:]))))
    return dx, dw, db, jnp.zeros_like(shard), recv


conv_silu_comm.defvjp(_conv_silu_comm_fwd, _conv_silu_comm_bwd)


ATT_SCALE = HEAD_DIM ** -0.5
Q_SCALE = ATT_SCALE * math.log2(math.e)
LN2 = math.log(2.0)
REP = N_Q_HEADS // N_KV_HEADS


HP = 2
assert REP % HP == 0


def _attn_fwd(q, k, v):
    s, dh = q.shape[0], HEAD_DIM
    hq = q.shape[1] // dh
    tq = _pick(s, (256, 128))

    v1 = jnp.concatenate([v, jnp.ones(v.shape[:2] + (1,), v.dtype), jnp.zeros(v.shape[:2] + (dh - 1,), v.dtype)],
                         axis=-1)

    def body(q_ref, k_ref, v_ref, o_ref, p_ref, linv_ref):
        for j in range(HP):
            sl = slice(j * dh, (j + 1) * dh)
            sc = lax.dot_general(q_ref[:, sl], k_ref[0], _DIMS["nt"], preferred_element_type=f32)
            m = jnp.max(sc, axis=-1, keepdims=True)
            p = jnp.exp2(sc - m).astype(bf16)
            p_ref[j] = p
            o1 = jnp.dot(p, v_ref[0], preferred_element_type=f32)
            linv = 1.0 / o1[:, dh:dh + 1]
            o_ref[:, sl] = (o1[:, :dh] * linv).astype(o_ref.dtype)
            linv_ref[j] = linv

    return pl.pallas_call(
        body, name="attn_fwd", grid=(hq // HP, s // tq),
        in_specs=[pl.BlockSpec((tq, HP * dh), lambda h, i: (i, h)),
                  pl.BlockSpec((1, s, dh), lambda h, i: (h * HP // REP, 0, 0)),
                  pl.BlockSpec((1, s, 2 * dh), lambda h, i: (h * HP // REP, 0, 0))],
        out_specs=[pl.BlockSpec((tq, HP * dh), lambda h, i: (i, h)),
                   pl.BlockSpec((HP, tq, s), lambda h, i: (h, i, 0)),
                   pl.BlockSpec((HP, tq, 1), lambda h, i: (h, i, 0))],
        out_shape=[jax.ShapeDtypeStruct((s, hq * dh), bf16), jax.ShapeDtypeStruct((hq, s, s), bf16),
                   jax.ShapeDtypeStruct((hq, s, 1), f32)],
        compiler_params=_cparams(dimension_semantics=("parallel", "arbitrary")),
    )(q, k, v1)


def _attn_bwd(p, do, o, q, k, v, linv):
    hq, s, _ = p.shape
    dh = HEAD_DIM
    tq = _pick(s, (256, 128))

    def body(p_ref, do_ref, o_ref, q_ref, k_ref, v_ref, linv_ref, dq_ref, dkt_ref, dvt_ref):
        @pl.when(pl.program_id(1) == 0)
        def _():
            dkt_ref[...] = jnp.zeros_like(dkt_ref)
            dvt_ref[...] = jnp.zeros_like(dvt_ref)

        for j in range(HP):
            sl = slice(j * dh, (j + 1) * dh)
            pp, doh, li = p_ref[j], do_ref[:, sl], linv_ref[j]
            do32 = doh.astype(f32)
            d = jnp.sum(do32 * o_ref[:, sl].astype(f32), axis=-1, keepdims=True)
            dp = lax.dot_general(doh, v_ref[0], _DIMS["nt"], preferred_element_type=f32)
            ds = (pp.astype(f32) * ((dp - d) * li)).astype(bf16)
            dq_ref[:, sl] = (jnp.dot(ds, k_ref[0], preferred_element_type=f32) * LN2).astype(dq_ref.dtype)
            dvt_ref[j] += lax.dot_general((do32 * li).astype(bf16), pp, _DIMS["tn"], preferred_element_type=f32)
            dkt_ref[j] += lax.dot_general(q_ref[:, sl], ds, _DIMS["tn"], preferred_element_type=f32)

    def row():
        return pl.BlockSpec((tq, HP * dh), lambda h, i: (i, h))

    return pl.pallas_call(
        body, name="attn_bwd", grid=(hq // HP, s // tq),
        in_specs=[pl.BlockSpec((HP, tq, s), lambda h, i: (h, i, 0)), row(), row(), row(),
                  pl.BlockSpec((1, s, dh), lambda h, i: (h * HP // REP, 0, 0)),
                  pl.BlockSpec((1, s, dh), lambda h, i: (h * HP // REP, 0, 0)),
                  pl.BlockSpec((HP, tq, 1), lambda h, i: (h, i, 0))],
        out_specs=[row(), pl.BlockSpec((HP, dh, s), lambda h, i: (h, 0, 0)),
                   pl.BlockSpec((HP, dh, s), lambda h, i: (h, 0, 0))],
        out_shape=[jax.ShapeDtypeStruct((s, hq * dh), q.dtype), jax.ShapeDtypeStruct((hq, dh, s), f32),
                   jax.ShapeDtypeStruct((hq, dh, s), f32)],
        compiler_params=_cparams(dimension_semantics=("parallel", "arbitrary")),
    )(p, do, o, q, k, v, linv)


@jax.custom_vjp
def attention(q, k, v):
    return _attn_fwd(q, k, v)[0]


def _attention_fwd(q, k, v):
    o, p, linv = _attn_fwd(q, k, v)
    return o, (q, k, v, o, p, linv)


def _attention_bwd(res, do):
    q, k, v, o, p, linv = res
    s = q.shape[0]
    dq, dkt, dvt = _attn_bwd(p, do.astype(bf16), o, q, k, v, linv)

    def per_kv_head(t):
        return jnp.swapaxes(t.reshape(N_KV_HEADS, REP, HEAD_DIM, s).sum(axis=1), 1, 2)

    return dq, (per_kv_head(dkt) * LN2).astype(k.dtype), per_kv_head(dvt).astype(v.dtype)


attention.defvjp(_attention_fwd, _attention_bwd)


HPG = N_SSD_HEADS // N_SSD_GROUPS
GW = HPG * SSD_HEAD_DIM
NEG = -1e30
SPLIT_ROWS = 32


def _ssd_consts():
    k = np.arange(SPLIT_ROWS)[:, None]
    live = k < 3 * HPG
    sel_chunk = ((k % HPG) == (np.arange(HPG * CHUNK)[None, :] // CHUNK)) & live
    sel_head = ((k % HPG) == (np.arange(GW)[None, :] // SSD_HEAD_DIM)) & live
    return jnp.asarray(sel_chunk, bf16), jnp.asarray(sel_head, bf16)


def _split3(x):
    hi = x.astype(bf16).astype(f32)
    r1 = x - hi
    mid = r1.astype(bf16).astype(f32)
    lo = (r1 - mid).astype(bf16).astype(f32)
    return jnp.concatenate([hi, mid, lo, jnp.zeros_like(hi)], axis=0).astype(bf16)


def _tn(a, b):
    return lax.dot_general(a, b, _DIMS["tn"], preferred_element_type=f32)


def _nt(a, b):
    return lax.dot_general(a, b, _DIMS["nt"], preferred_element_type=f32)


def _nn(a, b):
    return jnp.dot(a, b, preferred_element_type=f32)


def _head_sum(sel8, x):
    hi = x.astype(bf16)
    lo = (x - hi.astype(f32)).astype(bf16)
    return _nt(sel8, hi) + _nt(sel8, lo)


def _ssd_masks(reverse):
    r = lax.broadcasted_iota(jnp.int32, (CHUNK, CHUNK), 0)
    c = lax.broadcasted_iota(jnp.int32, (CHUNK, CHUNK), 1)
    lower, upper = r >= c, r <= c
    return (upper, lower) if reverse else (lower, upper)


def _ssd_in_specs(cidx):
    return [pl.BlockSpec((CHUNK, D_INNER), lambda c: (cidx(c), 0)),
            pl.BlockSpec((CHUNK, GN), lambda c: (cidx(c), D_INNER // GN)),
            pl.BlockSpec((CHUNK, GN), lambda c: (cidx(c), D_INNER // GN + 1)),
            pl.BlockSpec((N_SSD_HEADS, CHUNK), lambda c: (0, cidx(c))),
            pl.BlockSpec((N_SSD_HEADS, 1), lambda c: (0, 0)),
            pl.BlockSpec((SPLIT_ROWS, HPG * CHUNK), lambda c: (0, 0)),
            pl.BlockSpec((SPLIT_ROWS, GW), lambda c: (0, 0))]


def _ssd_chunk_common(dtt_ref, a_ref, et_ref, mask_t):
    dtt = dtt_ref[...]
    et = jnp.dot(dtt * a_ref[...], mask_t.astype(f32), precision=HIGHEST, preferred_element_type=f32)
    et_ref[...] = et
    return dtt, et


def _ssd_group_common(g, dtt, et, selc_ref, selh_ref, xs_ref, b_ref, c_ref, last):
    gr = slice(g * HPG, (g + 1) * HPG)
    e3 = _split3(et[gr])
    col = _tn(e3, selc_ref[...])
    eb = _tn(e3, selh_ref[...])
    dtb = _tn(_split3(dtt[gr]), selh_ref[...])
    tbc = eb[last:last + 1, :]
    xs = xs_ref[:, g * GW:(g + 1) * GW]
    bg = b_ref[:, g * D_STATE:(g + 1) * D_STATE].astype(bf16)
    cg = c_ref[:, g * D_STATE:(g + 1) * D_STATE].astype(bf16)
    return col, eb, dtb, tbc, xs, bg, cg


def _ssd_fwd(xbc, dtt, a_col, reverse, y_prev=None, dexp=None):
    s = xbc.shape[0]
    nc = s // CHUNK
    cidx = (lambda c: nc - 1 - c) if reverse else (lambda c: c)
    last = 0 if reverse else CHUNK - 1
    selc, selh = _ssd_consts()
    final = y_prev is not None
    n_in = 9 if final else 7

    def body(*refs):
        xs_ref, b_ref, c_ref, dtt_ref, a_ref, selc_ref, selh_ref = refs[:7]
        y_ref, st_ref, ht_ref, et_ref, yg_ref = refs[n_in:]

        @pl.when(pl.program_id(0) == 0)
        def _():
            ht_ref[...] = jnp.zeros_like(ht_ref)

        mask, mask_t = _ssd_masks(reverse)
        dtt_v, et = _ssd_chunk_common(dtt_ref, a_ref, et_ref, mask_t)
        for g in range(N_SSD_GROUPS):
            col, eb, dtb, tbc, xs, bg, cg = _ssd_group_common(g, dtt_v, et, selc_ref, selh_ref, xs_ref, b_ref, c_ref,
                                                              last)
            xd = xs * dtb
            cb = _nt(cg, bg)
            ht = ht_ref[g]
            st_ref[0, g] = ht
            yoff = _nn(cg, ht.astype(bf16)) * jnp.exp(eb)
            for j in range(HPG):
                h = g * HPG + j
                hs = slice(j * SSD_HEAD_DIM, (j + 1) * SSD_HEAD_DIM)
                lam = jnp.exp(jnp.where(mask, col[:, j * CHUNK:(j + 1) * CHUNK] - et_ref[h:h + 1, :], NEG))
                yg_ref[:, hs] = _nn((cb * lam).astype(bf16), xd[:, hs].astype(bf16))
            cols = slice(g * GW, (g + 1) * GW)
            yg = yg_ref[...] + yoff
            if final:
                yg = yg + refs[7][:, cols] + xs * refs[8][:, cols]
            y_ref[:, cols] = yg.astype(y_ref.dtype)
            ht_ref[g] = jnp.exp(tbc) * ht + _tn(bg, (xd * jnp.exp(tbc - eb)).astype(bf16))

    y_spec = pl.BlockSpec((CHUNK, D_INNER), lambda c: (cidx(c), 0))
    extra_specs = [y_spec, pl.BlockSpec((1, D_INNER), lambda c: (0, 0))] if final else []
    return pl.pallas_call(
        body, name="ssd_fwd_rev" if reverse else "ssd_fwd", grid=(nc,),
        in_specs=_ssd_in_specs(cidx) + extra_specs,
        out_specs=[y_spec, pl.BlockSpec((1, N_SSD_GROUPS, D_STATE, GW), lambda c: (cidx(c), 0, 0, 0))],
        out_shape=[jax.ShapeDtypeStruct((s, D_INNER), bf16 if final else f32),
                   jax.ShapeDtypeStruct((nc, N_SSD_GROUPS, D_STATE, GW), f32)],
        scratch_shapes=[pltpu.VMEM((N_SSD_GROUPS, D_STATE, GW), f32), pltpu.VMEM((N_SSD_HEADS, CHUNK), f32),
                        pltpu.VMEM((CHUNK, GW), f32)],
        compiler_params=_cparams(dimension_semantics=("arbitrary",)),
    )(xbc, xbc, xbc, dtt, a_col, selc, selh, *((y_prev, dexp) if final else ()))


def _ssd_bwd(xbc, dtt, a_col, states, dy, reverse, dxbc_prev=None, dexp=None):
    s = xbc.shape[0]
    nc = s // CHUNK
    cidx = (lambda c: c) if reverse else (lambda c: nc - 1 - c)
    last = 0 if reverse else CHUNK - 1
    selc, selh = _ssd_consts()
    final = dxbc_prev is not None
    n_in = 11 if final else 9
    n_out = 4 if final else 3

    def body(*refs):
        xs_ref, b_ref, c_ref, dtt_ref, a_ref, selc_ref, selh_ref, st_ref, dy_ref = refs[:9]
        dxbc_ref, ddtt_ref, da_ref = refs[n_in:n_in + 3]
        dh_ref, et_ref, det_ref, det2_ref, ddt_ref, q_ref = refs[n_in + n_out:]
        if final:
            prev_ref, dexp_ref, ddexp_ref = refs[9], refs[10], refs[n_in + 3]

        @pl.when(pl.program_id(0) == 0)
        def _():
            dh_ref[...] = jnp.zeros_like(dh_ref)
            da_ref[...] = jnp.zeros_like(da_ref)
            if final:
                ddexp_ref[...] = jnp.zeros_like(ddexp_ref)

        mask, mask_t = _ssd_masks(reverse)
        dtt_v, et = _ssd_chunk_common(dtt_ref, a_ref, et_ref, mask_t)
        sel8 = selh_ref[0:HPG, :]
        is_last = lax.broadcasted_iota(jnp.int32, (CHUNK, GW), 0) == last
        for g in range(N_SSD_GROUPS):
            col, eb, dtb, tbc, xs, bg, cg = _ssd_group_common(g, dtt_v, et, selc_ref, selh_ref, xs_ref, b_ref, c_ref,
                                                              last)
            xd = xs * dtb
            cb = _nt(cg, bg)
            cbt = _nt(bg, cg)
            exp_t = jnp.exp(tbc)
            dfac = jnp.exp(tbc - eb)
            ht = st_ref[0, g]
            dhn = dh_ref[g]
            ht16, dhn16 = ht.astype(bf16), dhn.astype(bf16)
            dy = dy_ref[:, g * GW:(g + 1) * GW].astype(f32)
            dye = dy * jnp.exp(eb)
            dye16 = dye.astype(bf16)
            dc = _nt(dye16, ht16)
            dh_ref[g] = exp_t * dhn + _tn(cg, dye16)
            deb = dye * _nn(cg, ht16)
            xdd = xd * dfac
            dxdd = _nn(bg, dhn16)
            db = _nt(xdd.astype(bf16), dhn16)
            dxd_state = dxdd * dfac
            ddf = dxdd * xdd
            dtbc = jnp.sum(ddf, axis=0, keepdims=True) + exp_t * jnp.sum(dhn * ht, axis=0, keepdims=True)
            deb = deb - ddf + jnp.where(is_last, dtbc, 0.0)
            dcb = jnp.zeros((CHUNK, CHUNK), f32)
            dcbt = jnp.zeros((CHUNK, CHUNK), f32)
            for j in range(HPG):
                h = g * HPG + j
                hs = slice(j * SSD_HEAD_DIM, (j + 1) * SSD_HEAD_DIM)
                colj = col[:, j * CHUNK:(j + 1) * CHUNK]
                row = et_ref[h:h + 1, :]
                lam = jnp.exp(jnp.where(mask, colj - row, NEG))
                lam_t = lam.T
                xdj, dyj = xd[:, hs].astype(bf16), dy[:, hs].astype(bf16)
                t1 = _nt(dyj, xdj) * lam
                t2 = _nt(xdj, dyj) * lam_t
                dcb, dcbt = dcb + t1, dcbt + t2
                det_ref[h:h + 1, :] = -jnp.sum(t1 * cb - t2 * cbt, axis=0, keepdims=True)
                q_ref[:, hs] = _nn((cbt * lam_t).astype(bf16), dyj)
            x_cols = slice(g * GW, (g + 1) * GW)
            dxd = q_ref[...] + dxd_state
            dxs = dxd * dtb
            if final:
                dxs = dxs + prev_ref[:, x_cols] + dy * dexp_ref[:, x_cols]
            dxbc_ref[:, x_cols] = dxs
            b_cols = slice(D_INNER + g * D_STATE, D_INNER + (g + 1) * D_STATE)
            c_cols = slice(D_INNER + GN + g * D_STATE, D_INNER + GN + (g + 1) * D_STATE)
            db = db + _nn(dcbt.astype(bf16), cg)
            dc = dc + _nn(dcb.astype(bf16), bg)
            if final:
                db, dc = db + prev_ref[:, b_cols], dc + prev_ref[:, c_cols]
                ddexp_ref[:, g * GW:(g + 1) * GW] += jnp.sum(dy * xs, axis=0, keepdims=True)
            dxbc_ref[:, b_cols] = db
            dxbc_ref[:, c_cols] = dc
            det2_ref[g * HPG:(g + 1) * HPG, :] = _head_sum(sel8, deb)
            ddt_ref[g * HPG:(g + 1) * HPG, :] = _head_sum(sel8, dxd * xs)
        dat = jnp.dot(det_ref[...] + det2_ref[...], mask.astype(f32), precision=HIGHEST, preferred_element_type=f32)
        ddtt_ref[...] = ddt_ref[...] + dat * a_ref[...]
        da_ref[...] += jnp.sum(dat * dtt_v, axis=1, keepdims=True)

    in_specs = _ssd_in_specs(cidx) + [
        pl.BlockSpec((1, N_SSD_GROUPS, D_STATE, GW), lambda c: (cidx(c), 0, 0, 0)),
        pl.BlockSpec((CHUNK, D_INNER), lambda c: (cidx(c), 0))]
    hl = pltpu.VMEM((N_SSD_HEADS, CHUNK), f32)
    dxbc_spec = pl.BlockSpec((CHUNK, CONV_DIM), lambda c: (cidx(c), 0))
    dexp_spec = pl.BlockSpec((1, D_INNER), lambda c: (0, 0))
    return pl.pallas_call(
        body, name="ssd_bwd_rev" if reverse else "ssd_bwd", grid=(nc,),
        in_specs=in_specs + ([dxbc_spec, dexp_spec] if final else []),
        out_specs=[dxbc_spec, pl.BlockSpec((N_SSD_HEADS, CHUNK), lambda c: (0, cidx(c))),
                   pl.BlockSpec((N_SSD_HEADS, 1), lambda c: (0, 0))] + ([dexp_spec] if final else []),
        out_shape=[jax.ShapeDtypeStruct((s, CONV_DIM), f32), jax.ShapeDtypeStruct((N_SSD_HEADS, s), f32),
                   jax.ShapeDtypeStruct((N_SSD_HEADS, 1), f32)]
        + ([jax.ShapeDtypeStruct((1, D_INNER), f32)] if final else []),
        scratch_shapes=[pltpu.VMEM((N_SSD_GROUPS, D_STATE, GW), f32), hl, hl, hl, hl, pltpu.VMEM((CHUNK, GW), f32)],
        compiler_params=_cparams(dimension_semantics=("arbitrary",)),
    )(xbc, xbc, xbc, dtt, a_col, selc, selh, states, dy, *((dxbc_prev, dexp) if final else ()))


@jax.custom_vjp
def ssd_bidir(xbc, dtt, a_col, dexp):
    y_f, _ = _ssd_fwd(xbc, dtt[:N_SSD_HEADS], a_col[:N_SSD_HEADS], False)
    return _ssd_fwd(xbc, dtt[N_SSD_HEADS:], a_col[N_SSD_HEADS:], True, y_prev=y_f, dexp=dexp)[0]


def _ssd_bidir_fwd(xbc, dtt, a_col, dexp):
    y_f, st_f = _ssd_fwd(xbc, dtt[:N_SSD_HEADS], a_col[:N_SSD_HEADS], False)
    y, st_b = _ssd_fwd(xbc, dtt[N_SSD_HEADS:], a_col[N_SSD_HEADS:], True, y_prev=y_f, dexp=dexp)
    return y, (xbc, dtt, a_col, dexp, st_f, st_b)


def _ssd_bidir_bwd(res, dy):
    xbc, dtt, a_col, dexp, st_f, st_b = res
    dxbc_f, ddtt_f, da_f = _ssd_bwd(xbc, dtt[:N_SSD_HEADS], a_col[:N_SSD_HEADS], st_f, dy, False)
    dxbc, ddtt_b, da_b, ddexp = _ssd_bwd(xbc, dtt[N_SSD_HEADS:], a_col[N_SSD_HEADS:], st_b, dy, True,
                                         dxbc_prev=dxbc_f, dexp=dexp)
    return dxbc, jnp.concatenate([ddtt_f, ddtt_b], axis=0), jnp.concatenate([da_f, da_b], axis=0), ddexp


ssd_bidir.defvjp(_ssd_bidir_fwd, _ssd_bidir_bwd)


def _rope_tables(s):
    rows = s // GRID_W
    pos_row = np.repeat(np.arange(rows), GRID_W).astype(np.float32)
    pos_col = np.tile(np.arange(GRID_W), rows).astype(np.float32)
    axis_dim = HEAD_DIM // 2
    inv_freq = np.float32(ROPE_THETA) ** (-np.arange(0, axis_dim, 2, dtype=np.float32) / np.float32(axis_dim))
    ang_r = pos_row[:, None] * inv_freq[None, :].astype(np.float32)
    ang_c = pos_col[:, None] * inv_freq[None, :].astype(np.float32)
    cos = np.concatenate([np.cos(ang_r), np.cos(ang_r), np.cos(ang_c), np.cos(ang_c)] * 2, axis=-1)
    sin = np.concatenate([np.sin(ang_r), np.sin(ang_r), np.sin(ang_c), np.sin(ang_c)] * 2, axis=-1)
    return jnp.asarray(cos, f32), jnp.asarray(sin, f32)


def _rope_perm():
    p = np.zeros((HEAD_DIM, HEAD_DIM), np.float32)
    for j in range(HEAD_DIM):
        if (j % 32) < 16:
            p[j + 16, j] = -1.0
        else:
            p[j - 16, j] = 1.0
    return p


def local_loss(x, mod, small, recv_in_like, recv_late_like, wfull, late_shard, target):
    s = x.shape[0]
    lin = {n: make_linear("lin_" + n) for n in LATE if not n.startswith("mlp")}
    wfull, wgrads = dict(wfull), {}
    shift1, scale1, gate1, shift2, scale2, gate2 = [mod[i] for i in range(6)]

    norm_mod = make_rowwise("norm_mod", _fn_norm_mod, [(D_MODEL, bf16), (D_MODEL, f32)])
    (h, x_res), _ = norm_mod((x,), (small["norm1_w"], scale1, shift1), (), ())

    proj = dict(zip(PROJ_NAMES, in_proj(h, tuple(wfull[n] for n in PROJ_NAMES), recv_in_like)))

    cos, sin = _rope_tables(s)

    def heads(t, nh):
        return t.reshape(s, nh, HEAD_DIM).transpose(1, 0, 2)

    qr = make_head_rope("q_norm_rope", N_Q_HEADS, Q_SCALE, False)(proj["q"], small["q_norm_w"], cos, sin)
    kr = make_head_rope("k_norm_rope", N_KV_HEADS, 1.0, True)(proj["k"], small["k_norm_w"], cos, sin)
    vh = heads(proj["v"], N_KV_HEADS).astype(bf16)
    att = attention(qr, kr, vh)

    xbc, gathered, *carriers = conv_silu_comm(proj["xbc"], small["conv_w"], small["conv_b"], late_shard,
                                              recv_late_like)
    wfull.update(_split_late(gathered))
    wgrads.update(zip(LATE, carriers))
    ao = lin["attn_out"](att, wfull["attn_out"], wgrads["attn_out"])
    softplus = make_rowwise("dt_softplus", _fn_softplus, [(2 * N_SSD_HEADS, f32)])
    (dt,), _ = softplus((proj["dt"][:, :2 * N_SSD_HEADS],), (small["dt_bias"].reshape(1, 2 * N_SSD_HEADS),), (), ())
    a_neg = -jnp.exp(small["A_log"])
    dexp = jnp.repeat(small["ssd_D"].reshape(N_SSD_HEADS), SSD_HEAD_DIM).reshape(1, D_INNER)
    y = ssd_bidir(xbc, dt.T, a_neg.reshape(2 * N_SSD_HEADS, 1), dexp)
    ssd_gate = make_rowwise("ssd_gate", _fn_ssd_gate, [(D_INNER, bf16)], tm_pref=256)
    (ssd_out,), _ = ssd_gate((y, proj["z"]), (small["ssd_norm_w"],), (), ())
    so = lin["ssd_out"](ssd_out, wfull["ssd_out"], wgrads["ssd_out"])

    merge = make_rowwise("merge", _fn_merge, [(D_MODEL, bf16)])
    (merged,), _ = merge((ao, so, proj["ga"], proj["gs"]), (), (), ())
    mo = lin["o"](merged, wfull["o"], wgrads["o"])

    res_norm = make_rowwise("res_norm", _fn_res_norm, [(D_MODEL, f32), (D_MODEL, bf16)])
    (x1, h2), _ = res_norm((x_res, mo), (gate1, small["norm2_w"], scale2, shift2), (), ())
    ff = mlp(h2, wfull["mlp1"], wgrads["mlp1"], wfull["mlp2"], wgrads["mlp2"])
    loss_op = make_rowwise("loss", _fn_loss, [], [(1, 1)])
    _, (loss,) = loss_op((x1, ff), (gate2,), (), (target,))
    return loss[0, 0]


_BC1 = 1.0 - ADAM_B1 ** ADAM_STEP
_BC2 = 1.0 - ADAM_B2 ** ADAM_STEP


def _adamw(w, g, m, v):
    m = ADAM_B1 * m + (1.0 - ADAM_B1) * g
    v = ADAM_B2 * v + (1.0 - ADAM_B2) * (g * g)
    delta = -ADAM_LR * ((m / _BC1) / (jnp.sqrt(v / _BC2) + ADAM_EPS) + ADAM_WD * w)
    return delta, m, v


def _ada_fwd(c_all, w, b):
    n = w.shape[1]

    def body(c_ref, w_ref, b_ref, o_ref):
        o_ref[...] = jnp.dot(_silu(c_ref[...]), w_ref[...], precision=HIGHEST, preferred_element_type=f32) + b_ref[...]

    return pl.pallas_call(body, name="ada_fwd", out_shape=jax.ShapeDtypeStruct((N_DEV, n), f32),
                          compiler_params=_cparams())(c_all, w, b)


def _ada_bwd_adamw(c_all, dmod, w, m, v):
    d, n = w.shape
    tr = _pick(d, (256, 128))

    def body(c_ref, dm_ref, w_ref, m_ref, v_ref, g_ref, dl_ref, mo_ref, vo_ref):
        g = lax.dot_general(_silu(c_ref[...]), dm_ref[...], _DIMS["tn"], precision=HIGHEST,
                            preferred_element_type=f32)
        g_ref[...] = g
        dl_ref[...], mo_ref[...], vo_ref[...] = _adamw(w_ref[...], g, m_ref[...], v_ref[...])

    blk = pl.BlockSpec((tr, n), lambda i: (i, 0))
    return pl.pallas_call(
        body, name="ada_bwd_adamw", grid=(d // tr,),
        in_specs=[pl.BlockSpec((N_DEV, tr), lambda i: (0, i)), pl.BlockSpec((N_DEV, n), lambda i: (0, 0)), blk, blk, blk],
        out_specs=[blk] * 4, out_shape=[jax.ShapeDtypeStruct((d, n), f32)] * 4,
        compiler_params=_cparams(dimension_semantics=("parallel",)),
    )(c_all, dmod, w, m, v)


def _sum_over_mesh(g):
    def body(g_ref, o_ref):
        acc = g_ref[0]
        for d in range(1, N_DEV):
            acc = acc + g_ref[d]
        o_ref[...] = acc

    return pl.pallas_call(body, name="sum_small", out_shape=jax.ShapeDtypeStruct(g.shape[1:], f32),
                          compiler_params=_cparams())(g)


def _adamw_small(w, g, m, v):
    def body(w_ref, g_ref, m_ref, v_ref, dl_ref, mo_ref, vo_ref):
        dl_ref[...], mo_ref[...], vo_ref[...] = _adamw(w_ref[...], g_ref[...], m_ref[...], v_ref[...])

    return pl.pallas_call(body, name="adamw_small", out_shape=[jax.ShapeDtypeStruct(w.shape, f32)] * 3,
                          compiler_params=_cparams())(w, g, m, v)


def _sum_adamw(recv, w, m, v, name):
    _, r, c = recv.shape
    tr = _pick(r, (256, 128, 64, 16))

    def body(g_ref, w_ref, m_ref, v_ref, go_ref, dl_ref, mo_ref, vo_ref):
        g = g_ref[0].astype(f32)
        for d in range(1, N_DEV):
            g = g + g_ref[d].astype(f32)
        go_ref[...] = g
        dl_ref[...], mo_ref[...], vo_ref[...] = _adamw(w_ref[...], g, m_ref[...], v_ref[...])

    blk = pl.BlockSpec((tr, c), lambda i: (i, 0))
    return pl.pallas_call(
        body, name=name, grid=(r // tr,),
        in_specs=[pl.BlockSpec((N_DEV, tr, c), lambda i: (0, i, 0)), blk, blk, blk],
        out_specs=[blk] * 4, out_shape=[jax.ShapeDtypeStruct((r, c), f32)] * 4,
        compiler_params=_cparams(dimension_semantics=("parallel",)),
    )(recv, w, m, v)


def _pack_small(arrs):
    parts = []
    for a in arrs:
        flat = a.reshape(-1).astype(f32)
        parts.append(jnp.pad(flat, (0, (-flat.shape[0]) % LANE)))
    flat = jnp.concatenate(parts)
    flat = jnp.pad(flat, (0, (-flat.shape[0]) % (8 * LANE)))
    return flat.reshape(-1, LANE)


def _unpack_small(packed, shapes):
    flat = packed.reshape(-1)
    out, off = [], 0
    for shp in shapes:
        n = int(np.prod(shp))
        out.append(flat[off:off + n].reshape(shp))
        off += n + (-n) % LANE
    return out


BIG = ("w_attn_out", "w_ssd_out", "w_o", "w_mlp1", "w_mlp2")
BIG_ROWS = (N_Q_HEADS * HEAD_DIM // N_DEV, D_INNER // N_DEV, D_MODEL // N_DEV,
            D_MODEL * (D_FF // N_DEV) // PACK_COLS, D_FF // N_DEV)
N_IN_SHARD = D_IN_PROJ // N_DEV
assert sum(BIG_ROWS) % 16 == 0


def _pack_big(shards, dtype):
    return jnp.concatenate([s.astype(dtype).reshape(-1, PACK_COLS) for s in shards], axis=0)


def _unpack_big(packed, shapes):
    out, off = [], 0
    for rows, shp in zip(BIG_ROWS, shapes):
        out.append(packed[off:off + rows].reshape(shp))
        off += rows
    return out


LATE = ("attn_out", "ssd_out", "o", "mlp1", "mlp2")
LATE_SHAPES = ((N_Q_HEADS * HEAD_DIM, D_MODEL), (D_INNER, D_MODEL), (D_MODEL, D_MODEL), (D_MODEL, D_FF),
               (D_FF, D_MODEL))


def _split_w_in(g_in):
    w_in = g_in.transpose(1, 0, 2).reshape(D_MODEL, D_IN_PROJ)
    w = {}
    off = 0
    for name, size in zip(PROJ_NAMES, PROJ_SIZES):
        w[name] = w_in[:, off:off + size]
        off += size
    w["dt"] = jnp.pad(w["dt"], ((0, 0), (0, DT_PAD - 2 * N_SSD_HEADS)))
    return w


def _split_late(g):
    offs = np.cumsum((0,) + BIG_ROWS)
    sl = [g[:, offs[i]:offs[i + 1]] for i in range(len(BIG))]
    return {"attn_out": sl[0].reshape(LATE_SHAPES[0]), "ssd_out": sl[1].reshape(LATE_SHAPES[1]),
            "o": sl[2].reshape(LATE_SHAPES[2]),
            "mlp1": sl[3].reshape(N_DEV, D_MODEL, D_FF // N_DEV).transpose(1, 0, 2).reshape(LATE_SHAPES[3]),
            "mlp2": sl[4].reshape(LATE_SHAPES[4])}


def _pack_in_grads(gw):
    gw = {n: g.astype(bf16) for n, g in gw.items()}
    gw["dt"] = gw["dt"][:, :2 * N_SSD_HEADS]
    g_in = jnp.concatenate([gw[n] for n in PROJ_NAMES], axis=1)
    return g_in.reshape(D_MODEL, N_DEV, N_IN_SHARD).transpose(1, 0, 2)


def _pack_late_grads(gw):
    gw = {n: g.astype(bf16) for n, g in gw.items()}
    parts = [
        gw["attn_out"].reshape(N_DEV, -1, PACK_COLS),
        gw["ssd_out"].reshape(N_DEV, -1, PACK_COLS),
        gw["o"].reshape(N_DEV, -1, PACK_COLS),
        gw["mlp1"].reshape(D_MODEL, N_DEV, D_FF // N_DEV).transpose(1, 0, 2).reshape(N_DEV, -1, PACK_COLS),
        gw["mlp2"].reshape(N_DEV, -1, PACK_COLS),
    ]
    return jnp.concatenate(parts, axis=1)


SMALL = ("norm1_w", "norm2_w", "q_norm_w", "k_norm_w", "conv_w", "conv_b", "A_log", "dt_bias", "ssd_D", "ssd_norm_w")


def kernel(x, c, w_ada, b_ada, norm1_w, norm2_w, w_in, q_norm_w, k_norm_w, conv_w, conv_b, A_log, dt_bias, ssd_D, ssd_norm_w, w_attn_out, w_ssd_out, w_o, w_mlp1, w_mlp2, loss_target, m_w_ada, m_b_ada, m_norm1_w, m_norm2_w, m_w_in, m_q_norm_w, m_k_norm_w, m_conv_w, m_conv_b, m_A_log, m_dt_bias, m_ssd_D, m_ssd_norm_w, m_w_attn_out, m_w_ssd_out, m_w_o, m_w_mlp1, m_w_mlp2, v_w_ada, v_b_ada, v_norm1_w, v_norm2_w, v_w_in, v_q_norm_w, v_k_norm_w, v_conv_w, v_conv_b, v_A_log, v_dt_bias, v_ssd_D, v_ssd_norm_w, v_w_attn_out, v_w_ssd_out, v_w_o, v_w_mlp1, v_w_mlp2):
    args = dict(locals())
    me = _my_index()
    n_ada = 6 * D_MODEL // N_DEV
    n_cw = CONV_DIM // N_DEV

    blk = jnp.zeros((8, D_MODEL), f32)
    blk = blk.at[0:1, :].set(c)
    blk = blk.at[1:1 + D_CONV, :n_cw].set(conv_w[0])
    g0 = _all_gather(blk, "gather_c_convw", in_vmem=True)
    c_all = g0[:, 0, :]
    conv_w_full = g0[:, 1:1 + D_CONV, :n_cw].transpose(1, 0, 2).reshape(D_CONV, CONV_DIM)

    b_shard = lax.dynamic_slice(b_ada, (0, me * n_ada), (1, n_ada))
    mod_cols = _ada_fwd(c_all, w_ada[0], b_shard)
    g1 = _all_gather(mod_cols, "gather_mod", in_vmem=True)
    mod_mine = lax.dynamic_index_in_dim(g1, me, axis=1, keepdims=False)
    mod = mod_mine.reshape(6, 1, D_MODEL)

    big_shapes = [args[n].shape[1:] for n in BIG]
    late_shard = _pack_big([args[n][0] for n in BIG], bf16)
    wfull = _split_w_in(_all_gather(w_in[0].astype(bf16), "gather_w_in", in_vmem=False))
    recv_in_like = jnp.zeros((N_DEV,) + w_in.shape[1:], bf16)
    recv_late_like = jnp.zeros((N_DEV,) + late_shard.shape, bf16)

    small = {"norm1_w": norm1_w, "norm2_w": norm2_w, "q_norm_w": q_norm_w, "k_norm_w": k_norm_w,
             "conv_w": conv_w_full, "conv_b": conv_b, "A_log": A_log[0], "dt_bias": dt_bias[0], "ssd_D": ssd_D,
             "ssd_norm_w": ssd_norm_w}

    loss, (gx, gmod, gsmall, recv_in, recv_late) = jax.value_and_grad(local_loss, argnums=(0, 1, 2, 3, 4))(
        x[0], mod, small, recv_in_like, recv_late_like, wfull, late_shard, loss_target[0])

    small_list = [gmod, gsmall["norm1_w"], gsmall["norm2_w"], gsmall["q_norm_w"], gsmall["k_norm_w"], gsmall["conv_w"],
                  gsmall["conv_b"], gsmall["A_log"], gsmall["dt_bias"], gsmall["ssd_D"], gsmall["ssd_norm_w"],
                  loss.reshape(1)]
    small_shapes = [a.shape for a in small_list]
    g2 = _all_gather(_pack_small(small_list), "gather_small_grads", in_vmem=True)
    summed = _unpack_small(_sum_over_mesh(g2), small_shapes)
    loss_total = summed[-1][0]
    g_b_ada = summed[0].reshape(1, 6 * D_MODEL)
    g_small = dict(zip(SMALL, summed[1:-1]))
    g_conv_w = lax.dynamic_slice(g_small["conv_w"], (0, me * n_cw), (D_CONV, n_cw))

    dmod_all = g2[:, :6 * D_MODEL // LANE, :].reshape(N_DEV, 6 * D_MODEL)
    dmod_shard = lax.dynamic_slice(dmod_all, (0, me * n_ada), (N_DEV, n_ada))
    ada = _ada_bwd_adamw(c_all, dmod_shard, w_ada[0], m_w_ada[0], v_w_ada[0])

    small_grads = {"b_ada": g_b_ada, "norm1_w": g_small["norm1_w"], "norm2_w": g_small["norm2_w"],
                   "q_norm_w": g_small["q_norm_w"], "k_norm_w": g_small["k_norm_w"], "conv_w": g_conv_w[None],
                   "conv_b": g_small["conv_b"], "A_log": g_small["A_log"][None], "dt_bias": g_small["dt_bias"][None],
                   "ssd_D": g_small["ssd_D"], "ssd_norm_w": g_small["ssd_norm_w"]}
    sm_names = list(small_grads)
    sm_shapes = [args[n].shape for n in sm_names]
    sm = _adamw_small(_pack_small([args[n] for n in sm_names]), _pack_small([small_grads[n] for n in sm_names]),
                      _pack_small([args["m_" + n] for n in sm_names]), _pack_small([args["v_" + n] for n in sm_names]))
    sm_delta, sm_m, sm_v = [dict(zip(sm_names, _unpack_small(t, sm_shapes))) for t in sm]
    small_grads = {n: small_grads[n].reshape(args[n].shape) for n in sm_names}

    w_in_out = _sum_adamw(recv_in, w_in[0], m_w_in[0], v_w_in[0], "sum_adamw_w_in")
    big = _sum_adamw(recv_late, _pack_big([args[n][0] for n in BIG], f32),
                     _pack_big([args["m_" + n][0] for n in BIG], f32),
                     _pack_big([args["v_" + n][0] for n in BIG], f32), "sum_adamw")
    big_g, big_delta, big_m, big_v = [dict(zip(BIG, [t[None] for t in _unpack_big(p, big_shapes)])) for p in big]
    big_g["w_in"], big_delta["w_in"], big_m["w_in"], big_v["w_in"] = [t[None] for t in w_in_out]

    names = ("w_ada", "b_ada", "norm1_w", "norm2_w", "w_in", "q_norm_w", "k_norm_w", "conv_w", "conv_b", "A_log",
             "dt_bias", "ssd_D", "ssd_norm_w", "w_attn_out", "w_ssd_out", "w_o", "w_mlp1", "w_mlp2")
    grads, deltas, new_m, new_v = {}, {}, {}, {}
    for n in names:
        if n == "w_ada":
            grads[n], deltas[n], new_m[n], new_v[n] = [t[None] for t in ada]
        elif n in big_g:
            grads[n], deltas[n], new_m[n], new_v[n] = big_g[n], big_delta[n], big_m[n], big_v[n]
        else:
            grads[n], deltas[n], new_m[n], new_v[n] = small_grads[n], sm_delta[n], sm_m[n], sm_v[n]
    return (loss_total, gx[None], *[grads[n] for n in names], *[deltas[n] for n in names],
            *[new_m[n] for n in names], *[new_v[n] for n in names])
```

```python
import functools
import math

import jax
import jax.numpy as jnp
import numpy as np
from jax import lax
from jax.experimental import pallas as pl
from jax.experimental.pallas import tpu as pltpu

f32 = jnp.float32
bf16 = jnp.bfloat16
HIGHEST = lax.Precision.HIGHEST
MESH = pl.DeviceIdType.MESH

N_DEV = 8
D_MODEL = 1024
GRID_W = 64
N_Q_HEADS = 16
N_KV_HEADS = 4
HEAD_DIM = 64
ROPE_THETA = 10000.0
D_INNER = 2048
SSD_HEAD_DIM = 64
N_SSD_HEADS = 32
N_SSD_GROUPS = 4
D_STATE = 128
D_CONV = 5
CHUNK = 128
D_FF = 4096
EPS = 1e-6
CONV_DIM = D_INNER + 2 * N_SSD_GROUPS * D_STATE
GN = N_SSD_GROUPS * D_STATE
PROJ_NAMES = ("q", "k", "v", "xbc", "z", "dt", "ga", "gs")
PROJ_SIZES = (N_Q_HEADS * HEAD_DIM, N_KV_HEADS * HEAD_DIM, N_KV_HEADS * HEAD_DIM, CONV_DIM, D_INNER,
              2 * N_SSD_HEADS, D_MODEL, D_MODEL)
D_IN_PROJ = sum(PROJ_SIZES)
PROJ_DTYPES = (jnp.bfloat16, jnp.bfloat16, jnp.bfloat16, jnp.float32, jnp.bfloat16, jnp.float32, jnp.bfloat16,
               jnp.bfloat16)
DT_PAD = 128

ADAM_LR, ADAM_B1, ADAM_B2, ADAM_EPS, ADAM_WD, ADAM_STEP = 0.001, 0.9, 0.999, 1e-08, 0.01, 10

V7X_VMEM_LIMIT = 56 * 1024 * 1024
LANE = 128
PACK_COLS = 1024


def _cparams(**kw):
    return pltpu.CompilerParams(vmem_limit_bytes=V7X_VMEM_LIMIT, **kw)


def _pick(dim, prefs):
    for p in prefs:
        if dim % p == 0:
            return p
    return dim


def _my_index():
    return 4 * lax.axis_index("x") + 2 * lax.axis_index("y") + lax.axis_index("c")


COMM_SEMS = [pltpu.SemaphoreType.DMA((7,)), pltpu.SemaphoreType.DMA((7,)), pltpu.SemaphoreType.DMA]


def _gather_phases(x_ref, out_ref, send_sems, recv_sems, local_sem):
    x, y, cc = lax.axis_index("x"), lax.axis_index("y"), lax.axis_index("c")
    me, sibling = (x, y, cc), (x, y, 1 - cc)
    chips = [(1 - x, y), (x, 1 - y), (1 - x, 1 - y)]

    def slot(px, py, pc):
        return out_ref.at[4 * px + 2 * py + pc]

    def copy(k, blk, to, src=None):
        return pltpu.make_async_remote_copy(
            src_ref=slot(*blk) if src is None else src, dst_ref=slot(*blk),
            send_sem=send_sems.at[k], recv_sem=recv_sems.at[k], device_id=to, device_id_type=MESH)

    mine = pltpu.make_async_copy(x_ref, slot(*me), local_sem)
    first = [copy(0, me, sibling, src=x_ref)]
    first += [copy(1 + j, me, (*chip, cc), src=x_ref) for j, chip in enumerate(chips)]
    passed = [copy(4 + j, (*chip, cc), sibling) for j, chip in enumerate(chips)]

    def start():
        mine.start()
        for cp in first:
            cp.start()

    def finish():
        for j, chip in enumerate(chips):
            copy(1 + j, (*chip, cc), me).wait_recv()
            passed[j].start()
        copy(0, sibling, me).wait_recv()
        for j, chip in enumerate(chips):
            copy(4 + j, (*chip, 1 - cc), me).wait_recv()
        for cp in first + passed:
            cp.wait_send()
        mine.wait()

    return start, finish


def _scatter_phases(g_ref, out_ref, send_sems, recv_sems, local_sem):
    x, y, cc = lax.axis_index("x"), lax.axis_index("y"), lax.axis_index("c")
    me = 4 * x + 2 * y + cc
    mine = pltpu.make_async_copy(g_ref.at[me], out_ref.at[me], local_sem)

    def copy(k):
        fx, fy, fc = (k >> 2) & 1, (k >> 1) & 1, k & 1
        px = x + fx - 2 * x * fx
        py = y + fy - 2 * y * fy
        pc = cc + fc - 2 * cc * fc
        peer = 4 * px + 2 * py + pc
        send = pltpu.make_async_remote_copy(
            src_ref=g_ref.at[peer], dst_ref=out_ref.at[me],
            send_sem=send_sems.at[k - 1], recv_sem=recv_sems.at[k - 1],
            device_id=(px, py, pc), device_id_type=MESH)
        recv = pltpu.make_async_remote_copy(
            src_ref=g_ref.at[peer], dst_ref=out_ref.at[peer],
            send_sem=send_sems.at[k - 1], recv_sem=recv_sems.at[k - 1],
            device_id=(px, py, pc), device_id_type=MESH)
        return send, recv

    pairs = [copy(k) for k in range(1, N_DEV)]

    def start():
        mine.start()
        for send, _ in pairs:
            send.start()

    def finish():
        for _, recv in pairs:
            recv.wait_recv()
        for send, _ in pairs:
            send.wait_send()
        mine.wait()

    return start, finish


def _all_gather(block, name, in_vmem):
    r, c = block.shape

    def body(x_ref, out_ref, send_sems, recv_sems, local_sem):
        start, finish = _gather_phases(x_ref, out_ref, send_sems, recv_sems, local_sem)
        start()
        finish()

    space = pltpu.VMEM if in_vmem else pl.ANY
    return pl.pallas_call(
        body, name=name,
        out_shape=jax.ShapeDtypeStruct((N_DEV, r, c), block.dtype),
        in_specs=[pl.BlockSpec(memory_space=space)],
        out_specs=pl.BlockSpec(memory_space=space),
        scratch_shapes=[pltpu.SemaphoreType.DMA((7,)), pltpu.SemaphoreType.DMA((7,)), pltpu.SemaphoreType.DMA],
    )(block)


def _scatter_blocks(g, name):
    _, r, c = g.shape

    def body(g_ref, out_ref, send_sems, recv_sems, local_sem):
        start, finish = _scatter_phases(g_ref, out_ref, send_sems, recv_sems, local_sem)
        start()
        finish()

    return pl.pallas_call(
        body, name=name,
        out_shape=jax.ShapeDtypeStruct(g.shape, g.dtype),
        in_specs=[pl.BlockSpec(memory_space=pl.ANY)],
        out_specs=pl.BlockSpec(memory_space=pl.ANY),
        scratch_shapes=[pltpu.SemaphoreType.DMA((7,)), pltpu.SemaphoreType.DMA((7,)), pltpu.SemaphoreType.DMA],
    )(g)


_DIMS = {"nn": (((1,), (0,)), ((), ())), "nt": (((1,), (1,)), ((), ())), "tn": (((0,), (0,)), ((), ()))}


def _matmul(a, b, mode, out_dtype, name, epilogue=None, side=None):
    if mode == "nn":
        (m, k), (_, n) = a.shape, b.shape
    elif mode == "nt":
        (m, k), (n, _) = a.shape, b.shape
    else:
        (k, m), (_, n) = a.shape, b.shape
    tm = _pick(m, (1024, 512, 256, 128))
    if mode == "tn":
        tn = _pick(n, (1536, 1024, 512, 256, 128))
        tk = _pick(k, (2048, 1024, 512, 256, 128)) if b.dtype == bf16 else _pick(k, (1024, 512, 256, 128))
    else:
        tn = _pick(n, (1024, 512, 384, 256, 128))
        tk = _pick(k, (2048, 1024, 512, 256, 128)) if a.dtype == bf16 else _pick(k, (1024, 512, 256, 128))
    nk = k // tk
    dims = _DIMS[mode]
    n_in = 3 if epilogue == "drelu2" else 2

    def body(*refs):
        a_ref, b_ref = refs[:2]
        o_ref, acc_ref = refs[n_in], refs[n_in + 1]
        kk = pl.program_id(2)
        part = lax.dot_general(a_ref[...].astype(bf16), b_ref[...].astype(bf16), dims, preferred_element_type=f32)

        def finish(acc):
            if epilogue == "relu2":
                r = jnp.maximum(acc, 0.0)
                o_ref[...] = (r * r).astype(out_dtype)
            elif epilogue == "drelu2":
                o_ref[...] = (acc * (2.0 * jnp.sqrt(refs[2][...].astype(f32)))).astype(out_dtype)
            else:
                o_ref[...] = acc.astype(out_dtype)

        if nk == 1:
            finish(part)
        else:
            @pl.when(kk == 0)
            def _():
                acc_ref[...] = part

            @pl.when(kk > 0)
            def _():
                acc_ref[...] += part

            @pl.when(kk == nk - 1)
            def _():
                finish(acc_ref[...])

    if mode == "tn":
        a_spec = pl.BlockSpec((tk, tm), lambda i, j, kk: (kk, i))
    else:
        a_spec = pl.BlockSpec((tm, tk), lambda i, j, kk: (i, kk))
    if mode == "nt":
        b_spec = pl.BlockSpec((tn, tk), lambda i, j, kk: (j, kk))
    else:
        b_spec = pl.BlockSpec((tk, tn), lambda i, j, kk: (kk, j))
    o_spec = pl.BlockSpec((tm, tn), lambda i, j, kk: (i, j))
    o_shape = jax.ShapeDtypeStruct((m, n), out_dtype)
    return pl.pallas_call(
        body, name=name, grid=(m // tm, n // tn, nk),
        in_specs=[a_spec, b_spec] + ([o_spec] if epilogue == "drelu2" else []),
        out_specs=o_spec, out_shape=o_shape,
        scratch_shapes=[pltpu.VMEM((tm, tn), f32)],
        compiler_params=_cparams(dimension_semantics=("parallel", "parallel", "arbitrary")),
    )(*((a, b, side) if epilogue == "drelu2" else (a, b)))


@jax.custom_vjp
def mlp(h, w1, w1grad, w2, w2grad):
    r = _matmul(h, w1, "nn", bf16, "mlp1_fwd", epilogue="relu2")
    return _matmul(r, w2, "nn", f32, "mlp2_fwd")


def _mlp_fwd(h, w1, w1grad, w2, w2grad):
    r = _matmul(h, w1, "nn", bf16, "mlp1_fwd", epilogue="relu2")
    return _matmul(r, w2, "nn", f32, "mlp2_fwd"), (h, w1, w2, r)


def _mlp_bwd(res, dy):
    h, w1, w2, r = res
    du = _matmul(dy, w2, "nt", bf16, "mlp2_dgrad", epilogue="drelu2", side=r)
    dw2 = _matmul(r, dy, "tn", f32, "mlp2_wgrad")
    dh = _matmul(du, w1, "nt", h.dtype, "mlp1_dgrad")
    dw1 = _matmul(h, du, "tn", f32, "mlp1_wgrad")
    return dh, jnp.zeros_like(w1), dw1, jnp.zeros_like(w2), dw2


mlp.defvjp(_mlp_fwd, _mlp_bwd)


def make_linear(name):
    @jax.custom_vjp
    def linear(a, w, wgrad):
        return _matmul(a, w, "nn", f32, name + "_fwd")

    def fwd(a, w, wgrad):
        return linear(a, w, wgrad), (a, w)

    def bwd(res, dy):
        a, w = res
        da = _matmul(dy, w, "nt", a.dtype, name + "_dgrad")
        dw = _matmul(a, dy, "tn", f32, name + "_wgrad")
        return da, jnp.zeros_like(w), dw

    linear.defvjp(fwd, bwd)
    return linear


def _in_proj_dgrad(dys, ws, g):
    s, d = dys[0].shape[0], ws[0].shape[0]
    tm = _pick(s, (1024, 512, 256, 128))
    tks = [w.shape[1] if w.shape[1] <= 1024 else 512 for w in ws]
    steps = [w.shape[1] // tk for w, tk in zip(ws, tks)]
    starts = [sum(steps[:p]) for p in range(len(ws))]
    total = sum(steps)
    n_p, n_i = len(ws), s // tm
    assert steps[0] == 1

    def body(*refs):
        dy_refs, w_refs, g_ref = refs[:n_p], refs[n_p:2 * n_p], refs[2 * n_p]
        dh_ref, recv_ref, acc_ref, send_sems, recv_sems, local_sem = refs[2 * n_p + 1:]
        i, t = pl.program_id(0), pl.program_id(1)
        start, finish = _scatter_phases(g_ref, recv_ref, send_sems, recv_sems, local_sem)

        @pl.when((i == 0) & (t == 0))
        def _():
            start()

        for p in range(n_p):
            @pl.when((t >= starts[p]) & (t < starts[p] + steps[p]))
            def _(p=p):
                part = lax.dot_general(dy_refs[p][...].astype(bf16), w_refs[p][...], _DIMS["nt"],
                                       preferred_element_type=f32)
                if p == 0:
                    acc_ref[...] = part
                else:
                    acc_ref[...] += part

        @pl.when(t == total - 1)
        def _():
            dh_ref[...] = acc_ref[...].astype(dh_ref.dtype)

        @pl.when((i == n_i - 1) & (t == total - 1))
        def _():
            finish()

    def piece_map(p, rows):
        def index_map(i, t):
            blk = jnp.clip(t - starts[p], 0, steps[p] - 1)
            return (i, blk) if rows else (0, blk)

        return index_map

    hbm = pl.BlockSpec(memory_space=pl.ANY)
    in_specs = [pl.BlockSpec((tm, tks[p]), piece_map(p, True)) for p in range(n_p)]
    in_specs += [pl.BlockSpec((d, tks[p]), piece_map(p, False)) for p in range(n_p)]
    return pl.pallas_call(
        body, name="in_proj_dgrad", grid=(n_i, total), in_specs=in_specs + [hbm],
        out_specs=[pl.BlockSpec((tm, d), lambda i, t: (i, 0)), hbm],
        out_shape=[jax.ShapeDtypeStruct((s, d), bf16), jax.ShapeDtypeStruct(g.shape, g.dtype)],
        scratch_shapes=[pltpu.VMEM((tm, d), f32)] + COMM_SEMS,
        compiler_params=_cparams(dimension_semantics=("arbitrary", "arbitrary")),
    )(*dys, *ws, g)


@jax.custom_vjp
def in_proj(h, ws, recv_like):
    return tuple(_matmul(h, w, "nn", dt, "lin_" + n + "_fwd") for n, w, dt in zip(PROJ_NAMES, ws, PROJ_DTYPES))


def _in_proj_fwd(h, ws, recv_like):
    return in_proj(h, ws, recv_like), (h, ws)


def _in_proj_bwd(res, dys):
    h, ws = res
    dws = {n: _matmul(h, dy, "tn", f32, "lin_" + n + "_wgrad") for n, dy in zip(PROJ_NAMES, dys)}
    dh, recv = _in_proj_dgrad(dys, ws, _pack_in_grads(dws))
    return dh.astype(h.dtype), tuple(jnp.zeros_like(w) for w in ws), recv


in_proj.defvjp(_in_proj_fwd, _in_proj_bwd)


def make_rowwise(name, fn, row_out, sum_out=(), tm_pref=512):
    def specs(rows, gpars, cpars, consts, tm):
        s = [pl.BlockSpec((tm, r.shape[1]), lambda i: (i, 0)) for r in rows]
        s += [pl.BlockSpec(p.shape, lambda i: (0, 0)) for p in gpars]
        s += [pl.BlockSpec(p.shape, lambda i: (0, 0)) for p in cpars]
        for cst in consts:
            nb = cst.shape[0] // tm
            s.append(pl.BlockSpec((tm, cst.shape[1]), lambda i, nb=nb: (i % nb, 0)))
        return s

    def tile_rows(rows, consts):
        r = rows[0].shape[0]
        common = math.gcd(r, *[cst.shape[0] for cst in consts])
        tm = _pick(common, (tm_pref, 512, 256, 128, 64, 32, 16, 8))
        return r, tm

    def forward(rows, gpars, cpars, consts):
        r, tm = tile_rows(rows, consts)
        nr, ng, nc, nk = len(rows), len(gpars), len(cpars), len(consts)

        def body(*refs):
            ins = refs[:nr + ng + nc + nk]
            outs = refs[nr + ng + nc + nk:]
            rv = [t[...].astype(f32) for t in ins[:nr]]
            gv = [t[...].astype(f32) for t in ins[nr:nr + ng]]
            cv = [t[...] for t in ins[nr + ng:nr + ng + nc]]
            kv = [t[...].astype(f32) for t in ins[nr + ng + nc:]]
            ro, so = fn(rv, gv, cv, kv)
            for o_ref, val in zip(outs[:len(row_out)], ro):
                o_ref[...] = val.astype(o_ref.dtype)
            if sum_out:
                @pl.when(pl.program_id(0) == 0)
                def _():
                    for o_ref in outs[len(row_out):]:
                        o_ref[...] = jnp.zeros_like(o_ref)
                for o_ref, val in zip(outs[len(row_out):], so):
                    o_ref[...] += val

        out_specs = [pl.BlockSpec((tm, w), lambda i: (i, 0)) for w, _ in row_out]
        out_specs += [pl.BlockSpec(shp, lambda i: (0, 0)) for shp in sum_out]
        out_shape = [jax.ShapeDtypeStruct((r, w), dt) for w, dt in row_out]
        out_shape += [jax.ShapeDtypeStruct(shp, f32) for shp in sum_out]
        res = pl.pallas_call(
            body, name=name + "_fwd", grid=(r // tm,),
            in_specs=specs(rows, gpars, cpars, consts, tm), out_specs=out_specs, out_shape=out_shape,
            compiler_params=_cparams(dimension_semantics=("arbitrary",)),
        )(*rows, *gpars, *cpars, *consts)
        return tuple(res[:len(row_out)]), tuple(res[len(row_out):])

    def backward(rows, gpars, cpars, consts, d_ro, d_so):
        r, tm = tile_rows(rows, consts)
        nr, ng, nc, nk = len(rows), len(gpars), len(cpars), len(consts)
        n_in = nr + ng + nc + nk + len(row_out) + len(sum_out)

        def body(*refs):
            ins, outs = refs[:n_in], refs[n_in:]
            rv = [t[...].astype(f32) for t in ins[:nr]]
            gv = [t[...].astype(f32) for t in ins[nr:nr + ng]]
            cv = [t[...] for t in ins[nr + ng:nr + ng + nc]]
            kv = [t[...].astype(f32) for t in ins[nr + ng + nc:nr + ng + nc + nk]]
            o = nr + ng + nc + nk
            dro = [t[...].astype(f32) for t in ins[o:o + len(row_out)]]
            dso = [t[...] for t in ins[o + len(row_out):]]
            _, vjp = jax.vjp(lambda a, b: tuple(tuple(t) for t in fn(a, b, cv, kv)), rv, gv)
            drv, dgv = vjp((tuple(dro), tuple(dso)))
            for o_ref, val in zip(outs[:nr], drv):
                o_ref[...] = val.astype(o_ref.dtype)
            if ng:
                @pl.when(pl.program_id(0) == 0)
                def _():
                    for o_ref in outs[nr:]:
                        o_ref[...] = jnp.zeros_like(o_ref)
                for o_ref, val in zip(outs[nr:], dgv):
                    o_ref[...] += val

        in_specs = specs(rows, gpars, cpars, consts, tm)
        in_specs += [pl.BlockSpec((tm, w), lambda i: (i, 0)) for w, _ in row_out]
        in_specs += [pl.BlockSpec(shp, lambda i: (0, 0)) for shp in sum_out]
        out_specs = [pl.BlockSpec((tm, t.shape[1]), lambda i: (i, 0)) for t in rows]
        out_specs += [pl.BlockSpec(p.shape, lambda i: (0, 0)) for p in gpars]
        out_shape = [jax.ShapeDtypeStruct(t.shape, t.dtype) for t in rows]
        out_shape += [jax.ShapeDtypeStruct(p.shape, f32) for p in gpars]
        res = pl.pallas_call(
            body, name=name + "_bwd", grid=(r // tm,),
            in_specs=in_specs, out_specs=out_specs, out_shape=out_shape,
            compiler_params=_cparams(dimension_semantics=("arbitrary",)),
        )(*rows, *gpars, *cpars, *consts, *d_ro, *d_so)
        return tuple(res[:nr]), tuple(res[nr:])

    @jax.custom_vjp
    def op(rows, gpars, cpars, consts):
        return forward(rows, gpars, cpars, consts)

    def op_fwd(rows, gpars, cpars, consts):
        return forward(rows, gpars, cpars, consts), (rows, gpars, cpars, consts)

    def op_bwd(res, cts):
        rows, gpars, cpars, consts = res
        d_ro, d_so = cts
        drows, dg = backward(rows, gpars, cpars, consts, d_ro, d_so)
        dg = tuple(d.astype(p.dtype) for d, p in zip(dg, gpars))
        return (drows, dg, tuple(jnp.zeros_like(p) for p in cpars), tuple(jnp.zeros_like(k) for k in consts))

    op.defvjp(op_fwd, op_bwd)
    return op


def _rms(x):
    return x * lax.rsqrt(jnp.mean(x * x, axis=-1, keepdims=True) + EPS)


def _silu(x):
    return x * jax.nn.sigmoid(x)


def _fn_norm_mod(rows, gp, cp, ks):
    (x,), (nw, sc, sh) = rows, gp
    return ((_rms(x) * nw) * (1.0 + sc) + sh, x), ()


PAIR = 2 * HEAD_DIM


def _exact_dot(a, m):
    hi = a.astype(bf16)
    lo = (a - hi.astype(f32)).astype(bf16)
    return jnp.dot(hi, m, preferred_element_type=f32) + jnp.dot(lo, m, preferred_element_type=f32)


def _make_sel_dot(sign):
    @jax.custom_vjp
    def sel_dot(a, m):
        return _exact_dot(a, m)

    def fwd(a, m):
        return _exact_dot(a, m), m

    def bwd(m, g):
        return sign * _exact_dot(g, m), jnp.zeros_like(m)

    sel_dot.defvjp(fwd, bwd)
    return sel_dot


_head_sum_dot = _make_sel_dot(1.0)
_rope_perm_dot = _make_sel_dot(-1.0)


def _pair_norm_rope(t, w2, gsum, perm, cos2, sin2, out_scale):
    ss = _head_sum_dot(t * t, gsum)
    u = t * lax.rsqrt(ss * (1.0 / HEAD_DIM) + EPS) * w2
    return (u * cos2 + _rope_perm_dot(u, perm) * sin2) * out_scale


def _pair_consts():
    eye = np.eye(2, dtype=np.float32)
    gsum = np.kron(eye, np.ones((HEAD_DIM, HEAD_DIM), np.float32))
    return jnp.asarray(gsum, bf16), jnp.asarray(np.kron(eye, _rope_perm()), bf16)


def make_head_rope(name, nh, out_scale, head_major):
    width = nh * HEAD_DIM
    fn = functools.partial(_pair_norm_rope, out_scale=out_scale)

    def out_spec(tm):
        if head_major:
            return pl.BlockSpec((nh, tm, HEAD_DIM), lambda i: (0, i, 0))
        return pl.BlockSpec((tm, width), lambda i: (i, 0))

    def specs(tm):
        def full(shp):
            return pl.BlockSpec(shp, lambda i: (0, 0))

        return [pl.BlockSpec((tm, width), lambda i: (i, 0)), full((1, PAIR)), full((PAIR, PAIR)), full((PAIR, PAIR)),
                pl.BlockSpec((tm, PAIR), lambda i: (i, 0)), pl.BlockSpec((tm, PAIR), lambda i: (i, 0))]

    def forward(t, w2, gsum, perm, cos2, sin2):
        s = t.shape[0]
        tm = _pick(s, (512, 256, 128))

        def body(t_ref, w_ref, g_ref, p_ref, cos_ref, sin_ref, o_ref):
            for b in range(nh // 2):
                val = fn(t_ref[:, b * PAIR:(b + 1) * PAIR].astype(f32), w_ref[...], g_ref[...], p_ref[...], cos_ref[...],
                         sin_ref[...]).astype(o_ref.dtype)
                if head_major:
                    o_ref[2 * b] = val[:, :HEAD_DIM]
                    o_ref[2 * b + 1] = val[:, HEAD_DIM:]
                else:
                    o_ref[:, b * PAIR:(b + 1) * PAIR] = val

        return pl.pallas_call(
            body, name=name + "_fwd", grid=(s // tm,), in_specs=specs(tm), out_specs=out_spec(tm),
            out_shape=jax.ShapeDtypeStruct((nh, s, HEAD_DIM) if head_major else (s, width), bf16),
            compiler_params=_cparams(dimension_semantics=("arbitrary",)),
        )(t, w2, gsum, perm, cos2, sin2)

    def backward(t, w2, gsum, perm, cos2, sin2, dout):
        s = t.shape[0]
        tm = _pick(s, (512, 256, 128))

        def body(t_ref, w_ref, g_ref, p_ref, cos_ref, sin_ref, do_ref, dt_ref, dw_ref, pair_buf):
            @pl.when(pl.program_id(0) == 0)
            def _():
                dw_ref[...] = jnp.zeros_like(dw_ref)

            g_v, p_v, cos_v, sin_v = g_ref[...], p_ref[...], cos_ref[...], sin_ref[...]
            dw = jnp.zeros((1, PAIR), f32)
            for b in range(nh // 2):
                sl = slice(b * PAIR, (b + 1) * PAIR)
                if head_major:
                    pair_buf[:, :HEAD_DIM] = do_ref[2 * b].astype(f32)
                    pair_buf[:, HEAD_DIM:] = do_ref[2 * b + 1].astype(f32)
                    ct = pair_buf[...]
                else:
                    ct = do_ref[:, sl].astype(f32)
                _, vjp = jax.vjp(lambda a, c: fn(a, c, g_v, p_v, cos_v, sin_v), t_ref[:, sl].astype(f32), w_ref[...])
                dtb, dwb = vjp(ct)
                dt_ref[:, sl] = dtb.astype(dt_ref.dtype)
                dw = dw + dwb
            dw_ref[...] += dw

        return pl.pallas_call(
            body, name=name + "_bwd", grid=(s // tm,), in_specs=specs(tm) + [out_spec(tm)],
            out_specs=[pl.BlockSpec((tm, width), lambda i: (i, 0)), pl.BlockSpec((1, PAIR), lambda i: (0, 0))],
            out_shape=[jax.ShapeDtypeStruct((s, width), t.dtype), jax.ShapeDtypeStruct((1, PAIR), f32)],
            scratch_shapes=[pltpu.VMEM((tm, PAIR), f32)],
            compiler_params=_cparams(dimension_semantics=("arbitrary",)),
        )(t, w2, gsum, perm, cos2, sin2, dout)

    @jax.custom_vjp
    def op(t, w2, gsum, perm, cos2, sin2):
        return forward(t, w2, gsum, perm, cos2, sin2)

    def op_fwd(*args):
        return forward(*args), args

    def op_bwd(res, dout):
        dt, dw = backward(*res, dout)
        return (dt, dw) + tuple(jnp.zeros_like(r) for r in res[2:])

    op.defvjp(op_fwd, op_bwd)

    def apply(t, w, cos2, sin2):
        gsum, perm = _pair_consts()
        return op(t, jnp.concatenate([w, w], axis=-1), gsum, perm, cos2, sin2)

    return apply


def _fn_softplus(rows, gp, cp, ks):
    (x,), (b,) = rows, gp
    v = x + b
    return (jnp.maximum(v, 0.0) + jnp.log(1.0 + jnp.exp(-jnp.abs(v))),), ()


def _fn_ssd_gate(rows, gp, cp, ks):
    (y, z), (nw,) = rows, gp
    return (_rms(y * _silu(z)) * nw,), ()


def _fn_merge(rows, gp, cp, ks):
    ao, so, ga, gs = rows
    return (jax.nn.sigmoid(ga) * ao + jax.nn.sigmoid(gs) * so,), ()


def _fn_res_norm(rows, gp, cp, ks):
    (x, mo), (g1, nw, sc, sh) = rows, gp
    x1 = x + g1 * mo
    return (x1, (_rms(x1) * nw) * (1.0 + sc) + sh), ()


def _fn_loss(rows, gp, cp, ks):
    (x1, ff), (g2,), (tgt,) = rows, gp, ks
    err = x1 + g2 * ff - tgt
    return (), (0.5 * jnp.sum(jnp.sum(err * err, axis=-1, keepdims=True), axis=0, keepdims=True) / D_MODEL,)


HALO = 8
HALO_BWD = 16


def _conv_tiles(s, c):
    return _pick(s, (512, 256, 128)), _pick(c, (1024, 512, 256, 128))


def _halo_specs(tm, tc, s, halo=HALO):
    nb = tm // halo
    last = s // halo - 1
    cur = pl.BlockSpec((tm, tc), lambda j, i: (i, j))
    prev = pl.BlockSpec((halo, tc), lambda j, i: (jnp.maximum(i * nb - 1, 0), j))
    nxt = pl.BlockSpec((halo, tc), lambda j, i: (jnp.minimum((i + 1) * nb, last), j))
    return cur, prev, nxt


def _fill_halo(buf, cur, prev, nxt, tm, i, n_i, halo=HALO):
    buf[halo:halo + tm, :] = cur[...]
    buf[0:halo, :] = jnp.where(i > 0, prev[...], 0.0)
    buf[halo + tm:, :] = jnp.where(i < n_i - 1, nxt[...], 0.0)


def _conv_fwd(x, w, b, shard):
    s, c = x.shape
    tm, tc = _conv_tiles(s, c)
    n_i, n_j = s // tm, c // tc

    def body(cur, prev, nxt, w_ref, b_ref, shard_ref, o_ref, gath_ref, buf, send_sems, recv_sems, local_sem):
        j, i = pl.program_id(0), pl.program_id(1)
        start, finish = _gather_phases(shard_ref, gath_ref, send_sems, recv_sems, local_sem)

        @pl.when((j == 0) & (i == 0))
        def _():
            start()

        _fill_halo(buf, cur, prev, nxt, tm, i, n_i)
        pre = jnp.zeros((tm, tc), f32) + b_ref[...]
        for k in range(D_CONV):
            pre = pre + buf[HALO - 2 + k:HALO - 2 + k + tm, :] * w_ref[k:k + 1, :]
        o_ref[...] = _silu(pre)

        @pl.when((j == n_j - 1) & (i == n_i - 1))
        def _():
            finish()

    cur, prev, nxt = _halo_specs(tm, tc, s)
    hbm = pl.BlockSpec(memory_space=pl.ANY)
    return pl.pallas_call(
        body, name="conv_silu_fwd", grid=(n_j, n_i),
        in_specs=[cur, prev, nxt, pl.BlockSpec((D_CONV, tc), lambda j, i: (0, j)),
                  pl.BlockSpec((1, tc), lambda j, i: (0, j)), hbm],
        out_specs=[pl.BlockSpec((tm, tc), lambda j, i: (i, j)), hbm],
        out_shape=[jax.ShapeDtypeStruct((s, c), f32), jax.ShapeDtypeStruct((N_DEV,) + shard.shape, shard.dtype)],
        scratch_shapes=[pltpu.VMEM((tm + 2 * HALO, tc), f32)] + COMM_SEMS,
        compiler_params=_cparams(dimension_semantics=("arbitrary", "arbitrary")),
    )(x, x, x, w, b, shard)


def _conv_bwd(x, w, b, dy, g):
    s, c = x.shape
    tm, tc = _conv_tiles(s, c)
    n_i, n_j = s // tm, c // tc
    ext = tm + 16

    def body(cur, prev, nxt, dcur, dprev, dnxt, w_ref, b_ref, g_ref, dx_ref, dw_ref, db_ref, recv_ref,
             xbuf, dbuf, pbuf, send_sems, recv_sems, local_sem):
        j, i = pl.program_id(0), pl.program_id(1)
        start, finish = _scatter_phases(g_ref, recv_ref, send_sems, recv_sems, local_sem)

        @pl.when((j == 0) & (i == 0))
        def _():
            start()

        _fill_halo(xbuf, cur, prev, nxt, tm, i, n_i, HALO_BWD)
        _fill_halo(dbuf, dcur, dprev, dnxt, tm, i, n_i, HALO_BWD)
        xs = [xbuf[6 + k:6 + k + ext, :] for k in range(D_CONV)]
        pre = jnp.zeros((ext, tc), f32) + b_ref[...]
        for k in range(D_CONV):
            pre = pre + xs[k] * w_ref[k:k + 1, :]
        sg = jax.nn.sigmoid(pre)
        pbuf[...] = dbuf[8:8 + ext, :] * (sg * (1.0 + pre * (1.0 - sg)))
        dx = jnp.zeros((tm, tc), f32)
        for k in range(D_CONV):
            dx = dx + pbuf[10 - k:10 - k + tm, :] * w_ref[k:k + 1, :]
        dx_ref[...] = dx

        @pl.when(i == 0)
        def _():
            dw_ref[...] = jnp.zeros_like(dw_ref)
            db_ref[...] = jnp.zeros_like(db_ref)

        dpre = pbuf[8:8 + tm, :]
        db_ref[...] += jnp.sum(dpre, axis=0, keepdims=True)
        for k in range(D_CONV):
            dw_ref[k:k + 1, :] += jnp.sum(dpre * xs[k][8:8 + tm, :], axis=0, keepdims=True)

        @pl.when((j == n_j - 1) & (i == n_i - 1))
        def _():
            finish()

    cur, prev, nxt = _halo_specs(tm, tc, s, HALO_BWD)
    hbm = pl.BlockSpec(memory_space=pl.ANY)
    return pl.pallas_call(
        body, name="conv_silu_bwd", grid=(n_j, n_i),
        in_specs=[cur, prev, nxt, cur, prev, nxt, pl.BlockSpec((D_CONV, tc), lambda j, i: (0, j)),
                  pl.BlockSpec((1, tc), lambda j, i: (0, j)), hbm],
        out_specs=[pl.BlockSpec((tm, tc), lambda j, i: (i, j)), pl.BlockSpec((D_CONV, tc), lambda j, i: (0, j)),
                   pl.BlockSpec((1, tc), lambda j, i: (0, j)), hbm],
        out_shape=[jax.ShapeDtypeStruct((s, c), f32), jax.ShapeDtypeStruct((D_CONV, c), f32),
                   jax.ShapeDtypeStruct((1, c), f32), jax.ShapeDtypeStruct(g.shape, g.dtype)],
        scratch_shapes=[pltpu.VMEM((tm + 2 * HALO_BWD, tc), f32), pltpu.VMEM((tm + 2 * HALO_BWD, tc), f32),
                        pltpu.VMEM((ext, tc), f32)] + COMM_SEMS,
        compiler_params=_cparams(dimension_semantics=("arbitrary", "arbitrary")),
    )(x, x, x, dy, dy, dy, w, b, g)


@jax.custom_vjp
def conv_silu_comm(x, w, b, shard, recv_like):
    act, gathered = _conv_fwd(x, w, b, shard)
    return (act, gathered) + tuple(jnp.zeros(shp, f32) for shp in LATE_SHAPES)


def _conv_silu_comm_fwd(x, w, b, shard, recv_like):
    return conv_silu_comm(x, w, b, shard, recv_like), (x, w, b, shard)


def _conv_silu_comm_bwd(res, cts):
    x, w, b, shard = res
    dx, dw, db, recv = _conv_bwd(x, w, b, cts[0], _pack_late_grads(dict(zip(LATE, cts[2:]))))
    return dx, dw, db, jnp.zeros_like(shard), recv


conv_silu_comm.defvjp(_conv_silu_comm_fwd, _conv_silu_comm_bwd)


ATT_SCALE = HEAD_DIM ** -0.5
Q_SCALE = ATT_SCALE * math.log2(math.e)
LN2 = math.log(2.0)
REP = N_Q_HEADS // N_KV_HEADS


HP = 2
assert REP % HP == 0


def _attn_fwd(q, k, v):
    s, dh = q.shape[0], HEAD_DIM
    hq = q.shape[1] // dh
    tq = _pick(s, (256, 128))

    v1 = jnp.concatenate([v, jnp.ones(v.shape[:2] + (1,), v.dtype), jnp.zeros(v.shape[:2] + (dh - 1,), v.dtype)],
                         axis=-1)

    def body(q_ref, k_ref, v_ref, o_ref, p_ref, linv_ref):
        for j in range(HP):
            sl = slice(j * dh, (j + 1) * dh)
            sc = lax.dot_general(q_ref[:, sl], k_ref[0], _DIMS["nt"], preferred_element_type=f32)
            m = jnp.max(sc, axis=-1, keepdims=True)
            p = jnp.exp2(sc - m).astype(bf16)
            p_ref[j] = p
            o1 = jnp.dot(p, v_ref[0], preferred_element_type=f32)
            linv = 1.0 / o1[:, dh:dh + 1]
            o_ref[:, sl] = (o1[:, :dh] * linv).astype(o_ref.dtype)
            linv_ref[j] = linv

    return pl.pallas_call(
        body, name="attn_fwd", grid=(hq // HP, s // tq),
        in_specs=[pl.BlockSpec((tq, HP * dh), lambda h, i: (i, h)),
                  pl.BlockSpec((1, s, dh), lambda h, i: (h * HP // REP, 0, 0)),
                  pl.BlockSpec((1, s, 2 * dh), lambda h, i: (h * HP // REP, 0, 0))],
        out_specs=[pl.BlockSpec((tq, HP * dh), lambda h, i: (i, h)),
                   pl.BlockSpec((HP, tq, s), lambda h, i: (h, i, 0)),
                   pl.BlockSpec((HP, tq, 1), lambda h, i: (h, i, 0))],
        out_shape=[jax.ShapeDtypeStruct((s, hq * dh), bf16), jax.ShapeDtypeStruct((hq, s, s), bf16),
                   jax.ShapeDtypeStruct((hq, s, 1), f32)],
        compiler_params=_cparams(dimension_semantics=("parallel", "arbitrary")),
    )(q, k, v1)


def _attn_bwd(p, do, o, q, k, v, linv):
    hq, s, _ = p.shape
    dh = HEAD_DIM
    tq = _pick(s, (256, 128))

    def body(p_ref, do_ref, o_ref, q_ref, k_ref, v_ref, linv_ref, dq_ref, dkt_ref, dvt_ref):
        @pl.when(pl.program_id(1) == 0)
        def _():
            dkt_ref[...] = jnp.zeros_like(dkt_ref)
            dvt_ref[...] = jnp.zeros_like(dvt_ref)

        for j in range(HP):
            sl = slice(j * dh, (j + 1) * dh)
            pp, doh, li = p_ref[j], do_ref[:, sl], linv_ref[j]
            do32 = doh.astype(f32)
            d = jnp.sum(do32 * o_ref[:, sl].astype(f32), axis=-1, keepdims=True)
            dp = lax.dot_general(doh, v_ref[0], _DIMS["nt"], preferred_element_type=f32)
            ds = (pp.astype(f32) * ((dp - d) * li)).astype(bf16)
            dq_ref[:, sl] = (jnp.dot(ds, k_ref[0], preferred_element_type=f32) * LN2).astype(dq_ref.dtype)
            dvt_ref[j] += lax.dot_general((do32 * li).astype(bf16), pp, _DIMS["tn"], preferred_element_type=f32)
            dkt_ref[j] += lax.dot_general(q_ref[:, sl], ds, _DIMS["tn"], preferred_element_type=f32)

    def row():
        return pl.BlockSpec((tq, HP * dh), lambda h, i: (i, h))

    return pl.pallas_call(
        body, name="attn_bwd", grid=(hq // HP, s // tq),
        in_specs=[pl.BlockSpec((HP, tq, s), lambda h, i: (h, i, 0)), row(), row(), row(),
                  pl.BlockSpec((1, s, dh), lambda h, i: (h * HP // REP, 0, 0)),
                  pl.BlockSpec((1, s, dh), lambda h, i: (h * HP // REP, 0, 0)),
                  pl.BlockSpec((HP, tq, 1), lambda h, i: (h, i, 0))],
        out_specs=[row(), pl.BlockSpec((HP, dh, s), lambda h, i: (h, 0, 0)),
                   pl.BlockSpec((HP, dh, s), lambda h, i: (h, 0, 0))],
        out_shape=[jax.ShapeDtypeStruct((s, hq * dh), q.dtype), jax.ShapeDtypeStruct((hq, dh, s), f32),
                   jax.ShapeDtypeStruct((hq, dh, s), f32)],
        compiler_params=_cparams(dimension_semantics=("parallel", "arbitrary")),
    )(p, do, o, q, k, v, linv)


@jax.custom_vjp
def attention(q, k, v):
    return _attn_fwd(q, k, v)[0]


def _attention_fwd(q, k, v):
    o, p, linv = _attn_fwd(q, k, v)
    return o, (q, k, v, o, p, linv)


def _attention_bwd(res, do):
    q, k, v, o, p, linv = res
    s = q.shape[0]
    dq, dkt, dvt = _attn_bwd(p, do.astype(bf16), o, q, k, v, linv)

    def per_kv_head(t):
        return jnp.swapaxes(t.reshape(N_KV_HEADS, REP, HEAD_DIM, s).sum(axis=1), 1, 2)

    return dq, (per_kv_head(dkt) * LN2).astype(k.dtype), per_kv_head(dvt).astype(v.dtype)


attention.defvjp(_attention_fwd, _attention_bwd)


HPG = N_SSD_HEADS // N_SSD_GROUPS
GW = HPG * SSD_HEAD_DIM
NEG = -1e30
SPLIT_ROWS = 32


def _ssd_consts():
    k = np.arange(SPLIT_ROWS)[:, None]
    live = k < 3 * HPG
    sel_chunk = ((k % HPG) == (np.arange(HPG * CHUNK)[None, :] // CHUNK)) & live
    sel_head = ((k % HPG) == (np.arange(GW)[None, :] // SSD_HEAD_DIM)) & live
    return jnp.asarray(sel_chunk, bf16), jnp.asarray(sel_head, bf16)


def _split3(x):
    hi = x.astype(bf16).astype(f32)
    r1 = x - hi
    mid = r1.astype(bf16).astype(f32)
    lo = (r1 - mid).astype(bf16).astype(f32)
    return jnp.concatenate([hi, mid, lo, jnp.zeros_like(hi)], axis=0).astype(bf16)


def _tn(a, b):
    return lax.dot_general(a, b, _DIMS["tn"], preferred_element_type=f32)


def _nt(a, b):
    return lax.dot_general(a, b, _DIMS["nt"], preferred_element_type=f32)


def _nn(a, b):
    return jnp.dot(a, b, preferred_element_type=f32)


def _head_sum(sel8, x):
    hi = x.astype(bf16)
    lo = (x - hi.astype(f32)).astype(bf16)
    return _nt(sel8, hi) + _nt(sel8, lo)


def _ssd_masks(reverse):
    r = lax.broadcasted_iota(jnp.int32, (CHUNK, CHUNK), 0)
    c = lax.broadcasted_iota(jnp.int32, (CHUNK, CHUNK), 1)
    lower, upper = r >= c, r <= c
    return (upper, lower) if reverse else (lower, upper)


def _ssd_in_specs(cidx):
    return [pl.BlockSpec((CHUNK, D_INNER), lambda c: (cidx(c), 0)),
            pl.BlockSpec((CHUNK, GN), lambda c: (cidx(c), D_INNER // GN)),
            pl.BlockSpec((CHUNK, GN), lambda c: (cidx(c), D_INNER // GN + 1)),
            pl.BlockSpec((N_SSD_HEADS, CHUNK), lambda c: (0, cidx(c))),
            pl.BlockSpec((N_SSD_HEADS, 1), lambda c: (0, 0)),
            pl.BlockSpec((SPLIT_ROWS, HPG * CHUNK), lambda c: (0, 0)),
            pl.BlockSpec((SPLIT_ROWS, GW), lambda c: (0, 0))]


def _ssd_chunk_common(dtt_ref, a_ref, et_ref, mask_t):
    dtt = dtt_ref[...]
    et = jnp.dot(dtt * a_ref[...], mask_t.astype(f32), precision=HIGHEST, preferred_element_type=f32)
    et_ref[...] = et
    return dtt, et


def _ssd_group_common(g, dtt, et, selc_ref, selh_ref, xs_ref, b_ref, c_ref, last):
    gr = slice(g * HPG, (g + 1) * HPG)
    e3 = _split3(et[gr])
    col = _tn(e3, selc_ref[...])
    eb = _tn(e3, selh_ref[...])
    dtb = _tn(_split3(dtt[gr]), selh_ref[...])
    tbc = eb[last:last + 1, :]
    xs = xs_ref[:, g * GW:(g + 1) * GW]
    bg = b_ref[:, g * D_STATE:(g + 1) * D_STATE].astype(bf16)
    cg = c_ref[:, g * D_STATE:(g + 1) * D_STATE].astype(bf16)
    return col, eb, dtb, tbc, xs, bg, cg


def _ssd_fwd(xbc, dtt, a_col, reverse, y_prev=None, dexp=None):
    s = xbc.shape[0]
    nc = s // CHUNK
    cidx = (lambda c: nc - 1 - c) if reverse else (lambda c: c)
    last = 0 if reverse else CHUNK - 1
    selc, selh = _ssd_consts()
    final = y_prev is not None
    n_in = 9 if final else 7

    def body(*refs):
        xs_ref, b_ref, c_ref, dtt_ref, a_ref, selc_ref, selh_ref = refs[:7]
        y_ref, st_ref, ht_ref, et_ref, yg_ref = refs[n_in:]

        @pl.when(pl.program_id(0) == 0)
        def _():
            ht_ref[...] = jnp.zeros_like(ht_ref)

        mask, mask_t = _ssd_masks(reverse)
        dtt_v, et = _ssd_chunk_common(dtt_ref, a_ref, et_ref, mask_t)
        for g in range(N_SSD_GROUPS):
            col, eb, dtb, tbc, xs, bg, cg = _ssd_group_common(g, dtt_v, et, selc_ref, selh_ref, xs_ref, b_ref, c_ref,
                                                              last)
            xd = xs * dtb
            cb = _nt(cg, bg)
            ht = ht_ref[g]
            st_ref[0, g] = ht
            yoff = _nn(cg, ht.astype(bf16)) * jnp.exp(eb)
            for j in range(HPG):
                h = g * HPG + j
                hs = slice(j * SSD_HEAD_DIM, (j + 1) * SSD_HEAD_DIM)
                lam = jnp.exp(jnp.where(mask, col[:, j * CHUNK:(j + 1) * CHUNK] - et_ref[h:h + 1, :], NEG))
                yg_ref[:, hs] = _nn((cb * lam).astype(bf16), xd[:, hs].astype(bf16))
            cols = slice(g * GW, (g + 1) * GW)
            yg = yg_ref[...] + yoff
            if final:
                yg = yg + refs[7][:, cols] + xs * refs[8][:, cols]
            y_ref[:, cols] = yg.astype(y_ref.dtype)
            ht_ref[g] = jnp.exp(tbc) * ht + _tn(bg, (xd * jnp.exp(tbc - eb)).astype(bf16))

    y_spec = pl.BlockSpec((CHUNK, D_INNER), lambda c: (cidx(c), 0))
    extra_specs = [y_spec, pl.BlockSpec((1, D_INNER), lambda c: (0, 0))] if final else []
    return pl.pallas_call(
        body, name="ssd_fwd_rev" if reverse else "ssd_fwd", grid=(nc,),
        in_specs=_ssd_in_specs(cidx) + extra_specs,
        out_specs=[y_spec, pl.BlockSpec((1, N_SSD_GROUPS, D_STATE, GW), lambda c: (cidx(c), 0, 0, 0))],
        out_shape=[jax.ShapeDtypeStruct((s, D_INNER), bf16 if final else f32),
                   jax.ShapeDtypeStruct((nc, N_SSD_GROUPS, D_STATE, GW), f32)],
        scratch_shapes=[pltpu.VMEM((N_SSD_GROUPS, D_STATE, GW), f32), pltpu.VMEM((N_SSD_HEADS, CHUNK), f32),
                        pltpu.VMEM((CHUNK, GW), f32)],
        compiler_params=_cparams(dimension_semantics=("arbitrary",)),
    )(xbc, xbc, xbc, dtt, a_col, selc, selh, *((y_prev, dexp) if final else ()))


def _ssd_bwd(xbc, dtt, a_col, states, dy, reverse, dxbc_prev=None, dexp=None):
    s = xbc.shape[0]
    nc = s // CHUNK
    cidx = (lambda c: c) if reverse else (lambda c: nc - 1 - c)
    last = 0 if reverse else CHUNK - 1
    selc, selh = _ssd_consts()
    final = dxbc_prev is not None
    n_in = 11 if final else 9
    n_out = 4 if final else 3

    def body(*refs):
        xs_ref, b_ref, c_ref, dtt_ref, a_ref, selc_ref, selh_ref, st_ref, dy_ref = refs[:9]
        dxbc_ref, ddtt_ref, da_ref = refs[n_in:n_in + 3]
        dh_ref, et_ref, det_ref, det2_ref, ddt_ref, q_ref = refs[n_in + n_out:]
        if final:
            prev_ref, dexp_ref, ddexp_ref = refs[9], refs[10], refs[n_in + 3]

        @pl.when(pl.program_id(0) == 0)
        def _():
            dh_ref[...] = jnp.zeros_like(dh_ref)
            da_ref[...] = jnp.zeros_like(da_ref)
            if final:
                ddexp_ref[...] = jnp.zeros_like(ddexp_ref)

        mask, mask_t = _ssd_masks(reverse)
        dtt_v, et = _ssd_chunk_common(dtt_ref, a_ref, et_ref, mask_t)
        sel8 = selh_ref[0:HPG, :]
        is_last = lax.broadcasted_iota(jnp.int32, (CHUNK, GW), 0) == last
        for g in range(N_SSD_GROUPS):
            col, eb, dtb, tbc, xs, bg, cg = _ssd_group_common(g, dtt_v, et, selc_ref, selh_ref, xs_ref, b_ref, c_ref,
                                                              last)
            xd = xs * dtb
            cb = _nt(cg, bg)
            cbt = _nt(bg, cg)
            exp_t = jnp.exp(tbc)
            dfac = jnp.exp(tbc - eb)
            ht = st_ref[0, g]
            dhn = dh_ref[g]
            ht16, dhn16 = ht.astype(bf16), dhn.astype(bf16)
            dy = dy_ref[:, g * GW:(g + 1) * GW].astype(f32)
            dye = dy * jnp.exp(eb)
            dye16 = dye.astype(bf16)
            dc = _nt(dye16, ht16)
            dh_ref[g] = exp_t * dhn + _tn(cg, dye16)
            deb = dye * _nn(cg, ht16)
            xdd = xd * dfac
            dxdd = _nn(bg, dhn16)
            db = _nt(xdd.astype(bf16), dhn16)
            dxd_state = dxdd * dfac
            ddf = dxdd * xdd
            dtbc = jnp.sum(ddf, axis=0, keepdims=True) + exp_t * jnp.sum(dhn * ht, axis=0, keepdims=True)
            deb = deb - ddf + jnp.where(is_last, dtbc, 0.0)
            dcb = jnp.zeros((CHUNK, CHUNK), f32)
            dcbt = jnp.zeros((CHUNK, CHUNK), f32)
            for j in range(HPG):
                h = g * HPG + j
                hs = slice(j * SSD_HEAD_DIM, (j + 1) * SSD_HEAD_DIM)
                colj = col[:, j * CHUNK:(j + 1) * CHUNK]
                row = et_ref[h:h + 1, :]
                lam = jnp.exp(jnp.where(mask, colj - row, NEG))
                lam_t = lam.T
                xdj, dyj = xd[:, hs].astype(bf16), dy[:, hs].astype(bf16)
                t1 = _nt(dyj, xdj) * lam
                t2 = _nt(xdj, dyj) * lam_t
                dcb, dcbt = dcb + t1, dcbt + t2
                det_ref[h:h + 1, :] = -jnp.sum(t1 * cb - t2 * cbt, axis=0, keepdims=True)
                q_ref[:, hs] = _nn((cbt * lam_t).astype(bf16), dyj)
            x_cols = slice(g * GW, (g + 1) * GW)
            dxd = q_ref[...] + dxd_state
            dxs = dxd * dtb
            if final:
                dxs = dxs + prev_ref[:, x_cols] + dy * dexp_ref[:, x_cols]
            dxbc_ref[:, x_cols] = dxs
            b_cols = slice(D_INNER + g * D_STATE, D_INNER + (g + 1) * D_STATE)
            c_cols = slice(D_INNER + GN + g * D_STATE, D_INNER + GN + (g + 1) * D_STATE)
            db = db + _nn(dcbt.astype(bf16), cg)
            dc = dc + _nn(dcb.astype(bf16), bg)
            if final:
                db, dc = db + prev_ref[:, b_cols], dc + prev_ref[:, c_cols]
                ddexp_ref[:, g * GW:(g + 1) * GW] += jnp.sum(dy * xs, axis=0, keepdims=True)
            dxbc_ref[:, b_cols] = db
            dxbc_ref[:, c_cols] = dc
            det2_ref[g * HPG:(g + 1) * HPG, :] = _head_sum(sel8, deb)
            ddt_ref[g * HPG:(g + 1) * HPG, :] = _head_sum(sel8, dxd * xs)
        dat = jnp.dot(det_ref[...] + det2_ref[...], mask.astype(f32), precision=HIGHEST, preferred_element_type=f32)
        ddtt_ref[...] = ddt_ref[...] + dat * a_ref[...]
        da_ref[...] += jnp.sum(dat * dtt_v, axis=1, keepdims=True)

    in_specs = _ssd_in_specs(cidx) + [
        pl.BlockSpec((1, N_SSD_GROUPS, D_STATE, GW), lambda c: (cidx(c), 0, 0, 0)),
        pl.BlockSpec((CHUNK, D_INNER), lambda c: (cidx(c), 0))]
    hl = pltpu.VMEM((N_SSD_HEADS, CHUNK), f32)
    dxbc_spec = pl.BlockSpec((CHUNK, CONV_DIM), lambda c: (cidx(c), 0))
    dexp_spec = pl.BlockSpec((1, D_INNER), lambda c: (0, 0))
    return pl.pallas_call(
        body, name="ssd_bwd_rev" if reverse else "ssd_bwd", grid=(nc,),
        in_specs=in_specs + ([dxbc_spec, dexp_spec] if final else []),
        out_specs=[dxbc_spec, pl.BlockSpec((N_SSD_HEADS, CHUNK), lambda c: (0, cidx(c))),
                   pl.BlockSpec((N_SSD_HEADS, 1), lambda c: (0, 0))] + ([dexp_spec] if final else []),
        out_shape=[jax.ShapeDtypeStruct((s, CONV_DIM), f32), jax.ShapeDtypeStruct((N_SSD_HEADS, s), f32),
                   jax.ShapeDtypeStruct((N_SSD_HEADS, 1), f32)]
        + ([jax.ShapeDtypeStruct((1, D_INNER), f32)] if final else []),
        scratch_shapes=[pltpu.VMEM((N_SSD_GROUPS, D_STATE, GW), f32), hl, hl, hl, hl, pltpu.VMEM((CHUNK, GW), f32)],
        compiler_params=_cparams(dimension_semantics=("arbitrary",)),
    )(xbc, xbc, xbc, dtt, a_col, selc, selh, states, dy, *((dxbc_prev, dexp) if final else ()))


@jax.custom_vjp
def ssd_bidir(xbc, dtt, a_col, dexp):
    y_f, _ = _ssd_fwd(xbc, dtt[:N_SSD_HEADS], a_col[:N_SSD_HEADS], False)
    return _ssd_fwd(xbc, dtt[N_SSD_HEADS:], a_col[N_SSD_HEADS:], True, y_prev=y_f, dexp=dexp)[0]


def _ssd_bidir_fwd(xbc, dtt, a_col, dexp):
    y_f, st_f = _ssd_fwd(xbc, dtt[:N_SSD_HEADS], a_col[:N_SSD_HEADS], False)
    y, st_b = _ssd_fwd(xbc, dtt[N_SSD_HEADS:], a_col[N_SSD_HEADS:], True, y_prev=y_f, dexp=dexp)
    return y, (xbc, dtt, a_col, dexp, st_f, st_b)


def _ssd_bidir_bwd(res, dy):
    xbc, dtt, a_col, dexp, st_f, st_b = res
    dxbc_f, ddtt_f, da_f = _ssd_bwd(xbc, dtt[:N_SSD_HEADS], a_col[:N_SSD_HEADS], st_f, dy, False)
    dxbc, ddtt_b, da_b, ddexp = _ssd_bwd(xbc, dtt[N_SSD_HEADS:], a_col[N_SSD_HEADS:], st_b, dy, True,
                                         dxbc_prev=dxbc_f, dexp=dexp)
    return dxbc, jnp.concatenate([ddtt_f, ddtt_b], axis=0), jnp.concatenate([da_f, da_b], axis=0), ddexp


ssd_bidir.defvjp(_ssd_bidir_fwd, _ssd_bidir_bwd)


def _rope_tables(s):
    rows = s // GRID_W
    pos_row = np.repeat(np.arange(rows), GRID_W).astype(np.float32)
    pos_col = np.tile(np.arange(GRID_W), rows).astype(np.float32)
    axis_dim = HEAD_DIM // 2
    inv_freq = np.float32(ROPE_THETA) ** (-np.arange(0, axis_dim, 2, dtype=np.float32) / np.float32(axis_dim))
    ang_r = pos_row[:, None] * inv_freq[None, :].astype(np.float32)
    ang_c = pos_col[:, None] * inv_freq[None, :].astype(np.float32)
    cos = np.concatenate([np.cos(ang_r), np.cos(ang_r), np.cos(ang_c), np.cos(ang_c)] * 2, axis=-1)
    sin = np.concatenate([np.sin(ang_r), np.sin(ang_r), np.sin(ang_c), np.sin(ang_c)] * 2, axis=-1)
    return jnp.asarray(cos, f32), jnp.asarray(sin, f32)


def _rope_perm():
    p = np.zeros((HEAD_DIM, HEAD_DIM), np.float32)
    for j in range(HEAD_DIM):
        if (j % 32) < 16:
            p[j + 16, j] = -1.0
        else:
            p[j - 16, j] = 1.0
    return p


def local_loss(x, mod, small, recv_in_like, recv_late_like, wfull, late_shard, target):
    s = x.shape[0]
    lin = {n: make_linear("lin_" + n) for n in LATE if not n.startswith("mlp")}
    wfull, wgrads = dict(wfull), {}
    shift1, scale1, gate1, shift2, scale2, gate2 = [mod[i] for i in range(6)]

    norm_mod = make_rowwise("norm_mod", _fn_norm_mod, [(D_MODEL, bf16), (D_MODEL, f32)])
    (h, x_res), _ = norm_mod((x,), (small["norm1_w"], scale1, shift1), (), ())

    proj = dict(zip(PROJ_NAMES, in_proj(h, tuple(wfull[n] for n in PROJ_NAMES), recv_in_like)))

    cos, sin = _rope_tables(s)

    def heads(t, nh):
        return t.reshape(s, nh, HEAD_DIM).transpose(1, 0, 2)

    qr = make_head_rope("q_norm_rope", N_Q_HEADS, Q_SCALE, False)(proj["q"], small["q_norm_w"], cos, sin)
    kr = make_head_rope("k_norm_rope", N_KV_HEADS, 1.0, True)(proj["k"], small["k_norm_w"], cos, sin)
    vh = heads(proj["v"], N_KV_HEADS).astype(bf16)
    att = attention(qr, kr, vh)

    xbc, gathered, *carriers = conv_silu_comm(proj["xbc"], small["conv_w"], small["conv_b"], late_shard,
                                              recv_late_like)
    wfull.update(_split_late(gathered))
    wgrads.update(zip(LATE, carriers))
    ao = lin["attn_out"](att, wfull["attn_out"], wgrads["attn_out"])
    softplus = make_rowwise("dt_softplus", _fn_softplus, [(2 * N_SSD_HEADS, f32)])
    (dt,), _ = softplus((proj["dt"][:, :2 * N_SSD_HEADS],), (small["dt_bias"].reshape(1, 2 * N_SSD_HEADS),), (), ())
    a_neg = -jnp.exp(small["A_log"])
    dexp = jnp.repeat(small["ssd_D"].reshape(N_SSD_HEADS), SSD_HEAD_DIM).reshape(1, D_INNER)
    y = ssd_bidir(xbc, dt.T, a_neg.reshape(2 * N_SSD_HEADS, 1), dexp)
    ssd_gate = make_rowwise("ssd_gate", _fn_ssd_gate, [(D_INNER, bf16)])
    (ssd_out,), _ = ssd_gate((y, proj["z"]), (small["ssd_norm_w"],), (), ())
    so = lin["ssd_out"](ssd_out, wfull["ssd_out"], wgrads["ssd_out"])

    merge = make_rowwise("merge", _fn_merge, [(D_MODEL, bf16)])
    (merged,), _ = merge((ao, so, proj["ga"], proj["gs"]), (), (), ())
    mo = lin["o"](merged, wfull["o"], wgrads["o"])

    res_norm = make_rowwise("res_norm", _fn_res_norm, [(D_MODEL, f32), (D_MODEL, bf16)])
    (x1, h2), _ = res_norm((x_res, mo), (gate1, small["norm2_w"], scale2, shift2), (), ())
    ff = mlp(h2, wfull["mlp1"], wgrads["mlp1"], wfull["mlp2"], wgrads["mlp2"])
    loss_op = make_rowwise("loss", _fn_loss, [], [(1, 1)])
    _, (loss,) = loss_op((x1, ff), (gate2,), (), (target,))
    return loss[0, 0]


_BC1 = 1.0 - ADAM_B1 ** ADAM_STEP
_BC2 = 1.0 - ADAM_B2 ** ADAM_STEP


def _adamw(w, g, m, v):
    m = ADAM_B1 * m + (1.0 - ADAM_B1) * g
    v = ADAM_B2 * v + (1.0 - ADAM_B2) * (g * g)
    delta = -ADAM_LR * ((m / _BC1) / (jnp.sqrt(v / _BC2) + ADAM_EPS) + ADAM_WD * w)
    return delta, m, v


def _ada_fwd(c_all, w, b):
    n = w.shape[1]

    def body(c_ref, w_ref, b_ref, o_ref):
        o_ref[...] = jnp.dot(_silu(c_ref[...]), w_ref[...], precision=HIGHEST, preferred_element_type=f32) + b_ref[...]

    return pl.pallas_call(body, name="ada_fwd", out_shape=jax.ShapeDtypeStruct((N_DEV, n), f32),
                          compiler_params=_cparams())(c_all, w, b)


def _ada_bwd_adamw(c_all, dmod, w, m, v):
    d, n = w.shape
    tr = _pick(d, (256, 128))

    def body(c_ref, dm_ref, w_ref, m_ref, v_ref, g_ref, dl_ref, mo_ref, vo_ref):
        g = lax.dot_general(_silu(c_ref[...]), dm_ref[...], _DIMS["tn"], precision=HIGHEST,
                            preferred_element_type=f32)
        g_ref[...] = g
        dl_ref[...], mo_ref[...], vo_ref[...] = _adamw(w_ref[...], g, m_ref[...], v_ref[...])

    blk = pl.BlockSpec((tr, n), lambda i: (i, 0))
    return pl.pallas_call(
        body, name="ada_bwd_adamw", grid=(d // tr,),
        in_specs=[pl.BlockSpec((N_DEV, tr), lambda i: (0, i)), pl.BlockSpec((N_DEV, n), lambda i: (0, 0)), blk, blk, blk],
        out_specs=[blk] * 4, out_shape=[jax.ShapeDtypeStruct((d, n), f32)] * 4,
        compiler_params=_cparams(dimension_semantics=("parallel",)),
    )(c_all, dmod, w, m, v)


def _sum_over_mesh(g):
    def body(g_ref, o_ref):
        acc = g_ref[0]
        for d in range(1, N_DEV):
            acc = acc + g_ref[d]
        o_ref[...] = acc

    return pl.pallas_call(body, name="sum_small", out_shape=jax.ShapeDtypeStruct(g.shape[1:], f32),
                          compiler_params=_cparams())(g)


def _adamw_small(w, g, m, v):
    def body(w_ref, g_ref, m_ref, v_ref, dl_ref, mo_ref, vo_ref):
        dl_ref[...], mo_ref[...], vo_ref[...] = _adamw(w_ref[...], g_ref[...], m_ref[...], v_ref[...])

    return pl.pallas_call(body, name="adamw_small", out_shape=[jax.ShapeDtypeStruct(w.shape, f32)] * 3,
                          compiler_params=_cparams())(w, g, m, v)


def _sum_adamw(recv, w, m, v, name):
    _, r, c = recv.shape
    tr = _pick(r, (256, 128, 64, 16))

    def body(g_ref, w_ref, m_ref, v_ref, go_ref, dl_ref, mo_ref, vo_ref):
        g = g_ref[0].astype(f32)
        for d in range(1, N_DEV):
            g = g + g_ref[d].astype(f32)
        go_ref[...] = g
        dl_ref[...], mo_ref[...], vo_ref[...] = _adamw(w_ref[...], g, m_ref[...], v_ref[...])

    blk = pl.BlockSpec((tr, c), lambda i: (i, 0))
    return pl.pallas_call(
        body, name=name, grid=(r // tr,),
        in_specs=[pl.BlockSpec((N_DEV, tr, c), lambda i: (0, i, 0)), blk, blk, blk],
        out_specs=[blk] * 4, out_shape=[jax.ShapeDtypeStruct((r, c), f32)] * 4,
        compiler_params=_cparams(dimension_semantics=("parallel",)),
    )(recv, w, m, v)


def _pack_small(arrs):
    parts = []
    for a in arrs:
        flat = a.reshape(-1).astype(f32)
        parts.append(jnp.pad(flat, (0, (-flat.shape[0]) % LANE)))
    flat = jnp.concatenate(parts)
    flat = jnp.pad(flat, (0, (-flat.shape[0]) % (8 * LANE)))
    return flat.reshape(-1, LANE)


def _unpack_small(packed, shapes):
    flat = packed.reshape(-1)
    out, off = [], 0
    for shp in shapes:
        n = int(np.prod(shp))
        out.append(flat[off:off + n].reshape(shp))
        off += n + (-n) % LANE
    return out


BIG = ("w_attn_out", "w_ssd_out", "w_o", "w_mlp1", "w_mlp2")
BIG_ROWS = (N_Q_HEADS * HEAD_DIM // N_DEV, D_INNER // N_DEV, D_MODEL // N_DEV,
            D_MODEL * (D_FF // N_DEV) // PACK_COLS, D_FF // N_DEV)
N_IN_SHARD = D_IN_PROJ // N_DEV
assert sum(BIG_ROWS) % 16 == 0


def _pack_big(shards, dtype):
    return jnp.concatenate([s.astype(dtype).reshape(-1, PACK_COLS) for s in shards], axis=0)


def _unpack_big(packed, shapes):
    out, off = [], 0
    for rows, shp in zip(BIG_ROWS, shapes):
        out.append(packed[off:off + rows].reshape(shp))
        off += rows
    return out


LATE = ("attn_out", "ssd_out", "o", "mlp1", "mlp2")
LATE_SHAPES = ((N_Q_HEADS * HEAD_DIM, D_MODEL), (D_INNER, D_MODEL), (D_MODEL, D_MODEL), (D_MODEL, D_FF),
               (D_FF, D_MODEL))


def _split_w_in(g_in):
    w_in = g_in.transpose(1, 0, 2).reshape(D_MODEL, D_IN_PROJ)
    w = {}
    off = 0
    for name, size in zip(PROJ_NAMES, PROJ_SIZES):
        w[name] = w_in[:, off:off + size]
        off += size
    w["dt"] = jnp.pad(w["dt"], ((0, 0), (0, DT_PAD - 2 * N_SSD_HEADS)))
    return w


def _split_late(g):
    offs = np.cumsum((0,) + BIG_ROWS)
    sl = [g[:, offs[i]:offs[i + 1]] for i in range(len(BIG))]
    return {"attn_out": sl[0].reshape(LATE_SHAPES[0]), "ssd_out": sl[1].reshape(LATE_SHAPES[1]),
            "o": sl[2].reshape(LATE_SHAPES[2]),
            "mlp1": sl[3].reshape(N_DEV, D_MODEL, D_FF // N_DEV).transpose(1, 0, 2).reshape(LATE_SHAPES[3]),
            "mlp2": sl[4].reshape(LATE_SHAPES[4])}


def _pack_in_grads(gw):
    gw = {n: g.astype(bf16) for n, g in gw.items()}
    gw["dt"] = gw["dt"][:, :2 * N_SSD_HEADS]
    g_in = jnp.concatenate([gw[n] for n in PROJ_NAMES], axis=1)
    return g_in.reshape(D_MODEL, N_DEV, N_IN_SHARD).transpose(1, 0, 2)


def _pack_late_grads(gw):
    gw = {n: g.astype(bf16) for n, g in gw.items()}
    parts = [
        gw["attn_out"].reshape(N_DEV, -1, PACK_COLS),
        gw["ssd_out"].reshape(N_DEV, -1, PACK_COLS),
        gw["o"].reshape(N_DEV, -1, PACK_COLS),
        gw["mlp1"].reshape(D_MODEL, N_DEV, D_FF // N_DEV).transpose(1, 0, 2).reshape(N_DEV, -1, PACK_COLS),
        gw["mlp2"].reshape(N_DEV, -1, PACK_COLS),
    ]
    return jnp.concatenate(parts, axis=1)


SMALL = ("norm1_w", "norm2_w", "q_norm_w", "k_norm_w", "conv_w", "conv_b", "A_log", "dt_bias", "ssd_D", "ssd_norm_w")


def kernel(x, c, w_ada, b_ada, norm1_w, norm2_w, w_in, q_norm_w, k_norm_w, conv_w, conv_b, A_log, dt_bias, ssd_D, ssd_norm_w, w_attn_out, w_ssd_out, w_o, w_mlp1, w_mlp2, loss_target, m_w_ada, m_b_ada, m_norm1_w, m_norm2_w, m_w_in, m_q_norm_w, m_k_norm_w, m_conv_w, m_conv_b, m_A_log, m_dt_bias, m_ssd_D, m_ssd_norm_w, m_w_attn_out, m_w_ssd_out, m_w_o, m_w_mlp1, m_w_mlp2, v_w_ada, v_b_ada, v_norm1_w, v_norm2_w, v_w_in, v_q_norm_w, v_k_norm_w, v_conv_w, v_conv_b, v_A_log, v_dt_bias, v_ssd_D, v_ssd_norm_w, v_w_attn_out, v_w_ssd_out, v_w_o, v_w_mlp1, v_w_mlp2):
    args = dict(locals())
    me = _my_index()
    n_ada = 6 * D_MODEL // N_DEV
    n_cw = CONV_DIM // N_DEV

    blk = jnp.zeros((8, D_MODEL), f32)
    blk = blk.at[0:1, :].set(c)
    blk = blk.at[1:1 + D_CONV, :n_cw].set(conv_w[0])
    g0 = _all_gather(blk, "gather_c_convw", in_vmem=True)
    c_all = g0[:, 0, :]
    conv_w_full = g0[:, 1:1 + D_CONV, :n_cw].transpose(1, 0, 2).reshape(D_CONV, CONV_DIM)

    b_shard = lax.dynamic_slice(b_ada, (0, me * n_ada), (1, n_ada))
    mod_cols = _ada_fwd(c_all, w_ada[0], b_shard)
    g1 = _all_gather(mod_cols, "gather_mod", in_vmem=True)
    mod_mine = lax.dynamic_index_in_dim(g1, me, axis=1, keepdims=False)
    mod = mod_mine.reshape(6, 1, D_MODEL)

    big_shapes = [args[n].shape[1:] for n in BIG]
    late_shard = _pack_big([args[n][0] for n in BIG], bf16)
    wfull = _split_w_in(_all_gather(w_in[0].astype(bf16), "gather_w_in", in_vmem=False))
    recv_in_like = jnp.zeros((N_DEV,) + w_in.shape[1:], bf16)
    recv_late_like = jnp.zeros((N_DEV,) + late_shard.shape, bf16)

    small = {"norm1_w": norm1_w, "norm2_w": norm2_w, "q_norm_w": q_norm_w, "k_norm_w": k_norm_w,
             "conv_w": conv_w_full, "conv_b": conv_b, "A_log": A_log[0], "dt_bias": dt_bias[0], "ssd_D": ssd_D,
             "ssd_norm_w": ssd_norm_w}

    loss, (gx, gmod, gsmall, recv_in, recv_late) = jax.value_and_grad(local_loss, argnums=(0, 1, 2, 3, 4))(
        x[0], mod, small, recv_in_like, recv_late_like, wfull, late_shard, loss_target[0])

    small_list = [gmod, gsmall["norm1_w"], gsmall["norm2_w"], gsmall["q_norm_w"], gsmall["k_norm_w"], gsmall["conv_w"],
                  gsmall["conv_b"], gsmall["A_log"], gsmall["dt_bias"], gsmall["ssd_D"], gsmall["ssd_norm_w"],
                  loss.reshape(1)]
    small_shapes = [a.shape for a in small_list]
    g2 = _all_gather(_pack_small(small_list), "gather_small_grads", in_vmem=True)
    summed = _unpack_small(_sum_over_mesh(g2), small_shapes)
    loss_total = summed[-1][0]
    g_b_ada = summed[0].reshape(1, 6 * D_MODEL)
    g_small = dict(zip(SMALL, summed[1:-1]))
    g_conv_w = lax.dynamic_slice(g_small["conv_w"], (0, me * n_cw), (D_CONV, n_cw))

    dmod_all = g2[:, :6 * D_MODEL // LANE, :].reshape(N_DEV, 6 * D_MODEL)
    dmod_shard = lax.dynamic_slice(dmod_all, (0, me * n_ada), (N_DEV, n_ada))
    ada = _ada_bwd_adamw(c_all, dmod_shard, w_ada[0], m_w_ada[0], v_w_ada[0])

    small_grads = {"b_ada": g_b_ada, "norm1_w": g_small["norm1_w"], "norm2_w": g_small["norm2_w"],
                   "q_norm_w": g_small["q_norm_w"], "k_norm_w": g_small["k_norm_w"], "conv_w": g_conv_w[None],
                   "conv_b": g_small["conv_b"], "A_log": g_small["A_log"][None], "dt_bias": g_small["dt_bias"][None],
                   "ssd_D": g_small["ssd_D"], "ssd_norm_w": g_small["ssd_norm_w"]}
    sm_names = list(small_grads)
    sm_shapes = [args[n].shape for n in sm_names]
    sm = _adamw_small(_pack_small([args[n] for n in sm_names]), _pack_small([small_grads[n] for n in sm_names]),
                      _pack_small([args["m_" + n] for n in sm_names]), _pack_small([args["v_" + n] for n in sm_names]))
    sm_delta, sm_m, sm_v = [dict(zip(sm_names, _unpack_small(t, sm_shapes))) for t in sm]
    small_grads = {n: small_grads[n].reshape(args[n].shape) for n in sm_names}

    w_in_out = _sum_adamw(recv_in, w_in[0], m_w_in[0], v_w_in[0], "sum_adamw_w_in")
    big = _sum_adamw(recv_late, _pack_big([args[n][0] for n in BIG], f32),
                     _pack_big([args["m_" + n][0] for n in BIG], f32),
                     _pack_big([args["v_" + n][0] for n in BIG], f32), "sum_adamw")
    big_g, big_delta, big_m, big_v = [dict(zip(BIG, [t[None] for t in _unpack_big(p, big_shapes)])) for p in big]
    big_g["w_in"], big_delta["w_in"], big_m["w_in"], big_v["w_in"] = [t[None] for t in w_in_out]

    names = ("w_ada", "b_ada", "norm1_w", "norm2_w", "w_in", "q_norm_w", "k_norm_w", "conv_w", "conv_b", "A_log",
             "dt_bias", "ssd_D", "ssd_norm_w", "w_attn_out", "w_ssd_out", "w_o", "w_mlp1", "w_mlp2")
    grads, deltas, new_m, new_v = {}, {}, {}, {}
    for n in names:
        if n == "w_ada":
            grads[n], deltas[n], new_m[n], new_v[n] = [t[None] for t in ada]
        elif n in big_g:
            grads[n], deltas[n], new_m[n], new_v[n] = big_g[n], big_delta[n], big_m[n], big_v[n]
        else:
            grads[n], deltas[n], new_m[n], new_v[n] = small_grads[n], sm_delta[n], sm_m[n], sm_v[n]
    return (loss_total, gx[None], *[grads[n] for n in names], *[deltas[n] for n in names],
            *[new_m[n] for n in names], *[new_v[n] for n in names])
```

```python
import functools
import math

import jax
import jax.numpy as jnp
import numpy as np
from jax import lax
from jax.experimental import pallas as pl
from jax.experimental.pallas import tpu as pltpu

f32 = jnp.float32
bf16 = jnp.bfloat16
HIGHEST = lax.Precision.HIGHEST
MESH = pl.DeviceIdType.MESH

N_DEV = 8
D_MODEL = 1024
GRID_W = 64
N_Q_HEADS = 16
N_KV_HEADS = 4
HEAD_DIM = 64
ROPE_THETA = 10000.0
D_INNER = 2048
SSD_HEAD_DIM = 64
N_SSD_HEADS = 32
N_SSD_GROUPS = 4
D_STATE = 128
D_CONV = 5
CHUNK = 128
D_FF = 4096
EPS = 1e-6
CONV_DIM = D_INNER + 2 * N_SSD_GROUPS * D_STATE
GN = N_SSD_GROUPS * D_STATE
PROJ_NAMES = ("q", "k", "v", "xbc", "z", "dt", "ga", "gs")
PROJ_SIZES = (N_Q_HEADS * HEAD_DIM, N_KV_HEADS * HEAD_DIM, N_KV_HEADS * HEAD_DIM, CONV_DIM, D_INNER,
              2 * N_SSD_HEADS, D_MODEL, D_MODEL)
D_IN_PROJ = sum(PROJ_SIZES)
PROJ_DTYPES = (jnp.bfloat16, jnp.bfloat16, jnp.bfloat16, jnp.float32, jnp.bfloat16, jnp.float32, jnp.bfloat16,
               jnp.bfloat16)
DT_PAD = 128

ADAM_LR, ADAM_B1, ADAM_B2, ADAM_EPS, ADAM_WD, ADAM_STEP = 0.001, 0.9, 0.999, 1e-08, 0.01, 10

V7X_VMEM_LIMIT = 56 * 1024 * 1024
LANE = 128
PACK_COLS = 1024


def _cparams(**kw):
    return pltpu.CompilerParams(vmem_limit_bytes=V7X_VMEM_LIMIT, **kw)


def _pick(dim, prefs):
    for p in prefs:
        if dim % p == 0:
            return p
    return dim


def _my_index():
    return 4 * lax.axis_index("x") + 2 * lax.axis_index("y") + lax.axis_index("c")


COMM_SEMS = [pltpu.SemaphoreType.DMA((7,)), pltpu.SemaphoreType.DMA((7,)), pltpu.SemaphoreType.DMA]


def _gather_phases(x_ref, out_ref, send_sems, recv_sems, local_sem):
    x, y, cc = lax.axis_index("x"), lax.axis_index("y"), lax.axis_index("c")
    me, sibling = (x, y, cc), (x, y, 1 - cc)
    chips = [(1 - x, y), (x, 1 - y), (1 - x, 1 - y)]

    def slot(px, py, pc):
        return out_ref.at[4 * px + 2 * py + pc]

    def copy(k, blk, to, src=None):
        return pltpu.make_async_remote_copy(
            src_ref=slot(*blk) if src is None else src, dst_ref=slot(*blk),
            send_sem=send_sems.at[k], recv_sem=recv_sems.at[k], device_id=to, device_id_type=MESH)

    mine = pltpu.make_async_copy(x_ref, slot(*me), local_sem)
    first = [copy(0, me, sibling, src=x_ref)]
    first += [copy(1 + j, me, (*chip, cc), src=x_ref) for j, chip in enumerate(chips)]
    passed = [copy(4 + j, (*chip, cc), sibling) for j, chip in enumerate(chips)]

    def start():
        mine.start()
        for cp in first:
            cp.start()

    def finish():
        for j, chip in enumerate(chips):
            copy(1 + j, (*chip, cc), me).wait_recv()
            passed[j].start()
        copy(0, sibling, me).wait_recv()
        for j, chip in enumerate(chips):
            copy(4 + j, (*chip, 1 - cc), me).wait_recv()
        for cp in first + passed:
            cp.wait_send()
        mine.wait()

    return start, finish


def _scatter_phases(g_ref, out_ref, send_sems, recv_sems, local_sem):
    x, y, cc = lax.axis_index("x"), lax.axis_index("y"), lax.axis_index("c")
    me = 4 * x + 2 * y + cc
    mine = pltpu.make_async_copy(g_ref.at[me], out_ref.at[me], local_sem)

    def copy(k):
        fx, fy, fc = (k >> 2) & 1, (k >> 1) & 1, k & 1
        px = x + fx - 2 * x * fx
        py = y + fy - 2 * y * fy
        pc = cc + fc - 2 * cc * fc
        peer = 4 * px + 2 * py + pc
        send = pltpu.make_async_remote_copy(
            src_ref=g_ref.at[peer], dst_ref=out_ref.at[me],
            send_sem=send_sems.at[k - 1], recv_sem=recv_sems.at[k - 1],
            device_id=(px, py, pc), device_id_type=MESH)
        recv = pltpu.make_async_remote_copy(
            src_ref=g_ref.at[peer], dst_ref=out_ref.at[peer],
            send_sem=send_sems.at[k - 1], recv_sem=recv_sems.at[k - 1],
            device_id=(px, py, pc), device_id_type=MESH)
        return send, recv

    pairs = [copy(k) for k in range(1, N_DEV)]

    def start():
        mine.start()
        for send, _ in pairs:
            send.start()

    def finish():
        for _, recv in pairs:
            recv.wait_recv()
        for send, _ in pairs:
            send.wait_send()
        mine.wait()

    return start, finish


def _all_gather(block, name, in_vmem):
    r, c = block.shape

    def body(x_ref, out_ref, send_sems, recv_sems, local_sem):
        start, finish = _gather_phases(x_ref, out_ref, send_sems, recv_sems, local_sem)
        start()
        finish()

    space = pltpu.VMEM if in_vmem else pl.ANY
    return pl.pallas_call(
        body, name=name,
        out_shape=jax.ShapeDtypeStruct((N_DEV, r, c), block.dtype),
        in_specs=[pl.BlockSpec(memory_space=space)],
        out_specs=pl.BlockSpec(memory_space=space),
        scratch_shapes=[pltpu.SemaphoreType.DMA((7,)), pltpu.SemaphoreType.DMA((7,)), pltpu.SemaphoreType.DMA],
    )(block)


def _scatter_blocks(g, name):
    _, r, c = g.shape

    def body(g_ref, out_ref, send_sems, recv_sems, local_sem):
        start, finish = _scatter_phases(g_ref, out_ref, send_sems, recv_sems, local_sem)
        start()
        finish()

    return pl.pallas_call(
        body, name=name,
        out_shape=jax.ShapeDtypeStruct(g.shape, g.dtype),
        in_specs=[pl.BlockSpec(memory_space=pl.ANY)],
        out_specs=pl.BlockSpec(memory_space=pl.ANY),
        scratch_shapes=[pltpu.SemaphoreType.DMA((7,)), pltpu.SemaphoreType.DMA((7,)), pltpu.SemaphoreType.DMA],
    )(g)


_DIMS = {"nn": (((1,), (0,)), ((), ())), "nt": (((1,), (1,)), ((), ())), "tn": (((0,), (0,)), ((), ()))}


def _matmul(a, b, mode, out_dtype, name, epilogue=None, side=None):
    if mode == "nn":
        (m, k), (_, n) = a.shape, b.shape
    elif mode == "nt":
        (m, k), (n, _) = a.shape, b.shape
    else:
        (k, m), (_, n) = a.shape, b.shape
    tm = _pick(m, (1024, 512, 256, 128))
    if mode == "tn":
        tn = _pick(n, (1536, 1024, 512, 256, 128))
        tk = _pick(k, (2048, 1024, 512, 256, 128)) if b.dtype == bf16 else _pick(k, (1024, 512, 256, 128))
    else:
        tn = _pick(n, (1024, 512, 384, 256, 128))
        tk = _pick(k, (2048, 1024, 512, 256, 128)) if a.dtype == bf16 else _pick(k, (1024, 512, 256, 128))
    nk = k // tk
    dims = _DIMS[mode]
    n_in = 3 if epilogue == "drelu2" else 2

    def body(*refs):
        a_ref, b_ref = refs[:2]
        o_ref, acc_ref = refs[n_in], refs[n_in + 1]
        kk = pl.program_id(2)
        part = lax.dot_general(a_ref[...].astype(bf16), b_ref[...].astype(bf16), dims, preferred_element_type=f32)

        def finish(acc):
            if epilogue == "relu2":
                r = jnp.maximum(acc, 0.0)
                o_ref[...] = (r * r).astype(out_dtype)
            elif epilogue == "drelu2":
                o_ref[...] = (acc * (2.0 * jnp.sqrt(refs[2][...].astype(f32)))).astype(out_dtype)
            else:
                o_ref[...] = acc.astype(out_dtype)

        if nk == 1:
            finish(part)
        else:
            @pl.when(kk == 0)
            def _():
                acc_ref[...] = part

            @pl.when(kk > 0)
            def _():
                acc_ref[...] += part

            @pl.when(kk == nk - 1)
            def _():
                finish(acc_ref[...])

    if mode == "tn":
        a_spec = pl.BlockSpec((tk, tm), lambda i, j, kk: (kk, i))
    else:
        a_spec = pl.BlockSpec((tm, tk), lambda i, j, kk: (i, kk))
    if mode == "nt":
        b_spec = pl.BlockSpec((tn, tk), lambda i, j, kk: (j, kk))
    else:
        b_spec = pl.BlockSpec((tk, tn), lambda i, j, kk: (kk, j))
    o_spec = pl.BlockSpec((tm, tn), lambda i, j, kk: (i, j))
    o_shape = jax.ShapeDtypeStruct((m, n), out_dtype)
    return pl.pallas_call(
        body, name=name, grid=(m // tm, n // tn, nk),
        in_specs=[a_spec, b_spec] + ([o_spec] if epilogue == "drelu2" else []),
        out_specs=o_spec, out_shape=o_shape,
        scratch_shapes=[pltpu.VMEM((tm, tn), f32)],
        compiler_params=_cparams(dimension_semantics=("parallel", "parallel", "arbitrary")),
    )(*((a, b, side) if epilogue == "drelu2" else (a, b)))


@jax.custom_vjp
def mlp(h, w1, w1grad, w2, w2grad):
    r = _matmul(h, w1, "nn", bf16, "mlp1_fwd", epilogue="relu2")
    return _matmul(r, w2, "nn", f32, "mlp2_fwd")


def _mlp_fwd(h, w1, w1grad, w2, w2grad):
    r = _matmul(h, w1, "nn", bf16, "mlp1_fwd", epilogue="relu2")
    return _matmul(r, w2, "nn", f32, "mlp2_fwd"), (h, w1, w2, r)


def _mlp_bwd(res, dy):
    h, w1, w2, r = res
    du = _matmul(dy, w2, "nt", bf16, "mlp2_dgrad", epilogue="drelu2", side=r)
    dw2 = _matmul(r, dy, "tn", f32, "mlp2_wgrad")
    dh = _matmul(du, w1, "nt", h.dtype, "mlp1_dgrad")
    dw1 = _matmul(h, du, "tn", f32, "mlp1_wgrad")
    return dh, jnp.zeros_like(w1), dw1, jnp.zeros_like(w2), dw2


mlp.defvjp(_mlp_fwd, _mlp_bwd)


def make_linear(name):
    @jax.custom_vjp
    def linear(a, w, wgrad):
        return _matmul(a, w, "nn", f32, name + "_fwd")

    def fwd(a, w, wgrad):
        return linear(a, w, wgrad), (a, w)

    def bwd(res, dy):
        a, w = res
        da = _matmul(dy, w, "nt", a.dtype, name + "_dgrad")
        dw = _matmul(a, dy, "tn", f32, name + "_wgrad")
        return da, jnp.zeros_like(w), dw

    linear.defvjp(fwd, bwd)
    return linear


def _in_proj_dgrad(dys, ws, g):
    s, d = dys[0].shape[0], ws[0].shape[0]
    tm = _pick(s, (1024, 512, 256, 128))
    tks = [w.shape[1] if w.shape[1] <= 1024 else 512 for w in ws]
    steps = [w.shape[1] // tk for w, tk in zip(ws, tks)]
    starts = [sum(steps[:p]) for p in range(len(ws))]
    total = sum(steps)
    n_p, n_i = len(ws), s // tm
    assert steps[0] == 1

    def body(*refs):
        dy_refs, w_refs, g_ref = refs[:n_p], refs[n_p:2 * n_p], refs[2 * n_p]
        dh_ref, recv_ref, acc_ref, send_sems, recv_sems, local_sem = refs[2 * n_p + 1:]
        i, t = pl.program_id(0), pl.program_id(1)
        start, finish = _scatter_phases(g_ref, recv_ref, send_sems, recv_sems, local_sem)

        @pl.when((i == 0) & (t == 0))
        def _():
            start()

        for p in range(n_p):
            @pl.when((t >= starts[p]) & (t < starts[p] + steps[p]))
            def _(p=p):
                part = lax.dot_general(dy_refs[p][...].astype(bf16), w_refs[p][...], _DIMS["nt"],
                                       preferred_element_type=f32)
                if p == 0:
                    acc_ref[...] = part
                else:
                    acc_ref[...] += part

        @pl.when(t == total - 1)
        def _():
            dh_ref[...] = acc_ref[...].astype(dh_ref.dtype)

        @pl.when((i == n_i - 1) & (t == total - 1))
        def _():
            finish()

    def piece_map(p, rows):
        def index_map(i, t):
            blk = jnp.clip(t - starts[p], 0, steps[p] - 1)
            return (i, blk) if rows else (0, blk)

        return index_map

    hbm = pl.BlockSpec(memory_space=pl.ANY)
    in_specs = [pl.BlockSpec((tm, tks[p]), piece_map(p, True)) for p in range(n_p)]
    in_specs += [pl.BlockSpec((d, tks[p]), piece_map(p, False)) for p in range(n_p)]
    return pl.pallas_call(
        body, name="in_proj_dgrad", grid=(n_i, total), in_specs=in_specs + [hbm],
        out_specs=[pl.BlockSpec((tm, d), lambda i, t: (i, 0)), hbm],
        out_shape=[jax.ShapeDtypeStruct((s, d), bf16), jax.ShapeDtypeStruct(g.shape, g.dtype)],
        scratch_shapes=[pltpu.VMEM((tm, d), f32)] + COMM_SEMS,
        compiler_params=_cparams(dimension_semantics=("arbitrary", "arbitrary")),
    )(*dys, *ws, g)


@jax.custom_vjp
def in_proj(h, ws, recv_like):
    return tuple(_matmul(h, w, "nn", dt, "lin_" + n + "_fwd") for n, w, dt in zip(PROJ_NAMES, ws, PROJ_DTYPES))


def _in_proj_fwd(h, ws, recv_like):
    return in_proj(h, ws, recv_like), (h, ws)


def _in_proj_bwd(res, dys):
    h, ws = res
    dws = {n: _matmul(h, dy, "tn", f32, "lin_" + n + "_wgrad") for n, dy in zip(PROJ_NAMES, dys)}
    dh, recv = _in_proj_dgrad(dys, ws, _pack_in_grads(dws))
    return dh.astype(h.dtype), tuple(jnp.zeros_like(w) for w in ws), recv


in_proj.defvjp(_in_proj_fwd, _in_proj_bwd)


def make_rowwise(name, fn, row_out, sum_out=(), tm_pref=512):
    def specs(rows, gpars, cpars, consts, tm):
        s = [pl.BlockSpec((tm, r.shape[1]), lambda i: (i, 0)) for r in rows]
        s += [pl.BlockSpec(p.shape, lambda i: (0, 0)) for p in gpars]
        s += [pl.BlockSpec(p.shape, lambda i: (0, 0)) for p in cpars]
        for cst in consts:
            nb = cst.shape[0] // tm
            s.append(pl.BlockSpec((tm, cst.shape[1]), lambda i, nb=nb: (i % nb, 0)))
        return s

    def tile_rows(rows, consts):
        r = rows[0].shape[0]
        common = math.gcd(r, *[cst.shape[0] for cst in consts])
        tm = _pick(common, (tm_pref, 512, 256, 128, 64, 32, 16, 8))
        return r, tm

    def forward(rows, gpars, cpars, consts):
        r, tm = tile_rows(rows, consts)
        nr, ng, nc, nk = len(rows), len(gpars), len(cpars), len(consts)

        def body(*refs):
            ins = refs[:nr + ng + nc + nk]
            outs = refs[nr + ng + nc + nk:]
            rv = [t[...].astype(f32) for t in ins[:nr]]
            gv = [t[...].astype(f32) for t in ins[nr:nr + ng]]
            cv = [t[...] for t in ins[nr + ng:nr + ng + nc]]
            kv = [t[...].astype(f32) for t in ins[nr + ng + nc:]]
            ro, so = fn(rv, gv, cv, kv)
            for o_ref, val in zip(outs[:len(row_out)], ro):
                o_ref[...] = val.astype(o_ref.dtype)
            if sum_out:
                @pl.when(pl.program_id(0) == 0)
                def _():
                    for o_ref in outs[len(row_out):]:
                        o_ref[...] = jnp.zeros_like(o_ref)
                for o_ref, val in zip(outs[len(row_out):], so):
                    o_ref[...] += val

        out_specs = [pl.BlockSpec((tm, w), lambda i: (i, 0)) for w, _ in row_out]
        out_specs += [pl.BlockSpec(shp, lambda i: (0, 0)) for shp in sum_out]
        out_shape = [jax.ShapeDtypeStruct((r, w), dt) for w, dt in row_out]
        out_shape += [jax.ShapeDtypeStruct(shp, f32) for shp in sum_out]
        res = pl.pallas_call(
            body, name=name + "_fwd", grid=(r // tm,),
            in_specs=specs(rows, gpars, cpars, consts, tm), out_specs=out_specs, out_shape=out_shape,
            compiler_params=_cparams(dimension_semantics=("arbitrary",)),
        )(*rows, *gpars, *cpars, *consts)
        return tuple(res[:len(row_out)]), tuple(res[len(row_out):])

    def backward(rows, gpars, cpars, consts, d_ro, d_so):
        r, tm = tile_rows(rows, consts)
        nr, ng, nc, nk = len(rows), len(gpars), len(cpars), len(consts)
        n_in = nr + ng + nc + nk + len(row_out) + len(sum_out)

        def body(*refs):
            ins, outs = refs[:n_in], refs[n_in:]
            rv = [t[...].astype(f32) for t in ins[:nr]]
            gv = [t[...].astype(f32) for t in ins[nr:nr + ng]]
            cv = [t[...] for t in ins[nr + ng:nr + ng + nc]]
            kv = [t[...].astype(f32) for t in ins[nr + ng + nc:nr + ng + nc + nk]]
            o = nr + ng + nc + nk
            dro = [t[...].astype(f32) for t in ins[o:o + len(row_out)]]
            dso = [t[...] for t in ins[o + len(row_out):]]
            _, vjp = jax.vjp(lambda a, b: tuple(tuple(t) for t in fn(a, b, cv, kv)), rv, gv)
            drv, dgv = vjp((tuple(dro), tuple(dso)))
            for o_ref, val in zip(outs[:nr], drv):
                o_ref[...] = val.astype(o_ref.dtype)
            if ng:
                @pl.when(pl.program_id(0) == 0)
                def _():
                    for o_ref in outs[nr:]:
                        o_ref[...] = jnp.zeros_like(o_ref)
                for o_ref, val in zip(outs[nr:], dgv):
                    o_ref[...] += val

        in_specs = specs(rows, gpars, cpars, consts, tm)
        in_specs += [pl.BlockSpec((tm, w), lambda i: (i, 0)) for w, _ in row_out]
        in_specs += [pl.BlockSpec(shp, lambda i: (0, 0)) for shp in sum_out]
        out_specs = [pl.BlockSpec((tm, t.shape[1]), lambda i: (i, 0)) for t in rows]
        out_specs += [pl.BlockSpec(p.shape, lambda i: (0, 0)) for p in gpars]
        out_shape = [jax.ShapeDtypeStruct(t.shape, t.dtype) for t in rows]
        out_shape += [jax.ShapeDtypeStruct(p.shape, f32) for p in gpars]
        res = pl.pallas_call(
            body, name=name + "_bwd", grid=(r // tm,),
            in_specs=in_specs, out_specs=out_specs, out_shape=out_shape,
            compiler_params=_cparams(dimension_semantics=("arbitrary",)),
        )(*rows, *gpars, *cpars, *consts, *d_ro, *d_so)
        return tuple(res[:nr]), tuple(res[nr:])

    @jax.custom_vjp
    def op(rows, gpars, cpars, consts):
        return forward(rows, gpars, cpars, consts)

    def op_fwd(rows, gpars, cpars, consts):
        return forward(rows, gpars, cpars, consts), (rows, gpars, cpars, consts)

    def op_bwd(res, cts):
        rows, gpars, cpars, consts = res
        d_ro, d_so = cts
        drows, dg = backward(rows, gpars, cpars, consts, d_ro, d_so)
        dg = tuple(d.astype(p.dtype) for d, p in zip(dg, gpars))
        return (drows, dg, tuple(jnp.zeros_like(p) for p in cpars), tuple(jnp.zeros_like(k) for k in consts))

    op.defvjp(op_fwd, op_bwd)
    return op


def _rms(x):
    return x * lax.rsqrt(jnp.mean(x * x, axis=-1, keepdims=True) + EPS)


def _silu(x):
    return x * jax.nn.sigmoid(x)


def _fn_norm_mod(rows, gp, cp, ks):
    (x,), (nw, sc, sh) = rows, gp
    return ((_rms(x) * nw) * (1.0 + sc) + sh, x), ()


PAIR = 2 * HEAD_DIM


def _exact_dot(a, m):
    hi = a.astype(bf16)
    lo = (a - hi.astype(f32)).astype(bf16)
    return jnp.dot(hi, m, preferred_element_type=f32) + jnp.dot(lo, m, preferred_element_type=f32)


def _make_sel_dot(sign):
    @jax.custom_vjp
    def sel_dot(a, m):
        return _exact_dot(a, m)

    def fwd(a, m):
        return _exact_dot(a, m), m

    def bwd(m, g):
        return sign * _exact_dot(g, m), jnp.zeros_like(m)

    sel_dot.defvjp(fwd, bwd)
    return sel_dot


_head_sum_dot = _make_sel_dot(1.0)
_rope_perm_dot = _make_sel_dot(-1.0)


def _pair_norm_rope(t, w2, gsum, perm, cos2, sin2, out_scale):
    ss = _head_sum_dot(t * t, gsum)
    u = t * lax.rsqrt(ss * (1.0 / HEAD_DIM) + EPS) * w2
    return (u * cos2 + _rope_perm_dot(u, perm) * sin2) * out_scale


def _pair_consts():
    eye = np.eye(2, dtype=np.float32)
    gsum = np.kron(eye, np.ones((HEAD_DIM, HEAD_DIM), np.float32))
    return jnp.asarray(gsum, bf16), jnp.asarray(np.kron(eye, _rope_perm()), bf16)


def make_head_rope(name, nh, out_scale, head_major):
    width = nh * HEAD_DIM
    fn = functools.partial(_pair_norm_rope, out_scale=out_scale)

    def out_spec(tm):
        if head_major:
            return pl.BlockSpec((nh, tm, HEAD_DIM), lambda i: (0, i, 0))
        return pl.BlockSpec((tm, width), lambda i: (i, 0))

    def specs(tm):
        def full(shp):
            return pl.BlockSpec(shp, lambda i: (0, 0))

        return [pl.BlockSpec((tm, width), lambda i: (i, 0)), full((1, PAIR)), full((PAIR, PAIR)), full((PAIR, PAIR)),
                pl.BlockSpec((tm, PAIR), lambda i: (i, 0)), pl.BlockSpec((tm, PAIR), lambda i: (i, 0))]

    def forward(t, w2, gsum, perm, cos2, sin2):
        s = t.shape[0]
        tm = _pick(s, (512, 256, 128))

        def body(t_ref, w_ref, g_ref, p_ref, cos_ref, sin_ref, o_ref):
            for b in range(nh // 2):
                val = fn(t_ref[:, b * PAIR:(b + 1) * PAIR].astype(f32), w_ref[...], g_ref[...], p_ref[...], cos_ref[...],
                         sin_ref[...]).astype(o_ref.dtype)
                if head_major:
                    o_ref[2 * b] = val[:, :HEAD_DIM]
                    o_ref[2 * b + 1] = val[:, HEAD_DIM:]
                else:
                    o_ref[:, b * PAIR:(b + 1) * PAIR] = val

        return pl.pallas_call(
            body, name=name + "_fwd", grid=(s // tm,), in_specs=specs(tm), out_specs=out_spec(tm),
            out_shape=jax.ShapeDtypeStruct((nh, s, HEAD_DIM) if head_major else (s, width), bf16),
            compiler_params=_cparams(dimension_semantics=("arbitrary",)),
        )(t, w2, gsum, perm, cos2, sin2)

    def backward(t, w2, gsum, perm, cos2, sin2, dout):
        s = t.shape[0]
        tm = _pick(s, (512, 256, 128))

        def body(t_ref, w_ref, g_ref, p_ref, cos_ref, sin_ref, do_ref, dt_ref, dw_ref, pair_buf):
            @pl.when(pl.program_id(0) == 0)
            def _():
                dw_ref[...] = jnp.zeros_like(dw_ref)

            g_v, p_v, cos_v, sin_v = g_ref[...], p_ref[...], cos_ref[...], sin_ref[...]
            dw = jnp.zeros((1, PAIR), f32)
            for b in range(nh // 2):
                sl = slice(b * PAIR, (b + 1) * PAIR)
                if head_major:
                    pair_buf[:, :HEAD_DIM] = do_ref[2 * b].astype(f32)
                    pair_buf[:, HEAD_DIM:] = do_ref[2 * b + 1].astype(f32)
                    ct = pair_buf[...]
                else:
                    ct = do_ref[:, sl].astype(f32)
                _, vjp = jax.vjp(lambda a, c: fn(a, c, g_v, p_v, cos_v, sin_v), t_ref[:, sl].astype(f32), w_ref[...])
                dtb, dwb = vjp(ct)
                dt_ref[:, sl] = dtb.astype(dt_ref.dtype)
                dw = dw + dwb
            dw_ref[...] += dw

        return pl.pallas_call(
            body, name=name + "_bwd", grid=(s // tm,), in_specs=specs(tm) + [out_spec(tm)],
            out_specs=[pl.BlockSpec((tm, width), lambda i: (i, 0)), pl.BlockSpec((1, PAIR), lambda i: (0, 0))],
            out_shape=[jax.ShapeDtypeStruct((s, width), t.dtype), jax.ShapeDtypeStruct((1, PAIR), f32)],
            scratch_shapes=[pltpu.VMEM((tm, PAIR), f32)],
            compiler_params=_cparams(dimension_semantics=("arbitrary",)),
        )(t, w2, gsum, perm, cos2, sin2, dout)

    @jax.custom_vjp
    def op(t, w2, gsum, perm, cos2, sin2):
        return forward(t, w2, gsum, perm, cos2, sin2)

    def op_fwd(*args):
        return forward(*args), args

    def op_bwd(res, dout):
        dt, dw = backward(*res, dout)
        return (dt, dw) + tuple(jnp.zeros_like(r) for r in res[2:])

    op.defvjp(op_fwd, op_bwd)

    def apply(t, w, cos2, sin2):
        gsum, perm = _pair_consts()
        return op(t, jnp.concatenate([w, w], axis=-1), gsum, perm, cos2, sin2)

    return apply


def _fn_softplus(rows, gp, cp, ks):
    (x,), (b,) = rows, gp
    v = x + b
    return (jnp.maximum(v, 0.0) + jnp.log(1.0 + jnp.exp(-jnp.abs(v))),), ()


def _fn_ssd_gate(rows, gp, cp, ks):
    (y, z), (nw,) = rows, gp
    return (_rms(y * _silu(z)) * nw,), ()


def _fn_merge(rows, gp, cp, ks):
    ao, so, ga, gs = rows
    return (jax.nn.sigmoid(ga) * ao + jax.nn.sigmoid(gs) * so,), ()


def _fn_res_norm(rows, gp, cp, ks):
    (x, mo), (g1, nw, sc, sh) = rows, gp
    x1 = x + g1 * mo
    return (x1, (_rms(x1) * nw) * (1.0 + sc) + sh), ()


def _fn_loss(rows, gp, cp, ks):
    (x1, ff), (g2,), (tgt,) = rows, gp, ks
    err = x1 + g2 * ff - tgt
    return (), (0.5 * jnp.sum(jnp.sum(err * err, axis=-1, keepdims=True), axis=0, keepdims=True) / D_MODEL,)


HALO = 8
HALO_BWD = 16


def _conv_tiles(s, c, wide):
    return _pick(s, (512, 256, 128)), _pick(c, (1024, 512, 256, 128) if wide else (512, 256, 128))


def _halo_specs(tm, tc, s, halo=HALO):
    nb = tm // halo
    last = s // halo - 1
    cur = pl.BlockSpec((tm, tc), lambda j, i: (i, j))
    prev = pl.BlockSpec((halo, tc), lambda j, i: (jnp.maximum(i * nb - 1, 0), j))
    nxt = pl.BlockSpec((halo, tc), lambda j, i: (jnp.minimum((i + 1) * nb, last), j))
    return cur, prev, nxt


def _fill_halo(buf, cur, prev, nxt, tm, i, n_i, halo=HALO):
    buf[halo:halo + tm, :] = cur[...]
    buf[0:halo, :] = jnp.where(i > 0, prev[...], 0.0)
    buf[halo + tm:, :] = jnp.where(i < n_i - 1, nxt[...], 0.0)


def _conv_fwd(x, w, b, shard):
    s, c = x.shape
    tm, tc = _conv_tiles(s, c, True)
    n_i, n_j = s // tm, c // tc

    def body(cur, prev, nxt, w_ref, b_ref, shard_ref, o_ref, gath_ref, buf, send_sems, recv_sems, local_sem):
        j, i = pl.program_id(0), pl.program_id(1)
        start, finish = _gather_phases(shard_ref, gath_ref, send_sems, recv_sems, local_sem)

        @pl.when((j == 0) & (i == 0))
        def _():
            start()

        _fill_halo(buf, cur, prev, nxt, tm, i, n_i)
        pre = jnp.zeros((tm, tc), f32) + b_ref[...]
        for k in range(D_CONV):
            pre = pre + buf[HALO - 2 + k:HALO - 2 + k + tm, :] * w_ref[k:k + 1, :]
        o_ref[...] = _silu(pre)

        @pl.when((j == n_j - 1) & (i == n_i - 1))
        def _():
            finish()

    cur, prev, nxt = _halo_specs(tm, tc, s)
    hbm = pl.BlockSpec(memory_space=pl.ANY)
    return pl.pallas_call(
        body, name="conv_silu_fwd", grid=(n_j, n_i),
        in_specs=[cur, prev, nxt, pl.BlockSpec((D_CONV, tc), lambda j, i: (0, j)),
                  pl.BlockSpec((1, tc), lambda j, i: (0, j)), hbm],
        out_specs=[pl.BlockSpec((tm, tc), lambda j, i: (i, j)), hbm],
        out_shape=[jax.ShapeDtypeStruct((s, c), f32), jax.ShapeDtypeStruct((N_DEV,) + shard.shape, shard.dtype)],
        scratch_shapes=[pltpu.VMEM((tm + 2 * HALO, tc), f32)] + COMM_SEMS,
        compiler_params=_cparams(dimension_semantics=("arbitrary", "arbitrary")),
    )(x, x, x, w, b, shard)


def _conv_bwd(x, w, b, dy, g):
    s, c = x.shape
    tm, tc = _conv_tiles(s, c, False)
    n_i, n_j = s // tm, c // tc
    ext = tm + 16

    def body(cur, prev, nxt, dcur, dprev, dnxt, w_ref, b_ref, g_ref, dx_ref, dw_ref, db_ref, recv_ref,
             xbuf, dbuf, pbuf, send_sems, recv_sems, local_sem):
        j, i = pl.program_id(0), pl.program_id(1)
        start, finish = _scatter_phases(g_ref, recv_ref, send_sems, recv_sems, local_sem)

        @pl.when((j == 0) & (i == 0))
        def _():
            start()

        _fill_halo(xbuf, cur, prev, nxt, tm, i, n_i, HALO_BWD)
        _fill_halo(dbuf, dcur, dprev, dnxt, tm, i, n_i, HALO_BWD)
        xs = [xbuf[6 + k:6 + k + ext, :] for k in range(D_CONV)]
        pre = jnp.zeros((ext, tc), f32) + b_ref[...]
        for k in range(D_CONV):
            pre = pre + xs[k] * w_ref[k:k + 1, :]
        sg = jax.nn.sigmoid(pre)
        pbuf[...] = dbuf[8:8 + ext, :] * (sg * (1.0 + pre * (1.0 - sg)))
        dx = jnp.zeros((tm, tc), f32)
        for k in range(D_CONV):
            dx = dx + pbuf[10 - k:10 - k + tm, :] * w_ref[k:k + 1, :]
        dx_ref[...] = dx

        @pl.when(i == 0)
        def _():
            dw_ref[...] = jnp.zeros_like(dw_ref)
            db_ref[...] = jnp.zeros_like(db_ref)

        dpre = pbuf[8:8 + tm, :]
        db_ref[...] += jnp.sum(dpre, axis=0, keepdims=True)
        for k in range(D_CONV):
            dw_ref[k:k + 1, :] += jnp.sum(dpre * xs[k][8:8 + tm, :], axis=0, keepdims=True)

        @pl.when((j == n_j - 1) & (i == n_i - 1))
        def _():
            finish()

    cur, prev, nxt = _halo_specs(tm, tc, s, HALO_BWD)
    hbm = pl.BlockSpec(memory_space=pl.ANY)
    return pl.pallas_call(
        body, name="conv_silu_bwd", grid=(n_j, n_i),
        in_specs=[cur, prev, nxt, cur, prev, nxt, pl.BlockSpec((D_CONV, tc), lambda j, i: (0, j)),
                  pl.BlockSpec((1, tc), lambda j, i: (0, j)), hbm],
        out_specs=[pl.BlockSpec((tm, tc), lambda j, i: (i, j)), pl.BlockSpec((D_CONV, tc), lambda j, i: (0, j)),
                   pl.BlockSpec((1, tc), lambda j, i: (0, j)), hbm],
        out_shape=[jax.ShapeDtypeStruct((s, c), f32), jax.ShapeDtypeStruct((D_CONV, c), f32),
                   jax.ShapeDtypeStruct((1, c), f32), jax.ShapeDtypeStruct(g.shape, g.dtype)],
        scratch_shapes=[pltpu.VMEM((tm + 2 * HALO_BWD, tc), f32), pltpu.VMEM((tm + 2 * HALO_BWD, tc), f32),
                        pltpu.VMEM((ext, tc), f32)] + COMM_SEMS,
        compiler_params=_cparams(dimension_semantics=("arbitrary", "arbitrary")),
    )(x, x, x, dy, dy, dy, w, b, g)


@jax.custom_vjp
def conv_silu_comm(x, w, b, shard, recv_like):
    act, gathered = _conv_fwd(x, w, b, shard)
    return (act, gathered) + tuple(jnp.zeros(shp, f32) for shp in LATE_SHAPES)


def _conv_silu_comm_fwd(x, w, b, shard, recv_like):
    return conv_silu_comm(x, w, b, shard, recv_like), (x, w, b, shard)


def _conv_silu_comm_bwd(res, cts):
    x, w, b, shard = res
    dx, dw, db, recv = _conv_bwd(x, w, b, cts[0], _pack_late_grads(dict(zip(LATE, cts[2:]))))
    return dx, dw, db, jnp.zeros_like(shard), recv


conv_silu_comm.defvjp(_conv_silu_comm_fwd, _conv_silu_comm_bwd)


ATT_SCALE = HEAD_DIM ** -0.5
Q_SCALE = ATT_SCALE * math.log2(math.e)
LN2 = math.log(2.0)
REP = N_Q_HEADS // N_KV_HEADS


HP = 2
assert REP % HP == 0


def _attn_fwd(q, k, v):
    s, dh = q.shape[0], HEAD_DIM
    hq = q.shape[1] // dh
    tq = _pick(s, (256, 128))

    v1 = jnp.concatenate([v, jnp.ones(v.shape[:2] + (1,), v.dtype), jnp.zeros(v.shape[:2] + (dh - 1,), v.dtype)],
                         axis=-1)

    def body(q_ref, k_ref, v_ref, o_ref, p_ref, linv_ref):
        for j in range(HP):
            sl = slice(j * dh, (j + 1) * dh)
            sc = lax.dot_general(q_ref[:, sl], k_ref[0], _DIMS["nt"], preferred_element_type=f32)
            m = jnp.max(sc, axis=-1, keepdims=True)
            p = jnp.exp2(sc - m).astype(bf16)
            p_ref[j] = p
            o1 = jnp.dot(p, v_ref[0], preferred_element_type=f32)
            linv = 1.0 / o1[:, dh:dh + 1]
            o_ref[:, sl] = (o1[:, :dh] * linv).astype(o_ref.dtype)
            linv_ref[j] = linv

    return pl.pallas_call(
        body, name="attn_fwd", grid=(hq // HP, s // tq),
        in_specs=[pl.BlockSpec((tq, HP * dh), lambda h, i: (i, h)),
                  pl.BlockSpec((1, s, dh), lambda h, i: (h * HP // REP, 0, 0)),
                  pl.BlockSpec((1, s, 2 * dh), lambda h, i: (h * HP // REP, 0, 0))],
        out_specs=[pl.BlockSpec((tq, HP * dh), lambda h, i: (i, h)),
                   pl.BlockSpec((HP, tq, s), lambda h, i: (h, i, 0)),
                   pl.BlockSpec((HP, tq, 1), lambda h, i: (h, i, 0))],
        out_shape=[jax.ShapeDtypeStruct((s, hq * dh), bf16), jax.ShapeDtypeStruct((hq, s, s), bf16),
                   jax.ShapeDtypeStruct((hq, s, 1), f32)],
        compiler_params=_cparams(dimension_semantics=("parallel", "arbitrary")),
    )(q, k, v1)


def _attn_bwd(p, do, o, q, k, v, linv):
    hq, s, _ = p.shape
    dh = HEAD_DIM
    tq = _pick(s, (256, 128))

    def body(p_ref, do_ref, o_ref, q_ref, k_ref, v_ref, linv_ref, dq_ref, dkt_ref, dvt_ref):
        @pl.when(pl.program_id(1) == 0)
        def _():
            dkt_ref[...] = jnp.zeros_like(dkt_ref)
            dvt_ref[...] = jnp.zeros_like(dvt_ref)

        for j in range(HP):
            sl = slice(j * dh, (j + 1) * dh)
            pp, doh, li = p_ref[j], do_ref[:, sl], linv_ref[j]
            do32 = doh.astype(f32)
            d = jnp.sum(do32 * o_ref[:, sl].astype(f32), axis=-1, keepdims=True)
            dp = lax.dot_general(doh, v_ref[0], _DIMS["nt"], preferred_element_type=f32)
            ds = (pp.astype(f32) * ((dp - d) * li)).astype(bf16)
            dq_ref[:, sl] = (jnp.dot(ds, k_ref[0], preferred_element_type=f32) * LN2).astype(dq_ref.dtype)
            dvt_ref[j] += lax.dot_general((do32 * li).astype(bf16), pp, _DIMS["tn"], preferred_element_type=f32)
            dkt_ref[j] += lax.dot_general(q_ref[:, sl], ds, _DIMS["tn"], preferred_element_type=f32)

    def row():
        return pl.BlockSpec((tq, HP * dh), lambda h, i: (i, h))

    return pl.pallas_call(
        body, name="attn_bwd", grid=(hq // HP, s // tq),
        in_specs=[pl.BlockSpec((HP, tq, s), lambda h, i: (h, i, 0)), row(), row(), row(),
                  pl.BlockSpec((1, s, dh), lambda h, i: (h * HP // REP, 0, 0)),
                  pl.BlockSpec((1, s, dh), lambda h, i: (h * HP // REP, 0, 0)),
                  pl.BlockSpec((HP, tq, 1), lambda h, i: (h, i, 0))],
        out_specs=[row(), pl.BlockSpec((HP, dh, s), lambda h, i: (h, 0, 0)),
                   pl.BlockSpec((HP, dh, s), lambda h, i: (h, 0, 0))],
        out_shape=[jax.ShapeDtypeStruct((s, hq * dh), q.dtype), jax.ShapeDtypeStruct((hq, dh, s), f32),
                   jax.ShapeDtypeStruct((hq, dh, s), f32)],
        compiler_params=_cparams(dimension_semantics=("parallel", "arbitrary")),
    )(p, do, o, q, k, v, linv)


@jax.custom_vjp
def attention(q, k, v):
    return _attn_fwd(q, k, v)[0]


def _attention_fwd(q, k, v):
    o, p, linv = _attn_fwd(q, k, v)
    return o, (q, k, v, o, p, linv)


def _attention_bwd(res, do):
    q, k, v, o, p, linv = res
    s = q.shape[0]
    dq, dkt, dvt = _attn_bwd(p, do.astype(bf16), o, q, k, v, linv)

    def per_kv_head(t):
        return jnp.swapaxes(t.reshape(N_KV_HEADS, REP, HEAD_DIM, s).sum(axis=1), 1, 2)

    return dq, (per_kv_head(dkt) * LN2).astype(k.dtype), per_kv_head(dvt).astype(v.dtype)


attention.defvjp(_attention_fwd, _attention_bwd)


HPG = N_SSD_HEADS // N_SSD_GROUPS
GW = HPG * SSD_HEAD_DIM
NEG = -1e30
SPLIT_ROWS = 32


def _ssd_consts():
    k = np.arange(SPLIT_ROWS)[:, None]
    live = k < 3 * HPG
    sel_chunk = ((k % HPG) == (np.arange(HPG * CHUNK)[None, :] // CHUNK)) & live
    sel_head = ((k % HPG) == (np.arange(GW)[None, :] // SSD_HEAD_DIM)) & live
    return jnp.asarray(sel_chunk, bf16), jnp.asarray(sel_head, bf16)


def _split3(x):
    hi = x.astype(bf16).astype(f32)
    r1 = x - hi
    mid = r1.astype(bf16).astype(f32)
    lo = (r1 - mid).astype(bf16).astype(f32)
    return jnp.concatenate([hi, mid, lo, jnp.zeros_like(hi)], axis=0).astype(bf16)


def _tn(a, b):
    return lax.dot_general(a, b, _DIMS["tn"], preferred_element_type=f32)


def _nt(a, b):
    return lax.dot_general(a, b, _DIMS["nt"], preferred_element_type=f32)


def _nn(a, b):
    return jnp.dot(a, b, preferred_element_type=f32)


def _head_sum(sel8, x):
    hi = x.astype(bf16)
    lo = (x - hi.astype(f32)).astype(bf16)
    return _nt(sel8, hi) + _nt(sel8, lo)


def _ssd_masks(reverse):
    r = lax.broadcasted_iota(jnp.int32, (CHUNK, CHUNK), 0)
    c = lax.broadcasted_iota(jnp.int32, (CHUNK, CHUNK), 1)
    lower, upper = r >= c, r <= c
    return (upper, lower) if reverse else (lower, upper)


def _ssd_in_specs(cidx):
    return [pl.BlockSpec((CHUNK, D_INNER), lambda c: (cidx(c), 0)),
            pl.BlockSpec((CHUNK, GN), lambda c: (cidx(c), D_INNER // GN)),
            pl.BlockSpec((CHUNK, GN), lambda c: (cidx(c), D_INNER // GN + 1)),
            pl.BlockSpec((N_SSD_HEADS, CHUNK), lambda c: (0, cidx(c))),
            pl.BlockSpec((N_SSD_HEADS, 1), lambda c: (0, 0)),
            pl.BlockSpec((SPLIT_ROWS, HPG * CHUNK), lambda c: (0, 0)),
            pl.BlockSpec((SPLIT_ROWS, GW), lambda c: (0, 0))]


def _ssd_chunk_common(dtt_ref, a_ref, et_ref, mask_t):
    dtt = dtt_ref[...]
    et = jnp.dot(dtt * a_ref[...], mask_t.astype(f32), precision=HIGHEST, preferred_element_type=f32)
    et_ref[...] = et
    return dtt, et


def _ssd_group_common(g, dtt, et, selc_ref, selh_ref, xs_ref, b_ref, c_ref, last):
    gr = slice(g * HPG, (g + 1) * HPG)
    e3 = _split3(et[gr])
    col = _tn(e3, selc_ref[...])
    eb = _tn(e3, selh_ref[...])
    dtb = _tn(_split3(dtt[gr]), selh_ref[...])
    tbc = eb[last:last + 1, :]
    xs = xs_ref[:, g * GW:(g + 1) * GW]
    bg = b_ref[:, g * D_STATE:(g + 1) * D_STATE].astype(bf16)
    cg = c_ref[:, g * D_STATE:(g + 1) * D_STATE].astype(bf16)
    return col, eb, dtb, tbc, xs, bg, cg


def _ssd_fwd(xbc, dtt, a_col, reverse, y_prev=None, dexp=None):
    s = xbc.shape[0]
    nc = s // CHUNK
    cidx = (lambda c: nc - 1 - c) if reverse else (lambda c: c)
    last = 0 if reverse else CHUNK - 1
    selc, selh = _ssd_consts()
    final = y_prev is not None
    n_in = 9 if final else 7

    def body(*refs):
        xs_ref, b_ref, c_ref, dtt_ref, a_ref, selc_ref, selh_ref = refs[:7]
        y_ref, st_ref, ht_ref, et_ref, yg_ref = refs[n_in:]

        @pl.when(pl.program_id(0) == 0)
        def _():
            ht_ref[...] = jnp.zeros_like(ht_ref)

        mask, mask_t = _ssd_masks(reverse)
        dtt_v, et = _ssd_chunk_common(dtt_ref, a_ref, et_ref, mask_t)
        for g in range(N_SSD_GROUPS):
            col, eb, dtb, tbc, xs, bg, cg = _ssd_group_common(g, dtt_v, et, selc_ref, selh_ref, xs_ref, b_ref, c_ref,
                                                              last)
            xd = xs * dtb
            cb = _nt(cg, bg)
            ht = ht_ref[g]
            st_ref[0, g] = ht
            yoff = _nn(cg, ht.astype(bf16)) * jnp.exp(eb)
            for j in range(HPG):
                h = g * HPG + j
                hs = slice(j * SSD_HEAD_DIM, (j + 1) * SSD_HEAD_DIM)
                lam = jnp.exp(jnp.where(mask, col[:, j * CHUNK:(j + 1) * CHUNK] - et_ref[h:h + 1, :], NEG))
                yg_ref[:, hs] = _nn((cb * lam).astype(bf16), xd[:, hs].astype(bf16))
            cols = slice(g * GW, (g + 1) * GW)
            yg = yg_ref[...] + yoff
            if final:
                yg = yg + refs[7][:, cols] + xs * refs[8][:, cols]
            y_ref[:, cols] = yg.astype(y_ref.dtype)
            ht_ref[g] = jnp.exp(tbc) * ht + _tn(bg, (xd * jnp.exp(tbc - eb)).astype(bf16))

    y_spec = pl.BlockSpec((CHUNK, D_INNER), lambda c: (cidx(c), 0))
    extra_specs = [y_spec, pl.BlockSpec((1, D_INNER), lambda c: (0, 0))] if final else []
    return pl.pallas_call(
        body, name="ssd_fwd_rev" if reverse else "ssd_fwd", grid=(nc,),
        in_specs=_ssd_in_specs(cidx) + extra_specs,
        out_specs=[y_spec, pl.BlockSpec((1, N_SSD_GROUPS, D_STATE, GW), lambda c: (cidx(c), 0, 0, 0))],
        out_shape=[jax.ShapeDtypeStruct((s, D_INNER), bf16 if final else f32),
                   jax.ShapeDtypeStruct((nc, N_SSD_GROUPS, D_STATE, GW), f32)],
        scratch_shapes=[pltpu.VMEM((N_SSD_GROUPS, D_STATE, GW), f32), pltpu.VMEM((N_SSD_HEADS, CHUNK), f32),
                        pltpu.VMEM((CHUNK, GW), f32)],
        compiler_params=_cparams(dimension_semantics=("arbitrary",)),
    )(xbc, xbc, xbc, dtt, a_col, selc, selh, *((y_prev, dexp) if final else ()))


def _ssd_bwd(xbc, dtt, a_col, states, dy, reverse, dxbc_prev=None, dexp=None):
    s = xbc.shape[0]
    nc = s // CHUNK
    cidx = (lambda c: c) if reverse else (lambda c: nc - 1 - c)
    last = 0 if reverse else CHUNK - 1
    selc, selh = _ssd_consts()
    final = dxbc_prev is not None
    n_in = 11 if final else 9
    n_out = 4 if final else 3

    def body(*refs):
        xs_ref, b_ref, c_ref, dtt_ref, a_ref, selc_ref, selh_ref, st_ref, dy_ref = refs[:9]
        dxbc_ref, ddtt_ref, da_ref = refs[n_in:n_in + 3]
        dh_ref, et_ref, det_ref, det2_ref, ddt_ref, q_ref = refs[n_in + n_out:]
        if final:
            prev_ref, dexp_ref, ddexp_ref = refs[9], refs[10], refs[n_in + 3]

        @pl.when(pl.program_id(0) == 0)
        def _():
            dh_ref[...] = jnp.zeros_like(dh_ref)
            da_ref[...] = jnp.zeros_like(da_ref)
            if final:
                ddexp_ref[...] = jnp.zeros_like(ddexp_ref)

        mask, mask_t = _ssd_masks(reverse)
        dtt_v, et = _ssd_chunk_common(dtt_ref, a_ref, et_ref, mask_t)
        sel8 = selh_ref[0:HPG, :]
        is_last = lax.broadcasted_iota(jnp.int32, (CHUNK, GW), 0) == last
        for g in range(N_SSD_GROUPS):
            col, eb, dtb, tbc, xs, bg, cg = _ssd_group_common(g, dtt_v, et, selc_ref, selh_ref, xs_ref, b_ref, c_ref,
                                                              last)
            xd = xs * dtb
            cb = _nt(cg, bg)
            cbt = _nt(bg, cg)
            exp_t = jnp.exp(tbc)
            dfac = jnp.exp(tbc - eb)
            ht = st_ref[0, g]
            dhn = dh_ref[g]
            ht16, dhn16 = ht.astype(bf16), dhn.astype(bf16)
            dy = dy_ref[:, g * GW:(g + 1) * GW].astype(f32)
            dye = dy * jnp.exp(eb)
            dye16 = dye.astype(bf16)
            dc = _nt(dye16, ht16)
            dh_ref[g] = exp_t * dhn + _tn(cg, dye16)
            deb = dye * _nn(cg, ht16)
            xdd = xd * dfac
            dxdd = _nn(bg, dhn16)
            db = _nt(xdd.astype(bf16), dhn16)
            dxd_state = dxdd * dfac
            ddf = dxdd * xdd
            dtbc = jnp.sum(ddf, axis=0, keepdims=True) + exp_t * jnp.sum(dhn * ht, axis=0, keepdims=True)
            deb = deb - ddf + jnp.where(is_last, dtbc, 0.0)
            dcb = jnp.zeros((CHUNK, CHUNK), f32)
            dcbt = jnp.zeros((CHUNK, CHUNK), f32)
            for j in range(HPG):
                h = g * HPG + j
                hs = slice(j * SSD_HEAD_DIM, (j + 1) * SSD_HEAD_DIM)
                colj = col[:, j * CHUNK:(j + 1) * CHUNK]
                row = et_ref[h:h + 1, :]
                lam = jnp.exp(jnp.where(mask, colj - row, NEG))
                lam_t = lam.T
                xdj, dyj = xd[:, hs].astype(bf16), dy[:, hs].astype(bf16)
                t1 = _nt(dyj, xdj) * lam
                t2 = _nt(xdj, dyj) * lam_t
                dcb, dcbt = dcb + t1, dcbt + t2
                det_ref[h:h + 1, :] = -jnp.sum(t1 * cb - t2 * cbt, axis=0, keepdims=True)
                q_ref[:, hs] = _nn((cbt * lam_t).astype(bf16), dyj)
            x_cols = slice(g * GW, (g + 1) * GW)
            dxd = q_ref[...] + dxd_state
            dxs = dxd * dtb
            if final:
                dxs = dxs + prev_ref[:, x_cols] + dy * dexp_ref[:, x_cols]
            dxbc_ref[:, x_cols] = dxs
            b_cols = slice(D_INNER + g * D_STATE, D_INNER + (g + 1) * D_STATE)
            c_cols = slice(D_INNER + GN + g * D_STATE, D_INNER + GN + (g + 1) * D_STATE)
            db = db + _nn(dcbt.astype(bf16), cg)
            dc = dc + _nn(dcb.astype(bf16), bg)
            if final:
                db, dc = db + prev_ref[:, b_cols], dc + prev_ref[:, c_cols]
                ddexp_ref[:, g * GW:(g + 1) * GW] += jnp.sum(dy * xs, axis=0, keepdims=True)
            dxbc_ref[:, b_cols] = db
            dxbc_ref[:, c_cols] = dc
            det2_ref[g * HPG:(g + 1) * HPG, :] = _head_sum(sel8, deb)
            ddt_ref[g * HPG:(g + 1) * HPG, :] = _head_sum(sel8, dxd * xs)
        dat = jnp.dot(det_ref[...] + det2_ref[...], mask.astype(f32), precision=HIGHEST, preferred_element_type=f32)
        ddtt_ref[...] = ddt_ref[...] + dat * a_ref[...]
        da_ref[...] += jnp.sum(dat * dtt_v, axis=1, keepdims=True)

    in_specs = _ssd_in_specs(cidx) + [
        pl.BlockSpec((1, N_SSD_GROUPS, D_STATE, GW), lambda c: (cidx(c), 0, 0, 0)),
        pl.BlockSpec((CHUNK, D_INNER), lambda c: (cidx(c), 0))]
    hl = pltpu.VMEM((N_SSD_HEADS, CHUNK), f32)
    dxbc_spec = pl.BlockSpec((CHUNK, CONV_DIM), lambda c: (cidx(c), 0))
    dexp_spec = pl.BlockSpec((1, D_INNER), lambda c: (0, 0))
    return pl.pallas_call(
        body, name="ssd_bwd_rev" if reverse else "ssd_bwd", grid=(nc,),
        in_specs=in_specs + ([dxbc_spec, dexp_spec] if final else []),
        out_specs=[dxbc_spec, pl.BlockSpec((N_SSD_HEADS, CHUNK), lambda c: (0, cidx(c))),
                   pl.BlockSpec((N_SSD_HEADS, 1), lambda c: (0, 0))] + ([dexp_spec] if final else []),
        out_shape=[jax.ShapeDtypeStruct((s, CONV_DIM), f32), jax.ShapeDtypeStruct((N_SSD_HEADS, s), f32),
                   jax.ShapeDtypeStruct((N_SSD_HEADS, 1), f32)]
        + ([jax.ShapeDtypeStruct((1, D_INNER), f32)] if final else []),
        scratch_shapes=[pltpu.VMEM((N_SSD_GROUPS, D_STATE, GW), f32), hl, hl, hl, hl, pltpu.VMEM((CHUNK, GW), f32)],
        compiler_params=_cparams(dimension_semantics=("arbitrary",)),
    )(xbc, xbc, xbc, dtt, a_col, selc, selh, states, dy, *((dxbc_prev, dexp) if final else ()))


@jax.custom_vjp
def ssd_bidir(xbc, dtt, a_col, dexp):
    y_f, _ = _ssd_fwd(xbc, dtt[:N_SSD_HEADS], a_col[:N_SSD_HEADS], False)
    return _ssd_fwd(xbc, dtt[N_SSD_HEADS:], a_col[N_SSD_HEADS:], True, y_prev=y_f, dexp=dexp)[0]


def _ssd_bidir_fwd(xbc, dtt, a_col, dexp):
    y_f, st_f = _ssd_fwd(xbc, dtt[:N_SSD_HEADS], a_col[:N_SSD_HEADS], False)
    y, st_b = _ssd_fwd(xbc, dtt[N_SSD_HEADS:], a_col[N_SSD_HEADS:], True, y_prev=y_f, dexp=dexp)
    return y, (xbc, dtt, a_col, dexp, st_f, st_b)


def _ssd_bidir_bwd(res, dy):
    xbc, dtt, a_col, dexp, st_f, st_b = res
    dxbc_f, ddtt_f, da_f = _ssd_bwd(xbc, dtt[:N_SSD_HEADS], a_col[:N_SSD_HEADS], st_f, dy, False)
    dxbc, ddtt_b, da_b, ddexp = _ssd_bwd(xbc, dtt[N_SSD_HEADS:], a_col[N_SSD_HEADS:], st_b, dy, True,
                                         dxbc_prev=dxbc_f, dexp=dexp)
    return dxbc, jnp.concatenate([ddtt_f, ddtt_b], axis=0), jnp.concatenate([da_f, da_b], axis=0), ddexp


ssd_bidir.defvjp(_ssd_bidir_fwd, _ssd_bidir_bwd)


def _rope_tables(s):
    rows = s // GRID_W
    pos_row = np.repeat(np.arange(rows), GRID_W).astype(np.float32)
    pos_col = np.tile(np.arange(GRID_W), rows).astype(np.float32)
    axis_dim = HEAD_DIM // 2
    inv_freq = np.float32(ROPE_THETA) ** (-np.arange(0, axis_dim, 2, dtype=np.float32) / np.float32(axis_dim))
    ang_r = pos_row[:, None] * inv_freq[None, :].astype(np.float32)
    ang_c = pos_col[:, None] * inv_freq[None, :].astype(np.float32)
    cos = np.concatenate([np.cos(ang_r), np.cos(ang_r), np.cos(ang_c), np.cos(ang_c)] * 2, axis=-1)
    sin = np.concatenate([np.sin(ang_r), np.sin(ang_r), np.sin(ang_c), np.sin(ang_c)] * 2, axis=-1)
    return jnp.asarray(cos, f32), jnp.asarray(sin, f32)


def _rope_perm():
    p = np.zeros((HEAD_DIM, HEAD_DIM), np.float32)
    for j in range(HEAD_DIM):
        if (j % 32) < 16:
            p[j + 16, j] = -1.0
        else:
            p[j - 16, j] = 1.0
    return p


def local_loss(x, mod, small, recv_in_like, recv_late_like, wfull, late_shard, target):
    s = x.shape[0]
    lin = {n: make_linear("lin_" + n) for n in LATE if not n.startswith("mlp")}
    wfull, wgrads = dict(wfull), {}
    shift1, scale1, gate1, shift2, scale2, gate2 = [mod[i] for i in range(6)]

    norm_mod = make_rowwise("norm_mod", _fn_norm_mod, [(D_MODEL, bf16), (D_MODEL, f32)])
    (h, x_res), _ = norm_mod((x,), (small["norm1_w"], scale1, shift1), (), ())

    proj = dict(zip(PROJ_NAMES, in_proj(h, tuple(wfull[n] for n in PROJ_NAMES), recv_in_like)))

    cos, sin = _rope_tables(s)

    def heads(t, nh):
        return t.reshape(s, nh, HEAD_DIM).transpose(1, 0, 2)

    qr = make_head_rope("q_norm_rope", N_Q_HEADS, Q_SCALE, False)(proj["q"], small["q_norm_w"], cos, sin)
    kr = make_head_rope("k_norm_rope", N_KV_HEADS, 1.0, True)(proj["k"], small["k_norm_w"], cos, sin)
    vh = heads(proj["v"], N_KV_HEADS).astype(bf16)
    att = attention(qr, kr, vh)

    xbc, gathered, *carriers = conv_silu_comm(proj["xbc"], small["conv_w"], small["conv_b"], late_shard,
                                              recv_late_like)
    wfull.update(_split_late(gathered))
    wgrads.update(zip(LATE, carriers))
    ao = lin["attn_out"](att, wfull["attn_out"], wgrads["attn_out"])
    softplus = make_rowwise("dt_softplus", _fn_softplus, [(2 * N_SSD_HEADS, f32)])
    (dt,), _ = softplus((proj["dt"][:, :2 * N_SSD_HEADS],), (small["dt_bias"].reshape(1, 2 * N_SSD_HEADS),), (), ())
    a_neg = -jnp.exp(small["A_log"])
    dexp = jnp.repeat(small["ssd_D"].reshape(N_SSD_HEADS), SSD_HEAD_DIM).reshape(1, D_INNER)
    y = ssd_bidir(xbc, dt.T, a_neg.reshape(2 * N_SSD_HEADS, 1), dexp)
    ssd_gate = make_rowwise("ssd_gate", _fn_ssd_gate, [(D_INNER, bf16)], tm_pref=256)
    (ssd_out,), _ = ssd_gate((y, proj["z"]), (small["ssd_norm_w"],), (), ())
    so = lin["ssd_out"](ssd_out, wfull["ssd_out"], wgrads["ssd_out"])

    merge = make_rowwise("merge", _fn_merge, [(D_MODEL, bf16)])
    (merged,), _ = merge((ao, so, proj["ga"], proj["gs"]), (), (), ())
    mo = lin["o"](merged, wfull["o"], wgrads["o"])

    res_norm = make_rowwise("res_norm", _fn_res_norm, [(D_MODEL, f32), (D_MODEL, bf16)])
    (x1, h2), _ = res_norm((x_res, mo), (gate1, small["norm2_w"], scale2, shift2), (), ())
    ff = mlp(h2, wfull["mlp1"], wgrads["mlp1"], wfull["mlp2"], wgrads["mlp2"])
    loss_op = make_rowwise("loss", _fn_loss, [], [(1, 1)])
    _, (loss,) = loss_op((x1, ff), (gate2,), (), (target,))
    return loss[0, 0]


_BC1 = 1.0 - ADAM_B1 ** ADAM_STEP
_BC2 = 1.0 - ADAM_B2 ** ADAM_STEP


def _adamw(w, g, m, v):
    m = ADAM_B1 * m + (1.0 - ADAM_B1) * g
    v = ADAM_B2 * v + (1.0 - ADAM_B2) * (g * g)
    delta = -ADAM_LR * ((m / _BC1) / (jnp.sqrt(v / _BC2) + ADAM_EPS) + ADAM_WD * w)
    return delta, m, v


def _ada_fwd(c_all, w, b):
    n = w.shape[1]

    def body(c_ref, w_ref, b_ref, o_ref):
        o_ref[...] = jnp.dot(_silu(c_ref[...]), w_ref[...], precision=HIGHEST, preferred_element_type=f32) + b_ref[...]

    return pl.pallas_call(body, name="ada_fwd", out_shape=jax.ShapeDtypeStruct((N_DEV, n), f32),
                          compiler_params=_cparams())(c_all, w, b)


def _ada_bwd_adamw(c_all, dmod, w, m, v):
    d, n = w.shape
    tr = _pick(d, (256, 128))

    def body(c_ref, dm_ref, w_ref, m_ref, v_ref, g_ref, dl_ref, mo_ref, vo_ref):
        g = lax.dot_general(_silu(c_ref[...]), dm_ref[...], _DIMS["tn"], precision=HIGHEST,
                            preferred_element_type=f32)
        g_ref[...] = g
        dl_ref[...], mo_ref[...], vo_ref[...] = _adamw(w_ref[...], g, m_ref[...], v_ref[...])

    blk = pl.BlockSpec((tr, n), lambda i: (i, 0))
    return pl.pallas_call(
        body, name="ada_bwd_adamw", grid=(d // tr,),
        in_specs=[pl.BlockSpec((N_DEV, tr), lambda i: (0, i)), pl.BlockSpec((N_DEV, n), lambda i: (0, 0)), blk, blk, blk],
        out_specs=[blk] * 4, out_shape=[jax.ShapeDtypeStruct((d, n), f32)] * 4,
        compiler_params=_cparams(dimension_semantics=("parallel",)),
    )(c_all, dmod, w, m, v)


def _sum_over_mesh(g):
    def body(g_ref, o_ref):
        acc = g_ref[0]
        for d in range(1, N_DEV):
            acc = acc + g_ref[d]
        o_ref[...] = acc

    return pl.pallas_call(body, name="sum_small", out_shape=jax.ShapeDtypeStruct(g.shape[1:], f32),
                          compiler_params=_cparams())(g)


def _adamw_small(w, g, m, v):
    def body(w_ref, g_ref, m_ref, v_ref, dl_ref, mo_ref, vo_ref):
        dl_ref[...], mo_ref[...], vo_ref[...] = _adamw(w_ref[...], g_ref[...], m_ref[...], v_ref[...])

    return pl.pallas_call(body, name="adamw_small", out_shape=[jax.ShapeDtypeStruct(w.shape, f32)] * 3,
                          compiler_params=_cparams())(w, g, m, v)


def _sum_adamw(recv, w, m, v, name):
    _, r, c = recv.shape
    tr = _pick(r, (256, 128, 64, 16))

    def body(g_ref, w_ref, m_ref, v_ref, go_ref, dl_ref, mo_ref, vo_ref):
        g = g_ref[0].astype(f32)
        for d in range(1, N_DEV):
            g = g + g_ref[d].astype(f32)
        go_ref[...] = g
        dl_ref[...], mo_ref[...], vo_ref[...] = _adamw(w_ref[...], g, m_ref[...], v_ref[...])

    blk = pl.BlockSpec((tr, c), lambda i: (i, 0))
    return pl.pallas_call(
        body, name=name, grid=(r // tr,),
        in_specs=[pl.BlockSpec((N_DEV, tr, c), lambda i: (0, i, 0)), blk, blk, blk],
        out_specs=[blk] * 4, out_shape=[jax.ShapeDtypeStruct((r, c), f32)] * 4,
        compiler_params=_cparams(dimension_semantics=("parallel",)),
    )(recv, w, m, v)


def _pack_small(arrs):
    parts = []
    for a in arrs:
        flat = a.reshape(-1).astype(f32)
        parts.append(jnp.pad(flat, (0, (-flat.shape[0]) % LANE)))
    flat = jnp.concatenate(parts)
    flat = jnp.pad(flat, (0, (-flat.shape[0]) % (8 * LANE)))
    return flat.reshape(-1, LANE)


def _unpack_small(packed, shapes):
    flat = packed.reshape(-1)
    out, off = [], 0
    for shp in shapes:
        n = int(np.prod(shp))
        out.append(flat[off:off + n].reshape(shp))
        off += n + (-n) % LANE
    return out


BIG = ("w_attn_out", "w_ssd_out", "w_o", "w_mlp1", "w_mlp2")
BIG_ROWS = (N_Q_HEADS * HEAD_DIM // N_DEV, D_INNER // N_DEV, D_MODEL // N_DEV,
            D_MODEL * (D_FF // N_DEV) // PACK_COLS, D_FF // N_DEV)
N_IN_SHARD = D_IN_PROJ // N_DEV
assert sum(BIG_ROWS) % 16 == 0


def _pack_big(shards, dtype):
    return jnp.concatenate([s.astype(dtype).reshape(-1, PACK_COLS) for s in shards], axis=0)


def _unpack_big(packed, shapes):
    out, off = [], 0
    for rows, shp in zip(BIG_ROWS, shapes):
        out.append(packed[off:off + rows].reshape(shp))
        off += rows
    return out


LATE = ("attn_out", "ssd_out", "o", "mlp1", "mlp2")
LATE_SHAPES = ((N_Q_HEADS * HEAD_DIM, D_MODEL), (D_INNER, D_MODEL), (D_MODEL, D_MODEL), (D_MODEL, D_FF),
               (D_FF, D_MODEL))


def _split_w_in(g_in):
    w_in = g_in.transpose(1, 0, 2).reshape(D_MODEL, D_IN_PROJ)
    w = {}
    off = 0
    for name, size in zip(PROJ_NAMES, PROJ_SIZES):
        w[name] = w_in[:, off:off + size]
        off += size
    w["dt"] = jnp.pad(w["dt"], ((0, 0), (0, DT_PAD - 2 * N_SSD_HEADS)))
    return w


def _split_late(g):
    offs = np.cumsum((0,) + BIG_ROWS)
    sl = [g[:, offs[i]:offs[i + 1]] for i in range(len(BIG))]
    return {"attn_out": sl[0].reshape(LATE_SHAPES[0]), "ssd_out": sl[1].reshape(LATE_SHAPES[1]),
            "o": sl[2].reshape(LATE_SHAPES[2]),
            "mlp1": sl[3].reshape(N_DEV, D_MODEL, D_FF // N_DEV).transpose(1, 0, 2).reshape(LATE_SHAPES[3]),
            "mlp2": sl[4].reshape(LATE_SHAPES[4])}


def _pack_in_grads(gw):
    gw = {n: g.astype(bf16) for n, g in gw.items()}
    gw["dt"] = gw["dt"][:, :2 * N_SSD_HEADS]
    g_in = jnp.concatenate([gw[n] for n in PROJ_NAMES], axis=1)
    return g_in.reshape(D_MODEL, N_DEV, N_IN_SHARD).transpose(1, 0, 2)


def _pack_late_grads(gw):
    gw = {n: g.astype(bf16) for n, g in gw.items()}
    parts = [
        gw["attn_out"].reshape(N_DEV, -1, PACK_COLS),
        gw["ssd_out"].reshape(N_DEV, -1, PACK_COLS),
        gw["o"].reshape(N_DEV, -1, PACK_COLS),
        gw["mlp1"].reshape(D_MODEL, N_DEV, D_FF // N_DEV).transpose(1, 0, 2).reshape(N_DEV, -1, PACK_COLS),
        gw["mlp2"].reshape(N_DEV, -1, PACK_COLS),
    ]
    return jnp.concatenate(parts, axis=1)


SMALL = ("norm1_w", "norm2_w", "q_norm_w", "k_norm_w", "conv_w", "conv_b", "A_log", "dt_bias", "ssd_D", "ssd_norm_w")


def kernel(x, c, w_ada, b_ada, norm1_w, norm2_w, w_in, q_norm_w, k_norm_w, conv_w, conv_b, A_log, dt_bias, ssd_D, ssd_norm_w, w_attn_out, w_ssd_out, w_o, w_mlp1, w_mlp2, loss_target, m_w_ada, m_b_ada, m_norm1_w, m_norm2_w, m_w_in, m_q_norm_w, m_k_norm_w, m_conv_w, m_conv_b, m_A_log, m_dt_bias, m_ssd_D, m_ssd_norm_w, m_w_attn_out, m_w_ssd_out, m_w_o, m_w_mlp1, m_w_mlp2, v_w_ada, v_b_ada, v_norm1_w, v_norm2_w, v_w_in, v_q_norm_w, v_k_norm_w, v_conv_w, v_conv_b, v_A_log, v_dt_bias, v_ssd_D, v_ssd_norm_w, v_w_attn_out, v_w_ssd_out, v_w_o, v_w_mlp1, v_w_mlp2):
    args = dict(locals())
    me = _my_index()
    n_ada = 6 * D_MODEL // N_DEV
    n_cw = CONV_DIM // N_DEV

    blk = jnp.zeros((8, D_MODEL), f32)
    blk = blk.at[0:1, :].set(c)
    blk = blk.at[1:1 + D_CONV, :n_cw].set(conv_w[0])
    g0 = _all_gather(blk, "gather_c_convw", in_vmem=True)
    c_all = g0[:, 0, :]
    conv_w_full = g0[:, 1:1 + D_CONV, :n_cw].transpose(1, 0, 2).reshape(D_CONV, CONV_DIM)

    b_shard = lax.dynamic_slice(b_ada, (0, me * n_ada), (1, n_ada))
    mod_cols = _ada_fwd(c_all, w_ada[0], b_shard)
    g1 = _all_gather(mod_cols, "gather_mod", in_vmem=True)
    mod_mine = lax.dynamic_index_in_dim(g1, me, axis=1, keepdims=False)
    mod = mod_mine.reshape(6, 1, D_MODEL)

    big_shapes = [args[n].shape[1:] for n in BIG]
    late_shard = _pack_big([args[n][0] for n in BIG], bf16)
    wfull = _split_w_in(_all_gather(w_in[0].astype(bf16), "gather_w_in", in_vmem=False))
    recv_in_like = jnp.zeros((N_DEV,) + w_in.shape[1:], bf16)
    recv_late_like = jnp.zeros((N_DEV,) + late_shard.shape, bf16)

    small = {"norm1_w": norm1_w, "norm2_w": norm2_w, "q_norm_w": q_norm_w, "k_norm_w": k_norm_w,
             "conv_w": conv_w_full, "conv_b": conv_b, "A_log": A_log[0], "dt_bias": dt_bias[0], "ssd_D": ssd_D,
             "ssd_norm_w": ssd_norm_w}

    loss, (gx, gmod, gsmall, recv_in, recv_late) = jax.value_and_grad(local_loss, argnums=(0, 1, 2, 3, 4))(
        x[0], mod, small, recv_in_like, recv_late_like, wfull, late_shard, loss_target[0])

    small_list = [gmod, gsmall["norm1_w"], gsmall["norm2_w"], gsmall["q_norm_w"], gsmall["k_norm_w"], gsmall["conv_w"],
                  gsmall["conv_b"], gsmall["A_log"], gsmall["dt_bias"], gsmall["ssd_D"], gsmall["ssd_norm_w"],
                  loss.reshape(1)]
    small_shapes = [a.shape for a in small_list]
    g2 = _all_gather(_pack_small(small_list), "gather_small_grads", in_vmem=True)
    summed = _unpack_small(_sum_over_mesh(g2), small_shapes)
    loss_total = summed[-1][0]
    g_b_ada = summed[0].reshape(1, 6 * D_MODEL)
    g_small = dict(zip(SMALL, summed[1:-1]))
    g_conv_w = lax.dynamic_slice(g_small["conv_w"], (0, me * n_cw), (D_CONV, n_cw))

    dmod_all = g2[:, :6 * D_MODEL // LANE, :].reshape(N_DEV, 6 * D_MODEL)
    dmod_shard = lax.dynamic_slice(dmod_all, (0, me * n_ada), (N_DEV, n_ada))
    ada = _ada_bwd_adamw(c_all, dmod_shard, w_ada[0], m_w_ada[0], v_w_ada[0])

    small_grads = {"b_ada": g_b_ada, "norm1_w": g_small["norm1_w"], "norm2_w": g_small["norm2_w"],
                   "q_norm_w": g_small["q_norm_w"], "k_norm_w": g_small["k_norm_w"], "conv_w": g_conv_w[None],
                   "conv_b": g_small["conv_b"], "A_log": g_small["A_log"][None], "dt_bias": g_small["dt_bias"][None],
                   "ssd_D": g_small["ssd_D"], "ssd_norm_w": g_small["ssd_norm_w"]}
    sm_names = list(small_grads)
    sm_shapes = [args[n].shape for n in sm_names]
    sm = _adamw_small(_pack_small([args[n] for n in sm_names]), _pack_small([small_grads[n] for n in sm_names]),
                      _pack_small([args["m_" + n] for n in sm_names]), _pack_small([args["v_" + n] for n in sm_names]))
    sm_delta, sm_m, sm_v = [dict(zip(sm_names, _unpack_small(t, sm_shapes))) for t in sm]
    small_grads = {n: small_grads[n].reshape(args[n].shape) for n in sm_names}

    w_in_out = _sum_adamw(recv_in, w_in[0], m_w_in[0], v_w_in[0], "sum_adamw_w_in")
    big = _sum_adamw(recv_late, _pack_big([args[n][0] for n in BIG], f32),
                     _pack_big([args["m_" + n][0] for n in BIG], f32),
                     _pack_big([args["v_" + n][0] for n in BIG], f32), "sum_adamw")
    big_g, big_delta, big_m, big_v = [dict(zip(BIG, [t[None] for t in _unpack_big(p, big_shapes)])) for p in big]
    big_g["w_in"], big_delta["w_in"], big_m["w_in"], big_v["w_in"] = [t[None] for t in w_in_out]

    names = ("w_ada", "b_ada", "norm1_w", "norm2_w", "w_in", "q_norm_w", "k_norm_w", "conv_w", "conv_b", "A_log",
             "dt_bias", "ssd_D", "ssd_norm_w", "w_attn_out", "w_ssd_out", "w_o", "w_mlp1", "w_mlp2")
    grads, deltas, new_m, new_v = {}, {}, {}, {}
    for n in names:
        if n == "w_ada":
            grads[n], deltas[n], new_m[n], new_v[n] = [t[None] for t in ada]
        elif n in big_g:
            grads[n], deltas[n], new_m[n], new_v[n] = big_g[n], big_delta[n], big_m[n], big_v[n]
        else:
            grads[n], deltas[n], new_m[n], new_v[n] = small_grads[n], sm_delta[n], sm_m[n], sm_v[n]
    return (loss_total, gx[None], *[grads[n] for n in names], *[deltas[n] for n in names],
            *[new_m[n] for n in names], *[new_v[n] for n in names])
```

```python
import functools
import math

import jax
import jax.numpy as jnp
import numpy as np
from jax import lax
from jax.experimental import pallas as pl
from jax.experimental.pallas import tpu as pltpu

f32 = jnp.float32
bf16 = jnp.bfloat16
HIGHEST = lax.Precision.HIGHEST
MESH = pl.DeviceIdType.MESH

N_DEV = 8
D_MODEL = 1024
GRID_W = 64
N_Q_HEADS = 16
N_KV_HEADS = 4
HEAD_DIM = 64
ROPE_THETA = 10000.0
D_INNER = 2048
SSD_HEAD_DIM = 64
N_SSD_HEADS = 32
N_SSD_GROUPS = 4
D_STATE = 128
D_CONV = 5
CHUNK = 128
D_FF = 4096
EPS = 1e-6
CONV_DIM = D_INNER + 2 * N_SSD_GROUPS * D_STATE
GN = N_SSD_GROUPS * D_STATE
PROJ_NAMES = ("q", "k", "v", "xbc", "z", "dt", "ga", "gs")
PROJ_SIZES = (N_Q_HEADS * HEAD_DIM, N_KV_HEADS * HEAD_DIM, N_KV_HEADS * HEAD_DIM, CONV_DIM, D_INNER,
              2 * N_SSD_HEADS, D_MODEL, D_MODEL)
D_IN_PROJ = sum(PROJ_SIZES)
PROJ_DTYPES = (jnp.bfloat16, jnp.bfloat16, jnp.bfloat16, jnp.float32, jnp.bfloat16, jnp.float32, jnp.bfloat16,
               jnp.bfloat16)
DT_PAD = 128

ADAM_LR, ADAM_B1, ADAM_B2, ADAM_EPS, ADAM_WD, ADAM_STEP = 0.001, 0.9, 0.999, 1e-08, 0.01, 10

V7X_VMEM_LIMIT = 56 * 1024 * 1024
LANE = 128
PACK_COLS = 1024


def _cparams(**kw):
    return pltpu.CompilerParams(vmem_limit_bytes=V7X_VMEM_LIMIT, **kw)


def _pick(dim, prefs):
    for p in prefs:
        if dim % p == 0:
            return p
    return dim


def _my_index():
    return 4 * lax.axis_index("x") + 2 * lax.axis_index("y") + lax.axis_index("c")


COMM_SEMS = [pltpu.SemaphoreType.DMA((7,)), pltpu.SemaphoreType.DMA((7,)), pltpu.SemaphoreType.DMA]


def _gather_phases(x_ref, out_ref, send_sems, recv_sems, local_sem):
    x, y, cc = lax.axis_index("x"), lax.axis_index("y"), lax.axis_index("c")
    me, sibling = (x, y, cc), (x, y, 1 - cc)
    chips = [(1 - x, y), (x, 1 - y), (1 - x, 1 - y)]

    def slot(px, py, pc):
        return out_ref.at[4 * px + 2 * py + pc]

    def copy(k, blk, to, src=None):
        return pltpu.make_async_remote_copy(
            src_ref=slot(*blk) if src is None else src, dst_ref=slot(*blk),
            send_sem=send_sems.at[k], recv_sem=recv_sems.at[k], device_id=to, device_id_type=MESH)

    mine = pltpu.make_async_copy(x_ref, slot(*me), local_sem)
    first = [copy(0, me, sibling, src=x_ref)]
    first += [copy(1 + j, me, (*chip, cc), src=x_ref) for j, chip in enumerate(chips)]
    passed = [copy(4 + j, (*chip, cc), sibling) for j, chip in enumerate(chips)]

    def start():
        mine.start()
        for cp in first:
            cp.start()

    def finish():
        for j, chip in enumerate(chips):
            copy(1 + j, (*chip, cc), me).wait_recv()
            passed[j].start()
        copy(0, sibling, me).wait_recv()
        for j, chip in enumerate(chips):
            copy(4 + j, (*chip, 1 - cc), me).wait_recv()
        for cp in first + passed:
            cp.wait_send()
        mine.wait()

    return start, finish


def _scatter_phases(g_ref, out_ref, send_sems, recv_sems, local_sem):
    x, y, cc = lax.axis_index("x"), lax.axis_index("y"), lax.axis_index("c")
    me = 4 * x + 2 * y + cc
    mine = pltpu.make_async_copy(g_ref.at[me], out_ref.at[me], local_sem)

    def copy(k):
        fx, fy, fc = (k >> 2) & 1, (k >> 1) & 1, k & 1
        px = x + fx - 2 * x * fx
        py = y + fy - 2 * y * fy
        pc = cc + fc - 2 * cc * fc
        peer = 4 * px + 2 * py + pc
        send = pltpu.make_async_remote_copy(
            src_ref=g_ref.at[peer], dst_ref=out_ref.at[me],
            send_sem=send_sems.at[k - 1], recv_sem=recv_sems.at[k - 1],
            device_id=(px, py, pc), device_id_type=MESH)
        recv = pltpu.make_async_remote_copy(
            src_ref=g_ref.at[peer], dst_ref=out_ref.at[peer],
            send_sem=send_sems.at[k - 1], recv_sem=recv_sems.at[k - 1],
            device_id=(px, py, pc), device_id_type=MESH)
        return send, recv

    pairs = [copy(k) for k in range(1, N_DEV)]

    def start():
        mine.start()
        for send, _ in pairs:
            send.start()

    def finish():
        for _, recv in pairs:
            recv.wait_recv()
        for send, _ in pairs:
            send.wait_send()
        mine.wait()

    return start, finish


def _all_gather(block, name, in_vmem):
    r, c = block.shape

    def body(x_ref, out_ref, send_sems, recv_sems, local_sem):
        start, finish = _gather_phases(x_ref, out_ref, send_sems, recv_sems, local_sem)
        start()
        finish()

    space = pltpu.VMEM if in_vmem else pl.ANY
    return pl.pallas_call(
        body, name=name,
        out_shape=jax.ShapeDtypeStruct((N_DEV, r, c), block.dtype),
        in_specs=[pl.BlockSpec(memory_space=space)],
        out_specs=pl.BlockSpec(memory_space=space),
        scratch_shapes=[pltpu.SemaphoreType.DMA((7,)), pltpu.SemaphoreType.DMA((7,)), pltpu.SemaphoreType.DMA],
    )(block)


def _scatter_blocks(g, name):
    _, r, c = g.shape

    def body(g_ref, out_ref, send_sems, recv_sems, local_sem):
        start, finish = _scatter_phases(g_ref, out_ref, send_sems, recv_sems, local_sem)
        start()
        finish()

    return pl.pallas_call(
        body, name=name,
        out_shape=jax.ShapeDtypeStruct(g.shape, g.dtype),
        in_specs=[pl.BlockSpec(memory_space=pl.ANY)],
        out_specs=pl.BlockSpec(memory_space=pl.ANY),
        scratch_shapes=[pltpu.SemaphoreType.DMA((7,)), pltpu.SemaphoreType.DMA((7,)), pltpu.SemaphoreType.DMA],
    )(g)


_DIMS = {"nn": (((1,), (0,)), ((), ())), "nt": (((1,), (1,)), ((), ())), "tn": (((0,), (0,)), ((), ()))}


def _matmul(a, b, mode, out_dtype, name, epilogue=None, side=None):
    if mode == "nn":
        (m, k), (_, n) = a.shape, b.shape
    elif mode == "nt":
        (m, k), (n, _) = a.shape, b.shape
    else:
        (k, m), (_, n) = a.shape, b.shape
    tm = _pick(m, (1024, 512, 256, 128))
    if mode == "tn":
        tn = _pick(n, (1536, 1024, 512, 256, 128))
        tk = _pick(k, (2048, 1024, 512, 256, 128)) if b.dtype == bf16 else _pick(k, (1024, 512, 256, 128))
    else:
        tn = _pick(n, (1024, 512, 384, 256, 128))
        tk = _pick(k, (2048, 1024, 512, 256, 128)) if a.dtype == bf16 else _pick(k, (1024, 512, 256, 128))
    nk = k // tk
    dims = _DIMS[mode]
    n_in = 3 if epilogue == "drelu2" else 2

    def body(*refs):
        a_ref, b_ref = refs[:2]
        o_ref, acc_ref = refs[n_in], refs[n_in + 1]
        kk = pl.program_id(2)
        part = lax.dot_general(a_ref[...].astype(bf16), b_ref[...].astype(bf16), dims, preferred_element_type=f32)

        def finish(acc):
            if epilogue == "relu2":
                r = jnp.maximum(acc, 0.0)
                o_ref[...] = (r * r).astype(out_dtype)
            elif epilogue == "drelu2":
                o_ref[...] = (acc * (2.0 * jnp.sqrt(refs[2][...].astype(f32)))).astype(out_dtype)
            else:
                o_ref[...] = acc.astype(out_dtype)

        if nk == 1:
            finish(part)
        else:
            @pl.when(kk == 0)
            def _():
                acc_ref[...] = part

            @pl.when(kk > 0)
            def _():
                acc_ref[...] += part

            @pl.when(kk == nk - 1)
            def _():
                finish(acc_ref[...])

    if mode == "tn":
        a_spec = pl.BlockSpec((tk, tm), lambda i, j, kk: (kk, i))
    else:
        a_spec = pl.BlockSpec((tm, tk), lambda i, j, kk: (i, kk))
    if mode == "nt":
        b_spec = pl.BlockSpec((tn, tk), lambda i, j, kk: (j, kk))
    else:
        b_spec = pl.BlockSpec((tk, tn), lambda i, j, kk: (kk, j))
    o_spec = pl.BlockSpec((tm, tn), lambda i, j, kk: (i, j))
    o_shape = jax.ShapeDtypeStruct((m, n), out_dtype)
    return pl.pallas_call(
        body, name=name, grid=(m // tm, n // tn, nk),
        in_specs=[a_spec, b_spec] + ([o_spec] if epilogue == "drelu2" else []),
        out_specs=o_spec, out_shape=o_shape,
        scratch_shapes=[pltpu.VMEM((tm, tn), f32)],
        compiler_params=_cparams(dimension_semantics=("parallel", "parallel", "arbitrary")),
    )(*((a, b, side) if epilogue == "drelu2" else (a, b)))


@jax.custom_vjp
def mlp(h, w1, w1grad, w2, w2grad):
    r = _matmul(h, w1, "nn", bf16, "mlp1_fwd", epilogue="relu2")
    return _matmul(r, w2, "nn", f32, "mlp2_fwd")


def _mlp_fwd(h, w1, w1grad, w2, w2grad):
    r = _matmul(h, w1, "nn", bf16, "mlp1_fwd", epilogue="relu2")
    return _matmul(r, w2, "nn", f32, "mlp2_fwd"), (h, w1, w2, r)


def _mlp_bwd(res, dy):
    h, w1, w2, r = res
    du = _matmul(dy, w2, "nt", bf16, "mlp2_dgrad", epilogue="drelu2", side=r)
    dw2 = _matmul(r, dy, "tn", f32, "mlp2_wgrad")
    dh = _matmul(du, w1, "nt", h.dtype, "mlp1_dgrad")
    dw1 = _matmul(h, du, "tn", f32, "mlp1_wgrad")
    return dh, jnp.zeros_like(w1), dw1, jnp.zeros_like(w2), dw2


mlp.defvjp(_mlp_fwd, _mlp_bwd)


def make_linear(name):
    @jax.custom_vjp
    def linear(a, w, wgrad):
        return _matmul(a, w, "nn", f32, name + "_fwd")

    def fwd(a, w, wgrad):
        return linear(a, w, wgrad), (a, w)

    def bwd(res, dy):
        a, w = res
        da = _matmul(dy, w, "nt", a.dtype, name + "_dgrad")
        dw = _matmul(a, dy, "tn", f32, name + "_wgrad")
        return da, jnp.zeros_like(w), dw

    linear.defvjp(fwd, bwd)
    return linear


def _in_proj_dgrad(dys, ws, g):
    s, d = dys[0].shape[0], ws[0].shape[0]
    tm = _pick(s, (1024, 512, 256, 128))
    tks = [w.shape[1] if w.shape[1] <= 1024 else 512 for w in ws]
    steps = [w.shape[1] // tk for w, tk in zip(ws, tks)]
    starts = [sum(steps[:p]) for p in range(len(ws))]
    total = sum(steps)
    n_p, n_i = len(ws), s // tm
    assert steps[0] == 1

    def body(*refs):
        dy_refs, w_refs, g_ref = refs[:n_p], refs[n_p:2 * n_p], refs[2 * n_p]
        dh_ref, recv_ref, acc_ref, send_sems, recv_sems, local_sem = refs[2 * n_p + 1:]
        i, t = pl.program_id(0), pl.program_id(1)
        start, finish = _scatter_phases(g_ref, recv_ref, send_sems, recv_sems, local_sem)

        @pl.when((i == 0) & (t == 0))
        def _():
            start()

        for p in range(n_p):
            @pl.when((t >= starts[p]) & (t < starts[p] + steps[p]))
            def _(p=p):
                part = lax.dot_general(dy_refs[p][...].astype(bf16), w_refs[p][...], _DIMS["nt"],
                                       preferred_element_type=f32)
                if p == 0:
                    acc_ref[...] = part
                else:
                    acc_ref[...] += part

        @pl.when(t == total - 1)
        def _():
            dh_ref[...] = acc_ref[...].astype(dh_ref.dtype)

        @pl.when((i == n_i - 1) & (t == total - 1))
        def _():
            finish()

    def piece_map(p, rows):
        def index_map(i, t):
            blk = jnp.clip(t - starts[p], 0, steps[p] - 1)
            return (i, blk) if rows else (0, blk)

        return index_map

    hbm = pl.BlockSpec(memory_space=pl.ANY)
    in_specs = [pl.BlockSpec((tm, tks[p]), piece_map(p, True)) for p in range(n_p)]
    in_specs += [pl.BlockSpec((d, tks[p]), piece_map(p, False)) for p in range(n_p)]
    return pl.pallas_call(
        body, name="in_proj_dgrad", grid=(n_i, total), in_specs=in_specs + [hbm],
        out_specs=[pl.BlockSpec((tm, d), lambda i, t: (i, 0)), hbm],
        out_shape=[jax.ShapeDtypeStruct((s, d), bf16), jax.ShapeDtypeStruct(g.shape, g.dtype)],
        scratch_shapes=[pltpu.VMEM((tm, d), f32)] + COMM_SEMS,
        compiler_params=_cparams(dimension_semantics=("arbitrary", "arbitrary")),
    )(*dys, *ws, g)


@jax.custom_vjp
def in_proj(h, ws, recv_like):
    return tuple(_matmul(h, w, "nn", dt, "lin_" + n + "_fwd") for n, w, dt in zip(PROJ_NAMES, ws, PROJ_DTYPES))


def _in_proj_fwd(h, ws, recv_like):
    return in_proj(h, ws, recv_like), (h, ws)


def _in_proj_bwd(res, dys):
    h, ws = res
    dws = {n: _matmul(h, dy, "tn", f32, "lin_" + n + "_wgrad") for n, dy in zip(PROJ_NAMES, dys)}
    dh, recv = _in_proj_dgrad(dys, ws, _pack_in_grads(dws))
    return dh.astype(h.dtype), tuple(jnp.zeros_like(w) for w in ws), recv


in_proj.defvjp(_in_proj_fwd, _in_proj_bwd)


def make_rowwise(name, fn, row_out, sum_out=(), tm_pref=512):
    def specs(rows, gpars, cpars, consts, tm):
        s = [pl.BlockSpec((tm, r.shape[1]), lambda i: (i, 0)) for r in rows]
        s += [pl.BlockSpec(p.shape, lambda i: (0, 0)) for p in gpars]
        s += [pl.BlockSpec(p.shape, lambda i: (0, 0)) for p in cpars]
        for cst in consts:
            nb = cst.shape[0] // tm
            s.append(pl.BlockSpec((tm, cst.shape[1]), lambda i, nb=nb: (i % nb, 0)))
        return s

    def tile_rows(rows, consts):
        r = rows[0].shape[0]
        common = math.gcd(r, *[cst.shape[0] for cst in consts])
        tm = _pick(common, (tm_pref, 512, 256, 128, 64, 32, 16, 8))
        return r, tm

    def forward(rows, gpars, cpars, consts):
        r, tm = tile_rows(rows, consts)
        nr, ng, nc, nk = len(rows), len(gpars), len(cpars), len(consts)

        def body(*refs):
            ins = refs[:nr + ng + nc + nk]
            outs = refs[nr + ng + nc + nk:]
            rv = [t[...].astype(f32) for t in ins[:nr]]
            gv = [t[...].astype(f32) for t in ins[nr:nr + ng]]
            cv = [t[...] for t in ins[nr + ng:nr + ng + nc]]
            kv = [t[...].astype(f32) for t in ins[nr + ng + nc:]]
            ro, so = fn(rv, gv, cv, kv)
            for o_ref, val in zip(outs[:len(row_out)], ro):
                o_ref[...] = val.astype(o_ref.dtype)
            if sum_out:
                @pl.when(pl.program_id(0) == 0)
                def _():
                    for o_ref in outs[len(row_out):]:
                        o_ref[...] = jnp.zeros_like(o_ref)
                for o_ref, val in zip(outs[len(row_out):], so):
                    o_ref[...] += val

        out_specs = [pl.BlockSpec((tm, w), lambda i: (i, 0)) for w, _ in row_out]
        out_specs += [pl.BlockSpec(shp, lambda i: (0, 0)) for shp in sum_out]
        out_shape = [jax.ShapeDtypeStruct((r, w), dt) for w, dt in row_out]
        out_shape += [jax.ShapeDtypeStruct(shp, f32) for shp in sum_out]
        res = pl.pallas_call(
            body, name=name + "_fwd", grid=(r // tm,),
            in_specs=specs(rows, gpars, cpars, consts, tm), out_specs=out_specs, out_shape=out_shape,
            compiler_params=_cparams(dimension_semantics=("arbitrary",)),
        )(*rows, *gpars, *cpars, *consts)
        return tuple(res[:len(row_out)]), tuple(res[len(row_out):])

    def backward(rows, gpars, cpars, consts, d_ro, d_so):
        r, tm = tile_rows(rows, consts)
        nr, ng, nc, nk = len(rows), len(gpars), len(cpars), len(consts)
        n_in = nr + ng + nc + nk + len(row_out) + len(sum_out)

        def body(*refs):
            ins, outs = refs[:n_in], refs[n_in:]
            rv = [t[...].astype(f32) for t in ins[:nr]]
            gv = [t[...].astype(f32) for t in ins[nr:nr + ng]]
            cv = [t[...] for t in ins[nr + ng:nr + ng + nc]]
            kv = [t[...].astype(f32) for t in ins[nr + ng + nc:nr + ng + nc + nk]]
            o = nr + ng + nc + nk
            dro = [t[...].astype(f32) for t in ins[o:o + len(row_out)]]
            dso = [t[...] for t in ins[o + len(row_out):]]
            _, vjp = jax.vjp(lambda a, b: tuple(tuple(t) for t in fn(a, b, cv, kv)), rv, gv)
            drv, dgv = vjp((tuple(dro), tuple(dso)))
            for o_ref, val in zip(outs[:nr], drv):
                o_ref[...] = val.astype(o_ref.dtype)
            if ng:
                @pl.when(pl.program_id(0) == 0)
                def _():
                    for o_ref in outs[nr:]:
                        o_ref[...] = jnp.zeros_like(o_ref)
                for o_ref, val in zip(outs[nr:], dgv):
                    o_ref[...] += val

        in_specs = specs(rows, gpars, cpars, consts, tm)
        in_specs += [pl.BlockSpec((tm, w), lambda i: (i, 0)) for w, _ in row_out]
        in_specs += [pl.BlockSpec(shp, lambda i: (0, 0)) for shp in sum_out]
        out_specs = [pl.BlockSpec((tm, t.shape[1]), lambda i: (i, 0)) for t in rows]
        out_specs += [pl.BlockSpec(p.shape, lambda i: (0, 0)) for p in gpars]
        out_shape = [jax.ShapeDtypeStruct(t.shape, t.dtype) for t in rows]
        out_shape += [jax.ShapeDtypeStruct(p.shape, f32) for p in gpars]
        res = pl.pallas_call(
            body, name=name + "_bwd", grid=(r // tm,),
            in_specs=in_specs, out_specs=out_specs, out_shape=out_shape,
            compiler_params=_cparams(dimension_semantics=("arbitrary",)),
        )(*rows, *gpars, *cpars, *consts, *d_ro, *d_so)
        return tuple(res[:nr]), tuple(res[nr:])

    @jax.custom_vjp
    def op(rows, gpars, cpars, consts):
        return forward(rows, gpars, cpars, consts)

    def op_fwd(rows, gpars, cpars, consts):
        return forward(rows, gpars, cpars, consts), (rows, gpars, cpars, consts)

    def op_bwd(res, cts):
        rows, gpars, cpars, consts = res
        d_ro, d_so = cts
        drows, dg = backward(rows, gpars, cpars, consts, d_ro, d_so)
        dg = tuple(d.astype(p.dtype) for d, p in zip(dg, gpars))
        return (drows, dg, tuple(jnp.zeros_like(p) for p in cpars), tuple(jnp.zeros_like(k) for k in consts))

    op.defvjp(op_fwd, op_bwd)
    return op


def _rms(x):
    return x * lax.rsqrt(jnp.mean(x * x, axis=-1, keepdims=True) + EPS)


def _silu(x):
    return x * jax.nn.sigmoid(x)


def _fn_norm_mod(rows, gp, cp, ks):
    (x,), (nw, sc, sh) = rows, gp
    return ((_rms(x) * nw) * (1.0 + sc) + sh, x), ()


PAIR = 2 * HEAD_DIM


def _exact_dot(a, m):
    hi = a.astype(bf16)
    lo = (a - hi.astype(f32)).astype(bf16)
    return jnp.dot(hi, m, preferred_element_type=f32) + jnp.dot(lo, m, preferred_element_type=f32)


def _make_sel_dot(sign):
    @jax.custom_vjp
    def sel_dot(a, m):
        return _exact_dot(a, m)

    def fwd(a, m):
        return _exact_dot(a, m), m

    def bwd(m, g):
        return sign * _exact_dot(g, m), jnp.zeros_like(m)

    sel_dot.defvjp(fwd, bwd)
    return sel_dot


_head_sum_dot = _make_sel_dot(1.0)
_rope_perm_dot = _make_sel_dot(-1.0)


def _pair_norm_rope(t, w2, gsum, perm, cos2, sin2, out_scale):
    ss = _head_sum_dot(t * t, gsum)
    u = t * lax.rsqrt(ss * (1.0 / HEAD_DIM) + EPS) * w2
    return (u * cos2 + _rope_perm_dot(u, perm) * sin2) * out_scale


def _pair_consts():
    eye = np.eye(2, dtype=np.float32)
    gsum = np.kron(eye, np.ones((HEAD_DIM, HEAD_DIM), np.float32))
    return jnp.asarray(gsum, bf16), jnp.asarray(np.kron(eye, _rope_perm()), bf16)


def make_head_rope(name, nh, out_scale, head_major):
    width = nh * HEAD_DIM
    fn = functools.partial(_pair_norm_rope, out_scale=out_scale)

    def out_spec(tm):
        if head_major:
            return pl.BlockSpec((nh, tm, HEAD_DIM), lambda i: (0, i, 0))
        return pl.BlockSpec((tm, width), lambda i: (i, 0))

    def specs(tm):
        def full(shp):
            return pl.BlockSpec(shp, lambda i: (0, 0))

        return [pl.BlockSpec((tm, width), lambda i: (i, 0)), full((1, PAIR)), full((PAIR, PAIR)), full((PAIR, PAIR)),
                pl.BlockSpec((tm, PAIR), lambda i: (i, 0)), pl.BlockSpec((tm, PAIR), lambda i: (i, 0))]

    def forward(t, w2, gsum, perm, cos2, sin2):
        s = t.shape[0]
        tm = _pick(s, (1024, 512, 256, 128))

        def body(t_ref, w_ref, g_ref, p_ref, cos_ref, sin_ref, o_ref):
            for b in range(nh // 2):
                val = fn(t_ref[:, b * PAIR:(b + 1) * PAIR].astype(f32), w_ref[...], g_ref[...], p_ref[...], cos_ref[...],
                         sin_ref[...]).astype(o_ref.dtype)
                if head_major:
                    o_ref[2 * b] = val[:, :HEAD_DIM]
                    o_ref[2 * b + 1] = val[:, HEAD_DIM:]
                else:
                    o_ref[:, b * PAIR:(b + 1) * PAIR] = val

        return pl.pallas_call(
            body, name=name + "_fwd", grid=(s // tm,), in_specs=specs(tm), out_specs=out_spec(tm),
            out_shape=jax.ShapeDtypeStruct((nh, s, HEAD_DIM) if head_major else (s, width), bf16),
            compiler_params=_cparams(dimension_semantics=("arbitrary",)),
        )(t, w2, gsum, perm, cos2, sin2)

    def backward(t, w2, gsum, perm, cos2, sin2, dout):
        s = t.shape[0]
        tm = _pick(s, (1024, 512, 256, 128))

        def body(t_ref, w_ref, g_ref, p_ref, cos_ref, sin_ref, do_ref, dt_ref, dw_ref, pair_buf):
            @pl.when(pl.program_id(0) == 0)
            def _():
                dw_ref[...] = jnp.zeros_like(dw_ref)

            g_v, p_v, cos_v, sin_v = g_ref[...], p_ref[...], cos_ref[...], sin_ref[...]
            dw = jnp.zeros((1, PAIR), f32)
            for b in range(nh // 2):
                sl = slice(b * PAIR, (b + 1) * PAIR)
                if head_major:
                    pair_buf[:, :HEAD_DIM] = do_ref[2 * b].astype(f32)
                    pair_buf[:, HEAD_DIM:] = do_ref[2 * b + 1].astype(f32)
                    ct = pair_buf[...]
                else:
                    ct = do_ref[:, sl].astype(f32)
                _, vjp = jax.vjp(lambda a, c: fn(a, c, g_v, p_v, cos_v, sin_v), t_ref[:, sl].astype(f32), w_ref[...])
                dtb, dwb = vjp(ct)
                dt_ref[:, sl] = dtb.astype(dt_ref.dtype)
                dw = dw + dwb
            dw_ref[...] += dw

        return pl.pallas_call(
            body, name=name + "_bwd", grid=(s // tm,), in_specs=specs(tm) + [out_spec(tm)],
            out_specs=[pl.BlockSpec((tm, width), lambda i: (i, 0)), pl.BlockSpec((1, PAIR), lambda i: (0, 0))],
            out_shape=[jax.ShapeDtypeStruct((s, width), t.dtype), jax.ShapeDtypeStruct((1, PAIR), f32)],
            scratch_shapes=[pltpu.VMEM((tm, PAIR), f32)],
            compiler_params=_cparams(dimension_semantics=("arbitrary",)),
        )(t, w2, gsum, perm, cos2, sin2, dout)

    @jax.custom_vjp
    def op(t, w2, gsum, perm, cos2, sin2):
        return forward(t, w2, gsum, perm, cos2, sin2)

    def op_fwd(*args):
        return forward(*args), args

    def op_bwd(res, dout):
        dt, dw = backward(*res, dout)
        return (dt, dw) + tuple(jnp.zeros_like(r) for r in res[2:])

    op.defvjp(op_fwd, op_bwd)

    def apply(t, w, cos2, sin2):
        gsum, perm = _pair_consts()
        return op(t, jnp.concatenate([w, w], axis=-1), gsum, perm, cos2, sin2)

    return apply


def _fn_softplus(rows, gp, cp, ks):
    (x,), (b,) = rows, gp
    v = x + b
    return (jnp.maximum(v, 0.0) + jnp.log(1.0 + jnp.exp(-jnp.abs(v))),), ()


def _fn_ssd_gate(rows, gp, cp, ks):
    (y, z), (nw,) = rows, gp
    return (_rms(y * _silu(z)) * nw,), ()


def _fn_merge(rows, gp, cp, ks):
    ao, so, ga, gs = rows
    return (jax.nn.sigmoid(ga) * ao + jax.nn.sigmoid(gs) * so,), ()


def _fn_res_norm(rows, gp, cp, ks):
    (x, mo), (g1, nw, sc, sh) = rows, gp
    x1 = x + g1 * mo
    return (x1, (_rms(x1) * nw) * (1.0 + sc) + sh), ()


def _fn_loss(rows, gp, cp, ks):
    (x1, ff), (g2,), (tgt,) = rows, gp, ks
    err = x1 + g2 * ff - tgt
    return (), (0.5 * jnp.sum(jnp.sum(err * err, axis=-1, keepdims=True), axis=0, keepdims=True) / D_MODEL,)


HALO = 8
HALO_BWD = 16


def _conv_tiles(s, c, wide):
    return _pick(s, (512, 256, 128)), _pick(c, (1024, 512, 256, 128) if wide else (512, 256, 128))


def _halo_specs(tm, tc, s, halo=HALO):
    nb = tm // halo
    last = s // halo - 1
    cur = pl.BlockSpec((tm, tc), lambda j, i: (i, j))
    prev = pl.BlockSpec((halo, tc), lambda j, i: (jnp.maximum(i * nb - 1, 0), j))
    nxt = pl.BlockSpec((halo, tc), lambda j, i: (jnp.minimum((i + 1) * nb, last), j))
    return cur, prev, nxt


def _fill_halo(buf, cur, prev, nxt, tm, i, n_i, halo=HALO):
    buf[halo:halo + tm, :] = cur[...]
    buf[0:halo, :] = jnp.where(i > 0, prev[...], 0.0)
    buf[halo + tm:, :] = jnp.where(i < n_i - 1, nxt[...], 0.0)


def _conv_fwd(x, w, b, shard):
    s, c = x.shape
    tm, tc = _conv_tiles(s, c, True)
    n_i, n_j = s // tm, c // tc

    def body(cur, prev, nxt, w_ref, b_ref, shard_ref, o_ref, gath_ref, buf, send_sems, recv_sems, local_sem):
        j, i = pl.program_id(0), pl.program_id(1)
        start, finish = _gather_phases(shard_ref, gath_ref, send_sems, recv_sems, local_sem)

        @pl.when((j == 0) & (i == 0))
        def _():
            start()

        _fill_halo(buf, cur, prev, nxt, tm, i, n_i)
        pre = jnp.zeros((tm, tc), f32) + b_ref[...]
        for k in range(D_CONV):
            pre = pre + buf[HALO - 2 + k:HALO - 2 + k + tm, :] * w_ref[k:k + 1, :]
        o_ref[...] = _silu(pre)

        @pl.when((j == n_j - 1) & (i == n_i - 1))
        def _():
            finish()

    cur, prev, nxt = _halo_specs(tm, tc, s)
    hbm = pl.BlockSpec(memory_space=pl.ANY)
    return pl.pallas_call(
        body, name="conv_silu_fwd", grid=(n_j, n_i),
        in_specs=[cur, prev, nxt, pl.BlockSpec((D_CONV, tc), lambda j, i: (0, j)),
                  pl.BlockSpec((1, tc), lambda j, i: (0, j)), hbm],
        out_specs=[pl.BlockSpec((tm, tc), lambda j, i: (i, j)), hbm],
        out_shape=[jax.ShapeDtypeStruct((s, c), f32), jax.ShapeDtypeStruct((N_DEV,) + shard.shape, shard.dtype)],
        scratch_shapes=[pltpu.VMEM((tm + 2 * HALO, tc), f32)] + COMM_SEMS,
        compiler_params=_cparams(dimension_semantics=("arbitrary", "arbitrary")),
    )(x, x, x, w, b, shard)


def _conv_bwd(x, w, b, dy, g):
    s, c = x.shape
    tm, tc = _conv_tiles(s, c, False)
    n_i, n_j = s // tm, c // tc
    ext = tm + 16

    def body(cur, prev, nxt, dcur, dprev, dnxt, w_ref, b_ref, g_ref, dx_ref, dw_ref, db_ref, recv_ref,
             xbuf, dbuf, pbuf, send_sems, recv_sems, local_sem):
        j, i = pl.program_id(0), pl.program_id(1)
        start, finish = _scatter_phases(g_ref, recv_ref, send_sems, recv_sems, local_sem)

        @pl.when((j == 0) & (i == 0))
        def _():
            start()

        _fill_halo(xbuf, cur, prev, nxt, tm, i, n_i, HALO_BWD)
        _fill_halo(dbuf, dcur, dprev, dnxt, tm, i, n_i, HALO_BWD)
        xs = [xbuf[6 + k:6 + k + ext, :] for k in range(D_CONV)]
        pre = jnp.zeros((ext, tc), f32) + b_ref[...]
        for k in range(D_CONV):
            pre = pre + xs[k] * w_ref[k:k + 1, :]
        sg = jax.nn.sigmoid(pre)
        pbuf[...] = dbuf[8:8 + ext, :] * (sg * (1.0 + pre * (1.0 - sg)))
        dx = jnp.zeros((tm, tc), f32)
        for k in range(D_CONV):
            dx = dx + pbuf[10 - k:10 - k + tm, :] * w_ref[k:k + 1, :]
        dx_ref[...] = dx

        @pl.when(i == 0)
        def _():
            dw_ref[...] = jnp.zeros_like(dw_ref)
            db_ref[...] = jnp.zeros_like(db_ref)

        dpre = pbuf[8:8 + tm, :]
        db_ref[...] += jnp.sum(dpre, axis=0, keepdims=True)
        for k in range(D_CONV):
            dw_ref[k:k + 1, :] += jnp.sum(dpre * xs[k][8:8 + tm, :], axis=0, keepdims=True)

        @pl.when((j == n_j - 1) & (i == n_i - 1))
        def _():
            finish()

    cur, prev, nxt = _halo_specs(tm, tc, s, HALO_BWD)
    hbm = pl.BlockSpec(memory_space=pl.ANY)
    return pl.pallas_call(
        body, name="conv_silu_bwd", grid=(n_j, n_i),
        in_specs=[cur, prev, nxt, cur, prev, nxt, pl.BlockSpec((D_CONV, tc), lambda j, i: (0, j)),
                  pl.BlockSpec((1, tc), lambda j, i: (0, j)), hbm],
        out_specs=[pl.BlockSpec((tm, tc), lambda j, i: (i, j)), pl.BlockSpec((D_CONV, tc), lambda j, i: (0, j)),
                   pl.BlockSpec((1, tc), lambda j, i: (0, j)), hbm],
        out_shape=[jax.ShapeDtypeStruct((s, c), f32), jax.ShapeDtypeStruct((D_CONV, c), f32),
                   jax.ShapeDtypeStruct((1, c), f32), jax.ShapeDtypeStruct(g.shape, g.dtype)],
        scratch_shapes=[pltpu.VMEM((tm + 2 * HALO_BWD, tc), f32), pltpu.VMEM((tm + 2 * HALO_BWD, tc), f32),
                        pltpu.VMEM((ext, tc), f32)] + COMM_SEMS,
        compiler_params=_cparams(dimension_semantics=("arbitrary", "arbitrary")),
    )(x, x, x, dy, dy, dy, w, b, g)


@jax.custom_vjp
def conv_silu_comm(x, w, b, shard, recv_like):
    act, gathered = _conv_fwd(x, w, b, shard)
    return (act, gathered) + tuple(jnp.zeros(shp, f32) for shp in LATE_SHAPES)


def _conv_silu_comm_fwd(x, w, b, shard, recv_like):
    return conv_silu_comm(x, w, b, shard, recv_like), (x, w, b, shard)


def _conv_silu_comm_bwd(res, cts):
    x, w, b, shard = res
    dx, dw, db, recv = _conv_bwd(x, w, b, cts[0], _pack_late_grads(dict(zip(LATE, cts[2:]))))
    return dx, dw, db, jnp.zeros_like(shard), recv


conv_silu_comm.defvjp(_conv_silu_comm_fwd, _conv_silu_comm_bwd)


ATT_SCALE = HEAD_DIM ** -0.5
Q_SCALE = ATT_SCALE * math.log2(math.e)
LN2 = math.log(2.0)
REP = N_Q_HEADS // N_KV_HEADS


HP = 2
assert REP % HP == 0


def _attn_fwd(q, k, v):
    s, dh = q.shape[0], HEAD_DIM
    hq = q.shape[1] // dh
    tq = _pick(s, (256, 128))

    v1 = jnp.concatenate([v, jnp.ones(v.shape[:2] + (1,), v.dtype), jnp.zeros(v.shape[:2] + (dh - 1,), v.dtype)],
                         axis=-1)

    def body(q_ref, k_ref, v_ref, o_ref, p_ref, linv_ref):
        for j in range(HP):
            sl = slice(j * dh, (j + 1) * dh)
            sc = lax.dot_general(q_ref[:, sl], k_ref[0], _DIMS["nt"], preferred_element_type=f32)
            m = jnp.max(sc, axis=-1, keepdims=True)
            p = jnp.exp2(sc - m).astype(bf16)
            p_ref[j] = p
            o1 = jnp.dot(p, v_ref[0], preferred_element_type=f32)
            linv = 1.0 / o1[:, dh:dh + 1]
            o_ref[:, sl] = (o1[:, :dh] * linv).astype(o_ref.dtype)
            linv_ref[j] = linv

    return pl.pallas_call(
        body, name="attn_fwd", grid=(hq // HP, s // tq),
        in_specs=[pl.BlockSpec((tq, HP * dh), lambda h, i: (i, h)),
                  pl.BlockSpec((1, s, dh), lambda h, i: (h * HP // REP, 0, 0)),
                  pl.BlockSpec((1, s, 2 * dh), lambda h, i: (h * HP // REP, 0, 0))],
        out_specs=[pl.BlockSpec((tq, HP * dh), lambda h, i: (i, h)),
                   pl.BlockSpec((HP, tq, s), lambda h, i: (h, i, 0)),
                   pl.BlockSpec((HP, tq, 1), lambda h, i: (h, i, 0))],
        out_shape=[jax.ShapeDtypeStruct((s, hq * dh), bf16), jax.ShapeDtypeStruct((hq, s, s), bf16),
                   jax.ShapeDtypeStruct((hq, s, 1), f32)],
        compiler_params=_cparams(dimension_semantics=("parallel", "arbitrary")),
    )(q, k, v1)


def _attn_bwd(p, do, o, q, k, v, linv):
    hq, s, _ = p.shape
    dh = HEAD_DIM
    tq = _pick(s, (256, 128))

    def body(p_ref, do_ref, o_ref, q_ref, k_ref, v_ref, linv_ref, dq_ref, dkt_ref, dvt_ref):
        @pl.when(pl.program_id(1) == 0)
        def _():
            dkt_ref[...] = jnp.zeros_like(dkt_ref)
            dvt_ref[...] = jnp.zeros_like(dvt_ref)

        for j in range(HP):
            sl = slice(j * dh, (j + 1) * dh)
            pp, doh, li = p_ref[j], do_ref[:, sl], linv_ref[j]
            do32 = doh.astype(f32)
            d = jnp.sum(do32 * o_ref[:, sl].astype(f32), axis=-1, keepdims=True)
            dp = lax.dot_general(doh, v_ref[0], _DIMS["nt"], preferred_element_type=f32)
            ds = (pp.astype(f32) * ((dp - d) * li)).astype(bf16)
            dq_ref[:, sl] = (jnp.dot(ds, k_ref[0], preferred_element_type=f32) * LN2).astype(dq_ref.dtype)
            dvt_ref[j] += lax.dot_general((do32 * li).astype(bf16), pp, _DIMS["tn"], preferred_element_type=f32)
            dkt_ref[j] += lax.dot_general(q_ref[:, sl], ds, _DIMS["tn"], preferred_element_type=f32)

    def row():
        return pl.BlockSpec((tq, HP * dh), lambda h, i: (i, h))

    return pl.pallas_call(
        body, name="attn_bwd", grid=(hq // HP, s // tq),
        in_specs=[pl.BlockSpec((HP, tq, s), lambda h, i: (h, i, 0)), row(), row(), row(),
                  pl.BlockSpec((1, s, dh), lambda h, i: (h * HP // REP, 0, 0)),
                  pl.BlockSpec((1, s, dh), lambda h, i: (h * HP // REP, 0, 0)),
                  pl.BlockSpec((HP, tq, 1), lambda h, i: (h, i, 0))],
        out_specs=[row(), pl.BlockSpec((HP, dh, s), lambda h, i: (h, 0, 0)),
                   pl.BlockSpec((HP, dh, s), lambda h, i: (h, 0, 0))],
        out_shape=[jax.ShapeDtypeStruct((s, hq * dh), q.dtype), jax.ShapeDtypeStruct((hq, dh, s), f32),
                   jax.ShapeDtypeStruct((hq, dh, s), f32)],
        compiler_params=_cparams(dimension_semantics=("parallel", "arbitrary")),
    )(p, do, o, q, k, v, linv)


@jax.custom_vjp
def attention(q, k, v):
    return _attn_fwd(q, k, v)[0]


def _attention_fwd(q, k, v):
    o, p, linv = _attn_fwd(q, k, v)
    return o, (q, k, v, o, p, linv)


def _attention_bwd(res, do):
    q, k, v, o, p, linv = res
    s = q.shape[0]
    dq, dkt, dvt = _attn_bwd(p, do.astype(bf16), o, q, k, v, linv)

    def per_kv_head(t):
        return jnp.swapaxes(t.reshape(N_KV_HEADS, REP, HEAD_DIM, s).sum(axis=1), 1, 2)

    return dq, (per_kv_head(dkt) * LN2).astype(k.dtype), per_kv_head(dvt).astype(v.dtype)


attention.defvjp(_attention_fwd, _attention_bwd)


HPG = N_SSD_HEADS // N_SSD_GROUPS
GW = HPG * SSD_HEAD_DIM
NEG = -1e30
SPLIT_ROWS = 32


def _ssd_consts():
    k = np.arange(SPLIT_ROWS)[:, None]
    live = k < 3 * HPG
    sel_chunk = ((k % HPG) == (np.arange(HPG * CHUNK)[None, :] // CHUNK)) & live
    sel_head = ((k % HPG) == (np.arange(GW)[None, :] // SSD_HEAD_DIM)) & live
    return jnp.asarray(sel_chunk, bf16), jnp.asarray(sel_head, bf16)


def _split3(x):
    hi = x.astype(bf16).astype(f32)
    r1 = x - hi
    mid = r1.astype(bf16).astype(f32)
    lo = (r1 - mid).astype(bf16).astype(f32)
    return jnp.concatenate([hi, mid, lo, jnp.zeros_like(hi)], axis=0).astype(bf16)


def _tn(a, b):
    return lax.dot_general(a, b, _DIMS["tn"], preferred_element_type=f32)


def _nt(a, b):
    return lax.dot_general(a, b, _DIMS["nt"], preferred_element_type=f32)


def _nn(a, b):
    return jnp.dot(a, b, preferred_element_type=f32)


def _head_sum(sel8, x):
    hi = x.astype(bf16)
    lo = (x - hi.astype(f32)).astype(bf16)
    return _nt(sel8, hi) + _nt(sel8, lo)


def _ssd_masks(reverse):
    r = lax.broadcasted_iota(jnp.int32, (CHUNK, CHUNK), 0)
    c = lax.broadcasted_iota(jnp.int32, (CHUNK, CHUNK), 1)
    lower, upper = r >= c, r <= c
    return (upper, lower) if reverse else (lower, upper)


def _ssd_in_specs(cidx):
    return [pl.BlockSpec((CHUNK, D_INNER), lambda c: (cidx(c), 0)),
            pl.BlockSpec((CHUNK, GN), lambda c: (cidx(c), D_INNER // GN)),
            pl.BlockSpec((CHUNK, GN), lambda c: (cidx(c), D_INNER // GN + 1)),
            pl.BlockSpec((N_SSD_HEADS, CHUNK), lambda c: (0, cidx(c))),
            pl.BlockSpec((N_SSD_HEADS, 1), lambda c: (0, 0)),
            pl.BlockSpec((SPLIT_ROWS, HPG * CHUNK), lambda c: (0, 0)),
            pl.BlockSpec((SPLIT_ROWS, GW), lambda c: (0, 0))]


def _ssd_chunk_common(dtt_ref, a_ref, et_ref, mask_t):
    dtt = dtt_ref[...]
    et = jnp.dot(dtt * a_ref[...], mask_t.astype(f32), precision=HIGHEST, preferred_element_type=f32)
    et_ref[...] = et
    return dtt, et


def _ssd_group_common(g, dtt, et, selc_ref, selh_ref, xs_ref, b_ref, c_ref, last):
    gr = slice(g * HPG, (g + 1) * HPG)
    e3 = _split3(et[gr])
    col = _tn(e3, selc_ref[...])
    eb = _tn(e3, selh_ref[...])
    dtb = _tn(_split3(dtt[gr]), selh_ref[...])
    tbc = eb[last:last + 1, :]
    xs = xs_ref[:, g * GW:(g + 1) * GW]
    bg = b_ref[:, g * D_STATE:(g + 1) * D_STATE].astype(bf16)
    cg = c_ref[:, g * D_STATE:(g + 1) * D_STATE].astype(bf16)
    return col, eb, dtb, tbc, xs, bg, cg


def _ssd_fwd(xbc, dtt, a_col, reverse, y_prev=None, dexp=None):
    s = xbc.shape[0]
    nc = s // CHUNK
    cidx = (lambda c: nc - 1 - c) if reverse else (lambda c: c)
    last = 0 if reverse else CHUNK - 1
    selc, selh = _ssd_consts()
    final = y_prev is not None
    n_in = 9 if final else 7

    def body(*refs):
        xs_ref, b_ref, c_ref, dtt_ref, a_ref, selc_ref, selh_ref = refs[:7]
        y_ref, st_ref, ht_ref, et_ref, yg_ref = refs[n_in:]

        @pl.when(pl.program_id(0) == 0)
        def _():
            ht_ref[...] = jnp.zeros_like(ht_ref)

        mask, mask_t = _ssd_masks(reverse)
        dtt_v, et = _ssd_chunk_common(dtt_ref, a_ref, et_ref, mask_t)
        for g in range(N_SSD_GROUPS):
            col, eb, dtb, tbc, xs, bg, cg = _ssd_group_common(g, dtt_v, et, selc_ref, selh_ref, xs_ref, b_ref, c_ref,
                                                              last)
            xd = xs * dtb
            cb = _nt(cg, bg)
            ht = ht_ref[g]
            st_ref[0, g] = ht
            yoff = _nn(cg, ht.astype(bf16)) * jnp.exp(eb)
            for j in range(HPG):
                h = g * HPG + j
                hs = slice(j * SSD_HEAD_DIM, (j + 1) * SSD_HEAD_DIM)
                lam = jnp.exp(jnp.where(mask, col[:, j * CHUNK:(j + 1) * CHUNK] - et_ref[h:h + 1, :], NEG))
                yg_ref[:, hs] = _nn((cb * lam).astype(bf16), xd[:, hs].astype(bf16))
            cols = slice(g * GW, (g + 1) * GW)
            yg = yg_ref[...] + yoff
            if final:
                yg = yg + refs[7][:, cols] + xs * refs[8][:, cols]
            y_ref[:, cols] = yg.astype(y_ref.dtype)
            ht_ref[g] = jnp.exp(tbc) * ht + _tn(bg, (xd * jnp.exp(tbc - eb)).astype(bf16))

    y_spec = pl.BlockSpec((CHUNK, D_INNER), lambda c: (cidx(c), 0))
    extra_specs = [y_spec, pl.BlockSpec((1, D_INNER), lambda c: (0, 0))] if final else []
    return pl.pallas_call(
        body, name="ssd_fwd_rev" if reverse else "ssd_fwd", grid=(nc,),
        in_specs=_ssd_in_specs(cidx) + extra_specs,
        out_specs=[y_spec, pl.BlockSpec((1, N_SSD_GROUPS, D_STATE, GW), lambda c: (cidx(c), 0, 0, 0))],
        out_shape=[jax.ShapeDtypeStruct((s, D_INNER), bf16 if final else f32),
                   jax.ShapeDtypeStruct((nc, N_SSD_GROUPS, D_STATE, GW), f32)],
        scratch_shapes=[pltpu.VMEM((N_SSD_GROUPS, D_STATE, GW), f32), pltpu.VMEM((N_SSD_HEADS, CHUNK), f32),
                        pltpu.VMEM((CHUNK, GW), f32)],
        compiler_params=_cparams(dimension_semantics=("arbitrary",)),
    )(xbc, xbc, xbc, dtt, a_col, selc, selh, *((y_prev, dexp) if final else ()))


def _ssd_bwd(xbc, dtt, a_col, states, dy, reverse, dxbc_prev=None, dexp=None):
    s = xbc.shape[0]
    nc = s // CHUNK
    cidx = (lambda c: c) if reverse else (lambda c: nc - 1 - c)
    last = 0 if reverse else CHUNK - 1
    selc, selh = _ssd_consts()
    final = dxbc_prev is not None
    n_in = 11 if final else 9
    n_out = 4 if final else 3

    def body(*refs):
        xs_ref, b_ref, c_ref, dtt_ref, a_ref, selc_ref, selh_ref, st_ref, dy_ref = refs[:9]
        dxbc_ref, ddtt_ref, da_ref = refs[n_in:n_in + 3]
        dh_ref, et_ref, det_ref, det2_ref, ddt_ref, q_ref = refs[n_in + n_out:]
        if final:
            prev_ref, dexp_ref, ddexp_ref = refs[9], refs[10], refs[n_in + 3]

        @pl.when(pl.program_id(0) == 0)
        def _():
            dh_ref[...] = jnp.zeros_like(dh_ref)
            da_ref[...] = jnp.zeros_like(da_ref)
            if final:
                ddexp_ref[...] = jnp.zeros_like(ddexp_ref)

        mask, mask_t = _ssd_masks(reverse)
        dtt_v, et = _ssd_chunk_common(dtt_ref, a_ref, et_ref, mask_t)
        sel8 = selh_ref[0:HPG, :]
        is_last = lax.broadcasted_iota(jnp.int32, (CHUNK, GW), 0) == last
        for g in range(N_SSD_GROUPS):
            col, eb, dtb, tbc, xs, bg, cg = _ssd_group_common(g, dtt_v, et, selc_ref, selh_ref, xs_ref, b_ref, c_ref,
                                                              last)
            xd = xs * dtb
            cb = _nt(cg, bg)
            cbt = _nt(bg, cg)
            exp_t = jnp.exp(tbc)
            dfac = jnp.exp(tbc - eb)
            ht = st_ref[0, g]
            dhn = dh_ref[g]
            ht16, dhn16 = ht.astype(bf16), dhn.astype(bf16)
            dy = dy_ref[:, g * GW:(g + 1) * GW].astype(f32)
            dye = dy * jnp.exp(eb)
            dye16 = dye.astype(bf16)
            dc = _nt(dye16, ht16)
            dh_ref[g] = exp_t * dhn + _tn(cg, dye16)
            deb = dye * _nn(cg, ht16)
            xdd = xd * dfac
            dxdd = _nn(bg, dhn16)
            db = _nt(xdd.astype(bf16), dhn16)
            dxd_state = dxdd * dfac
            ddf = dxdd * xdd
            dtbc = jnp.sum(ddf, axis=0, keepdims=True) + exp_t * jnp.sum(dhn * ht, axis=0, keepdims=True)
            deb = deb - ddf + jnp.where(is_last, dtbc, 0.0)
            dcb = jnp.zeros((CHUNK, CHUNK), f32)
            dcbt = jnp.zeros((CHUNK, CHUNK), f32)
            for j in range(HPG):
                h = g * HPG + j
                hs = slice(j * SSD_HEAD_DIM, (j + 1) * SSD_HEAD_DIM)
                colj = col[:, j * CHUNK:(j + 1) * CHUNK]
                row = et_ref[h:h + 1, :]
                lam = jnp.exp(jnp.where(mask, colj - row, NEG))
                lam_t = lam.T
                xdj, dyj = xd[:, hs].astype(bf16), dy[:, hs].astype(bf16)
                t1 = _nt(dyj, xdj) * lam
                t2 = _nt(xdj, dyj) * lam_t
                dcb, dcbt = dcb + t1, dcbt + t2
                det_ref[h:h + 1, :] = -jnp.sum(t1 * cb - t2 * cbt, axis=0, keepdims=True)
                q_ref[:, hs] = _nn((cbt * lam_t).astype(bf16), dyj)
            x_cols = slice(g * GW, (g + 1) * GW)
            dxd = q_ref[...] + dxd_state
            dxs = dxd * dtb
            if final:
                dxs = dxs + prev_ref[:, x_cols] + dy * dexp_ref[:, x_cols]
            dxbc_ref[:, x_cols] = dxs
            b_cols = slice(D_INNER + g * D_STATE, D_INNER + (g + 1) * D_STATE)
            c_cols = slice(D_INNER + GN + g * D_STATE, D_INNER + GN + (g + 1) * D_STATE)
            db = db + _nn(dcbt.astype(bf16), cg)
            dc = dc + _nn(dcb.astype(bf16), bg)
            if final:
                db, dc = db + prev_ref[:, b_cols], dc + prev_ref[:, c_cols]
                ddexp_ref[:, g * GW:(g + 1) * GW] += jnp.sum(dy * xs, axis=0, keepdims=True)
            dxbc_ref[:, b_cols] = db
            dxbc_ref[:, c_cols] = dc
            det2_ref[g * HPG:(g + 1) * HPG, :] = _head_sum(sel8, deb)
            ddt_ref[g * HPG:(g + 1) * HPG, :] = _head_sum(sel8, dxd * xs)
        dat = jnp.dot(det_ref[...] + det2_ref[...], mask.astype(f32), precision=HIGHEST, preferred_element_type=f32)
        ddtt_ref[...] = ddt_ref[...] + dat * a_ref[...]
        da_ref[...] += jnp.sum(dat * dtt_v, axis=1, keepdims=True)

    in_specs = _ssd_in_specs(cidx) + [
        pl.BlockSpec((1, N_SSD_GROUPS, D_STATE, GW), lambda c: (cidx(c), 0, 0, 0)),
        pl.BlockSpec((CHUNK, D_INNER), lambda c: (cidx(c), 0))]
    hl = pltpu.VMEM((N_SSD_HEADS, CHUNK), f32)
    dxbc_spec = pl.BlockSpec((CHUNK, CONV_DIM), lambda c: (cidx(c), 0))
    dexp_spec = pl.BlockSpec((1, D_INNER), lambda c: (0, 0))
    return pl.pallas_call(
        body, name="ssd_bwd_rev" if reverse else "ssd_bwd", grid=(nc,),
        in_specs=in_specs + ([dxbc_spec, dexp_spec] if final else []),
        out_specs=[dxbc_spec, pl.BlockSpec((N_SSD_HEADS, CHUNK), lambda c: (0, cidx(c))),
                   pl.BlockSpec((N_SSD_HEADS, 1), lambda c: (0, 0))] + ([dexp_spec] if final else []),
        out_shape=[jax.ShapeDtypeStruct((s, CONV_DIM), f32), jax.ShapeDtypeStruct((N_SSD_HEADS, s), f32),
                   jax.ShapeDtypeStruct((N_SSD_HEADS, 1), f32)]
        + ([jax.ShapeDtypeStruct((1, D_INNER), f32)] if final else []),
        scratch_shapes=[pltpu.VMEM((N_SSD_GROUPS, D_STATE, GW), f32), hl, hl, hl, hl, pltpu.VMEM((CHUNK, GW), f32)],
        compiler_params=_cparams(dimension_semantics=("arbitrary",)),
    )(xbc, xbc, xbc, dtt, a_col, selc, selh, states, dy, *((dxbc_prev, dexp) if final else ()))


@jax.custom_vjp
def ssd_bidir(xbc, dtt, a_col, dexp):
    y_f, _ = _ssd_fwd(xbc, dtt[:N_SSD_HEADS], a_col[:N_SSD_HEADS], False)
    return _ssd_fwd(xbc, dtt[N_SSD_HEADS:], a_col[N_SSD_HEADS:], True, y_prev=y_f, dexp=dexp)[0]


def _ssd_bidir_fwd(xbc, dtt, a_col, dexp):
    y_f, st_f = _ssd_fwd(xbc, dtt[:N_SSD_HEADS], a_col[:N_SSD_HEADS], False)
    y, st_b = _ssd_fwd(xbc, dtt[N_SSD_HEADS:], a_col[N_SSD_HEADS:], True, y_prev=y_f, dexp=dexp)
    return y, (xbc, dtt, a_col, dexp, st_f, st_b)


def _ssd_bidir_bwd(res, dy):
    xbc, dtt, a_col, dexp, st_f, st_b = res
    dxbc_f, ddtt_f, da_f = _ssd_bwd(xbc, dtt[:N_SSD_HEADS], a_col[:N_SSD_HEADS], st_f, dy, False)
    dxbc, ddtt_b, da_b, ddexp = _ssd_bwd(xbc, dtt[N_SSD_HEADS:], a_col[N_SSD_HEADS:], st_b, dy, True,
                                         dxbc_prev=dxbc_f, dexp=dexp)
    return dxbc, jnp.concatenate([ddtt_f, ddtt_b], axis=0), jnp.concatenate([da_f, da_b], axis=0), ddexp


ssd_bidir.defvjp(_ssd_bidir_fwd, _ssd_bidir_bwd)


def _rope_tables(s):
    rows = s // GRID_W
    pos_row = np.repeat(np.arange(rows), GRID_W).astype(np.float32)
    pos_col = np.tile(np.arange(GRID_W), rows).astype(np.float32)
    axis_dim = HEAD_DIM // 2
    inv_freq = np.float32(ROPE_THETA) ** (-np.arange(0, axis_dim, 2, dtype=np.float32) / np.float32(axis_dim))
    ang_r = pos_row[:, None] * inv_freq[None, :].astype(np.float32)
    ang_c = pos_col[:, None] * inv_freq[None, :].astype(np.float32)
    cos = np.concatenate([np.cos(ang_r), np.cos(ang_r), np.cos(ang_c), np.cos(ang_c)] * 2, axis=-1)
    sin = np.concatenate([np.sin(ang_r), np.sin(ang_r), np.sin(ang_c), np.sin(ang_c)] * 2, axis=-1)
    return jnp.asarray(cos, f32), jnp.asarray(sin, f32)


def _rope_perm():
    p = np.zeros((HEAD_DIM, HEAD_DIM), np.float32)
    for j in range(HEAD_DIM):
        if (j % 32) < 16:
            p[j + 16, j] = -1.0
        else:
            p[j - 16, j] = 1.0
    return p


def local_loss(x, mod, small, recv_in_like, recv_late_like, wfull, late_shard, target):
    s = x.shape[0]
    lin = {n: make_linear("lin_" + n) for n in LATE if not n.startswith("mlp")}
    wfull, wgrads = dict(wfull), {}
    shift1, scale1, gate1, shift2, scale2, gate2 = [mod[i] for i in range(6)]

    norm_mod = make_rowwise("norm_mod", _fn_norm_mod, [(D_MODEL, bf16), (D_MODEL, f32)], tm_pref=1024)
    (h, x_res), _ = norm_mod((x,), (small["norm1_w"], scale1, shift1), (), ())

    proj = dict(zip(PROJ_NAMES, in_proj(h, tuple(wfull[n] for n in PROJ_NAMES), recv_in_like)))

    cos, sin = _rope_tables(s)

    def heads(t, nh):
        return t.reshape(s, nh, HEAD_DIM).transpose(1, 0, 2)

    qr = make_head_rope("q_norm_rope", N_Q_HEADS, Q_SCALE, False)(proj["q"], small["q_norm_w"], cos, sin)
    kr = make_head_rope("k_norm_rope", N_KV_HEADS, 1.0, True)(proj["k"], small["k_norm_w"], cos, sin)
    vh = heads(proj["v"], N_KV_HEADS).astype(bf16)
    att = attention(qr, kr, vh)

    xbc, gathered, *carriers = conv_silu_comm(proj["xbc"], small["conv_w"], small["conv_b"], late_shard,
                                              recv_late_like)
    wfull.update(_split_late(gathered))
    wgrads.update(zip(LATE, carriers))
    ao = lin["attn_out"](att, wfull["attn_out"], wgrads["attn_out"])
    softplus = make_rowwise("dt_softplus", _fn_softplus, [(2 * N_SSD_HEADS, f32)])
    (dt,), _ = softplus((proj["dt"][:, :2 * N_SSD_HEADS],), (small["dt_bias"].reshape(1, 2 * N_SSD_HEADS),), (), ())
    a_neg = -jnp.exp(small["A_log"])
    dexp = jnp.repeat(small["ssd_D"].reshape(N_SSD_HEADS), SSD_HEAD_DIM).reshape(1, D_INNER)
    y = ssd_bidir(xbc, dt.T, a_neg.reshape(2 * N_SSD_HEADS, 1), dexp)
    ssd_gate = make_rowwise("ssd_gate", _fn_ssd_gate, [(D_INNER, bf16)], tm_pref=256)
    (ssd_out,), _ = ssd_gate((y, proj["z"]), (small["ssd_norm_w"],), (), ())
    so = lin["ssd_out"](ssd_out, wfull["ssd_out"], wgrads["ssd_out"])

    merge = make_rowwise("merge", _fn_merge, [(D_MODEL, bf16)])
    (merged,), _ = merge((ao, so, proj["ga"], proj["gs"]), (), (), ())
    mo = lin["o"](merged, wfull["o"], wgrads["o"])

    res_norm = make_rowwise("res_norm", _fn_res_norm, [(D_MODEL, f32), (D_MODEL, bf16)])
    (x1, h2), _ = res_norm((x_res, mo), (gate1, small["norm2_w"], scale2, shift2), (), ())
    ff = mlp(h2, wfull["mlp1"], wgrads["mlp1"], wfull["mlp2"], wgrads["mlp2"])
    loss_op = make_rowwise("loss", _fn_loss, [], [(1, 1)])
    _, (loss,) = loss_op((x1, ff), (gate2,), (), (target,))
    return loss[0, 0]


_BC1 = 1.0 - ADAM_B1 ** ADAM_STEP
_BC2 = 1.0 - ADAM_B2 ** ADAM_STEP


def _adamw(w, g, m, v):
    m = ADAM_B1 * m + (1.0 - ADAM_B1) * g
    v = ADAM_B2 * v + (1.0 - ADAM_B2) * (g * g)
    delta = -ADAM_LR * ((m / _BC1) / (jnp.sqrt(v / _BC2) + ADAM_EPS) + ADAM_WD * w)
    return delta, m, v


def _ada_fwd(c_all, w, b):
    n = w.shape[1]

    def body(c_ref, w_ref, b_ref, o_ref):
        o_ref[...] = jnp.dot(_silu(c_ref[...]), w_ref[...], precision=HIGHEST, preferred_element_type=f32) + b_ref[...]

    return pl.pallas_call(body, name="ada_fwd", out_shape=jax.ShapeDtypeStruct((N_DEV, n), f32),
                          compiler_params=_cparams())(c_all, w, b)


def _ada_bwd_adamw(c_all, dmod, w, m, v):
    d, n = w.shape
    tr = _pick(d, (256, 128))

    def body(c_ref, dm_ref, w_ref, m_ref, v_ref, g_ref, dl_ref, mo_ref, vo_ref):
        g = lax.dot_general(_silu(c_ref[...]), dm_ref[...], _DIMS["tn"], precision=HIGHEST,
                            preferred_element_type=f32)
        g_ref[...] = g
        dl_ref[...], mo_ref[...], vo_ref[...] = _adamw(w_ref[...], g, m_ref[...], v_ref[...])

    blk = pl.BlockSpec((tr, n), lambda i: (i, 0))
    return pl.pallas_call(
        body, name="ada_bwd_adamw", grid=(d // tr,),
        in_specs=[pl.BlockSpec((N_DEV, tr), lambda i: (0, i)), pl.BlockSpec((N_DEV, n), lambda i: (0, 0)), blk, blk, blk],
        out_specs=[blk] * 4, out_shape=[jax.ShapeDtypeStruct((d, n), f32)] * 4,
        compiler_params=_cparams(dimension_semantics=("parallel",)),
    )(c_all, dmod, w, m, v)


def _sum_over_mesh(g):
    def body(g_ref, o_ref):
        acc = g_ref[0]
        for d in range(1, N_DEV):
            acc = acc + g_ref[d]
        o_ref[...] = acc

    return pl.pallas_call(body, name="sum_small", out_shape=jax.ShapeDtypeStruct(g.shape[1:], f32),
                          compiler_params=_cparams())(g)


def _adamw_small(w, g, m, v):
    def body(w_ref, g_ref, m_ref, v_ref, dl_ref, mo_ref, vo_ref):
        dl_ref[...], mo_ref[...], vo_ref[...] = _adamw(w_ref[...], g_ref[...], m_ref[...], v_ref[...])

    return pl.pallas_call(body, name="adamw_small", out_shape=[jax.ShapeDtypeStruct(w.shape, f32)] * 3,
                          compiler_params=_cparams())(w, g, m, v)


def _sum_adamw(recv, w, m, v, name):
    _, r, c = recv.shape
    tr = _pick(r, (256, 128, 64, 16))

    def body(g_ref, w_ref, m_ref, v_ref, go_ref, dl_ref, mo_ref, vo_ref):
        g = g_ref[0].astype(f32)
        for d in range(1, N_DEV):
            g = g + g_ref[d].astype(f32)
        go_ref[...] = g
        dl_ref[...], mo_ref[...], vo_ref[...] = _adamw(w_ref[...], g, m_ref[...], v_ref[...])

    blk = pl.BlockSpec((tr, c), lambda i: (i, 0))
    return pl.pallas_call(
        body, name=name, grid=(r // tr,),
        in_specs=[pl.BlockSpec((N_DEV, tr, c), lambda i: (0, i, 0)), blk, blk, blk],
        out_specs=[blk] * 4, out_shape=[jax.ShapeDtypeStruct((r, c), f32)] * 4,
        compiler_params=_cparams(dimension_semantics=("parallel",)),
    )(recv, w, m, v)


def _pack_small(arrs):
    parts = []
    for a in arrs:
        flat = a.reshape(-1).astype(f32)
        parts.append(jnp.pad(flat, (0, (-flat.shape[0]) % LANE)))
    flat = jnp.concatenate(parts)
    flat = jnp.pad(flat, (0, (-flat.shape[0]) % (8 * LANE)))
    return flat.reshape(-1, LANE)


def _unpack_small(packed, shapes):
    flat = packed.reshape(-1)
    out, off = [], 0
    for shp in shapes:
        n = int(np.prod(shp))
        out.append(flat[off:off + n].reshape(shp))
        off += n + (-n) % LANE
    return out


BIG = ("w_attn_out", "w_ssd_out", "w_o", "w_mlp1", "w_mlp2")
BIG_ROWS = (N_Q_HEADS * HEAD_DIM // N_DEV, D_INNER // N_DEV, D_MODEL // N_DEV,
            D_MODEL * (D_FF // N_DEV) // PACK_COLS, D_FF // N_DEV)
N_IN_SHARD = D_IN_PROJ // N_DEV
assert sum(BIG_ROWS) % 16 == 0


def _pack_big(shards, dtype):
    return jnp.concatenate([s.astype(dtype).reshape(-1, PACK_COLS) for s in shards], axis=0)


def _unpack_big(packed, shapes):
    out, off = [], 0
    for rows, shp in zip(BIG_ROWS, shapes):
        out.append(packed[off:off + rows].reshape(shp))
        off += rows
    return out


LATE = ("attn_out", "ssd_out", "o", "mlp1", "mlp2")
LATE_SHAPES = ((N_Q_HEADS * HEAD_DIM, D_MODEL), (D_INNER, D_MODEL), (D_MODEL, D_MODEL), (D_MODEL, D_FF),
               (D_FF, D_MODEL))


def _split_w_in(g_in):
    w_in = g_in.transpose(1, 0, 2).reshape(D_MODEL, D_IN_PROJ)
    w = {}
    off = 0
    for name, size in zip(PROJ_NAMES, PROJ_SIZES):
        w[name] = w_in[:, off:off + size]
        off += size
    w["dt"] = jnp.pad(w["dt"], ((0, 0), (0, DT_PAD - 2 * N_SSD_HEADS)))
    return w


def _split_late(g):
    offs = np.cumsum((0,) + BIG_ROWS)
    sl = [g[:, offs[i]:offs[i + 1]] for i in range(len(BIG))]
    return {"attn_out": sl[0].reshape(LATE_SHAPES[0]), "ssd_out": sl[1].reshape(LATE_SHAPES[1]),
            "o": sl[2].reshape(LATE_SHAPES[2]),
            "mlp1": sl[3].reshape(N_DEV, D_MODEL, D_FF // N_DEV).transpose(1, 0, 2).reshape(LATE_SHAPES[3]),
            "mlp2": sl[4].reshape(LATE_SHAPES[4])}


def _pack_in_grads(gw):
    gw = {n: g.astype(bf16) for n, g in gw.items()}
    gw["dt"] = gw["dt"][:, :2 * N_SSD_HEADS]
    g_in = jnp.concatenate([gw[n] for n in PROJ_NAMES], axis=1)
    return g_in.reshape(D_MODEL, N_DEV, N_IN_SHARD).transpose(1, 0, 2)


def _pack_late_grads(gw):
    gw = {n: g.astype(bf16) for n, g in gw.items()}
    parts = [
        gw["attn_out"].reshape(N_DEV, -1, PACK_COLS),
        gw["ssd_out"].reshape(N_DEV, -1, PACK_COLS),
        gw["o"].reshape(N_DEV, -1, PACK_COLS),
        gw["mlp1"].reshape(D_MODEL, N_DEV, D_FF // N_DEV).transpose(1, 0, 2).reshape(N_DEV, -1, PACK_COLS),
        gw["mlp2"].reshape(N_DEV, -1, PACK_COLS),
    ]
    return jnp.concatenate(parts, axis=1)


SMALL = ("norm1_w", "norm2_w", "q_norm_w", "k_norm_w", "conv_w", "conv_b", "A_log", "dt_bias", "ssd_D", "ssd_norm_w")


def kernel(x, c, w_ada, b_ada, norm1_w, norm2_w, w_in, q_norm_w, k_norm_w, conv_w, conv_b, A_log, dt_bias, ssd_D, ssd_norm_w, w_attn_out, w_ssd_out, w_o, w_mlp1, w_mlp2, loss_target, m_w_ada, m_b_ada, m_norm1_w, m_norm2_w, m_w_in, m_q_norm_w, m_k_norm_w, m_conv_w, m_conv_b, m_A_log, m_dt_bias, m_ssd_D, m_ssd_norm_w, m_w_attn_out, m_w_ssd_out, m_w_o, m_w_mlp1, m_w_mlp2, v_w_ada, v_b_ada, v_norm1_w, v_norm2_w, v_w_in, v_q_norm_w, v_k_norm_w, v_conv_w, v_conv_b, v_A_log, v_dt_bias, v_ssd_D, v_ssd_norm_w, v_w_attn_out, v_w_ssd_out, v_w_o, v_w_mlp1, v_w_mlp2):
    args = dict(locals())
    me = _my_index()
    n_ada = 6 * D_MODEL // N_DEV
    n_cw = CONV_DIM // N_DEV

    blk = jnp.zeros((8, D_MODEL), f32)
    blk = blk.at[0:1, :].set(c)
    blk = blk.at[1:1 + D_CONV, :n_cw].set(conv_w[0])
    g0 = _all_gather(blk, "gather_c_convw", in_vmem=True)
    c_all = g0[:, 0, :]
    conv_w_full = g0[:, 1:1 + D_CONV, :n_cw].transpose(1, 0, 2).reshape(D_CONV, CONV_DIM)

    b_shard = lax.dynamic_slice(b_ada, (0, me * n_ada), (1, n_ada))
    mod_cols = _ada_fwd(c_all, w_ada[0], b_shard)
    g1 = _all_gather(mod_cols, "gather_mod", in_vmem=True)
    mod_mine = lax.dynamic_index_in_dim(g1, me, axis=1, keepdims=False)
    mod = mod_mine.reshape(6, 1, D_MODEL)

    big_shapes = [args[n].shape[1:] for n in BIG]
    late_shard = _pack_big([args[n][0] for n in BIG], bf16)
    wfull = _split_w_in(_all_gather(w_in[0].astype(bf16), "gather_w_in", in_vmem=False))
    recv_in_like = jnp.zeros((N_DEV,) + w_in.shape[1:], bf16)
    recv_late_like = jnp.zeros((N_DEV,) + late_shard.shape, bf16)

    small = {"norm1_w": norm1_w, "norm2_w": norm2_w, "q_norm_w": q_norm_w, "k_norm_w": k_norm_w,
             "conv_w": conv_w_full, "conv_b": conv_b, "A_log": A_log[0], "dt_bias": dt_bias[0], "ssd_D": ssd_D,
             "ssd_norm_w": ssd_norm_w}

    loss, (gx, gmod, gsmall, recv_in, recv_late) = jax.value_and_grad(local_loss, argnums=(0, 1, 2, 3, 4))(
        x[0], mod, small, recv_in_like, recv_late_like, wfull, late_shard, loss_target[0])

    small_list = [gmod, gsmall["norm1_w"], gsmall["norm2_w"], gsmall["q_norm_w"], gsmall["k_norm_w"], gsmall["conv_w"],
                  gsmall["conv_b"], gsmall["A_log"], gsmall["dt_bias"], gsmall["ssd_D"], gsmall["ssd_norm_w"],
                  loss.reshape(1)]
    small_shapes = [a.shape for a in small_list]
    g2 = _all_gather(_pack_small(small_list), "gather_small_grads", in_vmem=True)
    summed = _unpack_small(_sum_over_mesh(g2), small_shapes)
    loss_total = summed[-1][0]
    g_b_ada = summed[0].reshape(1, 6 * D_MODEL)
    g_small = dict(zip(SMALL, summed[1:-1]))
    g_conv_w = lax.dynamic_slice(g_small["conv_w"], (0, me * n_cw), (D_CONV, n_cw))

    dmod_all = g2[:, :6 * D_MODEL // LANE, :].reshape(N_DEV, 6 * D_MODEL)
    dmod_shard = lax.dynamic_slice(dmod_all, (0, me * n_ada), (N_DEV, n_ada))
    ada = _ada_bwd_adamw(c_all, dmod_shard, w_ada[0], m_w_ada[0], v_w_ada[0])

    small_grads = {"b_ada": g_b_ada, "norm1_w": g_small["norm1_w"], "norm2_w": g_small["norm2_w"],
                   "q_norm_w": g_small["q_norm_w"], "k_norm_w": g_small["k_norm_w"], "conv_w": g_conv_w[None],
                   "conv_b": g_small["conv_b"], "A_log": g_small["A_log"][None], "dt_bias": g_small["dt_bias"][None],
                   "ssd_D": g_small["ssd_D"], "ssd_norm_w": g_small["ssd_norm_w"]}
    sm_names = list(small_grads)
    sm_shapes = [args[n].shape for n in sm_names]
    sm = _adamw_small(_pack_small([args[n] for n in sm_names]), _pack_small([small_grads[n] for n in sm_names]),
                      _pack_small([args["m_" + n] for n in sm_names]), _pack_small([args["v_" + n] for n in sm_names]))
    sm_delta, sm_m, sm_v = [dict(zip(sm_names, _unpack_small(t, sm_shapes))) for t in sm]
    small_grads = {n: small_grads[n].reshape(args[n].shape) for n in sm_names}

    w_in_out = _sum_adamw(recv_in, w_in[0], m_w_in[0], v_w_in[0], "sum_adamw_w_in")
    big = _sum_adamw(recv_late, _pack_big([args[n][0] for n in BIG], f32),
                     _pack_big([args["m_" + n][0] for n in BIG], f32),
                     _pack_big([args["v_" + n][0] for n in BIG], f32), "sum_adamw")
    big_g, big_delta, big_m, big_v = [dict(zip(BIG, [t[None] for t in _unpack_big(p, big_shapes)])) for p in big]
    big_g["w_in"], big_delta["w_in"], big_m["w_in"], big_v["w_in"] = [t[None] for t in w_in_out]

    names = ("w_ada", "b_ada", "norm1_w", "norm2_w", "w_in", "q_norm_w", "k_norm_w", "conv_w", "conv_b", "A_log",
             "dt_bias", "ssd_D", "ssd_norm_w", "w_attn_out", "w_ssd_out", "w_o", "w_mlp1", "w_mlp2")
    grads, deltas, new_m, new_v = {}, {}, {}, {}
    for n in names:
        if n == "w_ada":
            grads[n], deltas[n], new_m[n], new_v[n] = [t[None] for t in ada]
        elif n in big_g:
            grads[n], deltas[n], new_m[n], new_v[n] = big_g[n], big_delta[n], big_m[n], big_v[n]
        else:
            grads[n], deltas[n], new_m[n], new_v[n] = small_grads[n], sm_delta[n], sm_m[n], sm_v[n]
    return (loss_total, gx[None], *[grads[n] for n in names], *[deltas[n] for n in names],
            *[new_m[n] for n in names], *[new_v[n] for n in names])
```

```python
import functools
import math

import jax
import jax.numpy as jnp
import numpy as np
from jax import lax
from jax.experimental import pallas as pl
from jax.experimental.pallas import tpu as pltpu

f32 = jnp.float32
bf16 = jnp.bfloat16
HIGHEST = lax.Precision.HIGHEST
MESH = pl.DeviceIdType.MESH

N_DEV = 8
D_MODEL = 1024
GRID_W = 64
N_Q_HEADS = 16
N_KV_HEADS = 4
HEAD_DIM = 64
ROPE_THETA = 10000.0
D_INNER = 2048
SSD_HEAD_DIM = 64
N_SSD_HEADS = 32
N_SSD_GROUPS = 4
D_STATE = 128
D_CONV = 5
CHUNK = 128
D_FF = 4096
EPS = 1e-6
CONV_DIM = D_INNER + 2 * N_SSD_GROUPS * D_STATE
GN = N_SSD_GROUPS * D_STATE
PROJ_NAMES = ("q", "k", "v", "xbc", "z", "dt", "ga", "gs")
PROJ_SIZES = (N_Q_HEADS * HEAD_DIM, N_KV_HEADS * HEAD_DIM, N_KV_HEADS * HEAD_DIM, CONV_DIM, D_INNER,
              2 * N_SSD_HEADS, D_MODEL, D_MODEL)
D_IN_PROJ = sum(PROJ_SIZES)
PROJ_DTYPES = (jnp.bfloat16, jnp.bfloat16, jnp.bfloat16, jnp.float32, jnp.bfloat16, jnp.float32, jnp.bfloat16,
               jnp.bfloat16)
DT_PAD = 128

ADAM_LR, ADAM_B1, ADAM_B2, ADAM_EPS, ADAM_WD, ADAM_STEP = 0.001, 0.9, 0.999, 1e-08, 0.01, 10

V7X_VMEM_LIMIT = 56 * 1024 * 1024
LANE = 128
PACK_COLS = 1024


def _cparams(**kw):
    return pltpu.CompilerParams(vmem_limit_bytes=V7X_VMEM_LIMIT, **kw)


def _pick(dim, prefs):
    for p in prefs:
        if dim % p == 0:
            return p
    return dim


def _my_index():
    return 4 * lax.axis_index("x") + 2 * lax.axis_index("y") + lax.axis_index("c")


COMM_SEMS = [pltpu.SemaphoreType.DMA((7,)), pltpu.SemaphoreType.DMA((7,)), pltpu.SemaphoreType.DMA]


def _gather_phases(x_ref, out_ref, send_sems, recv_sems, local_sem):
    x, y, cc = lax.axis_index("x"), lax.axis_index("y"), lax.axis_index("c")
    me, sibling = (x, y, cc), (x, y, 1 - cc)
    chips = [(1 - x, y), (x, 1 - y), (1 - x, 1 - y)]

    def slot(px, py, pc):
        return out_ref.at[4 * px + 2 * py + pc]

    def copy(k, blk, to, src=None):
        return pltpu.make_async_remote_copy(
            src_ref=slot(*blk) if src is None else src, dst_ref=slot(*blk),
            send_sem=send_sems.at[k], recv_sem=recv_sems.at[k], device_id=to, device_id_type=MESH)

    mine = pltpu.make_async_copy(x_ref, slot(*me), local_sem)
    first = [copy(0, me, sibling, src=x_ref)]
    first += [copy(1 + j, me, (*chip, cc), src=x_ref) for j, chip in enumerate(chips)]
    passed = [copy(4 + j, (*chip, cc), sibling) for j, chip in enumerate(chips)]

    def start():
        mine.start()
        for cp in first:
            cp.start()

    def finish():
        for j, chip in enumerate(chips):
            copy(1 + j, (*chip, cc), me).wait_recv()
            passed[j].start()
        copy(0, sibling, me).wait_recv()
        for j, chip in enumerate(chips):
            copy(4 + j, (*chip, 1 - cc), me).wait_recv()
        for cp in first + passed:
            cp.wait_send()
        mine.wait()

    return start, finish


def _scatter_phases(g_ref, out_ref, send_sems, recv_sems, local_sem):
    x, y, cc = lax.axis_index("x"), lax.axis_index("y"), lax.axis_index("c")
    me = 4 * x + 2 * y + cc
    mine = pltpu.make_async_copy(g_ref.at[me], out_ref.at[me], local_sem)

    def copy(k):
        fx, fy, fc = (k >> 2) & 1, (k >> 1) & 1, k & 1
        px = x + fx - 2 * x * fx
        py = y + fy - 2 * y * fy
        pc = cc + fc - 2 * cc * fc
        peer = 4 * px + 2 * py + pc
        send = pltpu.make_async_remote_copy(
            src_ref=g_ref.at[peer], dst_ref=out_ref.at[me],
            send_sem=send_sems.at[k - 1], recv_sem=recv_sems.at[k - 1],
            device_id=(px, py, pc), device_id_type=MESH)
        recv = pltpu.make_async_remote_copy(
            src_ref=g_ref.at[peer], dst_ref=out_ref.at[peer],
            send_sem=send_sems.at[k - 1], recv_sem=recv_sems.at[k - 1],
            device_id=(px, py, pc), device_id_type=MESH)
        return send, recv

    pairs = [copy(k) for k in range(1, N_DEV)]

    def start():
        mine.start()
        for send, _ in pairs:
            send.start()

    def finish():
        for _, recv in pairs:
            recv.wait_recv()
        for send, _ in pairs:
            send.wait_send()
        mine.wait()

    return start, finish


def _all_gather(block, name, in_vmem):
    r, c = block.shape

    def body(x_ref, out_ref, send_sems, recv_sems, local_sem):
        start, finish = _gather_phases(x_ref, out_ref, send_sems, recv_sems, local_sem)
        start()
        finish()

    space = pltpu.VMEM if in_vmem else pl.ANY
    return pl.pallas_call(
        body, name=name,
        out_shape=jax.ShapeDtypeStruct((N_DEV, r, c), block.dtype),
        in_specs=[pl.BlockSpec(memory_space=space)],
        out_specs=pl.BlockSpec(memory_space=space),
        scratch_shapes=[pltpu.SemaphoreType.DMA((7,)), pltpu.SemaphoreType.DMA((7,)), pltpu.SemaphoreType.DMA],
    )(block)


def _all_gather_pair(small, big, name):
    def body(s_ref, b_ref, so_ref, bo_ref, ss1, rs1, ls1, ss2, rs2, ls2):
        start_b, finish_b = _gather_phases(b_ref, bo_ref, ss2, rs2, ls2)
        start_s, finish_s = _gather_phases(s_ref, so_ref, ss1, rs1, ls1)
        start_b()
        start_s()
        finish_s()
        finish_b()

    return pl.pallas_call(
        body, name=name,
        out_shape=[jax.ShapeDtypeStruct((N_DEV,) + small.shape, small.dtype),
                   jax.ShapeDtypeStruct((N_DEV,) + big.shape, big.dtype)],
        in_specs=[pl.BlockSpec(memory_space=pltpu.VMEM), pl.BlockSpec(memory_space=pl.ANY)],
        out_specs=[pl.BlockSpec(memory_space=pltpu.VMEM), pl.BlockSpec(memory_space=pl.ANY)],
        scratch_shapes=COMM_SEMS + COMM_SEMS,
    )(small, big)


def _scatter_blocks(g, name):
    _, r, c = g.shape

    def body(g_ref, out_ref, send_sems, recv_sems, local_sem):
        start, finish = _scatter_phases(g_ref, out_ref, send_sems, recv_sems, local_sem)
        start()
        finish()

    return pl.pallas_call(
        body, name=name,
        out_shape=jax.ShapeDtypeStruct(g.shape, g.dtype),
        in_specs=[pl.BlockSpec(memory_space=pl.ANY)],
        out_specs=pl.BlockSpec(memory_space=pl.ANY),
        scratch_shapes=[pltpu.SemaphoreType.DMA((7,)), pltpu.SemaphoreType.DMA((7,)), pltpu.SemaphoreType.DMA],
    )(g)


_DIMS = {"nn": (((1,), (0,)), ((), ())), "nt": (((1,), (1,)), ((), ())), "tn": (((0,), (0,)), ((), ()))}


def _matmul(a, b, mode, out_dtype, name, epilogue=None, side=None):
    if mode == "nn":
        (m, k), (_, n) = a.shape, b.shape
    elif mode == "nt":
        (m, k), (n, _) = a.shape, b.shape
    else:
        (k, m), (_, n) = a.shape, b.shape
    tm = _pick(m, (1024, 512, 256, 128))
    if mode == "tn":
        tn = _pick(n, (1536, 1024, 512, 256, 128))
        tk = _pick(k, (2048, 1024, 512, 256, 128)) if b.dtype == bf16 else _pick(k, (1024, 512, 256, 128))
    else:
        tn = _pick(n, (1024, 512, 384, 256, 128))
        tk = _pick(k, (2048, 1024, 512, 256, 128)) if a.dtype == bf16 else _pick(k, (1024, 512, 256, 128))
    nk = k // tk
    dims = _DIMS[mode]
    n_in = 3 if epilogue == "drelu2" else 2

    def body(*refs):
        a_ref, b_ref = refs[:2]
        o_ref, acc_ref = refs[n_in], refs[n_in + 1]
        kk = pl.program_id(2)
        part = lax.dot_general(a_ref[...].astype(bf16), b_ref[...].astype(bf16), dims, preferred_element_type=f32)

        def finish(acc):
            if epilogue == "relu2":
                r = jnp.maximum(acc, 0.0)
                o_ref[...] = (r * r).astype(out_dtype)
            elif epilogue == "drelu2":
                o_ref[...] = (acc * (2.0 * jnp.sqrt(refs[2][...].astype(f32)))).astype(out_dtype)
            else:
                o_ref[...] = acc.astype(out_dtype)

        if nk == 1:
            finish(part)
        else:
            @pl.when(kk == 0)
            def _():
                acc_ref[...] = part

            @pl.when(kk > 0)
            def _():
                acc_ref[...] += part

            @pl.when(kk == nk - 1)
            def _():
                finish(acc_ref[...])

    if mode == "tn":
        a_spec = pl.BlockSpec((tk, tm), lambda i, j, kk: (kk, i))
    else:
        a_spec = pl.BlockSpec((tm, tk), lambda i, j, kk: (i, kk))
    if mode == "nt":
        b_spec = pl.BlockSpec((tn, tk), lambda i, j, kk: (j, kk))
    else:
        b_spec = pl.BlockSpec((tk, tn), lambda i, j, kk: (kk, j))
    o_spec = pl.BlockSpec((tm, tn), lambda i, j, kk: (i, j))
    o_shape = jax.ShapeDtypeStruct((m, n), out_dtype)
    return pl.pallas_call(
        body, name=name, grid=(m // tm, n // tn, nk),
        in_specs=[a_spec, b_spec] + ([o_spec] if epilogue == "drelu2" else []),
        out_specs=o_spec, out_shape=o_shape,
        scratch_shapes=[pltpu.VMEM((tm, tn), f32)],
        compiler_params=_cparams(dimension_semantics=("parallel", "parallel", "arbitrary")),
    )(*((a, b, side) if epilogue == "drelu2" else (a, b)))


@jax.custom_vjp
def mlp(h, w1, w1grad, w2, w2grad):
    r = _matmul(h, w1, "nn", bf16, "mlp1_fwd", epilogue="relu2")
    return _matmul(r, w2, "nn", f32, "mlp2_fwd")


def _mlp_fwd(h, w1, w1grad, w2, w2grad):
    r = _matmul(h, w1, "nn", bf16, "mlp1_fwd", epilogue="relu2")
    return _matmul(r, w2, "nn", f32, "mlp2_fwd"), (h, w1, w2, r)


def _mlp_bwd(res, dy):
    h, w1, w2, r = res
    du = _matmul(dy, w2, "nt", bf16, "mlp2_dgrad", epilogue="drelu2", side=r)
    dw2 = _matmul(r, dy, "tn", f32, "mlp2_wgrad")
    dh = _matmul(du, w1, "nt", h.dtype, "mlp1_dgrad")
    dw1 = _matmul(h, du, "tn", f32, "mlp1_wgrad")
    return dh, jnp.zeros_like(w1), dw1, jnp.zeros_like(w2), dw2


mlp.defvjp(_mlp_fwd, _mlp_bwd)


def make_linear(name):
    @jax.custom_vjp
    def linear(a, w, wgrad):
        return _matmul(a, w, "nn", f32, name + "_fwd")

    def fwd(a, w, wgrad):
        return linear(a, w, wgrad), (a, w)

    def bwd(res, dy):
        a, w = res
        da = _matmul(dy, w, "nt", a.dtype, name + "_dgrad")
        dw = _matmul(a, dy, "tn", f32, name + "_wgrad")
        return da, jnp.zeros_like(w), dw

    linear.defvjp(fwd, bwd)
    return linear


def _in_proj_dgrad(dys, ws, g):
    s, d = dys[0].shape[0], ws[0].shape[0]
    tm = _pick(s, (1024, 512, 256, 128))
    tks = [w.shape[1] if w.shape[1] <= 1024 else 512 for w in ws]
    steps = [w.shape[1] // tk for w, tk in zip(ws, tks)]
    starts = [sum(steps[:p]) for p in range(len(ws))]
    total = sum(steps)
    n_p, n_i = len(ws), s // tm
    assert steps[0] == 1

    def body(*refs):
        dy_refs, w_refs, g_ref = refs[:n_p], refs[n_p:2 * n_p], refs[2 * n_p]
        dh_ref, recv_ref, acc_ref, send_sems, recv_sems, local_sem = refs[2 * n_p + 1:]
        i, t = pl.program_id(0), pl.program_id(1)
        start, finish = _scatter_phases(g_ref, recv_ref, send_sems, recv_sems, local_sem)

        @pl.when((i == 0) & (t == 0))
        def _():
            start()

        for p in range(n_p):
            @pl.when((t >= starts[p]) & (t < starts[p] + steps[p]))
            def _(p=p):
                part = lax.dot_general(dy_refs[p][...].astype(bf16), w_refs[p][...], _DIMS["nt"],
                                       preferred_element_type=f32)
                if p == 0:
                    acc_ref[...] = part
                else:
                    acc_ref[...] += part

        @pl.when(t == total - 1)
        def _():
            dh_ref[...] = acc_ref[...].astype(dh_ref.dtype)

        @pl.when((i == n_i - 1) & (t == total - 1))
        def _():
            finish()

    def piece_map(p, rows):
        def index_map(i, t):
            blk = jnp.clip(t - starts[p], 0, steps[p] - 1)
            return (i, blk) if rows else (0, blk)

        return index_map

    hbm = pl.BlockSpec(memory_space=pl.ANY)
    in_specs = [pl.BlockSpec((tm, tks[p]), piece_map(p, True)) for p in range(n_p)]
    in_specs += [pl.BlockSpec((d, tks[p]), piece_map(p, False)) for p in range(n_p)]
    return pl.pallas_call(
        body, name="in_proj_dgrad", grid=(n_i, total), in_specs=in_specs + [hbm],
        out_specs=[pl.BlockSpec((tm, d), lambda i, t: (i, 0)), hbm],
        out_shape=[jax.ShapeDtypeStruct((s, d), bf16), jax.ShapeDtypeStruct(g.shape, g.dtype)],
        scratch_shapes=[pltpu.VMEM((tm, d), f32)] + COMM_SEMS,
        compiler_params=_cparams(dimension_semantics=("arbitrary", "arbitrary")),
    )(*dys, *ws, g)


@jax.custom_vjp
def in_proj(h, ws, recv_like):
    return tuple(_matmul(h, w, "nn", dt, "lin_" + n + "_fwd") for n, w, dt in zip(PROJ_NAMES, ws, PROJ_DTYPES))


def _in_proj_fwd(h, ws, recv_like):
    return in_proj(h, ws, recv_like), (h, ws)


def _in_proj_bwd(res, dys):
    h, ws = res
    dws = {n: _matmul(h, dy, "tn", f32, "lin_" + n + "_wgrad") for n, dy in zip(PROJ_NAMES, dys)}
    dh, recv = _in_proj_dgrad(dys, ws, _pack_in_grads(dws))
    return dh.astype(h.dtype), tuple(jnp.zeros_like(w) for w in ws), recv


in_proj.defvjp(_in_proj_fwd, _in_proj_bwd)


def make_rowwise(name, fn, row_out, sum_out=(), tm_pref=512):
    def specs(rows, gpars, cpars, consts, tm):
        s = [pl.BlockSpec((tm, r.shape[1]), lambda i: (i, 0)) for r in rows]
        s += [pl.BlockSpec(p.shape, lambda i: (0, 0)) for p in gpars]
        s += [pl.BlockSpec(p.shape, lambda i: (0, 0)) for p in cpars]
        for cst in consts:
            nb = cst.shape[0] // tm
            s.append(pl.BlockSpec((tm, cst.shape[1]), lambda i, nb=nb: (i % nb, 0)))
        return s

    def tile_rows(rows, consts):
        r = rows[0].shape[0]
        common = math.gcd(r, *[cst.shape[0] for cst in consts])
        tm = _pick(common, (tm_pref, 512, 256, 128, 64, 32, 16, 8))
        return r, tm

    def forward(rows, gpars, cpars, consts):
        r, tm = tile_rows(rows, consts)
        nr, ng, nc, nk = len(rows), len(gpars), len(cpars), len(consts)

        def body(*refs):
            ins = refs[:nr + ng + nc + nk]
            outs = refs[nr + ng + nc + nk:]
            rv = [t[...].astype(f32) for t in ins[:nr]]
            gv = [t[...].astype(f32) for t in ins[nr:nr + ng]]
            cv = [t[...] for t in ins[nr + ng:nr + ng + nc]]
            kv = [t[...].astype(f32) for t in ins[nr + ng + nc:]]
            ro, so = fn(rv, gv, cv, kv)
            for o_ref, val in zip(outs[:len(row_out)], ro):
                o_ref[...] = val.astype(o_ref.dtype)
            if sum_out:
                @pl.when(pl.program_id(0) == 0)
                def _():
                    for o_ref in outs[len(row_out):]:
                        o_ref[...] = jnp.zeros_like(o_ref)
                for o_ref, val in zip(outs[len(row_out):], so):
                    o_ref[...] += val

        out_specs = [pl.BlockSpec((tm, w), lambda i: (i, 0)) for w, _ in row_out]
        out_specs += [pl.BlockSpec(shp, lambda i: (0, 0)) for shp in sum_out]
        out_shape = [jax.ShapeDtypeStruct((r, w), dt) for w, dt in row_out]
        out_shape += [jax.ShapeDtypeStruct(shp, f32) for shp in sum_out]
        res = pl.pallas_call(
            body, name=name + "_fwd", grid=(r // tm,),
            in_specs=specs(rows, gpars, cpars, consts, tm), out_specs=out_specs, out_shape=out_shape,
            compiler_params=_cparams(dimension_semantics=("arbitrary",)),
        )(*rows, *gpars, *cpars, *consts)
        return tuple(res[:len(row_out)]), tuple(res[len(row_out):])

    def backward(rows, gpars, cpars, consts, d_ro, d_so):
        r, tm = tile_rows(rows, consts)
        nr, ng, nc, nk = len(rows), len(gpars), len(cpars), len(consts)
        n_in = nr + ng + nc + nk + len(row_out) + len(sum_out)

        def body(*refs):
            ins, outs = refs[:n_in], refs[n_in:]
            rv = [t[...].astype(f32) for t in ins[:nr]]
            gv = [t[...].astype(f32) for t in ins[nr:nr + ng]]
            cv = [t[...] for t in ins[nr + ng:nr + ng + nc]]
            kv = [t[...].astype(f32) for t in ins[nr + ng + nc:nr + ng + nc + nk]]
            o = nr + ng + nc + nk
            dro = [t[...].astype(f32) for t in ins[o:o + len(row_out)]]
            dso = [t[...] for t in ins[o + len(row_out):]]
            _, vjp = jax.vjp(lambda a, b: tuple(tuple(t) for t in fn(a, b, cv, kv)), rv, gv)
            drv, dgv = vjp((tuple(dro), tuple(dso)))
            for o_ref, val in zip(outs[:nr], drv):
                o_ref[...] = val.astype(o_ref.dtype)
            if ng:
                @pl.when(pl.program_id(0) == 0)
                def _():
                    for o_ref in outs[nr:]:
                        o_ref[...] = jnp.zeros_like(o_ref)
                for o_ref, val in zip(outs[nr:], dgv):
                    o_ref[...] += val

        in_specs = specs(rows, gpars, cpars, consts, tm)
        in_specs += [pl.BlockSpec((tm, w), lambda i: (i, 0)) for w, _ in row_out]
        in_specs += [pl.BlockSpec(shp, lambda i: (0, 0)) for shp in sum_out]
        out_specs = [pl.BlockSpec((tm, t.shape[1]), lambda i: (i, 0)) for t in rows]
        out_specs += [pl.BlockSpec(p.shape, lambda i: (0, 0)) for p in gpars]
        out_shape = [jax.ShapeDtypeStruct(t.shape, t.dtype) for t in rows]
        out_shape += [jax.ShapeDtypeStruct(p.shape, f32) for p in gpars]
        res = pl.pallas_call(
            body, name=name + "_bwd", grid=(r // tm,),
            in_specs=in_specs, out_specs=out_specs, out_shape=out_shape,
            compiler_params=_cparams(dimension_semantics=("arbitrary",)),
        )(*rows, *gpars, *cpars, *consts, *d_ro, *d_so)
        return tuple(res[:nr]), tuple(res[nr:])

    @jax.custom_vjp
    def op(rows, gpars, cpars, consts):
        return forward(rows, gpars, cpars, consts)

    def op_fwd(rows, gpars, cpars, consts):
        return forward(rows, gpars, cpars, consts), (rows, gpars, cpars, consts)

    def op_bwd(res, cts):
        rows, gpars, cpars, consts = res
        d_ro, d_so = cts
        drows, dg = backward(rows, gpars, cpars, consts, d_ro, d_so)
        dg = tuple(d.astype(p.dtype) for d, p in zip(dg, gpars))
        return (drows, dg, tuple(jnp.zeros_like(p) for p in cpars), tuple(jnp.zeros_like(k) for k in consts))

    op.defvjp(op_fwd, op_bwd)
    return op


def _rms(x):
    return x * lax.rsqrt(jnp.mean(x * x, axis=-1, keepdims=True) + EPS)


def _silu(x):
    return x * jax.nn.sigmoid(x)


def _fn_norm_mod(rows, gp, cp, ks):
    (x,), (nw, sc, sh) = rows, gp
    return ((_rms(x) * nw) * (1.0 + sc) + sh, x), ()


PAIR = 2 * HEAD_DIM


def _exact_dot(a, m):
    hi = a.astype(bf16)
    lo = (a - hi.astype(f32)).astype(bf16)
    return jnp.dot(hi, m, preferred_element_type=f32) + jnp.dot(lo, m, preferred_element_type=f32)


def _make_sel_dot(sign):
    @jax.custom_vjp
    def sel_dot(a, m):
        return _exact_dot(a, m)

    def fwd(a, m):
        return _exact_dot(a, m), m

    def bwd(m, g):
        return sign * _exact_dot(g, m), jnp.zeros_like(m)

    sel_dot.defvjp(fwd, bwd)
    return sel_dot


_head_sum_dot = _make_sel_dot(1.0)
_rope_perm_dot = _make_sel_dot(-1.0)


def _pair_norm_rope(t, w2, gsum, perm, cos2, sin2, out_scale):
    ss = _head_sum_dot(t * t, gsum)
    u = t * lax.rsqrt(ss * (1.0 / HEAD_DIM) + EPS) * w2
    return (u * cos2 + _rope_perm_dot(u, perm) * sin2) * out_scale


def _pair_consts():
    eye = np.eye(2, dtype=np.float32)
    gsum = np.kron(eye, np.ones((HEAD_DIM, HEAD_DIM), np.float32))
    return jnp.asarray(gsum, bf16), jnp.asarray(np.kron(eye, _rope_perm()), bf16)


def make_head_rope(name, nh, out_scale, head_major):
    width = nh * HEAD_DIM
    fn = functools.partial(_pair_norm_rope, out_scale=out_scale)

    def out_spec(tm):
        if head_major:
            return pl.BlockSpec((nh, tm, HEAD_DIM), lambda i: (0, i, 0))
        return pl.BlockSpec((tm, width), lambda i: (i, 0))

    def specs(tm):
        def full(shp):
            return pl.BlockSpec(shp, lambda i: (0, 0))

        return [pl.BlockSpec((tm, width), lambda i: (i, 0)), full((1, PAIR)), full((PAIR, PAIR)), full((PAIR, PAIR)),
                pl.BlockSpec((tm, PAIR), lambda i: (i, 0)), pl.BlockSpec((tm, PAIR), lambda i: (i, 0))]

    def forward(t, w2, gsum, perm, cos2, sin2):
        s = t.shape[0]
        tm = _pick(s, (1024, 512, 256, 128))

        def body(t_ref, w_ref, g_ref, p_ref, cos_ref, sin_ref, o_ref):
            for b in range(nh // 2):
                val = fn(t_ref[:, b * PAIR:(b + 1) * PAIR].astype(f32), w_ref[...], g_ref[...], p_ref[...], cos_ref[...],
                         sin_ref[...]).astype(o_ref.dtype)
                if head_major:
                    o_ref[2 * b] = val[:, :HEAD_DIM]
                    o_ref[2 * b + 1] = val[:, HEAD_DIM:]
                else:
                    o_ref[:, b * PAIR:(b + 1) * PAIR] = val

        return pl.pallas_call(
            body, name=name + "_fwd", grid=(s // tm,), in_specs=specs(tm), out_specs=out_spec(tm),
            out_shape=jax.ShapeDtypeStruct((nh, s, HEAD_DIM) if head_major else (s, width), bf16),
            compiler_params=_cparams(dimension_semantics=("arbitrary",)),
        )(t, w2, gsum, perm, cos2, sin2)

    def backward(t, w2, gsum, perm, cos2, sin2, dout):
        s = t.shape[0]
        tm = _pick(s, (1024, 512, 256, 128))

        def body(t_ref, w_ref, g_ref, p_ref, cos_ref, sin_ref, do_ref, dt_ref, dw_ref, pair_buf):
            @pl.when(pl.program_id(0) == 0)
            def _():
                dw_ref[...] = jnp.zeros_like(dw_ref)

            g_v, p_v, cos_v, sin_v = g_ref[...], p_ref[...], cos_ref[...], sin_ref[...]
            dw = jnp.zeros((1, PAIR), f32)
            for b in range(nh // 2):
                sl = slice(b * PAIR, (b + 1) * PAIR)
                if head_major:
                    pair_buf[:, :HEAD_DIM] = do_ref[2 * b].astype(f32)
                    pair_buf[:, HEAD_DIM:] = do_ref[2 * b + 1].astype(f32)
                    ct = pair_buf[...]
                else:
                    ct = do_ref[:, sl].astype(f32)
                _, vjp = jax.vjp(lambda a, c: fn(a, c, g_v, p_v, cos_v, sin_v), t_ref[:, sl].astype(f32), w_ref[...])
                dtb, dwb = vjp(ct)
                dt_ref[:, sl] = dtb.astype(dt_ref.dtype)
                dw = dw + dwb
            dw_ref[...] += dw

        return pl.pallas_call(
            body, name=name + "_bwd", grid=(s // tm,), in_specs=specs(tm) + [out_spec(tm)],
            out_specs=[pl.BlockSpec((tm, width), lambda i: (i, 0)), pl.BlockSpec((1, PAIR), lambda i: (0, 0))],
            out_shape=[jax.ShapeDtypeStruct((s, width), t.dtype), jax.ShapeDtypeStruct((1, PAIR), f32)],
            scratch_shapes=[pltpu.VMEM((tm, PAIR), f32)],
            compiler_params=_cparams(dimension_semantics=("arbitrary",)),
        )(t, w2, gsum, perm, cos2, sin2, dout)

    @jax.custom_vjp
    def op(t, w2, gsum, perm, cos2, sin2):
        return forward(t, w2, gsum, perm, cos2, sin2)

    def op_fwd(*args):
        return forward(*args), args

    def op_bwd(res, dout):
        dt, dw = backward(*res, dout)
        return (dt, dw) + tuple(jnp.zeros_like(r) for r in res[2:])

    op.defvjp(op_fwd, op_bwd)

    def apply(t, w, cos2, sin2):
        gsum, perm = _pair_consts()
        return op(t, jnp.concatenate([w, w], axis=-1), gsum, perm, cos2, sin2)

    return apply


def _fn_softplus(rows, gp, cp, ks):
    (x,), (b,) = rows, gp
    v = x + b
    return (jnp.maximum(v, 0.0) + jnp.log(1.0 + jnp.exp(-jnp.abs(v))),), ()


def _fn_ssd_gate(rows, gp, cp, ks):
    (y, z), (nw,) = rows, gp
    return (_rms(y * _silu(z)) * nw,), ()


def _fn_merge(rows, gp, cp, ks):
    ao, so, ga, gs = rows
    return (jax.nn.sigmoid(ga) * ao + jax.nn.sigmoid(gs) * so,), ()


def _fn_res_norm(rows, gp, cp, ks):
    (x, mo), (g1, nw, sc, sh) = rows, gp
    x1 = x + g1 * mo
    return (x1, (_rms(x1) * nw) * (1.0 + sc) + sh), ()


def _fn_loss(rows, gp, cp, ks):
    (x1, ff), (g2,), (tgt,) = rows, gp, ks
    err = x1 + g2 * ff - tgt
    return (), (0.5 * jnp.sum(jnp.sum(err * err, axis=-1, keepdims=True), axis=0, keepdims=True) / D_MODEL,)


HALO = 8
HALO_BWD = 16


def _conv_tiles(s, c, wide):
    return _pick(s, (512, 256, 128)), _pick(c, (1024, 512, 256, 128) if wide else (512, 256, 128))


def _halo_specs(tm, tc, s, halo=HALO):
    nb = tm // halo
    last = s // halo - 1
    cur = pl.BlockSpec((tm, tc), lambda j, i: (i, j))
    prev = pl.BlockSpec((halo, tc), lambda j, i: (jnp.maximum(i * nb - 1, 0), j))
    nxt = pl.BlockSpec((halo, tc), lambda j, i: (jnp.minimum((i + 1) * nb, last), j))
    return cur, prev, nxt


def _fill_halo(buf, cur, prev, nxt, tm, i, n_i, halo=HALO):
    buf[halo:halo + tm, :] = cur[...]
    buf[0:halo, :] = jnp.where(i > 0, prev[...], 0.0)
    buf[halo + tm:, :] = jnp.where(i < n_i - 1, nxt[...], 0.0)


def _conv_fwd(x, w, b, shard):
    s, c = x.shape
    tm, tc = _conv_tiles(s, c, True)
    n_i, n_j = s // tm, c // tc

    def body(cur, prev, nxt, w_ref, b_ref, shard_ref, o_ref, gath_ref, buf, send_sems, recv_sems, local_sem):
        j, i = pl.program_id(0), pl.program_id(1)
        start, finish = _gather_phases(shard_ref, gath_ref, send_sems, recv_sems, local_sem)

        @pl.when((j == 0) & (i == 0))
        def _():
            start()

        _fill_halo(buf, cur, prev, nxt, tm, i, n_i)
        pre = jnp.zeros((tm, tc), f32) + b_ref[...]
        for k in range(D_CONV):
            pre = pre + buf[HALO - 2 + k:HALO - 2 + k + tm, :] * w_ref[k:k + 1, :]
        o_ref[...] = _silu(pre)

        @pl.when((j == n_j - 1) & (i == n_i - 1))
        def _():
            finish()

    cur, prev, nxt = _halo_specs(tm, tc, s)
    hbm = pl.BlockSpec(memory_space=pl.ANY)
    return pl.pallas_call(
        body, name="conv_silu_fwd", grid=(n_j, n_i),
        in_specs=[cur, prev, nxt, pl.BlockSpec((D_CONV, tc), lambda j, i: (0, j)),
                  pl.BlockSpec((1, tc), lambda j, i: (0, j)), hbm],
        out_specs=[pl.BlockSpec((tm, tc), lambda j, i: (i, j)), hbm],
        out_shape=[jax.ShapeDtypeStruct((s, c), f32), jax.ShapeDtypeStruct((N_DEV,) + shard.shape, shard.dtype)],
        scratch_shapes=[pltpu.VMEM((tm + 2 * HALO, tc), f32)] + COMM_SEMS,
        compiler_params=_cparams(dimension_semantics=("arbitrary", "arbitrary")),
    )(x, x, x, w, b, shard)


def _conv_bwd(x, w, b, dy, g):
    s, c = x.shape
    tm, tc = _conv_tiles(s, c, False)
    n_i, n_j = s // tm, c // tc
    ext = tm + 16

    def body(cur, prev, nxt, dcur, dprev, dnxt, w_ref, b_ref, g_ref, dx_ref, dw_ref, db_ref, recv_ref,
             xbuf, dbuf, pbuf, send_sems, recv_sems, local_sem):
        j, i = pl.program_id(0), pl.program_id(1)
        start, finish = _scatter_phases(g_ref, recv_ref, send_sems, recv_sems, local_sem)

        @pl.when((j == 0) & (i == 0))
        def _():
            start()

        _fill_halo(xbuf, cur, prev, nxt, tm, i, n_i, HALO_BWD)
        _fill_halo(dbuf, dcur, dprev, dnxt, tm, i, n_i, HALO_BWD)
        xs = [xbuf[6 + k:6 + k + ext, :] for k in range(D_CONV)]
        pre = jnp.zeros((ext, tc), f32) + b_ref[...]
        for k in range(D_CONV):
            pre = pre + xs[k] * w_ref[k:k + 1, :]
        sg = jax.nn.sigmoid(pre)
        pbuf[...] = dbuf[8:8 + ext, :] * (sg * (1.0 + pre * (1.0 - sg)))
        dx = jnp.zeros((tm, tc), f32)
        for k in range(D_CONV):
            dx = dx + pbuf[10 - k:10 - k + tm, :] * w_ref[k:k + 1, :]
        dx_ref[...] = dx

        @pl.when(i == 0)
        def _():
            dw_ref[...] = jnp.zeros_like(dw_ref)
            db_ref[...] = jnp.zeros_like(db_ref)

        dpre = pbuf[8:8 + tm, :]
        db_ref[...] += jnp.sum(dpre, axis=0, keepdims=True)
        for k in range(D_CONV):
            dw_ref[k:k + 1, :] += jnp.sum(dpre * xs[k][8:8 + tm, :], axis=0, keepdims=True)

        @pl.when((j == n_j - 1) & (i == n_i - 1))
        def _():
            finish()

    cur, prev, nxt = _halo_specs(tm, tc, s, HALO_BWD)
    hbm = pl.BlockSpec(memory_space=pl.ANY)
    return pl.pallas_call(
        body, name="conv_silu_bwd", grid=(n_j, n_i),
        in_specs=[cur, prev, nxt, cur, prev, nxt, pl.BlockSpec((D_CONV, tc), lambda j, i: (0, j)),
                  pl.BlockSpec((1, tc), lambda j, i: (0, j)), hbm],
        out_specs=[pl.BlockSpec((tm, tc), lambda j, i: (i, j)), pl.BlockSpec((D_CONV, tc), lambda j, i: (0, j)),
                   pl.BlockSpec((1, tc), lambda j, i: (0, j)), hbm],
        out_shape=[jax.ShapeDtypeStruct((s, c), f32), jax.ShapeDtypeStruct((D_CONV, c), f32),
                   jax.ShapeDtypeStruct((1, c), f32), jax.ShapeDtypeStruct(g.shape, g.dtype)],
        scratch_shapes=[pltpu.VMEM((tm + 2 * HALO_BWD, tc), f32), pltpu.VMEM((tm + 2 * HALO_BWD, tc), f32),
                        pltpu.VMEM((ext, tc), f32)] + COMM_SEMS,
        compiler_params=_cparams(dimension_semantics=("arbitrary", "arbitrary")),
    )(x, x, x, dy, dy, dy, w, b, g)


@jax.custom_vjp
def conv_silu_comm(x, w, b, shard, recv_like):
    act, gathered = _conv_fwd(x, w, b, shard)
    return (act, gathered) + tuple(jnp.zeros(shp, f32) for shp in LATE_SHAPES)


def _conv_silu_comm_fwd(x, w, b, shard, recv_like):
    return conv_silu_comm(x, w, b, shard, recv_like), (x, w, b, shard)


def _conv_silu_comm_bwd(res, cts):
    x, w, b, shard = res
    dx, dw, db, recv = _conv_bwd(x, w, b, cts[0], _pack_late_grads(dict(zip(LATE, cts[2:]))))
    return dx, dw, db, jnp.zeros_like(shard), recv


conv_silu_comm.defvjp(_conv_silu_comm_fwd, _conv_silu_comm_bwd)


ATT_SCALE = HEAD_DIM ** -0.5
Q_SCALE = ATT_SCALE * math.log2(math.e)
LN2 = math.log(2.0)
REP = N_Q_HEADS // N_KV_HEADS


HP = 2
assert REP % HP == 0


def _attn_fwd(q, k, v):
    s, dh = q.shape[0], HEAD_DIM
    hq = q.shape[1] // dh
    tq = _pick(s, (256, 128))

    v1 = jnp.concatenate([v, jnp.ones(v.shape[:2] + (1,), v.dtype), jnp.zeros(v.shape[:2] + (dh - 1,), v.dtype)],
                         axis=-1)

    def body(q_ref, k_ref, v_ref, o_ref, p_ref, linv_ref):
        for j in range(HP):
            sl = slice(j * dh, (j + 1) * dh)
            sc = lax.dot_general(q_ref[:, sl], k_ref[0], _DIMS["nt"], preferred_element_type=f32)
            m = jnp.max(sc, axis=-1, keepdims=True)
            p = jnp.exp2(sc - m).astype(bf16)
            p_ref[j] = p
            o1 = jnp.dot(p, v_ref[0], preferred_element_type=f32)
            linv = 1.0 / o1[:, dh:dh + 1]
            o_ref[:, sl] = (o1[:, :dh] * linv).astype(o_ref.dtype)
            linv_ref[j] = linv

    return pl.pallas_call(
        body, name="attn_fwd", grid=(hq // HP, s // tq),
        in_specs=[pl.BlockSpec((tq, HP * dh), lambda h, i: (i, h)),
                  pl.BlockSpec((1, s, dh), lambda h, i: (h * HP // REP, 0, 0)),
                  pl.BlockSpec((1, s, 2 * dh), lambda h, i: (h * HP // REP, 0, 0))],
        out_specs=[pl.BlockSpec((tq, HP * dh), lambda h, i: (i, h)),
                   pl.BlockSpec((HP, tq, s), lambda h, i: (h, i, 0)),
                   pl.BlockSpec((HP, tq, 1), lambda h, i: (h, i, 0))],
        out_shape=[jax.ShapeDtypeStruct((s, hq * dh), bf16), jax.ShapeDtypeStruct((hq, s, s), bf16),
                   jax.ShapeDtypeStruct((hq, s, 1), f32)],
        compiler_params=_cparams(dimension_semantics=("parallel", "arbitrary")),
    )(q, k, v1)


def _attn_bwd(p, do, o, q, k, v, linv):
    hq, s, _ = p.shape
    dh = HEAD_DIM
    tq = _pick(s, (256, 128))

    def body(p_ref, do_ref, o_ref, q_ref, k_ref, v_ref, linv_ref, dq_ref, dkt_ref, dvt_ref):
        @pl.when(pl.program_id(1) == 0)
        def _():
            dkt_ref[...] = jnp.zeros_like(dkt_ref)
            dvt_ref[...] = jnp.zeros_like(dvt_ref)

        for j in range(HP):
            sl = slice(j * dh, (j + 1) * dh)
            pp, doh, li = p_ref[j], do_ref[:, sl], linv_ref[j]
            do32 = doh.astype(f32)
            d = jnp.sum(do32 * o_ref[:, sl].astype(f32), axis=-1, keepdims=True)
            dp = lax.dot_general(doh, v_ref[0], _DIMS["nt"], preferred_element_type=f32)
            ds = (pp.astype(f32) * ((dp - d) * li)).astype(bf16)
            dq_ref[:, sl] = (jnp.dot(ds, k_ref[0], preferred_element_type=f32) * LN2).astype(dq_ref.dtype)
            dvt_ref[j] += lax.dot_general((do32 * li).astype(bf16), pp, _DIMS["tn"], preferred_element_type=f32)
            dkt_ref[j] += lax.dot_general(q_ref[:, sl], ds, _DIMS["tn"], preferred_element_type=f32)

    def row():
        return pl.BlockSpec((tq, HP * dh), lambda h, i: (i, h))

    return pl.pallas_call(
        body, name="attn_bwd", grid=(hq // HP, s // tq),
        in_specs=[pl.BlockSpec((HP, tq, s), lambda h, i: (h, i, 0)), row(), row(), row(),
                  pl.BlockSpec((1, s, dh), lambda h, i: (h * HP // REP, 0, 0)),
                  pl.BlockSpec((1, s, dh), lambda h, i: (h * HP // REP, 0, 0)),
                  pl.BlockSpec((HP, tq, 1), lambda h, i: (h, i, 0))],
        out_specs=[row(), pl.BlockSpec((HP, dh, s), lambda h, i: (h, 0, 0)),
                   pl.BlockSpec((HP, dh, s), lambda h, i: (h, 0, 0))],
        out_shape=[jax.ShapeDtypeStruct((s, hq * dh), q.dtype), jax.ShapeDtypeStruct((hq, dh, s), f32),
                   jax.ShapeDtypeStruct((hq, dh, s), f32)],
        compiler_params=_cparams(dimension_semantics=("parallel", "arbitrary")),
    )(p, do, o, q, k, v, linv)


@jax.custom_vjp
def attention(q, k, v):
    return _attn_fwd(q, k, v)[0]


def _attention_fwd(q, k, v):
    o, p, linv = _attn_fwd(q, k, v)
    return o, (q, k, v, o, p, linv)


def _attention_bwd(res, do):
    q, k, v, o, p, linv = res
    s = q.shape[0]
    dq, dkt, dvt = _attn_bwd(p, do.astype(bf16), o, q, k, v, linv)

    def per_kv_head(t):
        return jnp.swapaxes(t.reshape(N_KV_HEADS, REP, HEAD_DIM, s).sum(axis=1), 1, 2)

    return dq, (per_kv_head(dkt) * LN2).astype(k.dtype), per_kv_head(dvt).astype(v.dtype)


attention.defvjp(_attention_fwd, _attention_bwd)


HPG = N_SSD_HEADS // N_SSD_GROUPS
GW = HPG * SSD_HEAD_DIM
NEG = -1e30
SPLIT_ROWS = 32


def _ssd_consts():
    k = np.arange(SPLIT_ROWS)[:, None]
    live = k < 3 * HPG
    sel_chunk = ((k % HPG) == (np.arange(HPG * CHUNK)[None, :] // CHUNK)) & live
    sel_head = ((k % HPG) == (np.arange(GW)[None, :] // SSD_HEAD_DIM)) & live
    return jnp.asarray(sel_chunk, bf16), jnp.asarray(sel_head, bf16)


def _split3(x):
    hi = x.astype(bf16).astype(f32)
    r1 = x - hi
    mid = r1.astype(bf16).astype(f32)
    lo = (r1 - mid).astype(bf16).astype(f32)
    return jnp.concatenate([hi, mid, lo, jnp.zeros_like(hi)], axis=0).astype(bf16)


def _tn(a, b):
    return lax.dot_general(a, b, _DIMS["tn"], preferred_element_type=f32)


def _nt(a, b):
    return lax.dot_general(a, b, _DIMS["nt"], preferred_element_type=f32)


def _nn(a, b):
    return jnp.dot(a, b, preferred_element_type=f32)


def _head_sum(sel8, x):
    hi = x.astype(bf16)
    lo = (x - hi.astype(f32)).astype(bf16)
    return _nt(sel8, hi) + _nt(sel8, lo)


def _ssd_masks(reverse):
    r = lax.broadcasted_iota(jnp.int32, (CHUNK, CHUNK), 0)
    c = lax.broadcasted_iota(jnp.int32, (CHUNK, CHUNK), 1)
    lower, upper = r >= c, r <= c
    return (upper, lower) if reverse else (lower, upper)


def _ssd_in_specs(cidx):
    return [pl.BlockSpec((CHUNK, D_INNER), lambda c: (cidx(c), 0)),
            pl.BlockSpec((CHUNK, GN), lambda c: (cidx(c), D_INNER // GN)),
            pl.BlockSpec((CHUNK, GN), lambda c: (cidx(c), D_INNER // GN + 1)),
            pl.BlockSpec((N_SSD_HEADS, CHUNK), lambda c: (0, cidx(c))),
            pl.BlockSpec((N_SSD_HEADS, 1), lambda c: (0, 0)),
            pl.BlockSpec((SPLIT_ROWS, HPG * CHUNK), lambda c: (0, 0)),
            pl.BlockSpec((SPLIT_ROWS, GW), lambda c: (0, 0))]


def _ssd_chunk_common(dtt_ref, a_ref, et_ref, mask_t):
    dtt = dtt_ref[...]
    et = jnp.dot(dtt * a_ref[...], mask_t.astype(f32), precision=HIGHEST, preferred_element_type=f32)
    et_ref[...] = et
    return dtt, et


def _ssd_group_common(g, dtt, et, selc_ref, selh_ref, xs_ref, b_ref, c_ref, last):
    gr = slice(g * HPG, (g + 1) * HPG)
    e3 = _split3(et[gr])
    col = _tn(e3, selc_ref[...])
    eb = _tn(e3, selh_ref[...])
    dtb = _tn(_split3(dtt[gr]), selh_ref[...])
    tbc = eb[last:last + 1, :]
    xs = xs_ref[:, g * GW:(g + 1) * GW]
    bg = b_ref[:, g * D_STATE:(g + 1) * D_STATE].astype(bf16)
    cg = c_ref[:, g * D_STATE:(g + 1) * D_STATE].astype(bf16)
    return col, eb, dtb, tbc, xs, bg, cg


def _ssd_fwd(xbc, dtt, a_col, reverse, y_prev=None, dexp=None):
    s = xbc.shape[0]
    nc = s // CHUNK
    cidx = (lambda c: nc - 1 - c) if reverse else (lambda c: c)
    last = 0 if reverse else CHUNK - 1
    selc, selh = _ssd_consts()
    final = y_prev is not None
    n_in = 9 if final else 7

    def body(*refs):
        xs_ref, b_ref, c_ref, dtt_ref, a_ref, selc_ref, selh_ref = refs[:7]
        y_ref, st_ref, ht_ref, et_ref, yg_ref = refs[n_in:]

        @pl.when(pl.program_id(0) == 0)
        def _():
            ht_ref[...] = jnp.zeros_like(ht_ref)

        mask, mask_t = _ssd_masks(reverse)
        dtt_v, et = _ssd_chunk_common(dtt_ref, a_ref, et_ref, mask_t)
        for g in range(N_SSD_GROUPS):
            col, eb, dtb, tbc, xs, bg, cg = _ssd_group_common(g, dtt_v, et, selc_ref, selh_ref, xs_ref, b_ref, c_ref,
                                                              last)
            xd = xs * dtb
            cb = _nt(cg, bg)
            ht = ht_ref[g]
            st_ref[0, g] = ht
            yoff = _nn(cg, ht.astype(bf16)) * jnp.exp(eb)
            for j in range(HPG):
                h = g * HPG + j
                hs = slice(j * SSD_HEAD_DIM, (j + 1) * SSD_HEAD_DIM)
                lam = jnp.exp(jnp.where(mask, col[:, j * CHUNK:(j + 1) * CHUNK] - et_ref[h:h + 1, :], NEG))
                yg_ref[:, hs] = _nn((cb * lam).astype(bf16), xd[:, hs].astype(bf16))
            cols = slice(g * GW, (g + 1) * GW)
            yg = yg_ref[...] + yoff
            if final:
                yg = yg + refs[7][:, cols] + xs * refs[8][:, cols]
            y_ref[:, cols] = yg.astype(y_ref.dtype)
            ht_ref[g] = jnp.exp(tbc) * ht + _tn(bg, (xd * jnp.exp(tbc - eb)).astype(bf16))

    y_spec = pl.BlockSpec((CHUNK, D_INNER), lambda c: (cidx(c), 0))
    extra_specs = [y_spec, pl.BlockSpec((1, D_INNER), lambda c: (0, 0))] if final else []
    return pl.pallas_call(
        body, name="ssd_fwd_rev" if reverse else "ssd_fwd", grid=(nc,),
        in_specs=_ssd_in_specs(cidx) + extra_specs,
        out_specs=[y_spec, pl.BlockSpec((1, N_SSD_GROUPS, D_STATE, GW), lambda c: (cidx(c), 0, 0, 0))],
        out_shape=[jax.ShapeDtypeStruct((s, D_INNER), bf16 if final else f32),
                   jax.ShapeDtypeStruct((nc, N_SSD_GROUPS, D_STATE, GW), f32)],
        scratch_shapes=[pltpu.VMEM((N_SSD_GROUPS, D_STATE, GW), f32), pltpu.VMEM((N_SSD_HEADS, CHUNK), f32),
                        pltpu.VMEM((CHUNK, GW), f32)],
        compiler_params=_cparams(dimension_semantics=("arbitrary",)),
    )(xbc, xbc, xbc, dtt, a_col, selc, selh, *((y_prev, dexp) if final else ()))


def _ssd_bwd(xbc, dtt, a_col, states, dy, reverse, dxbc_prev=None, dexp=None):
    s = xbc.shape[0]
    nc = s // CHUNK
    cidx = (lambda c: c) if reverse else (lambda c: nc - 1 - c)
    last = 0 if reverse else CHUNK - 1
    selc, selh = _ssd_consts()
    final = dxbc_prev is not None
    n_in = 11 if final else 9
    n_out = 4 if final else 3

    def body(*refs):
        xs_ref, b_ref, c_ref, dtt_ref, a_ref, selc_ref, selh_ref, st_ref, dy_ref = refs[:9]
        dxbc_ref, ddtt_ref, da_ref = refs[n_in:n_in + 3]
        dh_ref, et_ref, det_ref, det2_ref, ddt_ref, q_ref = refs[n_in + n_out:]
        if final:
            prev_ref, dexp_ref, ddexp_ref = refs[9], refs[10], refs[n_in + 3]

        @pl.when(pl.program_id(0) == 0)
        def _():
            dh_ref[...] = jnp.zeros_like(dh_ref)
            da_ref[...] = jnp.zeros_like(da_ref)
            if final:
                ddexp_ref[...] = jnp.zeros_like(ddexp_ref)

        mask, mask_t = _ssd_masks(reverse)
        dtt_v, et = _ssd_chunk_common(dtt_ref, a_ref, et_ref, mask_t)
        sel8 = selh_ref[0:HPG, :]
        is_last = lax.broadcasted_iota(jnp.int32, (CHUNK, GW), 0) == last
        for g in range(N_SSD_GROUPS):
            col, eb, dtb, tbc, xs, bg, cg = _ssd_group_common(g, dtt_v, et, selc_ref, selh_ref, xs_ref, b_ref, c_ref,
                                                              last)
            xd = xs * dtb
            cb = _nt(cg, bg)
            cbt = _nt(bg, cg)
            exp_t = jnp.exp(tbc)
            dfac = jnp.exp(tbc - eb)
            ht = st_ref[0, g]
            dhn = dh_ref[g]
            ht16, dhn16 = ht.astype(bf16), dhn.astype(bf16)
            dy = dy_ref[:, g * GW:(g + 1) * GW].astype(f32)
            dye = dy * jnp.exp(eb)
            dye16 = dye.astype(bf16)
            dc = _nt(dye16, ht16)
            dh_ref[g] = exp_t * dhn + _tn(cg, dye16)
            deb = dye * _nn(cg, ht16)
            xdd = xd * dfac
            dxdd = _nn(bg, dhn16)
            db = _nt(xdd.astype(bf16), dhn16)
            dxd_state = dxdd * dfac
            ddf = dxdd * xdd
            dtbc = jnp.sum(ddf, axis=0, keepdims=True) + exp_t * jnp.sum(dhn * ht, axis=0, keepdims=True)
            deb = deb - ddf + jnp.where(is_last, dtbc, 0.0)
            dcb = jnp.zeros((CHUNK, CHUNK), f32)
            dcbt = jnp.zeros((CHUNK, CHUNK), f32)
            for j in range(HPG):
                h = g * HPG + j
                hs = slice(j * SSD_HEAD_DIM, (j + 1) * SSD_HEAD_DIM)
                colj = col[:, j * CHUNK:(j + 1) * CHUNK]
                row = et_ref[h:h + 1, :]
                lam = jnp.exp(jnp.where(mask, colj - row, NEG))
                lam_t = lam.T
                xdj, dyj = xd[:, hs].astype(bf16), dy[:, hs].astype(bf16)
                t1 = _nt(dyj, xdj) * lam
                t2 = _nt(xdj, dyj) * lam_t
                dcb, dcbt = dcb + t1, dcbt + t2
                det_ref[h:h + 1, :] = -jnp.sum(t1 * cb - t2 * cbt, axis=0, keepdims=True)
                q_ref[:, hs] = _nn((cbt * lam_t).astype(bf16), dyj)
            x_cols = slice(g * GW, (g + 1) * GW)
            dxd = q_ref[...] + dxd_state
            dxs = dxd * dtb
            if final:
                dxs = dxs + prev_ref[:, x_cols] + dy * dexp_ref[:, x_cols]
            dxbc_ref[:, x_cols] = dxs
            b_cols = slice(D_INNER + g * D_STATE, D_INNER + (g + 1) * D_STATE)
            c_cols = slice(D_INNER + GN + g * D_STATE, D_INNER + GN + (g + 1) * D_STATE)
            db = db + _nn(dcbt.astype(bf16), cg)
            dc = dc + _nn(dcb.astype(bf16), bg)
            if final:
                db, dc = db + prev_ref[:, b_cols], dc + prev_ref[:, c_cols]
                ddexp_ref[:, g * GW:(g + 1) * GW] += jnp.sum(dy * xs, axis=0, keepdims=True)
            dxbc_ref[:, b_cols] = db
            dxbc_ref[:, c_cols] = dc
            det2_ref[g * HPG:(g + 1) * HPG, :] = _head_sum(sel8, deb)
            ddt_ref[g * HPG:(g + 1) * HPG, :] = _head_sum(sel8, dxd * xs)
        dat = jnp.dot(det_ref[...] + det2_ref[...], mask.astype(f32), precision=HIGHEST, preferred_element_type=f32)
        ddtt_ref[...] = ddt_ref[...] + dat * a_ref[...]
        da_ref[...] += jnp.sum(dat * dtt_v, axis=1, keepdims=True)

    in_specs = _ssd_in_specs(cidx) + [
        pl.BlockSpec((1, N_SSD_GROUPS, D_STATE, GW), lambda c: (cidx(c), 0, 0, 0)),
        pl.BlockSpec((CHUNK, D_INNER), lambda c: (cidx(c), 0))]
    hl = pltpu.VMEM((N_SSD_HEADS, CHUNK), f32)
    dxbc_spec = pl.BlockSpec((CHUNK, CONV_DIM), lambda c: (cidx(c), 0))
    dexp_spec = pl.BlockSpec((1, D_INNER), lambda c: (0, 0))
    return pl.pallas_call(
        body, name="ssd_bwd_rev" if reverse else "ssd_bwd", grid=(nc,),
        in_specs=in_specs + ([dxbc_spec, dexp_spec] if final else []),
        out_specs=[dxbc_spec, pl.BlockSpec((N_SSD_HEADS, CHUNK), lambda c: (0, cidx(c))),
                   pl.BlockSpec((N_SSD_HEADS, 1), lambda c: (0, 0))] + ([dexp_spec] if final else []),
        out_shape=[jax.ShapeDtypeStruct((s, CONV_DIM), f32), jax.ShapeDtypeStruct((N_SSD_HEADS, s), f32),
                   jax.ShapeDtypeStruct((N_SSD_HEADS, 1), f32)]
        + ([jax.ShapeDtypeStruct((1, D_INNER), f32)] if final else []),
        scratch_shapes=[pltpu.VMEM((N_SSD_GROUPS, D_STATE, GW), f32), hl, hl, hl, hl, pltpu.VMEM((CHUNK, GW), f32)],
        compiler_params=_cparams(dimension_semantics=("arbitrary",)),
    )(xbc, xbc, xbc, dtt, a_col, selc, selh, states, dy, *((dxbc_prev, dexp) if final else ()))


@jax.custom_vjp
def ssd_bidir(xbc, dtt, a_col, dexp):
    y_f, _ = _ssd_fwd(xbc, dtt[:N_SSD_HEADS], a_col[:N_SSD_HEADS], False)
    return _ssd_fwd(xbc, dtt[N_SSD_HEADS:], a_col[N_SSD_HEADS:], True, y_prev=y_f, dexp=dexp)[0]


def _ssd_bidir_fwd(xbc, dtt, a_col, dexp):
    y_f, st_f = _ssd_fwd(xbc, dtt[:N_SSD_HEADS], a_col[:N_SSD_HEADS], False)
    y, st_b = _ssd_fwd(xbc, dtt[N_SSD_HEADS:], a_col[N_SSD_HEADS:], True, y_prev=y_f, dexp=dexp)
    return y, (xbc, dtt, a_col, dexp, st_f, st_b)


def _ssd_bidir_bwd(res, dy):
    xbc, dtt, a_col, dexp, st_f, st_b = res
    dxbc_f, ddtt_f, da_f = _ssd_bwd(xbc, dtt[:N_SSD_HEADS], a_col[:N_SSD_HEADS], st_f, dy, False)
    dxbc, ddtt_b, da_b, ddexp = _ssd_bwd(xbc, dtt[N_SSD_HEADS:], a_col[N_SSD_HEADS:], st_b, dy, True,
                                         dxbc_prev=dxbc_f, dexp=dexp)
    return dxbc, jnp.concatenate([ddtt_f, ddtt_b], axis=0), jnp.concatenate([da_f, da_b], axis=0), ddexp


ssd_bidir.defvjp(_ssd_bidir_fwd, _ssd_bidir_bwd)


def _rope_tables(s):
    rows = s // GRID_W
    pos_row = np.repeat(np.arange(rows), GRID_W).astype(np.float32)
    pos_col = np.tile(np.arange(GRID_W), rows).astype(np.float32)
    axis_dim = HEAD_DIM // 2
    inv_freq = np.float32(ROPE_THETA) ** (-np.arange(0, axis_dim, 2, dtype=np.float32) / np.float32(axis_dim))
    ang_r = pos_row[:, None] * inv_freq[None, :].astype(np.float32)
    ang_c = pos_col[:, None] * inv_freq[None, :].astype(np.float32)
    cos = np.concatenate([np.cos(ang_r), np.cos(ang_r), np.cos(ang_c), np.cos(ang_c)] * 2, axis=-1)
    sin = np.concatenate([np.sin(ang_r), np.sin(ang_r), np.sin(ang_c), np.sin(ang_c)] * 2, axis=-1)
    return jnp.asarray(cos, f32), jnp.asarray(sin, f32)


def _rope_perm():
    p = np.zeros((HEAD_DIM, HEAD_DIM), np.float32)
    for j in range(HEAD_DIM):
        if (j % 32) < 16:
            p[j + 16, j] = -1.0
        else:
            p[j - 16, j] = 1.0
    return p


def local_loss(x, mod, small, recv_in_like, recv_late_like, wfull, late_shard, target):
    s = x.shape[0]
    lin = {n: make_linear("lin_" + n) for n in LATE if not n.startswith("mlp")}
    wfull, wgrads = dict(wfull), {}
    shift1, scale1, gate1, shift2, scale2, gate2 = [mod[i] for i in range(6)]

    norm_mod = make_rowwise("norm_mod", _fn_norm_mod, [(D_MODEL, bf16), (D_MODEL, f32)], tm_pref=1024)
    (h, x_res), _ = norm_mod((x,), (small["norm1_w"], scale1, shift1), (), ())

    proj = dict(zip(PROJ_NAMES, in_proj(h, tuple(wfull[n] for n in PROJ_NAMES), recv_in_like)))

    cos, sin = _rope_tables(s)

    def heads(t, nh):
        return t.reshape(s, nh, HEAD_DIM).transpose(1, 0, 2)

    qr = make_head_rope("q_norm_rope", N_Q_HEADS, Q_SCALE, False)(proj["q"], small["q_norm_w"], cos, sin)
    kr = make_head_rope("k_norm_rope", N_KV_HEADS, 1.0, True)(proj["k"], small["k_norm_w"], cos, sin)
    vh = heads(proj["v"], N_KV_HEADS).astype(bf16)
    att = attention(qr, kr, vh)

    xbc, gathered, *carriers = conv_silu_comm(proj["xbc"], small["conv_w"], small["conv_b"], late_shard,
                                              recv_late_like)
    wfull.update(_split_late(gathered))
    wgrads.update(zip(LATE, carriers))
    ao = lin["attn_out"](att, wfull["attn_out"], wgrads["attn_out"])
    softplus = make_rowwise("dt_softplus", _fn_softplus, [(2 * N_SSD_HEADS, f32)])
    (dt,), _ = softplus((proj["dt"][:, :2 * N_SSD_HEADS],), (small["dt_bias"].reshape(1, 2 * N_SSD_HEADS),), (), ())
    a_neg = -jnp.exp(small["A_log"])
    dexp = jnp.repeat(small["ssd_D"].reshape(N_SSD_HEADS), SSD_HEAD_DIM).reshape(1, D_INNER)
    y = ssd_bidir(xbc, dt.T, a_neg.reshape(2 * N_SSD_HEADS, 1), dexp)
    ssd_gate = make_rowwise("ssd_gate", _fn_ssd_gate, [(D_INNER, bf16)], tm_pref=256)
    (ssd_out,), _ = ssd_gate((y, proj["z"]), (small["ssd_norm_w"],), (), ())
    so = lin["ssd_out"](ssd_out, wfull["ssd_out"], wgrads["ssd_out"])

    merge = make_rowwise("merge", _fn_merge, [(D_MODEL, bf16)])
    (merged,), _ = merge((ao, so, proj["ga"], proj["gs"]), (), (), ())
    mo = lin["o"](merged, wfull["o"], wgrads["o"])

    res_norm = make_rowwise("res_norm", _fn_res_norm, [(D_MODEL, f32), (D_MODEL, bf16)])
    (x1, h2), _ = res_norm((x_res, mo), (gate1, small["norm2_w"], scale2, shift2), (), ())
    ff = mlp(h2, wfull["mlp1"], wgrads["mlp1"], wfull["mlp2"], wgrads["mlp2"])
    loss_op = make_rowwise("loss", _fn_loss, [], [(1, 1)])
    _, (loss,) = loss_op((x1, ff), (gate2,), (), (target,))
    return loss[0, 0]


_BC1 = 1.0 - ADAM_B1 ** ADAM_STEP
_BC2 = 1.0 - ADAM_B2 ** ADAM_STEP


def _adamw(w, g, m, v):
    m = ADAM_B1 * m + (1.0 - ADAM_B1) * g
    v = ADAM_B2 * v + (1.0 - ADAM_B2) * (g * g)
    delta = -ADAM_LR * ((m / _BC1) / (jnp.sqrt(v / _BC2) + ADAM_EPS) + ADAM_WD * w)
    return delta, m, v


def _ada_fwd(c_all, w, b):
    n = w.shape[1]

    def body(c_ref, w_ref, b_ref, o_ref):
        o_ref[...] = jnp.dot(_silu(c_ref[...]), w_ref[...], precision=HIGHEST, preferred_element_type=f32) + b_ref[...]

    return pl.pallas_call(body, name="ada_fwd", out_shape=jax.ShapeDtypeStruct((N_DEV, n), f32),
                          compiler_params=_cparams())(c_all, w, b)


def _ada_bwd_adamw(c_all, dmod, w, m, v):
    d, n = w.shape
    tr = _pick(d, (256, 128))

    def body(c_ref, dm_ref, w_ref, m_ref, v_ref, g_ref, dl_ref, mo_ref, vo_ref):
        g = lax.dot_general(_silu(c_ref[...]), dm_ref[...], _DIMS["tn"], precision=HIGHEST,
                            preferred_element_type=f32)
        g_ref[...] = g
        dl_ref[...], mo_ref[...], vo_ref[...] = _adamw(w_ref[...], g, m_ref[...], v_ref[...])

    blk = pl.BlockSpec((tr, n), lambda i: (i, 0))
    return pl.pallas_call(
        body, name="ada_bwd_adamw", grid=(d // tr,),
        in_specs=[pl.BlockSpec((N_DEV, tr), lambda i: (0, i)), pl.BlockSpec((N_DEV, n), lambda i: (0, 0)), blk, blk, blk],
        out_specs=[blk] * 4, out_shape=[jax.ShapeDtypeStruct((d, n), f32)] * 4,
        compiler_params=_cparams(dimension_semantics=("parallel",)),
    )(c_all, dmod, w, m, v)


def _sum_over_mesh(g):
    def body(g_ref, o_ref):
        acc = g_ref[0]
        for d in range(1, N_DEV):
            acc = acc + g_ref[d]
        o_ref[...] = acc

    return pl.pallas_call(body, name="sum_small", out_shape=jax.ShapeDtypeStruct(g.shape[1:], f32),
                          compiler_params=_cparams())(g)


def _adamw_small(w, g, m, v):
    def body(w_ref, g_ref, m_ref, v_ref, dl_ref, mo_ref, vo_ref):
        dl_ref[...], mo_ref[...], vo_ref[...] = _adamw(w_ref[...], g_ref[...], m_ref[...], v_ref[...])

    return pl.pallas_call(body, name="adamw_small", out_shape=[jax.ShapeDtypeStruct(w.shape, f32)] * 3,
                          compiler_params=_cparams())(w, g, m, v)


def _sum_adamw(recv, w, m, v, name):
    _, r, c = recv.shape
    tr = _pick(r, (256, 128, 64, 16))

    def body(g_ref, w_ref, m_ref, v_ref, go_ref, dl_ref, mo_ref, vo_ref):
        g = g_ref[0].astype(f32)
        for d in range(1, N_DEV):
            g = g + g_ref[d].astype(f32)
        go_ref[...] = g
        dl_ref[...], mo_ref[...], vo_ref[...] = _adamw(w_ref[...], g, m_ref[...], v_ref[...])

    blk = pl.BlockSpec((tr, c), lambda i: (i, 0))
    return pl.pallas_call(
        body, name=name, grid=(r // tr,),
        in_specs=[pl.BlockSpec((N_DEV, tr, c), lambda i: (0, i, 0)), blk, blk, blk],
        out_specs=[blk] * 4, out_shape=[jax.ShapeDtypeStruct((r, c), f32)] * 4,
        compiler_params=_cparams(dimension_semantics=("parallel",)),
    )(recv, w, m, v)


def _pack_small(arrs):
    parts = []
    for a in arrs:
        flat = a.reshape(-1).astype(f32)
        parts.append(jnp.pad(flat, (0, (-flat.shape[0]) % LANE)))
    flat = jnp.concatenate(parts)
    flat = jnp.pad(flat, (0, (-flat.shape[0]) % (8 * LANE)))
    return flat.reshape(-1, LANE)


def _unpack_small(packed, shapes):
    flat = packed.reshape(-1)
    out, off = [], 0
    for shp in shapes:
        n = int(np.prod(shp))
        out.append(flat[off:off + n].reshape(shp))
        off += n + (-n) % LANE
    return out


BIG = ("w_attn_out", "w_ssd_out", "w_o", "w_mlp1", "w_mlp2")
BIG_ROWS = (N_Q_HEADS * HEAD_DIM // N_DEV, D_INNER // N_DEV, D_MODEL // N_DEV,
            D_MODEL * (D_FF // N_DEV) // PACK_COLS, D_FF // N_DEV)
N_IN_SHARD = D_IN_PROJ // N_DEV
assert sum(BIG_ROWS) % 16 == 0


def _pack_big(shards, dtype):
    return jnp.concatenate([s.astype(dtype).reshape(-1, PACK_COLS) for s in shards], axis=0)


def _unpack_big(packed, shapes):
    out, off = [], 0
    for rows, shp in zip(BIG_ROWS, shapes):
        out.append(packed[off:off + rows].reshape(shp))
        off += rows
    return out


LATE = ("attn_out", "ssd_out", "o", "mlp1", "mlp2")
LATE_SHAPES = ((N_Q_HEADS * HEAD_DIM, D_MODEL), (D_INNER, D_MODEL), (D_MODEL, D_MODEL), (D_MODEL, D_FF),
               (D_FF, D_MODEL))


def _split_w_in(g_in):
    w_in = g_in.transpose(1, 0, 2).reshape(D_MODEL, D_IN_PROJ)
    w = {}
    off = 0
    for name, size in zip(PROJ_NAMES, PROJ_SIZES):
        w[name] = w_in[:, off:off + size]
        off += size
    w["dt"] = jnp.pad(w["dt"], ((0, 0), (0, DT_PAD - 2 * N_SSD_HEADS)))
    return w


def _split_late(g):
    offs = np.cumsum((0,) + BIG_ROWS)
    sl = [g[:, offs[i]:offs[i + 1]] for i in range(len(BIG))]
    return {"attn_out": sl[0].reshape(LATE_SHAPES[0]), "ssd_out": sl[1].reshape(LATE_SHAPES[1]),
            "o": sl[2].reshape(LATE_SHAPES[2]),
            "mlp1": sl[3].reshape(N_DEV, D_MODEL, D_FF // N_DEV).transpose(1, 0, 2).reshape(LATE_SHAPES[3]),
            "mlp2": sl[4].reshape(LATE_SHAPES[4])}


def _pack_in_grads(gw):
    gw = {n: g.astype(bf16) for n, g in gw.items()}
    gw["dt"] = gw["dt"][:, :2 * N_SSD_HEADS]
    g_in = jnp.concatenate([gw[n] for n in PROJ_NAMES], axis=1)
    return g_in.reshape(D_MODEL, N_DEV, N_IN_SHARD).transpose(1, 0, 2)


def _pack_late_grads(gw):
    gw = {n: g.astype(bf16) for n, g in gw.items()}
    parts = [
        gw["attn_out"].reshape(N_DEV, -1, PACK_COLS),
        gw["ssd_out"].reshape(N_DEV, -1, PACK_COLS),
        gw["o"].reshape(N_DEV, -1, PACK_COLS),
        gw["mlp1"].reshape(D_MODEL, N_DEV, D_FF // N_DEV).transpose(1, 0, 2).reshape(N_DEV, -1, PACK_COLS),
        gw["mlp2"].reshape(N_DEV, -1, PACK_COLS),
    ]
    return jnp.concatenate(parts, axis=1)


SMALL = ("norm1_w", "norm2_w", "q_norm_w", "k_norm_w", "conv_w", "conv_b", "A_log", "dt_bias", "ssd_D", "ssd_norm_w")


def kernel(x, c, w_ada, b_ada, norm1_w, norm2_w, w_in, q_norm_w, k_norm_w, conv_w, conv_b, A_log, dt_bias, ssd_D, ssd_norm_w, w_attn_out, w_ssd_out, w_o, w_mlp1, w_mlp2, loss_target, m_w_ada, m_b_ada, m_norm1_w, m_norm2_w, m_w_in, m_q_norm_w, m_k_norm_w, m_conv_w, m_conv_b, m_A_log, m_dt_bias, m_ssd_D, m_ssd_norm_w, m_w_attn_out, m_w_ssd_out, m_w_o, m_w_mlp1, m_w_mlp2, v_w_ada, v_b_ada, v_norm1_w, v_norm2_w, v_w_in, v_q_norm_w, v_k_norm_w, v_conv_w, v_conv_b, v_A_log, v_dt_bias, v_ssd_D, v_ssd_norm_w, v_w_attn_out, v_w_ssd_out, v_w_o, v_w_mlp1, v_w_mlp2):
    args = dict(locals())
    me = _my_index()
    n_ada = 6 * D_MODEL // N_DEV
    n_cw = CONV_DIM // N_DEV

    blk = jnp.zeros((8, D_MODEL), f32)
    blk = blk.at[0:1, :].set(c)
    blk = blk.at[1:1 + D_CONV, :n_cw].set(conv_w[0])
    g0, w_in_all = _all_gather_pair(blk, w_in[0].astype(bf16), "gather_c_convw_w_in")
    c_all = g0[:, 0, :]
    conv_w_full = g0[:, 1:1 + D_CONV, :n_cw].transpose(1, 0, 2).reshape(D_CONV, CONV_DIM)

    b_shard = lax.dynamic_slice(b_ada, (0, me * n_ada), (1, n_ada))
    mod_cols = _ada_fwd(c_all, w_ada[0], b_shard)
    g1 = _all_gather(mod_cols, "gather_mod", in_vmem=True)
    mod_mine = lax.dynamic_index_in_dim(g1, me, axis=1, keepdims=False)
    mod = mod_mine.reshape(6, 1, D_MODEL)

    big_shapes = [args[n].shape[1:] for n in BIG]
    late_shard = _pack_big([args[n][0] for n in BIG], bf16)
    wfull = _split_w_in(w_in_all)
    recv_in_like = jnp.zeros((N_DEV,) + w_in.shape[1:], bf16)
    recv_late_like = jnp.zeros((N_DEV,) + late_shard.shape, bf16)

    small = {"norm1_w": norm1_w, "norm2_w": norm2_w, "q_norm_w": q_norm_w, "k_norm_w": k_norm_w,
             "conv_w": conv_w_full, "conv_b": conv_b, "A_log": A_log[0], "dt_bias": dt_bias[0], "ssd_D": ssd_D,
             "ssd_norm_w": ssd_norm_w}

    loss, (gx, gmod, gsmall, recv_in, recv_late) = jax.value_and_grad(local_loss, argnums=(0, 1, 2, 3, 4))(
        x[0], mod, small, recv_in_like, recv_late_like, wfull, late_shard, loss_target[0])

    small_list = [gmod, gsmall["norm1_w"], gsmall["norm2_w"], gsmall["q_norm_w"], gsmall["k_norm_w"], gsmall["conv_w"],
                  gsmall["conv_b"], gsmall["A_log"], gsmall["dt_bias"], gsmall["ssd_D"], gsmall["ssd_norm_w"],
                  loss.reshape(1)]
    small_shapes = [a.shape for a in small_list]
    g2 = _all_gather(_pack_small(small_list), "gather_small_grads", in_vmem=True)
    summed = _unpack_small(_sum_over_mesh(g2), small_shapes)
    loss_total = summed[-1][0]
    g_b_ada = summed[0].reshape(1, 6 * D_MODEL)
    g_small = dict(zip(SMALL, summed[1:-1]))
    g_conv_w = lax.dynamic_slice(g_small["conv_w"], (0, me * n_cw), (D_CONV, n_cw))

    dmod_all = g2[:, :6 * D_MODEL // LANE, :].reshape(N_DEV, 6 * D_MODEL)
    dmod_shard = lax.dynamic_slice(dmod_all, (0, me * n_ada), (N_DEV, n_ada))
    ada = _ada_bwd_adamw(c_all, dmod_shard, w_ada[0], m_w_ada[0], v_w_ada[0])

    small_grads = {"b_ada": g_b_ada, "norm1_w": g_small["norm1_w"], "norm2_w": g_small["norm2_w"],
                   "q_norm_w": g_small["q_norm_w"], "k_norm_w": g_small["k_norm_w"], "conv_w": g_conv_w[None],
                   "conv_b": g_small["conv_b"], "A_log": g_small["A_log"][None], "dt_bias": g_small["dt_bias"][None],
                   "ssd_D": g_small["ssd_D"], "ssd_norm_w": g_small["ssd_norm_w"]}
    sm_names = list(small_grads)
    sm_shapes = [args[n].shape for n in sm_names]
    sm = _adamw_small(_pack_small([args[n] for n in sm_names]), _pack_small([small_grads[n] for n in sm_names]),
                      _pack_small([args["m_" + n] for n in sm_names]), _pack_small([args["v_" + n] for n in sm_names]))
    sm_delta, sm_m, sm_v = [dict(zip(sm_names, _unpack_small(t, sm_shapes))) for t in sm]
    small_grads = {n: small_grads[n].reshape(args[n].shape) for n in sm_names}

    w_in_out = _sum_adamw(recv_in, w_in[0], m_w_in[0], v_w_in[0], "sum_adamw_w_in")
    big = _sum_adamw(recv_late, _pack_big([args[n][0] for n in BIG], f32),
                     _pack_big([args["m_" + n][0] for n in BIG], f32),
                     _pack_big([args["v_" + n][0] for n in BIG], f32), "sum_adamw")
    big_g, big_delta, big_m, big_v = [dict(zip(BIG, [t[None] for t in _unpack_big(p, big_shapes)])) for p in big]
    big_g["w_in"], big_delta["w_in"], big_m["w_in"], big_v["w_in"] = [t[None] for t in w_in_out]

    names = ("w_ada", "b_ada", "norm1_w", "norm2_w", "w_in", "q_norm_w", "k_norm_w", "conv_w", "conv_b", "A_log",
             "dt_bias", "ssd_D", "ssd_norm_w", "w_attn_out", "w_ssd_out", "w_o", "w_mlp1", "w_mlp2")
    grads, deltas, new_m, new_v = {}, {}, {}, {}
    for n in names:
        if n == "w_ada":
            grads[n], deltas[n], new_m[n], new_v[n] = [t[None] for t in ada]
        elif n in big_g:
            grads[n], deltas[n], new_m[n], new_v[n] = big_g[n], big_delta[n], big_m[n], big_v[n]
        else:
            grads[n], deltas[n], new_m[n], new_v[n] = small_grads[n], sm_delta[n], sm_m[n], sm_v[n]
    return (loss_total, gx[None], *[grads[n] for n in names], *[deltas[n] for n in names],
            *[new_m[n] for n in names], *[new_v[n] for n in names])
```
